```python
import math
import jax, jax.numpy as jnp
from jax import lax
import numpy as np

D_MODEL = 2048
BATCH = 4
SEQ = 2048
DEPTH = 2
DEC_BATCH = 128
DEC_SEQ = 1
PAST_LEN = 16384
PAGE_SIZE = 128

N_BRANCH = 4
BRANCH_W = D_MODEL // 2
EPS = 1e-6
LRU_W = BRANCH_W
LRU_BLOCKS = 8
LRU_BLOCK = LRU_W // LRU_BLOCKS
LRU_C = 8.0
CONV_W = 4
HG_HEADS = 8
HG_DK = BRANCH_W // HG_HEADS
HG_DV = BRANCH_W // HG_HEADS
HG_CHUNK = 32
SSD_HEADDIM = 64
SSD_HEADS = BRANCH_W // SSD_HEADDIM
SSD_GROUPS = 2
SSD_STATE = 128
SSD_CONV_DIM = BRANCH_W + 2 * SSD_GROUPS * SSD_STATE
SSD_CHUNK = 64
RET_HEADS = 8
RET_DK = BRANCH_W // RET_HEADS
RET_DV = BRANCH_W // RET_HEADS
RET_CHUNK = 64
ROPE_BASE = 10000.0
D_FF = 5632
FFN_CONV_W = 3
IN_SIZES = (LRU_W, LRU_W,
            HG_HEADS * HG_DK, HG_HEADS * HG_DK, HG_HEADS * HG_DV, HG_HEADS * HG_DV,
            BRANCH_W, SSD_CONV_DIM, SSD_HEADS,
            RET_HEADS * RET_DK, RET_HEADS * RET_DK, RET_HEADS * RET_DV, RET_HEADS * RET_DV)
N_IN = sum(IN_SIZES)

kernel_name = 'hybrid_lru_hgrn2_ssd_retention_step'


def _rms(x):
    return x * lax.rsqrt(jnp.mean(x * x, axis=-1, keepdims=True) + EPS)


def rmsnorm(x, g):
    xf = x.astype(jnp.float32)
    return (_rms(xf) * g.astype(jnp.float32)).astype(x.dtype)


def causal_dwconv(x, buf, w, b):
    width = w.shape[0]
    T = x.shape[1]
    xc = jnp.concatenate([buf.astype(x.dtype), x], axis=1)
    y = b
    for j in range(width):
        y = y + xc[:, j:j + T] * w[j]
    return y, xc[:, T:]


def diag_linear_scan(a, u, h0):
    def combine(left, right):
        a_l, u_l = left
        a_r, u_r = right
        return a_l * a_r, a_r * u_l + u_r
    a_cum, u_cum = lax.associative_scan(combine, (a, u), axis=1)
    h = a_cum * h0[:, None] + u_cum
    return h, h[:, -1]


def _chunk(T, C):
    return C if T % C == 0 else T


def _masked_exp(diff, mask):
    return jnp.where(mask, jnp.exp(jnp.where(mask, diff, 0.0)), 0.0)


def gla_chunked(q, k, v, logf, S0, chunk):
    B, T, H, K = q.shape
    V = v.shape[-1]
    C = _chunk(T, chunk)
    N = T // C
    to_chunks = lambda z: jnp.moveaxis(z.reshape(B, N, C, H, z.shape[-1]), 1, 0)
    qc, kc, vc = to_chunks(q), to_chunks(k), to_chunks(v)
    bc = jnp.cumsum(to_chunks(logf), axis=2)
    mask = jnp.tril(jnp.ones((C, C), dtype=bool))[None, :, :, None, None]

    def step(S, inp):
        qi, ki, vi, bi = inp
        dec = _masked_exp(bi[:, :, None] - bi[:, None], mask)
        att = jnp.einsum('btshk,bshk->bhts', qi[:, :, None] * dec, ki)
        o = (jnp.einsum('bhts,bshv->bthv', att, vi)
             + jnp.einsum('bthk,bhkv->bthv', qi * jnp.exp(bi), S))
        b_last = bi[:, -1]
        S = (S * jnp.exp(b_last)[..., None]
             + jnp.einsum('bshk,bshv->bhkv', ki * jnp.exp(b_last[:, None] - bi), vi))
        return S, o

    S, o = lax.scan(step, S0, (qc, kc, vc, bc))
    return jnp.moveaxis(o, 0, 1).reshape(B, T, H, V), S


def scalar_decay_chunked(q, k, v, logd, S0, chunk):
    B, T, H, K = q.shape
    V = v.shape[-1]
    C = _chunk(T, chunk)
    N = T // C
    rs = lambda z: z.reshape((B, N, C) + z.shape[2:])
    q, k, v, logd = rs(q), rs(k), rs(v), rs(logd)
    b = jnp.cumsum(logd, axis=2)
    mask = jnp.tril(jnp.ones((C, C), dtype=bool))[None, None, :, :, None]
    L = _masked_exp(b[:, :, :, None] - b[:, :, None], mask)
    scores = jnp.einsum('bnthk,bnshk->bntsh', q, k) * L
    o_intra = jnp.einsum('bntsh,bnshv->bnthv', scores, v)
    b_last = b[:, :, -1]
    U = jnp.einsum('bnsh,bnshk,bnshv->bnhkv', jnp.exp(b_last[:, :, None] - b), k, v)
    q_in = q * jnp.exp(b)[..., None]

    def step(S, inp):
        qi, ui, di = inp
        o = jnp.einsum('bthk,bhkv->bthv', qi, S)
        return S * di[..., None, None] + ui, o

    S, o_inter = lax.scan(step, S0, (jnp.moveaxis(q_in, 1, 0), jnp.moveaxis(U, 1, 0),
                                     jnp.moveaxis(jnp.exp(b_last), 1, 0)))
    o = o_intra + jnp.moveaxis(o_inter, 0, 1)
    return o.reshape(B, T, H, V), S


def rope(x, pos):
    half = x.shape[-1] // 2
    freqs = ROPE_BASE ** (-jnp.arange(half, dtype=jnp.float32) / half)
    ang = pos[:, None] * freqs[None]
    cos = jnp.cos(ang)[None, :, None]
    sin = jnp.sin(ang)[None, :, None]
    x1, x2 = x[..., :half], x[..., half:]
    return jnp.concatenate([x1 * cos - x2 * sin, x1 * sin + x2 * cos], axis=-1)


def token_mixers(h, pos, lru_h, lru_conv, hg_S, ssd_S, ssd_conv, ret_S, P, lb):
    f32 = jnp.float32
    B, T, _ = h.shape
    proj = jnp.matmul(h, P['w_in']).astype(f32)
    splits = np.cumsum(IN_SIZES)[:-1].tolist()
    (xa, ya, hq, hf, hi, hg, sz, sxbc, sdt, rq, rk, rv, rg) = jnp.split(proj, splits, axis=-1)

    xa, lru_conv_new = causal_dwconv(xa, lru_conv.astype(f32), P['lru_conv_w'].astype(f32),
                                     P['lru_conv_b'].astype(f32))
    xblk = xa.reshape(B, T, LRU_BLOCKS, LRU_BLOCK)
    r = jax.nn.sigmoid(jnp.einsum('btnc,ncd->btnd', xblk, P['lru_wa'].astype(f32))
                       + P['lru_ba'].astype(f32)).reshape(B, T, LRU_W)
    i = jax.nn.sigmoid(jnp.einsum('btnc,ncd->btnd', xblk, P['lru_wx'].astype(f32))
                       + P['lru_bx'].astype(f32)).reshape(B, T, LRU_W)
    log_a = -LRU_C * r * jax.nn.softplus(-P['lru_lambda'].astype(f32))
    u = jnp.sqrt(-jnp.expm1(2.0 * log_a)) * (i * xa)
    hs, lru_h_new = diag_linear_scan(jnp.exp(log_a), u, lru_h.astype(f32))
    out_a = hs * jax.nn.gelu(ya)

    q_b = jax.nn.silu(hq).reshape(B, T, HG_HEADS, HG_DK)
    f = lb + (1.0 - lb) * jax.nn.sigmoid(hf)
    logf = jnp.log(f).reshape(B, T, HG_HEADS, HG_DK)
    k_b = ((1.0 - lb) * jax.nn.sigmoid(-hf)).reshape(B, T, HG_HEADS, HG_DK)
    o_b, hg_S_new = gla_chunked(q_b, k_b, hi.reshape(B, T, HG_HEADS, HG_DV), logf,
                                hg_S.astype(f32), HG_CHUNK)
    o_b = _rms(o_b) * P['hg_norm_w'].astype(f32) * jax.nn.silu(hg.reshape(B, T, HG_HEADS, HG_DV))
    out_b = o_b.reshape(B, T, BRANCH_W)

    xbc, ssd_conv_new = causal_dwconv(sxbc, ssd_conv.astype(f32), P['ssd_conv_w'].astype(f32),
                                      P['ssd_conv_b'].astype(f32))
    xbc = jax.nn.silu(xbc)
    xs_, Bm, Cm = jnp.split(xbc, [BRANCH_W, BRANCH_W + SSD_GROUPS * SSD_STATE], axis=-1)
    xs_ = xs_.reshape(B, T, SSD_HEADS, SSD_HEADDIM)
    rep = SSD_HEADS // SSD_GROUPS
    Bh = jnp.repeat(Bm.reshape(B, T, SSD_GROUPS, SSD_STATE), rep, axis=2)
    Ch = jnp.repeat(Cm.reshape(B, T, SSD_GROUPS, SSD_STATE), rep, axis=2)
    dt = jax.nn.softplus(sdt + P['ssd_dt_bias'].astype(f32))
    A = -jnp.exp(P['ssd_a_log'].astype(f32))
    y, ssd_S_new = scalar_decay_chunked(Ch, Bh, xs_ * dt[..., None], dt * A,
                                        ssd_S.astype(f32), SSD_CHUNK)
    y = y + P['ssd_d'].astype(f32)[:, None] * xs_
    y = (y.reshape(B, T, BRANCH_W) * jax.nn.silu(sz)).reshape(B, T, SSD_GROUPS, BRANCH_W // SSD_GROUPS)
    out_c = (_rms(y) * P['ssd_norm_w'].astype(f32).reshape(SSD_GROUPS, -1)).reshape(B, T, BRANCH_W)

    q_d = rope(rq.reshape(B, T, RET_HEADS, RET_DK), pos)
    k_d = rope(rk.reshape(B, T, RET_HEADS, RET_DK), pos) * RET_DK ** -0.5
    log_gamma = jnp.log1p(-jnp.exp2(-5.0 - jnp.arange(RET_HEADS, dtype=f32)))
    o_d, ret_S_new = scalar_decay_chunked(q_d, k_d, rv.reshape(B, T, RET_HEADS, RET_DV),
                                          jnp.broadcast_to(log_gamma, (B, T, RET_HEADS)),
                                          ret_S.astype(f32), RET_CHUNK)
    out_d = (_rms(o_d) * jax.nn.silu(rg.reshape(B, T, RET_HEADS, RET_DV))).reshape(B, T, BRANCH_W)

    stacked = jnp.stack([out_a, out_b, out_c, out_d], axis=2).astype(h.dtype)
    branches = jnp.einsum('btkc,kcd->btkd', stacked, P['w_branch'])
    gates = jax.nn.sigmoid(jnp.einsum('btd,dke->btke', h, P['w_gate']).astype(f32))
    merged = jnp.sum(gates * branches.astype(f32), axis=2).astype(h.dtype)
    out = jnp.matmul(merged, P['w_out'])
    return out, (lru_h_new, lru_conv_new, hg_S_new, ssd_S_new, ssd_conv_new, ret_S_new)


def conv_ffn(h, buf, P):
    f32 = jnp.float32
    u = jnp.matmul(h, P['ffn_w_up']).astype(f32)
    v = jnp.matmul(h, P['ffn_w_val']).astype(f32)
    uc, buf_new = causal_dwconv(u, buf.astype(f32), P['ffn_conv_w'].astype(f32), P['ffn_conv_b'].astype(f32))
    a = (jax.nn.gelu(uc) * v).astype(h.dtype)
    return jnp.matmul(a, P['ffn_w_down']), buf_new


def trunk_layer(x, start, states, P, lb):
    T = x.shape[1]
    pos = jnp.arange(T, dtype=jnp.float32) + start
    mix, mix_states = token_mixers(rmsnorm(x, P['g_mix']), pos, states[0], states[1], states[2],
                                   states[3], states[4], states[5], P, lb)
    x = x + mix.astype(x.dtype)
    ffn, ffn_buf = conv_ffn(rmsnorm(x, P['g_ffn']), states[6], P)
    x = x + ffn.astype(x.dtype)
    new_states = tuple(s.astype(x.dtype) for s in mix_states + (ffn_buf,))
    return x, new_states


def setup_inputs(seed: int = 0) -> dict:
    key = jax.random.key(seed)
    ks = iter(jax.random.split(key, 48))
    f32 = jnp.float32
    L = DEPTH
    D = D_MODEL

    def nrm(shape, scale):
        return jax.random.normal(next(ks), shape, f32) * scale

    def unif(shape, lo, hi):
        return jax.random.uniform(next(ks), shape, f32, minval=lo, maxval=hi)

    x_prompt = nrm((BATCH, SEQ, D), 1.0)
    x_sample = nrm((DEC_BATCH, DEC_SEQ, D), 1.0)
    state_lru_h = nrm((L, DEC_BATCH, LRU_W), 0.5)
    state_lru_conv = nrm((L, DEC_BATCH, CONV_W - 1, LRU_W), 1.0)
    state_hgrn = nrm((L, DEC_BATCH, HG_HEADS, HG_DK, HG_DV), 0.5)
    state_ssd = nrm((L, DEC_BATCH, SSD_HEADS, SSD_STATE, SSD_HEADDIM), 0.5)
    state_ssd_conv = nrm((L, DEC_BATCH, CONV_W - 1, SSD_CONV_DIM), 1.0)
    state_ret = nrm((L, DEC_BATCH, RET_HEADS, RET_DK, RET_DV), 1.0)
    state_ffn_conv = nrm((L, DEC_BATCH, FFN_CONV_W - 1, D_FF), 1.0)

    g_mix = 1.0 + nrm((L, D), 0.02)
    g_ffn = 1.0 + nrm((L, D), 0.02)
    w_in = nrm((L, D, N_IN), D ** -0.5)
    lru_conv_w = nrm((L, CONV_W, LRU_W), CONV_W ** -0.5)
    lru_conv_b = nrm((L, LRU_W), 0.02)
    lru_wa = nrm((L, LRU_BLOCKS, LRU_BLOCK, LRU_BLOCK), LRU_BLOCK ** -0.5)
    lru_ba = nrm((L, LRU_BLOCKS, LRU_BLOCK), 0.02)
    lru_wx = nrm((L, LRU_BLOCKS, LRU_BLOCK, LRU_BLOCK), LRU_BLOCK ** -0.5)
    lru_bx = nrm((L, LRU_BLOCKS, LRU_BLOCK), 0.02)
    s = unif((L, LRU_W), 0.9, 0.999) ** (1.0 / LRU_C)
    lru_lambda = jnp.log(s) - jnp.log1p(-s)
    hg_lb_logits = nrm((L, HG_HEADS * HG_DK), 0.5)
    hg_norm_w = 1.0 + nrm((L, HG_DV), 0.02)
    ssd_conv_w = nrm((L, CONV_W, SSD_CONV_DIM), CONV_W ** -0.5)
    ssd_conv_b = nrm((L, SSD_CONV_DIM), 0.02)
    dt0 = jnp.exp(unif((L, SSD_HEADS), math.log(1e-3), math.log(1e-1)))
    ssd_dt_bias = dt0 + jnp.log(-jnp.expm1(-dt0))
    ssd_a_log = jnp.log(unif((L, SSD_HEADS), 1.0, 16.0))
    ssd_d = 1.0 + nrm((L, SSD_HEADS), 0.02)
    ssd_norm_w = 1.0 + nrm((L, BRANCH_W), 0.02)
    w_branch = nrm((L, N_BRANCH, BRANCH_W, D), BRANCH_W ** -0.5)
    w_gate = nrm((L, D, N_BRANCH, D), D ** -0.5)
    w_out = nrm((L, D, D), D ** -0.5)
    ffn_w_up = nrm((L, D, D_FF), D ** -0.5)
    ffn_w_val = nrm((L, D, D_FF), D ** -0.5)
    ffn_conv_w = nrm((L, FFN_CONV_W, D_FF), FFN_CONV_W ** -0.5)
    ffn_conv_b = nrm((L, D_FF), 0.02)
    ffn_w_down = nrm((L, D_FF, D), D_FF ** -0.5)
    g_final = 1.0 + nrm((D,), 0.02)
    return {'x_prompt': x_prompt, 'x_sample': x_sample,
            'state_lru_h': state_lru_h, 'state_lru_conv': state_lru_conv, 'state_hgrn': state_hgrn,
            'state_ssd': state_ssd, 'state_ssd_conv': state_ssd_conv, 'state_ret': state_ret,
            'state_ffn_conv': state_ffn_conv,
            'g_mix': g_mix, 'g_ffn': g_ffn, 'w_in': w_in,
            'lru_conv_w': lru_conv_w, 'lru_conv_b': lru_conv_b, 'lru_wa': lru_wa, 'lru_ba': lru_ba,
            'lru_wx': lru_wx, 'lru_bx': lru_bx, 'lru_lambda': lru_lambda,
            'hg_lb_logits': hg_lb_logits, 'hg_norm_w': hg_norm_w,
            'ssd_conv_w': ssd_conv_w, 'ssd_conv_b': ssd_conv_b, 'ssd_dt_bias': ssd_dt_bias,
            'ssd_a_log': ssd_a_log, 'ssd_d': ssd_d, 'ssd_norm_w': ssd_norm_w,
            'w_branch': w_branch, 'w_gate': w_gate, 'w_out': w_out,
            'ffn_w_up': ffn_w_up, 'ffn_w_val': ffn_w_val, 'ffn_conv_w': ffn_conv_w,
            'ffn_conv_b': ffn_conv_b, 'ffn_w_down': ffn_w_down, 'g_final': g_final}


def reference(x_prompt, x_sample, state_lru_h, state_lru_conv, state_hgrn, state_ssd, state_ssd_conv,
              state_ret, state_ffn_conv, g_mix, g_ffn, w_in, lru_conv_w, lru_conv_b, lru_wa, lru_ba,
              lru_wx, lru_bx, lru_lambda, hg_lb_logits, hg_norm_w, ssd_conv_w, ssd_conv_b, ssd_dt_bias,
              ssd_a_log, ssd_d, ssd_norm_w, w_branch, w_gate, w_out, ffn_w_up, ffn_w_val, ffn_conv_w,
              ffn_conv_b, ffn_w_down, g_final):
    f32 = jnp.float32
    ls = jax.nn.softmax(hg_lb_logits.astype(f32), axis=0)
    lbs = jnp.cumsum(ls, axis=0) - ls[0]

    bp = x_prompt.shape[0]
    dtp = x_prompt.dtype
    zero_states = (jnp.zeros((bp, LRU_W), dtp), jnp.zeros((bp, CONV_W - 1, LRU_W), dtp),
                   jnp.zeros((bp, HG_HEADS, HG_DK, HG_DV), dtp),
                   jnp.zeros((bp, SSD_HEADS, SSD_STATE, SSD_HEADDIM), dtp),
                   jnp.zeros((bp, CONV_W - 1, SSD_CONV_DIM), dtp),
                   jnp.zeros((bp, RET_HEADS, RET_DK, RET_DV), dtp),
                   jnp.zeros((bp, FFN_CONV_W - 1, D_FF), dtp))

    xp, xs = x_prompt, x_sample
    new_p, new_s = [], []
    for l in range(DEPTH):
        P = {'g_mix': g_mix[l], 'g_ffn': g_ffn[l], 'w_in': w_in[l],
             'lru_conv_w': lru_conv_w[l], 'lru_conv_b': lru_conv_b[l], 'lru_wa': lru_wa[l],
             'lru_ba': lru_ba[l], 'lru_wx': lru_wx[l], 'lru_bx': lru_bx[l], 'lru_lambda': lru_lambda[l],
             'hg_norm_w': hg_norm_w[l], 'ssd_conv_w': ssd_conv_w[l], 'ssd_conv_b': ssd_conv_b[l],
             'ssd_dt_bias': ssd_dt_bias[l], 'ssd_a_log': ssd_a_log[l], 'ssd_d': ssd_d[l],
             'ssd_norm_w': ssd_norm_w[l], 'w_branch': w_branch[l], 'w_gate': w_gate[l], 'w_out': w_out[l],
             'ffn_w_up': ffn_w_up[l], 'ffn_w_val': ffn_w_val[l], 'ffn_conv_w': ffn_conv_w[l],
             'ffn_conv_b': ffn_conv_b[l], 'ffn_w_down': ffn_w_down[l]}
        xp, sp = trunk_layer(xp, 0, zero_states, P, lbs[l])
        sample_states = (state_lru_h[l], state_lru_conv[l], state_hgrn[l], state_ssd[l],
                         state_ssd_conv[l], state_ret[l], state_ffn_conv[l])
        xs, ss = trunk_layer(xs, PAST_LEN, sample_states, P, lbs[l])
        new_p.append(sp)
        new_s.append(ss)

    def stack(lst, i):
        return jnp.stack([st[i] for st in lst], axis=0)

    y_prompt = rmsnorm(xp, g_final)
    y_sample = rmsnorm(xs, g_final)
    return (y_prompt, y_sample,
            stack(new_p, 0), stack(new_s, 0), stack(new_p, 1), stack(new_s, 1),
            stack(new_p, 2), stack(new_s, 2), stack(new_p, 3), stack(new_s, 3),
            stack(new_p, 4), stack(new_s, 4), stack(new_p, 5), stack(new_s, 5),
            stack(new_p, 6), stack(new_s, 6))
```

```python
import functools
import math

import numpy as np
import jax
import jax.numpy as jnp
from jax import lax
from jax.experimental import pallas as pl
from jax.experimental.pallas import tpu as pltpu

F32 = jnp.float32
BF16 = jnp.bfloat16
HIGHEST = lax.Precision.HIGHEST

D_MODEL = 2048
BATCH = 4
SEQ = 2048
DEPTH = 2
DEC_BATCH = 128
PAST_LEN = 16384
BRANCH_W = D_MODEL // 2
EPS = 1e-6
LRU_BLOCKS = 8
LRU_BLOCK = BRANCH_W // LRU_BLOCKS
LRU_C = 8.0
CONV_W = 4
HG_HEADS = 8
HG_DK = BRANCH_W // HG_HEADS
SSD_HEADDIM = 64
SSD_HEADS = BRANCH_W // SSD_HEADDIM
SSD_GROUPS = 2
SSD_STATE = 128
SSD_BC = 2 * SSD_GROUPS * SSD_STATE
SSD_CONV_DIM = BRANCH_W + SSD_BC
RET_HEADS = 8
RET_DK = BRANCH_W // RET_HEADS
ROPE_BASE = 10000.0
D_FF = 5632
FFN_CONV_W = 3

V7X_VMEM_BYTES = 64 * 1024 * 1024
VMEM_LIMIT_BYTES = V7X_VMEM_BYTES - 8 * 1024 * 1024
LANES = 128
SUBLANES = 8

N_IN_ORIG = 12816
OFF_XA, OFF_YA = 0, 1024
OFF_HQ, OFF_HF, OFF_HI, OFF_HG = 2048, 3072, 4096, 5120
OFF_SZ, OFF_SX, OFF_SBC = 6144, 7168, 8192
OFF_RQ, OFF_RK, OFF_RV, OFF_RG = 8704, 9728, 10752, 11776
OFF_SDT = 12800
N_IN_PAD = 13056
ORIG_SDT = 8704
MM_TN = 768

HG_CHUNK = 128
HG_LEVELS = (8, 16, 32, 64)
SSD_CHUNK = 128
RET_CHUNK = 256
SAMPLE_BLK = 32
SSD_SAMPLE_BLK = 8


def _cparams(n_axes):
    return pltpu.CompilerParams(dimension_semantics=("arbitrary",) * n_axes,
                                vmem_limit_bytes=VMEM_LIMIT_BYTES)


def _rms_rows(x):
    return x * lax.rsqrt(jnp.mean(x * x, axis=-1, keepdims=True) + EPS)


def _shift_rows(x, d, row):
    return jnp.where(row >= d, pltpu.roll(x, d, axis=0), 0.0)


def _nt_dot(a, b):
    return lax.dot_general(a, b, (((1,), (1,)), ((), ())), preferred_element_type=F32)


def _tn_dot(a, b):
    return lax.dot_general(a, b, (((0,), (0,)), ((), ())), preferred_element_type=F32)


def _rows_to_cols(x):
    n = x.shape[0]
    if n < LANES:
        x = jnp.concatenate([x, jnp.zeros((LANES - n, x.shape[1]), x.dtype)], axis=0)
    return x.T


def _norm_kernel(x_ref, g_ref, o_ref):
    o_ref[...] = (_rms_rows(x_ref[...]) * g_ref[...]).astype(o_ref.dtype)


def rmsnorm(x, g, out_dtype, tm):
    m, d = x.shape
    return pl.pallas_call(
        _norm_kernel,
        out_shape=jax.ShapeDtypeStruct((m, d), out_dtype),
        grid=(m // tm,),
        in_specs=[pl.BlockSpec((tm, d), lambda i: (i, 0)),
                  pl.BlockSpec((1, d), lambda i: (0, 0))],
        out_specs=pl.BlockSpec((tm, d), lambda i: (i, 0)),
        compiler_params=_cparams(1),
        name="rmsnorm",
    )(x, g.reshape(1, d))


def _mm_kernel(a_ref, b_ref, o_ref):
    o_ref[...] = jnp.dot(a_ref[...], b_ref[...], preferred_element_type=F32)


def in_proj(h, w, tm):
    m, k = h.shape
    n = w.shape[1]
    return pl.pallas_call(
        _mm_kernel,
        out_shape=jax.ShapeDtypeStruct((m, n), F32),
        grid=(m // tm, n // MM_TN),
        in_specs=[pl.BlockSpec((tm, k), lambda i, j: (i, 0)),
                  pl.BlockSpec((k, MM_TN), lambda i, j: (0, j))],
        out_specs=pl.BlockSpec((tm, MM_TN), lambda i, j: (i, j)),
        compiler_params=_cparams(2),
        name="in_proj",
    )(h, w)


def _merge_kernel(h_ref, a_ref, b_ref, c_ref, d_ref, g0, g1, g2, g3, w0, w1, w2, w3, o_ref):
    h = h_ref[...]
    acc = None
    for br_ref, g_ref, w_ref in ((a_ref, g0, w0), (b_ref, g1, w1), (c_ref, g2, w2), (d_ref, g3, w3)):
        gate = jax.nn.sigmoid(jnp.dot(h, g_ref[...], preferred_element_type=F32))
        br = jnp.dot(br_ref[...].astype(BF16), w_ref[...], preferred_element_type=F32)
        acc = gate * br if acc is None else acc + gate * br
    o_ref[...] = acc.astype(o_ref.dtype)


def gated_merge(h, branches, w_gate, w_branch, tm, tn):
    m = h.shape[0]
    nj = D_MODEL // tn
    br_specs = [pl.BlockSpec((tm, BRANCH_W), lambda i, j: (i, 0)) for _ in range(4)]
    g_specs = [pl.BlockSpec((D_MODEL, tn), functools.partial(lambda i, j, k: (0, k * nj + j), k=k))
               for k in range(4)]
    w_specs = [pl.BlockSpec((None, BRANCH_W, tn), functools.partial(lambda i, j, k: (k, 0, j), k=k))
               for k in range(4)]
    return pl.pallas_call(
        _merge_kernel,
        out_shape=jax.ShapeDtypeStruct((m, D_MODEL), BF16),
        grid=(m // tm, nj),
        in_specs=[pl.BlockSpec((tm, D_MODEL), lambda i, j: (i, 0))] + br_specs + g_specs + w_specs,
        out_specs=pl.BlockSpec((tm, tn), lambda i, j: (i, j)),
        compiler_params=_cparams(2),
        name="gated_merge",
    )(h, *branches, w_gate, w_gate, w_gate, w_gate, w_branch, w_branch, w_branch, w_branch)


def _out_proj_kernel(m_ref, w_ref, x_ref, g_ref, xo_ref, ho_ref):
    x_new = x_ref[...] + jnp.dot(m_ref[...], w_ref[...], preferred_element_type=F32)
    xo_ref[...] = x_new
    ho_ref[...] = (_rms_rows(x_new) * g_ref[...]).astype(ho_ref.dtype)


def out_proj_residual_norm(merged, w_out, x, g, tm):
    m = x.shape[0]
    return pl.pallas_call(
        _out_proj_kernel,
        out_shape=(jax.ShapeDtypeStruct((m, D_MODEL), F32), jax.ShapeDtypeStruct((m, D_MODEL), BF16)),
        grid=(m // tm,),
        in_specs=[pl.BlockSpec((tm, D_MODEL), lambda i: (i, 0)),
                  pl.BlockSpec((D_MODEL, D_MODEL), lambda i: (0, 0)),
                  pl.BlockSpec((tm, D_MODEL), lambda i: (i, 0)),
                  pl.BlockSpec((1, D_MODEL), lambda i: (0, 0))],
        out_specs=(pl.BlockSpec((tm, D_MODEL), lambda i: (i, 0)),
                   pl.BlockSpec((tm, D_MODEL), lambda i: (i, 0))),
        compiler_params=_cparams(1),
        name="out_proj",
    )(merged, w_out, x, g.reshape(1, D_MODEL))


def _down_proj_kernel(emit_x, a_ref, w_ref, x_ref, g_ref, *refs):
    if emit_x:
        xo_ref, no_ref, acc_ref = refs
    else:
        no_ref, acc_ref = refs
    kk = pl.program_id(1)

    @pl.when(kk == 0)
    def _():
        acc_ref[...] = x_ref[...]

    acc_ref[...] += jnp.dot(a_ref[...], w_ref[...], preferred_element_type=F32)

    @pl.when(kk == pl.num_programs(1) - 1)
    def _():
        x_new = acc_ref[...]
        if emit_x:
            xo_ref[...] = x_new
        no_ref[...] = (_rms_rows(x_new) * g_ref[...]).astype(no_ref.dtype)


def down_proj_residual_norm(a, w_down, x, g, tm, tk, emit_x, norm_dtype):
    m = x.shape[0]
    out_shape = [jax.ShapeDtypeStruct((m, D_MODEL), norm_dtype)]
    out_specs = [pl.BlockSpec((tm, D_MODEL), lambda i, k: (i, 0))]
    if emit_x:
        out_shape = [jax.ShapeDtypeStruct((m, D_MODEL), F32)] + out_shape
        out_specs = [pl.BlockSpec((tm, D_MODEL), lambda i, k: (i, 0))] + out_specs
    return pl.pallas_call(
        functools.partial(_down_proj_kernel, emit_x),
        out_shape=tuple(out_shape),
        grid=(m // tm, D_FF // tk),
        in_specs=[pl.BlockSpec((tm, tk), lambda i, k: (i, k)),
                  pl.BlockSpec((tk, D_MODEL), lambda i, k: (k, 0)),
                  pl.BlockSpec((tm, D_MODEL), lambda i, k: (i, 0)),
                  pl.BlockSpec((1, D_MODEL), lambda i, k: (0, 0))],
        out_specs=tuple(out_specs),
        scratch_shapes=[pltpu.VMEM((tm, D_MODEL), F32)],
        compiler_params=_cparams(2),
        name="down_proj",
    )(a, w_down, x, g.reshape(1, D_MODEL))


def _ffn_prompt_kernel(h_ref, wu_ref, wv_ref, cw_ref, cb_ref, a_ref, st_ref):
    h = h_ref[...]
    u = jnp.dot(h, wu_ref[...], preferred_element_type=F32)
    v = jnp.dot(h, wv_ref[...], preferred_element_type=F32)
    row = lax.broadcasted_iota(jnp.int32, u.shape, 0)
    cw = cw_ref[...]
    uc = cb_ref[...] + cw[2:3] * u + cw[1:2] * _shift_rows(u, 1, row) + cw[0:1] * _shift_rows(u, 2, row)
    a_ref[...] = (jax.nn.gelu(uc) * v).astype(a_ref.dtype)
    t = u.shape[0]
    st_ref[...] = u[t - (FFN_CONV_W - 1):, :]


def ffn_prompt(h2, w_up, w_val, conv_w, conv_b, tn):
    return pl.pallas_call(
        _ffn_prompt_kernel,
        out_shape=(jax.ShapeDtypeStruct((BATCH * SEQ, D_FF), BF16),
                   jax.ShapeDtypeStruct((BATCH, FFN_CONV_W - 1, D_FF), F32)),
        grid=(BATCH, D_FF // tn),
        in_specs=[pl.BlockSpec((SEQ, D_MODEL), lambda b, j: (b, 0)),
                  pl.BlockSpec((D_MODEL, tn), lambda b, j: (0, j)),
                  pl.BlockSpec((D_MODEL, tn), lambda b, j: (0, j)),
                  pl.BlockSpec((FFN_CONV_W, tn), lambda b, j: (0, j)),
                  pl.BlockSpec((1, tn), lambda b, j: (0, j))],
        out_specs=(pl.BlockSpec((SEQ, tn), lambda b, j: (b, j)),
                   pl.BlockSpec((None, FFN_CONV_W - 1, tn), lambda b, j: (b, 0, j))),
        compiler_params=_cparams(2),
        name="ffn_prompt",
    )(h2, w_up, w_val, conv_w, conv_b.reshape(1, D_FF))


def _ffn_sample_kernel(h_ref, wu_ref, wv_ref, cw_ref, cb_ref, buf_ref, a_ref, nb_ref):
    h = h_ref[...]
    u = jnp.dot(h, wu_ref[...], preferred_element_type=F32)
    v = jnp.dot(h, wv_ref[...], preferred_element_type=F32)
    cw = cw_ref[...]
    b0 = buf_ref[0]
    b1 = buf_ref[1]
    uc = cb_ref[...] + cw[0:1] * b0 + cw[1:2] * b1 + cw[2:3] * u
    a_ref[...] = (jax.nn.gelu(uc) * v).astype(a_ref.dtype)
    nb_ref[0] = b1
    nb_ref[1] = u


def ffn_sample(h2, w_up, w_val, conv_w, conv_b, buf_t, tn):
    return pl.pallas_call(
        _ffn_sample_kernel,
        out_shape=(jax.ShapeDtypeStruct((DEC_BATCH, D_FF), BF16),
                   jax.ShapeDtypeStruct((FFN_CONV_W - 1, DEC_BATCH, D_FF), F32)),
        grid=(D_FF // tn,),
        in_specs=[pl.BlockSpec((DEC_BATCH, D_MODEL), lambda j: (0, 0)),
                  pl.BlockSpec((D_MODEL, tn), lambda j: (0, j)),
                  pl.BlockSpec((D_MODEL, tn), lambda j: (0, j)),
                  pl.BlockSpec((FFN_CONV_W, tn), lambda j: (0, j)),
                  pl.BlockSpec((1, tn), lambda j: (0, j)),
                  pl.BlockSpec((FFN_CONV_W - 1, DEC_BATCH, tn), lambda j: (0, 0, j))],
        out_specs=(pl.BlockSpec((DEC_BATCH, tn), lambda j: (0, j)),
                   pl.BlockSpec((FFN_CONV_W - 1, DEC_BATCH, tn), lambda j: (0, 0, j))),
        compiler_params=_cparams(1),
        name="ffn_sample",
    )(h2, w_up, w_val, conv_w, conv_b.reshape(1, D_FF), buf_t)


def _rope_table_kernel(start, consecutive, freq_ref, sign_ref, cos_ref, sin_ref):
    shape = cos_ref.shape
    if consecutive:
        pos = lax.broadcasted_iota(jnp.int32, shape, 0).astype(F32) + float(start)
    else:
        pos = jnp.full(shape, float(start), F32)
    ang = pos * freq_ref[...]
    cos_ref[...] = jnp.cos(ang)
    sin_ref[...] = sign_ref[...] * jnp.sin(ang)


def rope_tables(n_rows, start, consecutive):
    half = RET_DK // 2
    freqs = ROPE_BASE ** (-jnp.arange(half, dtype=F32) / half)
    freq2 = jnp.concatenate([freqs, freqs]).reshape(1, RET_DK)
    sign = jnp.concatenate([-jnp.ones((half,), F32), jnp.ones((half,), F32)]).reshape(1, RET_DK)
    return pl.pallas_call(
        functools.partial(_rope_table_kernel, start, consecutive),
        out_shape=(jax.ShapeDtypeStruct((n_rows, RET_DK), F32), jax.ShapeDtypeStruct((n_rows, RET_DK), F32)),
        name="rope_tables",
    )(freq2, sign)


def _rope(x, cos, sin_signed):
    return x * cos + pltpu.roll(x, RET_DK // 2, axis=1) * sin_signed


def _lru_gates(conv, wa_ref, ba_ref, wx_ref, bx_ref, lam_ref):
    xb = conv.astype(BF16)
    r = jax.nn.sigmoid(jnp.dot(xb, wa_ref[...].astype(BF16), preferred_element_type=F32) + ba_ref[...])
    i = jax.nn.sigmoid(jnp.dot(xb, wx_ref[...].astype(BF16), preferred_element_type=F32) + bx_ref[...])
    log_a = -LRU_C * r * jax.nn.softplus(-lam_ref[...])
    a = jnp.exp(log_a)
    u = jnp.sqrt(1.0 - jnp.exp(2.0 * log_a)) * (i * conv)
    return a, u


def _lru_prompt_kernel(xa_ref, ya_ref, cw_ref, cb_ref, wa_ref, ba_ref, wx_ref, bx_ref, lam_ref,
                       out_ref, h_ref, conv_ref):
    x = xa_ref[...]
    t_len = x.shape[0]
    row = lax.broadcasted_iota(jnp.int32, x.shape, 0)
    cw = cw_ref[...]
    conv = cb_ref[...] + cw[CONV_W - 1:CONV_W] * x
    for d in range(1, CONV_W):
        conv = conv + cw[CONV_W - 1 - d:CONV_W - d] * _shift_rows(x, d, row)
    a, u = _lru_gates(conv, wa_ref, ba_ref, wx_ref, bx_ref, lam_ref)
    d = 1
    while d < t_len:
        keep = row >= d
        u = jnp.where(keep, a * pltpu.roll(u, d, axis=0) + u, u)
        a = jnp.where(keep, a * pltpu.roll(a, d, axis=0), a)
        d *= 2
    out_ref[...] = (u * jax.nn.gelu(ya_ref[...])).astype(out_ref.dtype)
    h_ref[...] = u[t_len - 1:, :]
    conv_ref[...] = xa_ref[pl.ds(t_len - (CONV_W - 1), CONV_W - 1), :]


def _lru_param_specs(n_axes_fn):
    blk3 = lambda shape: pl.BlockSpec(shape, n_axes_fn(lambda n: (n, 0, 0)))
    return [pl.BlockSpec((CONV_W, LRU_BLOCK), n_axes_fn(lambda n: (0, n))),
            pl.BlockSpec((1, LRU_BLOCK), n_axes_fn(lambda n: (0, n))),
            blk3((None, LRU_BLOCK, LRU_BLOCK)), blk3((None, 1, LRU_BLOCK)),
            blk3((None, LRU_BLOCK, LRU_BLOCK)), blk3((None, 1, LRU_BLOCK)),
            blk3((None, 1, LRU_BLOCK))]


def _lru_params(p):
    return (p["lru_conv_w"], p["lru_conv_b"].reshape(1, BRANCH_W),
            p["lru_wa"], p["lru_ba"].reshape(LRU_BLOCKS, 1, LRU_BLOCK),
            p["lru_wx"], p["lru_bx"].reshape(LRU_BLOCKS, 1, LRU_BLOCK),
            p["lru_lambda"].reshape(LRU_BLOCKS, 1, LRU_BLOCK))


def lru_prompt(proj, p):
    cb = lambda off: off // LRU_BLOCK
    wrap = lambda f: (lambda b, n: f(n))
    out, h, conv = pl.pallas_call(
        _lru_prompt_kernel,
        out_shape=(jax.ShapeDtypeStruct((BATCH * SEQ, BRANCH_W), BF16),
                   jax.ShapeDtypeStruct((BATCH, 1, BRANCH_W), F32),
                   jax.ShapeDtypeStruct((BATCH, CONV_W - 1, BRANCH_W), F32)),
        grid=(BATCH, LRU_BLOCKS),
        in_specs=[pl.BlockSpec((SEQ, LRU_BLOCK), lambda b, n: (b, cb(OFF_XA) + n)),
                  pl.BlockSpec((SEQ, LRU_BLOCK), lambda b, n: (b, cb(OFF_YA) + n))] + _lru_param_specs(wrap),
        out_specs=(pl.BlockSpec((SEQ, LRU_BLOCK), lambda b, n: (b, n)),
                   pl.BlockSpec((None, 1, LRU_BLOCK), lambda b, n: (b, 0, n)),
                   pl.BlockSpec((None, CONV_W - 1, LRU_BLOCK), lambda b, n: (b, 0, n))),
        compiler_params=_cparams(2),
        name="lru_prompt",
    )(proj, proj, *_lru_params(p))
    return out, h.reshape(BATCH, BRANCH_W), conv


def _lru_sample_kernel(xa_ref, ya_ref, cw_ref, cb_ref, wa_ref, ba_ref, wx_ref, bx_ref, lam_ref,
                       h0_ref, buf_ref, out_ref, h_ref, nbuf_ref):
    x = xa_ref[...]
    cw = cw_ref[...]
    conv = cb_ref[...] + cw[CONV_W - 1:CONV_W] * x
    for j in range(CONV_W - 1):
        conv = conv + cw[j:j + 1] * buf_ref[j]
    a, u = _lru_gates(conv, wa_ref, ba_ref, wx_ref, bx_ref, lam_ref)
    h = a * h0_ref[...] + u
    out_ref[...] = h * jax.nn.gelu(ya_ref[...])
    h_ref[...] = h
    for j in range(CONV_W - 2):
        nbuf_ref[j] = buf_ref[j + 1]
    nbuf_ref[CONV_W - 2] = x


def lru_sample(proj, p, h0, buf_t):
    cb = lambda off: off // LRU_BLOCK
    wrap = lambda f: f
    return pl.pallas_call(
        _lru_sample_kernel,
        out_shape=(jax.ShapeDtypeStruct((DEC_BATCH, BRANCH_W), F32),
                   jax.ShapeDtypeStruct((DEC_BATCH, BRANCH_W), F32),
                   jax.ShapeDtypeStruct((CONV_W - 1, DEC_BATCH, BRANCH_W), F32)),
        grid=(LRU_BLOCKS,),
        in_specs=[pl.BlockSpec((DEC_BATCH, LRU_BLOCK), lambda n: (0, cb(OFF_XA) + n)),
                  pl.BlockSpec((DEC_BATCH, LRU_BLOCK), lambda n: (0, cb(OFF_YA) + n))]
                 + _lru_param_specs(wrap)
                 + [pl.BlockSpec((DEC_BATCH, LRU_BLOCK), lambda n: (0, n)),
                    pl.BlockSpec((CONV_W - 1, DEC_BATCH, LRU_BLOCK), lambda n: (0, 0, n))],
        out_specs=(pl.BlockSpec((DEC_BATCH, LRU_BLOCK), lambda n: (0, n)),
                   pl.BlockSpec((DEC_BATCH, LRU_BLOCK), lambda n: (0, n)),
                   pl.BlockSpec((CONV_W - 1, DEC_BATCH, LRU_BLOCK), lambda n: (0, 0, n))),
        compiler_params=_cparams(1),
        name="lru_sample",
    )(proj, proj, *_lru_params(p), h0, buf_t)


def _hgrn_lower_bound(layer, logits):
    mx = jnp.max(logits, axis=0, keepdims=True)
    e = jnp.exp(logits - mx)
    ls = e / jnp.sum(e, axis=0, keepdims=True)
    lb = jnp.zeros_like(ls[0:1])
    for i in range(1, layer + 1):
        lb = lb + ls[i:i + 1]
    return lb


def _hgrn_gates(layer, hq, hf, lbl_ref):
    lb = _hgrn_lower_bound(layer, lbl_ref[...])
    q = jax.nn.silu(hq)
    f = lb + (1.0 - lb) * jax.nn.sigmoid(hf)
    k = (1.0 - lb) * jax.nn.sigmoid(-hf)
    return q, f, k


def _hgrn_select_matrices():
    c = HG_CHUNK
    t = np.arange(c)[:, None]
    s = np.arange(c)[None, :]
    mats = [(s <= t)]
    level = np.zeros((c, c), np.int32)
    for li, m in enumerate(HG_LEVELS):
        same = (t // (2 * m)) == (s // (2 * m))
        t_hi = (t % (2 * m)) >= m
        s_hi = (s % (2 * m)) >= m
        mats.append(same & t_hi & s_hi & (s <= t))
        mats.append(same & ~t_hi & ~s_hi & (s > t))
        level[same & t_hi & ~s_hi] = li + 1
    sel = np.concatenate(mats, axis=0).astype(np.float32)
    return sel, level


def _hgrn_prompt_kernel(layer, q_ref, f_ref, i_ref, g_ref, lbl_ref, nw_ref, sel_ref, lvl_ref,
                        out_ref, st_ref, lf_s, q_s, k_s, od_s, o_s, st_s):
    c = HG_CHUNK
    t_len = q_ref.shape[0]
    q, f, k = _hgrn_gates(layer, q_ref[...], f_ref[...], lbl_ref)
    logf = jnp.log(f)
    lf_s[...] = logf
    q_s[...] = q
    k_s[...] = k
    v = i_ref[...]

    row = lax.broadcasted_iota(jnp.int32, q.shape, 0)
    sub = jnp.bitwise_and(row, SUBLANES - 1)
    ones = jnp.ones((LANES, LANES), BF16)
    dd = jnp.zeros_like(logf)
    od = None
    for d in range(SUBLANES):
        if d > 0:
            dd = dd + pltpu.roll(logf, d - 1, axis=0) if d > 1 else dd + logf
        kd = k if d == 0 else pltpu.roll(k, d, axis=0)
        vd = v if d == 0 else pltpu.roll(v, d, axis=0)
        p = q * jnp.exp(dd) * kd if d > 0 else q * kd
        p = jnp.where(sub >= d, p, 0.0)
        r = jnp.dot(p.astype(BF16), ones, preferred_element_type=F32)
        od = r * vd if od is None else od + r * vd
    od_s[...] = od
    st_s[...] = jnp.zeros_like(st_s)

    lvl = lvl_ref[...]

    def chunk(ci, carry):
        sl = pl.ds(pl.multiple_of(ci * c, c), c)
        lf = lf_s[sl, :]
        qc = q_s[sl, :]
        kc = k_s[sl, :]
        vb = i_ref[sl, :].astype(BF16)
        cs = jnp.dot(sel_ref[...], lf, precision=HIGHEST, preferred_element_type=F32)
        b = cs[0:c]
        att = jnp.zeros((c, c), F32)
        for li in range(len(HG_LEVELS)):
            dq = cs[(1 + 2 * li) * c:(2 + 2 * li) * c]
            dk = cs[(2 + 2 * li) * c:(3 + 2 * li) * c]
            a_l = _nt_dot((qc * jnp.exp(dq)).astype(BF16), (kc * jnp.exp(dk)).astype(BF16))
            att = jnp.where(lvl == li + 1, a_l, att)
        st = st_s[...]
        o = (jnp.dot(att.astype(BF16), vb, preferred_element_type=F32) + od_s[sl, :]
             + _nt_dot((qc * jnp.exp(b)).astype(BF16), st.astype(BF16)))
        o_s[sl, :] = o
        bl = b[c - 1:c, :]
        kdec = (kc * jnp.exp(bl - b)).astype(BF16)
        st_s[...] = st * jnp.exp(bl) + _tn_dot(vb, kdec)
        return carry

    lax.fori_loop(0, t_len // c, chunk, 0)
    o = o_s[...]
    out_ref[...] = (_rms_rows(o) * nw_ref[...] * jax.nn.silu(g_ref[...])).astype(out_ref.dtype)
    st_ref[...] = st_s[...].T


def hgrn_prompt(layer, proj, p):
    cb = lambda off: off // HG_DK
    sel, level = _hgrn_select_matrices()
    col = lambda off: pl.BlockSpec((SEQ, HG_DK), lambda b, h: (b, cb(off) + h))
    return pl.pallas_call(
        functools.partial(_hgrn_prompt_kernel, layer),
        out_shape=(jax.ShapeDtypeStruct((BATCH * SEQ, BRANCH_W), BF16),
                   jax.ShapeDtypeStruct((BATCH, HG_HEADS, HG_DK, HG_DK), F32)),
        grid=(BATCH, HG_HEADS),
        in_specs=[col(OFF_HQ), col(OFF_HF), col(OFF_HI), col(OFF_HG),
                  pl.BlockSpec((DEPTH, HG_DK), lambda b, h: (0, h)),
                  pl.BlockSpec((1, HG_DK), lambda b, h: (0, 0)),
                  pl.BlockSpec(sel.shape, lambda b, h: (0, 0)),
                  pl.BlockSpec(level.shape, lambda b, h: (0, 0))],
        out_specs=(pl.BlockSpec((SEQ, HG_DK), lambda b, h: (b, h)),
                   pl.BlockSpec((None, None, HG_DK, HG_DK), lambda b, h: (b, h, 0, 0))),
        scratch_shapes=[pltpu.VMEM((SEQ, HG_DK), F32)] * 5 + [pltpu.VMEM((HG_DK, HG_DK), F32)],
        compiler_params=_cparams(2),
        name="hgrn_prompt",
    )(proj, proj, proj, proj, p["hg_lb_logits"], p["hg_norm_w"].reshape(1, HG_DK),
      jnp.asarray(sel), jnp.asarray(level))


def _hgrn_sample_kernel(layer, q_ref, f_ref, i_ref, g_ref, lbl_ref, nw_ref, s_ref, out_ref, so_ref, o_s):
    q, f, k = _hgrn_gates(layer, q_ref[...], f_ref[...], lbl_ref)
    qc, fc, kc = _rows_to_cols(q), _rows_to_cols(f), _rows_to_cols(k)
    nb = q.shape[0]
    for j in range(nb):
        s_new = s_ref[j] * fc[:, j:j + 1] + kc[:, j:j + 1] * i_ref[j:j + 1, :]
        so_ref[j] = s_new
        o_s[j:j + 1, :] = jnp.sum(s_new * qc[:, j:j + 1], axis=0, keepdims=True)
    out_ref[...] = _rms_rows(o_s[...]) * nw_ref[...] * jax.nn.silu(g_ref[...])


def hgrn_sample(layer, proj, p, state, state_out):
    cb = lambda off: off // HG_DK
    nb = SAMPLE_BLK
    col = lambda off: pl.BlockSpec((nb, HG_DK), lambda h, i: (i, cb(off) + h))
    st_spec = pl.BlockSpec((None, nb, None, HG_DK, HG_DK), lambda h, i: (layer, i, h, 0, 0))
    args = [proj, proj, proj, proj, p["hg_lb_logits"], p["hg_norm_w"].reshape(1, HG_DK), state]
    in_specs = [col(OFF_HQ), col(OFF_HF), col(OFF_HI), col(OFF_HG),
                pl.BlockSpec((DEPTH, HG_DK), lambda h, i: (0, h)),
                pl.BlockSpec((1, HG_DK), lambda h, i: (0, 0)),
                st_spec]
    aliases = {}
    kern = functools.partial(_hgrn_sample_kernel, layer)
    if state_out is not None:
        args.append(state_out)
        in_specs.append(pl.BlockSpec(memory_space=pl.ANY))
        aliases = {len(args) - 1: 1}
        kern = functools.partial(_drop_alias_arg, kern, 7)
    return pl.pallas_call(
        kern,
        out_shape=(jax.ShapeDtypeStruct((DEC_BATCH, BRANCH_W), F32),
                   jax.ShapeDtypeStruct(state.shape, F32)),
        grid=(HG_HEADS, DEC_BATCH // nb),
        in_specs=in_specs,
        out_specs=(pl.BlockSpec((nb, HG_DK), lambda h, i: (i, h)), st_spec),
        scratch_shapes=[pltpu.VMEM((nb, HG_DK), F32)],
        input_output_aliases=aliases,
        compiler_params=_cparams(2),
        name="hgrn_sample",
    )(*args)


def _drop_alias_arg(kern, pos, *refs):
    return kern(*refs[:pos], *refs[pos + 1:])


def _head_pair(cols, h0, lane_lo):
    return jnp.where(lane_lo, cols[:, h0:h0 + 1], cols[:, h0 + 1:h0 + 2])


def _ssd_prompt_kernel(z_ref, x_ref, bc_ref, dt_ref, cw_ref, cb_ref, dtb_ref, alog_ref, dpar_ref, nw_ref,
                       tril_ref, out_ref, st_ref, cst_ref, cx_s, cbc_s, s_s):
    c = SSD_CHUNK
    ci = pl.program_id(1)

    @pl.when(ci == 0)
    def _():
        cx_s[...] = jnp.zeros_like(cx_s)
        cbc_s[...] = jnp.zeros_like(cbc_s)
        s_s[...] = jnp.zeros_like(s_s)

    cw = cw_ref[...]
    cbias = cb_ref[...]

    def conv_silu(raw, carry_ref, lo, hi):
        xx = jnp.concatenate([carry_ref[...], raw], axis=0)
        y = cbias[:, lo:hi] + cw[CONV_W - 1:CONV_W, lo:hi] * raw
        for d in range(1, CONV_W):
            y = y + cw[CONV_W - 1 - d:CONV_W - d, lo:hi] * pltpu.roll(xx, d, axis=0)[SUBLANES:]
        carry_ref[...] = raw[c - SUBLANES:, :]
        return jax.nn.silu(y)

    x_raw = x_ref[...]
    bc_raw = bc_ref[...]
    xs = conv_silu(x_raw, cx_s, 0, BRANCH_W)
    bc = conv_silu(bc_raw, cbc_s, BRANCH_W, SSD_CONV_DIM)

    dt = jax.nn.softplus(dt_ref[...] + dtb_ref[...])
    a_neg = -jnp.exp(alog_ref[...])
    logd = dt * a_neg
    b = jnp.dot(tril_ref[...], logd, precision=HIGHEST, preferred_element_type=F32)
    b_t = b.T
    bl = b[c - 1:c, :]
    e_in = jnp.exp(b)
    w_out = jnp.exp(bl - b)
    e_last = jnp.exp(bl)
    dfull = dpar_ref[...]

    tri = lax.broadcasted_iota(jnp.int32, (c, c), 0) >= lax.broadcasted_iota(jnp.int32, (c, c), 1)
    lane_lo = lax.broadcasted_iota(jnp.int32, (c, LANES), 1) < SSD_HEADDIM
    lane_lo_row = lax.broadcasted_iota(jnp.int32, (1, LANES), 1) < SSD_HEADDIM

    ys = []
    for g in range(SSD_GROUPS):
        bm = bc[:, g * SSD_STATE:(g + 1) * SSD_STATE].astype(BF16)
        cm = bc[:, (SSD_GROUPS + g) * SSD_STATE:(SSD_GROUPS + g + 1) * SSD_STATE].astype(BF16)
        gmat = _nt_dot(cm, bm)
        for pp in range(SSD_HEADS // SSD_GROUPS // 2):
            pi = g * (SSD_HEADS // SSD_GROUPS // 2) + pp
            h0 = 2 * pi
            xs_p = xs[:, pi * LANES:(pi + 1) * LANES]
            vdt = xs_p * _head_pair(dt, h0, lane_lo)
            vdt_b = vdt.astype(BF16)
            o_heads = []
            for hh in (h0, h0 + 1):
                diff = b[:, hh:hh + 1] - b_t[hh:hh + 1, :]
                dec = jnp.where(tri, jnp.exp(jnp.where(tri, diff, 0.0)), 0.0)
                o_heads.append(jnp.dot((gmat * dec).astype(BF16), vdt_b, preferred_element_type=F32))
            o_intra = jnp.where(lane_lo, o_heads[0], o_heads[1])
            s_p = s_s[pi]
            o_inter = _head_pair(e_in, h0, lane_lo) * jnp.dot(cm, s_p.astype(BF16), preferred_element_type=F32)
            ys.append(o_intra + o_inter + dfull[:, pi * LANES:(pi + 1) * LANES] * xs_p)
            upd = _tn_dot(bm, (vdt * _head_pair(w_out, h0, lane_lo)).astype(BF16))
            s_s[pi] = s_p * _head_pair(e_last, h0, lane_lo_row) + upd

    y = jnp.concatenate(ys, axis=1) * jax.nn.silu(z_ref[...])
    gw = BRANCH_W // SSD_GROUPS
    nw = nw_ref[...]
    outs = [_rms_rows(y[:, g * gw:(g + 1) * gw]) * nw[:, g * gw:(g + 1) * gw] for g in range(SSD_GROUPS)]
    out_ref[...] = jnp.concatenate(outs, axis=1).astype(out_ref.dtype)

    @pl.when(ci == pl.num_programs(1) - 1)
    def _():
        for pi in range(SSD_HEADS // 2):
            s_p = s_s[pi]
            st_ref[2 * pi] = s_p[:, :SSD_HEADDIM]
            st_ref[2 * pi + 1] = s_p[:, SSD_HEADDIM:]
        cst_ref[:, 0:BRANCH_W] = x_raw[c - (CONV_W - 1):, :]
        cst_ref[:, BRANCH_W:SSD_CONV_DIM] = bc_raw[c - (CONV_W - 1):, :]


def _pad_lanes(v):
    return jnp.pad(v.astype(F32), (0, LANES - v.shape[0])).reshape(1, LANES)


def _ssd_params(p):
    return (p["ssd_conv_w"], p["ssd_conv_b"].reshape(1, SSD_CONV_DIM), _pad_lanes(p["ssd_dt_bias"]),
            _pad_lanes(p["ssd_a_log"]), jnp.repeat(p["ssd_d"].astype(F32), SSD_HEADDIM).reshape(1, BRANCH_W),
            p["ssd_norm_w"].reshape(1, BRANCH_W))


def ssd_prompt(proj, p):
    c = SSD_CHUNK
    nc = SEQ // c
    tril = jnp.asarray(np.tril(np.ones((c, c), np.float32)))
    const = lambda shape: pl.BlockSpec(shape, lambda b, i: (0, 0))
    rowblk = lambda w, off: pl.BlockSpec((c, w), lambda b, i: (b * nc + i, off // w))
    return pl.pallas_call(
        _ssd_prompt_kernel,
        out_shape=(jax.ShapeDtypeStruct((BATCH * SEQ, BRANCH_W), BF16),
                   jax.ShapeDtypeStruct((BATCH, SSD_HEADS, SSD_STATE, SSD_HEADDIM), F32),
                   jax.ShapeDtypeStruct((BATCH, CONV_W - 1, SSD_CONV_DIM), F32)),
        grid=(BATCH, nc),
        in_specs=[rowblk(BRANCH_W, OFF_SZ), rowblk(BRANCH_W, OFF_SX), rowblk(SSD_BC, OFF_SBC),
                  rowblk(LANES, OFF_SDT),
                  const((CONV_W, SSD_CONV_DIM)), const((1, SSD_CONV_DIM)), const((1, LANES)),
                  const((1, LANES)), const((1, BRANCH_W)), const((1, BRANCH_W)), const((c, c))],
        out_specs=(pl.BlockSpec((c, BRANCH_W), lambda b, i: (b * nc + i, 0)),
                   pl.BlockSpec((None, SSD_HEADS, SSD_STATE, SSD_HEADDIM), lambda b, i: (b, 0, 0, 0)),
                   pl.BlockSpec((None, CONV_W - 1, SSD_CONV_DIM), lambda b, i: (b, 0, 0))),
        scratch_shapes=[pltpu.VMEM((SUBLANES, BRANCH_W), F32), pltpu.VMEM((SUBLANES, SSD_BC), F32),
                        pltpu.VMEM((SSD_HEADS // 2, SSD_STATE, LANES), F32)],
        compiler_params=_cparams(2),
        name="ssd_prompt",
    )(proj, proj, proj, proj, *_ssd_params(p), tril)


def _ssd_sample_kernel(z_ref, x_ref, bc_ref, dt_ref, cw_ref, cb_ref, dtb_ref, alog_ref, dpar_ref, nw_ref,
                       bufx_ref, bufbc_ref, s_ref, out_ref, so_ref, nbx_ref, nbbc_ref, y_s):
    cw = cw_ref[...]
    cbias = cb_ref[...]

    def conv_silu(raw, buf_ref, nbuf_ref, lo, hi):
        y = cbias[:, lo:hi] + cw[CONV_W - 1:CONV_W, lo:hi] * raw
        for j in range(CONV_W - 1):
            y = y + cw[j:j + 1, lo:hi] * buf_ref[j]
        for j in range(CONV_W - 2):
            nbuf_ref[j] = buf_ref[j + 1]
        nbuf_ref[CONV_W - 2] = raw
        return jax.nn.silu(y)

    xs = conv_silu(x_ref[...], bufx_ref, nbx_ref, 0, BRANCH_W)
    bc = conv_silu(bc_ref[...], bufbc_ref, nbbc_ref, BRANCH_W, SSD_CONV_DIM)
    dt = jax.nn.softplus(dt_ref[...] + dtb_ref[...])
    decay = jnp.exp(dt * (-jnp.exp(alog_ref[...])))
    nb = xs.shape[0]
    bcols = [_rows_to_cols(bc[:, g * SSD_STATE:(g + 1) * SSD_STATE]) for g in range(SSD_GROUPS)]
    ccols = [_rows_to_cols(bc[:, (SSD_GROUPS + g) * SSD_STATE:(SSD_GROUPS + g + 1) * SSD_STATE])
             for g in range(SSD_GROUPS)]
    hpg = SSD_HEADS // SSD_GROUPS
    for j in range(nb):
        for h in range(SSD_HEADS):
            g = h // hpg
            lo, hi = h * SSD_HEADDIM, (h + 1) * SSD_HEADDIM
            xdt = xs[j:j + 1, lo:hi] * dt[j:j + 1, h:h + 1]
            s_new = s_ref[j, h] * decay[j:j + 1, h:h + 1] + bcols[g][:, j:j + 1] * xdt
            so_ref[j, h] = s_new
            y_s[j:j + 1, lo:hi] = jnp.sum(s_new * ccols[g][:, j:j + 1], axis=0, keepdims=True)
    y = (y_s[...] + dpar_ref[...] * xs) * jax.nn.silu(z_ref[...])
    gw = BRANCH_W // SSD_GROUPS
    nw = nw_ref[...]
    outs = [_rms_rows(y[:, g * gw:(g + 1) * gw]) * nw[:, g * gw:(g + 1) * gw] for g in range(SSD_GROUPS)]
    out_ref[...] = jnp.concatenate(outs, axis=1)


def ssd_sample(layer, proj, p, state, state_out, buf_t):
    nb = SSD_SAMPLE_BLK
    const = lambda shape: pl.BlockSpec(shape, lambda i: (0,) * len(shape))
    rowblk = lambda w, off: pl.BlockSpec((nb, w), lambda i: (i, off // w))
    st_spec = pl.BlockSpec((None, nb, SSD_HEADS, SSD_STATE, SSD_HEADDIM), lambda i: (layer, i, 0, 0, 0))
    bufx_spec = pl.BlockSpec((CONV_W - 1, nb, BRANCH_W), lambda i: (0, i, 0))
    bufbc_spec = pl.BlockSpec((CONV_W - 1, nb, SSD_BC), lambda i: (0, i, BRANCH_W // SSD_BC))
    args = [proj, proj, proj, proj, *_ssd_params(p), buf_t, buf_t, state]
    in_specs = [rowblk(BRANCH_W, OFF_SZ), rowblk(BRANCH_W, OFF_SX), rowblk(SSD_BC, OFF_SBC), rowblk(LANES, OFF_SDT),
                const((CONV_W, SSD_CONV_DIM)), const((1, SSD_CONV_DIM)), const((1, LANES)), const((1, LANES)),
                const((1, BRANCH_W)), const((1, BRANCH_W)), bufx_spec, bufbc_spec, st_spec]
    aliases = {}
    kern = _ssd_sample_kernel
    if state_out is not None:
        args.append(state_out)
        in_specs.append(pl.BlockSpec(memory_space=pl.ANY))
        aliases = {len(args) - 1: 1}
        kern = functools.partial(_drop_alias_arg, kern, 13)
    out, st, nbx, nbbc = pl.pallas_call(
        kern,
        out_shape=(jax.ShapeDtypeStruct((DEC_BATCH, BRANCH_W), F32),
                   jax.ShapeDtypeStruct(state.shape, F32),
                   jax.ShapeDtypeStruct((CONV_W - 1, DEC_BATCH, BRANCH_W), F32),
                   jax.ShapeDtypeStruct((CONV_W - 1, DEC_BATCH, SSD_BC), F32)),
        grid=(DEC_BATCH // nb,),
        in_specs=in_specs,
        out_specs=(pl.BlockSpec((nb, BRANCH_W), lambda i: (i, 0)), st_spec,
                   pl.BlockSpec((CONV_W - 1, nb, BRANCH_W), lambda i: (0, i, 0)),
                   pl.BlockSpec((CONV_W - 1, nb, SSD_BC), lambda i: (0, i, 0))),
        scratch_shapes=[pltpu.VMEM((nb, BRANCH_W), F32)],
        input_output_aliases=aliases,
        compiler_params=_cparams(1),
        name="ssd_sample",
    )(*args)
    return out, st, jnp.concatenate([nbx, nbbc], axis=-1)


def _ret_prompt_kernel(q_ref, k_ref, v_ref, g_ref, cos_ref, sin_ref, lg_ref, out_ref, st_ref, q_s, k_s, o_s, s_s):
    c = RET_CHUNK
    t_len = q_ref.shape[0]
    cos = cos_ref[...]
    sin = sin_ref[...]
    q_s[...] = _rope(q_ref[...], cos, sin)
    k_s[...] = _rope(k_ref[...], cos, sin) * RET_DK ** -0.5
    lg = lg_ref[...]
    lg128 = lg[:, :LANES]
    ti = lax.broadcasted_iota(jnp.int32, (c, c), 0)
    si = lax.broadcasted_iota(jnp.int32, (c, c), 1)
    tri = ti >= si
    dec = jnp.where(tri, jnp.exp(jnp.where(tri, (ti - si).astype(F32) * lg, 0.0)), 0.0)
    tt = lax.broadcasted_iota(jnp.int32, (c, LANES), 0).astype(F32)
    g_in = jnp.exp((tt + 1.0) * lg128)
    g_out = jnp.exp((c - 1.0 - tt) * lg128)
    g_all = jnp.exp(float(c) * lg128)
    s_s[...] = jnp.zeros_like(s_s)

    def chunk(ci, carry):
        sl = pl.ds(pl.multiple_of(ci * c, c), c)
        qc = q_s[sl, :]
        kc = k_s[sl, :]
        vb = v_ref[sl, :].astype(BF16)
        s = s_s[...]
        scores = _nt_dot(qc.astype(BF16), kc.astype(BF16)) * dec
        o_s[sl, :] = (jnp.dot(scores.astype(BF16), vb, preferred_element_type=F32)
                      + jnp.dot((qc * g_in).astype(BF16), s.astype(BF16), preferred_element_type=F32))
        s_s[...] = s * g_all + _tn_dot((kc * g_out).astype(BF16), vb)
        return carry

    lax.fori_loop(0, t_len // c, chunk, 0)
    out_ref[...] = (_rms_rows(o_s[...]) * jax.nn.silu(g_ref[...])).astype(out_ref.dtype)
    st_ref[...] = s_s[...]


def _log_gamma_rows(width):
    lg = jnp.log1p(-jnp.exp2(-5.0 - jnp.arange(RET_HEADS, dtype=F32)))
    return jnp.broadcast_to(lg[:, None, None], (RET_HEADS, 1, width))


def ret_prompt(proj, cos, sin):
    cb = lambda off: off // RET_DK
    col = lambda off: pl.BlockSpec((SEQ, RET_DK), lambda b, h: (b, cb(off) + h))
    tab = pl.BlockSpec((SEQ, RET_DK), lambda b, h: (0, 0))
    return pl.pallas_call(
        _ret_prompt_kernel,
        out_shape=(jax.ShapeDtypeStruct((BATCH * SEQ, BRANCH_W), BF16),
                   jax.ShapeDtypeStruct((BATCH, RET_HEADS, RET_DK, RET_DK), F32)),
        grid=(BATCH, RET_HEADS),
        in_specs=[col(OFF_RQ), col(OFF_RK), col(OFF_RV), col(OFF_RG), tab, tab,
                  pl.BlockSpec((None, 1, RET_CHUNK), lambda b, h: (h, 0, 0))],
        out_specs=(pl.BlockSpec((SEQ, RET_DK), lambda b, h: (b, h)),
                   pl.BlockSpec((None, None, RET_DK, RET_DK), lambda b, h: (b, h, 0, 0))),
        scratch_shapes=[pltpu.VMEM((SEQ, RET_DK), F32)] * 3 + [pltpu.VMEM((RET_DK, RET_DK), F32)],
        compiler_params=_cparams(2),
        name="ret_prompt",
    )(proj, proj, proj, proj, cos, sin, _log_gamma_rows(RET_CHUNK))


def _ret_sample_kernel(q_ref, k_ref, v_ref, g_ref, cos_ref, sin_ref, lg_ref, s_ref, out_ref, so_ref, o_s):
    cos = cos_ref[0:1, :]
    sin = sin_ref[0:1, :]
    q = _rope(q_ref[...], cos, sin)
    k = _rope(k_ref[...], cos, sin) * RET_DK ** -0.5
    gamma = jnp.exp(lg_ref[...])
    qc, kc = _rows_to_cols(q), _rows_to_cols(k)
    nb = q.shape[0]
    for j in range(nb):
        s_new = s_ref[j] * gamma + kc[:, j:j + 1] * v_ref[j:j + 1, :]
        so_ref[j] = s_new
        o_s[j:j + 1, :] = jnp.sum(s_new * qc[:, j:j + 1], axis=0, keepdims=True)
    out_ref[...] = _rms_rows(o_s[...]) * jax.nn.silu(g_ref[...])


def ret_sample(layer, proj, cos, sin, state, state_out):
    cb = lambda off: off // RET_DK
    nb = SAMPLE_BLK
    col = lambda off: pl.BlockSpec((nb, RET_DK), lambda h, i: (i, cb(off) + h))
    tab = pl.BlockSpec((SUBLANES, RET_DK), lambda h, i: (0, 0))
    st_spec = pl.BlockSpec((None, nb, None, RET_DK, RET_DK), lambda h, i: (layer, i, h, 0, 0))
    args = [proj, proj, proj, proj, cos, sin, _log_gamma_rows(LANES), state]
    in_specs = [col(OFF_RQ), col(OFF_RK), col(OFF_RV), col(OFF_RG), tab, tab,
                pl.BlockSpec((None, 1, LANES), lambda h, i: (h, 0, 0)), st_spec]
    aliases = {}
    kern = _ret_sample_kernel
    if state_out is not None:
        args.append(state_out)
        in_specs.append(pl.BlockSpec(memory_space=pl.ANY))
        aliases = {len(args) - 1: 1}
        kern = functools.partial(_drop_alias_arg, kern, 8)
    return pl.pallas_call(
        kern,
        out_shape=(jax.ShapeDtypeStruct((DEC_BATCH, BRANCH_W), F32),
                   jax.ShapeDtypeStruct(state.shape, F32)),
        grid=(RET_HEADS, DEC_BATCH // nb),
        in_specs=in_specs,
        out_specs=(pl.BlockSpec((nb, RET_DK), lambda h, i: (i, h)), st_spec),
        scratch_shapes=[pltpu.VMEM((nb, RET_DK), F32)],
        input_output_aliases=aliases,
        compiler_params=_cparams(2),
        name="ret_sample",
    )(*args)


def _reorder_w_in(w):
    pad = jnp.zeros((w.shape[0], N_IN_PAD - N_IN_ORIG), w.dtype)
    w = jnp.concatenate([w[:, :ORIG_SDT], w[:, ORIG_SDT + SSD_HEADS:], w[:, ORIG_SDT:ORIG_SDT + SSD_HEADS], pad],
                        axis=1)
    return w.astype(BF16)


def kernel(x_prompt, x_sample, state_lru_h, state_lru_conv, state_hgrn, state_ssd, state_ssd_conv, state_ret, state_ffn_conv, g_mix, g_ffn, w_in, lru_conv_w, lru_conv_b, lru_wa, lru_ba, lru_wx, lru_bx, lru_lambda, hg_lb_logits, hg_norm_w, ssd_conv_w, ssd_conv_b, ssd_dt_bias, ssd_a_log, ssd_d, ssd_norm_w, w_branch, w_gate, w_out, ffn_w_up, ffn_w_val, ffn_conv_w, ffn_conv_b, ffn_w_down, g_final):
    xp = x_prompt.reshape(BATCH * SEQ, D_MODEL)
    xs = x_sample.reshape(DEC_BATCH, D_MODEL)

    cos_p, sin_p = rope_tables(SEQ, 0, True)
    cos_s, sin_s = rope_tables(SUBLANES, PAST_LEN, False)

    hp = rmsnorm(xp, g_mix[0], BF16, 512)
    hs = rmsnorm(xs, g_mix[0], BF16, DEC_BATCH)

    prompt_states, sample_small = [], []
    hg_out = ssd_out = ret_out = None
    for l in range(DEPTH):
        p = {"lru_conv_w": lru_conv_w[l], "lru_conv_b": lru_conv_b[l], "lru_wa": lru_wa[l], "lru_ba": lru_ba[l],
             "lru_wx": lru_wx[l], "lru_bx": lru_bx[l], "lru_lambda": lru_lambda[l],
             "hg_lb_logits": hg_lb_logits, "hg_norm_w": hg_norm_w[l],
             "ssd_conv_w": ssd_conv_w[l], "ssd_conv_b": ssd_conv_b[l], "ssd_dt_bias": ssd_dt_bias[l],
             "ssd_a_log": ssd_a_log[l], "ssd_d": ssd_d[l], "ssd_norm_w": ssd_norm_w[l]}
        w_in_b = _reorder_w_in(w_in[l])
        w_gate_b = w_gate[l].reshape(D_MODEL, 4 * D_MODEL).astype(BF16)
        w_branch_b = w_branch[l].astype(BF16)
        w_out_b = w_out[l].astype(BF16)
        w_up_b = ffn_w_up[l].astype(BF16)
        w_val_b = ffn_w_val[l].astype(BF16)
        w_down_b = ffn_w_down[l].astype(BF16)
        last = l == DEPTH - 1
        g_next = g_final if last else g_mix[l + 1]

        proj_p = in_proj(hp, w_in_b, 1024)
        out_a, lru_h_p, lru_conv_p = lru_prompt(proj_p, p)
        out_b, hg_p = hgrn_prompt(l, proj_p, p)
        out_c, ssd_p, ssd_conv_p = ssd_prompt(proj_p, p)
        out_d, ret_p = ret_prompt(proj_p, cos_p, sin_p)
        merged = gated_merge(hp, (out_a, out_b, out_c, out_d), w_gate_b, w_branch_b, 1024, 256)
        xp, h2 = out_proj_residual_norm(merged, w_out_b, xp, g_ffn[l], 512)
        act, ffn_conv_p = ffn_prompt(h2, w_up_b, w_val_b, ffn_conv_w[l], ffn_conv_b[l], 256)
        res = down_proj_residual_norm(act, w_down_b, xp, g_next, 512, 512, not last, F32 if last else BF16)
        if last:
            (yp,) = res
        else:
            xp, hp = res
        prompt_states.append((lru_h_p, lru_conv_p, hg_p, ssd_p, ssd_conv_p, ret_p, ffn_conv_p))

        proj_s = in_proj(hs, w_in_b, DEC_BATCH)
        lru_buf_t = jnp.swapaxes(state_lru_conv[l], 0, 1)
        ssd_buf_t = jnp.swapaxes(state_ssd_conv[l], 0, 1)
        ffn_buf_t = jnp.swapaxes(state_ffn_conv[l], 0, 1)
        s_a, lru_h_s, lru_nbuf = lru_sample(proj_s, p, state_lru_h[l], lru_buf_t)
        s_b, hg_out = hgrn_sample(l, proj_s, p, state_hgrn, hg_out)
        s_c, ssd_out, ssd_nbuf = ssd_sample(l, proj_s, p, state_ssd, ssd_out, ssd_buf_t)
        s_d, ret_out = ret_sample(l, proj_s, cos_s, sin_s, state_ret, ret_out)
        merged_s = gated_merge(hs, (s_a, s_b, s_c, s_d), w_gate_b, w_branch_b, DEC_BATCH, 256)
        xs, h2s = out_proj_residual_norm(merged_s, w_out_b, xs, g_ffn[l], DEC_BATCH)
        act_s, ffn_nbuf = ffn_sample(h2s, w_up_b, w_val_b, ffn_conv_w[l], ffn_conv_b[l], ffn_buf_t, 512)
        res = down_proj_residual_norm(act_s, w_down_b, xs, g_next, DEC_BATCH, 512, not last, F32 if last else BF16)
        if last:
            (ys,) = res
        else:
            xs, hs = res
        sample_small.append((lru_h_s, jnp.swapaxes(lru_nbuf, 0, 1), jnp.swapaxes(ssd_nbuf, 0, 1),
                             jnp.swapaxes(ffn_nbuf, 0, 1)))

    stack_p = lambda i: jnp.stack([st[i] for st in prompt_states], axis=0)
    stack_s = lambda i: jnp.stack([st[i] for st in sample_small], axis=0)
    return (yp.reshape(BATCH, SEQ, D_MODEL), ys.reshape(DEC_BATCH, 1, D_MODEL),
            stack_p(0), stack_s(0), stack_p(1), stack_s(1),
            stack_p(2), hg_out, stack_p(3), ssd_out,
            stack_p(4), stack_s(2), stack_p(5), ret_out,
            stack_p(6), stack_s(3))
```

```python
import functools
import math

import numpy as np
import jax
import jax.numpy as jnp
from jax import lax
from jax.experimental import pallas as pl
from jax.experimental.pallas import tpu as pltpu

F32 = jnp.float32
BF16 = jnp.bfloat16
HIGHEST = lax.Precision.HIGHEST

D_MODEL = 2048
BATCH = 4
SEQ = 2048
DEPTH = 2
DEC_BATCH = 128
PAST_LEN = 16384
BRANCH_W = D_MODEL // 2
EPS = 1e-6
LRU_BLOCKS = 8
LRU_BLOCK = BRANCH_W // LRU_BLOCKS
LRU_C = 8.0
CONV_W = 4
HG_HEADS = 8
HG_DK = BRANCH_W // HG_HEADS
SSD_HEADDIM = 64
SSD_HEADS = BRANCH_W // SSD_HEADDIM
SSD_GROUPS = 2
SSD_STATE = 128
SSD_BC = 2 * SSD_GROUPS * SSD_STATE
SSD_CONV_DIM = BRANCH_W + SSD_BC
RET_HEADS = 8
RET_DK = BRANCH_W // RET_HEADS
ROPE_BASE = 10000.0
D_FF = 5632
FFN_CONV_W = 3

V7X_VMEM_BYTES = 64 * 1024 * 1024
VMEM_LIMIT_BYTES = V7X_VMEM_BYTES - 8 * 1024 * 1024
LANES = 128
SUBLANES = 8

N_IN_ORIG = 12816
OFF_XA, OFF_YA = 0, 1024
OFF_HQ, OFF_HF, OFF_HI, OFF_HG = 2048, 3072, 4096, 5120
OFF_SZ, OFF_SX, OFF_SBC = 6144, 7168, 8192
OFF_RQ, OFF_RK, OFF_RV, OFF_RG = 8704, 9728, 10752, 11776
OFF_SDT = 12800
N_IN_PAD = 13056
ORIG_SDT = 8704
MM_TN = 768

HG_CHUNK = 128
HG_LEVELS = (8, 16, 32, 64)
SSD_CHUNK = 128
RET_CHUNK = 256
SAMPLE_BLK = 32
SSD_SAMPLE_BLK = 8


def _cparams(n_axes):
    return pltpu.CompilerParams(dimension_semantics=("arbitrary",) * n_axes,
                                vmem_limit_bytes=VMEM_LIMIT_BYTES)


def _rms_rows(x):
    return x * lax.rsqrt(jnp.mean(x * x, axis=-1, keepdims=True) + EPS)


def _shift_rows(x, d, row):
    return jnp.where(row >= d, pltpu.roll(x, d, axis=0), 0.0)


def _nt_dot(a, b):
    return lax.dot_general(a, b, (((1,), (1,)), ((), ())), preferred_element_type=F32)


def _tn_dot(a, b):
    return lax.dot_general(a, b, (((0,), (0,)), ((), ())), preferred_element_type=F32)


def _rows_to_cols(x):
    n = x.shape[0]
    if n < LANES:
        x = jnp.concatenate([x, jnp.zeros((LANES - n, x.shape[1]), x.dtype)], axis=0)
    return x.T


def _norm_kernel(x_ref, g_ref, o_ref):
    o_ref[...] = (_rms_rows(x_ref[...]) * g_ref[...]).astype(o_ref.dtype)


def rmsnorm(x, g, out_dtype, tm):
    m, d = x.shape
    return pl.pallas_call(
        _norm_kernel,
        out_shape=jax.ShapeDtypeStruct((m, d), out_dtype),
        grid=(m // tm,),
        in_specs=[pl.BlockSpec((tm, d), lambda i: (i, 0)),
                  pl.BlockSpec((1, d), lambda i: (0, 0))],
        out_specs=pl.BlockSpec((tm, d), lambda i: (i, 0)),
        compiler_params=_cparams(1),
        name="rmsnorm",
    )(x, g.reshape(1, d))


def _mm_kernel(a_ref, b_ref, o_ref):
    o_ref[...] = jnp.dot(a_ref[...], b_ref[...], preferred_element_type=F32)


def in_proj(layer, h, w, tm):
    m, k = h.shape
    n = w.shape[2]
    return pl.pallas_call(
        _mm_kernel,
        out_shape=jax.ShapeDtypeStruct((m, n), F32),
        grid=(m // tm, n // MM_TN),
        in_specs=[pl.BlockSpec((tm, k), lambda i, j: (i, 0)),
                  pl.BlockSpec((None, k, MM_TN), lambda i, j: (layer, 0, j))],
        out_specs=pl.BlockSpec((tm, MM_TN), lambda i, j: (i, j)),
        compiler_params=_cparams(2),
        name="in_proj",
    )(h, w)


def _merge_kernel(h_ref, a_ref, b_ref, c_ref, d_ref, g0, g1, g2, g3, w0, w1, w2, w3, o_ref):
    h = h_ref[...]
    acc = None
    for br_ref, g_ref, w_ref in ((a_ref, g0, w0), (b_ref, g1, w1), (c_ref, g2, w2), (d_ref, g3, w3)):
        gate = jax.nn.sigmoid(jnp.dot(h, g_ref[...], preferred_element_type=F32))
        br = jnp.dot(br_ref[...].astype(BF16), w_ref[...], preferred_element_type=F32)
        acc = gate * br if acc is None else acc + gate * br
    o_ref[...] = acc.astype(o_ref.dtype)


def gated_merge(layer, h, branches, w_gate, w_branch, tm, tn):
    m = h.shape[0]
    nj = D_MODEL // tn
    br_specs = [pl.BlockSpec((tm, BRANCH_W), lambda i, j: (i, 0)) for _ in range(4)]
    g_specs = [pl.BlockSpec((None, D_MODEL, tn), functools.partial(lambda i, j, k: (layer, 0, k * nj + j), k=k))
               for k in range(4)]
    w_specs = [pl.BlockSpec((None, None, BRANCH_W, tn), functools.partial(lambda i, j, k: (layer, k, 0, j), k=k))
               for k in range(4)]
    return pl.pallas_call(
        _merge_kernel,
        out_shape=jax.ShapeDtypeStruct((m, D_MODEL), BF16),
        grid=(m // tm, nj),
        in_specs=[pl.BlockSpec((tm, D_MODEL), lambda i, j: (i, 0))] + br_specs + g_specs + w_specs,
        out_specs=pl.BlockSpec((tm, tn), lambda i, j: (i, j)),
        compiler_params=_cparams(2),
        name="gated_merge",
    )(h, *branches, w_gate, w_gate, w_gate, w_gate, w_branch, w_branch, w_branch, w_branch)


def _out_proj_kernel(m_ref, w_ref, x_ref, g_ref, xo_ref, ho_ref):
    x_new = x_ref[...] + jnp.dot(m_ref[...], w_ref[...], preferred_element_type=F32)
    xo_ref[...] = x_new
    ho_ref[...] = (_rms_rows(x_new) * g_ref[...]).astype(ho_ref.dtype)


def out_proj_residual_norm(layer, merged, w_out, x, g, tm):
    m = x.shape[0]
    return pl.pallas_call(
        _out_proj_kernel,
        out_shape=(jax.ShapeDtypeStruct((m, D_MODEL), F32), jax.ShapeDtypeStruct((m, D_MODEL), BF16)),
        grid=(m // tm,),
        in_specs=[pl.BlockSpec((tm, D_MODEL), lambda i: (i, 0)),
                  pl.BlockSpec((None, D_MODEL, D_MODEL), lambda i: (layer, 0, 0)),
                  pl.BlockSpec((tm, D_MODEL), lambda i: (i, 0)),
                  pl.BlockSpec((1, D_MODEL), lambda i: (0, 0))],
        out_specs=(pl.BlockSpec((tm, D_MODEL), lambda i: (i, 0)),
                   pl.BlockSpec((tm, D_MODEL), lambda i: (i, 0))),
        compiler_params=_cparams(1),
        name="out_proj",
    )(merged, w_out, x, g.reshape(1, D_MODEL))


def _down_proj_kernel(emit_x, a_ref, w_ref, x_ref, g_ref, *refs):
    if emit_x:
        xo_ref, no_ref, acc_ref = refs
    else:
        no_ref, acc_ref = refs
    kk = pl.program_id(1)

    @pl.when(kk == 0)
    def _():
        acc_ref[...] = x_ref[...]

    acc_ref[...] += jnp.dot(a_ref[...], w_ref[...], preferred_element_type=F32)

    @pl.when(kk == pl.num_programs(1) - 1)
    def _():
        x_new = acc_ref[...]
        if emit_x:
            xo_ref[...] = x_new
        no_ref[...] = (_rms_rows(x_new) * g_ref[...]).astype(no_ref.dtype)


def down_proj_residual_norm(layer, a, w_down, x, g, tm, tk, emit_x, norm_dtype):
    m = x.shape[0]
    out_shape = [jax.ShapeDtypeStruct((m, D_MODEL), norm_dtype)]
    out_specs = [pl.BlockSpec((tm, D_MODEL), lambda i, k: (i, 0))]
    if emit_x:
        out_shape = [jax.ShapeDtypeStruct((m, D_MODEL), F32)] + out_shape
        out_specs = [pl.BlockSpec((tm, D_MODEL), lambda i, k: (i, 0))] + out_specs
    return pl.pallas_call(
        functools.partial(_down_proj_kernel, emit_x),
        out_shape=tuple(out_shape),
        grid=(m // tm, D_FF // tk),
        in_specs=[pl.BlockSpec((tm, tk), lambda i, k: (i, k)),
                  pl.BlockSpec((None, tk, D_MODEL), lambda i, k: (layer, k, 0)),
                  pl.BlockSpec((tm, D_MODEL), lambda i, k: (i, 0)),
                  pl.BlockSpec((1, D_MODEL), lambda i, k: (0, 0))],
        out_specs=tuple(out_specs),
        scratch_shapes=[pltpu.VMEM((tm, D_MODEL), F32)],
        compiler_params=_cparams(2),
        name="down_proj",
    )(a, w_down, x, g.reshape(1, D_MODEL))


def _ffn_prompt_kernel(h_ref, wu_ref, wv_ref, cw_ref, cb_ref, a_ref, st_ref):
    h = h_ref[...]
    u = jnp.dot(h, wu_ref[...], preferred_element_type=F32)
    v = jnp.dot(h, wv_ref[...], preferred_element_type=F32)
    row = lax.broadcasted_iota(jnp.int32, u.shape, 0)
    cw = cw_ref[...]
    uc = cb_ref[...] + cw[2:3] * u + cw[1:2] * _shift_rows(u, 1, row) + cw[0:1] * _shift_rows(u, 2, row)
    a_ref[...] = (jax.nn.gelu(uc) * v).astype(a_ref.dtype)
    t = u.shape[0]
    st_ref[...] = u[t - (FFN_CONV_W - 1):, :]


def ffn_prompt(layer, h2, w_up, w_val, conv_w, conv_b, tn):
    return pl.pallas_call(
        _ffn_prompt_kernel,
        out_shape=(jax.ShapeDtypeStruct((BATCH * SEQ, D_FF), BF16),
                   jax.ShapeDtypeStruct((BATCH, FFN_CONV_W - 1, D_FF), F32)),
        grid=(BATCH, D_FF // tn),
        in_specs=[pl.BlockSpec((SEQ, D_MODEL), lambda b, j: (b, 0)),
                  pl.BlockSpec((None, D_MODEL, tn), lambda b, j: (layer, 0, j)),
                  pl.BlockSpec((None, D_MODEL, tn), lambda b, j: (layer, 0, j)),
                  pl.BlockSpec((FFN_CONV_W, tn), lambda b, j: (0, j)),
                  pl.BlockSpec((1, tn), lambda b, j: (0, j))],
        out_specs=(pl.BlockSpec((SEQ, tn), lambda b, j: (b, j)),
                   pl.BlockSpec((None, FFN_CONV_W - 1, tn), lambda b, j: (b, 0, j))),
        compiler_params=_cparams(2),
        name="ffn_prompt",
    )(h2, w_up, w_val, conv_w, conv_b.reshape(1, D_FF))


def _ffn_sample_kernel(h_ref, wu_ref, wv_ref, cw_ref, cb_ref, buf_ref, a_ref, nb_ref):
    h = h_ref[...]
    u = jnp.dot(h, wu_ref[...], preferred_element_type=F32)
    v = jnp.dot(h, wv_ref[...], preferred_element_type=F32)
    cw = cw_ref[...]
    b0 = buf_ref[:, 0, :]
    b1 = buf_ref[:, 1, :]
    uc = cb_ref[...] + cw[0:1] * b0 + cw[1:2] * b1 + cw[2:3] * u
    a_ref[...] = (jax.nn.gelu(uc) * v).astype(a_ref.dtype)
    nb_ref[:, 0, :] = b1
    nb_ref[:, 1, :] = u


def ffn_sample(layer, h2, w_up, w_val, conv_w, conv_b, buf, tn):
    return pl.pallas_call(
        _ffn_sample_kernel,
        out_shape=(jax.ShapeDtypeStruct((DEC_BATCH, D_FF), BF16),
                   jax.ShapeDtypeStruct((DEC_BATCH, FFN_CONV_W - 1, D_FF), F32)),
        grid=(D_FF // tn,),
        in_specs=[pl.BlockSpec((DEC_BATCH, D_MODEL), lambda j: (0, 0)),
                  pl.BlockSpec((None, D_MODEL, tn), lambda j: (layer, 0, j)),
                  pl.BlockSpec((None, D_MODEL, tn), lambda j: (layer, 0, j)),
                  pl.BlockSpec((FFN_CONV_W, tn), lambda j: (0, j)),
                  pl.BlockSpec((1, tn), lambda j: (0, j)),
                  pl.BlockSpec((None, DEC_BATCH, FFN_CONV_W - 1, tn), lambda j: (layer, 0, 0, j))],
        out_specs=(pl.BlockSpec((DEC_BATCH, tn), lambda j: (0, j)),
                   pl.BlockSpec((DEC_BATCH, FFN_CONV_W - 1, tn), lambda j: (0, 0, j))),
        compiler_params=_cparams(1),
        name="ffn_sample",
    )(h2, w_up, w_val, conv_w, conv_b.reshape(1, D_FF), buf)


def _rope_table_kernel(start, consecutive, freq_ref, sign_ref, cos_ref, sin_ref):
    shape = cos_ref.shape
    if consecutive:
        pos = lax.broadcasted_iota(jnp.int32, shape, 0).astype(F32) + float(start)
    else:
        pos = jnp.full(shape, float(start), F32)
    ang = pos * freq_ref[...]
    cos_ref[...] = jnp.cos(ang)
    sin_ref[...] = sign_ref[...] * jnp.sin(ang)


def rope_tables(n_rows, start, consecutive):
    half = RET_DK // 2
    freqs = ROPE_BASE ** (-jnp.arange(half, dtype=F32) / half)
    freq2 = jnp.concatenate([freqs, freqs]).reshape(1, RET_DK)
    sign = jnp.concatenate([-jnp.ones((half,), F32), jnp.ones((half,), F32)]).reshape(1, RET_DK)
    return pl.pallas_call(
        functools.partial(_rope_table_kernel, start, consecutive),
        out_shape=(jax.ShapeDtypeStruct((n_rows, RET_DK), F32), jax.ShapeDtypeStruct((n_rows, RET_DK), F32)),
        name="rope_tables",
    )(freq2, sign)


def _rope(x, cos, sin_signed):
    return x * cos + pltpu.roll(x, RET_DK // 2, axis=1) * sin_signed


def _lru_gates(conv, wa_ref, ba_ref, wx_ref, bx_ref, lam_ref):
    xb = conv.astype(BF16)
    r = jax.nn.sigmoid(jnp.dot(xb, wa_ref[...].astype(BF16), preferred_element_type=F32) + ba_ref[...])
    i = jax.nn.sigmoid(jnp.dot(xb, wx_ref[...].astype(BF16), preferred_element_type=F32) + bx_ref[...])
    log_a = -LRU_C * r * jax.nn.softplus(-lam_ref[...])
    a = jnp.exp(log_a)
    u = jnp.sqrt(1.0 - a * a) * (i * conv)
    return a, u


def _lru_prompt_kernel(xa_ref, ya_ref, cw_ref, cb_ref, wa_ref, ba_ref, wx_ref, bx_ref, lam_ref,
                       out_ref, h_ref, conv_ref, ag_s, ug_s):
    x = xa_ref[...]
    t_len = x.shape[0]
    row = lax.broadcasted_iota(jnp.int32, x.shape, 0)
    cw = cw_ref[...]
    conv = cb_ref[...] + cw[CONV_W - 1:CONV_W] * x
    for d in range(1, CONV_W):
        conv = conv + cw[CONV_W - 1 - d:CONV_W - d] * _shift_rows(x, d, row)
    a, u = _lru_gates(conv, wa_ref, ba_ref, wx_ref, bx_ref, lam_ref)
    ng = t_len // SUBLANES
    a3 = a.reshape(ng, SUBLANES, LRU_BLOCK)
    u3 = u.reshape(ng, SUBLANES, LRU_BLOCK)
    sub = lax.broadcasted_iota(jnp.int32, a3.shape, 1)
    d = 1
    while d < SUBLANES:
        keep = sub >= d
        u3 = jnp.where(keep, a3 * pltpu.roll(u3, d, axis=1) + u3, u3)
        a3 = jnp.where(keep, a3 * pltpu.roll(a3, d, axis=1), a3)
        d *= 2
    ag_s[...] = a3.reshape(t_len, LRU_BLOCK)
    ug_s[...] = u3.reshape(t_len, LRU_BLOCK)
    ag = ag_s[pl.ds(SUBLANES - 1, ng, stride=SUBLANES), :]
    ug = ug_s[pl.ds(SUBLANES - 1, ng, stride=SUBLANES), :]
    grow = lax.broadcasted_iota(jnp.int32, ag.shape, 0)
    d = 1
    while d < ng:
        keep = grow >= d
        ug = jnp.where(keep, ag * pltpu.roll(ug, d, axis=0) + ug, ug)
        ag = jnp.where(keep, ag * pltpu.roll(ag, d, axis=0), ag)
        d *= 2
    carry = _shift_rows(ug, 1, grow)
    h3 = a3 * jnp.broadcast_to(carry[:, None, :], a3.shape) + u3
    hs = h3.reshape(t_len, LRU_BLOCK)
    out_ref[...] = (hs * jax.nn.gelu(ya_ref[...])).astype(out_ref.dtype)
    h_ref[...] = ug[ng - 1:, :]
    conv_ref[...] = xa_ref[pl.ds(t_len - (CONV_W - 1), CONV_W - 1), :]


def _lru_param_specs(n_axes_fn):
    blk3 = lambda shape: pl.BlockSpec(shape, n_axes_fn(lambda n: (n, 0, 0)))
    return [pl.BlockSpec((CONV_W, LRU_BLOCK), n_axes_fn(lambda n: (0, n))),
            pl.BlockSpec((1, LRU_BLOCK), n_axes_fn(lambda n: (0, n))),
            blk3((None, LRU_BLOCK, LRU_BLOCK)), blk3((None, 1, LRU_BLOCK)),
            blk3((None, LRU_BLOCK, LRU_BLOCK)), blk3((None, 1, LRU_BLOCK)),
            blk3((None, 1, LRU_BLOCK))]


def _lru_params(p):
    return (p["lru_conv_w"], p["lru_conv_b"].reshape(1, BRANCH_W),
            p["lru_wa"], p["lru_ba"].reshape(LRU_BLOCKS, 1, LRU_BLOCK),
            p["lru_wx"], p["lru_bx"].reshape(LRU_BLOCKS, 1, LRU_BLOCK),
            p["lru_lambda"].reshape(LRU_BLOCKS, 1, LRU_BLOCK))


def lru_prompt(proj, p):
    cb = lambda off: off // LRU_BLOCK
    wrap = lambda f: (lambda b, n: f(n))
    out, h, conv = pl.pallas_call(
        _lru_prompt_kernel,
        out_shape=(jax.ShapeDtypeStruct((BATCH * SEQ, BRANCH_W), BF16),
                   jax.ShapeDtypeStruct((BATCH, 1, BRANCH_W), F32),
                   jax.ShapeDtypeStruct((BATCH, CONV_W - 1, BRANCH_W), F32)),
        grid=(BATCH, LRU_BLOCKS),
        in_specs=[pl.BlockSpec((SEQ, LRU_BLOCK), lambda b, n: (b, cb(OFF_XA) + n)),
                  pl.BlockSpec((SEQ, LRU_BLOCK), lambda b, n: (b, cb(OFF_YA) + n))] + _lru_param_specs(wrap),
        out_specs=(pl.BlockSpec((SEQ, LRU_BLOCK), lambda b, n: (b, n)),
                   pl.BlockSpec((None, 1, LRU_BLOCK), lambda b, n: (b, 0, n)),
                   pl.BlockSpec((None, CONV_W - 1, LRU_BLOCK), lambda b, n: (b, 0, n))),
        scratch_shapes=[pltpu.VMEM((SEQ, LRU_BLOCK), F32)] * 2,
        compiler_params=_cparams(2),
        name="lru_prompt",
    )(proj, proj, *_lru_params(p))
    return out, h.reshape(BATCH, BRANCH_W), conv


def _lru_sample_kernel(xa_ref, ya_ref, cw_ref, cb_ref, wa_ref, ba_ref, wx_ref, bx_ref, lam_ref,
                       h0_ref, buf_ref, out_ref, h_ref, nbuf_ref):
    x = xa_ref[...]
    cw = cw_ref[...]
    conv = cb_ref[...] + cw[CONV_W - 1:CONV_W] * x
    for j in range(CONV_W - 1):
        conv = conv + cw[j:j + 1] * buf_ref[j]
    a, u = _lru_gates(conv, wa_ref, ba_ref, wx_ref, bx_ref, lam_ref)
    h = a * h0_ref[...] + u
    out_ref[...] = h * jax.nn.gelu(ya_ref[...])
    h_ref[...] = h
    for j in range(CONV_W - 2):
        nbuf_ref[j] = buf_ref[j + 1]
    nbuf_ref[CONV_W - 2] = x


def lru_sample(proj, p, h0, buf_t):
    cb = lambda off: off // LRU_BLOCK
    wrap = lambda f: f
    return pl.pallas_call(
        _lru_sample_kernel,
        out_shape=(jax.ShapeDtypeStruct((DEC_BATCH, BRANCH_W), F32),
                   jax.ShapeDtypeStruct((DEC_BATCH, BRANCH_W), F32),
                   jax.ShapeDtypeStruct((CONV_W - 1, DEC_BATCH, BRANCH_W), F32)),
        grid=(LRU_BLOCKS,),
        in_specs=[pl.BlockSpec((DEC_BATCH, LRU_BLOCK), lambda n: (0, cb(OFF_XA) + n)),
                  pl.BlockSpec((DEC_BATCH, LRU_BLOCK), lambda n: (0, cb(OFF_YA) + n))]
                 + _lru_param_specs(wrap)
                 + [pl.BlockSpec((DEC_BATCH, LRU_BLOCK), lambda n: (0, n)),
                    pl.BlockSpec((CONV_W - 1, DEC_BATCH, LRU_BLOCK), lambda n: (0, 0, n))],
        out_specs=(pl.BlockSpec((DEC_BATCH, LRU_BLOCK), lambda n: (0, n)),
                   pl.BlockSpec((DEC_BATCH, LRU_BLOCK), lambda n: (0, n)),
                   pl.BlockSpec((CONV_W - 1, DEC_BATCH, LRU_BLOCK), lambda n: (0, 0, n))),
        compiler_params=_cparams(1),
        name="lru_sample",
    )(proj, proj, *_lru_params(p), h0, buf_t)


def _hgrn_lower_bound(layer, logits):
    mx = jnp.max(logits, axis=0, keepdims=True)
    e = jnp.exp(logits - mx)
    ls = e / jnp.sum(e, axis=0, keepdims=True)
    lb = jnp.zeros_like(ls[0:1])
    for i in range(1, layer + 1):
        lb = lb + ls[i:i + 1]
    return lb


def _hgrn_gates(layer, hq, hf, lbl_ref):
    lb = _hgrn_lower_bound(layer, lbl_ref[...])
    q = jax.nn.silu(hq)
    sg = jax.nn.sigmoid(hf)
    f = lb + (1.0 - lb) * sg
    k = (1.0 - lb) * (1.0 - sg)
    return q, f, k


def _hgrn_level_ids():
    c = HG_CHUNK
    t = np.arange(c)[:, None]
    s = np.arange(c)[None, :]
    level = np.zeros((c, c), np.int32)
    for li, m in enumerate(HG_LEVELS):
        same = (t // (2 * m)) == (s // (2 * m))
        level[same & ((t % (2 * m)) >= m) & ((s % (2 * m)) < m)] = li + 1
    return level


def _split3_bf16(x):
    hi = x.astype(BF16)
    r1 = x - hi.astype(F32)
    mid = r1.astype(BF16)
    lo = (r1 - mid.astype(F32)).astype(BF16)
    return hi, mid, lo


def _cumsum_rows(tril_b, x):
    return sum(jnp.dot(tril_b, piece, preferred_element_type=F32) for piece in _split3_bf16(x))


def _hgrn_prompt_kernel(layer, q_ref, f_ref, i_ref, g_ref, lbl_ref, nw_ref, lvl_ref,
                        out_ref, st_ref, lf_s, q_s, k_s, od_s, o_s):
    c = HG_CHUNK
    t_len = q_ref.shape[0]
    ng = t_len // SUBLANES
    q, f, k = _hgrn_gates(layer, q_ref[...], f_ref[...], lbl_ref)
    logf = jnp.log(f)
    lf_s[...] = logf
    q_s[...] = q
    k_s[...] = k

    g3 = lambda x: x.reshape(ng, SUBLANES, LANES)
    q3, k3, lf3, v3 = g3(q), g3(k), g3(logf), g3(i_ref[...])
    sub = lax.broadcasted_iota(jnp.int32, q3.shape, 1)
    ones = jnp.ones((LANES, LANES), BF16)
    dd = None
    od = None
    for d in range(SUBLANES):
        if d == 0:
            p = q3 * k3
            vd = v3
        else:
            step = lf3 if d == 1 else pltpu.roll(lf3, d - 1, axis=1)
            dd = step if dd is None else dd + step
            p = jnp.where(sub >= d, q3 * jnp.exp(dd) * pltpu.roll(k3, d, axis=1), 0.0)
            vd = pltpu.roll(v3, d, axis=1)
        r = jnp.dot(p.reshape(t_len, LANES).astype(BF16), ones, preferred_element_type=F32)
        od = g3(r) * vd if od is None else od + g3(r) * vd
    od_s[...] = od.reshape(t_len, LANES)

    lvl = lvl_ref[...]
    tril_b = (lax.broadcasted_iota(jnp.int32, (c, c), 0) >= lax.broadcasted_iota(jnp.int32, (c, c), 1)).astype(BF16)

    st = jnp.zeros((HG_DK, HG_DK), F32)
    for ci in range(t_len // c):
        sl = slice(ci * c, (ci + 1) * c)
        qc = q_s[sl, :]
        kc = k_s[sl, :]
        vb = i_ref[sl, :].astype(BF16)
        b = _cumsum_rows(tril_b, lf_s[sl, :])
        att = jnp.zeros((c, c), F32)
        for li, m in enumerate(HG_LEVELS):
            b3 = b.reshape(c // (2 * m), 2 * m, LANES)
            mid = b3[:, m - 1:m, :]
            upper = lax.broadcasted_iota(jnp.int32, b3.shape, 1) >= m
            e = jnp.exp(jnp.where(upper, b3 - mid, mid - b3)).reshape(c, LANES)
            a_l = _nt_dot((qc * e).astype(BF16), (kc * e).astype(BF16))
            att = jnp.where(lvl == li + 1, a_l, att)
        o = (jnp.dot(att.astype(BF16), vb, preferred_element_type=F32) + od_s[sl, :]
             + _nt_dot((qc * jnp.exp(b)).astype(BF16), st.astype(BF16)))
        o_s[sl, :] = o
        bl = b[c - 1:c, :]
        kdec = (kc * jnp.exp(bl - b)).astype(BF16)
        st = st * jnp.exp(bl) + _tn_dot(vb, kdec)
    o = o_s[...]
    out_ref[...] = (_rms_rows(o) * nw_ref[...] * jax.nn.silu(g_ref[...])).astype(out_ref.dtype)
    st_ref[...] = st.T


def hgrn_prompt(layer, proj, p):
    cb = lambda off: off // HG_DK
    level = _hgrn_level_ids()
    col = lambda off: pl.BlockSpec((SEQ, HG_DK), lambda b, h: (b, cb(off) + h))
    return pl.pallas_call(
        functools.partial(_hgrn_prompt_kernel, layer),
        out_shape=(jax.ShapeDtypeStruct((BATCH * SEQ, BRANCH_W), BF16),
                   jax.ShapeDtypeStruct((BATCH, HG_HEADS, HG_DK, HG_DK), F32)),
        grid=(BATCH, HG_HEADS),
        in_specs=[col(OFF_HQ), col(OFF_HF), col(OFF_HI), col(OFF_HG),
                  pl.BlockSpec((DEPTH, HG_DK), lambda b, h: (0, h)),
                  pl.BlockSpec((1, HG_DK), lambda b, h: (0, 0)),
                  pl.BlockSpec(level.shape, lambda b, h: (0, 0))],
        out_specs=(pl.BlockSpec((SEQ, HG_DK), lambda b, h: (b, h)),
                   pl.BlockSpec((None, None, HG_DK, HG_DK), lambda b, h: (b, h, 0, 0))),
        scratch_shapes=[pltpu.VMEM((SEQ, HG_DK), F32)] * 5,
        compiler_params=_cparams(2),
        name="hgrn_prompt",
    )(proj, proj, proj, proj, p["hg_lb_logits"], p["hg_norm_w"].reshape(1, HG_DK),
      jnp.asarray(level))


def _hgrn_sample_kernel(layer, q_ref, f_ref, i_ref, g_ref, lbl_ref, nw_ref, s_ref, out_ref, so_ref, o_s):
    q, f, k = _hgrn_gates(layer, q_ref[...], f_ref[...], lbl_ref)
    qc, fc, kc = _rows_to_cols(q), _rows_to_cols(f), _rows_to_cols(k)
    nb = q.shape[0]
    for j in range(nb):
        s_new = s_ref[j] * fc[:, j:j + 1] + kc[:, j:j + 1] * i_ref[j:j + 1, :]
        so_ref[j] = s_new
        o_s[j:j + 1, :] = jnp.sum(s_new * qc[:, j:j + 1], axis=0, keepdims=True)
    out_ref[...] = _rms_rows(o_s[...]) * nw_ref[...] * jax.nn.silu(g_ref[...])


def hgrn_sample(layer, proj, p, state, state_out):
    cb = lambda off: off // HG_DK
    nb = SAMPLE_BLK
    col = lambda off: pl.BlockSpec((nb, HG_DK), lambda h, i: (i, cb(off) + h))
    st_spec = pl.BlockSpec((None, nb, None, HG_DK, HG_DK), lambda h, i: (layer, i, h, 0, 0))
    args = [proj, proj, proj, proj, p["hg_lb_logits"], p["hg_norm_w"].reshape(1, HG_DK), state]
    in_specs = [col(OFF_HQ), col(OFF_HF), col(OFF_HI), col(OFF_HG),
                pl.BlockSpec((DEPTH, HG_DK), lambda h, i: (0, h)),
                pl.BlockSpec((1, HG_DK), lambda h, i: (0, 0)),
                st_spec]
    aliases = {}
    kern = functools.partial(_hgrn_sample_kernel, layer)
    if state_out is not None:
        args.append(state_out)
        in_specs.append(pl.BlockSpec(memory_space=pl.ANY))
        aliases = {len(args) - 1: 1}
        kern = functools.partial(_drop_alias_arg, kern, 7)
    return pl.pallas_call(
        kern,
        out_shape=(jax.ShapeDtypeStruct((DEC_BATCH, BRANCH_W), F32),
                   jax.ShapeDtypeStruct(state.shape, F32)),
        grid=(HG_HEADS, DEC_BATCH // nb),
        in_specs=in_specs,
        out_specs=(pl.BlockSpec((nb, HG_DK), lambda h, i: (i, h)), st_spec),
        scratch_shapes=[pltpu.VMEM((nb, HG_DK), F32)],
        input_output_aliases=aliases,
        compiler_params=_cparams(2),
        name="hgrn_sample",
    )(*args)


def _drop_alias_arg(kern, pos, *refs):
    return kern(*refs[:pos], *refs[pos + 1:])


def _head_pair(cols, h0, lane_lo):
    return jnp.where(lane_lo, cols[:, h0:h0 + 1], cols[:, h0 + 1:h0 + 2])


def _ssd_prompt_kernel(z_ref, x_ref, bc_ref, dt_ref, cw_ref, cb_ref, dtb_ref, alog_ref, dpar_ref, nw_ref,
                       tril_ref, out_ref, st_ref, cst_ref, cx_s, cbc_s, s_s):
    c = SSD_CHUNK
    ci = pl.program_id(1)

    @pl.when(ci == 0)
    def _():
        cx_s[...] = jnp.zeros_like(cx_s)
        cbc_s[...] = jnp.zeros_like(cbc_s)
        s_s[...] = jnp.zeros_like(s_s)

    cw = cw_ref[...]
    cbias = cb_ref[...]

    def conv_silu(raw, carry_ref, lo, hi):
        xx = jnp.concatenate([carry_ref[...], raw], axis=0)
        y = cbias[:, lo:hi] + cw[CONV_W - 1:CONV_W, lo:hi] * raw
        for d in range(1, CONV_W):
            y = y + cw[CONV_W - 1 - d:CONV_W - d, lo:hi] * pltpu.roll(xx, d, axis=0)[SUBLANES:]
        carry_ref[...] = raw[c - SUBLANES:, :]
        return jax.nn.silu(y)

    x_raw = x_ref[...]
    bc_raw = bc_ref[...]
    xs = conv_silu(x_raw, cx_s, 0, BRANCH_W)
    bc = conv_silu(bc_raw, cbc_s, BRANCH_W, SSD_CONV_DIM)

    dt = jax.nn.softplus(dt_ref[...] + dtb_ref[...])
    a_neg = -jnp.exp(alog_ref[...])
    logd = dt * a_neg
    b = jnp.dot(tril_ref[...], logd, precision=HIGHEST, preferred_element_type=F32)
    b_t = b.T
    bl = b[c - 1:c, :]
    e_in = jnp.exp(b)
    w_out = jnp.exp(bl - b)
    e_last = jnp.exp(bl)
    dfull = dpar_ref[...]

    tri = lax.broadcasted_iota(jnp.int32, (c, c), 0) >= lax.broadcasted_iota(jnp.int32, (c, c), 1)
    lane_lo = lax.broadcasted_iota(jnp.int32, (c, LANES), 1) < SSD_HEADDIM
    lane_lo_row = lax.broadcasted_iota(jnp.int32, (1, LANES), 1) < SSD_HEADDIM

    ys = []
    for g in range(SSD_GROUPS):
        bm = bc[:, g * SSD_STATE:(g + 1) * SSD_STATE].astype(BF16)
        cm = bc[:, (SSD_GROUPS + g) * SSD_STATE:(SSD_GROUPS + g + 1) * SSD_STATE].astype(BF16)
        gmat = _nt_dot(cm, bm)
        for pp in range(SSD_HEADS // SSD_GROUPS // 2):
            pi = g * (SSD_HEADS // SSD_GROUPS // 2) + pp
            h0 = 2 * pi
            xs_p = xs[:, pi * LANES:(pi + 1) * LANES]
            vdt = xs_p * _head_pair(dt, h0, lane_lo)
            vdt_b = vdt.astype(BF16)
            o_heads = []
            for hh in (h0, h0 + 1):
                diff = b[:, hh:hh + 1] - b_t[hh:hh + 1, :]
                dec = jnp.where(tri, jnp.exp(jnp.where(tri, diff, 0.0)), 0.0)
                o_heads.append(jnp.dot((gmat * dec).astype(BF16), vdt_b, preferred_element_type=F32))
            o_intra = jnp.where(lane_lo, o_heads[0], o_heads[1])
            s_p = s_s[pi]
            o_inter = _head_pair(e_in, h0, lane_lo) * jnp.dot(cm, s_p.astype(BF16), preferred_element_type=F32)
            ys.append(o_intra + o_inter + dfull[:, pi * LANES:(pi + 1) * LANES] * xs_p)
            upd = _tn_dot(bm, (vdt * _head_pair(w_out, h0, lane_lo)).astype(BF16))
            s_s[pi] = s_p * _head_pair(e_last, h0, lane_lo_row) + upd

    y = jnp.concatenate(ys, axis=1) * jax.nn.silu(z_ref[...])
    gw = BRANCH_W // SSD_GROUPS
    nw = nw_ref[...]
    outs = [_rms_rows(y[:, g * gw:(g + 1) * gw]) * nw[:, g * gw:(g + 1) * gw] for g in range(SSD_GROUPS)]
    out_ref[...] = jnp.concatenate(outs, axis=1).astype(out_ref.dtype)

    @pl.when(ci == pl.num_programs(1) - 1)
    def _():
        for pi in range(SSD_HEADS // 2):
            s_t = s_s[pi].T
            st_ref[2 * pi] = s_t[:SSD_HEADDIM, :]
            st_ref[2 * pi + 1] = s_t[SSD_HEADDIM:, :]
        cst_ref[:, 0:BRANCH_W] = x_raw[c - (CONV_W - 1):, :]
        cst_ref[:, BRANCH_W:SSD_CONV_DIM] = bc_raw[c - (CONV_W - 1):, :]


def _pad_lanes(v):
    return jnp.pad(v.astype(F32), (0, LANES - v.shape[0])).reshape(1, LANES)


def _ssd_params(p):
    return (p["ssd_conv_w"], p["ssd_conv_b"].reshape(1, SSD_CONV_DIM), _pad_lanes(p["ssd_dt_bias"]),
            _pad_lanes(p["ssd_a_log"]), jnp.repeat(p["ssd_d"].astype(F32), SSD_HEADDIM).reshape(1, BRANCH_W),
            p["ssd_norm_w"].reshape(1, BRANCH_W))


def ssd_prompt(proj, p):
    c = SSD_CHUNK
    nc = SEQ // c
    tril = jnp.asarray(np.tril(np.ones((c, c), np.float32)))
    const = lambda shape: pl.BlockSpec(shape, lambda b, i: (0, 0))
    rowblk = lambda w, off: pl.BlockSpec((c, w), lambda b, i: (b * nc + i, off // w))
    return pl.pallas_call(
        _ssd_prompt_kernel,
        out_shape=(jax.ShapeDtypeStruct((BATCH * SEQ, BRANCH_W), BF16),
                   jax.ShapeDtypeStruct((BATCH, SSD_HEADS, SSD_HEADDIM, SSD_STATE), F32),
                   jax.ShapeDtypeStruct((BATCH, CONV_W - 1, SSD_CONV_DIM), F32)),
        grid=(BATCH, nc),
        in_specs=[rowblk(BRANCH_W, OFF_SZ), rowblk(BRANCH_W, OFF_SX), rowblk(SSD_BC, OFF_SBC),
                  rowblk(LANES, OFF_SDT),
                  const((CONV_W, SSD_CONV_DIM)), const((1, SSD_CONV_DIM)), const((1, LANES)),
                  const((1, LANES)), const((1, BRANCH_W)), const((1, BRANCH_W)), const((c, c))],
        out_specs=(pl.BlockSpec((c, BRANCH_W), lambda b, i: (b * nc + i, 0)),
                   pl.BlockSpec((None, SSD_HEADS, SSD_HEADDIM, SSD_STATE), lambda b, i: (b, 0, 0, 0)),
                   pl.BlockSpec((None, CONV_W - 1, SSD_CONV_DIM), lambda b, i: (b, 0, 0))),
        scratch_shapes=[pltpu.VMEM((SUBLANES, BRANCH_W), F32), pltpu.VMEM((SUBLANES, SSD_BC), F32),
                        pltpu.VMEM((SSD_HEADS // 2, SSD_STATE, LANES), F32)],
        compiler_params=_cparams(2),
        name="ssd_prompt",
    )(proj, proj, proj, proj, *_ssd_params(p), tril)


def _ssd_sample_kernel(z_ref, x_ref, bc_ref, dt_ref, cw_ref, cb_ref, dtb_ref, alog_ref, dpar_ref, nw_ref,
                       bufx_ref, bufbc_ref, s_ref, out_ref, so_ref, nbx_ref, nbbc_ref, y_s):
    cw = cw_ref[...]
    cbias = cb_ref[...]

    def conv_silu(raw, buf_ref, nbuf_ref, lo, hi):
        y = cbias[:, lo:hi] + cw[CONV_W - 1:CONV_W, lo:hi] * raw
        for j in range(CONV_W - 1):
            y = y + cw[j:j + 1, lo:hi] * buf_ref[j]
        for j in range(CONV_W - 2):
            nbuf_ref[j] = buf_ref[j + 1]
        nbuf_ref[CONV_W - 2] = raw
        return jax.nn.silu(y)

    xs = conv_silu(x_ref[...], bufx_ref, nbx_ref, 0, BRANCH_W)
    bc = conv_silu(bc_ref[...], bufbc_ref, nbbc_ref, BRANCH_W, SSD_CONV_DIM)
    dt = jax.nn.softplus(dt_ref[...] + dtb_ref[...])
    decay = jnp.exp(dt * (-jnp.exp(alog_ref[...])))
    nb = xs.shape[0]
    hpg = SSD_HEADS // SSD_GROUPS
    lane_lo = lax.broadcasted_iota(jnp.int32, (nb, LANES), 1) < SSD_HEADDIM
    for pi in range(SSD_HEADS // 2):
        h0 = 2 * pi
        g = h0 // hpg
        xdt_cols = _rows_to_cols(xs[:, pi * LANES:(pi + 1) * LANES] * _head_pair(dt, h0, lane_lo))
        for e in range(2):
            h = h0 + e
            for j in range(nb):
                xcol = xdt_cols[e * SSD_HEADDIM:(e + 1) * SSD_HEADDIM, j:j + 1]
                brow = bc[j:j + 1, g * SSD_STATE:(g + 1) * SSD_STATE]
                so_ref[j, h] = s_ref[j, h] * decay[j:j + 1, h:h + 1] + xcol * brow
    for g in range(SSD_GROUPS):
        cm = bc[:, (SSD_GROUPS + g) * SSD_STATE:(SSD_GROUPS + g + 1) * SSD_STATE].astype(BF16)
        for j in range(nb):
            s_new = so_ref[j, g * hpg:(g + 1) * hpg].reshape(hpg * SSD_HEADDIM, SSD_STATE)
            y_s[j:j + 1, g * hpg * SSD_HEADDIM:(g + 1) * hpg * SSD_HEADDIM] = _nt_dot(cm, s_new.astype(BF16))[j:j + 1, :]
    y = (y_s[...] + dpar_ref[...] * xs) * jax.nn.silu(z_ref[...])
    gw = BRANCH_W // SSD_GROUPS
    nw = nw_ref[...]
    outs = [_rms_rows(y[:, g * gw:(g + 1) * gw]) * nw[:, g * gw:(g + 1) * gw] for g in range(SSD_GROUPS)]
    out_ref[...] = jnp.concatenate(outs, axis=1)


def ssd_sample(layer, proj, p, state, state_out, buf_t):
    nb = SSD_SAMPLE_BLK
    const = lambda shape: pl.BlockSpec(shape, lambda i: (0,) * len(shape))
    rowblk = lambda w, off: pl.BlockSpec((nb, w), lambda i: (i, off // w))
    st_spec = pl.BlockSpec((None, nb, SSD_HEADS, SSD_HEADDIM, SSD_STATE), lambda i: (layer, i, 0, 0, 0))
    bufx_spec = pl.BlockSpec((CONV_W - 1, nb, BRANCH_W), lambda i: (0, i, 0))
    bufbc_spec = pl.BlockSpec((CONV_W - 1, nb, SSD_BC), lambda i: (0, i, BRANCH_W // SSD_BC))
    args = [proj, proj, proj, proj, *_ssd_params(p), buf_t, buf_t, state]
    in_specs = [rowblk(BRANCH_W, OFF_SZ), rowblk(BRANCH_W, OFF_SX), rowblk(SSD_BC, OFF_SBC), rowblk(LANES, OFF_SDT),
                const((CONV_W, SSD_CONV_DIM)), const((1, SSD_CONV_DIM)), const((1, LANES)), const((1, LANES)),
                const((1, BRANCH_W)), const((1, BRANCH_W)), bufx_spec, bufbc_spec, st_spec]
    aliases = {}
    kern = _ssd_sample_kernel
    if state_out is not None:
        args.append(state_out)
        in_specs.append(pl.BlockSpec(memory_space=pl.ANY))
        aliases = {len(args) - 1: 1}
        kern = functools.partial(_drop_alias_arg, kern, 13)
    out, st, nbx, nbbc = pl.pallas_call(
        kern,
        out_shape=(jax.ShapeDtypeStruct((DEC_BATCH, BRANCH_W), F32),
                   jax.ShapeDtypeStruct(state.shape, F32),
                   jax.ShapeDtypeStruct((CONV_W - 1, DEC_BATCH, BRANCH_W), F32),
                   jax.ShapeDtypeStruct((CONV_W - 1, DEC_BATCH, SSD_BC), F32)),
        grid=(DEC_BATCH // nb,),
        in_specs=in_specs,
        out_specs=(pl.BlockSpec((nb, BRANCH_W), lambda i: (i, 0)), st_spec,
                   pl.BlockSpec((CONV_W - 1, nb, BRANCH_W), lambda i: (0, i, 0)),
                   pl.BlockSpec((CONV_W - 1, nb, SSD_BC), lambda i: (0, i, 0))),
        scratch_shapes=[pltpu.VMEM((nb, BRANCH_W), F32)],
        input_output_aliases=aliases,
        compiler_params=_cparams(1),
        name="ssd_sample",
    )(*args)
    return out, st, jnp.concatenate([nbx, nbbc], axis=-1)


def _ret_prompt_kernel(q_ref, k_ref, v_ref, g_ref, cos_ref, sin_ref, lg_ref, out_ref, st_ref, q_s, k_s, o_s, s_s):
    c = RET_CHUNK
    t_len = q_ref.shape[0]
    cos = cos_ref[...]
    sin = sin_ref[...]
    q_s[...] = _rope(q_ref[...], cos, sin)
    k_s[...] = _rope(k_ref[...], cos, sin) * RET_DK ** -0.5
    lg = lg_ref[...]
    lg128 = lg[:, :LANES]
    ti = lax.broadcasted_iota(jnp.int32, (c, c), 0)
    si = lax.broadcasted_iota(jnp.int32, (c, c), 1)
    tri = ti >= si
    dec = jnp.where(tri, jnp.exp(jnp.where(tri, (ti - si).astype(F32) * lg, 0.0)), 0.0)
    tt = lax.broadcasted_iota(jnp.int32, (c, LANES), 0).astype(F32)
    g_in = jnp.exp((tt + 1.0) * lg128)
    g_out = jnp.exp((c - 1.0 - tt) * lg128)
    g_all = jnp.exp(float(c) * lg128)
    s_s[...] = jnp.zeros_like(s_s)

    def chunk(ci, carry):
        sl = pl.ds(pl.multiple_of(ci * c, c), c)
        qc = q_s[sl, :]
        kc = k_s[sl, :]
        vb = v_ref[sl, :].astype(BF16)
        s = s_s[...]
        scores = _nt_dot(qc.astype(BF16), kc.astype(BF16)) * dec
        o_s[sl, :] = (jnp.dot(scores.astype(BF16), vb, preferred_element_type=F32)
                      + jnp.dot((qc * g_in).astype(BF16), s.astype(BF16), preferred_element_type=F32))
        s_s[...] = s * g_all + _tn_dot((kc * g_out).astype(BF16), vb)
        return carry

    lax.fori_loop(0, t_len // c, chunk, 0)
    out_ref[...] = (_rms_rows(o_s[...]) * jax.nn.silu(g_ref[...])).astype(out_ref.dtype)
    st_ref[...] = s_s[...]


def _log_gamma_rows(width):
    lg = jnp.log1p(-jnp.exp2(-5.0 - jnp.arange(RET_HEADS, dtype=F32)))
    return jnp.broadcast_to(lg[:, None, None], (RET_HEADS, 1, width))


def ret_prompt(proj, cos, sin):
    cb = lambda off: off // RET_DK
    col = lambda off: pl.BlockSpec((SEQ, RET_DK), lambda b, h: (b, cb(off) + h))
    tab = pl.BlockSpec((SEQ, RET_DK), lambda b, h: (0, 0))
    return pl.pallas_call(
        _ret_prompt_kernel,
        out_shape=(jax.ShapeDtypeStruct((BATCH * SEQ, BRANCH_W), BF16),
                   jax.ShapeDtypeStruct((BATCH, RET_HEADS, RET_DK, RET_DK), F32)),
        grid=(BATCH, RET_HEADS),
        in_specs=[col(OFF_RQ), col(OFF_RK), col(OFF_RV), col(OFF_RG), tab, tab,
                  pl.BlockSpec((None, 1, RET_CHUNK), lambda b, h: (h, 0, 0))],
        out_specs=(pl.BlockSpec((SEQ, RET_DK), lambda b, h: (b, h)),
                   pl.BlockSpec((None, None, RET_DK, RET_DK), lambda b, h: (b, h, 0, 0))),
        scratch_shapes=[pltpu.VMEM((SEQ, RET_DK), F32)] * 3 + [pltpu.VMEM((RET_DK, RET_DK), F32)],
        compiler_params=_cparams(2),
        name="ret_prompt",
    )(proj, proj, proj, proj, cos, sin, _log_gamma_rows(RET_CHUNK))


def _ret_sample_kernel(q_ref, k_ref, v_ref, g_ref, cos_ref, sin_ref, lg_ref, s_ref, out_ref, so_ref, o_s):
    cos = cos_ref[0:1, :]
    sin = sin_ref[0:1, :]
    q = _rope(q_ref[...], cos, sin)
    k = _rope(k_ref[...], cos, sin) * RET_DK ** -0.5
    gamma = jnp.exp(lg_ref[...])
    qc, kc = _rows_to_cols(q), _rows_to_cols(k)
    nb = q.shape[0]
    for j in range(nb):
        s_new = s_ref[j] * gamma + kc[:, j:j + 1] * v_ref[j:j + 1, :]
        so_ref[j] = s_new
        o_s[j:j + 1, :] = jnp.sum(s_new * qc[:, j:j + 1], axis=0, keepdims=True)
    out_ref[...] = _rms_rows(o_s[...]) * jax.nn.silu(g_ref[...])


def ret_sample(layer, proj, cos, sin, state, state_out):
    cb = lambda off: off // RET_DK
    nb = SAMPLE_BLK
    col = lambda off: pl.BlockSpec((nb, RET_DK), lambda h, i: (i, cb(off) + h))
    tab = pl.BlockSpec((SUBLANES, RET_DK), lambda h, i: (0, 0))
    st_spec = pl.BlockSpec((None, nb, None, RET_DK, RET_DK), lambda h, i: (layer, i, h, 0, 0))
    args = [proj, proj, proj, proj, cos, sin, _log_gamma_rows(LANES), state]
    in_specs = [col(OFF_RQ), col(OFF_RK), col(OFF_RV), col(OFF_RG), tab, tab,
                pl.BlockSpec((None, 1, LANES), lambda h, i: (h, 0, 0)), st_spec]
    aliases = {}
    kern = _ret_sample_kernel
    if state_out is not None:
        args.append(state_out)
        in_specs.append(pl.BlockSpec(memory_space=pl.ANY))
        aliases = {len(args) - 1: 1}
        kern = functools.partial(_drop_alias_arg, kern, 8)
    return pl.pallas_call(
        kern,
        out_shape=(jax.ShapeDtypeStruct((DEC_BATCH, BRANCH_W), F32),
                   jax.ShapeDtypeStruct(state.shape, F32)),
        grid=(RET_HEADS, DEC_BATCH // nb),
        in_specs=in_specs,
        out_specs=(pl.BlockSpec((nb, RET_DK), lambda h, i: (i, h)), st_spec),
        scratch_shapes=[pltpu.VMEM((nb, RET_DK), F32)],
        input_output_aliases=aliases,
        compiler_params=_cparams(2),
        name="ret_sample",
    )(*args)


def _reorder_w_in(w):
    w = w.astype(BF16)
    pad = jnp.zeros(w.shape[:-1] + (N_IN_PAD - N_IN_ORIG,), BF16)
    return jnp.concatenate([w[..., :ORIG_SDT], w[..., ORIG_SDT + SSD_HEADS:], w[..., ORIG_SDT:ORIG_SDT + SSD_HEADS], pad],
                           axis=-1)


def kernel(x_prompt, x_sample, state_lru_h, state_lru_conv, state_hgrn, state_ssd, state_ssd_conv, state_ret, state_ffn_conv, g_mix, g_ffn, w_in, lru_conv_w, lru_conv_b, lru_wa, lru_ba, lru_wx, lru_bx, lru_lambda, hg_lb_logits, hg_norm_w, ssd_conv_w, ssd_conv_b, ssd_dt_bias, ssd_a_log, ssd_d, ssd_norm_w, w_branch, w_gate, w_out, ffn_w_up, ffn_w_val, ffn_conv_w, ffn_conv_b, ffn_w_down, g_final):
    xp = x_prompt.reshape(BATCH * SEQ, D_MODEL)
    xs = x_sample.reshape(DEC_BATCH, D_MODEL)

    cos_p, sin_p = rope_tables(SEQ, 0, True)
    cos_s, sin_s = rope_tables(SUBLANES, PAST_LEN, False)

    hp = rmsnorm(xp, g_mix[0], BF16, 512)
    hs = rmsnorm(xs, g_mix[0], BF16, DEC_BATCH)

    state_ssd_t = jnp.swapaxes(state_ssd, -1, -2)

    w_in_b = _reorder_w_in(w_in)
    w_gate_b = w_gate.reshape(DEPTH, D_MODEL, 4 * D_MODEL).astype(BF16)
    w_branch_b = w_branch.astype(BF16)
    w_out_b = w_out.astype(BF16)
    w_up_b = ffn_w_up.astype(BF16)
    w_val_b = ffn_w_val.astype(BF16)
    w_down_b = ffn_w_down.astype(BF16)

    prompt_states, sample_small = [], []
    hg_out = ssd_out = ret_out = None
    for l in range(DEPTH):
        p = {"lru_conv_w": lru_conv_w[l], "lru_conv_b": lru_conv_b[l], "lru_wa": lru_wa[l], "lru_ba": lru_ba[l],
             "lru_wx": lru_wx[l], "lru_bx": lru_bx[l], "lru_lambda": lru_lambda[l],
             "hg_lb_logits": hg_lb_logits, "hg_norm_w": hg_norm_w[l],
             "ssd_conv_w": ssd_conv_w[l], "ssd_conv_b": ssd_conv_b[l], "ssd_dt_bias": ssd_dt_bias[l],
             "ssd_a_log": ssd_a_log[l], "ssd_d": ssd_d[l], "ssd_norm_w": ssd_norm_w[l]}
        last = l == DEPTH - 1
        g_next = g_final if last else g_mix[l + 1]

        proj_p = in_proj(l, hp, w_in_b, 1024)
        out_a, lru_h_p, lru_conv_p = lru_prompt(proj_p, p)
        out_b, hg_p = hgrn_prompt(l, proj_p, p)
        out_c, ssd_p, ssd_conv_p = ssd_prompt(proj_p, p)
        out_d, ret_p = ret_prompt(proj_p, cos_p, sin_p)
        merged = gated_merge(l, hp, (out_a, out_b, out_c, out_d), w_gate_b, w_branch_b, 1024, 256)
        xp, h2 = out_proj_residual_norm(l, merged, w_out_b, xp, g_ffn[l], 512)
        act, ffn_conv_p = ffn_prompt(l, h2, w_up_b, w_val_b, ffn_conv_w[l], ffn_conv_b[l], 256)
        res = down_proj_residual_norm(l, act, w_down_b, xp, g_next, 512, 512, not last, F32 if last else BF16)
        if last:
            (yp,) = res
        else:
            xp, hp = res
        prompt_states.append((lru_h_p, lru_conv_p, hg_p, ssd_p, ssd_conv_p, ret_p, ffn_conv_p))

        proj_s = in_proj(l, hs, w_in_b, DEC_BATCH)
        lru_buf_t = jnp.swapaxes(state_lru_conv[l], 0, 1)
        ssd_buf_t = jnp.swapaxes(state_ssd_conv[l], 0, 1)
        s_a, lru_h_s, lru_nbuf = lru_sample(proj_s, p, state_lru_h[l], lru_buf_t)
        s_b, hg_out = hgrn_sample(l, proj_s, p, state_hgrn, hg_out)
        s_c, ssd_out, ssd_nbuf = ssd_sample(l, proj_s, p, state_ssd_t, ssd_out, ssd_buf_t)
        s_d, ret_out = ret_sample(l, proj_s, cos_s, sin_s, state_ret, ret_out)
        merged_s = gated_merge(l, hs, (s_a, s_b, s_c, s_d), w_gate_b, w_branch_b, DEC_BATCH, 256)
        xs, h2s = out_proj_residual_norm(l, merged_s, w_out_b, xs, g_ffn[l], DEC_BATCH)
        act_s, ffn_nbuf = ffn_sample(l, h2s, w_up_b, w_val_b, ffn_conv_w[l], ffn_conv_b[l], state_ffn_conv, 512)
        res = down_proj_residual_norm(l, act_s, w_down_b, xs, g_next, DEC_BATCH, 512, not last, F32 if last else BF16)
        if last:
            (ys,) = res
        else:
            xs, hs = res
        sample_small.append((lru_h_s, jnp.swapaxes(lru_nbuf, 0, 1), jnp.swapaxes(ssd_nbuf, 0, 1), ffn_nbuf))

    stack_p = lambda i: jnp.stack([st[i] for st in prompt_states], axis=0)
    stack_s = lambda i: jnp.stack([st[i] for st in sample_small], axis=0)
    return (yp.reshape(BATCH, SEQ, D_MODEL), ys.reshape(DEC_BATCH, 1, D_MODEL),
            stack_p(0), stack_s(0), stack_p(1), stack_s(1),
            stack_p(2), hg_out, jnp.swapaxes(stack_p(3), -1, -2), jnp.swapaxes(ssd_out, -1, -2),
            stack_p(4), stack_s(2), stack_p(5), ret_out,
            stack_p(6), stack_s(3))
```

```python
import functools
import math

import numpy as np
import jax
import jax.numpy as jnp
from jax import lax
from jax.experimental import pallas as pl
from jax.experimental.pallas import tpu as pltpu

F32 = jnp.float32
BF16 = jnp.bfloat16
HIGHEST = lax.Precision.HIGHEST

D_MODEL = 2048
BATCH = 4
SEQ = 2048
DEPTH = 2
DEC_BATCH = 128
PAST_LEN = 16384
BRANCH_W = D_MODEL // 2
EPS = 1e-6
LRU_BLOCKS = 8
LRU_BLOCK = BRANCH_W // LRU_BLOCKS
LRU_C = 8.0
CONV_W = 4
HG_HEADS = 8
HG_DK = BRANCH_W // HG_HEADS
SSD_HEADDIM = 64
SSD_HEADS = BRANCH_W // SSD_HEADDIM
SSD_GROUPS = 2
SSD_STATE = 128
SSD_BC = 2 * SSD_GROUPS * SSD_STATE
SSD_CONV_DIM = BRANCH_W + SSD_BC
RET_HEADS = 8
RET_DK = BRANCH_W // RET_HEADS
ROPE_BASE = 10000.0
D_FF = 5632
FFN_CONV_W = 3

V7X_VMEM_BYTES = 64 * 1024 * 1024
VMEM_LIMIT_BYTES = V7X_VMEM_BYTES - 8 * 1024 * 1024
LANES = 128
SUBLANES = 8

N_BRANCH = 4
N_IN = 12816
OFF_XA, OFF_YA = 0, 1024
OFF_HQ, OFF_HF, OFF_HI, OFF_HG = 2048, 3072, 4096, 5120
OFF_SZ, OFF_SX, OFF_SBC = 6144, 7168, 8192
OFF_SDT = 8704
RET_SHIFT = SSD_HEADS
OFF_RQ, OFF_RK, OFF_RV, OFF_RG = 8704, 9728, 10752, 11776
N_IN_MAIN = 12800
N_PROJ = N_IN_MAIN + LANES
IN_PROJ_TN = 1280

HG_CHUNK = 128
HG_LEVELS = (8, 16, 32, 64)
SSD_CHUNK = 128
RET_CHUNK = 256
SAMPLE_BLK = 32
SSD_SAMPLE_BLK = 8


def _cparams(n_axes):
    return pltpu.CompilerParams(dimension_semantics=("arbitrary",) * n_axes,
                                vmem_limit_bytes=VMEM_LIMIT_BYTES)


def _rms_rows(x):
    return x * lax.rsqrt(jnp.mean(x * x, axis=-1, keepdims=True) + EPS)


def _shift_rows(x, d, row):
    return jnp.where(row >= d, pltpu.roll(x, d, axis=0), 0.0)


def _nt_dot(a, b):
    return lax.dot_general(a, b, (((1,), (1,)), ((), ())), preferred_element_type=F32)


def _tn_dot(a, b):
    return lax.dot_general(a, b, (((0,), (0,)), ((), ())), preferred_element_type=F32)


def _rows_to_cols(x):
    n = x.shape[0]
    if n < LANES:
        x = jnp.concatenate([x, jnp.zeros((LANES - n, x.shape[1]), x.dtype)], axis=0)
    return x.T


def _norm_kernel(x_ref, g_ref, o_ref):
    o_ref[...] = (_rms_rows(x_ref[...]) * g_ref[...]).astype(o_ref.dtype)


def rmsnorm(x, g, out_dtype, tm):
    m, d = x.shape
    return pl.pallas_call(
        _norm_kernel,
        out_shape=jax.ShapeDtypeStruct((m, d), out_dtype),
        grid=(m // tm,),
        in_specs=[pl.BlockSpec((tm, d), lambda i: (i, 0)),
                  pl.BlockSpec((1, d), lambda i: (0, 0))],
        out_specs=pl.BlockSpec((tm, d), lambda i: (i, 0)),
        compiler_params=_cparams(1),
        name="rmsnorm",
    )(x, g.reshape(1, d))


def _mm_kernel(a_ref, b_ref, o_ref):
    o_ref[...] = jnp.dot(a_ref[...], b_ref[...], preferred_element_type=F32)


def _mm_cast_kernel(a_ref, w_ref, o_ref, wb_ref):
    @pl.when(pl.program_id(1) == 0)
    def _():
        wb_ref[...] = w_ref[...].astype(BF16)

    o_ref[...] = jnp.dot(a_ref[...], wb_ref[...], preferred_element_type=F32)


def in_proj_prompt(layer, h, w_in, tm):
    m = h.shape[0]
    return pl.pallas_call(
        _mm_cast_kernel,
        out_shape=(jax.ShapeDtypeStruct((m, N_PROJ), F32), jax.ShapeDtypeStruct((D_MODEL, N_IN_MAIN), BF16)),
        grid=(N_IN_MAIN // IN_PROJ_TN, m // tm),
        in_specs=[pl.BlockSpec((tm, D_MODEL), lambda j, i: (i, 0)),
                  pl.BlockSpec((None, D_MODEL, IN_PROJ_TN), lambda j, i: (layer, 0, j))],
        out_specs=(pl.BlockSpec((tm, IN_PROJ_TN), lambda j, i: (i, j)),
                   pl.BlockSpec((D_MODEL, IN_PROJ_TN), lambda j, i: (0, j))),
        compiler_params=_cparams(2),
        name="in_proj_prompt",
    )(h, w_in)


def in_proj_sample(h, w_main_b):
    m = h.shape[0]
    return pl.pallas_call(
        _mm_kernel,
        out_shape=jax.ShapeDtypeStruct((m, N_PROJ), F32),
        grid=(N_IN_MAIN // IN_PROJ_TN,),
        in_specs=[pl.BlockSpec((m, D_MODEL), lambda j: (0, 0)),
                  pl.BlockSpec((D_MODEL, IN_PROJ_TN), lambda j: (0, j))],
        out_specs=pl.BlockSpec((m, IN_PROJ_TN), lambda j: (0, j)),
        compiler_params=_cparams(1),
        name="in_proj_sample",
    )(h, w_main_b)


def _mm_tail_kernel(a_ref, b_ref, proj_hbm_ref, o_ref):
    del proj_hbm_ref
    o_ref[...] = jnp.dot(a_ref[...], b_ref[...], preferred_element_type=F32)


def in_proj_tail(layer, h, w_tail, proj, tm):
    m = h.shape[0]
    return pl.pallas_call(
        _mm_tail_kernel,
        out_shape=jax.ShapeDtypeStruct(proj.shape, F32),
        grid=(m // tm,),
        in_specs=[pl.BlockSpec((tm, D_MODEL), lambda i: (i, 0)),
                  pl.BlockSpec((None, D_MODEL, LANES), lambda i: (layer, 0, 0)),
                  pl.BlockSpec(memory_space=pl.ANY)],
        out_specs=pl.BlockSpec((tm, LANES), lambda i: (i, N_IN_MAIN // LANES)),
        input_output_aliases={2: 0},
        compiler_params=_cparams(1),
        name="in_proj_tail",
    )(h, w_tail, proj)


def _gated_sum(h, br_refs, gate_w, branch_w):
    acc = None
    for k, br_ref in enumerate(br_refs):
        gate = jax.nn.sigmoid(jnp.dot(h, gate_w(k), preferred_element_type=F32))
        br = jnp.dot(br_ref[...].astype(BF16), branch_w(k), preferred_element_type=F32)
        acc = gate * br if acc is None else acc + gate * br
    return acc


def _merge_cast_kernel(h_ref, a_ref, b_ref, c_ref, d_ref, wg_ref, wb_ref, o_ref, wgb_ref, wbb_ref):
    @pl.when(pl.program_id(1) == 0)
    def _():
        for k in range(N_BRANCH):
            wgb_ref[k] = wg_ref[:, k, :].astype(BF16)
            wbb_ref[k] = wb_ref[k].astype(BF16)

    acc = _gated_sum(h_ref[...], (a_ref, b_ref, c_ref, d_ref), lambda k: wgb_ref[k], lambda k: wbb_ref[k])
    o_ref[...] = acc.astype(o_ref.dtype)


def gated_merge_prompt(layer, h, branches, w_gate, w_branch, tm, tn):
    m = h.shape[0]
    br_specs = [pl.BlockSpec((tm, BRANCH_W), lambda j, i: (i, 0)) for _ in range(N_BRANCH)]
    return pl.pallas_call(
        _merge_cast_kernel,
        out_shape=(jax.ShapeDtypeStruct((m, D_MODEL), BF16),
                   jax.ShapeDtypeStruct((N_BRANCH, D_MODEL, D_MODEL), BF16),
                   jax.ShapeDtypeStruct((N_BRANCH, BRANCH_W, D_MODEL), BF16)),
        grid=(D_MODEL // tn, m // tm),
        in_specs=[pl.BlockSpec((tm, D_MODEL), lambda j, i: (i, 0))] + br_specs
                 + [pl.BlockSpec((None, D_MODEL, N_BRANCH, tn), lambda j, i: (layer, 0, 0, j)),
                    pl.BlockSpec((None, N_BRANCH, BRANCH_W, tn), lambda j, i: (layer, 0, 0, j))],
        out_specs=(pl.BlockSpec((tm, tn), lambda j, i: (i, j)),
                   pl.BlockSpec((N_BRANCH, D_MODEL, tn), lambda j, i: (0, 0, j)),
                   pl.BlockSpec((N_BRANCH, BRANCH_W, tn), lambda j, i: (0, 0, j))),
        compiler_params=_cparams(2),
        name="gated_merge_prompt",
    )(h, *branches, w_gate, w_branch)


def _merge_kernel(h_ref, a_ref, b_ref, c_ref, d_ref, wgb_ref, wbb_ref, o_ref):
    acc = _gated_sum(h_ref[...], (a_ref, b_ref, c_ref, d_ref), lambda k: wgb_ref[k], lambda k: wbb_ref[k])
    o_ref[...] = acc.astype(o_ref.dtype)


def gated_merge_sample(h, branches, w_gate_b, w_branch_b, tn):
    m = h.shape[0]
    br_specs = [pl.BlockSpec((m, BRANCH_W), lambda j: (0, 0)) for _ in range(N_BRANCH)]
    return pl.pallas_call(
        _merge_kernel,
        out_shape=jax.ShapeDtypeStruct((m, D_MODEL), BF16),
        grid=(D_MODEL // tn,),
        in_specs=[pl.BlockSpec((m, D_MODEL), lambda j: (0, 0))] + br_specs
                 + [pl.BlockSpec((N_BRANCH, D_MODEL, tn), lambda j: (0, 0, j)),
                    pl.BlockSpec((N_BRANCH, BRANCH_W, tn), lambda j: (0, 0, j))],
        out_specs=pl.BlockSpec((m, tn), lambda j: (0, j)),
        compiler_params=_cparams(1),
        name="gated_merge_sample",
    )(h, *branches, w_gate_b, w_branch_b)


def _out_proj_kernel(m_ref, w_ref, x_ref, g_ref, xo_ref, ho_ref):
    x_new = x_ref[...] + jnp.dot(m_ref[...], w_ref[...], preferred_element_type=F32)
    xo_ref[...] = x_new
    ho_ref[...] = (_rms_rows(x_new) * g_ref[...]).astype(ho_ref.dtype)


def out_proj_residual_norm(layer, merged, w_out, x, g, tm):
    m = x.shape[0]
    return pl.pallas_call(
        _out_proj_kernel,
        out_shape=(jax.ShapeDtypeStruct((m, D_MODEL), F32), jax.ShapeDtypeStruct((m, D_MODEL), BF16)),
        grid=(m // tm,),
        in_specs=[pl.BlockSpec((tm, D_MODEL), lambda i: (i, 0)),
                  pl.BlockSpec((None, D_MODEL, D_MODEL), lambda i: (layer, 0, 0)),
                  pl.BlockSpec((tm, D_MODEL), lambda i: (i, 0)),
                  pl.BlockSpec((1, D_MODEL), lambda i: (0, 0))],
        out_specs=(pl.BlockSpec((tm, D_MODEL), lambda i: (i, 0)),
                   pl.BlockSpec((tm, D_MODEL), lambda i: (i, 0))),
        compiler_params=_cparams(1),
        name="out_proj",
    )(merged, w_out, x, g.reshape(1, D_MODEL))


def _down_proj_kernel(emit_x, a_ref, w_ref, x_ref, g_ref, *refs):
    if emit_x:
        xo_ref, no_ref, acc_ref = refs
    else:
        no_ref, acc_ref = refs
    kk = pl.program_id(1)

    @pl.when(kk == 0)
    def _():
        acc_ref[...] = x_ref[...]

    acc_ref[...] += jnp.dot(a_ref[...], w_ref[...], preferred_element_type=F32)

    @pl.when(kk == pl.num_programs(1) - 1)
    def _():
        x_new = acc_ref[...]
        if emit_x:
            xo_ref[...] = x_new
        no_ref[...] = (_rms_rows(x_new) * g_ref[...]).astype(no_ref.dtype)


def down_proj_residual_norm(layer, a, w_down, x, g, tm, tk, emit_x, norm_dtype):
    m = x.shape[0]
    out_shape = [jax.ShapeDtypeStruct((m, D_MODEL), norm_dtype)]
    out_specs = [pl.BlockSpec((tm, D_MODEL), lambda i, k: (i, 0))]
    if emit_x:
        out_shape = [jax.ShapeDtypeStruct((m, D_MODEL), F32)] + out_shape
        out_specs = [pl.BlockSpec((tm, D_MODEL), lambda i, k: (i, 0))] + out_specs
    return pl.pallas_call(
        functools.partial(_down_proj_kernel, emit_x),
        out_shape=tuple(out_shape),
        grid=(m // tm, D_FF // tk),
        in_specs=[pl.BlockSpec((tm, tk), lambda i, k: (i, k)),
                  pl.BlockSpec((None, tk, D_MODEL), lambda i, k: (layer, k, 0)),
                  pl.BlockSpec((tm, D_MODEL), lambda i, k: (i, 0)),
                  pl.BlockSpec((1, D_MODEL), lambda i, k: (0, 0))],
        out_specs=tuple(out_specs),
        scratch_shapes=[pltpu.VMEM((tm, D_MODEL), F32)],
        compiler_params=_cparams(2),
        name="down_proj",
    )(a, w_down, x, g.reshape(1, D_MODEL))


def _ffn_prompt_kernel(h_ref, wu_ref, wv_ref, cw_ref, cb_ref, a_ref, st_ref, wub_ref, wvb_ref):
    @pl.when(pl.program_id(1) == 0)
    def _():
        wub_ref[...] = wu_ref[...].astype(BF16)
        wvb_ref[...] = wv_ref[...].astype(BF16)

    h = h_ref[...]
    u = jnp.dot(h, wub_ref[...], preferred_element_type=F32)
    v = jnp.dot(h, wvb_ref[...], preferred_element_type=F32)
    row = lax.broadcasted_iota(jnp.int32, u.shape, 0)
    cw = cw_ref[...]
    uc = cb_ref[...] + cw[2:3] * u + cw[1:2] * _shift_rows(u, 1, row) + cw[0:1] * _shift_rows(u, 2, row)
    a_ref[...] = (jax.nn.gelu(uc) * v).astype(a_ref.dtype)
    t = u.shape[0]
    st_ref[...] = u[t - (FFN_CONV_W - 1):, :]


def ffn_prompt(layer, h2, w_up, w_val, conv_w, conv_b, tn):
    wspec = pl.BlockSpec((None, D_MODEL, tn), lambda j, b: (layer, 0, j))
    wbspec = pl.BlockSpec((D_MODEL, tn), lambda j, b: (0, j))
    return pl.pallas_call(
        _ffn_prompt_kernel,
        out_shape=(jax.ShapeDtypeStruct((BATCH * SEQ, D_FF), BF16),
                   jax.ShapeDtypeStruct((BATCH, FFN_CONV_W - 1, D_FF), F32),
                   jax.ShapeDtypeStruct((D_MODEL, D_FF), BF16),
                   jax.ShapeDtypeStruct((D_MODEL, D_FF), BF16)),
        grid=(D_FF // tn, BATCH),
        in_specs=[pl.BlockSpec((SEQ, D_MODEL), lambda j, b: (b, 0)), wspec, wspec,
                  pl.BlockSpec((FFN_CONV_W, tn), lambda j, b: (0, j)),
                  pl.BlockSpec((1, tn), lambda j, b: (0, j))],
        out_specs=(pl.BlockSpec((SEQ, tn), lambda j, b: (b, j)),
                   pl.BlockSpec((None, FFN_CONV_W - 1, tn), lambda j, b: (b, 0, j)),
                   wbspec, wbspec),
        compiler_params=_cparams(2),
        name="ffn_prompt",
    )(h2, w_up, w_val, conv_w, conv_b.reshape(1, D_FF))


def _ffn_sample_kernel(h_ref, wu_ref, wv_ref, cw_ref, cb_ref, buf_ref, a_ref, nb_ref):
    h = h_ref[...]
    u = jnp.dot(h, wu_ref[...], preferred_element_type=F32)
    v = jnp.dot(h, wv_ref[...], preferred_element_type=F32)
    cw = cw_ref[...]
    b0 = buf_ref[:, 0, :]
    b1 = buf_ref[:, 1, :]
    uc = cb_ref[...] + cw[0:1] * b0 + cw[1:2] * b1 + cw[2:3] * u
    a_ref[...] = (jax.nn.gelu(uc) * v).astype(a_ref.dtype)
    nb_ref[:, 0, :] = b1
    nb_ref[:, 1, :] = u


def ffn_sample(layer, h2, w_up, w_val, conv_w, conv_b, buf, tn):
    return pl.pallas_call(
        _ffn_sample_kernel,
        out_shape=(jax.ShapeDtypeStruct((DEC_BATCH, D_FF), BF16),
                   jax.ShapeDtypeStruct((DEC_BATCH, FFN_CONV_W - 1, D_FF), F32)),
        grid=(D_FF // tn,),
        in_specs=[pl.BlockSpec((DEC_BATCH, D_MODEL), lambda j: (0, 0)),
                  pl.BlockSpec((D_MODEL, tn), lambda j: (0, j)),
                  pl.BlockSpec((D_MODEL, tn), lambda j: (0, j)),
                  pl.BlockSpec((FFN_CONV_W, tn), lambda j: (0, j)),
                  pl.BlockSpec((1, tn), lambda j: (0, j)),
                  pl.BlockSpec((None, DEC_BATCH, FFN_CONV_W - 1, tn), lambda j: (layer, 0, 0, j))],
        out_specs=(pl.BlockSpec((DEC_BATCH, tn), lambda j: (0, j)),
                   pl.BlockSpec((DEC_BATCH, FFN_CONV_W - 1, tn), lambda j: (0, 0, j))),
        compiler_params=_cparams(1),
        name="ffn_sample",
    )(h2, w_up, w_val, conv_w, conv_b.reshape(1, D_FF), buf)


def _rope_table_kernel(start, consecutive, freq_ref, sign_ref, cos_ref, sin_ref):
    shape = cos_ref.shape
    if consecutive:
        pos = lax.broadcasted_iota(jnp.int32, shape, 0).astype(F32) + float(start)
    else:
        pos = jnp.full(shape, float(start), F32)
    ang = pos * freq_ref[...]
    cos_ref[...] = jnp.cos(ang)
    sin_ref[...] = sign_ref[...] * jnp.sin(ang)


def rope_tables(n_rows, start, consecutive):
    half = RET_DK // 2
    freqs = ROPE_BASE ** (-jnp.arange(half, dtype=F32) / half)
    freq2 = jnp.concatenate([freqs, freqs]).reshape(1, RET_DK)
    sign = jnp.concatenate([-jnp.ones((half,), F32), jnp.ones((half,), F32)]).reshape(1, RET_DK)
    return pl.pallas_call(
        functools.partial(_rope_table_kernel, start, consecutive),
        out_shape=(jax.ShapeDtypeStruct((n_rows, RET_DK), F32), jax.ShapeDtypeStruct((n_rows, RET_DK), F32)),
        name="rope_tables",
    )(freq2, sign)


def _rope(x, cos, sin_signed):
    return x * cos + pltpu.roll(x, RET_DK // 2, axis=1) * sin_signed


def _lru_gates(conv, wa_ref, ba_ref, wx_ref, bx_ref, lam_ref):
    xb = conv.astype(BF16)
    r = jax.nn.sigmoid(jnp.dot(xb, wa_ref[...].astype(BF16), preferred_element_type=F32) + ba_ref[...])
    i = jax.nn.sigmoid(jnp.dot(xb, wx_ref[...].astype(BF16), preferred_element_type=F32) + bx_ref[...])
    log_a = -LRU_C * r * jax.nn.softplus(-lam_ref[...])
    a = jnp.exp(log_a)
    u = jnp.sqrt(1.0 - a * a) * (i * conv)
    return a, u


def _lru_prompt_kernel(xa_ref, ya_ref, cw_ref, cb_ref, wa_ref, ba_ref, wx_ref, bx_ref, lam_ref,
                       out_ref, h_ref, conv_ref, ag_s, ug_s):
    x = xa_ref[...]
    t_len = x.shape[0]
    row = lax.broadcasted_iota(jnp.int32, x.shape, 0)
    cw = cw_ref[...]
    conv = cb_ref[...] + cw[CONV_W - 1:CONV_W] * x
    for d in range(1, CONV_W):
        conv = conv + cw[CONV_W - 1 - d:CONV_W - d] * _shift_rows(x, d, row)
    a, u = _lru_gates(conv, wa_ref, ba_ref, wx_ref, bx_ref, lam_ref)
    ng = t_len // SUBLANES
    a3 = a.reshape(ng, SUBLANES, LRU_BLOCK)
    u3 = u.reshape(ng, SUBLANES, LRU_BLOCK)
    sub = lax.broadcasted_iota(jnp.int32, a3.shape, 1)
    d = 1
    while d < SUBLANES:
        keep = sub >= d
        u3 = jnp.where(keep, a3 * pltpu.roll(u3, d, axis=1) + u3, u3)
        a3 = jnp.where(keep, a3 * pltpu.roll(a3, d, axis=1), a3)
        d *= 2
    ag_s[...] = a3.reshape(t_len, LRU_BLOCK)
    ug_s[...] = u3.reshape(t_len, LRU_BLOCK)
    ag = ag_s[pl.ds(SUBLANES - 1, ng, stride=SUBLANES), :]
    ug = ug_s[pl.ds(SUBLANES - 1, ng, stride=SUBLANES), :]
    grow = lax.broadcasted_iota(jnp.int32, ag.shape, 0)
    d = 1
    while d < ng:
        keep = grow >= d
        ug = jnp.where(keep, ag * pltpu.roll(ug, d, axis=0) + ug, ug)
        ag = jnp.where(keep, ag * pltpu.roll(ag, d, axis=0), ag)
        d *= 2
    carry = _shift_rows(ug, 1, grow)
    h3 = a3 * jnp.broadcast_to(carry[:, None, :], a3.shape) + u3
    hs = h3.reshape(t_len, LRU_BLOCK)
    out_ref[...] = (hs * jax.nn.gelu(ya_ref[...])).astype(out_ref.dtype)
    h_ref[...] = ug[ng - 1:, :]
    conv_ref[...] = xa_ref[pl.ds(t_len - (CONV_W - 1), CONV_W - 1), :]


def _lru_param_specs(n_axes_fn):
    blk3 = lambda shape: pl.BlockSpec(shape, n_axes_fn(lambda n: (n, 0, 0)))
    return [pl.BlockSpec((CONV_W, LRU_BLOCK), n_axes_fn(lambda n: (0, n))),
            pl.BlockSpec((1, LRU_BLOCK), n_axes_fn(lambda n: (0, n))),
            blk3((None, LRU_BLOCK, LRU_BLOCK)), blk3((None, 1, LRU_BLOCK)),
            blk3((None, LRU_BLOCK, LRU_BLOCK)), blk3((None, 1, LRU_BLOCK)),
            blk3((None, 1, LRU_BLOCK))]


def _lru_params(p):
    return (p["lru_conv_w"], p["lru_conv_b"].reshape(1, BRANCH_W),
            p["lru_wa"], p["lru_ba"].reshape(LRU_BLOCKS, 1, LRU_BLOCK),
            p["lru_wx"], p["lru_bx"].reshape(LRU_BLOCKS, 1, LRU_BLOCK),
            p["lru_lambda"].reshape(LRU_BLOCKS, 1, LRU_BLOCK))


def lru_prompt(proj, p):
    cb = lambda off: off // LRU_BLOCK
    wrap = lambda f: (lambda b, n: f(n))
    out, h, conv = pl.pallas_call(
        _lru_prompt_kernel,
        out_shape=(jax.ShapeDtypeStruct((BATCH * SEQ, BRANCH_W), BF16),
                   jax.ShapeDtypeStruct((BATCH, 1, BRANCH_W), F32),
                   jax.ShapeDtypeStruct((BATCH, CONV_W - 1, BRANCH_W), F32)),
        grid=(BATCH, LRU_BLOCKS),
        in_specs=[pl.BlockSpec((SEQ, LRU_BLOCK), lambda b, n: (b, cb(OFF_XA) + n)),
                  pl.BlockSpec((SEQ, LRU_BLOCK), lambda b, n: (b, cb(OFF_YA) + n))] + _lru_param_specs(wrap),
        out_specs=(pl.BlockSpec((SEQ, LRU_BLOCK), lambda b, n: (b, n)),
                   pl.BlockSpec((None, 1, LRU_BLOCK), lambda b, n: (b, 0, n)),
                   pl.BlockSpec((None, CONV_W - 1, LRU_BLOCK), lambda b, n: (b, 0, n))),
        scratch_shapes=[pltpu.VMEM((SEQ, LRU_BLOCK), F32)] * 2,
        compiler_params=_cparams(2),
        name="lru_prompt",
    )(proj, proj, *_lru_params(p))
    return out, h.reshape(BATCH, BRANCH_W), conv


def _lru_sample_kernel(xa_ref, ya_ref, cw_ref, cb_ref, wa_ref, ba_ref, wx_ref, bx_ref, lam_ref,
                       h0_ref, buf_ref, out_ref, h_ref, nbuf_ref):
    x = xa_ref[...]
    cw = cw_ref[...]
    conv = cb_ref[...] + cw[CONV_W - 1:CONV_W] * x
    for j in range(CONV_W - 1):
        conv = conv + cw[j:j + 1] * buf_ref[j]
    a, u = _lru_gates(conv, wa_ref, ba_ref, wx_ref, bx_ref, lam_ref)
    h = a * h0_ref[...] + u
    out_ref[...] = h * jax.nn.gelu(ya_ref[...])
    h_ref[...] = h
    for j in range(CONV_W - 2):
        nbuf_ref[j] = buf_ref[j + 1]
    nbuf_ref[CONV_W - 2] = x


def lru_sample(proj, p, h0, buf_t):
    cb = lambda off: off // LRU_BLOCK
    wrap = lambda f: f
    return pl.pallas_call(
        _lru_sample_kernel,
        out_shape=(jax.ShapeDtypeStruct((DEC_BATCH, BRANCH_W), F32),
                   jax.ShapeDtypeStruct((DEC_BATCH, BRANCH_W), F32),
                   jax.ShapeDtypeStruct((CONV_W - 1, DEC_BATCH, BRANCH_W), F32)),
        grid=(LRU_BLOCKS,),
        in_specs=[pl.BlockSpec((DEC_BATCH, LRU_BLOCK), lambda n: (0, cb(OFF_XA) + n)),
                  pl.BlockSpec((DEC_BATCH, LRU_BLOCK), lambda n: (0, cb(OFF_YA) + n))]
                 + _lru_param_specs(wrap)
                 + [pl.BlockSpec((DEC_BATCH, LRU_BLOCK), lambda n: (0, n)),
                    pl.BlockSpec((CONV_W - 1, DEC_BATCH, LRU_BLOCK), lambda n: (0, 0, n))],
        out_specs=(pl.BlockSpec((DEC_BATCH, LRU_BLOCK), lambda n: (0, n)),
                   pl.BlockSpec((DEC_BATCH, LRU_BLOCK), lambda n: (0, n)),
                   pl.BlockSpec((CONV_W - 1, DEC_BATCH, LRU_BLOCK), lambda n: (0, 0, n))),
        compiler_params=_cparams(1),
        name="lru_sample",
    )(proj, proj, *_lru_params(p), h0, buf_t)


def _hgrn_lower_bound(layer, logits):
    mx = jnp.max(logits, axis=0, keepdims=True)
    e = jnp.exp(logits - mx)
    ls = e / jnp.sum(e, axis=0, keepdims=True)
    lb = jnp.zeros_like(ls[0:1])
    for i in range(1, layer + 1):
        lb = lb + ls[i:i + 1]
    return lb


def _hgrn_gates(layer, hq, hf, lbl_ref):
    lb = _hgrn_lower_bound(layer, lbl_ref[...])
    q = jax.nn.silu(hq)
    sg = jax.nn.sigmoid(hf)
    f = lb + (1.0 - lb) * sg
    k = (1.0 - lb) * (1.0 - sg)
    return q, f, k


def _hgrn_level_ids():
    c = HG_CHUNK
    t = np.arange(c)[:, None]
    s = np.arange(c)[None, :]
    level = np.zeros((c, c), np.int32)
    for li, m in enumerate(HG_LEVELS):
        same = (t // (2 * m)) == (s // (2 * m))
        level[same & ((t % (2 * m)) >= m) & ((s % (2 * m)) < m)] = li + 1
    return level


def _split3_bf16(x):
    hi = x.astype(BF16)
    r1 = x - hi.astype(F32)
    mid = r1.astype(BF16)
    lo = (r1 - mid.astype(F32)).astype(BF16)
    return hi, mid, lo


def _cumsum_rows(tril_b, x):
    return sum(jnp.dot(tril_b, piece, preferred_element_type=F32) for piece in _split3_bf16(x))


def _hgrn_prompt_kernel(layer, q_ref, f_ref, i_ref, g_ref, lbl_ref, nw_ref, lvl_ref,
                        out_ref, st_ref, lf_s, q_s, k_s, od_s, o_s):
    c = HG_CHUNK
    t_len = q_ref.shape[0]
    ng = t_len // SUBLANES
    q, f, k = _hgrn_gates(layer, q_ref[...], f_ref[...], lbl_ref)
    logf = jnp.log(f)
    lf_s[...] = logf
    q_s[...] = q
    k_s[...] = k

    g3 = lambda x: x.reshape(ng, SUBLANES, LANES)
    q3, k3, lf3, v3 = g3(q), g3(k), g3(logf), g3(i_ref[...])
    sub = lax.broadcasted_iota(jnp.int32, q3.shape, 1)
    ones = jnp.ones((LANES, LANES), BF16)
    dd = None
    od = None
    for d in range(SUBLANES):
        if d == 0:
            p = q3 * k3
            vd = v3
        else:
            step = lf3 if d == 1 else pltpu.roll(lf3, d - 1, axis=1)
            dd = step if dd is None else dd + step
            p = jnp.where(sub >= d, q3 * jnp.exp(dd) * pltpu.roll(k3, d, axis=1), 0.0)
            vd = pltpu.roll(v3, d, axis=1)
        r = jnp.dot(p.reshape(t_len, LANES).astype(BF16), ones, preferred_element_type=F32)
        od = g3(r) * vd if od is None else od + g3(r) * vd
    od_s[...] = od.reshape(t_len, LANES)

    lvl = lvl_ref[...]
    tril_b = (lax.broadcasted_iota(jnp.int32, (c, c), 0) >= lax.broadcasted_iota(jnp.int32, (c, c), 1)).astype(BF16)

    st = jnp.zeros((HG_DK, HG_DK), F32)
    for ci in range(t_len // c):
        sl = slice(ci * c, (ci + 1) * c)
        qc = q_s[sl, :]
        kc = k_s[sl, :]
        vb = i_ref[sl, :].astype(BF16)
        b = _cumsum_rows(tril_b, lf_s[sl, :])
        att = jnp.zeros((c, c), F32)
        for li, m in enumerate(HG_LEVELS):
            b3 = b.reshape(c // (2 * m), 2 * m, LANES)
            mid = b3[:, m - 1:m, :]
            upper = lax.broadcasted_iota(jnp.int32, b3.shape, 1) >= m
            e = jnp.exp(jnp.where(upper, b3 - mid, mid - b3)).reshape(c, LANES)
            a_l = _nt_dot((qc * e).astype(BF16), (kc * e).astype(BF16))
            att = jnp.where(lvl == li + 1, a_l, att)
        o = (jnp.dot(att.astype(BF16), vb, preferred_element_type=F32) + od_s[sl, :]
             + _nt_dot((qc * jnp.exp(b)).astype(BF16), st.astype(BF16)))
        o_s[sl, :] = o
        bl = b[c - 1:c, :]
        kdec = (kc * jnp.exp(bl - b)).astype(BF16)
        st = st * jnp.exp(bl) + _tn_dot(vb, kdec)
    o = o_s[...]
    out_ref[...] = (_rms_rows(o) * nw_ref[...] * jax.nn.silu(g_ref[...])).astype(out_ref.dtype)
    st_ref[...] = st.T


def hgrn_prompt(layer, proj, p):
    cb = lambda off: off // HG_DK
    level = _hgrn_level_ids()
    col = lambda off: pl.BlockSpec((SEQ, HG_DK), lambda b, h: (b, cb(off) + h))
    return pl.pallas_call(
        functools.partial(_hgrn_prompt_kernel, layer),
        out_shape=(jax.ShapeDtypeStruct((BATCH * SEQ, BRANCH_W), BF16),
                   jax.ShapeDtypeStruct((BATCH, HG_HEADS, HG_DK, HG_DK), F32)),
        grid=(BATCH, HG_HEADS),
        in_specs=[col(OFF_HQ), col(OFF_HF), col(OFF_HI), col(OFF_HG),
                  pl.BlockSpec((DEPTH, HG_DK), lambda b, h: (0, h)),
                  pl.BlockSpec((1, HG_DK), lambda b, h: (0, 0)),
                  pl.BlockSpec(level.shape, lambda b, h: (0, 0))],
        out_specs=(pl.BlockSpec((SEQ, HG_DK), lambda b, h: (b, h)),
                   pl.BlockSpec((None, None, HG_DK, HG_DK), lambda b, h: (b, h, 0, 0))),
        scratch_shapes=[pltpu.VMEM((SEQ, HG_DK), F32)] * 5,
        compiler_params=_cparams(2),
        name="hgrn_prompt",
    )(proj, proj, proj, proj, p["hg_lb_logits"], p["hg_norm_w"].reshape(1, HG_DK),
      jnp.asarray(level))


def _hgrn_sample_kernel(layer, q_ref, f_ref, i_ref, g_ref, lbl_ref, nw_ref, s_ref, out_ref, so_ref, o_s):
    q, f, k = _hgrn_gates(layer, q_ref[...], f_ref[...], lbl_ref)
    qc, fc, kc = _rows_to_cols(q), _rows_to_cols(f), _rows_to_cols(k)
    nb = q.shape[0]
    for j in range(nb):
        s_new = s_ref[j] * fc[:, j:j + 1] + kc[:, j:j + 1] * i_ref[j:j + 1, :]
        so_ref[j] = s_new
        o_s[j:j + 1, :] = jnp.sum(s_new * qc[:, j:j + 1], axis=0, keepdims=True)
    out_ref[...] = _rms_rows(o_s[...]) * nw_ref[...] * jax.nn.silu(g_ref[...])


def hgrn_sample(layer, proj, p, state, state_out):
    cb = lambda off: off // HG_DK
    nb = SAMPLE_BLK
    col = lambda off: pl.BlockSpec((nb, HG_DK), lambda h, i: (i, cb(off) + h))
    st_spec = pl.BlockSpec((None, nb, None, HG_DK, HG_DK), lambda h, i: (layer, i, h, 0, 0))
    args = [proj, proj, proj, proj, p["hg_lb_logits"], p["hg_norm_w"].reshape(1, HG_DK), state]
    in_specs = [col(OFF_HQ), col(OFF_HF), col(OFF_HI), col(OFF_HG),
                pl.BlockSpec((DEPTH, HG_DK), lambda h, i: (0, h)),
                pl.BlockSpec((1, HG_DK), lambda h, i: (0, 0)),
                st_spec]
    aliases = {}
    kern = functools.partial(_hgrn_sample_kernel, layer)
    if state_out is not None:
        args.append(state_out)
        in_specs.append(pl.BlockSpec(memory_space=pl.ANY))
        aliases = {len(args) - 1: 1}
        kern = functools.partial(_drop_alias_arg, kern, 7)
    return pl.pallas_call(
        kern,
        out_shape=(jax.ShapeDtypeStruct((DEC_BATCH, BRANCH_W), F32),
                   jax.ShapeDtypeStruct(state.shape, F32)),
        grid=(HG_HEADS, DEC_BATCH // nb),
        in_specs=in_specs,
        out_specs=(pl.BlockSpec((nb, HG_DK), lambda h, i: (i, h)), st_spec),
        scratch_shapes=[pltpu.VMEM((nb, HG_DK), F32)],
        input_output_aliases=aliases,
        compiler_params=_cparams(2),
        name="hgrn_sample",
    )(*args)


def _drop_alias_arg(kern, pos, *refs):
    return kern(*refs[:pos], *refs[pos + 1:])


def _head_pair(cols, h0, lane_lo):
    return jnp.where(lane_lo, cols[:, h0:h0 + 1], cols[:, h0 + 1:h0 + 2])


def _ssd_prompt_kernel(z_ref, x_ref, bc_ref, dt_ref, cw_ref, cb_ref, dtb_ref, alog_ref, dpar_ref, nw_ref,
                       tril_ref, out_ref, st_ref, cst_ref, cx_s, cbc_s, s_s):
    c = SSD_CHUNK
    ci = pl.program_id(1)

    @pl.when(ci == 0)
    def _():
        cx_s[...] = jnp.zeros_like(cx_s)
        cbc_s[...] = jnp.zeros_like(cbc_s)
        s_s[...] = jnp.zeros_like(s_s)

    cw = cw_ref[...]
    cbias = cb_ref[...]

    def conv_silu(raw, carry_ref, lo, hi):
        xx = jnp.concatenate([carry_ref[...], raw], axis=0)
        y = cbias[:, lo:hi] + cw[CONV_W - 1:CONV_W, lo:hi] * raw
        for d in range(1, CONV_W):
            y = y + cw[CONV_W - 1 - d:CONV_W - d, lo:hi] * pltpu.roll(xx, d, axis=0)[SUBLANES:]
        carry_ref[...] = raw[c - SUBLANES:, :]
        return jax.nn.silu(y)

    x_raw = x_ref[...]
    bc_raw = bc_ref[...]
    xs = conv_silu(x_raw, cx_s, 0, BRANCH_W)
    bc = conv_silu(bc_raw, cbc_s, BRANCH_W, SSD_CONV_DIM)

    dt = jax.nn.softplus(dt_ref[...] + dtb_ref[...])
    a_neg = -jnp.exp(alog_ref[...])
    logd = dt * a_neg
    b = jnp.dot(tril_ref[...], logd, precision=HIGHEST, preferred_element_type=F32)
    b_t = b.T
    bl = b[c - 1:c, :]
    e_in = jnp.exp(b)
    w_out = jnp.exp(bl - b)
    e_last = jnp.exp(bl)
    dfull = dpar_ref[...]

    tri = lax.broadcasted_iota(jnp.int32, (c, c), 0) >= lax.broadcasted_iota(jnp.int32, (c, c), 1)
    lane_lo = lax.broadcasted_iota(jnp.int32, (c, LANES), 1) < SSD_HEADDIM
    lane_lo_row = lax.broadcasted_iota(jnp.int32, (1, LANES), 1) < SSD_HEADDIM

    ys = []
    for g in range(SSD_GROUPS):
        bm = bc[:, g * SSD_STATE:(g + 1) * SSD_STATE].astype(BF16)
        cm = bc[:, (SSD_GROUPS + g) * SSD_STATE:(SSD_GROUPS + g + 1) * SSD_STATE].astype(BF16)
        gmat = _nt_dot(cm, bm)
        for pp in range(SSD_HEADS // SSD_GROUPS // 2):
            pi = g * (SSD_HEADS // SSD_GROUPS // 2) + pp
            h0 = 2 * pi
            xs_p = xs[:, pi * LANES:(pi + 1) * LANES]
            vdt = xs_p * _head_pair(dt, h0, lane_lo)
            vdt_b = vdt.astype(BF16)
            o_heads = []
            for hh in (h0, h0 + 1):
                diff = b[:, hh:hh + 1] - b_t[hh:hh + 1, :]
                dec = jnp.where(tri, jnp.exp(jnp.where(tri, diff, 0.0)), 0.0)
                o_heads.append(jnp.dot((gmat * dec).astype(BF16), vdt_b, preferred_element_type=F32))
            o_intra = jnp.where(lane_lo, o_heads[0], o_heads[1])
            s_p = s_s[pi]
            o_inter = _head_pair(e_in, h0, lane_lo) * jnp.dot(cm, s_p.astype(BF16), preferred_element_type=F32)
            ys.append(o_intra + o_inter + dfull[:, pi * LANES:(pi + 1) * LANES] * xs_p)
            upd = _tn_dot(bm, (vdt * _head_pair(w_out, h0, lane_lo)).astype(BF16))
            s_s[pi] = s_p * _head_pair(e_last, h0, lane_lo_row) + upd

    y = jnp.concatenate(ys, axis=1) * jax.nn.silu(z_ref[...])
    gw = BRANCH_W // SSD_GROUPS
    nw = nw_ref[...]
    outs = [_rms_rows(y[:, g * gw:(g + 1) * gw]) * nw[:, g * gw:(g + 1) * gw] for g in range(SSD_GROUPS)]
    out_ref[...] = jnp.concatenate(outs, axis=1).astype(out_ref.dtype)

    @pl.when(ci == pl.num_programs(1) - 1)
    def _():
        for pi in range(SSD_HEADS // 2):
            s_t = s_s[pi].T
            st_ref[2 * pi] = s_t[:SSD_HEADDIM, :]
            st_ref[2 * pi + 1] = s_t[SSD_HEADDIM:, :]
        cst_ref[:, 0:BRANCH_W] = x_raw[c - (CONV_W - 1):, :]
        cst_ref[:, BRANCH_W:SSD_CONV_DIM] = bc_raw[c - (CONV_W - 1):, :]


def _pad_lanes(v):
    return jnp.pad(v.astype(F32), (0, LANES - v.shape[0])).reshape(1, LANES)


def _ssd_params(p):
    return (p["ssd_conv_w"], p["ssd_conv_b"].reshape(1, SSD_CONV_DIM), _pad_lanes(p["ssd_dt_bias"]),
            _pad_lanes(p["ssd_a_log"]), jnp.repeat(p["ssd_d"].astype(F32), SSD_HEADDIM).reshape(1, BRANCH_W),
            p["ssd_norm_w"].reshape(1, BRANCH_W))


def ssd_prompt(proj, p):
    c = SSD_CHUNK
    nc = SEQ // c
    tril = jnp.asarray(np.tril(np.ones((c, c), np.float32)))
    const = lambda shape: pl.BlockSpec(shape, lambda b, i: (0, 0))
    rowblk = lambda w, off: pl.BlockSpec((c, w), lambda b, i: (b * nc + i, off // w))
    return pl.pallas_call(
        _ssd_prompt_kernel,
        out_shape=(jax.ShapeDtypeStruct((BATCH * SEQ, BRANCH_W), BF16),
                   jax.ShapeDtypeStruct((BATCH, SSD_HEADS, SSD_HEADDIM, SSD_STATE), F32),
                   jax.ShapeDtypeStruct((BATCH, CONV_W - 1, SSD_CONV_DIM), F32)),
        grid=(BATCH, nc),
        in_specs=[rowblk(BRANCH_W, OFF_SZ), rowblk(BRANCH_W, OFF_SX), rowblk(SSD_BC, OFF_SBC),
                  rowblk(LANES, OFF_SDT),
                  const((CONV_W, SSD_CONV_DIM)), const((1, SSD_CONV_DIM)), const((1, LANES)),
                  const((1, LANES)), const((1, BRANCH_W)), const((1, BRANCH_W)), const((c, c))],
        out_specs=(pl.BlockSpec((c, BRANCH_W), lambda b, i: (b * nc + i, 0)),
                   pl.BlockSpec((None, SSD_HEADS, SSD_HEADDIM, SSD_STATE), lambda b, i: (b, 0, 0, 0)),
                   pl.BlockSpec((None, CONV_W - 1, SSD_CONV_DIM), lambda b, i: (b, 0, 0))),
        scratch_shapes=[pltpu.VMEM((SUBLANES, BRANCH_W), F32), pltpu.VMEM((SUBLANES, SSD_BC), F32),
                        pltpu.VMEM((SSD_HEADS // 2, SSD_STATE, LANES), F32)],
        compiler_params=_cparams(2),
        name="ssd_prompt",
    )(proj, proj, proj, proj, *_ssd_params(p), tril)


def _ssd_sample_kernel(z_ref, x_ref, bc_ref, dt_ref, cw_ref, cb_ref, dtb_ref, alog_ref, dpar_ref, nw_ref,
                       bufx_ref, bufbc_ref, s_ref, out_ref, so_ref, nbx_ref, nbbc_ref, y_s):
    cw = cw_ref[...]
    cbias = cb_ref[...]

    def conv_silu(raw, buf_ref, nbuf_ref, lo, hi):
        y = cbias[:, lo:hi] + cw[CONV_W - 1:CONV_W, lo:hi] * raw
        for j in range(CONV_W - 1):
            y = y + cw[j:j + 1, lo:hi] * buf_ref[j]
        for j in range(CONV_W - 2):
            nbuf_ref[j] = buf_ref[j + 1]
        nbuf_ref[CONV_W - 2] = raw
        return jax.nn.silu(y)

    xs = conv_silu(x_ref[...], bufx_ref, nbx_ref, 0, BRANCH_W)
    bc = conv_silu(bc_ref[...], bufbc_ref, nbbc_ref, BRANCH_W, SSD_CONV_DIM)
    dt = jax.nn.softplus(dt_ref[...] + dtb_ref[...])
    decay = jnp.exp(dt * (-jnp.exp(alog_ref[...])))
    nb = xs.shape[0]
    hpg = SSD_HEADS // SSD_GROUPS
    lane_lo = lax.broadcasted_iota(jnp.int32, (nb, LANES), 1) < SSD_HEADDIM
    for pi in range(SSD_HEADS // 2):
        h0 = 2 * pi
        g = h0 // hpg
        xdt_cols = _rows_to_cols(xs[:, pi * LANES:(pi + 1) * LANES] * _head_pair(dt, h0, lane_lo))
        for e in range(2):
            h = h0 + e
            for j in range(nb):
                xcol = xdt_cols[e * SSD_HEADDIM:(e + 1) * SSD_HEADDIM, j:j + 1]
                brow = bc[j:j + 1, g * SSD_STATE:(g + 1) * SSD_STATE]
                so_ref[j, h] = s_ref[j, h] * decay[j:j + 1, h:h + 1] + xcol * brow
    for g in range(SSD_GROUPS):
        cm = bc[:, (SSD_GROUPS + g) * SSD_STATE:(SSD_GROUPS + g + 1) * SSD_STATE].astype(BF16)
        for j in range(nb):
            s_new = so_ref[j, g * hpg:(g + 1) * hpg].reshape(hpg * SSD_HEADDIM, SSD_STATE)
            y_s[j:j + 1, g * hpg * SSD_HEADDIM:(g + 1) * hpg * SSD_HEADDIM] = _nt_dot(cm, s_new.astype(BF16))[j:j + 1, :]
    y = (y_s[...] + dpar_ref[...] * xs) * jax.nn.silu(z_ref[...])
    gw = BRANCH_W // SSD_GROUPS
    nw = nw_ref[...]
    outs = [_rms_rows(y[:, g * gw:(g + 1) * gw]) * nw[:, g * gw:(g + 1) * gw] for g in range(SSD_GROUPS)]
    out_ref[...] = jnp.concatenate(outs, axis=1)


def ssd_sample(layer, proj, p, state, state_out, buf_t):
    nb = SSD_SAMPLE_BLK
    const = lambda shape: pl.BlockSpec(shape, lambda i: (0,) * len(shape))
    rowblk = lambda w, off: pl.BlockSpec((nb, w), lambda i: (i, off // w))
    st_spec = pl.BlockSpec((None, nb, SSD_HEADS, SSD_HEADDIM, SSD_STATE), lambda i: (layer, i, 0, 0, 0))
    bufx_spec = pl.BlockSpec((CONV_W - 1, nb, BRANCH_W), lambda i: (0, i, 0))
    bufbc_spec = pl.BlockSpec((CONV_W - 1, nb, SSD_BC), lambda i: (0, i, BRANCH_W // SSD_BC))
    args = [proj, proj, proj, proj, *_ssd_params(p), buf_t, buf_t, state]
    in_specs = [rowblk(BRANCH_W, OFF_SZ), rowblk(BRANCH_W, OFF_SX), rowblk(SSD_BC, OFF_SBC), rowblk(LANES, OFF_SDT),
                const((CONV_W, SSD_CONV_DIM)), const((1, SSD_CONV_DIM)), const((1, LANES)), const((1, LANES)),
                const((1, BRANCH_W)), const((1, BRANCH_W)), bufx_spec, bufbc_spec, st_spec]
    aliases = {}
    kern = _ssd_sample_kernel
    if state_out is not None:
        args.append(state_out)
        in_specs.append(pl.BlockSpec(memory_space=pl.ANY))
        aliases = {len(args) - 1: 1}
        kern = functools.partial(_drop_alias_arg, kern, 13)
    out, st, nbx, nbbc = pl.pallas_call(
        kern,
        out_shape=(jax.ShapeDtypeStruct((DEC_BATCH, BRANCH_W), F32),
                   jax.ShapeDtypeStruct(state.shape, F32),
                   jax.ShapeDtypeStruct((CONV_W - 1, DEC_BATCH, BRANCH_W), F32),
                   jax.ShapeDtypeStruct((CONV_W - 1, DEC_BATCH, SSD_BC), F32)),
        grid=(DEC_BATCH // nb,),
        in_specs=in_specs,
        out_specs=(pl.BlockSpec((nb, BRANCH_W), lambda i: (i, 0)), st_spec,
                   pl.BlockSpec((CONV_W - 1, nb, BRANCH_W), lambda i: (0, i, 0)),
                   pl.BlockSpec((CONV_W - 1, nb, SSD_BC), lambda i: (0, i, 0))),
        scratch_shapes=[pltpu.VMEM((nb, BRANCH_W), F32)],
        input_output_aliases=aliases,
        compiler_params=_cparams(1),
        name="ssd_sample",
    )(*args)
    return out, st, jnp.concatenate([nbx, nbbc], axis=-1)


def _ret_cols(a_ref, b_ref):
    a = a_ref[...]
    lane = lax.broadcasted_iota(jnp.int32, a.shape, 1)
    keep = LANES - RET_SHIFT
    return jnp.where(lane < keep, pltpu.roll(a, keep, axis=1), pltpu.roll(b_ref[...], keep, axis=1))


def _ret_prompt_kernel(qa_ref, qb_ref, ka_ref, kb_ref, va_ref, vb_ref, ga_ref, gb_ref, cos_ref, sin_ref, lg_ref,
                       out_ref, st_ref, q_s, k_s, v_s, o_s):
    c = RET_CHUNK
    t_len = qa_ref.shape[0]
    cos = cos_ref[...]
    sin = sin_ref[...]
    q_s[...] = _rope(_ret_cols(qa_ref, qb_ref), cos, sin)
    k_s[...] = _rope(_ret_cols(ka_ref, kb_ref), cos, sin) * RET_DK ** -0.5
    v_s[...] = _ret_cols(va_ref, vb_ref).astype(BF16)
    lg = lg_ref[...]
    lg128 = lg[:, :LANES]
    ti = lax.broadcasted_iota(jnp.int32, (c, c), 0)
    si = lax.broadcasted_iota(jnp.int32, (c, c), 1)
    tri = ti >= si
    dec = jnp.where(tri, jnp.exp(jnp.where(tri, (ti - si).astype(F32) * lg, 0.0)), 0.0)
    tt = lax.broadcasted_iota(jnp.int32, (c, LANES), 0).astype(F32)
    g_in = jnp.exp((tt + 1.0) * lg128)
    g_out = jnp.exp((c - 1.0 - tt) * lg128)
    g_all = jnp.exp(float(c) * lg128)
    s = jnp.zeros((RET_DK, RET_DK), F32)
    for ci in range(t_len // c):
        sl = slice(ci * c, (ci + 1) * c)
        qc = q_s[sl, :]
        kc = k_s[sl, :]
        vb = v_s[sl, :]
        scores = _nt_dot(qc.astype(BF16), kc.astype(BF16)) * dec
        o_s[sl, :] = (jnp.dot(scores.astype(BF16), vb, preferred_element_type=F32)
                      + jnp.dot((qc * g_in).astype(BF16), s.astype(BF16), preferred_element_type=F32))
        s = s * g_all + _tn_dot((kc * g_out).astype(BF16), vb)
    out_ref[...] = (_rms_rows(o_s[...]) * jax.nn.silu(_ret_cols(ga_ref, gb_ref))).astype(out_ref.dtype)
    st_ref[...] = s


def _log_gamma_rows(width):
    lg = jnp.log1p(-jnp.exp2(-5.0 - jnp.arange(RET_HEADS, dtype=F32)))
    return jnp.broadcast_to(lg[:, None, None], (RET_HEADS, 1, width))


def _ret_col_specs(rows, index):
    specs = []
    for off in (OFF_RQ, OFF_RK, OFF_RV, OFF_RG):
        for extra in (0, 1):
            specs.append(pl.BlockSpec((rows, RET_DK), functools.partial(
                lambda *g, base: (index(*g)[0], base + index(*g)[1]), base=off // RET_DK + extra)))
    return specs


def ret_prompt(proj, cos, sin):
    tab = pl.BlockSpec((SEQ, RET_DK), lambda b, h: (0, 0))
    return pl.pallas_call(
        _ret_prompt_kernel,
        out_shape=(jax.ShapeDtypeStruct((BATCH * SEQ, BRANCH_W), BF16),
                   jax.ShapeDtypeStruct((BATCH, RET_HEADS, RET_DK, RET_DK), F32)),
        grid=(BATCH, RET_HEADS),
        in_specs=_ret_col_specs(SEQ, lambda b, h: (b, h))
                 + [tab, tab, pl.BlockSpec((None, 1, RET_CHUNK), lambda b, h: (h, 0, 0))],
        out_specs=(pl.BlockSpec((SEQ, RET_DK), lambda b, h: (b, h)),
                   pl.BlockSpec((None, None, RET_DK, RET_DK), lambda b, h: (b, h, 0, 0))),
        scratch_shapes=[pltpu.VMEM((SEQ, RET_DK), F32), pltpu.VMEM((SEQ, RET_DK), F32),
                        pltpu.VMEM((SEQ, RET_DK), BF16), pltpu.VMEM((SEQ, RET_DK), F32)],
        compiler_params=_cparams(2),
        name="ret_prompt",
    )(*([proj] * 8), cos, sin, _log_gamma_rows(RET_CHUNK))


def _ret_sample_kernel(qa_ref, qb_ref, ka_ref, kb_ref, va_ref, vb_ref, ga_ref, gb_ref, cos_ref, sin_ref, lg_ref,
                       s_ref, out_ref, so_ref, o_s):
    cos = cos_ref[0:1, :]
    sin = sin_ref[0:1, :]
    q = _rope(_ret_cols(qa_ref, qb_ref), cos, sin)
    k = _rope(_ret_cols(ka_ref, kb_ref), cos, sin) * RET_DK ** -0.5
    v = _ret_cols(va_ref, vb_ref)
    gamma = jnp.exp(lg_ref[...])
    qc, kc = _rows_to_cols(q), _rows_to_cols(k)
    nb = q.shape[0]
    for j in range(nb):
        s_new = s_ref[j] * gamma + kc[:, j:j + 1] * v[j:j + 1, :]
        so_ref[j] = s_new
        o_s[j:j + 1, :] = jnp.sum(s_new * qc[:, j:j + 1], axis=0, keepdims=True)
    out_ref[...] = _rms_rows(o_s[...]) * jax.nn.silu(_ret_cols(ga_ref, gb_ref))


def ret_sample(layer, proj, cos, sin, state, state_out):
    nb = SAMPLE_BLK
    tab = pl.BlockSpec((SUBLANES, RET_DK), lambda h, i: (0, 0))
    st_spec = pl.BlockSpec((None, nb, None, RET_DK, RET_DK), lambda h, i: (layer, i, h, 0, 0))
    args = [proj] * 8 + [cos, sin, _log_gamma_rows(LANES), state]
    in_specs = (_ret_col_specs(nb, lambda h, i: (i, h))
                + [tab, tab, pl.BlockSpec((None, 1, LANES), lambda h, i: (h, 0, 0)), st_spec])
    aliases = {}
    kern = _ret_sample_kernel
    if state_out is not None:
        args.append(state_out)
        in_specs.append(pl.BlockSpec(memory_space=pl.ANY))
        aliases = {len(args) - 1: 1}
        kern = functools.partial(_drop_alias_arg, kern, len(args) - 1)
    return pl.pallas_call(
        kern,
        out_shape=(jax.ShapeDtypeStruct((DEC_BATCH, BRANCH_W), F32),
                   jax.ShapeDtypeStruct(state.shape, F32)),
        grid=(RET_HEADS, DEC_BATCH // nb),
        in_specs=in_specs,
        out_specs=(pl.BlockSpec((nb, RET_DK), lambda h, i: (i, h)), st_spec),
        scratch_shapes=[pltpu.VMEM((nb, RET_DK), F32)],
        input_output_aliases=aliases,
        compiler_params=_cparams(2),
        name="ret_sample",
    )(*args)


def _w_in_tail(w):
    tail = w[..., N_IN_MAIN:].astype(BF16)
    return jnp.pad(tail, ((0, 0), (0, 0), (0, LANES - (N_IN - N_IN_MAIN))))


def kernel(x_prompt, x_sample, state_lru_h, state_lru_conv, state_hgrn, state_ssd, state_ssd_conv, state_ret, state_ffn_conv, g_mix, g_ffn, w_in, lru_conv_w, lru_conv_b, lru_wa, lru_ba, lru_wx, lru_bx, lru_lambda, hg_lb_logits, hg_norm_w, ssd_conv_w, ssd_conv_b, ssd_dt_bias, ssd_a_log, ssd_d, ssd_norm_w, w_branch, w_gate, w_out, ffn_w_up, ffn_w_val, ffn_conv_w, ffn_conv_b, ffn_w_down, g_final):
    xp = x_prompt.reshape(BATCH * SEQ, D_MODEL)
    xs = x_sample.reshape(DEC_BATCH, D_MODEL)

    cos_p, sin_p = rope_tables(SEQ, 0, True)
    cos_s, sin_s = rope_tables(SUBLANES, PAST_LEN, False)

    hp = rmsnorm(xp, g_mix[0], BF16, 512)
    hs = rmsnorm(xs, g_mix[0], BF16, DEC_BATCH)

    state_ssd_t = jnp.swapaxes(state_ssd, -1, -2)

    w_in_tail_b = _w_in_tail(w_in)
    w_out_b = w_out.astype(BF16)
    w_down_b = ffn_w_down.astype(BF16)

    prompt_states, sample_small = [], []
    hg_out = ssd_out = ret_out = None
    for l in range(DEPTH):
        p = {"lru_conv_w": lru_conv_w[l], "lru_conv_b": lru_conv_b[l], "lru_wa": lru_wa[l], "lru_ba": lru_ba[l],
             "lru_wx": lru_wx[l], "lru_bx": lru_bx[l], "lru_lambda": lru_lambda[l],
             "hg_lb_logits": hg_lb_logits, "hg_norm_w": hg_norm_w[l],
             "ssd_conv_w": ssd_conv_w[l], "ssd_conv_b": ssd_conv_b[l], "ssd_dt_bias": ssd_dt_bias[l],
             "ssd_a_log": ssd_a_log[l], "ssd_d": ssd_d[l], "ssd_norm_w": ssd_norm_w[l]}
        last = l == DEPTH - 1
        g_next = g_final if last else g_mix[l + 1]

        proj_p, w_in_b = in_proj_prompt(l, hp, w_in, 512)
        proj_p = in_proj_tail(l, hp, w_in_tail_b, proj_p, 1024)
        out_a, lru_h_p, lru_conv_p = lru_prompt(proj_p, p)
        out_b, hg_p = hgrn_prompt(l, proj_p, p)
        out_c, ssd_p, ssd_conv_p = ssd_prompt(proj_p, p)
        out_d, ret_p = ret_prompt(proj_p, cos_p, sin_p)
        merged, w_gate_b, w_branch_b = gated_merge_prompt(l, hp, (out_a, out_b, out_c, out_d), w_gate, w_branch,
                                                          512, 256)
        xp, h2 = out_proj_residual_norm(l, merged, w_out_b, xp, g_ffn[l], 512)
        act, ffn_conv_p, w_up_b, w_val_b = ffn_prompt(l, h2, ffn_w_up, ffn_w_val, ffn_conv_w[l], ffn_conv_b[l], 256)
        res = down_proj_residual_norm(l, act, w_down_b, xp, g_next, 512, 512, not last, F32 if last else BF16)
        if last:
            (yp,) = res
        else:
            xp, hp = res
        prompt_states.append((lru_h_p, lru_conv_p, hg_p, ssd_p, ssd_conv_p, ret_p, ffn_conv_p))

        proj_s = in_proj_tail(l, hs, w_in_tail_b, in_proj_sample(hs, w_in_b), DEC_BATCH)
        lru_buf_t = jnp.swapaxes(state_lru_conv[l], 0, 1)
        ssd_buf_t = jnp.swapaxes(state_ssd_conv[l], 0, 1)
        s_a, lru_h_s, lru_nbuf = lru_sample(proj_s, p, state_lru_h[l], lru_buf_t)
        s_b, hg_out = hgrn_sample(l, proj_s, p, state_hgrn, hg_out)
        s_c, ssd_out, ssd_nbuf = ssd_sample(l, proj_s, p, state_ssd_t, ssd_out, ssd_buf_t)
        s_d, ret_out = ret_sample(l, proj_s, cos_s, sin_s, state_ret, ret_out)
        merged_s = gated_merge_sample(hs, (s_a, s_b, s_c, s_d), w_gate_b, w_branch_b, 256)
        xs, h2s = out_proj_residual_norm(l, merged_s, w_out_b, xs, g_ffn[l], DEC_BATCH)
        act_s, ffn_nbuf = ffn_sample(l, h2s, w_up_b, w_val_b, ffn_conv_w[l], ffn_conv_b[l], state_ffn_conv, 512)
        res = down_proj_residual_norm(l, act_s, w_down_b, xs, g_next, DEC_BATCH, 512, not last, F32 if last else BF16)
        if last:
            (ys,) = res
        else:
            xs, hs = res
        sample_small.append((lru_h_s, jnp.swapaxes(lru_nbuf, 0, 1), jnp.swapaxes(ssd_nbuf, 0, 1), ffn_nbuf))

    stack_p = lambda i: jnp.stack([st[i] for st in prompt_states], axis=0)
    stack_s = lambda i: jnp.stack([st[i] for st in sample_small], axis=0)
    return (yp.reshape(BATCH, SEQ, D_MODEL), ys.reshape(DEC_BATCH, 1, D_MODEL),
            stack_p(0), stack_s(0), stack_p(1), stack_s(1),
            stack_p(2), hg_out, jnp.swapaxes(stack_p(3), -1, -2), jnp.swapaxes(ssd_out, -1, -2),
            stack_p(4), stack_s(2), stack_p(5), ret_out,
            stack_p(6), stack_s(3))
```

```python
import functools
import math

import numpy as np
import jax
import jax.numpy as jnp
from jax import lax
from jax.experimental import pallas as pl
from jax.experimental.pallas import tpu as pltpu

F32 = jnp.float32
BF16 = jnp.bfloat16
HIGHEST = lax.Precision.HIGHEST

D_MODEL = 2048
BATCH = 4
SEQ = 2048
DEPTH = 2
DEC_BATCH = 128
PAST_LEN = 16384
BRANCH_W = D_MODEL // 2
EPS = 1e-6
LRU_BLOCKS = 8
LRU_BLOCK = BRANCH_W // LRU_BLOCKS
LRU_C = 8.0
CONV_W = 4
HG_HEADS = 8
HG_DK = BRANCH_W // HG_HEADS
SSD_HEADDIM = 64
SSD_HEADS = BRANCH_W // SSD_HEADDIM
SSD_GROUPS = 2
SSD_STATE = 128
SSD_BC = 2 * SSD_GROUPS * SSD_STATE
SSD_CONV_DIM = BRANCH_W + SSD_BC
RET_HEADS = 8
RET_DK = BRANCH_W // RET_HEADS
ROPE_BASE = 10000.0
D_FF = 5632
FFN_CONV_W = 3

V7X_VMEM_BYTES = 64 * 1024 * 1024
VMEM_LIMIT_BYTES = V7X_VMEM_BYTES - 8 * 1024 * 1024
LANES = 128
SUBLANES = 8

N_BRANCH = 4
N_IN = 12816
OFF_XA, OFF_YA = 0, 1024
OFF_HQ, OFF_HF, OFF_HI, OFF_HG = 2048, 3072, 4096, 5120
OFF_SZ, OFF_SX, OFF_SBC = 6144, 7168, 8192
OFF_SDT = 8704
RET_SHIFT = SSD_HEADS
OFF_RQ, OFF_RK, OFF_RV, OFF_RG = 8704, 9728, 10752, 11776
N_IN_MAIN = 12800
N_PROJ = N_IN_MAIN + LANES
IN_PROJ_TN = 1280
FFN_ROW_CHUNK = 256
DOWN_PROJ_TK = 1408

HG_CHUNK = 128
HG_LEVELS = (8, 16, 32, 64)
SSD_CHUNK = 128
RET_CHUNK = 256
SAMPLE_BLK = 32
SSD_SAMPLE_BLK = 8


def _cparams(n_axes):
    return pltpu.CompilerParams(dimension_semantics=("arbitrary",) * n_axes,
                                vmem_limit_bytes=VMEM_LIMIT_BYTES)


def _rms_rows(x):
    return x * lax.rsqrt(jnp.mean(x * x, axis=-1, keepdims=True) + EPS)


def _shift_rows(x, d, row):
    return jnp.where(row >= d, pltpu.roll(x, d, axis=0), 0.0)


def _nt_dot(a, b):
    return lax.dot_general(a, b, (((1,), (1,)), ((), ())), preferred_element_type=F32)


def _tn_dot(a, b):
    return lax.dot_general(a, b, (((0,), (0,)), ((), ())), preferred_element_type=F32)


def _rows_to_cols(x):
    n = x.shape[0]
    if n < LANES:
        x = jnp.concatenate([x, jnp.zeros((LANES - n, x.shape[1]), x.dtype)], axis=0)
    return x.T


def _norm_kernel(x_ref, g_ref, o_ref):
    o_ref[...] = (_rms_rows(x_ref[...]) * g_ref[...]).astype(o_ref.dtype)


def rmsnorm(x, g, out_dtype, tm):
    m, d = x.shape
    return pl.pallas_call(
        _norm_kernel,
        out_shape=jax.ShapeDtypeStruct((m, d), out_dtype),
        grid=(m // tm,),
        in_specs=[pl.BlockSpec((tm, d), lambda i: (i, 0)),
                  pl.BlockSpec((1, d), lambda i: (0, 0))],
        out_specs=pl.BlockSpec((tm, d), lambda i: (i, 0)),
        compiler_params=_cparams(1),
        name="rmsnorm",
    )(x, g.reshape(1, d))


def _mm_nt_kernel(a_ref, bt_ref, o_ref):
    o_ref[...] = _nt_dot(a_ref[...], bt_ref[...])


def _mm_nt_cast_kernel(a_ref, wt_ref, o_ref, wb_ref):
    @pl.when(pl.program_id(1) == 0)
    def _():
        wb_ref[...] = wt_ref[...].astype(BF16)

    o_ref[...] = _nt_dot(a_ref[...], wb_ref[...])


def in_proj_prompt(layer, h, w_in_t, tm):
    m = h.shape[0]
    return pl.pallas_call(
        _mm_nt_cast_kernel,
        out_shape=(jax.ShapeDtypeStruct((m, N_PROJ), F32), jax.ShapeDtypeStruct((N_IN_MAIN, D_MODEL), BF16)),
        grid=(N_IN_MAIN // IN_PROJ_TN, m // tm),
        in_specs=[pl.BlockSpec((tm, D_MODEL), lambda j, i: (i, 0)),
                  pl.BlockSpec((None, IN_PROJ_TN, D_MODEL), lambda j, i: (layer, j, 0))],
        out_specs=(pl.BlockSpec((tm, IN_PROJ_TN), lambda j, i: (i, j)),
                   pl.BlockSpec((IN_PROJ_TN, D_MODEL), lambda j, i: (j, 0))),
        compiler_params=_cparams(2),
        name="in_proj_prompt",
    )(h, w_in_t)


def in_proj_sample(h, w_main_bt):
    m = h.shape[0]
    return pl.pallas_call(
        _mm_nt_kernel,
        out_shape=jax.ShapeDtypeStruct((m, N_PROJ), F32),
        grid=(N_IN_MAIN // IN_PROJ_TN,),
        in_specs=[pl.BlockSpec((m, D_MODEL), lambda j: (0, 0)),
                  pl.BlockSpec((IN_PROJ_TN, D_MODEL), lambda j: (j, 0))],
        out_specs=pl.BlockSpec((m, IN_PROJ_TN), lambda j: (0, j)),
        compiler_params=_cparams(1),
        name="in_proj_sample",
    )(h, w_main_bt)


def _mm_tail_kernel(a_ref, bt_ref, proj_hbm_ref, o_ref):
    del proj_hbm_ref
    o_ref[...] = _nt_dot(a_ref[...], bt_ref[...])


def in_proj_tail(layer, h, w_tail_t, proj, tm):
    m = h.shape[0]
    return pl.pallas_call(
        _mm_tail_kernel,
        out_shape=jax.ShapeDtypeStruct(proj.shape, F32),
        grid=(m // tm,),
        in_specs=[pl.BlockSpec((tm, D_MODEL), lambda i: (i, 0)),
                  pl.BlockSpec((None, LANES, D_MODEL), lambda i: (layer, 0, 0)),
                  pl.BlockSpec(memory_space=pl.ANY)],
        out_specs=pl.BlockSpec((tm, LANES), lambda i: (i, N_IN_MAIN // LANES)),
        input_output_aliases={2: 0},
        compiler_params=_cparams(1),
        name="in_proj_tail",
    )(h, w_tail_t, proj)


def _gated_sum(h, br_refs, gate_w, branch_w):
    acc = None
    for k, br_ref in enumerate(br_refs):
        gate = jax.nn.sigmoid(jnp.dot(h, gate_w(k), preferred_element_type=F32))
        br = jnp.dot(br_ref[...].astype(BF16), branch_w(k), preferred_element_type=F32)
        acc = gate * br if acc is None else acc + gate * br
    return acc


def _merge_cast_kernel(h_ref, a_ref, b_ref, c_ref, d_ref, wg_ref, wb_ref, o_ref, wbb_ref):
    @pl.when(pl.program_id(1) == 0)
    def _():
        for k in range(N_BRANCH):
            wbb_ref[k] = wb_ref[k].astype(BF16)

    acc = _gated_sum(h_ref[...], (a_ref, b_ref, c_ref, d_ref), lambda k: wg_ref[k], lambda k: wbb_ref[k])
    o_ref[...] = acc.astype(o_ref.dtype)


def gated_merge_prompt(layer, h, branches, w_gate_b, w_branch, tm, tn):
    m = h.shape[0]
    br_specs = [pl.BlockSpec((tm, BRANCH_W), lambda j, i: (i, 0)) for _ in range(N_BRANCH)]
    return pl.pallas_call(
        _merge_cast_kernel,
        out_shape=(jax.ShapeDtypeStruct((m, D_MODEL), BF16),
                   jax.ShapeDtypeStruct((N_BRANCH, BRANCH_W, D_MODEL), BF16)),
        grid=(D_MODEL // tn, m // tm),
        in_specs=[pl.BlockSpec((tm, D_MODEL), lambda j, i: (i, 0))] + br_specs
                 + [pl.BlockSpec((None, N_BRANCH, D_MODEL, tn), lambda j, i: (layer, 0, 0, j)),
                    pl.BlockSpec((None, N_BRANCH, BRANCH_W, tn), lambda j, i: (layer, 0, 0, j))],
        out_specs=(pl.BlockSpec((tm, tn), lambda j, i: (i, j)),
                   pl.BlockSpec((N_BRANCH, BRANCH_W, tn), lambda j, i: (0, 0, j))),
        compiler_params=_cparams(2),
        name="gated_merge_prompt",
    )(h, *branches, w_gate_b, w_branch)


def _merge_kernel(h_ref, a_ref, b_ref, c_ref, d_ref, wgb_ref, wbb_ref, o_ref):
    acc = _gated_sum(h_ref[...], (a_ref, b_ref, c_ref, d_ref), lambda k: wgb_ref[k], lambda k: wbb_ref[k])
    o_ref[...] = acc.astype(o_ref.dtype)


def gated_merge_sample(layer, h, branches, w_gate_b, w_branch_b, tn):
    m = h.shape[0]
    br_specs = [pl.BlockSpec((m, BRANCH_W), lambda j: (0, 0)) for _ in range(N_BRANCH)]
    return pl.pallas_call(
        _merge_kernel,
        out_shape=jax.ShapeDtypeStruct((m, D_MODEL), BF16),
        grid=(D_MODEL // tn,),
        in_specs=[pl.BlockSpec((m, D_MODEL), lambda j: (0, 0))] + br_specs
                 + [pl.BlockSpec((None, N_BRANCH, D_MODEL, tn), lambda j: (layer, 0, 0, j)),
                    pl.BlockSpec((N_BRANCH, BRANCH_W, tn), lambda j: (0, 0, j))],
        out_specs=pl.BlockSpec((m, tn), lambda j: (0, j)),
        compiler_params=_cparams(1),
        name="gated_merge_sample",
    )(h, *branches, w_gate_b, w_branch_b)


def _out_proj_kernel(m_ref, w_ref, x_ref, g_ref, xo_ref, ho_ref):
    x_new = x_ref[...] + jnp.dot(m_ref[...], w_ref[...], preferred_element_type=F32)
    xo_ref[...] = x_new
    ho_ref[...] = (_rms_rows(x_new) * g_ref[...]).astype(ho_ref.dtype)


def out_proj_residual_norm(layer, merged, w_out, x, g, tm):
    m = x.shape[0]
    return pl.pallas_call(
        _out_proj_kernel,
        out_shape=(jax.ShapeDtypeStruct((m, D_MODEL), F32), jax.ShapeDtypeStruct((m, D_MODEL), BF16)),
        grid=(m // tm,),
        in_specs=[pl.BlockSpec((tm, D_MODEL), lambda i: (i, 0)),
                  pl.BlockSpec((None, D_MODEL, D_MODEL), lambda i: (layer, 0, 0)),
                  pl.BlockSpec((tm, D_MODEL), lambda i: (i, 0)),
                  pl.BlockSpec((1, D_MODEL), lambda i: (0, 0))],
        out_specs=(pl.BlockSpec((tm, D_MODEL), lambda i: (i, 0)),
                   pl.BlockSpec((tm, D_MODEL), lambda i: (i, 0))),
        compiler_params=_cparams(1),
        name="out_proj",
    )(merged, w_out, x, g.reshape(1, D_MODEL))


def _down_proj_kernel(emit_x, a_ref, w_ref, x_ref, g_ref, *refs):
    if emit_x:
        xo_ref, no_ref, acc_ref = refs
    else:
        no_ref, acc_ref = refs
    kk = pl.program_id(1)

    @pl.when(kk == 0)
    def _():
        acc_ref[...] = x_ref[...]

    acc_ref[...] += jnp.dot(a_ref[...], w_ref[...], preferred_element_type=F32)

    @pl.when(kk == pl.num_programs(1) - 1)
    def _():
        x_new = acc_ref[...]
        if emit_x:
            xo_ref[...] = x_new
        no_ref[...] = (_rms_rows(x_new) * g_ref[...]).astype(no_ref.dtype)


def down_proj_residual_norm(layer, a, w_down, x, g, tm, tk, emit_x, norm_dtype):
    m = x.shape[0]
    out_shape = [jax.ShapeDtypeStruct((m, D_MODEL), norm_dtype)]
    out_specs = [pl.BlockSpec((tm, D_MODEL), lambda i, k: (i, 0))]
    if emit_x:
        out_shape = [jax.ShapeDtypeStruct((m, D_MODEL), F32)] + out_shape
        out_specs = [pl.BlockSpec((tm, D_MODEL), lambda i, k: (i, 0))] + out_specs
    return pl.pallas_call(
        functools.partial(_down_proj_kernel, emit_x),
        out_shape=tuple(out_shape),
        grid=(m // tm, D_FF // tk),
        in_specs=[pl.BlockSpec((tm, tk), lambda i, k: (i, k)),
                  pl.BlockSpec((None, tk, D_MODEL), lambda i, k: (layer, k, 0)),
                  pl.BlockSpec((tm, D_MODEL), lambda i, k: (i, 0)),
                  pl.BlockSpec((1, D_MODEL), lambda i, k: (0, 0))],
        out_specs=tuple(out_specs),
        scratch_shapes=[pltpu.VMEM((tm, D_MODEL), F32)],
        compiler_params=_cparams(2),
        name="down_proj",
    )(a, w_down, x, g.reshape(1, D_MODEL))


def _ffn_prompt_kernel(h_ref, wu_ref, wv_ref, cw_ref, cb_ref, a_ref, st_ref, wub_ref, wvb_ref):
    @pl.when(pl.program_id(1) == 0)
    def _():
        wub_ref[...] = wu_ref[...].astype(BF16)
        wvb_ref[...] = wv_ref[...].astype(BF16)

    rc = FFN_ROW_CHUNK
    tn = a_ref.shape[1]
    cw = cw_ref[...]
    cbias = cb_ref[...]
    row8 = lax.broadcasted_iota(jnp.int32, (SUBLANES, tn), 0)
    tail = jnp.zeros((SUBLANES, tn), F32)
    def up_val(c):
        hc = h_ref[c * rc:(c + 1) * rc, :]
        return (jnp.dot(hc, wub_ref[...], preferred_element_type=F32),
                jnp.dot(hc, wvb_ref[...], preferred_element_type=F32))

    n_chunks = h_ref.shape[0] // rc
    nxt = up_val(0)
    for c in range(n_chunks):
        u, v = nxt
        if c + 1 < n_chunks:
            nxt = up_val(c + 1)
        uc = cbias + cw[FFN_CONV_W - 1:FFN_CONV_W] * u
        for d in range(1, FFN_CONV_W):
            rolled = pltpu.roll(u, d, axis=0)
            top = jnp.where(row8 >= d, rolled[:SUBLANES], pltpu.roll(tail, d, axis=0))
            uc = uc + cw[FFN_CONV_W - 1 - d:FFN_CONV_W - d] * jnp.concatenate([top, rolled[SUBLANES:]], axis=0)
        a_ref[c * rc:(c + 1) * rc, :] = (jax.nn.gelu(uc) * v).astype(a_ref.dtype)
        tail = u[rc - SUBLANES:, :]
    st_ref[...] = tail[SUBLANES - (FFN_CONV_W - 1):, :]


def ffn_prompt(layer, h2, w_up, w_val, conv_w, conv_b, tn):
    wspec = pl.BlockSpec((None, D_MODEL, tn), lambda j, b: (layer, 0, j))
    wbspec = pl.BlockSpec((D_MODEL, tn), lambda j, b: (0, j))
    return pl.pallas_call(
        _ffn_prompt_kernel,
        out_shape=(jax.ShapeDtypeStruct((BATCH * SEQ, D_FF), BF16),
                   jax.ShapeDtypeStruct((BATCH, FFN_CONV_W - 1, D_FF), F32),
                   jax.ShapeDtypeStruct((D_MODEL, D_FF), BF16),
                   jax.ShapeDtypeStruct((D_MODEL, D_FF), BF16)),
        grid=(D_FF // tn, BATCH),
        in_specs=[pl.BlockSpec((SEQ, D_MODEL), lambda j, b: (b, 0)), wspec, wspec,
                  pl.BlockSpec((FFN_CONV_W, tn), lambda j, b: (0, j)),
                  pl.BlockSpec((1, tn), lambda j, b: (0, j))],
        out_specs=(pl.BlockSpec((SEQ, tn), lambda j, b: (b, j)),
                   pl.BlockSpec((None, FFN_CONV_W - 1, tn), lambda j, b: (b, 0, j)),
                   wbspec, wbspec),
        compiler_params=_cparams(2),
        name="ffn_prompt",
    )(h2, w_up, w_val, conv_w, conv_b.reshape(1, D_FF))


def _ffn_sample_kernel(h_ref, wu_ref, wv_ref, cw_ref, cb_ref, buf_ref, a_ref, nb_ref):
    h = h_ref[...]
    u = jnp.dot(h, wu_ref[...], preferred_element_type=F32)
    v = jnp.dot(h, wv_ref[...], preferred_element_type=F32)
    cw = cw_ref[...]
    b0 = buf_ref[:, 0, :]
    b1 = buf_ref[:, 1, :]
    uc = cb_ref[...] + cw[0:1] * b0 + cw[1:2] * b1 + cw[2:3] * u
    a_ref[...] = (jax.nn.gelu(uc) * v).astype(a_ref.dtype)
    nb_ref[:, 0, :] = b1
    nb_ref[:, 1, :] = u


def ffn_sample(layer, h2, w_up, w_val, conv_w, conv_b, buf, tn):
    return pl.pallas_call(
        _ffn_sample_kernel,
        out_shape=(jax.ShapeDtypeStruct((DEC_BATCH, D_FF), BF16),
                   jax.ShapeDtypeStruct((DEC_BATCH, FFN_CONV_W - 1, D_FF), F32)),
        grid=(D_FF // tn,),
        in_specs=[pl.BlockSpec((DEC_BATCH, D_MODEL), lambda j: (0, 0)),
                  pl.BlockSpec((D_MODEL, tn), lambda j: (0, j)),
                  pl.BlockSpec((D_MODEL, tn), lambda j: (0, j)),
                  pl.BlockSpec((FFN_CONV_W, tn), lambda j: (0, j)),
                  pl.BlockSpec((1, tn), lambda j: (0, j)),
                  pl.BlockSpec((None, DEC_BATCH, FFN_CONV_W - 1, tn), lambda j: (layer, 0, 0, j))],
        out_specs=(pl.BlockSpec((DEC_BATCH, tn), lambda j: (0, j)),
                   pl.BlockSpec((DEC_BATCH, FFN_CONV_W - 1, tn), lambda j: (0, 0, j))),
        compiler_params=_cparams(1),
        name="ffn_sample",
    )(h2, w_up, w_val, conv_w, conv_b.reshape(1, D_FF), buf)


def _rope_table_kernel(start, consecutive, freq_ref, sign_ref, cos_ref, sin_ref):
    shape = cos_ref.shape
    if consecutive:
        pos = lax.broadcasted_iota(jnp.int32, shape, 0).astype(F32) + float(start)
    else:
        pos = jnp.full(shape, float(start), F32)
    ang = pos * freq_ref[...]
    cos_ref[...] = jnp.cos(ang)
    sin_ref[...] = sign_ref[...] * jnp.sin(ang)


def rope_tables(n_rows, start, consecutive):
    half = RET_DK // 2
    freqs = ROPE_BASE ** (-jnp.arange(half, dtype=F32) / half)
    freq2 = jnp.concatenate([freqs, freqs]).reshape(1, RET_DK)
    sign = jnp.concatenate([-jnp.ones((half,), F32), jnp.ones((half,), F32)]).reshape(1, RET_DK)
    return pl.pallas_call(
        functools.partial(_rope_table_kernel, start, consecutive),
        out_shape=(jax.ShapeDtypeStruct((n_rows, RET_DK), F32), jax.ShapeDtypeStruct((n_rows, RET_DK), F32)),
        name="rope_tables",
    )(freq2, sign)


def _rope(x, cos, sin_signed):
    return x * cos + pltpu.roll(x, RET_DK // 2, axis=1) * sin_signed


def _lru_gates(conv, wa_ref, ba_ref, wx_ref, bx_ref, lam_ref):
    xb = conv.astype(BF16)
    r = jax.nn.sigmoid(jnp.dot(xb, wa_ref[...].astype(BF16), preferred_element_type=F32) + ba_ref[...])
    i = jax.nn.sigmoid(jnp.dot(xb, wx_ref[...].astype(BF16), preferred_element_type=F32) + bx_ref[...])
    log_a = -LRU_C * r * jax.nn.softplus(-lam_ref[...])
    a = jnp.exp(log_a)
    u = jnp.sqrt(1.0 - a * a) * (i * conv)
    return a, u


def _lru_prompt_kernel(xa_ref, ya_ref, cw_ref, cb_ref, wa_ref, ba_ref, wx_ref, bx_ref, lam_ref,
                       out_ref, h_ref, conv_ref, ag_s, ug_s):
    x = xa_ref[...]
    t_len = x.shape[0]
    row = lax.broadcasted_iota(jnp.int32, x.shape, 0)
    cw = cw_ref[...]
    conv = cb_ref[...] + cw[CONV_W - 1:CONV_W] * x
    for d in range(1, CONV_W):
        conv = conv + cw[CONV_W - 1 - d:CONV_W - d] * _shift_rows(x, d, row)
    a, u = _lru_gates(conv, wa_ref, ba_ref, wx_ref, bx_ref, lam_ref)
    ng = t_len // SUBLANES
    a3 = a.reshape(ng, SUBLANES, LRU_BLOCK)
    u3 = u.reshape(ng, SUBLANES, LRU_BLOCK)
    sub = lax.broadcasted_iota(jnp.int32, a3.shape, 1)
    d = 1
    while d < SUBLANES:
        keep = sub >= d
        u3 = jnp.where(keep, a3 * pltpu.roll(u3, d, axis=1) + u3, u3)
        a3 = jnp.where(keep, a3 * pltpu.roll(a3, d, axis=1), a3)
        d *= 2
    ag_s[...] = a3.reshape(t_len, LRU_BLOCK)
    ug_s[...] = u3.reshape(t_len, LRU_BLOCK)
    ag = ag_s[pl.ds(SUBLANES - 1, ng, stride=SUBLANES), :]
    ug = ug_s[pl.ds(SUBLANES - 1, ng, stride=SUBLANES), :]
    grow = lax.broadcasted_iota(jnp.int32, ag.shape, 0)
    d = 1
    while d < ng:
        keep = grow >= d
        ug = jnp.where(keep, ag * pltpu.roll(ug, d, axis=0) + ug, ug)
        ag = jnp.where(keep, ag * pltpu.roll(ag, d, axis=0), ag)
        d *= 2
    carry = _shift_rows(ug, 1, grow)
    h3 = a3 * jnp.broadcast_to(carry[:, None, :], a3.shape) + u3
    hs = h3.reshape(t_len, LRU_BLOCK)
    out_ref[...] = (hs * jax.nn.gelu(ya_ref[...])).astype(out_ref.dtype)
    h_ref[...] = ug[ng - 1:, :]
    conv_ref[...] = xa_ref[pl.ds(t_len - (CONV_W - 1), CONV_W - 1), :]


def _lru_param_specs(n_axes_fn):
    blk3 = lambda shape: pl.BlockSpec(shape, n_axes_fn(lambda n: (n, 0, 0)))
    return [pl.BlockSpec((CONV_W, LRU_BLOCK), n_axes_fn(lambda n: (0, n))),
            pl.BlockSpec((1, LRU_BLOCK), n_axes_fn(lambda n: (0, n))),
            blk3((None, LRU_BLOCK, LRU_BLOCK)), blk3((None, 1, LRU_BLOCK)),
            blk3((None, LRU_BLOCK, LRU_BLOCK)), blk3((None, 1, LRU_BLOCK)),
            blk3((None, 1, LRU_BLOCK))]


def _lru_params(p):
    return (p["lru_conv_w"], p["lru_conv_b"].reshape(1, BRANCH_W),
            p["lru_wa"], p["lru_ba"].reshape(LRU_BLOCKS, 1, LRU_BLOCK),
            p["lru_wx"], p["lru_bx"].reshape(LRU_BLOCKS, 1, LRU_BLOCK),
            p["lru_lambda"].reshape(LRU_BLOCKS, 1, LRU_BLOCK))


def lru_prompt(proj, p):
    cb = lambda off: off // LRU_BLOCK
    wrap = lambda f: (lambda b, n: f(n))
    out, h, conv = pl.pallas_call(
        _lru_prompt_kernel,
        out_shape=(jax.ShapeDtypeStruct((BATCH * SEQ, BRANCH_W), BF16),
                   jax.ShapeDtypeStruct((BATCH, 1, BRANCH_W), F32),
                   jax.ShapeDtypeStruct((BATCH, CONV_W - 1, BRANCH_W), F32)),
        grid=(BATCH, LRU_BLOCKS),
        in_specs=[pl.BlockSpec((SEQ, LRU_BLOCK), lambda b, n: (b, cb(OFF_XA) + n)),
                  pl.BlockSpec((SEQ, LRU_BLOCK), lambda b, n: (b, cb(OFF_YA) + n))] + _lru_param_specs(wrap),
        out_specs=(pl.BlockSpec((SEQ, LRU_BLOCK), lambda b, n: (b, n)),
                   pl.BlockSpec((None, 1, LRU_BLOCK), lambda b, n: (b, 0, n)),
                   pl.BlockSpec((None, CONV_W - 1, LRU_BLOCK), lambda b, n: (b, 0, n))),
        scratch_shapes=[pltpu.VMEM((SEQ, LRU_BLOCK), F32)] * 2,
        compiler_params=_cparams(2),
        name="lru_prompt",
    )(proj, proj, *_lru_params(p))
    return out, h.reshape(BATCH, BRANCH_W), conv


def _lru_sample_kernel(xa_ref, ya_ref, cw_ref, cb_ref, wa_ref, ba_ref, wx_ref, bx_ref, lam_ref,
                       h0_ref, buf_ref, out_ref, h_ref, nbuf_ref):
    x = xa_ref[...]
    cw = cw_ref[...]
    conv = cb_ref[...] + cw[CONV_W - 1:CONV_W] * x
    for j in range(CONV_W - 1):
        conv = conv + cw[j:j + 1] * buf_ref[j]
    a, u = _lru_gates(conv, wa_ref, ba_ref, wx_ref, bx_ref, lam_ref)
    h = a * h0_ref[...] + u
    out_ref[...] = h * jax.nn.gelu(ya_ref[...])
    h_ref[...] = h
    for j in range(CONV_W - 2):
        nbuf_ref[j] = buf_ref[j + 1]
    nbuf_ref[CONV_W - 2] = x


def lru_sample(proj, p, h0, buf_t):
    cb = lambda off: off // LRU_BLOCK
    wrap = lambda f: f
    return pl.pallas_call(
        _lru_sample_kernel,
        out_shape=(jax.ShapeDtypeStruct((DEC_BATCH, BRANCH_W), F32),
                   jax.ShapeDtypeStruct((DEC_BATCH, BRANCH_W), F32),
                   jax.ShapeDtypeStruct((CONV_W - 1, DEC_BATCH, BRANCH_W), F32)),
        grid=(LRU_BLOCKS,),
        in_specs=[pl.BlockSpec((DEC_BATCH, LRU_BLOCK), lambda n: (0, cb(OFF_XA) + n)),
                  pl.BlockSpec((DEC_BATCH, LRU_BLOCK), lambda n: (0, cb(OFF_YA) + n))]
                 + _lru_param_specs(wrap)
                 + [pl.BlockSpec((DEC_BATCH, LRU_BLOCK), lambda n: (0, n)),
                    pl.BlockSpec((CONV_W - 1, DEC_BATCH, LRU_BLOCK), lambda n: (0, 0, n))],
        out_specs=(pl.BlockSpec((DEC_BATCH, LRU_BLOCK), lambda n: (0, n)),
                   pl.BlockSpec((DEC_BATCH, LRU_BLOCK), lambda n: (0, n)),
                   pl.BlockSpec((CONV_W - 1, DEC_BATCH, LRU_BLOCK), lambda n: (0, 0, n))),
        compiler_params=_cparams(1),
        name="lru_sample",
    )(proj, proj, *_lru_params(p), h0, buf_t)


def _hgrn_lower_bound(layer, logits):
    mx = jnp.max(logits, axis=0, keepdims=True)
    e = jnp.exp(logits - mx)
    ls = e / jnp.sum(e, axis=0, keepdims=True)
    lb = jnp.zeros_like(ls[0:1])
    for i in range(1, layer + 1):
        lb = lb + ls[i:i + 1]
    return lb


def _hgrn_gates(layer, hq, hf, lbl_ref):
    lb = _hgrn_lower_bound(layer, lbl_ref[...])
    q = jax.nn.silu(hq)
    sg = jax.nn.sigmoid(hf)
    f = lb + (1.0 - lb) * sg
    k = (1.0 - lb) * (1.0 - sg)
    return q, f, k


def _hgrn_level_ids():
    c = HG_CHUNK
    t = np.arange(c)[:, None]
    s = np.arange(c)[None, :]
    level = np.zeros((c, c), np.int32)
    for li, m in enumerate(HG_LEVELS):
        same = (t // (2 * m)) == (s // (2 * m))
        level[same & ((t % (2 * m)) >= m) & ((s % (2 * m)) < m)] = li + 1
    return level


def _split3_bf16(x):
    hi = x.astype(BF16)
    r1 = x - hi.astype(F32)
    mid = r1.astype(BF16)
    lo = (r1 - mid.astype(F32)).astype(BF16)
    return hi, mid, lo


def _cumsum_rows(tril_b, x):
    return sum(jnp.dot(tril_b, piece, preferred_element_type=F32) for piece in _split3_bf16(x))


def _hgrn_prompt_kernel(layer, q_ref, f_ref, i_ref, g_ref, lbl_ref, nw_ref, lvl_ref,
                        out_ref, st_ref, lf_s, q_s, k_s, od_s, o_s):
    c = HG_CHUNK
    t_len = q_ref.shape[0]
    ng = t_len // SUBLANES
    q, f, k = _hgrn_gates(layer, q_ref[...], f_ref[...], lbl_ref)
    logf = jnp.log(f)
    lf_s[...] = logf
    q_s[...] = q
    k_s[...] = k

    g3 = lambda x: x.reshape(ng, SUBLANES, LANES)
    q3, k3, lf3, v3 = g3(q), g3(k), g3(logf), g3(i_ref[...])
    sub = lax.broadcasted_iota(jnp.int32, q3.shape, 1)
    ones = jnp.ones((LANES, LANES), BF16)
    dd = None
    od = None
    for d in range(SUBLANES):
        if d == 0:
            p = q3 * k3
            vd = v3
        else:
            step = lf3 if d == 1 else pltpu.roll(lf3, d - 1, axis=1)
            dd = step if dd is None else dd + step
            p = jnp.where(sub >= d, q3 * jnp.exp(dd) * pltpu.roll(k3, d, axis=1), 0.0)
            vd = pltpu.roll(v3, d, axis=1)
        r = jnp.dot(p.reshape(t_len, LANES).astype(BF16), ones, preferred_element_type=F32)
        od = g3(r) * vd if od is None else od + g3(r) * vd
    od_s[...] = od.reshape(t_len, LANES)

    lvl = lvl_ref[...]
    tril_b = (lax.broadcasted_iota(jnp.int32, (c, c), 0) >= lax.broadcasted_iota(jnp.int32, (c, c), 1)).astype(BF16)

    st = jnp.zeros((HG_DK, HG_DK), F32)
    for ci in range(t_len // c):
        sl = slice(ci * c, (ci + 1) * c)
        qc = q_s[sl, :]
        kc = k_s[sl, :]
        vb = i_ref[sl, :].astype(BF16)
        b = _cumsum_rows(tril_b, lf_s[sl, :])
        att = jnp.zeros((c, c), F32)
        for li, m in enumerate(HG_LEVELS):
            b3 = b.reshape(c // (2 * m), 2 * m, LANES)
            mid = b3[:, m - 1:m, :]
            upper = lax.broadcasted_iota(jnp.int32, b3.shape, 1) >= m
            e = jnp.exp(jnp.where(upper, b3 - mid, mid - b3)).reshape(c, LANES)
            a_l = _nt_dot((qc * e).astype(BF16), (kc * e).astype(BF16))
            att = jnp.where(lvl == li + 1, a_l, att)
        o = (jnp.dot(att.astype(BF16), vb, preferred_element_type=F32) + od_s[sl, :]
             + _nt_dot((qc * jnp.exp(b)).astype(BF16), st.astype(BF16)))
        o_s[sl, :] = o
        bl = b[c - 1:c, :]
        kdec = (kc * jnp.exp(bl - b)).astype(BF16)
        st = st * jnp.exp(bl) + _tn_dot(vb, kdec)
    o = o_s[...]
    out_ref[...] = (_rms_rows(o) * nw_ref[...] * jax.nn.silu(g_ref[...])).astype(out_ref.dtype)
    st_ref[...] = st.T


def hgrn_prompt(layer, proj, p):
    cb = lambda off: off // HG_DK
    level = _hgrn_level_ids()
    col = lambda off: pl.BlockSpec((SEQ, HG_DK), lambda b, h: (b, cb(off) + h))
    return pl.pallas_call(
        functools.partial(_hgrn_prompt_kernel, layer),
        out_shape=(jax.ShapeDtypeStruct((BATCH * SEQ, BRANCH_W), BF16),
                   jax.ShapeDtypeStruct((BATCH, HG_HEADS, HG_DK, HG_DK), F32)),
        grid=(BATCH, HG_HEADS),
        in_specs=[col(OFF_HQ), col(OFF_HF), col(OFF_HI), col(OFF_HG),
                  pl.BlockSpec((DEPTH, HG_DK), lambda b, h: (0, h)),
                  pl.BlockSpec((1, HG_DK), lambda b, h: (0, 0)),
                  pl.BlockSpec(level.shape, lambda b, h: (0, 0))],
        out_specs=(pl.BlockSpec((SEQ, HG_DK), lambda b, h: (b, h)),
                   pl.BlockSpec((None, None, HG_DK, HG_DK), lambda b, h: (b, h, 0, 0))),
        scratch_shapes=[pltpu.VMEM((SEQ, HG_DK), F32)] * 5,
        compiler_params=_cparams(2),
        name="hgrn_prompt",
    )(proj, proj, proj, proj, p["hg_lb_logits"], p["hg_norm_w"].reshape(1, HG_DK),
      jnp.asarray(level))


def _hgrn_sample_kernel(layer, q_ref, f_ref, i_ref, g_ref, lbl_ref, nw_ref, s_ref, out_ref, so_ref, o_s):
    q, f, k = _hgrn_gates(layer, q_ref[...], f_ref[...], lbl_ref)
    qc, fc, kc = _rows_to_cols(q), _rows_to_cols(f), _rows_to_cols(k)
    nb = q.shape[0]
    for j in range(nb):
        s_new = s_ref[j] * fc[:, j:j + 1] + kc[:, j:j + 1] * i_ref[j:j + 1, :]
        so_ref[j] = s_new
        o_s[j:j + 1, :] = jnp.sum(s_new * qc[:, j:j + 1], axis=0, keepdims=True)
    out_ref[...] = _rms_rows(o_s[...]) * nw_ref[...] * jax.nn.silu(g_ref[...])


def hgrn_sample(layer, proj, p, state, state_out):
    cb = lambda off: off // HG_DK
    nb = SAMPLE_BLK
    col = lambda off: pl.BlockSpec((nb, HG_DK), lambda h, i: (i, cb(off) + h))
    st_spec = pl.BlockSpec((None, nb, None, HG_DK, HG_DK), lambda h, i: (layer, i, h, 0, 0))
    args = [proj, proj, proj, proj, p["hg_lb_logits"], p["hg_norm_w"].reshape(1, HG_DK), state]
    in_specs = [col(OFF_HQ), col(OFF_HF), col(OFF_HI), col(OFF_HG),
                pl.BlockSpec((DEPTH, HG_DK), lambda h, i: (0, h)),
                pl.BlockSpec((1, HG_DK), lambda h, i: (0, 0)),
                st_spec]
    aliases = {}
    kern = functools.partial(_hgrn_sample_kernel, layer)
    if state_out is not None:
        args.append(state_out)
        in_specs.append(pl.BlockSpec(memory_space=pl.ANY))
        aliases = {len(args) - 1: 1}
        kern = functools.partial(_drop_alias_arg, kern, 7)
    return pl.pallas_call(
        kern,
        out_shape=(jax.ShapeDtypeStruct((DEC_BATCH, BRANCH_W), F32),
                   jax.ShapeDtypeStruct(state.shape, F32)),
        grid=(HG_HEADS, DEC_BATCH // nb),
        in_specs=in_specs,
        out_specs=(pl.BlockSpec((nb, HG_DK), lambda h, i: (i, h)), st_spec),
        scratch_shapes=[pltpu.VMEM((nb, HG_DK), F32)],
        input_output_aliases=aliases,
        compiler_params=_cparams(2),
        name="hgrn_sample",
    )(*args)


def _drop_alias_arg(kern, pos, *refs):
    return kern(*refs[:pos], *refs[pos + 1:])


def _head_pair(cols, h0, lane_lo):
    return jnp.where(lane_lo, cols[:, h0:h0 + 1], cols[:, h0 + 1:h0 + 2])


def _ssd_prompt_kernel(z_ref, x_ref, bc_ref, dt_ref, cw_ref, cb_ref, dtb_ref, alog_ref, dpar_ref, nw_ref,
                       tril_ref, out_ref, st_ref, cst_ref, cx_s, cbc_s, s_s):
    c = SSD_CHUNK
    ci = pl.program_id(1)

    @pl.when(ci == 0)
    def _():
        cx_s[...] = jnp.zeros_like(cx_s)
        cbc_s[...] = jnp.zeros_like(cbc_s)
        s_s[...] = jnp.zeros_like(s_s)

    cw = cw_ref[...]
    cbias = cb_ref[...]

    def conv_silu(raw, carry_ref, lo, hi):
        xx = jnp.concatenate([carry_ref[...], raw], axis=0)
        y = cbias[:, lo:hi] + cw[CONV_W - 1:CONV_W, lo:hi] * raw
        for d in range(1, CONV_W):
            y = y + cw[CONV_W - 1 - d:CONV_W - d, lo:hi] * pltpu.roll(xx, d, axis=0)[SUBLANES:]
        carry_ref[...] = raw[c - SUBLANES:, :]
        return jax.nn.silu(y)

    x_raw = x_ref[...]
    bc_raw = bc_ref[...]
    xs = conv_silu(x_raw, cx_s, 0, BRANCH_W)
    bc = conv_silu(bc_raw, cbc_s, BRANCH_W, SSD_CONV_DIM)

    dt = jax.nn.softplus(dt_ref[...] + dtb_ref[...])
    a_neg = -jnp.exp(alog_ref[...])
    logd = dt * a_neg
    b = jnp.dot(tril_ref[...], logd, precision=HIGHEST, preferred_element_type=F32)
    b_t = b.T
    bl = b[c - 1:c, :]
    e_in = jnp.exp(b)
    w_out = jnp.exp(bl - b)
    e_last = jnp.exp(bl)
    dfull = dpar_ref[...]

    tri = lax.broadcasted_iota(jnp.int32, (c, c), 0) >= lax.broadcasted_iota(jnp.int32, (c, c), 1)
    lane_lo = lax.broadcasted_iota(jnp.int32, (c, LANES), 1) < SSD_HEADDIM
    lane_lo_row = lax.broadcasted_iota(jnp.int32, (1, LANES), 1) < SSD_HEADDIM

    ys = []
    for g in range(SSD_GROUPS):
        bm = bc[:, g * SSD_STATE:(g + 1) * SSD_STATE].astype(BF16)
        cm = bc[:, (SSD_GROUPS + g) * SSD_STATE:(SSD_GROUPS + g + 1) * SSD_STATE].astype(BF16)
        gmat = _nt_dot(cm, bm)
        for pp in range(SSD_HEADS // SSD_GROUPS // 2):
            pi = g * (SSD_HEADS // SSD_GROUPS // 2) + pp
            h0 = 2 * pi
            xs_p = xs[:, pi * LANES:(pi + 1) * LANES]
            vdt = xs_p * _head_pair(dt, h0, lane_lo)
            vdt_b = vdt.astype(BF16)
            o_heads = []
            for hh in (h0, h0 + 1):
                diff = b[:, hh:hh + 1] - b_t[hh:hh + 1, :]
                dec = jnp.where(tri, jnp.exp(jnp.where(tri, diff, 0.0)), 0.0)
                o_heads.append(jnp.dot((gmat * dec).astype(BF16), vdt_b, preferred_element_type=F32))
            o_intra = jnp.where(lane_lo, o_heads[0], o_heads[1])
            s_p = s_s[pi]
            o_inter = _head_pair(e_in, h0, lane_lo) * jnp.dot(cm, s_p.astype(BF16), preferred_element_type=F32)
            ys.append(o_intra + o_inter + dfull[:, pi * LANES:(pi + 1) * LANES] * xs_p)
            upd = _tn_dot(bm, (vdt * _head_pair(w_out, h0, lane_lo)).astype(BF16))
            s_s[pi] = s_p * _head_pair(e_last, h0, lane_lo_row) + upd

    y = jnp.concatenate(ys, axis=1) * jax.nn.silu(z_ref[...])
    gw = BRANCH_W // SSD_GROUPS
    nw = nw_ref[...]
    outs = [_rms_rows(y[:, g * gw:(g + 1) * gw]) * nw[:, g * gw:(g + 1) * gw] for g in range(SSD_GROUPS)]
    out_ref[...] = jnp.concatenate(outs, axis=1).astype(out_ref.dtype)

    @pl.when(ci == pl.num_programs(1) - 1)
    def _():
        for pi in range(SSD_HEADS // 2):
            s_t = s_s[pi].T
            st_ref[2 * pi] = s_t[:SSD_HEADDIM, :]
            st_ref[2 * pi + 1] = s_t[SSD_HEADDIM:, :]
        cst_ref[:, 0:BRANCH_W] = x_raw[c - (CONV_W - 1):, :]
        cst_ref[:, BRANCH_W:SSD_CONV_DIM] = bc_raw[c - (CONV_W - 1):, :]


def _pad_lanes(v):
    return jnp.pad(v.astype(F32), (0, LANES - v.shape[0])).reshape(1, LANES)


def _ssd_params(p):
    return (p["ssd_conv_w"], p["ssd_conv_b"].reshape(1, SSD_CONV_DIM), _pad_lanes(p["ssd_dt_bias"]),
            _pad_lanes(p["ssd_a_log"]), jnp.repeat(p["ssd_d"].astype(F32), SSD_HEADDIM).reshape(1, BRANCH_W),
            p["ssd_norm_w"].reshape(1, BRANCH_W))


def ssd_prompt(proj, p):
    c = SSD_CHUNK
    nc = SEQ // c
    tril = jnp.asarray(np.tril(np.ones((c, c), np.float32)))
    const = lambda shape: pl.BlockSpec(shape, lambda b, i: (0, 0))
    rowblk = lambda w, off: pl.BlockSpec((c, w), lambda b, i: (b * nc + i, off // w))
    return pl.pallas_call(
        _ssd_prompt_kernel,
        out_shape=(jax.ShapeDtypeStruct((BATCH * SEQ, BRANCH_W), BF16),
                   jax.ShapeDtypeStruct((BATCH, SSD_HEADS, SSD_HEADDIM, SSD_STATE), F32),
                   jax.ShapeDtypeStruct((BATCH, CONV_W - 1, SSD_CONV_DIM), F32)),
        grid=(BATCH, nc),
        in_specs=[rowblk(BRANCH_W, OFF_SZ), rowblk(BRANCH_W, OFF_SX), rowblk(SSD_BC, OFF_SBC),
                  rowblk(LANES, OFF_SDT),
                  const((CONV_W, SSD_CONV_DIM)), const((1, SSD_CONV_DIM)), const((1, LANES)),
                  const((1, LANES)), const((1, BRANCH_W)), const((1, BRANCH_W)), const((c, c))],
        out_specs=(pl.BlockSpec((c, BRANCH_W), lambda b, i: (b * nc + i, 0)),
                   pl.BlockSpec((None, SSD_HEADS, SSD_HEADDIM, SSD_STATE), lambda b, i: (b, 0, 0, 0)),
                   pl.BlockSpec((None, CONV_W - 1, SSD_CONV_DIM), lambda b, i: (b, 0, 0))),
        scratch_shapes=[pltpu.VMEM((SUBLANES, BRANCH_W), F32), pltpu.VMEM((SUBLANES, SSD_BC), F32),
                        pltpu.VMEM((SSD_HEADS // 2, SSD_STATE, LANES), F32)],
        compiler_params=_cparams(2),
        name="ssd_prompt",
    )(proj, proj, proj, proj, *_ssd_params(p), tril)


def _ssd_sample_kernel(z_ref, x_ref, bc_ref, dt_ref, cw_ref, cb_ref, dtb_ref, alog_ref, dpar_ref, nw_ref,
                       bufx_ref, bufbc_ref, s_ref, out_ref, so_ref, nbx_ref, nbbc_ref, y_s):
    cw = cw_ref[...]
    cbias = cb_ref[...]

    def conv_silu(raw, buf_ref, nbuf_ref, lo, hi):
        y = cbias[:, lo:hi] + cw[CONV_W - 1:CONV_W, lo:hi] * raw
        for j in range(CONV_W - 1):
            y = y + cw[j:j + 1, lo:hi] * buf_ref[j]
        for j in range(CONV_W - 2):
            nbuf_ref[j] = buf_ref[j + 1]
        nbuf_ref[CONV_W - 2] = raw
        return jax.nn.silu(y)

    xs = conv_silu(x_ref[...], bufx_ref, nbx_ref, 0, BRANCH_W)
    bc = conv_silu(bc_ref[...], bufbc_ref, nbbc_ref, BRANCH_W, SSD_CONV_DIM)
    dt = jax.nn.softplus(dt_ref[...] + dtb_ref[...])
    decay = jnp.exp(dt * (-jnp.exp(alog_ref[...])))
    nb = xs.shape[0]
    hpg = SSD_HEADS // SSD_GROUPS
    lane_lo = lax.broadcasted_iota(jnp.int32, (nb, LANES), 1) < SSD_HEADDIM
    for pi in range(SSD_HEADS // 2):
        h0 = 2 * pi
        g = h0 // hpg
        xdt_cols = _rows_to_cols(xs[:, pi * LANES:(pi + 1) * LANES] * _head_pair(dt, h0, lane_lo))
        for e in range(2):
            h = h0 + e
            for j in range(nb):
                xcol = xdt_cols[e * SSD_HEADDIM:(e + 1) * SSD_HEADDIM, j:j + 1]
                brow = bc[j:j + 1, g * SSD_STATE:(g + 1) * SSD_STATE]
                so_ref[j, h] = s_ref[j, h] * decay[j:j + 1, h:h + 1] + xcol * brow
    for g in range(SSD_GROUPS):
        cm = bc[:, (SSD_GROUPS + g) * SSD_STATE:(SSD_GROUPS + g + 1) * SSD_STATE].astype(BF16)
        for j in range(nb):
            s_new = so_ref[j, g * hpg:(g + 1) * hpg].reshape(hpg * SSD_HEADDIM, SSD_STATE)
            y_s[j:j + 1, g * hpg * SSD_HEADDIM:(g + 1) * hpg * SSD_HEADDIM] = _nt_dot(cm, s_new.astype(BF16))[j:j + 1, :]
    y = (y_s[...] + dpar_ref[...] * xs) * jax.nn.silu(z_ref[...])
    gw = BRANCH_W // SSD_GROUPS
    nw = nw_ref[...]
    outs = [_rms_rows(y[:, g * gw:(g + 1) * gw]) * nw[:, g * gw:(g + 1) * gw] for g in range(SSD_GROUPS)]
    out_ref[...] = jnp.concatenate(outs, axis=1)


def ssd_sample(layer, proj, p, state, state_out, buf_t):
    nb = SSD_SAMPLE_BLK
    const = lambda shape: pl.BlockSpec(shape, lambda i: (0,) * len(shape))
    rowblk = lambda w, off: pl.BlockSpec((nb, w), lambda i: (i, off // w))
    st_spec = pl.BlockSpec((None, nb, SSD_HEADS, SSD_HEADDIM, SSD_STATE), lambda i: (layer, i, 0, 0, 0))
    bufx_spec = pl.BlockSpec((CONV_W - 1, nb, BRANCH_W), lambda i: (0, i, 0))
    bufbc_spec = pl.BlockSpec((CONV_W - 1, nb, SSD_BC), lambda i: (0, i, BRANCH_W // SSD_BC))
    args = [proj, proj, proj, proj, *_ssd_params(p), buf_t, buf_t, state]
    in_specs = [rowblk(BRANCH_W, OFF_SZ), rowblk(BRANCH_W, OFF_SX), rowblk(SSD_BC, OFF_SBC), rowblk(LANES, OFF_SDT),
                const((CONV_W, SSD_CONV_DIM)), const((1, SSD_CONV_DIM)), const((1, LANES)), const((1, LANES)),
                const((1, BRANCH_W)), const((1, BRANCH_W)), bufx_spec, bufbc_spec, st_spec]
    aliases = {}
    kern = _ssd_sample_kernel
    if state_out is not None:
        args.append(state_out)
        in_specs.append(pl.BlockSpec(memory_space=pl.ANY))
        aliases = {len(args) - 1: 1}
        kern = functools.partial(_drop_alias_arg, kern, 13)
    out, st, nbx, nbbc = pl.pallas_call(
        kern,
        out_shape=(jax.ShapeDtypeStruct((DEC_BATCH, BRANCH_W), F32),
                   jax.ShapeDtypeStruct(state.shape, F32),
                   jax.ShapeDtypeStruct((CONV_W - 1, DEC_BATCH, BRANCH_W), F32),
                   jax.ShapeDtypeStruct((CONV_W - 1, DEC_BATCH, SSD_BC), F32)),
        grid=(DEC_BATCH // nb,),
        in_specs=in_specs,
        out_specs=(pl.BlockSpec((nb, BRANCH_W), lambda i: (i, 0)), st_spec,
                   pl.BlockSpec((CONV_W - 1, nb, BRANCH_W), lambda i: (0, i, 0)),
                   pl.BlockSpec((CONV_W - 1, nb, SSD_BC), lambda i: (0, i, 0))),
        scratch_shapes=[pltpu.VMEM((nb, BRANCH_W), F32)],
        input_output_aliases=aliases,
        compiler_params=_cparams(1),
        name="ssd_sample",
    )(*args)
    return out, st, jnp.concatenate([nbx, nbbc], axis=-1)


def _ret_cols(a_ref, b_ref):
    a = a_ref[...]
    lane = lax.broadcasted_iota(jnp.int32, a.shape, 1)
    keep = LANES - RET_SHIFT
    return jnp.where(lane < keep, pltpu.roll(a, keep, axis=1), pltpu.roll(b_ref[...], keep, axis=1))


def _ret_prompt_kernel(qa_ref, qb_ref, ka_ref, kb_ref, va_ref, vb_ref, ga_ref, gb_ref, cos_ref, sin_ref, lg_ref,
                       out_ref, st_ref, q_s, k_s, v_s, o_s):
    c = RET_CHUNK
    t_len = qa_ref.shape[0]
    cos = cos_ref[...]
    sin = sin_ref[...]
    q_s[...] = _rope(_ret_cols(qa_ref, qb_ref), cos, sin)
    k_s[...] = _rope(_ret_cols(ka_ref, kb_ref), cos, sin) * RET_DK ** -0.5
    v_s[...] = _ret_cols(va_ref, vb_ref).astype(BF16)
    lg = lg_ref[...]
    lg128 = lg[:, :LANES]
    ti = lax.broadcasted_iota(jnp.int32, (c, c), 0)
    si = lax.broadcasted_iota(jnp.int32, (c, c), 1)
    tri = ti >= si
    dec = jnp.where(tri, jnp.exp(jnp.where(tri, (ti - si).astype(F32) * lg, 0.0)), 0.0)
    tt = lax.broadcasted_iota(jnp.int32, (c, LANES), 0).astype(F32)
    g_in = jnp.exp((tt + 1.0) * lg128)
    g_out = jnp.exp((c - 1.0 - tt) * lg128)
    g_all = jnp.exp(float(c) * lg128)
    s = jnp.zeros((RET_DK, RET_DK), F32)
    for ci in range(t_len // c):
        sl = slice(ci * c, (ci + 1) * c)
        qc = q_s[sl, :]
        kc = k_s[sl, :]
        vb = v_s[sl, :]
        scores = _nt_dot(qc.astype(BF16), kc.astype(BF16)) * dec
        o_s[sl, :] = (jnp.dot(scores.astype(BF16), vb, preferred_element_type=F32)
                      + jnp.dot((qc * g_in).astype(BF16), s.astype(BF16), preferred_element_type=F32))
        s = s * g_all + _tn_dot((kc * g_out).astype(BF16), vb)
    out_ref[...] = (_rms_rows(o_s[...]) * jax.nn.silu(_ret_cols(ga_ref, gb_ref))).astype(out_ref.dtype)
    st_ref[...] = s


def _log_gamma_rows(width):
    lg = jnp.log1p(-jnp.exp2(-5.0 - jnp.arange(RET_HEADS, dtype=F32)))
    return jnp.broadcast_to(lg[:, None, None], (RET_HEADS, 1, width))


def _ret_col_specs(rows, index):
    specs = []
    for off in (OFF_RQ, OFF_RK, OFF_RV, OFF_RG):
        for extra in (0, 1):
            specs.append(pl.BlockSpec((rows, RET_DK), functools.partial(
                lambda *g, base: (index(*g)[0], base + index(*g)[1]), base=off // RET_DK + extra)))
    return specs


def ret_prompt(proj, cos, sin):
    tab = pl.BlockSpec((SEQ, RET_DK), lambda b, h: (0, 0))
    return pl.pallas_call(
        _ret_prompt_kernel,
        out_shape=(jax.ShapeDtypeStruct((BATCH * SEQ, BRANCH_W), BF16),
                   jax.ShapeDtypeStruct((BATCH, RET_HEADS, RET_DK, RET_DK), F32)),
        grid=(BATCH, RET_HEADS),
        in_specs=_ret_col_specs(SEQ, lambda b, h: (b, h))
                 + [tab, tab, pl.BlockSpec((None, 1, RET_CHUNK), lambda b, h: (h, 0, 0))],
        out_specs=(pl.BlockSpec((SEQ, RET_DK), lambda b, h: (b, h)),
                   pl.BlockSpec((None, None, RET_DK, RET_DK), lambda b, h: (b, h, 0, 0))),
        scratch_shapes=[pltpu.VMEM((SEQ, RET_DK), F32), pltpu.VMEM((SEQ, RET_DK), F32),
                        pltpu.VMEM((SEQ, RET_DK), BF16), pltpu.VMEM((SEQ, RET_DK), F32)],
        compiler_params=_cparams(2),
        name="ret_prompt",
    )(*([proj] * 8), cos, sin, _log_gamma_rows(RET_CHUNK))


def _ret_sample_kernel(qa_ref, qb_ref, ka_ref, kb_ref, va_ref, vb_ref, ga_ref, gb_ref, cos_ref, sin_ref, lg_ref,
                       s_ref, out_ref, so_ref, o_s):
    cos = cos_ref[0:1, :]
    sin = sin_ref[0:1, :]
    q = _rope(_ret_cols(qa_ref, qb_ref), cos, sin)
    k = _rope(_ret_cols(ka_ref, kb_ref), cos, sin) * RET_DK ** -0.5
    v = _ret_cols(va_ref, vb_ref)
    gamma = jnp.exp(lg_ref[...])
    qc, kc = _rows_to_cols(q), _rows_to_cols(k)
    nb = q.shape[0]
    for j in range(nb):
        s_new = s_ref[j] * gamma + kc[:, j:j + 1] * v[j:j + 1, :]
        so_ref[j] = s_new
        o_s[j:j + 1, :] = jnp.sum(s_new * qc[:, j:j + 1], axis=0, keepdims=True)
    out_ref[...] = _rms_rows(o_s[...]) * jax.nn.silu(_ret_cols(ga_ref, gb_ref))


def ret_sample(layer, proj, cos, sin, state, state_out):
    nb = SAMPLE_BLK
    tab = pl.BlockSpec((SUBLANES, RET_DK), lambda h, i: (0, 0))
    st_spec = pl.BlockSpec((None, nb, None, RET_DK, RET_DK), lambda h, i: (layer, i, h, 0, 0))
    args = [proj] * 8 + [cos, sin, _log_gamma_rows(LANES), state]
    in_specs = (_ret_col_specs(nb, lambda h, i: (i, h))
                + [tab, tab, pl.BlockSpec((None, 1, LANES), lambda h, i: (h, 0, 0)), st_spec])
    aliases = {}
    kern = _ret_sample_kernel
    if state_out is not None:
        args.append(state_out)
        in_specs.append(pl.BlockSpec(memory_space=pl.ANY))
        aliases = {len(args) - 1: 1}
        kern = functools.partial(_drop_alias_arg, kern, len(args) - 1)
    return pl.pallas_call(
        kern,
        out_shape=(jax.ShapeDtypeStruct((DEC_BATCH, BRANCH_W), F32),
                   jax.ShapeDtypeStruct(state.shape, F32)),
        grid=(RET_HEADS, DEC_BATCH // nb),
        in_specs=in_specs,
        out_specs=(pl.BlockSpec((nb, RET_DK), lambda h, i: (i, h)), st_spec),
        scratch_shapes=[pltpu.VMEM((nb, RET_DK), F32)],
        input_output_aliases=aliases,
        compiler_params=_cparams(2),
        name="ret_sample",
    )(*args)


def _w_in_tail(w_in_t):
    tail = w_in_t[:, N_IN_MAIN:, :].astype(BF16)
    return jnp.pad(tail, ((0, 0), (0, LANES - (N_IN - N_IN_MAIN)), (0, 0)))


def kernel(x_prompt, x_sample, state_lru_h, state_lru_conv, state_hgrn, state_ssd, state_ssd_conv, state_ret, state_ffn_conv, g_mix, g_ffn, w_in, lru_conv_w, lru_conv_b, lru_wa, lru_ba, lru_wx, lru_bx, lru_lambda, hg_lb_logits, hg_norm_w, ssd_conv_w, ssd_conv_b, ssd_dt_bias, ssd_a_log, ssd_d, ssd_norm_w, w_branch, w_gate, w_out, ffn_w_up, ffn_w_val, ffn_conv_w, ffn_conv_b, ffn_w_down, g_final):
    xp = x_prompt.reshape(BATCH * SEQ, D_MODEL)
    xs = x_sample.reshape(DEC_BATCH, D_MODEL)

    cos_p, sin_p = rope_tables(SEQ, 0, True)
    cos_s, sin_s = rope_tables(SUBLANES, PAST_LEN, False)

    hp = rmsnorm(xp, g_mix[0], BF16, 512)
    hs = rmsnorm(xs, g_mix[0], BF16, DEC_BATCH)

    state_ssd_t = jnp.swapaxes(state_ssd, -1, -2)

    w_in_t = jnp.swapaxes(w_in, 1, 2)
    w_in_tail_b = _w_in_tail(w_in_t)
    w_gate_b = jnp.transpose(w_gate, (0, 2, 1, 3)).astype(BF16)
    w_out_b = w_out.astype(BF16)
    w_down_b = ffn_w_down.astype(BF16)

    prompt_states, sample_small = [], []
    hg_out = ssd_out = ret_out = None
    for l in range(DEPTH):
        p = {"lru_conv_w": lru_conv_w[l], "lru_conv_b": lru_conv_b[l], "lru_wa": lru_wa[l], "lru_ba": lru_ba[l],
             "lru_wx": lru_wx[l], "lru_bx": lru_bx[l], "lru_lambda": lru_lambda[l],
             "hg_lb_logits": hg_lb_logits, "hg_norm_w": hg_norm_w[l],
             "ssd_conv_w": ssd_conv_w[l], "ssd_conv_b": ssd_conv_b[l], "ssd_dt_bias": ssd_dt_bias[l],
             "ssd_a_log": ssd_a_log[l], "ssd_d": ssd_d[l], "ssd_norm_w": ssd_norm_w[l]}
        last = l == DEPTH - 1
        g_next = g_final if last else g_mix[l + 1]

        proj_p, w_in_b = in_proj_prompt(l, hp, w_in_t, 512)
        proj_p = in_proj_tail(l, hp, w_in_tail_b, proj_p, 1024)
        out_a, lru_h_p, lru_conv_p = lru_prompt(proj_p, p)
        out_b, hg_p = hgrn_prompt(l, proj_p, p)
        out_c, ssd_p, ssd_conv_p = ssd_prompt(proj_p, p)
        out_d, ret_p = ret_prompt(proj_p, cos_p, sin_p)
        merged, w_branch_b = gated_merge_prompt(l, hp, (out_a, out_b, out_c, out_d), w_gate_b, w_branch, 512, 256)
        xp, h2 = out_proj_residual_norm(l, merged, w_out_b, xp, g_ffn[l], 512)
        act, ffn_conv_p, w_up_b, w_val_b = ffn_prompt(l, h2, ffn_w_up, ffn_w_val, ffn_conv_w[l], ffn_conv_b[l], 256)
        res = down_proj_residual_norm(l, act, w_down_b, xp, g_next, 512, DOWN_PROJ_TK, not last,
                                      F32 if last else BF16)
        if last:
            (yp,) = res
        else:
            xp, hp = res
        prompt_states.append((lru_h_p, lru_conv_p, hg_p, ssd_p, ssd_conv_p, ret_p, ffn_conv_p))

        proj_s = in_proj_tail(l, hs, w_in_tail_b, in_proj_sample(hs, w_in_b), DEC_BATCH)
        lru_buf_t = jnp.swapaxes(state_lru_conv[l], 0, 1)
        ssd_buf_t = jnp.swapaxes(state_ssd_conv[l], 0, 1)
        s_a, lru_h_s, lru_nbuf = lru_sample(proj_s, p, state_lru_h[l], lru_buf_t)
        s_b, hg_out = hgrn_sample(l, proj_s, p, state_hgrn, hg_out)
        s_c, ssd_out, ssd_nbuf = ssd_sample(l, proj_s, p, state_ssd_t, ssd_out, ssd_buf_t)
        s_d, ret_out = ret_sample(l, proj_s, cos_s, sin_s, state_ret, ret_out)
        merged_s = gated_merge_sample(l, hs, (s_a, s_b, s_c, s_d), w_gate_b, w_branch_b, 256)
        xs, h2s = out_proj_residual_norm(l, merged_s, w_out_b, xs, g_ffn[l], DEC_BATCH)
        act_s, ffn_nbuf = ffn_sample(l, h2s, w_up_b, w_val_b, ffn_conv_w[l], ffn_conv_b[l], state_ffn_conv, 512)
        res = down_proj_residual_norm(l, act_s, w_down_b, xs, g_next, DEC_BATCH, DOWN_PROJ_TK, not last,
                                      F32 if last else BF16)
        if last:
            (ys,) = res
        else:
            xs, hs = res
        sample_small.append((lru_h_s, jnp.swapaxes(lru_nbuf, 0, 1), jnp.swapaxes(ssd_nbuf, 0, 1), ffn_nbuf))

    stack_p = lambda i: jnp.stack([st[i] for st in prompt_states], axis=0)
    stack_s = lambda i: jnp.stack([st[i] for st in sample_small], axis=0)
    return (yp.reshape(BATCH, SEQ, D_MODEL), ys.reshape(DEC_BATCH, 1, D_MODEL),
            stack_p(0), stack_s(0), stack_p(1), stack_s(1),
            stack_p(2), hg_out, jnp.swapaxes(stack_p(3), -1, -2), jnp.swapaxes(ssd_out, -1, -2),
            stack_p(4), stack_s(2), stack_p(5), ret_out,
            stack_p(6), stack_s(3))
```

```python
import functools
import math

import numpy as np
import jax
import jax.numpy as jnp
from jax import lax
from jax.experimental import pallas as pl
from jax.experimental.pallas import tpu as pltpu

F32 = jnp.float32
BF16 = jnp.bfloat16
HIGHEST = lax.Precision.HIGHEST

D_MODEL = 2048
BATCH = 4
SEQ = 2048
DEPTH = 2
DEC_BATCH = 128
PAST_LEN = 16384
BRANCH_W = D_MODEL // 2
EPS = 1e-6
LRU_BLOCKS = 8
LRU_BLOCK = BRANCH_W // LRU_BLOCKS
LRU_C = 8.0
CONV_W = 4
HG_HEADS = 8
HG_DK = BRANCH_W // HG_HEADS
SSD_HEADDIM = 64
SSD_HEADS = BRANCH_W // SSD_HEADDIM
SSD_GROUPS = 2
SSD_STATE = 128
SSD_BC = 2 * SSD_GROUPS * SSD_STATE
SSD_CONV_DIM = BRANCH_W + SSD_BC
RET_HEADS = 8
RET_DK = BRANCH_W // RET_HEADS
ROPE_BASE = 10000.0
D_FF = 5632
FFN_CONV_W = 3

V7X_VMEM_BYTES = 64 * 1024 * 1024
VMEM_LIMIT_BYTES = V7X_VMEM_BYTES - 8 * 1024 * 1024
LANES = 128
SUBLANES = 8

N_BRANCH = 4
N_IN = 12816
OFF_XA, OFF_YA = 0, 1024
OFF_HQ, OFF_HF, OFF_HI, OFF_HG = 2048, 3072, 4096, 5120
OFF_SZ, OFF_SX, OFF_SBC = 6144, 7168, 8192
OFF_SDT = 8704
MAIN_TN, MAIN_TILES = 1280, 7
N_MAIN = MAIN_TN * MAIN_TILES
RET_COL0 = OFF_SDT + SSD_HEADS
RET_TN, RET_TILES = 1024, 4
OFF_RQ, OFF_RK, OFF_RV, OFF_RG = 0, 1024, 2048, 3072
FFN_ROW_CHUNK = 1024
DOWN_PROJ_TK = 1408

HG_CHUNK = 128
HG_LEVELS = (1, 2, 4, 8, 16, 32, 64)
SSD_CHUNK = 128
RET_CHUNK = 256
SAMPLE_BLK = 32
SSD_SAMPLE_BLK = 8


def _cparams(n_axes):
    return pltpu.CompilerParams(dimension_semantics=("arbitrary",) * n_axes,
                                vmem_limit_bytes=VMEM_LIMIT_BYTES)


def _rms_rows(x):
    return x * lax.rsqrt(jnp.mean(x * x, axis=-1, keepdims=True) + EPS)


def _shift_rows(x, d, row):
    return jnp.where(row >= d, pltpu.roll(x, d, axis=0), 0.0)


def _nt_dot(a, b):
    return lax.dot_general(a, b, (((1,), (1,)), ((), ())), preferred_element_type=F32)


def _tn_dot(a, b):
    return lax.dot_general(a, b, (((0,), (0,)), ((), ())), preferred_element_type=F32)


def _rows_to_cols(x):
    n = x.shape[0]
    if n < LANES:
        x = jnp.concatenate([x, jnp.zeros((LANES - n, x.shape[1]), x.dtype)], axis=0)
    return x.T


def _norm_kernel(x_ref, g_ref, o_ref):
    o_ref[...] = (_rms_rows(x_ref[...]) * g_ref[...]).astype(o_ref.dtype)


def rmsnorm(x, g, out_dtype, tm):
    m, d = x.shape
    return pl.pallas_call(
        _norm_kernel,
        out_shape=jax.ShapeDtypeStruct((m, d), out_dtype),
        grid=(m // tm,),
        in_specs=[pl.BlockSpec((tm, d), lambda i: (i, 0)),
                  pl.BlockSpec((1, d), lambda i: (0, 0))],
        out_specs=pl.BlockSpec((tm, d), lambda i: (i, 0)),
        compiler_params=_cparams(1),
        name="rmsnorm",
    )(x, g.reshape(1, d))


def _mm_nt_kernel(a_ref, bt_ref, o_ref):
    o_ref[...] = _nt_dot(a_ref[...], bt_ref[...])


def _mm_nt_cast_kernel(a_ref, wt_ref, o_ref, wb_ref):
    @pl.when(pl.program_id(1) == 0)
    def _():
        wb_ref[...] = wt_ref[0].astype(BF16)

    o_ref[...] = _nt_dot(a_ref[...], wb_ref[...])


def in_proj_prompt(layer, h, w_in_t, row0, n_tiles, tn, tm):
    m = h.shape[0]
    n = n_tiles * tn
    w_spec = pl.BlockSpec((pl.Element(1), pl.Element(tn), pl.Element(D_MODEL)),
                          lambda j, i: (layer, pl.multiple_of(row0 + j * tn, SUBLANES), 0))
    return pl.pallas_call(
        _mm_nt_cast_kernel,
        out_shape=(jax.ShapeDtypeStruct((m, n), F32), jax.ShapeDtypeStruct((n, D_MODEL), BF16)),
        grid=(n_tiles, m // tm),
        in_specs=[pl.BlockSpec((tm, D_MODEL), lambda j, i: (i, 0)), w_spec],
        out_specs=(pl.BlockSpec((tm, tn), lambda j, i: (i, j)),
                   pl.BlockSpec((tn, D_MODEL), lambda j, i: (j, 0))),
        compiler_params=_cparams(2),
        name="in_proj_prompt",
    )(h, w_in_t)


def in_proj_sample(h, w_bt, tn):
    m = h.shape[0]
    n = w_bt.shape[0]
    return pl.pallas_call(
        _mm_nt_kernel,
        out_shape=jax.ShapeDtypeStruct((m, n), F32),
        grid=(n // tn,),
        in_specs=[pl.BlockSpec((m, D_MODEL), lambda j: (0, 0)),
                  pl.BlockSpec((tn, D_MODEL), lambda j: (j, 0))],
        out_specs=pl.BlockSpec((m, tn), lambda j: (0, j)),
        compiler_params=_cparams(1),
        name="in_proj_sample",
    )(h, w_bt)


def _gated_sum(h, br_refs, gate_w, branch_w):
    acc = None
    for k, br_ref in enumerate(br_refs):
        gate = jax.nn.sigmoid(jnp.dot(h, gate_w(k), preferred_element_type=F32))
        br = jnp.dot(br_ref[...].astype(BF16), branch_w(k), preferred_element_type=F32)
        acc = gate * br if acc is None else acc + gate * br
    return acc


def _merge_cast_kernel(h_ref, a_ref, b_ref, c_ref, d_ref, wg_ref, wb_ref, o_ref, wbb_ref):
    @pl.when(pl.program_id(1) == 0)
    def _():
        for k in range(N_BRANCH):
            wbb_ref[k] = wb_ref[k].astype(BF16)

    acc = _gated_sum(h_ref[...], (a_ref, b_ref, c_ref, d_ref), lambda k: wg_ref[k], lambda k: wbb_ref[k])
    o_ref[...] = acc.astype(o_ref.dtype)


def gated_merge_prompt(layer, h, branches, w_gate_b, w_branch, tm, tn):
    m = h.shape[0]
    br_specs = [pl.BlockSpec((tm, BRANCH_W), lambda j, i: (i, 0)) for _ in range(N_BRANCH)]
    return pl.pallas_call(
        _merge_cast_kernel,
        out_shape=(jax.ShapeDtypeStruct((m, D_MODEL), BF16),
                   jax.ShapeDtypeStruct((N_BRANCH, BRANCH_W, D_MODEL), BF16)),
        grid=(D_MODEL // tn, m // tm),
        in_specs=[pl.BlockSpec((tm, D_MODEL), lambda j, i: (i, 0))] + br_specs
                 + [pl.BlockSpec((None, N_BRANCH, D_MODEL, tn), lambda j, i: (layer, 0, 0, j)),
                    pl.BlockSpec((None, N_BRANCH, BRANCH_W, tn), lambda j, i: (layer, 0, 0, j))],
        out_specs=(pl.BlockSpec((tm, tn), lambda j, i: (i, j)),
                   pl.BlockSpec((N_BRANCH, BRANCH_W, tn), lambda j, i: (0, 0, j))),
        compiler_params=_cparams(2),
        name="gated_merge_prompt",
    )(h, *branches, w_gate_b, w_branch)


def _merge_kernel(h_ref, a_ref, b_ref, c_ref, d_ref, wgb_ref, wbb_ref, o_ref):
    acc = _gated_sum(h_ref[...], (a_ref, b_ref, c_ref, d_ref), lambda k: wgb_ref[k], lambda k: wbb_ref[k])
    o_ref[...] = acc.astype(o_ref.dtype)


def gated_merge_sample(layer, h, branches, w_gate_b, w_branch_b, tn):
    m = h.shape[0]
    br_specs = [pl.BlockSpec((m, BRANCH_W), lambda j: (0, 0)) for _ in range(N_BRANCH)]
    return pl.pallas_call(
        _merge_kernel,
        out_shape=jax.ShapeDtypeStruct((m, D_MODEL), BF16),
        grid=(D_MODEL // tn,),
        in_specs=[pl.BlockSpec((m, D_MODEL), lambda j: (0, 0))] + br_specs
                 + [pl.BlockSpec((None, N_BRANCH, D_MODEL, tn), lambda j: (layer, 0, 0, j)),
                    pl.BlockSpec((N_BRANCH, BRANCH_W, tn), lambda j: (0, 0, j))],
        out_specs=pl.BlockSpec((m, tn), lambda j: (0, j)),
        compiler_params=_cparams(1),
        name="gated_merge_sample",
    )(h, *branches, w_gate_b, w_branch_b)


def _out_proj_kernel(m_ref, w_ref, x_ref, g_ref, xo_ref, ho_ref):
    x_new = x_ref[...] + jnp.dot(m_ref[...], w_ref[...], preferred_element_type=F32)
    xo_ref[...] = x_new
    ho_ref[...] = (_rms_rows(x_new) * g_ref[...]).astype(ho_ref.dtype)


def out_proj_residual_norm(layer, merged, w_out, x, g, tm):
    m = x.shape[0]
    return pl.pallas_call(
        _out_proj_kernel,
        out_shape=(jax.ShapeDtypeStruct((m, D_MODEL), F32), jax.ShapeDtypeStruct((m, D_MODEL), BF16)),
        grid=(m // tm,),
        in_specs=[pl.BlockSpec((tm, D_MODEL), lambda i: (i, 0)),
                  pl.BlockSpec((None, D_MODEL, D_MODEL), lambda i: (layer, 0, 0)),
                  pl.BlockSpec((tm, D_MODEL), lambda i: (i, 0)),
                  pl.BlockSpec((1, D_MODEL), lambda i: (0, 0))],
        out_specs=(pl.BlockSpec((tm, D_MODEL), lambda i: (i, 0)),
                   pl.BlockSpec((tm, D_MODEL), lambda i: (i, 0))),
        compiler_params=_cparams(1),
        name="out_proj",
    )(merged, w_out, x, g.reshape(1, D_MODEL))


def _down_proj_kernel(emit_x, a_ref, w_ref, x_ref, g_ref, *refs):
    if emit_x:
        xo_ref, no_ref, acc_ref = refs
    else:
        no_ref, acc_ref = refs
    kk = pl.program_id(1)

    @pl.when(kk == 0)
    def _():
        acc_ref[...] = x_ref[...]

    acc_ref[...] += jnp.dot(a_ref[...], w_ref[...], preferred_element_type=F32)

    @pl.when(kk == pl.num_programs(1) - 1)
    def _():
        x_new = acc_ref[...]
        if emit_x:
            xo_ref[...] = x_new
        no_ref[...] = (_rms_rows(x_new) * g_ref[...]).astype(no_ref.dtype)


def down_proj_residual_norm(layer, a, w_down, x, g, tm, tk, emit_x, norm_dtype):
    m = x.shape[0]
    out_shape = [jax.ShapeDtypeStruct((m, D_MODEL), norm_dtype)]
    out_specs = [pl.BlockSpec((tm, D_MODEL), lambda i, k: (i, 0))]
    if emit_x:
        out_shape = [jax.ShapeDtypeStruct((m, D_MODEL), F32)] + out_shape
        out_specs = [pl.BlockSpec((tm, D_MODEL), lambda i, k: (i, 0))] + out_specs
    return pl.pallas_call(
        functools.partial(_down_proj_kernel, emit_x),
        out_shape=tuple(out_shape),
        grid=(m // tm, D_FF // tk),
        in_specs=[pl.BlockSpec((tm, tk), lambda i, k: (i, k)),
                  pl.BlockSpec((None, tk, D_MODEL), lambda i, k: (layer, k, 0)),
                  pl.BlockSpec((tm, D_MODEL), lambda i, k: (i, 0)),
                  pl.BlockSpec((1, D_MODEL), lambda i, k: (0, 0))],
        out_specs=tuple(out_specs),
        scratch_shapes=[pltpu.VMEM((tm, D_MODEL), F32)],
        compiler_params=_cparams(2),
        name="down_proj",
    )(a, w_down, x, g.reshape(1, D_MODEL))


def _ffn_prompt_kernel(h_ref, wu_ref, wv_ref, cw_ref, cb_ref, a_ref, st_ref, wub_ref, wvb_ref):
    @pl.when(pl.program_id(1) == 0)
    def _():
        wub_ref[...] = wu_ref[...].astype(BF16)
        wvb_ref[...] = wv_ref[...].astype(BF16)

    rc = FFN_ROW_CHUNK
    tn = a_ref.shape[1]
    cw = cw_ref[...]
    cbias = cb_ref[...]
    row8 = lax.broadcasted_iota(jnp.int32, (SUBLANES, tn), 0)
    tail = jnp.zeros((SUBLANES, tn), F32)
    pending = None
    for c in range(h_ref.shape[0] // rc):
        rows = slice(c * rc, (c + 1) * rc)
        hc = h_ref[rows, :]
        u = jnp.dot(hc, wub_ref[...], preferred_element_type=F32)
        if pending is not None:
            prev_rows, g_prev, v_prev = pending
            a_ref[prev_rows, :] = (g_prev * v_prev).astype(a_ref.dtype)
        v = jnp.dot(hc, wvb_ref[...], preferred_element_type=F32)
        uc = cbias + cw[FFN_CONV_W - 1:FFN_CONV_W] * u
        for d in range(1, FFN_CONV_W):
            rolled = pltpu.roll(u, d, axis=0)
            top = jnp.where(row8 >= d, rolled[:SUBLANES], pltpu.roll(tail, d, axis=0))
            uc = uc + cw[FFN_CONV_W - 1 - d:FFN_CONV_W - d] * jnp.concatenate([top, rolled[SUBLANES:]], axis=0)
        pending = (rows, jax.nn.gelu(uc), v)
        tail = u[rc - SUBLANES:, :]
    prev_rows, g_prev, v_prev = pending
    a_ref[prev_rows, :] = (g_prev * v_prev).astype(a_ref.dtype)
    st_ref[...] = tail[SUBLANES - (FFN_CONV_W - 1):, :]


def ffn_prompt(layer, h2, w_up, w_val, conv_w, conv_b, tn):
    wspec = pl.BlockSpec((None, D_MODEL, tn), lambda j, b: (layer, 0, j))
    wbspec = pl.BlockSpec((D_MODEL, tn), lambda j, b: (0, j))
    return pl.pallas_call(
        _ffn_prompt_kernel,
        out_shape=(jax.ShapeDtypeStruct((BATCH * SEQ, D_FF), BF16),
                   jax.ShapeDtypeStruct((BATCH, FFN_CONV_W - 1, D_FF), F32),
                   jax.ShapeDtypeStruct((D_MODEL, D_FF), BF16),
                   jax.ShapeDtypeStruct((D_MODEL, D_FF), BF16)),
        grid=(D_FF // tn, BATCH),
        in_specs=[pl.BlockSpec((SEQ, D_MODEL), lambda j, b: (b, 0)), wspec, wspec,
                  pl.BlockSpec((FFN_CONV_W, tn), lambda j, b: (0, j)),
                  pl.BlockSpec((1, tn), lambda j, b: (0, j))],
        out_specs=(pl.BlockSpec((SEQ, tn), lambda j, b: (b, j)),
                   pl.BlockSpec((None, FFN_CONV_W - 1, tn), lambda j, b: (b, 0, j)),
                   wbspec, wbspec),
        compiler_params=_cparams(2),
        name="ffn_prompt",
    )(h2, w_up, w_val, conv_w, conv_b.reshape(1, D_FF))


def _ffn_sample_kernel(h_ref, wu_ref, wv_ref, cw_ref, cb_ref, buf_ref, a_ref, nb_ref):
    h = h_ref[...]
    u = jnp.dot(h, wu_ref[...], preferred_element_type=F32)
    v = jnp.dot(h, wv_ref[...], preferred_element_type=F32)
    cw = cw_ref[...]
    b0 = buf_ref[:, 0, :]
    b1 = buf_ref[:, 1, :]
    uc = cb_ref[...] + cw[0:1] * b0 + cw[1:2] * b1 + cw[2:3] * u
    a_ref[...] = (jax.nn.gelu(uc) * v).astype(a_ref.dtype)
    nb_ref[:, 0, :] = b1
    nb_ref[:, 1, :] = u


def ffn_sample(layer, h2, w_up, w_val, conv_w, conv_b, buf, tn):
    return pl.pallas_call(
        _ffn_sample_kernel,
        out_shape=(jax.ShapeDtypeStruct((DEC_BATCH, D_FF), BF16),
                   jax.ShapeDtypeStruct((DEC_BATCH, FFN_CONV_W - 1, D_FF), F32)),
        grid=(D_FF // tn,),
        in_specs=[pl.BlockSpec((DEC_BATCH, D_MODEL), lambda j: (0, 0)),
                  pl.BlockSpec((D_MODEL, tn), lambda j: (0, j)),
                  pl.BlockSpec((D_MODEL, tn), lambda j: (0, j)),
                  pl.BlockSpec((FFN_CONV_W, tn), lambda j: (0, j)),
                  pl.BlockSpec((1, tn), lambda j: (0, j)),
                  pl.BlockSpec((None, DEC_BATCH, FFN_CONV_W - 1, tn), lambda j: (layer, 0, 0, j))],
        out_specs=(pl.BlockSpec((DEC_BATCH, tn), lambda j: (0, j)),
                   pl.BlockSpec((DEC_BATCH, FFN_CONV_W - 1, tn), lambda j: (0, 0, j))),
        compiler_params=_cparams(1),
        name="ffn_sample",
    )(h2, w_up, w_val, conv_w, conv_b.reshape(1, D_FF), buf)


def _rope_table_kernel(start, consecutive, freq_ref, sign_ref, cos_ref, sin_ref):
    shape = cos_ref.shape
    if consecutive:
        pos = lax.broadcasted_iota(jnp.int32, shape, 0).astype(F32) + float(start)
    else:
        pos = jnp.full(shape, float(start), F32)
    ang = pos * freq_ref[...]
    cos_ref[...] = jnp.cos(ang)
    sin_ref[...] = sign_ref[...] * jnp.sin(ang)


def rope_tables(n_rows, start, consecutive):
    half = RET_DK // 2
    freqs = ROPE_BASE ** (-jnp.arange(half, dtype=F32) / half)
    freq2 = jnp.concatenate([freqs, freqs]).reshape(1, RET_DK)
    sign = jnp.concatenate([-jnp.ones((half,), F32), jnp.ones((half,), F32)]).reshape(1, RET_DK)
    return pl.pallas_call(
        functools.partial(_rope_table_kernel, start, consecutive),
        out_shape=(jax.ShapeDtypeStruct((n_rows, RET_DK), F32), jax.ShapeDtypeStruct((n_rows, RET_DK), F32)),
        name="rope_tables",
    )(freq2, sign)


def _rope(x, cos, sin_signed):
    return x * cos + pltpu.roll(x, RET_DK // 2, axis=1) * sin_signed


def _lru_gates(conv, wa_ref, ba_ref, wx_ref, bx_ref, lam_ref):
    xb = conv.astype(BF16)
    r = jax.nn.sigmoid(jnp.dot(xb, wa_ref[...].astype(BF16), preferred_element_type=F32) + ba_ref[...])
    i = jax.nn.sigmoid(jnp.dot(xb, wx_ref[...].astype(BF16), preferred_element_type=F32) + bx_ref[...])
    log_a = -LRU_C * r * jax.nn.softplus(-lam_ref[...])
    a = jnp.exp(log_a)
    u = jnp.sqrt(1.0 - a * a) * (i * conv)
    return a, u


def _lru_prompt_kernel(xa_ref, ya_ref, cw_ref, cb_ref, wa_ref, ba_ref, wx_ref, bx_ref, lam_ref,
                       out_ref, h_ref, conv_ref, ag_s, ug_s):
    x = xa_ref[...]
    t_len = x.shape[0]
    row = lax.broadcasted_iota(jnp.int32, x.shape, 0)
    cw = cw_ref[...]
    conv = cb_ref[...] + cw[CONV_W - 1:CONV_W] * x
    for d in range(1, CONV_W):
        conv = conv + cw[CONV_W - 1 - d:CONV_W - d] * _shift_rows(x, d, row)
    a, u = _lru_gates(conv, wa_ref, ba_ref, wx_ref, bx_ref, lam_ref)
    ng = t_len // SUBLANES
    a3 = a.reshape(ng, SUBLANES, LRU_BLOCK)
    u3 = u.reshape(ng, SUBLANES, LRU_BLOCK)
    sub = lax.broadcasted_iota(jnp.int32, a3.shape, 1)
    d = 1
    while d < SUBLANES:
        keep = sub >= d
        u3 = jnp.where(keep, a3 * pltpu.roll(u3, d, axis=1) + u3, u3)
        a3 = jnp.where(keep, a3 * pltpu.roll(a3, d, axis=1), a3)
        d *= 2
    ag_s[...] = a3.reshape(t_len, LRU_BLOCK)
    ug_s[...] = u3.reshape(t_len, LRU_BLOCK)
    ag = ag_s[pl.ds(SUBLANES - 1, ng, stride=SUBLANES), :]
    ug = ug_s[pl.ds(SUBLANES - 1, ng, stride=SUBLANES), :]
    grow = lax.broadcasted_iota(jnp.int32, ag.shape, 0)
    d = 1
    while d < ng:
        keep = grow >= d
        ug = jnp.where(keep, ag * pltpu.roll(ug, d, axis=0) + ug, ug)
        ag = jnp.where(keep, ag * pltpu.roll(ag, d, axis=0), ag)
        d *= 2
    carry = _shift_rows(ug, 1, grow)
    h3 = a3 * jnp.broadcast_to(carry[:, None, :], a3.shape) + u3
    hs = h3.reshape(t_len, LRU_BLOCK)
    out_ref[...] = (hs * jax.nn.gelu(ya_ref[...])).astype(out_ref.dtype)
    h_ref[...] = ug[ng - 1:, :]
    conv_ref[...] = xa_ref[pl.ds(t_len - (CONV_W - 1), CONV_W - 1), :]


def _lru_param_specs(n_axes_fn):
    blk3 = lambda shape: pl.BlockSpec(shape, n_axes_fn(lambda n: (n, 0, 0)))
    return [pl.BlockSpec((CONV_W, LRU_BLOCK), n_axes_fn(lambda n: (0, n))),
            pl.BlockSpec((1, LRU_BLOCK), n_axes_fn(lambda n: (0, n))),
            blk3((None, LRU_BLOCK, LRU_BLOCK)), blk3((None, 1, LRU_BLOCK)),
            blk3((None, LRU_BLOCK, LRU_BLOCK)), blk3((None, 1, LRU_BLOCK)),
            blk3((None, 1, LRU_BLOCK))]


def _lru_params(p):
    return (p["lru_conv_w"], p["lru_conv_b"].reshape(1, BRANCH_W),
            p["lru_wa"], p["lru_ba"].reshape(LRU_BLOCKS, 1, LRU_BLOCK),
            p["lru_wx"], p["lru_bx"].reshape(LRU_BLOCKS, 1, LRU_BLOCK),
            p["lru_lambda"].reshape(LRU_BLOCKS, 1, LRU_BLOCK))


def lru_prompt(proj, p):
    cb = lambda off: off // LRU_BLOCK
    wrap = lambda f: (lambda b, n: f(n))
    out, h, conv = pl.pallas_call(
        _lru_prompt_kernel,
        out_shape=(jax.ShapeDtypeStruct((BATCH * SEQ, BRANCH_W), BF16),
                   jax.ShapeDtypeStruct((BATCH, 1, BRANCH_W), F32),
                   jax.ShapeDtypeStruct((BATCH, CONV_W - 1, BRANCH_W), F32)),
        grid=(BATCH, LRU_BLOCKS),
        in_specs=[pl.BlockSpec((SEQ, LRU_BLOCK), lambda b, n: (b, cb(OFF_XA) + n)),
                  pl.BlockSpec((SEQ, LRU_BLOCK), lambda b, n: (b, cb(OFF_YA) + n))] + _lru_param_specs(wrap),
        out_specs=(pl.BlockSpec((SEQ, LRU_BLOCK), lambda b, n: (b, n)),
                   pl.BlockSpec((None, 1, LRU_BLOCK), lambda b, n: (b, 0, n)),
                   pl.BlockSpec((None, CONV_W - 1, LRU_BLOCK), lambda b, n: (b, 0, n))),
        scratch_shapes=[pltpu.VMEM((SEQ, LRU_BLOCK), F32)] * 2,
        compiler_params=_cparams(2),
        name="lru_prompt",
    )(proj, proj, *_lru_params(p))
    return out, h.reshape(BATCH, BRANCH_W), conv


def _lru_sample_kernel(xa_ref, ya_ref, cw_ref, cb_ref, wa_ref, ba_ref, wx_ref, bx_ref, lam_ref,
                       h0_ref, buf_ref, out_ref, h_ref, nbuf_ref):
    x = xa_ref[...]
    cw = cw_ref[...]
    conv = cb_ref[...] + cw[CONV_W - 1:CONV_W] * x
    for j in range(CONV_W - 1):
        conv = conv + cw[j:j + 1] * buf_ref[j]
    a, u = _lru_gates(conv, wa_ref, ba_ref, wx_ref, bx_ref, lam_ref)
    h = a * h0_ref[...] + u
    out_ref[...] = h * jax.nn.gelu(ya_ref[...])
    h_ref[...] = h
    for j in range(CONV_W - 2):
        nbuf_ref[j] = buf_ref[j + 1]
    nbuf_ref[CONV_W - 2] = x


def lru_sample(proj, p, h0, buf_t):
    cb = lambda off: off // LRU_BLOCK
    wrap = lambda f: f
    return pl.pallas_call(
        _lru_sample_kernel,
        out_shape=(jax.ShapeDtypeStruct((DEC_BATCH, BRANCH_W), F32),
                   jax.ShapeDtypeStruct((DEC_BATCH, BRANCH_W), F32),
                   jax.ShapeDtypeStruct((CONV_W - 1, DEC_BATCH, BRANCH_W), F32)),
        grid=(LRU_BLOCKS,),
        in_specs=[pl.BlockSpec((DEC_BATCH, LRU_BLOCK), lambda n: (0, cb(OFF_XA) + n)),
                  pl.BlockSpec((DEC_BATCH, LRU_BLOCK), lambda n: (0, cb(OFF_YA) + n))]
                 + _lru_param_specs(wrap)
                 + [pl.BlockSpec((DEC_BATCH, LRU_BLOCK), lambda n: (0, n)),
                    pl.BlockSpec((CONV_W - 1, DEC_BATCH, LRU_BLOCK), lambda n: (0, 0, n))],
        out_specs=(pl.BlockSpec((DEC_BATCH, LRU_BLOCK), lambda n: (0, n)),
                   pl.BlockSpec((DEC_BATCH, LRU_BLOCK), lambda n: (0, n)),
                   pl.BlockSpec((CONV_W - 1, DEC_BATCH, LRU_BLOCK), lambda n: (0, 0, n))),
        compiler_params=_cparams(1),
        name="lru_sample",
    )(proj, proj, *_lru_params(p), h0, buf_t)


def _hgrn_lower_bound(layer, logits):
    mx = jnp.max(logits, axis=0, keepdims=True)
    e = jnp.exp(logits - mx)
    ls = e / jnp.sum(e, axis=0, keepdims=True)
    lb = jnp.zeros_like(ls[0:1])
    for i in range(1, layer + 1):
        lb = lb + ls[i:i + 1]
    return lb


def _hgrn_gates(layer, hq, hf, lbl_ref):
    lb = _hgrn_lower_bound(layer, lbl_ref[...])
    q = jax.nn.silu(hq)
    sg = jax.nn.sigmoid(hf)
    f = lb + (1.0 - lb) * sg
    k = (1.0 - lb) * (1.0 - sg)
    return q, f, k


def _hgrn_level_ids():
    c = HG_CHUNK
    t = np.arange(c)[:, None]
    s = np.arange(c)[None, :]
    level = np.zeros((c, c), np.int32)
    for li, m in enumerate(HG_LEVELS):
        same = (t // (2 * m)) == (s // (2 * m))
        level[same & ((t % (2 * m)) >= m) & ((s % (2 * m)) < m)] = li + 1
    level[t == s] = len(HG_LEVELS) + 1
    return level


def _split3_bf16(x):
    hi = x.astype(BF16)
    r1 = x - hi.astype(F32)
    mid = r1.astype(BF16)
    lo = (r1 - mid.astype(F32)).astype(BF16)
    return hi, mid, lo


def _cumsum_rows(tril_b, x):
    return sum(jnp.dot(tril_b, piece, preferred_element_type=F32) for piece in _split3_bf16(x))


def _hgrn_midpoint_factor(b, m):
    c = b.shape[0]
    if 2 * m >= SUBLANES:
        b3 = b.reshape(c // (2 * m), 2 * m, LANES)
        mid = b3[:, m - 1:m, :]
        upper = lax.broadcasted_iota(jnp.int32, b3.shape, 1) >= m
    else:
        b3 = b.reshape(c // SUBLANES, SUBLANES, LANES)
        sub = lax.broadcasted_iota(jnp.int32, b3.shape, 1)
        mid = b3[:, m - 1:m, :]
        for blk in range(1, SUBLANES // (2 * m)):
            lo = blk * 2 * m
            mid = jnp.where(sub >= lo, b3[:, lo + m - 1:lo + m, :], mid)
        upper = jnp.bitwise_and(sub, m) != 0
    return jnp.exp(jnp.where(upper, b3 - mid, mid - b3)).reshape(c, LANES)


def _hgrn_prompt_kernel(layer, q_ref, f_ref, i_ref, g_ref, lbl_ref, nw_ref, lvl_ref,
                        out_ref, st_ref, lf_s, q_s, k_s, o_s):
    c = HG_CHUNK
    t_len = q_ref.shape[0]
    q, f, k = _hgrn_gates(layer, q_ref[...], f_ref[...], lbl_ref)
    lf_s[...] = jnp.log(f)
    q_s[...] = q
    k_s[...] = k

    lvl = lvl_ref[...]
    tril_b = (lax.broadcasted_iota(jnp.int32, (c, c), 0) >= lax.broadcasted_iota(jnp.int32, (c, c), 1)).astype(BF16)

    st = jnp.zeros((HG_DK, HG_DK), F32)
    for ci in range(t_len // c):
        sl = slice(ci * c, (ci + 1) * c)
        qc = q_s[sl, :]
        kc = k_s[sl, :]
        vb = i_ref[sl, :].astype(BF16)
        b = _cumsum_rows(tril_b, lf_s[sl, :])
        att = jnp.where(lvl == len(HG_LEVELS) + 1, _nt_dot(qc.astype(BF16), kc.astype(BF16)), 0.0)
        for li, m in enumerate(HG_LEVELS):
            e = _hgrn_midpoint_factor(b, m)
            a_l = _nt_dot((qc * e).astype(BF16), (kc * e).astype(BF16))
            att = jnp.where(lvl == li + 1, a_l, att)
        o = (jnp.dot(att.astype(BF16), vb, preferred_element_type=F32)
             + _nt_dot((qc * jnp.exp(b)).astype(BF16), st.astype(BF16)))
        o_s[sl, :] = o
        bl = b[c - 1:c, :]
        kdec = (kc * jnp.exp(bl - b)).astype(BF16)
        st = st * jnp.exp(bl) + _tn_dot(vb, kdec)
    o = o_s[...]
    out_ref[...] = (_rms_rows(o) * nw_ref[...] * jax.nn.silu(g_ref[...])).astype(out_ref.dtype)
    st_ref[...] = st.T


def hgrn_prompt(layer, proj, p):
    cb = lambda off: off // HG_DK
    level = _hgrn_level_ids()
    col = lambda off: pl.BlockSpec((SEQ, HG_DK), lambda b, h: (b, cb(off) + h))
    return pl.pallas_call(
        functools.partial(_hgrn_prompt_kernel, layer),
        out_shape=(jax.ShapeDtypeStruct((BATCH * SEQ, BRANCH_W), BF16),
                   jax.ShapeDtypeStruct((BATCH, HG_HEADS, HG_DK, HG_DK), F32)),
        grid=(BATCH, HG_HEADS),
        in_specs=[col(OFF_HQ), col(OFF_HF), col(OFF_HI), col(OFF_HG),
                  pl.BlockSpec((DEPTH, HG_DK), lambda b, h: (0, h)),
                  pl.BlockSpec((1, HG_DK), lambda b, h: (0, 0)),
                  pl.BlockSpec(level.shape, lambda b, h: (0, 0))],
        out_specs=(pl.BlockSpec((SEQ, HG_DK), lambda b, h: (b, h)),
                   pl.BlockSpec((None, None, HG_DK, HG_DK), lambda b, h: (b, h, 0, 0))),
        scratch_shapes=[pltpu.VMEM((SEQ, HG_DK), F32)] * 4,
        compiler_params=_cparams(2),
        name="hgrn_prompt",
    )(proj, proj, proj, proj, p["hg_lb_logits"], p["hg_norm_w"].reshape(1, HG_DK),
      jnp.asarray(level))


def _hgrn_sample_kernel(layer, q_ref, f_ref, i_ref, g_ref, lbl_ref, nw_ref, s_ref, out_ref, so_ref, o_s):
    q, f, k = _hgrn_gates(layer, q_ref[...], f_ref[...], lbl_ref)
    qc, fc, kc = _rows_to_cols(q), _rows_to_cols(f), _rows_to_cols(k)
    nb = q.shape[0]
    for j in range(nb):
        s_new = s_ref[j] * fc[:, j:j + 1] + kc[:, j:j + 1] * i_ref[j:j + 1, :]
        so_ref[j] = s_new
        o_s[j:j + 1, :] = jnp.sum(s_new * qc[:, j:j + 1], axis=0, keepdims=True)
    out_ref[...] = _rms_rows(o_s[...]) * nw_ref[...] * jax.nn.silu(g_ref[...])


def hgrn_sample(layer, proj, p, state, state_out):
    cb = lambda off: off // HG_DK
    nb = SAMPLE_BLK
    col = lambda off: pl.BlockSpec((nb, HG_DK), lambda h, i: (i, cb(off) + h))
    st_spec = pl.BlockSpec((None, nb, None, HG_DK, HG_DK), lambda h, i: (layer, i, h, 0, 0))
    args = [proj, proj, proj, proj, p["hg_lb_logits"], p["hg_norm_w"].reshape(1, HG_DK), state]
    in_specs = [col(OFF_HQ), col(OFF_HF), col(OFF_HI), col(OFF_HG),
                pl.BlockSpec((DEPTH, HG_DK), lambda h, i: (0, h)),
                pl.BlockSpec((1, HG_DK), lambda h, i: (0, 0)),
                st_spec]
    aliases = {}
    kern = functools.partial(_hgrn_sample_kernel, layer)
    if state_out is not None:
        args.append(state_out)
        in_specs.append(pl.BlockSpec(memory_space=pl.ANY))
        aliases = {len(args) - 1: 1}
        kern = functools.partial(_drop_alias_arg, kern, 7)
    return pl.pallas_call(
        kern,
        out_shape=(jax.ShapeDtypeStruct((DEC_BATCH, BRANCH_W), F32),
                   jax.ShapeDtypeStruct(state.shape, F32)),
        grid=(HG_HEADS, DEC_BATCH // nb),
        in_specs=in_specs,
        out_specs=(pl.BlockSpec((nb, HG_DK), lambda h, i: (i, h)), st_spec),
        scratch_shapes=[pltpu.VMEM((nb, HG_DK), F32)],
        input_output_aliases=aliases,
        compiler_params=_cparams(2),
        name="hgrn_sample",
    )(*args)


def _drop_alias_arg(kern, pos, *refs):
    return kern(*refs[:pos], *refs[pos + 1:])


def _head_pair(cols, h0, lane_lo):
    return jnp.where(lane_lo, cols[:, h0:h0 + 1], cols[:, h0 + 1:h0 + 2])


def _ssd_prompt_kernel(z_ref, x_ref, bc_ref, dt_ref, cw_ref, cb_ref, dtb_ref, alog_ref, dpar_ref, nw_ref,
                       tril_ref, out_ref, st_ref, cst_ref, cx_s, cbc_s, s_s):
    c = SSD_CHUNK
    ci = pl.program_id(1)

    @pl.when(ci == 0)
    def _():
        cx_s[...] = jnp.zeros_like(cx_s)
        cbc_s[...] = jnp.zeros_like(cbc_s)
        s_s[...] = jnp.zeros_like(s_s)

    cw = cw_ref[...]
    cbias = cb_ref[...]

    def conv_silu(raw, carry_ref, lo, hi):
        xx = jnp.concatenate([carry_ref[...], raw], axis=0)
        y = cbias[:, lo:hi] + cw[CONV_W - 1:CONV_W, lo:hi] * raw
        for d in range(1, CONV_W):
            y = y + cw[CONV_W - 1 - d:CONV_W - d, lo:hi] * pltpu.roll(xx, d, axis=0)[SUBLANES:]
        carry_ref[...] = raw[c - SUBLANES:, :]
        return jax.nn.silu(y)

    x_raw = x_ref[...]
    bc_raw = bc_ref[...]
    xs = conv_silu(x_raw, cx_s, 0, BRANCH_W)
    bc = conv_silu(bc_raw, cbc_s, BRANCH_W, SSD_CONV_DIM)

    dt = jax.nn.softplus(dt_ref[...] + dtb_ref[...])
    a_neg = -jnp.exp(alog_ref[...])
    logd = dt * a_neg
    b = jnp.dot(tril_ref[...], logd, precision=HIGHEST, preferred_element_type=F32)
    b_t = b.T
    bl = b[c - 1:c, :]
    e_in = jnp.exp(b)
    w_out = jnp.exp(bl - b)
    e_last = jnp.exp(bl)
    dfull = dpar_ref[...]

    tri = lax.broadcasted_iota(jnp.int32, (c, c), 0) >= lax.broadcasted_iota(jnp.int32, (c, c), 1)
    lane_lo = lax.broadcasted_iota(jnp.int32, (c, LANES), 1) < SSD_HEADDIM
    lane_lo_row = lax.broadcasted_iota(jnp.int32, (1, LANES), 1) < SSD_HEADDIM

    ys = []
    for g in range(SSD_GROUPS):
        bm = bc[:, g * SSD_STATE:(g + 1) * SSD_STATE].astype(BF16)
        cm = bc[:, (SSD_GROUPS + g) * SSD_STATE:(SSD_GROUPS + g + 1) * SSD_STATE].astype(BF16)
        gmat = _nt_dot(cm, bm)
        for pp in range(SSD_HEADS // SSD_GROUPS // 2):
            pi = g * (SSD_HEADS // SSD_GROUPS // 2) + pp
            h0 = 2 * pi
            xs_p = xs[:, pi * LANES:(pi + 1) * LANES]
            vdt = xs_p * _head_pair(dt, h0, lane_lo)
            vdt_b = vdt.astype(BF16)
            o_heads = []
            for hh in (h0, h0 + 1):
                diff = b[:, hh:hh + 1] - b_t[hh:hh + 1, :]
                dec = jnp.where(tri, jnp.exp(jnp.where(tri, diff, 0.0)), 0.0)
                o_heads.append(jnp.dot((gmat * dec).astype(BF16), vdt_b, preferred_element_type=F32))
            o_intra = jnp.where(lane_lo, o_heads[0], o_heads[1])
            s_p = s_s[pi]
            o_inter = _head_pair(e_in, h0, lane_lo) * jnp.dot(cm, s_p.astype(BF16), preferred_element_type=F32)
            ys.append(o_intra + o_inter + dfull[:, pi * LANES:(pi + 1) * LANES] * xs_p)
            upd = _tn_dot(bm, (vdt * _head_pair(w_out, h0, lane_lo)).astype(BF16))
            s_s[pi] = s_p * _head_pair(e_last, h0, lane_lo_row) + upd

    y = jnp.concatenate(ys, axis=1) * jax.nn.silu(z_ref[...])
    gw = BRANCH_W // SSD_GROUPS
    nw = nw_ref[...]
    outs = [_rms_rows(y[:, g * gw:(g + 1) * gw]) * nw[:, g * gw:(g + 1) * gw] for g in range(SSD_GROUPS)]
    out_ref[...] = jnp.concatenate(outs, axis=1).astype(out_ref.dtype)

    @pl.when(ci == pl.num_programs(1) - 1)
    def _():
        for pi in range(SSD_HEADS // 2):
            s_t = s_s[pi].T
            st_ref[2 * pi] = s_t[:SSD_HEADDIM, :]
            st_ref[2 * pi + 1] = s_t[SSD_HEADDIM:, :]
        cst_ref[:, 0:BRANCH_W] = x_raw[c - (CONV_W - 1):, :]
        cst_ref[:, BRANCH_W:SSD_CONV_DIM] = bc_raw[c - (CONV_W - 1):, :]


def _pad_lanes(v):
    return jnp.pad(v.astype(F32), (0, LANES - v.shape[0])).reshape(1, LANES)


def _ssd_params(p):
    return (p["ssd_conv_w"], p["ssd_conv_b"].reshape(1, SSD_CONV_DIM), _pad_lanes(p["ssd_dt_bias"]),
            _pad_lanes(p["ssd_a_log"]), jnp.repeat(p["ssd_d"].astype(F32), SSD_HEADDIM).reshape(1, BRANCH_W),
            p["ssd_norm_w"].reshape(1, BRANCH_W))


def ssd_prompt(proj, p):
    c = SSD_CHUNK
    nc = SEQ // c
    tril = jnp.asarray(np.tril(np.ones((c, c), np.float32)))
    const = lambda shape: pl.BlockSpec(shape, lambda b, i: (0, 0))
    rowblk = lambda w, off: pl.BlockSpec((c, w), lambda b, i: (b * nc + i, off // w))
    return pl.pallas_call(
        _ssd_prompt_kernel,
        out_shape=(jax.ShapeDtypeStruct((BATCH * SEQ, BRANCH_W), BF16),
                   jax.ShapeDtypeStruct((BATCH, SSD_HEADS, SSD_HEADDIM, SSD_STATE), F32),
                   jax.ShapeDtypeStruct((BATCH, CONV_W - 1, SSD_CONV_DIM), F32)),
        grid=(BATCH, nc),
        in_specs=[rowblk(BRANCH_W, OFF_SZ), rowblk(BRANCH_W, OFF_SX), rowblk(SSD_BC, OFF_SBC),
                  rowblk(LANES, OFF_SDT),
                  const((CONV_W, SSD_CONV_DIM)), const((1, SSD_CONV_DIM)), const((1, LANES)),
                  const((1, LANES)), const((1, BRANCH_W)), const((1, BRANCH_W)), const((c, c))],
        out_specs=(pl.BlockSpec((c, BRANCH_W), lambda b, i: (b * nc + i, 0)),
                   pl.BlockSpec((None, SSD_HEADS, SSD_HEADDIM, SSD_STATE), lambda b, i: (b, 0, 0, 0)),
                   pl.BlockSpec((None, CONV_W - 1, SSD_CONV_DIM), lambda b, i: (b, 0, 0))),
        scratch_shapes=[pltpu.VMEM((SUBLANES, BRANCH_W), F32), pltpu.VMEM((SUBLANES, SSD_BC), F32),
                        pltpu.VMEM((SSD_HEADS // 2, SSD_STATE, LANES), F32)],
        compiler_params=_cparams(2),
        name="ssd_prompt",
    )(proj, proj, proj, proj, *_ssd_params(p), tril)


def _ssd_sample_kernel(z_ref, x_ref, bc_ref, dt_ref, cw_ref, cb_ref, dtb_ref, alog_ref, dpar_ref, nw_ref,
                       bufx_ref, bufbc_ref, s_ref, out_ref, so_ref, nbx_ref, nbbc_ref, y_s):
    cw = cw_ref[...]
    cbias = cb_ref[...]

    def conv_silu(raw, buf_ref, nbuf_ref, lo, hi):
        y = cbias[:, lo:hi] + cw[CONV_W - 1:CONV_W, lo:hi] * raw
        for j in range(CONV_W - 1):
            y = y + cw[j:j + 1, lo:hi] * buf_ref[j]
        for j in range(CONV_W - 2):
            nbuf_ref[j] = buf_ref[j + 1]
        nbuf_ref[CONV_W - 2] = raw
        return jax.nn.silu(y)

    xs = conv_silu(x_ref[...], bufx_ref, nbx_ref, 0, BRANCH_W)
    bc = conv_silu(bc_ref[...], bufbc_ref, nbbc_ref, BRANCH_W, SSD_CONV_DIM)
    dt = jax.nn.softplus(dt_ref[...] + dtb_ref[...])
    decay = jnp.exp(dt * (-jnp.exp(alog_ref[...])))
    nb = xs.shape[0]
    hpg = SSD_HEADS // SSD_GROUPS
    lane_lo = lax.broadcasted_iota(jnp.int32, (nb, LANES), 1) < SSD_HEADDIM
    for pi in range(SSD_HEADS // 2):
        h0 = 2 * pi
        g = h0 // hpg
        xdt_cols = _rows_to_cols(xs[:, pi * LANES:(pi + 1) * LANES] * _head_pair(dt, h0, lane_lo))
        for e in range(2):
            h = h0 + e
            for j in range(nb):
                xcol = xdt_cols[e * SSD_HEADDIM:(e + 1) * SSD_HEADDIM, j:j + 1]
                brow = bc[j:j + 1, g * SSD_STATE:(g + 1) * SSD_STATE]
                so_ref[j, h] = s_ref[j, h] * decay[j:j + 1, h:h + 1] + xcol * brow
    for g in range(SSD_GROUPS):
        cm = bc[:, (SSD_GROUPS + g) * SSD_STATE:(SSD_GROUPS + g + 1) * SSD_STATE].astype(BF16)
        for j in range(nb):
            s_new = so_ref[j, g * hpg:(g + 1) * hpg].reshape(hpg * SSD_HEADDIM, SSD_STATE)
            y_s[j:j + 1, g * hpg * SSD_HEADDIM:(g + 1) * hpg * SSD_HEADDIM] = _nt_dot(cm, s_new.astype(BF16))[j:j + 1, :]
    y = (y_s[...] + dpar_ref[...] * xs) * jax.nn.silu(z_ref[...])
    gw = BRANCH_W // SSD_GROUPS
    nw = nw_ref[...]
    outs = [_rms_rows(y[:, g * gw:(g + 1) * gw]) * nw[:, g * gw:(g + 1) * gw] for g in range(SSD_GROUPS)]
    out_ref[...] = jnp.concatenate(outs, axis=1)


def ssd_sample(layer, proj, p, state, state_out, buf_t):
    nb = SSD_SAMPLE_BLK
    const = lambda shape: pl.BlockSpec(shape, lambda i: (0,) * len(shape))
    rowblk = lambda w, off: pl.BlockSpec((nb, w), lambda i: (i, off // w))
    st_spec = pl.BlockSpec((None, nb, SSD_HEADS, SSD_HEADDIM, SSD_STATE), lambda i: (layer, i, 0, 0, 0))
    bufx_spec = pl.BlockSpec((CONV_W - 1, nb, BRANCH_W), lambda i: (0, i, 0))
    bufbc_spec = pl.BlockSpec((CONV_W - 1, nb, SSD_BC), lambda i: (0, i, BRANCH_W // SSD_BC))
    args = [proj, proj, proj, proj, *_ssd_params(p), buf_t, buf_t, state]
    in_specs = [rowblk(BRANCH_W, OFF_SZ), rowblk(BRANCH_W, OFF_SX), rowblk(SSD_BC, OFF_SBC), rowblk(LANES, OFF_SDT),
                const((CONV_W, SSD_CONV_DIM)), const((1, SSD_CONV_DIM)), const((1, LANES)), const((1, LANES)),
                const((1, BRANCH_W)), const((1, BRANCH_W)), bufx_spec, bufbc_spec, st_spec]
    aliases = {}
    kern = _ssd_sample_kernel
    if state_out is not None:
        args.append(state_out)
        in_specs.append(pl.BlockSpec(memory_space=pl.ANY))
        aliases = {len(args) - 1: 1}
        kern = functools.partial(_drop_alias_arg, kern, 13)
    out, st, nbx, nbbc = pl.pallas_call(
        kern,
        out_shape=(jax.ShapeDtypeStruct((DEC_BATCH, BRANCH_W), F32),
                   jax.ShapeDtypeStruct(state.shape, F32),
                   jax.ShapeDtypeStruct((CONV_W - 1, DEC_BATCH, BRANCH_W), F32),
                   jax.ShapeDtypeStruct((CONV_W - 1, DEC_BATCH, SSD_BC), F32)),
        grid=(DEC_BATCH // nb,),
        in_specs=in_specs,
        out_specs=(pl.BlockSpec((nb, BRANCH_W), lambda i: (i, 0)), st_spec,
                   pl.BlockSpec((CONV_W - 1, nb, BRANCH_W), lambda i: (0, i, 0)),
                   pl.BlockSpec((CONV_W - 1, nb, SSD_BC), lambda i: (0, i, 0))),
        scratch_shapes=[pltpu.VMEM((nb, BRANCH_W), F32)],
        input_output_aliases=aliases,
        compiler_params=_cparams(1),
        name="ssd_sample",
    )(*args)
    return out, st, jnp.concatenate([nbx, nbbc], axis=-1)


def _ret_prompt_kernel(q_ref, k_ref, v_ref, g_ref, cos_ref, sin_ref, lg_ref, out_ref, st_ref, q_s, k_s, v_s, o_s):
    c = RET_CHUNK
    t_len = q_ref.shape[0]
    cos = cos_ref[...]
    sin = sin_ref[...]
    q_s[...] = _rope(q_ref[...], cos, sin)
    k_s[...] = _rope(k_ref[...], cos, sin) * RET_DK ** -0.5
    v_s[...] = v_ref[...].astype(BF16)
    lg = lg_ref[...]
    lg128 = lg[:, :LANES]
    ti = lax.broadcasted_iota(jnp.int32, (c, c), 0)
    si = lax.broadcasted_iota(jnp.int32, (c, c), 1)
    tri = ti >= si
    dec = jnp.where(tri, jnp.exp(jnp.where(tri, (ti - si).astype(F32) * lg, 0.0)), 0.0)
    tt = lax.broadcasted_iota(jnp.int32, (c, LANES), 0).astype(F32)
    g_in = jnp.exp((tt + 1.0) * lg128)
    g_out = jnp.exp((c - 1.0 - tt) * lg128)
    g_all = jnp.exp(float(c) * lg128)
    s = jnp.zeros((RET_DK, RET_DK), F32)
    for ci in range(t_len // c):
        sl = slice(ci * c, (ci + 1) * c)
        qc = q_s[sl, :]
        kc = k_s[sl, :]
        vb = v_s[sl, :]
        scores = _nt_dot(qc.astype(BF16), kc.astype(BF16)) * dec
        o_s[sl, :] = (jnp.dot(scores.astype(BF16), vb, preferred_element_type=F32)
                      + jnp.dot((qc * g_in).astype(BF16), s.astype(BF16), preferred_element_type=F32))
        s = s * g_all + _tn_dot((kc * g_out).astype(BF16), vb)
    out_ref[...] = (_rms_rows(o_s[...]) * jax.nn.silu(g_ref[...])).astype(out_ref.dtype)
    st_ref[...] = s


def _log_gamma_rows(width):
    lg = jnp.log1p(-jnp.exp2(-5.0 - jnp.arange(RET_HEADS, dtype=F32)))
    return jnp.broadcast_to(lg[:, None, None], (RET_HEADS, 1, width))


def ret_prompt(proj, cos, sin):
    cb = lambda off: off // RET_DK
    col = lambda off: pl.BlockSpec((SEQ, RET_DK), lambda b, h: (b, cb(off) + h))
    tab = pl.BlockSpec((SEQ, RET_DK), lambda b, h: (0, 0))
    return pl.pallas_call(
        _ret_prompt_kernel,
        out_shape=(jax.ShapeDtypeStruct((BATCH * SEQ, BRANCH_W), BF16),
                   jax.ShapeDtypeStruct((BATCH, RET_HEADS, RET_DK, RET_DK), F32)),
        grid=(BATCH, RET_HEADS),
        in_specs=[col(OFF_RQ), col(OFF_RK), col(OFF_RV), col(OFF_RG), tab, tab,
                  pl.BlockSpec((None, 1, RET_CHUNK), lambda b, h: (h, 0, 0))],
        out_specs=(pl.BlockSpec((SEQ, RET_DK), lambda b, h: (b, h)),
                   pl.BlockSpec((None, None, RET_DK, RET_DK), lambda b, h: (b, h, 0, 0))),
        scratch_shapes=[pltpu.VMEM((SEQ, RET_DK), F32), pltpu.VMEM((SEQ, RET_DK), F32),
                        pltpu.VMEM((SEQ, RET_DK), BF16), pltpu.VMEM((SEQ, RET_DK), F32)],
        compiler_params=_cparams(2),
        name="ret_prompt",
    )(proj, proj, proj, proj, cos, sin, _log_gamma_rows(RET_CHUNK))


def _ret_sample_kernel(q_ref, k_ref, v_ref, g_ref, cos_ref, sin_ref, lg_ref, s_ref, out_ref, so_ref, o_s):
    cos = cos_ref[0:1, :]
    sin = sin_ref[0:1, :]
    q = _rope(q_ref[...], cos, sin)
    k = _rope(k_ref[...], cos, sin) * RET_DK ** -0.5
    v = v_ref[...]
    gamma = jnp.exp(lg_ref[...])
    qc, kc = _rows_to_cols(q), _rows_to_cols(k)
    nb = q.shape[0]
    for j in range(nb):
        s_new = s_ref[j] * gamma + kc[:, j:j + 1] * v[j:j + 1, :]
        so_ref[j] = s_new
        o_s[j:j + 1, :] = jnp.sum(s_new * qc[:, j:j + 1], axis=0, keepdims=True)
    out_ref[...] = _rms_rows(o_s[...]) * jax.nn.silu(g_ref[...])


def ret_sample(layer, proj, cos, sin, state, state_out):
    cb = lambda off: off // RET_DK
    nb = SAMPLE_BLK
    col = lambda off: pl.BlockSpec((nb, RET_DK), lambda h, i: (i, cb(off) + h))
    tab = pl.BlockSpec((SUBLANES, RET_DK), lambda h, i: (0, 0))
    st_spec = pl.BlockSpec((None, nb, None, RET_DK, RET_DK), lambda h, i: (layer, i, h, 0, 0))
    args = [proj, proj, proj, proj, cos, sin, _log_gamma_rows(LANES), state]
    in_specs = [col(OFF_RQ), col(OFF_RK), col(OFF_RV), col(OFF_RG), tab, tab,
                pl.BlockSpec((None, 1, LANES), lambda h, i: (h, 0, 0)), st_spec]
    aliases = {}
    kern = _ret_sample_kernel
    if state_out is not None:
        args.append(state_out)
        in_specs.append(pl.BlockSpec(memory_space=pl.ANY))
        aliases = {len(args) - 1: 1}
        kern = functools.partial(_drop_alias_arg, kern, len(args) - 1)
    return pl.pallas_call(
        kern,
        out_shape=(jax.ShapeDtypeStruct((DEC_BATCH, BRANCH_W), F32),
                   jax.ShapeDtypeStruct(state.shape, F32)),
        grid=(RET_HEADS, DEC_BATCH // nb),
        in_specs=in_specs,
        out_specs=(pl.BlockSpec((nb, RET_DK), lambda h, i: (i, h)), st_spec),
        scratch_shapes=[pltpu.VMEM((nb, RET_DK), F32)],
        input_output_aliases=aliases,
        compiler_params=_cparams(2),
        name="ret_sample",
    )(*args)


def kernel(x_prompt, x_sample, state_lru_h, state_lru_conv, state_hgrn, state_ssd, state_ssd_conv, state_ret, state_ffn_conv, g_mix, g_ffn, w_in, lru_conv_w, lru_conv_b, lru_wa, lru_ba, lru_wx, lru_bx, lru_lambda, hg_lb_logits, hg_norm_w, ssd_conv_w, ssd_conv_b, ssd_dt_bias, ssd_a_log, ssd_d, ssd_norm_w, w_branch, w_gate, w_out, ffn_w_up, ffn_w_val, ffn_conv_w, ffn_conv_b, ffn_w_down, g_final):
    xp = x_prompt.reshape(BATCH * SEQ, D_MODEL)
    xs = x_sample.reshape(DEC_BATCH, D_MODEL)

    cos_p, sin_p = rope_tables(SEQ, 0, True)
    cos_s, sin_s = rope_tables(SUBLANES, PAST_LEN, False)

    hp = rmsnorm(xp, g_mix[0], BF16, 512)
    hs = rmsnorm(xs, g_mix[0], BF16, DEC_BATCH)

    state_ssd_t = jnp.swapaxes(state_ssd, -1, -2)

    w_in_t = jnp.swapaxes(w_in, 1, 2)
    w_gate_b = jnp.transpose(w_gate, (0, 2, 1, 3)).astype(BF16)
    w_out_b = w_out.astype(BF16)
    w_down_b = ffn_w_down.astype(BF16)

    prompt_states, sample_small = [], []
    hg_out = ssd_out = ret_out = None
    for l in range(DEPTH):
        p = {"lru_conv_w": lru_conv_w[l], "lru_conv_b": lru_conv_b[l], "lru_wa": lru_wa[l], "lru_ba": lru_ba[l],
             "lru_wx": lru_wx[l], "lru_bx": lru_bx[l], "lru_lambda": lru_lambda[l],
             "hg_lb_logits": hg_lb_logits, "hg_norm_w": hg_norm_w[l],
             "ssd_conv_w": ssd_conv_w[l], "ssd_conv_b": ssd_conv_b[l], "ssd_dt_bias": ssd_dt_bias[l],
             "ssd_a_log": ssd_a_log[l], "ssd_d": ssd_d[l], "ssd_norm_w": ssd_norm_w[l]}
        last = l == DEPTH - 1
        g_next = g_final if last else g_mix[l + 1]

        proj_p, w_main_b = in_proj_prompt(l, hp, w_in_t, 0, MAIN_TILES, MAIN_TN, 512)
        proj_ret_p, w_ret_b = in_proj_prompt(l, hp, w_in_t, RET_COL0, RET_TILES, RET_TN, 512)
        out_a, lru_h_p, lru_conv_p = lru_prompt(proj_p, p)
        out_b, hg_p = hgrn_prompt(l, proj_p, p)
        out_c, ssd_p, ssd_conv_p = ssd_prompt(proj_p, p)
        out_d, ret_p = ret_prompt(proj_ret_p, cos_p, sin_p)
        merged, w_branch_b = gated_merge_prompt(l, hp, (out_a, out_b, out_c, out_d), w_gate_b, w_branch, 512, 256)
        xp, h2 = out_proj_residual_norm(l, merged, w_out_b, xp, g_ffn[l], 512)
        act, ffn_conv_p, w_up_b, w_val_b = ffn_prompt(l, h2, ffn_w_up, ffn_w_val, ffn_conv_w[l], ffn_conv_b[l], 256)
        res = down_proj_residual_norm(l, act, w_down_b, xp, g_next, 512, DOWN_PROJ_TK, not last,
                                      F32 if last else BF16)
        if last:
            (yp,) = res
        else:
            xp, hp = res
        prompt_states.append((lru_h_p, lru_conv_p, hg_p, ssd_p, ssd_conv_p, ret_p, ffn_conv_p))

        proj_s = in_proj_sample(hs, w_main_b, MAIN_TN)
        proj_ret_s = in_proj_sample(hs, w_ret_b, RET_TN)
        lru_buf_t = jnp.swapaxes(state_lru_conv[l], 0, 1)
        ssd_buf_t = jnp.swapaxes(state_ssd_conv[l], 0, 1)
        s_a, lru_h_s, lru_nbuf = lru_sample(proj_s, p, state_lru_h[l], lru_buf_t)
        s_b, hg_out = hgrn_sample(l, proj_s, p, state_hgrn, hg_out)
        s_c, ssd_out, ssd_nbuf = ssd_sample(l, proj_s, p, state_ssd_t, ssd_out, ssd_buf_t)
        s_d, ret_out = ret_sample(l, proj_ret_s, cos_s, sin_s, state_ret, ret_out)
        merged_s = gated_merge_sample(l, hs, (s_a, s_b, s_c, s_d), w_gate_b, w_branch_b, 256)
        xs, h2s = out_proj_residual_norm(l, merged_s, w_out_b, xs, g_ffn[l], DEC_BATCH)
        act_s, ffn_nbuf = ffn_sample(l, h2s, w_up_b, w_val_b, ffn_conv_w[l], ffn_conv_b[l], state_ffn_conv, 512)
        res = down_proj_residual_norm(l, act_s, w_down_b, xs, g_next, DEC_BATCH, DOWN_PROJ_TK, not last,
                                      F32 if last else BF16)
        if last:
            (ys,) = res
        else:
            xs, hs = res
        sample_small.append((lru_h_s, jnp.swapaxes(lru_nbuf, 0, 1), jnp.swapaxes(ssd_nbuf, 0, 1), ffn_nbuf))

    stack_p = lambda i: jnp.stack([st[i] for st in prompt_states], axis=0)
    stack_s = lambda i: jnp.stack([st[i] for st in sample_small], axis=0)
    return (yp.reshape(BATCH, SEQ, D_MODEL), ys.reshape(DEC_BATCH, 1, D_MODEL),
            stack_p(0), stack_s(0), stack_p(1), stack_s(1),
            stack_p(2), hg_out, jnp.swapaxes(stack_p(3), -1, -2), jnp.swapaxes(ssd_out, -1, -2),
            stack_p(4), stack_s(2), stack_p(5), ret_out,
            stack_p(6), stack_s(3))
```

```python
import functools
import math

import numpy as np
import jax
import jax.numpy as jnp
from jax import lax
from jax.experimental import pallas as pl
from jax.experimental.pallas import tpu as pltpu

F32 = jnp.float32
BF16 = jnp.bfloat16
HIGHEST = lax.Precision.HIGHEST

D_MODEL = 2048
BATCH = 4
SEQ = 2048
DEPTH = 2
DEC_BATCH = 128
PAST_LEN = 16384
BRANCH_W = D_MODEL // 2
EPS = 1e-6
LRU_BLOCKS = 8
LRU_BLOCK = BRANCH_W // LRU_BLOCKS
LRU_C = 8.0
CONV_W = 4
HG_HEADS = 8
HG_DK = BRANCH_W // HG_HEADS
SSD_HEADDIM = 64
SSD_HEADS = BRANCH_W // SSD_HEADDIM
SSD_GROUPS = 2
SSD_STATE = 128
SSD_BC = 2 * SSD_GROUPS * SSD_STATE
SSD_CONV_DIM = BRANCH_W + SSD_BC
RET_HEADS = 8
RET_DK = BRANCH_W // RET_HEADS
ROPE_BASE = 10000.0
D_FF = 5632
FFN_CONV_W = 3

V7X_VMEM_BYTES = 64 * 1024 * 1024
VMEM_LIMIT_BYTES = V7X_VMEM_BYTES - 8 * 1024 * 1024
LANES = 128
SUBLANES = 8

N_BRANCH = 4
N_IN = 12816
OFF_XA, OFF_YA = 0, 1024
OFF_HQ, OFF_HF, OFF_HI, OFF_HG = 2048, 3072, 4096, 5120
OFF_SZ, OFF_SX, OFF_SBC = 6144, 7168, 8192
OFF_SDT = 8704
MAIN_TN, MAIN_TILES = 1280, 7
N_MAIN = MAIN_TN * MAIN_TILES
RET_COL0 = OFF_SDT + SSD_HEADS
RET_TN, RET_TILES = 1024, 4
SSD_TN, SSD_TILES = 896, 3
SSD_PROJ_W = SSD_TN * SSD_TILES
OFF_RQ, OFF_RK, OFF_RV, OFF_RG = 0, 1024, 2048, 3072
FFN_ROW_CHUNK = 1024
DOWN_PROJ_TK = 1408

HG_CHUNK = 128
HG_LEVELS = (1, 2, 4, 8, 16, 32, 64)
SSD_CHUNK = 128
RET_CHUNK = 256
SAMPLE_BLK = 32
SSD_SAMPLE_BLK = 8


def _cparams(n_axes):
    return pltpu.CompilerParams(dimension_semantics=("arbitrary",) * n_axes,
                                vmem_limit_bytes=VMEM_LIMIT_BYTES)


def _rms_rows(x):
    return x * lax.rsqrt(jnp.mean(x * x, axis=-1, keepdims=True) + EPS)


def _shift_rows(x, d, row):
    return jnp.where(row >= d, pltpu.roll(x, d, axis=0), 0.0)


def _nt_dot(a, b):
    return lax.dot_general(a, b, (((1,), (1,)), ((), ())), preferred_element_type=F32)


def _tn_dot(a, b):
    return lax.dot_general(a, b, (((0,), (0,)), ((), ())), preferred_element_type=F32)


def _project(h_ref, w_refs):
    w = jnp.concatenate([w_ref[0].astype(BF16) for w_ref in w_refs], axis=0)
    return _nt_dot(h_ref[...], w)


def _w_in_row_specs(layer, offsets, index_map_for):
    return [pl.BlockSpec((pl.Element(1), pl.Element(LANES), pl.Element(D_MODEL)), index_map_for(layer, off))
            for off in offsets]


def _rows_to_cols(x):
    n = x.shape[0]
    if n < LANES:
        x = jnp.concatenate([x, jnp.zeros((LANES - n, x.shape[1]), x.dtype)], axis=0)
    return x.T


def _norm_kernel(x_ref, g_ref, o_ref):
    o_ref[...] = (_rms_rows(x_ref[...]) * g_ref[...]).astype(o_ref.dtype)


def rmsnorm(x, g, out_dtype, tm):
    m, d = x.shape
    return pl.pallas_call(
        _norm_kernel,
        out_shape=jax.ShapeDtypeStruct((m, d), out_dtype),
        grid=(m // tm,),
        in_specs=[pl.BlockSpec((tm, d), lambda i: (i, 0)),
                  pl.BlockSpec((1, d), lambda i: (0, 0))],
        out_specs=pl.BlockSpec((tm, d), lambda i: (i, 0)),
        compiler_params=_cparams(1),
        name="rmsnorm",
    )(x, g.reshape(1, d))


def _mm_nt_kernel(a_ref, bt_ref, o_ref):
    o_ref[...] = _nt_dot(a_ref[...], bt_ref[...])


def _mm_nt_cast_kernel(a_ref, wt_ref, o_ref, wb_ref):
    @pl.when(pl.program_id(1) == 0)
    def _():
        wb_ref[...] = wt_ref[0].astype(BF16)

    o_ref[...] = _nt_dot(a_ref[...], wb_ref[...])


def in_proj_prompt(layer, h, w_in_t, row0, n_tiles, tn, tm):
    m = h.shape[0]
    n = n_tiles * tn
    w_spec = pl.BlockSpec((pl.Element(1), pl.Element(tn), pl.Element(D_MODEL)),
                          lambda j, i: (layer, pl.multiple_of(row0 + j * tn, SUBLANES), 0))
    return pl.pallas_call(
        _mm_nt_cast_kernel,
        out_shape=(jax.ShapeDtypeStruct((m, n), F32), jax.ShapeDtypeStruct((n, D_MODEL), BF16)),
        grid=(n_tiles, m // tm),
        in_specs=[pl.BlockSpec((tm, D_MODEL), lambda j, i: (i, 0)), w_spec],
        out_specs=(pl.BlockSpec((tm, tn), lambda j, i: (i, j)),
                   pl.BlockSpec((tn, D_MODEL), lambda j, i: (j, 0))),
        compiler_params=_cparams(2),
        name="in_proj_prompt",
    )(h, w_in_t)


def _mm_nt_castw_kernel(a_ref, wt_ref, o_ref):
    o_ref[...] = _nt_dot(a_ref[...], wt_ref[0].astype(BF16))


def in_proj_sample(layer, h, w_in_t, row0, n_tiles, tn):
    m = h.shape[0]
    w_spec = pl.BlockSpec((pl.Element(1), pl.Element(tn), pl.Element(D_MODEL)),
                          lambda j: (layer, pl.multiple_of(row0 + j * tn, SUBLANES), 0))
    return pl.pallas_call(
        _mm_nt_castw_kernel,
        out_shape=jax.ShapeDtypeStruct((m, n_tiles * tn), F32),
        grid=(n_tiles,),
        in_specs=[pl.BlockSpec((m, D_MODEL), lambda j: (0, 0)), w_spec],
        out_specs=pl.BlockSpec((m, tn), lambda j: (0, j)),
        compiler_params=_cparams(1),
        name="in_proj_sample",
    )(h, w_in_t)


def _gated_sum(h, br_refs, gate_w, branch_w):
    acc = None
    for k, br_ref in enumerate(br_refs):
        gate = jax.nn.sigmoid(jnp.dot(h, gate_w(k), preferred_element_type=F32))
        br = jnp.dot(br_ref[...].astype(BF16), branch_w(k), preferred_element_type=F32)
        acc = gate * br if acc is None else acc + gate * br
    return acc


def _merge_cast_kernel(h_ref, a_ref, b_ref, c_ref, d_ref, wg_ref, wb_ref, o_ref, wbb_ref):
    @pl.when(pl.program_id(1) == 0)
    def _():
        for k in range(N_BRANCH):
            wbb_ref[k] = wb_ref[k].astype(BF16)

    acc = _gated_sum(h_ref[...], (a_ref, b_ref, c_ref, d_ref), lambda k: wg_ref[k], lambda k: wbb_ref[k])
    o_ref[...] = acc.astype(o_ref.dtype)


def gated_merge_prompt(layer, h, branches, w_gate_b, w_branch, tm, tn):
    m = h.shape[0]
    br_specs = [pl.BlockSpec((tm, BRANCH_W), lambda j, i: (i, 0)) for _ in range(N_BRANCH)]
    return pl.pallas_call(
        _merge_cast_kernel,
        out_shape=(jax.ShapeDtypeStruct((m, D_MODEL), BF16),
                   jax.ShapeDtypeStruct((N_BRANCH, BRANCH_W, D_MODEL), BF16)),
        grid=(D_MODEL // tn, m // tm),
        in_specs=[pl.BlockSpec((tm, D_MODEL), lambda j, i: (i, 0))] + br_specs
                 + [pl.BlockSpec((None, N_BRANCH, D_MODEL, tn), lambda j, i: (layer, 0, 0, j)),
                    pl.BlockSpec((None, N_BRANCH, BRANCH_W, tn), lambda j, i: (layer, 0, 0, j))],
        out_specs=(pl.BlockSpec((tm, tn), lambda j, i: (i, j)),
                   pl.BlockSpec((N_BRANCH, BRANCH_W, tn), lambda j, i: (0, 0, j))),
        compiler_params=_cparams(2),
        name="gated_merge_prompt",
    )(h, *branches, w_gate_b, w_branch)


def _merge_kernel(h_ref, a_ref, b_ref, c_ref, d_ref, wgb_ref, wbb_ref, o_ref):
    acc = _gated_sum(h_ref[...], (a_ref, b_ref, c_ref, d_ref), lambda k: wgb_ref[k], lambda k: wbb_ref[k])
    o_ref[...] = acc.astype(o_ref.dtype)


def gated_merge_sample(layer, h, branches, w_gate_b, w_branch_b, tn):
    m = h.shape[0]
    br_specs = [pl.BlockSpec((m, BRANCH_W), lambda j: (0, 0)) for _ in range(N_BRANCH)]
    return pl.pallas_call(
        _merge_kernel,
        out_shape=jax.ShapeDtypeStruct((m, D_MODEL), BF16),
        grid=(D_MODEL // tn,),
        in_specs=[pl.BlockSpec((m, D_MODEL), lambda j: (0, 0))] + br_specs
                 + [pl.BlockSpec((None, N_BRANCH, D_MODEL, tn), lambda j: (layer, 0, 0, j)),
                    pl.BlockSpec((N_BRANCH, BRANCH_W, tn), lambda j: (0, 0, j))],
        out_specs=pl.BlockSpec((m, tn), lambda j: (0, j)),
        compiler_params=_cparams(1),
        name="gated_merge_sample",
    )(h, *branches, w_gate_b, w_branch_b)


def _out_proj_kernel(m_ref, w_ref, x_ref, g_ref, xo_ref, ho_ref):
    x_new = x_ref[...] + jnp.dot(m_ref[...], w_ref[...], preferred_element_type=F32)
    xo_ref[...] = x_new
    ho_ref[...] = (_rms_rows(x_new) * g_ref[...]).astype(ho_ref.dtype)


def out_proj_residual_norm(layer, merged, w_out, x, g, tm):
    m = x.shape[0]
    return pl.pallas_call(
        _out_proj_kernel,
        out_shape=(jax.ShapeDtypeStruct((m, D_MODEL), F32), jax.ShapeDtypeStruct((m, D_MODEL), BF16)),
        grid=(m // tm,),
        in_specs=[pl.BlockSpec((tm, D_MODEL), lambda i: (i, 0)),
                  pl.BlockSpec((None, D_MODEL, D_MODEL), lambda i: (layer, 0, 0)),
                  pl.BlockSpec((tm, D_MODEL), lambda i: (i, 0)),
                  pl.BlockSpec((1, D_MODEL), lambda i: (0, 0))],
        out_specs=(pl.BlockSpec((tm, D_MODEL), lambda i: (i, 0)),
                   pl.BlockSpec((tm, D_MODEL), lambda i: (i, 0))),
        compiler_params=_cparams(1),
        name="out_proj",
    )(merged, w_out, x, g.reshape(1, D_MODEL))


def _down_proj_kernel(emit_x, a_ref, w_ref, x_ref, g_ref, *refs):
    if emit_x:
        xo_ref, no_ref, acc_ref = refs
    else:
        no_ref, acc_ref = refs
    kk = pl.program_id(1)

    @pl.when(kk == 0)
    def _():
        acc_ref[...] = x_ref[...]

    acc_ref[...] += jnp.dot(a_ref[...], w_ref[...], preferred_element_type=F32)

    @pl.when(kk == pl.num_programs(1) - 1)
    def _():
        x_new = acc_ref[...]
        if emit_x:
            xo_ref[...] = x_new
        no_ref[...] = (_rms_rows(x_new) * g_ref[...]).astype(no_ref.dtype)


def down_proj_residual_norm(layer, a, w_down, x, g, tm, tk, emit_x, norm_dtype):
    m = x.shape[0]
    out_shape = [jax.ShapeDtypeStruct((m, D_MODEL), norm_dtype)]
    out_specs = [pl.BlockSpec((tm, D_MODEL), lambda i, k: (i, 0))]
    if emit_x:
        out_shape = [jax.ShapeDtypeStruct((m, D_MODEL), F32)] + out_shape
        out_specs = [pl.BlockSpec((tm, D_MODEL), lambda i, k: (i, 0))] + out_specs
    return pl.pallas_call(
        functools.partial(_down_proj_kernel, emit_x),
        out_shape=tuple(out_shape),
        grid=(m // tm, D_FF // tk),
        in_specs=[pl.BlockSpec((tm, tk), lambda i, k: (i, k)),
                  pl.BlockSpec((None, tk, D_MODEL), lambda i, k: (layer, k, 0)),
                  pl.BlockSpec((tm, D_MODEL), lambda i, k: (i, 0)),
                  pl.BlockSpec((1, D_MODEL), lambda i, k: (0, 0))],
        out_specs=tuple(out_specs),
        scratch_shapes=[pltpu.VMEM((tm, D_MODEL), F32)],
        compiler_params=_cparams(2),
        name="down_proj",
    )(a, w_down, x, g.reshape(1, D_MODEL))


def _ffn_prompt_kernel(h_ref, wu_ref, wv_ref, cw_ref, cb_ref, a_ref, st_ref, wub_ref, wvb_ref):
    @pl.when(pl.program_id(1) == 0)
    def _():
        wub_ref[...] = wu_ref[...].astype(BF16)
        wvb_ref[...] = wv_ref[...].astype(BF16)

    rc = FFN_ROW_CHUNK
    tn = a_ref.shape[1]
    cw = cw_ref[...]
    cbias = cb_ref[...]
    row8 = lax.broadcasted_iota(jnp.int32, (SUBLANES, tn), 0)
    tail = jnp.zeros((SUBLANES, tn), F32)
    pending = None
    for c in range(h_ref.shape[0] // rc):
        rows = slice(c * rc, (c + 1) * rc)
        hc = h_ref[rows, :]
        u = jnp.dot(hc, wub_ref[...], preferred_element_type=F32)
        if pending is not None:
            prev_rows, g_prev, v_prev = pending
            a_ref[prev_rows, :] = (g_prev * v_prev).astype(a_ref.dtype)
        v = jnp.dot(hc, wvb_ref[...], preferred_element_type=F32)
        uc = cbias + cw[FFN_CONV_W - 1:FFN_CONV_W] * u
        for d in range(1, FFN_CONV_W):
            rolled = pltpu.roll(u, d, axis=0)
            top = jnp.where(row8 >= d, rolled[:SUBLANES], pltpu.roll(tail, d, axis=0))
            uc = uc + cw[FFN_CONV_W - 1 - d:FFN_CONV_W - d] * jnp.concatenate([top, rolled[SUBLANES:]], axis=0)
        pending = (rows, jax.nn.gelu(uc), v)
        tail = u[rc - SUBLANES:, :]
    prev_rows, g_prev, v_prev = pending
    a_ref[prev_rows, :] = (g_prev * v_prev).astype(a_ref.dtype)
    st_ref[...] = tail[SUBLANES - (FFN_CONV_W - 1):, :]


def ffn_prompt(layer, h2, w_up, w_val, conv_w, conv_b, tn):
    wspec = pl.BlockSpec((None, D_MODEL, tn), lambda j, b: (layer, 0, j))
    wbspec = pl.BlockSpec((D_MODEL, tn), lambda j, b: (0, j))
    return pl.pallas_call(
        _ffn_prompt_kernel,
        out_shape=(jax.ShapeDtypeStruct((BATCH * SEQ, D_FF), BF16),
                   jax.ShapeDtypeStruct((BATCH, FFN_CONV_W - 1, D_FF), F32),
                   jax.ShapeDtypeStruct((D_MODEL, D_FF), BF16),
                   jax.ShapeDtypeStruct((D_MODEL, D_FF), BF16)),
        grid=(D_FF // tn, BATCH),
        in_specs=[pl.BlockSpec((SEQ, D_MODEL), lambda j, b: (b, 0)), wspec, wspec,
                  pl.BlockSpec((FFN_CONV_W, tn), lambda j, b: (0, j)),
                  pl.BlockSpec((1, tn), lambda j, b: (0, j))],
        out_specs=(pl.BlockSpec((SEQ, tn), lambda j, b: (b, j)),
                   pl.BlockSpec((None, FFN_CONV_W - 1, tn), lambda j, b: (b, 0, j)),
                   wbspec, wbspec),
        compiler_params=_cparams(2),
        name="ffn_prompt",
    )(h2, w_up, w_val, conv_w, conv_b.reshape(1, D_FF))


def _ffn_sample_kernel(h_ref, wu_ref, wv_ref, cw_ref, cb_ref, buf_ref, a_ref, nb_ref):
    h = h_ref[...]
    u = jnp.dot(h, wu_ref[...], preferred_element_type=F32)
    v = jnp.dot(h, wv_ref[...], preferred_element_type=F32)
    cw = cw_ref[...]
    b0 = buf_ref[:, 0, :]
    b1 = buf_ref[:, 1, :]
    uc = cb_ref[...] + cw[0:1] * b0 + cw[1:2] * b1 + cw[2:3] * u
    a_ref[...] = (jax.nn.gelu(uc) * v).astype(a_ref.dtype)
    nb_ref[:, 0, :] = b1
    nb_ref[:, 1, :] = u


def ffn_sample(layer, h2, w_up, w_val, conv_w, conv_b, buf, tn):
    return pl.pallas_call(
        _ffn_sample_kernel,
        out_shape=(jax.ShapeDtypeStruct((DEC_BATCH, D_FF), BF16),
                   jax.ShapeDtypeStruct((DEC_BATCH, FFN_CONV_W - 1, D_FF), F32)),
        grid=(D_FF // tn,),
        in_specs=[pl.BlockSpec((DEC_BATCH, D_MODEL), lambda j: (0, 0)),
                  pl.BlockSpec((D_MODEL, tn), lambda j: (0, j)),
                  pl.BlockSpec((D_MODEL, tn), lambda j: (0, j)),
                  pl.BlockSpec((FFN_CONV_W, tn), lambda j: (0, j)),
                  pl.BlockSpec((1, tn), lambda j: (0, j)),
                  pl.BlockSpec((None, DEC_BATCH, FFN_CONV_W - 1, tn), lambda j: (layer, 0, 0, j))],
        out_specs=(pl.BlockSpec((DEC_BATCH, tn), lambda j: (0, j)),
                   pl.BlockSpec((DEC_BATCH, FFN_CONV_W - 1, tn), lambda j: (0, 0, j))),
        compiler_params=_cparams(1),
        name="ffn_sample",
    )(h2, w_up, w_val, conv_w, conv_b.reshape(1, D_FF), buf)


def _rope_table_kernel(start, consecutive, freq_ref, sign_ref, cos_ref, sin_ref):
    shape = cos_ref.shape
    if consecutive:
        pos = lax.broadcasted_iota(jnp.int32, shape, 0).astype(F32) + float(start)
    else:
        pos = jnp.full(shape, float(start), F32)
    ang = pos * freq_ref[...]
    cos_ref[...] = jnp.cos(ang)
    sin_ref[...] = sign_ref[...] * jnp.sin(ang)


def rope_tables(n_rows, start, consecutive):
    half = RET_DK // 2
    freqs = ROPE_BASE ** (-jnp.arange(half, dtype=F32) / half)
    freq2 = jnp.concatenate([freqs, freqs]).reshape(1, RET_DK)
    sign = jnp.concatenate([-jnp.ones((half,), F32), jnp.ones((half,), F32)]).reshape(1, RET_DK)
    return pl.pallas_call(
        functools.partial(_rope_table_kernel, start, consecutive),
        out_shape=(jax.ShapeDtypeStruct((n_rows, RET_DK), F32), jax.ShapeDtypeStruct((n_rows, RET_DK), F32)),
        name="rope_tables",
    )(freq2, sign)


def _rope(x, cos, sin_signed):
    return x * cos + pltpu.roll(x, RET_DK // 2, axis=1) * sin_signed


def _lru_gates(conv, wa_ref, ba_ref, wx_ref, bx_ref, lam_ref):
    xb = conv.astype(BF16)
    r = jax.nn.sigmoid(jnp.dot(xb, wa_ref[...].astype(BF16), preferred_element_type=F32) + ba_ref[...])
    i = jax.nn.sigmoid(jnp.dot(xb, wx_ref[...].astype(BF16), preferred_element_type=F32) + bx_ref[...])
    log_a = -LRU_C * r * jax.nn.softplus(-lam_ref[...])
    a = jnp.exp(log_a)
    u = jnp.sqrt(1.0 - a * a) * (i * conv)
    return a, u


def _lru_prompt_kernel(hin_ref, wxa_ref, wya_ref, cw_ref, cb_ref, wa_ref, ba_ref, wx_ref, bx_ref, lam_ref,
                       out_ref, h_ref, conv_ref, ag_s, ug_s):
    proj = _project(hin_ref, (wxa_ref, wya_ref))
    x = proj[:, :LRU_BLOCK]
    ya = proj[:, LRU_BLOCK:]
    t_len = x.shape[0]
    row = lax.broadcasted_iota(jnp.int32, x.shape, 0)
    cw = cw_ref[...]
    conv = cb_ref[...] + cw[CONV_W - 1:CONV_W] * x
    for d in range(1, CONV_W):
        conv = conv + cw[CONV_W - 1 - d:CONV_W - d] * _shift_rows(x, d, row)
    a, u = _lru_gates(conv, wa_ref, ba_ref, wx_ref, bx_ref, lam_ref)
    ng = t_len // SUBLANES
    a3 = a.reshape(ng, SUBLANES, LRU_BLOCK)
    u3 = u.reshape(ng, SUBLANES, LRU_BLOCK)
    sub = lax.broadcasted_iota(jnp.int32, a3.shape, 1)
    d = 1
    while d < SUBLANES:
        keep = sub >= d
        u3 = jnp.where(keep, a3 * pltpu.roll(u3, d, axis=1) + u3, u3)
        a3 = jnp.where(keep, a3 * pltpu.roll(a3, d, axis=1), a3)
        d *= 2
    ag_s[...] = a3.reshape(t_len, LRU_BLOCK)
    ug_s[...] = u3.reshape(t_len, LRU_BLOCK)
    ag = ag_s[pl.ds(SUBLANES - 1, ng, stride=SUBLANES), :]
    ug = ug_s[pl.ds(SUBLANES - 1, ng, stride=SUBLANES), :]
    grow = lax.broadcasted_iota(jnp.int32, ag.shape, 0)
    d = 1
    while d < ng:
        keep = grow >= d
        ug = jnp.where(keep, ag * pltpu.roll(ug, d, axis=0) + ug, ug)
        ag = jnp.where(keep, ag * pltpu.roll(ag, d, axis=0), ag)
        d *= 2
    carry = _shift_rows(ug, 1, grow)
    h3 = a3 * jnp.broadcast_to(carry[:, None, :], a3.shape) + u3
    hs = h3.reshape(t_len, LRU_BLOCK)
    out_ref[...] = (hs * jax.nn.gelu(ya)).astype(out_ref.dtype)
    h_ref[...] = ug[ng - 1:, :]
    conv_ref[...] = x[t_len - (CONV_W - 1):, :]


def _lru_param_specs(n_axes_fn):
    blk3 = lambda shape: pl.BlockSpec(shape, n_axes_fn(lambda n: (n, 0, 0)))
    return [pl.BlockSpec((CONV_W, LRU_BLOCK), n_axes_fn(lambda n: (0, n))),
            pl.BlockSpec((1, LRU_BLOCK), n_axes_fn(lambda n: (0, n))),
            blk3((None, LRU_BLOCK, LRU_BLOCK)), blk3((None, 1, LRU_BLOCK)),
            blk3((None, LRU_BLOCK, LRU_BLOCK)), blk3((None, 1, LRU_BLOCK)),
            blk3((None, 1, LRU_BLOCK))]


def _lru_params(p):
    return (p["lru_conv_w"], p["lru_conv_b"].reshape(1, BRANCH_W),
            p["lru_wa"], p["lru_ba"].reshape(LRU_BLOCKS, 1, LRU_BLOCK),
            p["lru_wx"], p["lru_bx"].reshape(LRU_BLOCKS, 1, LRU_BLOCK),
            p["lru_lambda"].reshape(LRU_BLOCKS, 1, LRU_BLOCK))


def _prompt_unit_rows(layer, off):
    return lambda b, u: (layer, pl.multiple_of(off + u * LANES, SUBLANES), 0)


def lru_prompt(layer, hin, w_in_t, p):
    wrap = lambda f: (lambda b, n: f(n))
    out, h, conv = pl.pallas_call(
        _lru_prompt_kernel,
        out_shape=(jax.ShapeDtypeStruct((BATCH * SEQ, BRANCH_W), BF16),
                   jax.ShapeDtypeStruct((BATCH, 1, BRANCH_W), F32),
                   jax.ShapeDtypeStruct((BATCH, CONV_W - 1, BRANCH_W), F32)),
        grid=(BATCH, LRU_BLOCKS),
        in_specs=[pl.BlockSpec((SEQ, D_MODEL), lambda b, n: (b, 0))]
                 + _w_in_row_specs(layer, (OFF_XA, OFF_YA), _prompt_unit_rows) + _lru_param_specs(wrap),
        out_specs=(pl.BlockSpec((SEQ, LRU_BLOCK), lambda b, n: (b, n)),
                   pl.BlockSpec((None, 1, LRU_BLOCK), lambda b, n: (b, 0, n)),
                   pl.BlockSpec((None, CONV_W - 1, LRU_BLOCK), lambda b, n: (b, 0, n))),
        scratch_shapes=[pltpu.VMEM((SEQ, LRU_BLOCK), F32)] * 2,
        compiler_params=_cparams(2),
        name="lru_prompt",
    )(hin, w_in_t, w_in_t, *_lru_params(p))
    return out, h.reshape(BATCH, BRANCH_W), conv


def _lru_sample_kernel(xa_ref, ya_ref, cw_ref, cb_ref, wa_ref, ba_ref, wx_ref, bx_ref, lam_ref,
                       h0_ref, buf_ref, out_ref, h_ref, nbuf_ref):
    x = xa_ref[...]
    cw = cw_ref[...]
    conv = cb_ref[...] + cw[CONV_W - 1:CONV_W] * x
    for j in range(CONV_W - 1):
        conv = conv + cw[j:j + 1] * buf_ref[j]
    a, u = _lru_gates(conv, wa_ref, ba_ref, wx_ref, bx_ref, lam_ref)
    h = a * h0_ref[...] + u
    out_ref[...] = h * jax.nn.gelu(ya_ref[...])
    h_ref[...] = h
    for j in range(CONV_W - 2):
        nbuf_ref[j] = buf_ref[j + 1]
    nbuf_ref[CONV_W - 2] = x


def lru_sample(proj, p, h0, buf_t):
    cb = lambda off: off // LRU_BLOCK
    wrap = lambda f: f
    return pl.pallas_call(
        _lru_sample_kernel,
        out_shape=(jax.ShapeDtypeStruct((DEC_BATCH, BRANCH_W), F32),
                   jax.ShapeDtypeStruct((DEC_BATCH, BRANCH_W), F32),
                   jax.ShapeDtypeStruct((CONV_W - 1, DEC_BATCH, BRANCH_W), F32)),
        grid=(LRU_BLOCKS,),
        in_specs=[pl.BlockSpec((DEC_BATCH, LRU_BLOCK), lambda n: (0, cb(OFF_XA) + n)),
                  pl.BlockSpec((DEC_BATCH, LRU_BLOCK), lambda n: (0, cb(OFF_YA) + n))]
                 + _lru_param_specs(wrap)
                 + [pl.BlockSpec((DEC_BATCH, LRU_BLOCK), lambda n: (0, n)),
                    pl.BlockSpec((CONV_W - 1, DEC_BATCH, LRU_BLOCK), lambda n: (0, 0, n))],
        out_specs=(pl.BlockSpec((DEC_BATCH, LRU_BLOCK), lambda n: (0, n)),
                   pl.BlockSpec((DEC_BATCH, LRU_BLOCK), lambda n: (0, n)),
                   pl.BlockSpec((CONV_W - 1, DEC_BATCH, LRU_BLOCK), lambda n: (0, 0, n))),
        compiler_params=_cparams(1),
        name="lru_sample",
    )(proj, proj, *_lru_params(p), h0, buf_t)


def _hgrn_lower_bound(layer, logits):
    mx = jnp.max(logits, axis=0, keepdims=True)
    e = jnp.exp(logits - mx)
    ls = e / jnp.sum(e, axis=0, keepdims=True)
    lb = jnp.zeros_like(ls[0:1])
    for i in range(1, layer + 1):
        lb = lb + ls[i:i + 1]
    return lb


def _hgrn_gates(layer, hq, hf, lbl_ref):
    lb = _hgrn_lower_bound(layer, lbl_ref[...])
    q = jax.nn.silu(hq)
    sg = jax.nn.sigmoid(hf)
    f = lb + (1.0 - lb) * sg
    k = (1.0 - lb) * (1.0 - sg)
    return q, f, k


def _hgrn_level_ids():
    c = HG_CHUNK
    t = np.arange(c)[:, None]
    s = np.arange(c)[None, :]
    level = np.zeros((c, c), np.int32)
    for li, m in enumerate(HG_LEVELS):
        same = (t // (2 * m)) == (s // (2 * m))
        level[same & ((t % (2 * m)) >= m) & ((s % (2 * m)) < m)] = li + 1
    level[t == s] = len(HG_LEVELS) + 1
    return level


def _split3_bf16(x):
    hi = x.astype(BF16)
    r1 = x - hi.astype(F32)
    mid = r1.astype(BF16)
    lo = (r1 - mid.astype(F32)).astype(BF16)
    return hi, mid, lo


def _cumsum_rows(tril_b, x):
    return sum(jnp.dot(tril_b, piece, preferred_element_type=F32) for piece in _split3_bf16(x))


def _hgrn_midpoint_factor(b, m):
    c = b.shape[0]
    if 2 * m >= SUBLANES:
        b3 = b.reshape(c // (2 * m), 2 * m, LANES)
        mid = b3[:, m - 1:m, :]
        upper = lax.broadcasted_iota(jnp.int32, b3.shape, 1) >= m
    else:
        b3 = b.reshape(c // SUBLANES, SUBLANES, LANES)
        sub = lax.broadcasted_iota(jnp.int32, b3.shape, 1)
        mid = b3[:, m - 1:m, :]
        for blk in range(1, SUBLANES // (2 * m)):
            lo = blk * 2 * m
            mid = jnp.where(sub >= lo, b3[:, lo + m - 1:lo + m, :], mid)
        upper = jnp.bitwise_and(sub, m) != 0
    return jnp.exp(jnp.where(upper, b3 - mid, mid - b3)).reshape(c, LANES)


def _hgrn_prompt_kernel(layer, hin_ref, wq_ref, wf_ref, wi_ref, wg_ref, lbl_ref, nw_ref, lvl_ref,
                        out_ref, st_ref, lf_s, q_s, k_s, v_s, o_s):
    c = HG_CHUNK
    t_len = hin_ref.shape[0]
    proj = _project(hin_ref, (wq_ref, wf_ref, wi_ref, wg_ref))
    q, f, k = _hgrn_gates(layer, proj[:, 0:HG_DK], proj[:, HG_DK:2 * HG_DK], lbl_ref)
    lf_s[...] = jnp.log(f)
    q_s[...] = q
    k_s[...] = k
    v_s[...] = proj[:, 2 * HG_DK:3 * HG_DK].astype(BF16)
    gate = proj[:, 3 * HG_DK:]

    lvl = lvl_ref[...]
    tril_b = (lax.broadcasted_iota(jnp.int32, (c, c), 0) >= lax.broadcasted_iota(jnp.int32, (c, c), 1)).astype(BF16)

    st = jnp.zeros((HG_DK, HG_DK), F32)
    for ci in range(t_len // c):
        sl = slice(ci * c, (ci + 1) * c)
        qc = q_s[sl, :]
        kc = k_s[sl, :]
        vb = v_s[sl, :]
        b = _cumsum_rows(tril_b, lf_s[sl, :])
        att = jnp.where(lvl == len(HG_LEVELS) + 1, _nt_dot(qc.astype(BF16), kc.astype(BF16)), 0.0)
        for li, m in enumerate(HG_LEVELS):
            e = _hgrn_midpoint_factor(b, m)
            a_l = _nt_dot((qc * e).astype(BF16), (kc * e).astype(BF16))
            att = jnp.where(lvl == li + 1, a_l, att)
        o = (jnp.dot(att.astype(BF16), vb, preferred_element_type=F32)
             + _nt_dot((qc * jnp.exp(b)).astype(BF16), st.astype(BF16)))
        o_s[sl, :] = o
        bl = b[c - 1:c, :]
        kdec = (kc * jnp.exp(bl - b)).astype(BF16)
        st = st * jnp.exp(bl) + _tn_dot(vb, kdec)
    o = o_s[...]
    out_ref[...] = (_rms_rows(o) * nw_ref[...] * jax.nn.silu(gate)).astype(out_ref.dtype)
    st_ref[...] = st.T


def hgrn_prompt(layer, hin, w_in_t, p):
    level = _hgrn_level_ids()
    return pl.pallas_call(
        functools.partial(_hgrn_prompt_kernel, layer),
        out_shape=(jax.ShapeDtypeStruct((BATCH * SEQ, BRANCH_W), BF16),
                   jax.ShapeDtypeStruct((BATCH, HG_HEADS, HG_DK, HG_DK), F32)),
        grid=(BATCH, HG_HEADS),
        in_specs=[pl.BlockSpec((SEQ, D_MODEL), lambda b, h: (b, 0))]
                 + _w_in_row_specs(layer, (OFF_HQ, OFF_HF, OFF_HI, OFF_HG), _prompt_unit_rows)
                 + [pl.BlockSpec((DEPTH, HG_DK), lambda b, h: (0, h)),
                  pl.BlockSpec((1, HG_DK), lambda b, h: (0, 0)),
                  pl.BlockSpec(level.shape, lambda b, h: (0, 0))],
        out_specs=(pl.BlockSpec((SEQ, HG_DK), lambda b, h: (b, h)),
                   pl.BlockSpec((None, None, HG_DK, HG_DK), lambda b, h: (b, h, 0, 0))),
        scratch_shapes=[pltpu.VMEM((SEQ, HG_DK), F32)] * 3
                       + [pltpu.VMEM((SEQ, HG_DK), BF16), pltpu.VMEM((SEQ, HG_DK), F32)],
        compiler_params=_cparams(2),
        name="hgrn_prompt",
    )(hin, w_in_t, w_in_t, w_in_t, w_in_t, p["hg_lb_logits"], p["hg_norm_w"].reshape(1, HG_DK),
      jnp.asarray(level))


def _hgrn_sample_kernel(layer, q_ref, f_ref, i_ref, g_ref, lbl_ref, nw_ref, s_ref, out_ref, so_ref, o_s):
    q, f, k = _hgrn_gates(layer, q_ref[...], f_ref[...], lbl_ref)
    qc, fc, kc = _rows_to_cols(q), _rows_to_cols(f), _rows_to_cols(k)
    nb = q.shape[0]
    for j in range(nb):
        s_new = s_ref[j] * fc[:, j:j + 1] + kc[:, j:j + 1] * i_ref[j:j + 1, :]
        so_ref[j] = s_new
        o_s[j:j + 1, :] = jnp.sum(s_new * qc[:, j:j + 1], axis=0, keepdims=True)
    out_ref[...] = _rms_rows(o_s[...]) * nw_ref[...] * jax.nn.silu(g_ref[...])


def hgrn_sample(layer, proj, p, state, state_out):
    cb = lambda off: off // HG_DK
    nb = SAMPLE_BLK
    col = lambda off: pl.BlockSpec((nb, HG_DK), lambda h, i: (i, cb(off) + h))
    st_spec = pl.BlockSpec((None, nb, None, HG_DK, HG_DK), lambda h, i: (layer, i, h, 0, 0))
    args = [proj, proj, proj, proj, p["hg_lb_logits"], p["hg_norm_w"].reshape(1, HG_DK), state]
    in_specs = [col(OFF_HQ), col(OFF_HF), col(OFF_HI), col(OFF_HG),
                pl.BlockSpec((DEPTH, HG_DK), lambda h, i: (0, h)),
                pl.BlockSpec((1, HG_DK), lambda h, i: (0, 0)),
                st_spec]
    aliases = {}
    kern = functools.partial(_hgrn_sample_kernel, layer)
    if state_out is not None:
        args.append(state_out)
        in_specs.append(pl.BlockSpec(memory_space=pl.ANY))
        aliases = {len(args) - 1: 1}
        kern = functools.partial(_drop_alias_arg, kern, 7)
    return pl.pallas_call(
        kern,
        out_shape=(jax.ShapeDtypeStruct((DEC_BATCH, BRANCH_W), F32),
                   jax.ShapeDtypeStruct(state.shape, F32)),
        grid=(HG_HEADS, DEC_BATCH // nb),
        in_specs=in_specs,
        out_specs=(pl.BlockSpec((nb, HG_DK), lambda h, i: (i, h)), st_spec),
        scratch_shapes=[pltpu.VMEM((nb, HG_DK), F32)],
        input_output_aliases=aliases,
        compiler_params=_cparams(2),
        name="hgrn_sample",
    )(*args)


def _drop_alias_arg(kern, pos, *refs):
    return kern(*refs[:pos], *refs[pos + 1:])


def _head_pair(cols, h0, lane_lo):
    return jnp.where(lane_lo, cols[:, h0:h0 + 1], cols[:, h0 + 1:h0 + 2])


def _ssd_prompt_kernel(z_ref, x_ref, bc_ref, dt_ref, cw_ref, cb_ref, dtb_ref, alog_ref, dpar_ref, nw_ref,
                       tril_ref, out_ref, st_ref, cst_ref, cx_s, cbc_s, s_s):
    c = SSD_CHUNK
    ci = pl.program_id(1)

    @pl.when(ci == 0)
    def _():
        cx_s[...] = jnp.zeros_like(cx_s)
        cbc_s[...] = jnp.zeros_like(cbc_s)
        s_s[...] = jnp.zeros_like(s_s)

    cw = cw_ref[...]
    cbias = cb_ref[...]

    def conv_silu(raw, carry_ref, lo, hi):
        xx = jnp.concatenate([carry_ref[...], raw], axis=0)
        y = cbias[:, lo:hi] + cw[CONV_W - 1:CONV_W, lo:hi] * raw
        for d in range(1, CONV_W):
            y = y + cw[CONV_W - 1 - d:CONV_W - d, lo:hi] * pltpu.roll(xx, d, axis=0)[SUBLANES:]
        carry_ref[...] = raw[c - SUBLANES:, :]
        return jax.nn.silu(y)

    x_raw = x_ref[...]
    bc_raw = bc_ref[...]
    xs = conv_silu(x_raw, cx_s, 0, BRANCH_W)
    bc = conv_silu(bc_raw, cbc_s, BRANCH_W, SSD_CONV_DIM)

    dt = jax.nn.softplus(dt_ref[...] + dtb_ref[...])
    a_neg = -jnp.exp(alog_ref[...])
    logd = dt * a_neg
    b = jnp.dot(tril_ref[...], logd, precision=HIGHEST, preferred_element_type=F32)
    b_t = b.T
    bl = b[c - 1:c, :]
    e_in = jnp.exp(b)
    w_out = jnp.exp(bl - b)
    e_last = jnp.exp(bl)
    dfull = dpar_ref[...]

    tri = lax.broadcasted_iota(jnp.int32, (c, c), 0) >= lax.broadcasted_iota(jnp.int32, (c, c), 1)
    lane_lo = lax.broadcasted_iota(jnp.int32, (c, LANES), 1) < SSD_HEADDIM
    lane_lo_row = lax.broadcasted_iota(jnp.int32, (1, LANES), 1) < SSD_HEADDIM

    ys = []
    for g in range(SSD_GROUPS):
        bm = bc[:, g * SSD_STATE:(g + 1) * SSD_STATE].astype(BF16)
        cm = bc[:, (SSD_GROUPS + g) * SSD_STATE:(SSD_GROUPS + g + 1) * SSD_STATE].astype(BF16)
        gmat = _nt_dot(cm, bm)
        for pp in range(SSD_HEADS // SSD_GROUPS // 2):
            pi = g * (SSD_HEADS // SSD_GROUPS // 2) + pp
            h0 = 2 * pi
            xs_p = xs[:, pi * LANES:(pi + 1) * LANES]
            vdt = xs_p * _head_pair(dt, h0, lane_lo)
            vdt_b = vdt.astype(BF16)
            o_heads = []
            for hh in (h0, h0 + 1):
                diff = b[:, hh:hh + 1] - b_t[hh:hh + 1, :]
                dec = jnp.where(tri, jnp.exp(jnp.where(tri, diff, 0.0)), 0.0)
                o_heads.append(jnp.dot((gmat * dec).astype(BF16), vdt_b, preferred_element_type=F32))
            o_intra = jnp.where(lane_lo, o_heads[0], o_heads[1])
            s_p = s_s[pi]
            o_inter = _head_pair(e_in, h0, lane_lo) * jnp.dot(cm, s_p.astype(BF16), preferred_element_type=F32)
            ys.append(o_intra + o_inter + dfull[:, pi * LANES:(pi + 1) * LANES] * xs_p)
            upd = _tn_dot(bm, (vdt * _head_pair(w_out, h0, lane_lo)).astype(BF16))
            s_s[pi] = s_p * _head_pair(e_last, h0, lane_lo_row) + upd

    y = jnp.concatenate(ys, axis=1) * jax.nn.silu(z_ref[...])
    gw = BRANCH_W // SSD_GROUPS
    nw = nw_ref[...]
    outs = [_rms_rows(y[:, g * gw:(g + 1) * gw]) * nw[:, g * gw:(g + 1) * gw] for g in range(SSD_GROUPS)]
    out_ref[...] = jnp.concatenate(outs, axis=1).astype(out_ref.dtype)

    @pl.when(ci == pl.num_programs(1) - 1)
    def _():
        for pi in range(SSD_HEADS // 2):
            s_t = s_s[pi].T
            st_ref[2 * pi] = s_t[:SSD_HEADDIM, :]
            st_ref[2 * pi + 1] = s_t[SSD_HEADDIM:, :]
        cst_ref[:, 0:BRANCH_W] = x_raw[c - (CONV_W - 1):, :]
        cst_ref[:, BRANCH_W:SSD_CONV_DIM] = bc_raw[c - (CONV_W - 1):, :]


def _pad_lanes(v):
    return jnp.pad(v.astype(F32), (0, LANES - v.shape[0])).reshape(1, LANES)


def _ssd_params(p):
    return (p["ssd_conv_w"], p["ssd_conv_b"].reshape(1, SSD_CONV_DIM), _pad_lanes(p["ssd_dt_bias"]),
            _pad_lanes(p["ssd_a_log"]), jnp.repeat(p["ssd_d"].astype(F32), SSD_HEADDIM).reshape(1, BRANCH_W),
            p["ssd_norm_w"].reshape(1, BRANCH_W))


def ssd_prompt(proj, p):
    c = SSD_CHUNK
    nc = SEQ // c
    tril = jnp.asarray(np.tril(np.ones((c, c), np.float32)))
    const = lambda shape: pl.BlockSpec(shape, lambda b, i: (0, 0))
    rowblk = lambda w, off: pl.BlockSpec((c, w), lambda b, i: (b * nc + i, (off - OFF_SZ) // w))
    return pl.pallas_call(
        _ssd_prompt_kernel,
        out_shape=(jax.ShapeDtypeStruct((BATCH * SEQ, BRANCH_W), BF16),
                   jax.ShapeDtypeStruct((BATCH, SSD_HEADS, SSD_HEADDIM, SSD_STATE), F32),
                   jax.ShapeDtypeStruct((BATCH, CONV_W - 1, SSD_CONV_DIM), F32)),
        grid=(BATCH, nc),
        in_specs=[rowblk(BRANCH_W, OFF_SZ), rowblk(BRANCH_W, OFF_SX), rowblk(SSD_BC, OFF_SBC),
                  rowblk(LANES, OFF_SDT),
                  const((CONV_W, SSD_CONV_DIM)), const((1, SSD_CONV_DIM)), const((1, LANES)),
                  const((1, LANES)), const((1, BRANCH_W)), const((1, BRANCH_W)), const((c, c))],
        out_specs=(pl.BlockSpec((c, BRANCH_W), lambda b, i: (b * nc + i, 0)),
                   pl.BlockSpec((None, SSD_HEADS, SSD_HEADDIM, SSD_STATE), lambda b, i: (b, 0, 0, 0)),
                   pl.BlockSpec((None, CONV_W - 1, SSD_CONV_DIM), lambda b, i: (b, 0, 0))),
        scratch_shapes=[pltpu.VMEM((SUBLANES, BRANCH_W), F32), pltpu.VMEM((SUBLANES, SSD_BC), F32),
                        pltpu.VMEM((SSD_HEADS // 2, SSD_STATE, LANES), F32)],
        compiler_params=_cparams(2),
        name="ssd_prompt",
    )(proj, proj, proj, proj, *_ssd_params(p), tril)


def _ssd_sample_kernel(z_ref, x_ref, bc_ref, dt_ref, cw_ref, cb_ref, dtb_ref, alog_ref, dpar_ref, nw_ref,
                       bufx_ref, bufbc_ref, s_ref, out_ref, so_ref, nbx_ref, nbbc_ref, y_s):
    cw = cw_ref[...]
    cbias = cb_ref[...]

    def conv_silu(raw, buf_ref, nbuf_ref, lo, hi):
        y = cbias[:, lo:hi] + cw[CONV_W - 1:CONV_W, lo:hi] * raw
        for j in range(CONV_W - 1):
            y = y + cw[j:j + 1, lo:hi] * buf_ref[j]
        for j in range(CONV_W - 2):
            nbuf_ref[j] = buf_ref[j + 1]
        nbuf_ref[CONV_W - 2] = raw
        return jax.nn.silu(y)

    xs = conv_silu(x_ref[...], bufx_ref, nbx_ref, 0, BRANCH_W)
    bc = conv_silu(bc_ref[...], bufbc_ref, nbbc_ref, BRANCH_W, SSD_CONV_DIM)
    dt = jax.nn.softplus(dt_ref[...] + dtb_ref[...])
    decay = jnp.exp(dt * (-jnp.exp(alog_ref[...])))
    nb = xs.shape[0]
    hpg = SSD_HEADS // SSD_GROUPS
    lane_lo = lax.broadcasted_iota(jnp.int32, (nb, LANES), 1) < SSD_HEADDIM
    for pi in range(SSD_HEADS // 2):
        h0 = 2 * pi
        g = h0 // hpg
        xdt_cols = _rows_to_cols(xs[:, pi * LANES:(pi + 1) * LANES] * _head_pair(dt, h0, lane_lo))
        for e in range(2):
            h = h0 + e
            for j in range(nb):
                xcol = xdt_cols[e * SSD_HEADDIM:(e + 1) * SSD_HEADDIM, j:j + 1]
                brow = bc[j:j + 1, g * SSD_STATE:(g + 1) * SSD_STATE]
                so_ref[j, h] = s_ref[j, h] * decay[j:j + 1, h:h + 1] + xcol * brow
    for g in range(SSD_GROUPS):
        cm = bc[:, (SSD_GROUPS + g) * SSD_STATE:(SSD_GROUPS + g + 1) * SSD_STATE].astype(BF16)
        for j in range(nb):
            s_new = so_ref[j, g * hpg:(g + 1) * hpg].reshape(hpg * SSD_HEADDIM, SSD_STATE)
            y_s[j:j + 1, g * hpg * SSD_HEADDIM:(g + 1) * hpg * SSD_HEADDIM] = _nt_dot(cm, s_new.astype(BF16))[j:j + 1, :]
    y = (y_s[...] + dpar_ref[...] * xs) * jax.nn.silu(z_ref[...])
    gw = BRANCH_W // SSD_GROUPS
    nw = nw_ref[...]
    outs = [_rms_rows(y[:, g * gw:(g + 1) * gw]) * nw[:, g * gw:(g + 1) * gw] for g in range(SSD_GROUPS)]
    out_ref[...] = jnp.concatenate(outs, axis=1)


def ssd_sample(layer, proj, p, state, state_out, buf_t):
    nb = SSD_SAMPLE_BLK
    const = lambda shape: pl.BlockSpec(shape, lambda i: (0,) * len(shape))
    rowblk = lambda w, off: pl.BlockSpec((nb, w), lambda i: (i, off // w))
    st_spec = pl.BlockSpec((None, nb, SSD_HEADS, SSD_HEADDIM, SSD_STATE), lambda i: (layer, i, 0, 0, 0))
    bufx_spec = pl.BlockSpec((CONV_W - 1, nb, BRANCH_W), lambda i: (0, i, 0))
    bufbc_spec = pl.BlockSpec((CONV_W - 1, nb, SSD_BC), lambda i: (0, i, BRANCH_W // SSD_BC))
    args = [proj, proj, proj, proj, *_ssd_params(p), buf_t, buf_t, state]
    in_specs = [rowblk(BRANCH_W, OFF_SZ), rowblk(BRANCH_W, OFF_SX), rowblk(SSD_BC, OFF_SBC), rowblk(LANES, OFF_SDT),
                const((CONV_W, SSD_CONV_DIM)), const((1, SSD_CONV_DIM)), const((1, LANES)), const((1, LANES)),
                const((1, BRANCH_W)), const((1, BRANCH_W)), bufx_spec, bufbc_spec, st_spec]
    aliases = {}
    kern = _ssd_sample_kernel
    if state_out is not None:
        args.append(state_out)
        in_specs.append(pl.BlockSpec(memory_space=pl.ANY))
        aliases = {len(args) - 1: 1}
        kern = functools.partial(_drop_alias_arg, kern, 13)
    out, st, nbx, nbbc = pl.pallas_call(
        kern,
        out_shape=(jax.ShapeDtypeStruct((DEC_BATCH, BRANCH_W), F32),
                   jax.ShapeDtypeStruct(state.shape, F32),
                   jax.ShapeDtypeStruct((CONV_W - 1, DEC_BATCH, BRANCH_W), F32),
                   jax.ShapeDtypeStruct((CONV_W - 1, DEC_BATCH, SSD_BC), F32)),
        grid=(DEC_BATCH // nb,),
        in_specs=in_specs,
        out_specs=(pl.BlockSpec((nb, BRANCH_W), lambda i: (i, 0)), st_spec,
                   pl.BlockSpec((CONV_W - 1, nb, BRANCH_W), lambda i: (0, i, 0)),
                   pl.BlockSpec((CONV_W - 1, nb, SSD_BC), lambda i: (0, i, 0))),
        scratch_shapes=[pltpu.VMEM((nb, BRANCH_W), F32)],
        input_output_aliases=aliases,
        compiler_params=_cparams(1),
        name="ssd_sample",
    )(*args)
    return out, st, jnp.concatenate([nbx, nbbc], axis=-1)


def _ret_prompt_kernel(hin_ref, wq_ref, wk_ref, wv_ref, wg_ref, cos_ref, sin_ref, lg_ref, out_ref, st_ref,
                       q_s, k_s, v_s, o_s):
    c = RET_CHUNK
    t_len = hin_ref.shape[0]
    proj = _project(hin_ref, (wq_ref, wk_ref, wv_ref, wg_ref))
    cos = cos_ref[...]
    sin = sin_ref[...]
    q_s[...] = _rope(proj[:, 0:RET_DK], cos, sin)
    k_s[...] = _rope(proj[:, RET_DK:2 * RET_DK], cos, sin) * RET_DK ** -0.5
    v_s[...] = proj[:, 2 * RET_DK:3 * RET_DK].astype(BF16)
    gate = proj[:, 3 * RET_DK:]
    lg = lg_ref[...]
    lg128 = lg[:, :LANES]
    ti = lax.broadcasted_iota(jnp.int32, (c, c), 0)
    si = lax.broadcasted_iota(jnp.int32, (c, c), 1)
    tri = ti >= si
    dec = jnp.where(tri, jnp.exp(jnp.where(tri, (ti - si).astype(F32) * lg, 0.0)), 0.0)
    tt = lax.broadcasted_iota(jnp.int32, (c, LANES), 0).astype(F32)
    g_in = jnp.exp((tt + 1.0) * lg128)
    g_out = jnp.exp((c - 1.0 - tt) * lg128)
    g_all = jnp.exp(float(c) * lg128)
    s = jnp.zeros((RET_DK, RET_DK), F32)
    for ci in range(t_len // c):
        sl = slice(ci * c, (ci + 1) * c)
        qc = q_s[sl, :]
        kc = k_s[sl, :]
        vb = v_s[sl, :]
        scores = _nt_dot(qc.astype(BF16), kc.astype(BF16)) * dec
        o_s[sl, :] = (jnp.dot(scores.astype(BF16), vb, preferred_element_type=F32)
                      + jnp.dot((qc * g_in).astype(BF16), s.astype(BF16), preferred_element_type=F32))
        s = s * g_all + _tn_dot((kc * g_out).astype(BF16), vb)
    out_ref[...] = (_rms_rows(o_s[...]) * jax.nn.silu(gate)).astype(out_ref.dtype)
    st_ref[...] = s


def _log_gamma_rows(width):
    lg = jnp.log1p(-jnp.exp2(-5.0 - jnp.arange(RET_HEADS, dtype=F32)))
    return jnp.broadcast_to(lg[:, None, None], (RET_HEADS, 1, width))


def ret_prompt(layer, hin, w_in_t, cos, sin):
    tab = pl.BlockSpec((SEQ, RET_DK), lambda b, h: (0, 0))
    ret_rows = tuple(RET_COL0 + off for off in (OFF_RQ, OFF_RK, OFF_RV, OFF_RG))
    return pl.pallas_call(
        _ret_prompt_kernel,
        out_shape=(jax.ShapeDtypeStruct((BATCH * SEQ, BRANCH_W), BF16),
                   jax.ShapeDtypeStruct((BATCH, RET_HEADS, RET_DK, RET_DK), F32)),
        grid=(BATCH, RET_HEADS),
        in_specs=[pl.BlockSpec((SEQ, D_MODEL), lambda b, h: (b, 0))]
                 + _w_in_row_specs(layer, ret_rows, _prompt_unit_rows)
                 + [tab, tab, pl.BlockSpec((None, 1, RET_CHUNK), lambda b, h: (h, 0, 0))],
        out_specs=(pl.BlockSpec((SEQ, RET_DK), lambda b, h: (b, h)),
                   pl.BlockSpec((None, None, RET_DK, RET_DK), lambda b, h: (b, h, 0, 0))),
        scratch_shapes=[pltpu.VMEM((SEQ, RET_DK), F32), pltpu.VMEM((SEQ, RET_DK), F32),
                        pltpu.VMEM((SEQ, RET_DK), BF16), pltpu.VMEM((SEQ, RET_DK), F32)],
        compiler_params=_cparams(2),
        name="ret_prompt",
    )(hin, w_in_t, w_in_t, w_in_t, w_in_t, cos, sin, _log_gamma_rows(RET_CHUNK))


def _ret_sample_kernel(q_ref, k_ref, v_ref, g_ref, cos_ref, sin_ref, lg_ref, s_ref, out_ref, so_ref, o_s):
    cos = cos_ref[0:1, :]
    sin = sin_ref[0:1, :]
    q = _rope(q_ref[...], cos, sin)
    k = _rope(k_ref[...], cos, sin) * RET_DK ** -0.5
    v = v_ref[...]
    gamma = jnp.exp(lg_ref[...])
    qc, kc = _rows_to_cols(q), _rows_to_cols(k)
    nb = q.shape[0]
    for j in range(nb):
        s_new = s_ref[j] * gamma + kc[:, j:j + 1] * v[j:j + 1, :]
        so_ref[j] = s_new
        o_s[j:j + 1, :] = jnp.sum(s_new * qc[:, j:j + 1], axis=0, keepdims=True)
    out_ref[...] = _rms_rows(o_s[...]) * jax.nn.silu(g_ref[...])


def ret_sample(layer, proj, cos, sin, state, state_out):
    cb = lambda off: off // RET_DK
    nb = SAMPLE_BLK
    col = lambda off: pl.BlockSpec((nb, RET_DK), lambda h, i: (i, cb(off) + h))
    tab = pl.BlockSpec((SUBLANES, RET_DK), lambda h, i: (0, 0))
    st_spec = pl.BlockSpec((None, nb, None, RET_DK, RET_DK), lambda h, i: (layer, i, h, 0, 0))
    args = [proj, proj, proj, proj, cos, sin, _log_gamma_rows(LANES), state]
    in_specs = [col(OFF_RQ), col(OFF_RK), col(OFF_RV), col(OFF_RG), tab, tab,
                pl.BlockSpec((None, 1, LANES), lambda h, i: (h, 0, 0)), st_spec]
    aliases = {}
    kern = _ret_sample_kernel
    if state_out is not None:
        args.append(state_out)
        in_specs.append(pl.BlockSpec(memory_space=pl.ANY))
        aliases = {len(args) - 1: 1}
        kern = functools.partial(_drop_alias_arg, kern, len(args) - 1)
    return pl.pallas_call(
        kern,
        out_shape=(jax.ShapeDtypeStruct((DEC_BATCH, BRANCH_W), F32),
                   jax.ShapeDtypeStruct(state.shape, F32)),
        grid=(RET_HEADS, DEC_BATCH // nb),
        in_specs=in_specs,
        out_specs=(pl.BlockSpec((nb, RET_DK), lambda h, i: (i, h)), st_spec),
        scratch_shapes=[pltpu.VMEM((nb, RET_DK), F32)],
        input_output_aliases=aliases,
        compiler_params=_cparams(2),
        name="ret_sample",
    )(*args)


def kernel(x_prompt, x_sample, state_lru_h, state_lru_conv, state_hgrn, state_ssd, state_ssd_conv, state_ret, state_ffn_conv, g_mix, g_ffn, w_in, lru_conv_w, lru_conv_b, lru_wa, lru_ba, lru_wx, lru_bx, lru_lambda, hg_lb_logits, hg_norm_w, ssd_conv_w, ssd_conv_b, ssd_dt_bias, ssd_a_log, ssd_d, ssd_norm_w, w_branch, w_gate, w_out, ffn_w_up, ffn_w_val, ffn_conv_w, ffn_conv_b, ffn_w_down, g_final):
    xp = x_prompt.reshape(BATCH * SEQ, D_MODEL)
    xs = x_sample.reshape(DEC_BATCH, D_MODEL)

    cos_p, sin_p = rope_tables(SEQ, 0, True)
    cos_s, sin_s = rope_tables(SUBLANES, PAST_LEN, False)

    hp = rmsnorm(xp, g_mix[0], BF16, 512)
    hs = rmsnorm(xs, g_mix[0], BF16, DEC_BATCH)

    state_ssd_t = jnp.swapaxes(state_ssd, -1, -2)

    w_in_t = jnp.swapaxes(w_in, 1, 2)
    w_gate_b = jnp.transpose(w_gate, (0, 2, 1, 3)).astype(BF16)
    w_out_b = w_out.astype(BF16)
    w_down_b = ffn_w_down.astype(BF16)

    prompt_states, sample_small = [], []
    hg_out = ssd_out = ret_out = None
    for l in range(DEPTH):
        p = {"lru_conv_w": lru_conv_w[l], "lru_conv_b": lru_conv_b[l], "lru_wa": lru_wa[l], "lru_ba": lru_ba[l],
             "lru_wx": lru_wx[l], "lru_bx": lru_bx[l], "lru_lambda": lru_lambda[l],
             "hg_lb_logits": hg_lb_logits, "hg_norm_w": hg_norm_w[l],
             "ssd_conv_w": ssd_conv_w[l], "ssd_conv_b": ssd_conv_b[l], "ssd_dt_bias": ssd_dt_bias[l],
             "ssd_a_log": ssd_a_log[l], "ssd_d": ssd_d[l], "ssd_norm_w": ssd_norm_w[l]}
        last = l == DEPTH - 1
        g_next = g_final if last else g_mix[l + 1]

        out_a, lru_h_p, lru_conv_p = lru_prompt(l, hp, w_in_t, p)
        out_b, hg_p = hgrn_prompt(l, hp, w_in_t, p)
        proj_ssd_p, _ = in_proj_prompt(l, hp, w_in_t, OFF_SZ, SSD_TILES, SSD_TN, 512)
        out_c, ssd_p, ssd_conv_p = ssd_prompt(proj_ssd_p, p)
        out_d, ret_p = ret_prompt(l, hp, w_in_t, cos_p, sin_p)
        merged, w_branch_b = gated_merge_prompt(l, hp, (out_a, out_b, out_c, out_d), w_gate_b, w_branch, 512, 256)
        xp, h2 = out_proj_residual_norm(l, merged, w_out_b, xp, g_ffn[l], 512)
        act, ffn_conv_p, w_up_b, w_val_b = ffn_prompt(l, h2, ffn_w_up, ffn_w_val, ffn_conv_w[l], ffn_conv_b[l], 256)
        res = down_proj_residual_norm(l, act, w_down_b, xp, g_next, 512, DOWN_PROJ_TK, not last,
                                      F32 if last else BF16)
        if last:
            (yp,) = res
        else:
            xp, hp = res
        prompt_states.append((lru_h_p, lru_conv_p, hg_p, ssd_p, ssd_conv_p, ret_p, ffn_conv_p))

        proj_s = in_proj_sample(l, hs, w_in_t, 0, MAIN_TILES, MAIN_TN)
        proj_ret_s = in_proj_sample(l, hs, w_in_t, RET_COL0, RET_TILES, RET_TN)
        lru_buf_t = jnp.swapaxes(state_lru_conv[l], 0, 1)
        ssd_buf_t = jnp.swapaxes(state_ssd_conv[l], 0, 1)
        s_a, lru_h_s, lru_nbuf = lru_sample(proj_s, p, state_lru_h[l], lru_buf_t)
        s_b, hg_out = hgrn_sample(l, proj_s, p, state_hgrn, hg_out)
        s_c, ssd_out, ssd_nbuf = ssd_sample(l, proj_s, p, state_ssd_t, ssd_out, ssd_buf_t)
        s_d, ret_out = ret_sample(l, proj_ret_s, cos_s, sin_s, state_ret, ret_out)
        merged_s = gated_merge_sample(l, hs, (s_a, s_b, s_c, s_d), w_gate_b, w_branch_b, 256)
        xs, h2s = out_proj_residual_norm(l, merged_s, w_out_b, xs, g_ffn[l], DEC_BATCH)
        act_s, ffn_nbuf = ffn_sample(l, h2s, w_up_b, w_val_b, ffn_conv_w[l], ffn_conv_b[l], state_ffn_conv, 512)
        res = down_proj_residual_norm(l, act_s, w_down_b, xs, g_next, DEC_BATCH, DOWN_PROJ_TK, not last,
                                      F32 if last else BF16)
        if last:
            (ys,) = res
        else:
            xs, hs = res
        sample_small.append((lru_h_s, jnp.swapaxes(lru_nbuf, 0, 1), jnp.swapaxes(ssd_nbuf, 0, 1), ffn_nbuf))

    stack_p = lambda i: jnp.stack([st[i] for st in prompt_states], axis=0)
    stack_s = lambda i: jnp.stack([st[i] for st in sample_small], axis=0)
    return (yp.reshape(BATCH, SEQ, D_MODEL), ys.reshape(DEC_BATCH, 1, D_MODEL),
            stack_p(0), stack_s(0), stack_p(1), stack_s(1),
            stack_p(2), hg_out, jnp.swapaxes(stack_p(3), -1, -2), jnp.swapaxes(ssd_out, -1, -2),
            stack_p(4), stack_s(2), stack_p(5), ret_out,
            stack_p(6), stack_s(3))
```

```python
import functools
import math

import numpy as np
import jax
import jax.numpy as jnp
from jax import lax
from jax.experimental import pallas as pl
from jax.experimental.pallas import tpu as pltpu

F32 = jnp.float32
BF16 = jnp.bfloat16
HIGHEST = lax.Precision.HIGHEST

D_MODEL = 2048
BATCH = 4
SEQ = 2048
DEPTH = 2
DEC_BATCH = 128
PAST_LEN = 16384
BRANCH_W = D_MODEL // 2
EPS = 1e-6
LRU_BLOCKS = 8
LRU_BLOCK = BRANCH_W // LRU_BLOCKS
LRU_C = 8.0
CONV_W = 4
HG_HEADS = 8
HG_DK = BRANCH_W // HG_HEADS
SSD_HEADDIM = 64
SSD_HEADS = BRANCH_W // SSD_HEADDIM
SSD_GROUPS = 2
SSD_STATE = 128
SSD_BC = 2 * SSD_GROUPS * SSD_STATE
SSD_CONV_DIM = BRANCH_W + SSD_BC
RET_HEADS = 8
RET_DK = BRANCH_W // RET_HEADS
ROPE_BASE = 10000.0
D_FF = 5632
FFN_CONV_W = 3

V7X_VMEM_BYTES = 64 * 1024 * 1024
VMEM_LIMIT_BYTES = V7X_VMEM_BYTES - 8 * 1024 * 1024
LANES = 128
SUBLANES = 8

N_BRANCH = 4
N_IN = 12816
OFF_XA, OFF_YA = 0, 1024
OFF_HQ, OFF_HF, OFF_HI, OFF_HG = 2048, 3072, 4096, 5120
OFF_SZ, OFF_SX, OFF_SBC = 6144, 7168, 8192
OFF_SDT = 8704
MAIN_TN, MAIN_TILES = 1280, 7
N_MAIN = MAIN_TN * MAIN_TILES
RET_COL0 = OFF_SDT + SSD_HEADS
RET_TN, RET_TILES = 1024, 4
SSD_TN, SSD_TILES = 896, 3
SSD_PROJ_W = SSD_TN * SSD_TILES
OFF_RQ, OFF_RK, OFF_RV, OFF_RG = 0, 1024, 2048, 3072
FFN_ROW_CHUNK = 128
DOWN_PROJ_TK = 1408

HG_CHUNK = 128
HG_LEVELS = (1, 2, 4, 8, 16, 32, 64)
SSD_CHUNK = 128
RET_CHUNK = 256
SAMPLE_BLK = 32
SSD_SAMPLE_BLK = 8


def _cparams(n_axes):
    return pltpu.CompilerParams(dimension_semantics=("arbitrary",) * n_axes,
                                vmem_limit_bytes=VMEM_LIMIT_BYTES)


def _rms_rows(x):
    return x * lax.rsqrt(jnp.mean(x * x, axis=-1, keepdims=True) + EPS)


def _shift_rows(x, d, row):
    return jnp.where(row >= d, pltpu.roll(x, d, axis=0), 0.0)


def _nt_dot(a, b):
    return lax.dot_general(a, b, (((1,), (1,)), ((), ())), preferred_element_type=F32)


def _tn_dot(a, b):
    return lax.dot_general(a, b, (((0,), (0,)), ((), ())), preferred_element_type=F32)


def _project(h_ref, w_refs):
    w = jnp.concatenate([w_ref[0].astype(BF16) for w_ref in w_refs], axis=0)
    return _nt_dot(h_ref[...], w)


def _w_in_row_specs(layer, offsets, index_map_for):
    return [pl.BlockSpec((pl.Element(1), pl.Element(LANES), pl.Element(D_MODEL)), index_map_for(layer, off))
            for off in offsets]


def _rows_to_cols(x):
    n = x.shape[0]
    if n < LANES:
        x = jnp.concatenate([x, jnp.zeros((LANES - n, x.shape[1]), x.dtype)], axis=0)
    return x.T


def _norm_kernel(x_ref, g_ref, o_ref):
    o_ref[...] = (_rms_rows(x_ref[...]) * g_ref[...]).astype(o_ref.dtype)


def rmsnorm(x, g, out_dtype, tm):
    m, d = x.shape
    return pl.pallas_call(
        _norm_kernel,
        out_shape=jax.ShapeDtypeStruct((m, d), out_dtype),
        grid=(m // tm,),
        in_specs=[pl.BlockSpec((tm, d), lambda i: (i, 0)),
                  pl.BlockSpec((1, d), lambda i: (0, 0))],
        out_specs=pl.BlockSpec((tm, d), lambda i: (i, 0)),
        compiler_params=_cparams(1),
        name="rmsnorm",
    )(x, g.reshape(1, d))


def _mm_nt_kernel(a_ref, bt_ref, o_ref):
    o_ref[...] = _nt_dot(a_ref[...], bt_ref[...])


def _mm_nt_cast_kernel(a_ref, wt_ref, o_ref, wb_ref):
    @pl.when(pl.program_id(1) == 0)
    def _():
        wb_ref[...] = wt_ref[0].astype(BF16)

    o_ref[...] = _nt_dot(a_ref[...], wb_ref[...])


def in_proj_prompt(layer, h, w_in_t, row0, n_tiles, tn, tm):
    m = h.shape[0]
    n = n_tiles * tn
    w_spec = pl.BlockSpec((pl.Element(1), pl.Element(tn), pl.Element(D_MODEL)),
                          lambda j, i: (layer, pl.multiple_of(row0 + j * tn, SUBLANES), 0))
    return pl.pallas_call(
        _mm_nt_cast_kernel,
        out_shape=(jax.ShapeDtypeStruct((m, n), F32), jax.ShapeDtypeStruct((n, D_MODEL), BF16)),
        grid=(n_tiles, m // tm),
        in_specs=[pl.BlockSpec((tm, D_MODEL), lambda j, i: (i, 0)), w_spec],
        out_specs=(pl.BlockSpec((tm, tn), lambda j, i: (i, j)),
                   pl.BlockSpec((tn, D_MODEL), lambda j, i: (j, 0))),
        compiler_params=_cparams(2),
        name="in_proj_prompt",
    )(h, w_in_t)


def _mm_nt_castw_kernel(a_ref, wt_ref, o_ref):
    o_ref[...] = _nt_dot(a_ref[...], wt_ref[0].astype(BF16))


def in_proj_sample(layer, h, w_in_t, row0, n_tiles, tn):
    m = h.shape[0]
    w_spec = pl.BlockSpec((pl.Element(1), pl.Element(tn), pl.Element(D_MODEL)),
                          lambda j: (layer, pl.multiple_of(row0 + j * tn, SUBLANES), 0))
    return pl.pallas_call(
        _mm_nt_castw_kernel,
        out_shape=jax.ShapeDtypeStruct((m, n_tiles * tn), F32),
        grid=(n_tiles,),
        in_specs=[pl.BlockSpec((m, D_MODEL), lambda j: (0, 0)), w_spec],
        out_specs=pl.BlockSpec((m, tn), lambda j: (0, j)),
        compiler_params=_cparams(1),
        name="in_proj_sample",
    )(h, w_in_t)


def _gated_sum(h, br_refs, gate_w, branch_w):
    acc = None
    for k, br_ref in enumerate(br_refs):
        gate = jax.nn.sigmoid(jnp.dot(h, gate_w(k), preferred_element_type=F32))
        br = jnp.dot(br_ref[...].astype(BF16), branch_w(k), preferred_element_type=F32)
        acc = gate * br if acc is None else acc + gate * br
    return acc


def _merge_cast_kernel(h_ref, a_ref, b_ref, c_ref, d_ref, wg_ref, wb_ref, o_ref, wbb_ref):
    @pl.when(pl.program_id(1) == 0)
    def _():
        for k in range(N_BRANCH):
            wbb_ref[k] = wb_ref[k].astype(BF16)

    acc = _gated_sum(h_ref[...], (a_ref, b_ref, c_ref, d_ref), lambda k: wg_ref[k], lambda k: wbb_ref[k])
    o_ref[...] = acc.astype(o_ref.dtype)


def gated_merge_prompt(layer, h, branches, w_gate_b, w_branch, tm, tn):
    m = h.shape[0]
    br_specs = [pl.BlockSpec((tm, BRANCH_W), lambda j, i: (i, 0)) for _ in range(N_BRANCH)]
    return pl.pallas_call(
        _merge_cast_kernel,
        out_shape=(jax.ShapeDtypeStruct((m, D_MODEL), BF16),
                   jax.ShapeDtypeStruct((N_BRANCH, BRANCH_W, D_MODEL), BF16)),
        grid=(D_MODEL // tn, m // tm),
        in_specs=[pl.BlockSpec((tm, D_MODEL), lambda j, i: (i, 0))] + br_specs
                 + [pl.BlockSpec((None, N_BRANCH, D_MODEL, tn), lambda j, i: (layer, 0, 0, j)),
                    pl.BlockSpec((None, N_BRANCH, BRANCH_W, tn), lambda j, i: (layer, 0, 0, j))],
        out_specs=(pl.BlockSpec((tm, tn), lambda j, i: (i, j)),
                   pl.BlockSpec((N_BRANCH, BRANCH_W, tn), lambda j, i: (0, 0, j))),
        compiler_params=_cparams(2),
        name="gated_merge_prompt",
    )(h, *branches, w_gate_b, w_branch)


def _merge_kernel(h_ref, a_ref, b_ref, c_ref, d_ref, wgb_ref, wbb_ref, o_ref):
    acc = _gated_sum(h_ref[...], (a_ref, b_ref, c_ref, d_ref), lambda k: wgb_ref[k], lambda k: wbb_ref[k])
    o_ref[...] = acc.astype(o_ref.dtype)


def gated_merge_sample(layer, h, branches, w_gate_b, w_branch_b, tn):
    m = h.shape[0]
    br_specs = [pl.BlockSpec((m, BRANCH_W), lambda j: (0, 0)) for _ in range(N_BRANCH)]
    return pl.pallas_call(
        _merge_kernel,
        out_shape=jax.ShapeDtypeStruct((m, D_MODEL), BF16),
        grid=(D_MODEL // tn,),
        in_specs=[pl.BlockSpec((m, D_MODEL), lambda j: (0, 0))] + br_specs
                 + [pl.BlockSpec((None, N_BRANCH, D_MODEL, tn), lambda j: (layer, 0, 0, j)),
                    pl.BlockSpec((N_BRANCH, BRANCH_W, tn), lambda j: (0, 0, j))],
        out_specs=pl.BlockSpec((m, tn), lambda j: (0, j)),
        compiler_params=_cparams(1),
        name="gated_merge_sample",
    )(h, *branches, w_gate_b, w_branch_b)


def _out_proj_kernel(m_ref, w_ref, x_ref, g_ref, xo_ref, ho_ref):
    x_new = x_ref[...] + jnp.dot(m_ref[...], w_ref[...], preferred_element_type=F32)
    xo_ref[...] = x_new
    ho_ref[...] = (_rms_rows(x_new) * g_ref[...]).astype(ho_ref.dtype)


def out_proj_residual_norm(layer, merged, w_out, x, g, tm):
    m = x.shape[0]
    return pl.pallas_call(
        _out_proj_kernel,
        out_shape=(jax.ShapeDtypeStruct((m, D_MODEL), F32), jax.ShapeDtypeStruct((m, D_MODEL), BF16)),
        grid=(m // tm,),
        in_specs=[pl.BlockSpec((tm, D_MODEL), lambda i: (i, 0)),
                  pl.BlockSpec((None, D_MODEL, D_MODEL), lambda i: (layer, 0, 0)),
                  pl.BlockSpec((tm, D_MODEL), lambda i: (i, 0)),
                  pl.BlockSpec((1, D_MODEL), lambda i: (0, 0))],
        out_specs=(pl.BlockSpec((tm, D_MODEL), lambda i: (i, 0)),
                   pl.BlockSpec((tm, D_MODEL), lambda i: (i, 0))),
        compiler_params=_cparams(1),
        name="out_proj",
    )(merged, w_out, x, g.reshape(1, D_MODEL))


def _down_proj_kernel(emit_x, a_ref, w_ref, x_ref, g_ref, *refs):
    if emit_x:
        xo_ref, no_ref, acc_ref = refs
    else:
        no_ref, acc_ref = refs
    kk = pl.program_id(1)

    @pl.when(kk == 0)
    def _():
        acc_ref[...] = x_ref[...]

    acc_ref[...] += jnp.dot(a_ref[...], w_ref[...], preferred_element_type=F32)

    @pl.when(kk == pl.num_programs(1) - 1)
    def _():
        x_new = acc_ref[...]
        if emit_x:
            xo_ref[...] = x_new
        no_ref[...] = (_rms_rows(x_new) * g_ref[...]).astype(no_ref.dtype)


def down_proj_residual_norm(layer, a, w_down, x, g, tm, tk, emit_x, norm_dtype):
    m = x.shape[0]
    out_shape = [jax.ShapeDtypeStruct((m, D_MODEL), norm_dtype)]
    out_specs = [pl.BlockSpec((tm, D_MODEL), lambda i, k: (i, 0))]
    if emit_x:
        out_shape = [jax.ShapeDtypeStruct((m, D_MODEL), F32)] + out_shape
        out_specs = [pl.BlockSpec((tm, D_MODEL), lambda i, k: (i, 0))] + out_specs
    return pl.pallas_call(
        functools.partial(_down_proj_kernel, emit_x),
        out_shape=tuple(out_shape),
        grid=(m // tm, D_FF // tk),
        in_specs=[pl.BlockSpec((tm, tk), lambda i, k: (i, k)),
                  pl.BlockSpec((None, tk, D_MODEL), lambda i, k: (layer, k, 0)),
                  pl.BlockSpec((tm, D_MODEL), lambda i, k: (i, 0)),
                  pl.BlockSpec((1, D_MODEL), lambda i, k: (0, 0))],
        out_specs=tuple(out_specs),
        scratch_shapes=[pltpu.VMEM((tm, D_MODEL), F32)],
        compiler_params=_cparams(2),
        name="down_proj",
    )(a, w_down, x, g.reshape(1, D_MODEL))


def _ffn_prompt_kernel(h_ref, wu_ref, wv_ref, cw_ref, cb_ref, a_ref, st_ref, wub_ref, wvb_ref):
    @pl.when(pl.program_id(1) == 0)
    def _():
        wub_ref[...] = wu_ref[...].astype(BF16)
        wvb_ref[...] = wv_ref[...].astype(BF16)

    rc = FFN_ROW_CHUNK
    tn = a_ref.shape[1]
    cw = cw_ref[...]
    cbias = cb_ref[...]
    row8 = lax.broadcasted_iota(jnp.int32, (SUBLANES, tn), 0)
    tail = jnp.zeros((SUBLANES, tn), F32)
    pending = None
    for c in range(h_ref.shape[0] // rc):
        rows = slice(c * rc, (c + 1) * rc)
        hc = h_ref[rows, :]
        u = jnp.dot(hc, wub_ref[...], preferred_element_type=F32)
        if pending is not None:
            prev_rows, g_prev, v_prev = pending
            a_ref[prev_rows, :] = (g_prev * v_prev).astype(a_ref.dtype)
        v = jnp.dot(hc, wvb_ref[...], preferred_element_type=F32)
        uc = cbias + cw[FFN_CONV_W - 1:FFN_CONV_W] * u
        for d in range(1, FFN_CONV_W):
            rolled = pltpu.roll(u, d, axis=0)
            top = jnp.where(row8 >= d, rolled[:SUBLANES], pltpu.roll(tail, d, axis=0))
            uc = uc + cw[FFN_CONV_W - 1 - d:FFN_CONV_W - d] * jnp.concatenate([top, rolled[SUBLANES:]], axis=0)
        pending = (rows, jax.nn.gelu(uc), v)
        tail = u[rc - SUBLANES:, :]
    prev_rows, g_prev, v_prev = pending
    a_ref[prev_rows, :] = (g_prev * v_prev).astype(a_ref.dtype)
    st_ref[...] = tail[SUBLANES - (FFN_CONV_W - 1):, :]


def ffn_prompt(layer, h2, w_up, w_val, conv_w, conv_b, tn):
    wspec = pl.BlockSpec((None, D_MODEL, tn), lambda j, b: (layer, 0, j))
    wbspec = pl.BlockSpec((D_MODEL, tn), lambda j, b: (0, j))
    return pl.pallas_call(
        _ffn_prompt_kernel,
        out_shape=(jax.ShapeDtypeStruct((BATCH * SEQ, D_FF), BF16),
                   jax.ShapeDtypeStruct((BATCH, FFN_CONV_W - 1, D_FF), F32),
                   jax.ShapeDtypeStruct((D_MODEL, D_FF), BF16),
                   jax.ShapeDtypeStruct((D_MODEL, D_FF), BF16)),
        grid=(D_FF // tn, BATCH),
        in_specs=[pl.BlockSpec((SEQ, D_MODEL), lambda j, b: (b, 0)), wspec, wspec,
                  pl.BlockSpec((FFN_CONV_W, tn), lambda j, b: (0, j)),
                  pl.BlockSpec((1, tn), lambda j, b: (0, j))],
        out_specs=(pl.BlockSpec((SEQ, tn), lambda j, b: (b, j)),
                   pl.BlockSpec((None, FFN_CONV_W - 1, tn), lambda j, b: (b, 0, j)),
                   wbspec, wbspec),
        compiler_params=_cparams(2),
        name="ffn_prompt",
    )(h2, w_up, w_val, conv_w, conv_b.reshape(1, D_FF))


def _ffn_sample_kernel(h_ref, wu_ref, wv_ref, cw_ref, cb_ref, buf_ref, a_ref, nb_ref):
    h = h_ref[...]
    u = jnp.dot(h, wu_ref[...], preferred_element_type=F32)
    v = jnp.dot(h, wv_ref[...], preferred_element_type=F32)
    cw = cw_ref[...]
    b0 = buf_ref[:, 0, :]
    b1 = buf_ref[:, 1, :]
    uc = cb_ref[...] + cw[0:1] * b0 + cw[1:2] * b1 + cw[2:3] * u
    a_ref[...] = (jax.nn.gelu(uc) * v).astype(a_ref.dtype)
    nb_ref[:, 0, :] = b1
    nb_ref[:, 1, :] = u


def ffn_sample(layer, h2, w_up, w_val, conv_w, conv_b, buf, tn):
    return pl.pallas_call(
        _ffn_sample_kernel,
        out_shape=(jax.ShapeDtypeStruct((DEC_BATCH, D_FF), BF16),
                   jax.ShapeDtypeStruct((DEC_BATCH, FFN_CONV_W - 1, D_FF), F32)),
        grid=(D_FF // tn,),
        in_specs=[pl.BlockSpec((DEC_BATCH, D_MODEL), lambda j: (0, 0)),
                  pl.BlockSpec((D_MODEL, tn), lambda j: (0, j)),
                  pl.BlockSpec((D_MODEL, tn), lambda j: (0, j)),
                  pl.BlockSpec((FFN_CONV_W, tn), lambda j: (0, j)),
                  pl.BlockSpec((1, tn), lambda j: (0, j)),
                  pl.BlockSpec((None, DEC_BATCH, FFN_CONV_W - 1, tn), lambda j: (layer, 0, 0, j))],
        out_specs=(pl.BlockSpec((DEC_BATCH, tn), lambda j: (0, j)),
                   pl.BlockSpec((DEC_BATCH, FFN_CONV_W - 1, tn), lambda j: (0, 0, j))),
        compiler_params=_cparams(1),
        name="ffn_sample",
    )(h2, w_up, w_val, conv_w, conv_b.reshape(1, D_FF), buf)


def _rope_table_kernel(start, consecutive, freq_ref, sign_ref, cos_ref, sin_ref):
    shape = cos_ref.shape
    if consecutive:
        pos = lax.broadcasted_iota(jnp.int32, shape, 0).astype(F32) + float(start)
    else:
        pos = jnp.full(shape, float(start), F32)
    ang = pos * freq_ref[...]
    cos_ref[...] = jnp.cos(ang)
    sin_ref[...] = sign_ref[...] * jnp.sin(ang)


def rope_tables(n_rows, start, consecutive):
    half = RET_DK // 2
    freqs = ROPE_BASE ** (-jnp.arange(half, dtype=F32) / half)
    freq2 = jnp.concatenate([freqs, freqs]).reshape(1, RET_DK)
    sign = jnp.concatenate([-jnp.ones((half,), F32), jnp.ones((half,), F32)]).reshape(1, RET_DK)
    return pl.pallas_call(
        functools.partial(_rope_table_kernel, start, consecutive),
        out_shape=(jax.ShapeDtypeStruct((n_rows, RET_DK), F32), jax.ShapeDtypeStruct((n_rows, RET_DK), F32)),
        name="rope_tables",
    )(freq2, sign)


def _rope(x, cos, sin_signed):
    return x * cos + pltpu.roll(x, RET_DK // 2, axis=1) * sin_signed


def _lru_gates(conv, wa_ref, ba_ref, wx_ref, bx_ref, lam_ref):
    xb = conv.astype(BF16)
    r = jax.nn.sigmoid(jnp.dot(xb, wa_ref[...].astype(BF16), preferred_element_type=F32) + ba_ref[...])
    i = jax.nn.sigmoid(jnp.dot(xb, wx_ref[...].astype(BF16), preferred_element_type=F32) + bx_ref[...])
    log_a = -LRU_C * r * jax.nn.softplus(-lam_ref[...])
    a = jnp.exp(log_a)
    u = jnp.sqrt(1.0 - a * a) * (i * conv)
    return a, u


def _lru_prompt_kernel(hin_ref, wxa_ref, wya_ref, cw_ref, cb_ref, wa_ref, ba_ref, wx_ref, bx_ref, lam_ref,
                       out_ref, h_ref, conv_ref, ag_s, ug_s):
    proj = _project(hin_ref, (wxa_ref, wya_ref))
    x = proj[:, :LRU_BLOCK]
    ya = proj[:, LRU_BLOCK:]
    t_len = x.shape[0]
    row = lax.broadcasted_iota(jnp.int32, x.shape, 0)
    cw = cw_ref[...]
    conv = cb_ref[...] + cw[CONV_W - 1:CONV_W] * x
    for d in range(1, CONV_W):
        conv = conv + cw[CONV_W - 1 - d:CONV_W - d] * _shift_rows(x, d, row)
    a, u = _lru_gates(conv, wa_ref, ba_ref, wx_ref, bx_ref, lam_ref)
    ng = t_len // SUBLANES
    a3 = a.reshape(ng, SUBLANES, LRU_BLOCK)
    u3 = u.reshape(ng, SUBLANES, LRU_BLOCK)
    sub = lax.broadcasted_iota(jnp.int32, a3.shape, 1)
    d = 1
    while d < SUBLANES:
        keep = sub >= d
        u3 = jnp.where(keep, a3 * pltpu.roll(u3, d, axis=1) + u3, u3)
        a3 = jnp.where(keep, a3 * pltpu.roll(a3, d, axis=1), a3)
        d *= 2
    ag_s[...] = a3.reshape(t_len, LRU_BLOCK)
    ug_s[...] = u3.reshape(t_len, LRU_BLOCK)
    ag = ag_s[pl.ds(SUBLANES - 1, ng, stride=SUBLANES), :]
    ug = ug_s[pl.ds(SUBLANES - 1, ng, stride=SUBLANES), :]
    grow = lax.broadcasted_iota(jnp.int32, ag.shape, 0)
    d = 1
    while d < ng:
        keep = grow >= d
        ug = jnp.where(keep, ag * pltpu.roll(ug, d, axis=0) + ug, ug)
        ag = jnp.where(keep, ag * pltpu.roll(ag, d, axis=0), ag)
        d *= 2
    carry = _shift_rows(ug, 1, grow)
    h3 = a3 * jnp.broadcast_to(carry[:, None, :], a3.shape) + u3
    hs = h3.reshape(t_len, LRU_BLOCK)
    out_ref[...] = (hs * jax.nn.gelu(ya)).astype(out_ref.dtype)
    h_ref[...] = ug[ng - 1:, :]
    conv_ref[...] = x[t_len - (CONV_W - 1):, :]


def _lru_param_specs(n_axes_fn):
    blk3 = lambda shape: pl.BlockSpec(shape, n_axes_fn(lambda n: (n, 0, 0)))
    return [pl.BlockSpec((CONV_W, LRU_BLOCK), n_axes_fn(lambda n: (0, n))),
            pl.BlockSpec((1, LRU_BLOCK), n_axes_fn(lambda n: (0, n))),
            blk3((None, LRU_BLOCK, LRU_BLOCK)), blk3((None, 1, LRU_BLOCK)),
            blk3((None, LRU_BLOCK, LRU_BLOCK)), blk3((None, 1, LRU_BLOCK)),
            blk3((None, 1, LRU_BLOCK))]


def _lru_params(p):
    return (p["lru_conv_w"], p["lru_conv_b"].reshape(1, BRANCH_W),
            p["lru_wa"], p["lru_ba"].reshape(LRU_BLOCKS, 1, LRU_BLOCK),
            p["lru_wx"], p["lru_bx"].reshape(LRU_BLOCKS, 1, LRU_BLOCK),
            p["lru_lambda"].reshape(LRU_BLOCKS, 1, LRU_BLOCK))


def _prompt_unit_rows(layer, off):
    return lambda b, u: (layer, pl.multiple_of(off + u * LANES, SUBLANES), 0)


def lru_prompt(layer, hin, w_in_t, p):
    wrap = lambda f: (lambda b, n: f(n))
    out, h, conv = pl.pallas_call(
        _lru_prompt_kernel,
        out_shape=(jax.ShapeDtypeStruct((BATCH * SEQ, BRANCH_W), BF16),
                   jax.ShapeDtypeStruct((BATCH, 1, BRANCH_W), F32),
                   jax.ShapeDtypeStruct((BATCH, CONV_W - 1, BRANCH_W), F32)),
        grid=(BATCH, LRU_BLOCKS),
        in_specs=[pl.BlockSpec((SEQ, D_MODEL), lambda b, n: (b, 0))]
                 + _w_in_row_specs(layer, (OFF_XA, OFF_YA), _prompt_unit_rows) + _lru_param_specs(wrap),
        out_specs=(pl.BlockSpec((SEQ, LRU_BLOCK), lambda b, n: (b, n)),
                   pl.BlockSpec((None, 1, LRU_BLOCK), lambda b, n: (b, 0, n)),
                   pl.BlockSpec((None, CONV_W - 1, LRU_BLOCK), lambda b, n: (b, 0, n))),
        scratch_shapes=[pltpu.VMEM((SEQ, LRU_BLOCK), F32)] * 2,
        compiler_params=_cparams(2),
        name="lru_prompt",
    )(hin, w_in_t, w_in_t, *_lru_params(p))
    return out, h.reshape(BATCH, BRANCH_W), conv


def _lru_sample_kernel(xa_ref, ya_ref, cw_ref, cb_ref, wa_ref, ba_ref, wx_ref, bx_ref, lam_ref,
                       h0_ref, buf_ref, out_ref, h_ref, nbuf_ref):
    x = xa_ref[...]
    cw = cw_ref[...]
    conv = cb_ref[...] + cw[CONV_W - 1:CONV_W] * x
    for j in range(CONV_W - 1):
        conv = conv + cw[j:j + 1] * buf_ref[j]
    a, u = _lru_gates(conv, wa_ref, ba_ref, wx_ref, bx_ref, lam_ref)
    h = a * h0_ref[...] + u
    out_ref[...] = h * jax.nn.gelu(ya_ref[...])
    h_ref[...] = h
    for j in range(CONV_W - 2):
        nbuf_ref[j] = buf_ref[j + 1]
    nbuf_ref[CONV_W - 2] = x


def lru_sample(proj, p, h0, buf_t):
    cb = lambda off: off // LRU_BLOCK
    wrap = lambda f: f
    return pl.pallas_call(
        _lru_sample_kernel,
        out_shape=(jax.ShapeDtypeStruct((DEC_BATCH, BRANCH_W), F32),
                   jax.ShapeDtypeStruct((DEC_BATCH, BRANCH_W), F32),
                   jax.ShapeDtypeStruct((CONV_W - 1, DEC_BATCH, BRANCH_W), F32)),
        grid=(LRU_BLOCKS,),
        in_specs=[pl.BlockSpec((DEC_BATCH, LRU_BLOCK), lambda n: (0, cb(OFF_XA) + n)),
                  pl.BlockSpec((DEC_BATCH, LRU_BLOCK), lambda n: (0, cb(OFF_YA) + n))]
                 + _lru_param_specs(wrap)
                 + [pl.BlockSpec((DEC_BATCH, LRU_BLOCK), lambda n: (0, n)),
                    pl.BlockSpec((CONV_W - 1, DEC_BATCH, LRU_BLOCK), lambda n: (0, 0, n))],
        out_specs=(pl.BlockSpec((DEC_BATCH, LRU_BLOCK), lambda n: (0, n)),
                   pl.BlockSpec((DEC_BATCH, LRU_BLOCK), lambda n: (0, n)),
                   pl.BlockSpec((CONV_W - 1, DEC_BATCH, LRU_BLOCK), lambda n: (0, 0, n))),
        compiler_params=_cparams(1),
        name="lru_sample",
    )(proj, proj, *_lru_params(p), h0, buf_t)


def _hgrn_lower_bound(layer, logits):
    mx = jnp.max(logits, axis=0, keepdims=True)
    e = jnp.exp(logits - mx)
    ls = e / jnp.sum(e, axis=0, keepdims=True)
    lb = jnp.zeros_like(ls[0:1])
    for i in range(1, layer + 1):
        lb = lb + ls[i:i + 1]
    return lb


def _hgrn_gates(layer, hq, hf, lbl_ref):
    lb = _hgrn_lower_bound(layer, lbl_ref[...])
    q = jax.nn.silu(hq)
    sg = jax.nn.sigmoid(hf)
    f = lb + (1.0 - lb) * sg
    k = (1.0 - lb) * (1.0 - sg)
    return q, f, k


def _hgrn_level_ids():
    c = HG_CHUNK
    t = np.arange(c)[:, None]
    s = np.arange(c)[None, :]
    level = np.zeros((c, c), np.int32)
    for li, m in enumerate(HG_LEVELS):
        same = (t // (2 * m)) == (s // (2 * m))
        level[same & ((t % (2 * m)) >= m) & ((s % (2 * m)) < m)] = li + 1
    level[t == s] = len(HG_LEVELS) + 1
    return level


def _split3_bf16(x):
    hi = x.astype(BF16)
    r1 = x - hi.astype(F32)
    mid = r1.astype(BF16)
    lo = (r1 - mid.astype(F32)).astype(BF16)
    return hi, mid, lo


def _cumsum_rows(tril_b, x):
    return sum(jnp.dot(tril_b, piece, preferred_element_type=F32) for piece in _split3_bf16(x))


def _hgrn_midpoint_factor(b, m):
    c = b.shape[0]
    if 2 * m >= SUBLANES:
        b3 = b.reshape(c // (2 * m), 2 * m, LANES)
        mid = b3[:, m - 1:m, :]
        upper = lax.broadcasted_iota(jnp.int32, b3.shape, 1) >= m
    else:
        b3 = b.reshape(c // SUBLANES, SUBLANES, LANES)
        sub = lax.broadcasted_iota(jnp.int32, b3.shape, 1)
        mid = b3[:, m - 1:m, :]
        for blk in range(1, SUBLANES // (2 * m)):
            lo = blk * 2 * m
            mid = jnp.where(sub >= lo, b3[:, lo + m - 1:lo + m, :], mid)
        upper = jnp.bitwise_and(sub, m) != 0
    return jnp.exp(jnp.where(upper, b3 - mid, mid - b3)).reshape(c, LANES)


def _hgrn_prompt_kernel(layer, hin_ref, wq_ref, wf_ref, wi_ref, wg_ref, lbl_ref, nw_ref, lvl_ref,
                        out_ref, st_ref, lf_s, q_s, k_s, v_s, o_s):
    c = HG_CHUNK
    t_len = hin_ref.shape[0]
    proj = _project(hin_ref, (wq_ref, wf_ref, wi_ref, wg_ref))
    q, f, k = _hgrn_gates(layer, proj[:, 0:HG_DK], proj[:, HG_DK:2 * HG_DK], lbl_ref)
    lf_s[...] = jnp.log(f)
    q_s[...] = q
    k_s[...] = k
    v_s[...] = proj[:, 2 * HG_DK:3 * HG_DK].astype(BF16)
    gate = proj[:, 3 * HG_DK:]

    lvl = lvl_ref[...]
    tril_b = (lax.broadcasted_iota(jnp.int32, (c, c), 0) >= lax.broadcasted_iota(jnp.int32, (c, c), 1)).astype(BF16)

    st = jnp.zeros((HG_DK, HG_DK), F32)
    for ci in range(t_len // c):
        sl = slice(ci * c, (ci + 1) * c)
        qc = q_s[sl, :]
        kc = k_s[sl, :]
        vb = v_s[sl, :]
        b = _cumsum_rows(tril_b, lf_s[sl, :])
        att = jnp.where(lvl == len(HG_LEVELS) + 1, _nt_dot(qc.astype(BF16), kc.astype(BF16)), 0.0)
        for li, m in enumerate(HG_LEVELS):
            e = _hgrn_midpoint_factor(b, m)
            a_l = _nt_dot((qc * e).astype(BF16), (kc * e).astype(BF16))
            att = jnp.where(lvl == li + 1, a_l, att)
        o = (jnp.dot(att.astype(BF16), vb, preferred_element_type=F32)
             + _nt_dot((qc * jnp.exp(b)).astype(BF16), st.astype(BF16)))
        o_s[sl, :] = o
        bl = b[c - 1:c, :]
        kdec = (kc * jnp.exp(bl - b)).astype(BF16)
        st = st * jnp.exp(bl) + _tn_dot(vb, kdec)
    o = o_s[...]
    out_ref[...] = (_rms_rows(o) * nw_ref[...] * jax.nn.silu(gate)).astype(out_ref.dtype)
    st_ref[...] = st.T


def hgrn_prompt(layer, hin, w_in_t, p):
    level = _hgrn_level_ids()
    return pl.pallas_call(
        functools.partial(_hgrn_prompt_kernel, layer),
        out_shape=(jax.ShapeDtypeStruct((BATCH * SEQ, BRANCH_W), BF16),
                   jax.ShapeDtypeStruct((BATCH, HG_HEADS, HG_DK, HG_DK), F32)),
        grid=(BATCH, HG_HEADS),
        in_specs=[pl.BlockSpec((SEQ, D_MODEL), lambda b, h: (b, 0))]
                 + _w_in_row_specs(layer, (OFF_HQ, OFF_HF, OFF_HI, OFF_HG), _prompt_unit_rows)
                 + [pl.BlockSpec((DEPTH, HG_DK), lambda b, h: (0, h)),
                    pl.BlockSpec((1, HG_DK), lambda b, h: (0, 0)),
                    pl.BlockSpec(level.shape, lambda b, h: (0, 0))],
        out_specs=(pl.BlockSpec((SEQ, HG_DK), lambda b, h: (b, h)),
                   pl.BlockSpec((None, None, HG_DK, HG_DK), lambda b, h: (b, h, 0, 0))),
        scratch_shapes=[pltpu.VMEM((SEQ, HG_DK), F32)] * 3
                       + [pltpu.VMEM((SEQ, HG_DK), BF16), pltpu.VMEM((SEQ, HG_DK), F32)],
        compiler_params=_cparams(2),
        name="hgrn_prompt",
    )(hin, w_in_t, w_in_t, w_in_t, w_in_t, p["hg_lb_logits"], p["hg_norm_w"].reshape(1, HG_DK),
      jnp.asarray(level))


def _hgrn_sample_kernel(layer, q_ref, f_ref, i_ref, g_ref, lbl_ref, nw_ref, s_ref, out_ref, so_ref, o_s):
    q, f, k = _hgrn_gates(layer, q_ref[...], f_ref[...], lbl_ref)
    qc, fc, kc = _rows_to_cols(q), _rows_to_cols(f), _rows_to_cols(k)
    nb = q.shape[0]
    for j in range(nb):
        s_new = s_ref[j] * fc[:, j:j + 1] + kc[:, j:j + 1] * i_ref[j:j + 1, :]
        so_ref[j] = s_new
        o_s[j:j + 1, :] = jnp.sum(s_new * qc[:, j:j + 1], axis=0, keepdims=True)
    out_ref[...] = _rms_rows(o_s[...]) * nw_ref[...] * jax.nn.silu(g_ref[...])


def hgrn_sample(layer, proj, p, state, state_out):
    cb = lambda off: off // HG_DK
    nb = SAMPLE_BLK
    col = lambda off: pl.BlockSpec((nb, HG_DK), lambda h, i: (i, cb(off) + h))
    st_spec = pl.BlockSpec((None, nb, None, HG_DK, HG_DK), lambda h, i: (layer, i, h, 0, 0))
    args = [proj, proj, proj, proj, p["hg_lb_logits"], p["hg_norm_w"].reshape(1, HG_DK), state]
    in_specs = [col(OFF_HQ), col(OFF_HF), col(OFF_HI), col(OFF_HG),
                pl.BlockSpec((DEPTH, HG_DK), lambda h, i: (0, h)),
                pl.BlockSpec((1, HG_DK), lambda h, i: (0, 0)),
                st_spec]
    aliases = {}
    kern = functools.partial(_hgrn_sample_kernel, layer)
    if state_out is not None:
        args.append(state_out)
        in_specs.append(pl.BlockSpec(memory_space=pl.ANY))
        aliases = {len(args) - 1: 1}
        kern = functools.partial(_drop_alias_arg, kern, 7)
    return pl.pallas_call(
        kern,
        out_shape=(jax.ShapeDtypeStruct((DEC_BATCH, BRANCH_W), F32),
                   jax.ShapeDtypeStruct(state.shape, F32)),
        grid=(HG_HEADS, DEC_BATCH // nb),
        in_specs=in_specs,
        out_specs=(pl.BlockSpec((nb, HG_DK), lambda h, i: (i, h)), st_spec),
        scratch_shapes=[pltpu.VMEM((nb, HG_DK), F32)],
        input_output_aliases=aliases,
        compiler_params=_cparams(2),
        name="hgrn_sample",
    )(*args)


def _drop_alias_arg(kern, pos, *refs):
    return kern(*refs[:pos], *refs[pos + 1:])


def _head_pair(cols, h0, lane_lo):
    return jnp.where(lane_lo, cols[:, h0:h0 + 1], cols[:, h0 + 1:h0 + 2])


def _ssd_prompt_kernel(z_ref, x_ref, bc_ref, dt_ref, cw_ref, cb_ref, dtb_ref, alog_ref, dpar_ref, nw_ref,
                       tril_ref, out_ref, st_ref, cst_ref, cx_s, cbc_s, s_s):
    c = SSD_CHUNK
    ci = pl.program_id(1)

    @pl.when(ci == 0)
    def _():
        cx_s[...] = jnp.zeros_like(cx_s)
        cbc_s[...] = jnp.zeros_like(cbc_s)
        s_s[...] = jnp.zeros_like(s_s)

    cw = cw_ref[...]
    cbias = cb_ref[...]

    def conv_silu(raw, carry_ref, lo, hi):
        xx = jnp.concatenate([carry_ref[...], raw], axis=0)
        y = cbias[:, lo:hi] + cw[CONV_W - 1:CONV_W, lo:hi] * raw
        for d in range(1, CONV_W):
            y = y + cw[CONV_W - 1 - d:CONV_W - d, lo:hi] * pltpu.roll(xx, d, axis=0)[SUBLANES:]
        carry_ref[...] = raw[c - SUBLANES:, :]
        return jax.nn.silu(y)

    x_raw = x_ref[...]
    bc_raw = bc_ref[...]
    xs = conv_silu(x_raw, cx_s, 0, BRANCH_W)
    bc = conv_silu(bc_raw, cbc_s, BRANCH_W, SSD_CONV_DIM)

    dt = jax.nn.softplus(dt_ref[...] + dtb_ref[...])
    a_neg = -jnp.exp(alog_ref[...])
    logd = dt * a_neg
    b = jnp.dot(tril_ref[...], logd, precision=HIGHEST, preferred_element_type=F32)
    b_t = b.T
    bl = b[c - 1:c, :]
    e_in = jnp.exp(b)
    w_out = jnp.exp(bl - b)
    e_last = jnp.exp(bl)
    dfull = dpar_ref[...]

    tri = lax.broadcasted_iota(jnp.int32, (c, c), 0) >= lax.broadcasted_iota(jnp.int32, (c, c), 1)
    lane_lo = lax.broadcasted_iota(jnp.int32, (c, LANES), 1) < SSD_HEADDIM
    lane_lo_row = lax.broadcasted_iota(jnp.int32, (1, LANES), 1) < SSD_HEADDIM

    ys = []
    for g in range(SSD_GROUPS):
        bm = bc[:, g * SSD_STATE:(g + 1) * SSD_STATE].astype(BF16)
        cm = bc[:, (SSD_GROUPS + g) * SSD_STATE:(SSD_GROUPS + g + 1) * SSD_STATE].astype(BF16)
        gmat = _nt_dot(cm, bm)
        for pp in range(SSD_HEADS // SSD_GROUPS // 2):
            pi = g * (SSD_HEADS // SSD_GROUPS // 2) + pp
            h0 = 2 * pi
            xs_p = xs[:, pi * LANES:(pi + 1) * LANES]
            vdt = xs_p * _head_pair(dt, h0, lane_lo)
            vdt_b = vdt.astype(BF16)
            o_heads = []
            for hh in (h0, h0 + 1):
                diff = b[:, hh:hh + 1] - b_t[hh:hh + 1, :]
                dec = jnp.where(tri, jnp.exp(jnp.where(tri, diff, 0.0)), 0.0)
                o_heads.append(jnp.dot((gmat * dec).astype(BF16), vdt_b, preferred_element_type=F32))
            o_intra = jnp.where(lane_lo, o_heads[0], o_heads[1])
            s_p = s_s[pi]
            o_inter = _head_pair(e_in, h0, lane_lo) * jnp.dot(cm, s_p.astype(BF16), preferred_element_type=F32)
            ys.append(o_intra + o_inter + dfull[:, pi * LANES:(pi + 1) * LANES] * xs_p)
            upd = _tn_dot(bm, (vdt * _head_pair(w_out, h0, lane_lo)).astype(BF16))
            s_s[pi] = s_p * _head_pair(e_last, h0, lane_lo_row) + upd

    y = jnp.concatenate(ys, axis=1) * jax.nn.silu(z_ref[...])
    gw = BRANCH_W // SSD_GROUPS
    nw = nw_ref[...]
    outs = [_rms_rows(y[:, g * gw:(g + 1) * gw]) * nw[:, g * gw:(g + 1) * gw] for g in range(SSD_GROUPS)]
    out_ref[...] = jnp.concatenate(outs, axis=1).astype(out_ref.dtype)

    @pl.when(ci == pl.num_programs(1) - 1)
    def _():
        for pi in range(SSD_HEADS // 2):
            s_t = s_s[pi].T
            st_ref[2 * pi] = s_t[:SSD_HEADDIM, :]
            st_ref[2 * pi + 1] = s_t[SSD_HEADDIM:, :]
        cst_ref[:, 0:BRANCH_W] = x_raw[c - (CONV_W - 1):, :]
        cst_ref[:, BRANCH_W:SSD_CONV_DIM] = bc_raw[c - (CONV_W - 1):, :]


def _pad_lanes(v):
    return jnp.pad(v.astype(F32), (0, LANES - v.shape[0])).reshape(1, LANES)


def _ssd_params(p):
    return (p["ssd_conv_w"], p["ssd_conv_b"].reshape(1, SSD_CONV_DIM), _pad_lanes(p["ssd_dt_bias"]),
            _pad_lanes(p["ssd_a_log"]), jnp.repeat(p["ssd_d"].astype(F32), SSD_HEADDIM).reshape(1, BRANCH_W),
            p["ssd_norm_w"].reshape(1, BRANCH_W))


def ssd_prompt(proj, p):
    c = SSD_CHUNK
    nc = SEQ // c
    tril = jnp.asarray(np.tril(np.ones((c, c), np.float32)))
    const = lambda shape: pl.BlockSpec(shape, lambda b, i: (0, 0))
    rowblk = lambda w, off: pl.BlockSpec((c, w), lambda b, i: (b * nc + i, (off - OFF_SZ) // w))
    return pl.pallas_call(
        _ssd_prompt_kernel,
        out_shape=(jax.ShapeDtypeStruct((BATCH * SEQ, BRANCH_W), BF16),
                   jax.ShapeDtypeStruct((BATCH, SSD_HEADS, SSD_HEADDIM, SSD_STATE), F32),
                   jax.ShapeDtypeStruct((BATCH, CONV_W - 1, SSD_CONV_DIM), F32)),
        grid=(BATCH, nc),
        in_specs=[rowblk(BRANCH_W, OFF_SZ), rowblk(BRANCH_W, OFF_SX), rowblk(SSD_BC, OFF_SBC),
                  rowblk(LANES, OFF_SDT),
                  const((CONV_W, SSD_CONV_DIM)), const((1, SSD_CONV_DIM)), const((1, LANES)),
                  const((1, LANES)), const((1, BRANCH_W)), const((1, BRANCH_W)), const((c, c))],
        out_specs=(pl.BlockSpec((c, BRANCH_W), lambda b, i: (b * nc + i, 0)),
                   pl.BlockSpec((None, SSD_HEADS, SSD_HEADDIM, SSD_STATE), lambda b, i: (b, 0, 0, 0)),
                   pl.BlockSpec((None, CONV_W - 1, SSD_CONV_DIM), lambda b, i: (b, 0, 0))),
        scratch_shapes=[pltpu.VMEM((SUBLANES, BRANCH_W), F32), pltpu.VMEM((SUBLANES, SSD_BC), F32),
                        pltpu.VMEM((SSD_HEADS // 2, SSD_STATE, LANES), F32)],
        compiler_params=_cparams(2),
        name="ssd_prompt",
    )(proj, proj, proj, proj, *_ssd_params(p), tril)


def _ssd_sample_kernel(z_ref, x_ref, bc_ref, dt_ref, cw_ref, cb_ref, dtb_ref, alog_ref, dpar_ref, nw_ref,
                       bufx_ref, bufbc_ref, s_ref, out_ref, so_ref, nbx_ref, nbbc_ref, y_s):
    cw = cw_ref[...]
    cbias = cb_ref[...]

    def conv_silu(raw, buf_ref, nbuf_ref, lo, hi):
        y = cbias[:, lo:hi] + cw[CONV_W - 1:CONV_W, lo:hi] * raw
        for j in range(CONV_W - 1):
            y = y + cw[j:j + 1, lo:hi] * buf_ref[j]
        for j in range(CONV_W - 2):
            nbuf_ref[j] = buf_ref[j + 1]
        nbuf_ref[CONV_W - 2] = raw
        return jax.nn.silu(y)

    xs = conv_silu(x_ref[...], bufx_ref, nbx_ref, 0, BRANCH_W)
    bc = conv_silu(bc_ref[...], bufbc_ref, nbbc_ref, BRANCH_W, SSD_CONV_DIM)
    dt = jax.nn.softplus(dt_ref[...] + dtb_ref[...])
    decay = jnp.exp(dt * (-jnp.exp(alog_ref[...])))
    nb = xs.shape[0]
    hpg = SSD_HEADS // SSD_GROUPS
    lane_lo = lax.broadcasted_iota(jnp.int32, (nb, LANES), 1) < SSD_HEADDIM
    for pi in range(SSD_HEADS // 2):
        h0 = 2 * pi
        g = h0 // hpg
        xdt_cols = _rows_to_cols(xs[:, pi * LANES:(pi + 1) * LANES] * _head_pair(dt, h0, lane_lo))
        for e in range(2):
            h = h0 + e
            for j in range(nb):
                xcol = xdt_cols[e * SSD_HEADDIM:(e + 1) * SSD_HEADDIM, j:j + 1]
                brow = bc[j:j + 1, g * SSD_STATE:(g + 1) * SSD_STATE]
                so_ref[j, h] = s_ref[j, h] * decay[j:j + 1, h:h + 1] + xcol * brow
    for g in range(SSD_GROUPS):
        cm = bc[:, (SSD_GROUPS + g) * SSD_STATE:(SSD_GROUPS + g + 1) * SSD_STATE].astype(BF16)
        for j in range(nb):
            s_new = so_ref[j, g * hpg:(g + 1) * hpg].reshape(hpg * SSD_HEADDIM, SSD_STATE)
            y_s[j:j + 1, g * hpg * SSD_HEADDIM:(g + 1) * hpg * SSD_HEADDIM] = _nt_dot(cm, s_new.astype(BF16))[j:j + 1, :]
    y = (y_s[...] + dpar_ref[...] * xs) * jax.nn.silu(z_ref[...])
    gw = BRANCH_W // SSD_GROUPS
    nw = nw_ref[...]
    outs = [_rms_rows(y[:, g * gw:(g + 1) * gw]) * nw[:, g * gw:(g + 1) * gw] for g in range(SSD_GROUPS)]
    out_ref[...] = jnp.concatenate(outs, axis=1)


def ssd_sample(layer, proj, p, state, state_out, buf_t):
    nb = SSD_SAMPLE_BLK
    const = lambda shape: pl.BlockSpec(shape, lambda i: (0,) * len(shape))
    rowblk = lambda w, off: pl.BlockSpec((nb, w), lambda i: (i, off // w))
    st_spec = pl.BlockSpec((None, nb, SSD_HEADS, SSD_HEADDIM, SSD_STATE), lambda i: (layer, i, 0, 0, 0))
    bufx_spec = pl.BlockSpec((CONV_W - 1, nb, BRANCH_W), lambda i: (0, i, 0))
    bufbc_spec = pl.BlockSpec((CONV_W - 1, nb, SSD_BC), lambda i: (0, i, BRANCH_W // SSD_BC))
    args = [proj, proj, proj, proj, *_ssd_params(p), buf_t, buf_t, state]
    in_specs = [rowblk(BRANCH_W, OFF_SZ), rowblk(BRANCH_W, OFF_SX), rowblk(SSD_BC, OFF_SBC), rowblk(LANES, OFF_SDT),
                const((CONV_W, SSD_CONV_DIM)), const((1, SSD_CONV_DIM)), const((1, LANES)), const((1, LANES)),
                const((1, BRANCH_W)), const((1, BRANCH_W)), bufx_spec, bufbc_spec, st_spec]
    aliases = {}
    kern = _ssd_sample_kernel
    if state_out is not None:
        args.append(state_out)
        in_specs.append(pl.BlockSpec(memory_space=pl.ANY))
        aliases = {len(args) - 1: 1}
        kern = functools.partial(_drop_alias_arg, kern, 13)
    out, st, nbx, nbbc = pl.pallas_call(
        kern,
        out_shape=(jax.ShapeDtypeStruct((DEC_BATCH, BRANCH_W), F32),
                   jax.ShapeDtypeStruct(state.shape, F32),
                   jax.ShapeDtypeStruct((CONV_W - 1, DEC_BATCH, BRANCH_W), F32),
                   jax.ShapeDtypeStruct((CONV_W - 1, DEC_BATCH, SSD_BC), F32)),
        grid=(DEC_BATCH // nb,),
        in_specs=in_specs,
        out_specs=(pl.BlockSpec((nb, BRANCH_W), lambda i: (i, 0)), st_spec,
                   pl.BlockSpec((CONV_W - 1, nb, BRANCH_W), lambda i: (0, i, 0)),
                   pl.BlockSpec((CONV_W - 1, nb, SSD_BC), lambda i: (0, i, 0))),
        scratch_shapes=[pltpu.VMEM((nb, BRANCH_W), F32)],
        input_output_aliases=aliases,
        compiler_params=_cparams(1),
        name="ssd_sample",
    )(*args)
    return out, st, jnp.concatenate([nbx, nbbc], axis=-1)


def _ret_prompt_kernel(hin_ref, wq_ref, wk_ref, wv_ref, wg_ref, cos_ref, sin_ref, lg_ref, out_ref, st_ref,
                       q_s, k_s, v_s, o_s):
    c = RET_CHUNK
    t_len = hin_ref.shape[0]
    proj = _project(hin_ref, (wq_ref, wk_ref, wv_ref, wg_ref))
    cos = cos_ref[...]
    sin = sin_ref[...]
    q_s[...] = _rope(proj[:, 0:RET_DK], cos, sin)
    k_s[...] = _rope(proj[:, RET_DK:2 * RET_DK], cos, sin) * RET_DK ** -0.5
    v_s[...] = proj[:, 2 * RET_DK:3 * RET_DK].astype(BF16)
    gate = proj[:, 3 * RET_DK:]
    lg = lg_ref[...]
    lg128 = lg[:, :LANES]
    ti = lax.broadcasted_iota(jnp.int32, (c, c), 0)
    si = lax.broadcasted_iota(jnp.int32, (c, c), 1)
    tri = ti >= si
    dec = jnp.where(tri, jnp.exp(jnp.where(tri, (ti - si).astype(F32) * lg, 0.0)), 0.0)
    tt = lax.broadcasted_iota(jnp.int32, (c, LANES), 0).astype(F32)
    g_in = jnp.exp((tt + 1.0) * lg128)
    g_out = jnp.exp((c - 1.0 - tt) * lg128)
    g_all = jnp.exp(float(c) * lg128)
    s = jnp.zeros((RET_DK, RET_DK), F32)
    for ci in range(t_len // c):
        sl = slice(ci * c, (ci + 1) * c)
        qc = q_s[sl, :]
        kc = k_s[sl, :]
        vb = v_s[sl, :]
        scores = _nt_dot(qc.astype(BF16), kc.astype(BF16)) * dec
        o_s[sl, :] = (jnp.dot(scores.astype(BF16), vb, preferred_element_type=F32)
                      + jnp.dot((qc * g_in).astype(BF16), s.astype(BF16), preferred_element_type=F32))
        s = s * g_all + _tn_dot((kc * g_out).astype(BF16), vb)
    out_ref[...] = (_rms_rows(o_s[...]) * jax.nn.silu(gate)).astype(out_ref.dtype)
    st_ref[...] = s


def _log_gamma_rows(width):
    lg = jnp.log1p(-jnp.exp2(-5.0 - jnp.arange(RET_HEADS, dtype=F32)))
    return jnp.broadcast_to(lg[:, None, None], (RET_HEADS, 1, width))


def ret_prompt(layer, hin, w_in_t, cos, sin):
    tab = pl.BlockSpec((SEQ, RET_DK), lambda b, h: (0, 0))
    ret_rows = tuple(RET_COL0 + off for off in (OFF_RQ, OFF_RK, OFF_RV, OFF_RG))
    return pl.pallas_call(
        _ret_prompt_kernel,
        out_shape=(jax.ShapeDtypeStruct((BATCH * SEQ, BRANCH_W), BF16),
                   jax.ShapeDtypeStruct((BATCH, RET_HEADS, RET_DK, RET_DK), F32)),
        grid=(BATCH, RET_HEADS),
        in_specs=[pl.BlockSpec((SEQ, D_MODEL), lambda b, h: (b, 0))]
                 + _w_in_row_specs(layer, ret_rows, _prompt_unit_rows)
                 + [tab, tab, pl.BlockSpec((None, 1, RET_CHUNK), lambda b, h: (h, 0, 0))],
        out_specs=(pl.BlockSpec((SEQ, RET_DK), lambda b, h: (b, h)),
                   pl.BlockSpec((None, None, RET_DK, RET_DK), lambda b, h: (b, h, 0, 0))),
        scratch_shapes=[pltpu.VMEM((SEQ, RET_DK), F32), pltpu.VMEM((SEQ, RET_DK), F32),
                        pltpu.VMEM((SEQ, RET_DK), BF16), pltpu.VMEM((SEQ, RET_DK), F32)],
        compiler_params=_cparams(2),
        name="ret_prompt",
    )(hin, w_in_t, w_in_t, w_in_t, w_in_t, cos, sin, _log_gamma_rows(RET_CHUNK))


def _ret_sample_kernel(q_ref, k_ref, v_ref, g_ref, cos_ref, sin_ref, lg_ref, s_ref, out_ref, so_ref, o_s):
    cos = cos_ref[0:1, :]
    sin = sin_ref[0:1, :]
    q = _rope(q_ref[...], cos, sin)
    k = _rope(k_ref[...], cos, sin) * RET_DK ** -0.5
    v = v_ref[...]
    gamma = jnp.exp(lg_ref[...])
    qc, kc = _rows_to_cols(q), _rows_to_cols(k)
    nb = q.shape[0]
    for j in range(nb):
        s_new = s_ref[j] * gamma + kc[:, j:j + 1] * v[j:j + 1, :]
        so_ref[j] = s_new
        o_s[j:j + 1, :] = jnp.sum(s_new * qc[:, j:j + 1], axis=0, keepdims=True)
    out_ref[...] = _rms_rows(o_s[...]) * jax.nn.silu(g_ref[...])


def ret_sample(layer, proj, cos, sin, state, state_out):
    cb = lambda off: off // RET_DK
    nb = SAMPLE_BLK
    col = lambda off: pl.BlockSpec((nb, RET_DK), lambda h, i: (i, cb(off) + h))
    tab = pl.BlockSpec((SUBLANES, RET_DK), lambda h, i: (0, 0))
    st_spec = pl.BlockSpec((None, nb, None, RET_DK, RET_DK), lambda h, i: (layer, i, h, 0, 0))
    args = [proj, proj, proj, proj, cos, sin, _log_gamma_rows(LANES), state]
    in_specs = [col(OFF_RQ), col(OFF_RK), col(OFF_RV), col(OFF_RG), tab, tab,
                pl.BlockSpec((None, 1, LANES), lambda h, i: (h, 0, 0)), st_spec]
    aliases = {}
    kern = _ret_sample_kernel
    if state_out is not None:
        args.append(state_out)
        in_specs.append(pl.BlockSpec(memory_space=pl.ANY))
        aliases = {len(args) - 1: 1}
        kern = functools.partial(_drop_alias_arg, kern, len(args) - 1)
    return pl.pallas_call(
        kern,
        out_shape=(jax.ShapeDtypeStruct((DEC_BATCH, BRANCH_W), F32),
                   jax.ShapeDtypeStruct(state.shape, F32)),
        grid=(RET_HEADS, DEC_BATCH // nb),
        in_specs=in_specs,
        out_specs=(pl.BlockSpec((nb, RET_DK), lambda h, i: (i, h)), st_spec),
        scratch_shapes=[pltpu.VMEM((nb, RET_DK), F32)],
        input_output_aliases=aliases,
        compiler_params=_cparams(2),
        name="ret_sample",
    )(*args)


def kernel(x_prompt, x_sample, state_lru_h, state_lru_conv, state_hgrn, state_ssd, state_ssd_conv, state_ret, state_ffn_conv, g_mix, g_ffn, w_in, lru_conv_w, lru_conv_b, lru_wa, lru_ba, lru_wx, lru_bx, lru_lambda, hg_lb_logits, hg_norm_w, ssd_conv_w, ssd_conv_b, ssd_dt_bias, ssd_a_log, ssd_d, ssd_norm_w, w_branch, w_gate, w_out, ffn_w_up, ffn_w_val, ffn_conv_w, ffn_conv_b, ffn_w_down, g_final):
    xp = x_prompt.reshape(BATCH * SEQ, D_MODEL)
    xs = x_sample.reshape(DEC_BATCH, D_MODEL)

    cos_p, sin_p = rope_tables(SEQ, 0, True)
    cos_s, sin_s = rope_tables(SUBLANES, PAST_LEN, False)

    hp = rmsnorm(xp, g_mix[0], BF16, 512)
    hs = rmsnorm(xs, g_mix[0], BF16, DEC_BATCH)

    state_ssd_t = jnp.swapaxes(state_ssd, -1, -2)

    w_in_t = jnp.swapaxes(w_in, 1, 2)
    w_gate_b = jnp.transpose(w_gate, (0, 2, 1, 3)).astype(BF16)
    w_out_b = w_out.astype(BF16)
    w_down_b = ffn_w_down.astype(BF16)

    prompt_states, sample_small = [], []
    hg_out = ssd_out = ret_out = None
    for l in range(DEPTH):
        p = {"lru_conv_w": lru_conv_w[l], "lru_conv_b": lru_conv_b[l], "lru_wa": lru_wa[l], "lru_ba": lru_ba[l],
             "lru_wx": lru_wx[l], "lru_bx": lru_bx[l], "lru_lambda": lru_lambda[l],
             "hg_lb_logits": hg_lb_logits, "hg_norm_w": hg_norm_w[l],
             "ssd_conv_w": ssd_conv_w[l], "ssd_conv_b": ssd_conv_b[l], "ssd_dt_bias": ssd_dt_bias[l],
             "ssd_a_log": ssd_a_log[l], "ssd_d": ssd_d[l], "ssd_norm_w": ssd_norm_w[l]}
        last = l == DEPTH - 1
        g_next = g_final if last else g_mix[l + 1]

        out_a, lru_h_p, lru_conv_p = lru_prompt(l, hp, w_in_t, p)
        out_b, hg_p = hgrn_prompt(l, hp, w_in_t, p)
        proj_ssd_p, _ = in_proj_prompt(l, hp, w_in_t, OFF_SZ, SSD_TILES, SSD_TN, 512)
        out_c, ssd_p, ssd_conv_p = ssd_prompt(proj_ssd_p, p)
        out_d, ret_p = ret_prompt(l, hp, w_in_t, cos_p, sin_p)
        merged, w_branch_b = gated_merge_prompt(l, hp, (out_a, out_b, out_c, out_d), w_gate_b, w_branch, 1024, 256)
        xp, h2 = out_proj_residual_norm(l, merged, w_out_b, xp, g_ffn[l], 512)
        act, ffn_conv_p, w_up_b, w_val_b = ffn_prompt(l, h2, ffn_w_up, ffn_w_val, ffn_conv_w[l], ffn_conv_b[l], 256)
        res = down_proj_residual_norm(l, act, w_down_b, xp, g_next, 512, DOWN_PROJ_TK, not last,
                                      F32 if last else BF16)
        if last:
            (yp,) = res
        else:
            xp, hp = res
        prompt_states.append((lru_h_p, lru_conv_p, hg_p, ssd_p, ssd_conv_p, ret_p, ffn_conv_p))

        proj_s = in_proj_sample(l, hs, w_in_t, 0, MAIN_TILES, MAIN_TN)
        proj_ret_s = in_proj_sample(l, hs, w_in_t, RET_COL0, RET_TILES, RET_TN)
        lru_buf_t = jnp.swapaxes(state_lru_conv[l], 0, 1)
        ssd_buf_t = jnp.swapaxes(state_ssd_conv[l], 0, 1)
        s_a, lru_h_s, lru_nbuf = lru_sample(proj_s, p, state_lru_h[l], lru_buf_t)
        s_b, hg_out = hgrn_sample(l, proj_s, p, state_hgrn, hg_out)
        s_c, ssd_out, ssd_nbuf = ssd_sample(l, proj_s, p, state_ssd_t, ssd_out, ssd_buf_t)
        s_d, ret_out = ret_sample(l, proj_ret_s, cos_s, sin_s, state_ret, ret_out)
        merged_s = gated_merge_sample(l, hs, (s_a, s_b, s_c, s_d), w_gate_b, w_branch_b, 256)
        xs, h2s = out_proj_residual_norm(l, merged_s, w_out_b, xs, g_ffn[l], DEC_BATCH)
        act_s, ffn_nbuf = ffn_sample(l, h2s, w_up_b, w_val_b, ffn_conv_w[l], ffn_conv_b[l], state_ffn_conv, 512)
        res = down_proj_residual_norm(l, act_s, w_down_b, xs, g_next, DEC_BATCH, DOWN_PROJ_TK, not last,
                                      F32 if last else BF16)
        if last:
            (ys,) = res
        else:
            xs, hs = res
        sample_small.append((lru_h_s, jnp.swapaxes(lru_nbuf, 0, 1), jnp.swapaxes(ssd_nbuf, 0, 1), ffn_nbuf))

    stack_p = lambda i: jnp.stack([st[i] for st in prompt_states], axis=0)
    stack_s = lambda i: jnp.stack([st[i] for st in sample_small], axis=0)
    return (yp.reshape(BATCH, SEQ, D_MODEL), ys.reshape(DEC_BATCH, 1, D_MODEL),
            stack_p(0), stack_s(0), stack_p(1), stack_s(1),
            stack_p(2), hg_out, jnp.swapaxes(stack_p(3), -1, -2), jnp.swapaxes(ssd_out, -1, -2),
            stack_p(4), stack_s(2), stack_p(5), ret_out,
            stack_p(6), stack_s(3))
```

```python
import functools
import math

import numpy as np
import jax
import jax.numpy as jnp
from jax import lax
from jax.experimental import pallas as pl
from jax.experimental.pallas import tpu as pltpu

F32 = jnp.float32
BF16 = jnp.bfloat16
HIGHEST = lax.Precision.HIGHEST

D_MODEL = 2048
BATCH = 4
SEQ = 2048
DEPTH = 2
DEC_BATCH = 128
PAST_LEN = 16384
BRANCH_W = D_MODEL // 2
EPS = 1e-6
LRU_BLOCKS = 8
LRU_BLOCK = BRANCH_W // LRU_BLOCKS
LRU_C = 8.0
CONV_W = 4
HG_HEADS = 8
HG_DK = BRANCH_W // HG_HEADS
SSD_HEADDIM = 64
SSD_HEADS = BRANCH_W // SSD_HEADDIM
SSD_GROUPS = 2
SSD_STATE = 128
SSD_BC = 2 * SSD_GROUPS * SSD_STATE
SSD_CONV_DIM = BRANCH_W + SSD_BC
RET_HEADS = 8
RET_DK = BRANCH_W // RET_HEADS
ROPE_BASE = 10000.0
D_FF = 5632
FFN_CONV_W = 3

V7X_VMEM_BYTES = 64 * 1024 * 1024
VMEM_LIMIT_BYTES = V7X_VMEM_BYTES - 8 * 1024 * 1024
LANES = 128
SUBLANES = 8

N_BRANCH = 4
N_IN = 12816
OFF_XA, OFF_YA = 0, 1024
OFF_HQ, OFF_HF, OFF_HI, OFF_HG = 2048, 3072, 4096, 5120
OFF_SZ, OFF_SX, OFF_SBC = 6144, 7168, 8192
OFF_SDT = 8704
MAIN_TN, MAIN_TILES = 1280, 7
N_MAIN = MAIN_TN * MAIN_TILES
RET_COL0 = OFF_SDT + SSD_HEADS
RET_TN, RET_TILES = 1024, 4
SSD_TN, SSD_TILES = 896, 3
SSD_PROJ_W = SSD_TN * SSD_TILES
OFF_RQ, OFF_RK, OFF_RV, OFF_RG = 0, 1024, 2048, 3072
FFN_ROW_CHUNK = 1024
DOWN_PROJ_TK = 1408

HG_CHUNK = 128
HG_LEVELS = (1, 2, 4, 8, 16, 32, 64)
SSD_CHUNK = 128
RET_CHUNK = 256
SAMPLE_BLK = 32
SSD_SAMPLE_BLK = 8


def _cparams(n_axes):
    return pltpu.CompilerParams(dimension_semantics=("arbitrary",) * n_axes,
                                vmem_limit_bytes=VMEM_LIMIT_BYTES)


def _rms_rows(x):
    return x * lax.rsqrt(jnp.mean(x * x, axis=-1, keepdims=True) + EPS)


def _shift_rows(x, d, row):
    return jnp.where(row >= d, pltpu.roll(x, d, axis=0), 0.0)


def _nt_dot(a, b):
    return lax.dot_general(a, b, (((1,), (1,)), ((), ())), preferred_element_type=F32)


def _tn_dot(a, b):
    return lax.dot_general(a, b, (((0,), (0,)), ((), ())), preferred_element_type=F32)


def _project(h_ref, w_refs):
    w = jnp.concatenate([w_ref[0].astype(BF16) for w_ref in w_refs], axis=0)
    return _nt_dot(h_ref[...], w)


def _w_in_row_specs(layer, offsets, index_map_for):
    return [pl.BlockSpec((pl.Element(1), pl.Element(LANES), pl.Element(D_MODEL)), index_map_for(layer, off))
            for off in offsets]


def _rows_to_cols(x):
    n = x.shape[0]
    if n < LANES:
        x = jnp.concatenate([x, jnp.zeros((LANES - n, x.shape[1]), x.dtype)], axis=0)
    return x.T


def _norm_kernel(x_ref, g_ref, o_ref):
    o_ref[...] = (_rms_rows(x_ref[...]) * g_ref[...]).astype(o_ref.dtype)


def rmsnorm(x, g, out_dtype, tm):
    m, d = x.shape
    return pl.pallas_call(
        _norm_kernel,
        out_shape=jax.ShapeDtypeStruct((m, d), out_dtype),
        grid=(m // tm,),
        in_specs=[pl.BlockSpec((tm, d), lambda i: (i, 0)),
                  pl.BlockSpec((1, d), lambda i: (0, 0))],
        out_specs=pl.BlockSpec((tm, d), lambda i: (i, 0)),
        compiler_params=_cparams(1),
        name="rmsnorm",
    )(x, g.reshape(1, d))


def _mm_nt_kernel(a_ref, bt_ref, o_ref):
    o_ref[...] = _nt_dot(a_ref[...], bt_ref[...])


def _mm_nt_cast_kernel(a_ref, wt_ref, o_ref, wb_ref):
    @pl.when(pl.program_id(1) == 0)
    def _():
        wb_ref[...] = wt_ref[0].astype(BF16)

    o_ref[...] = _nt_dot(a_ref[...], wb_ref[...])


def in_proj_prompt(layer, h, w_in_t, row0, n_tiles, tn, tm):
    m = h.shape[0]
    n = n_tiles * tn
    w_spec = pl.BlockSpec((pl.Element(1), pl.Element(tn), pl.Element(D_MODEL)),
                          lambda j, i: (layer, pl.multiple_of(row0 + j * tn, SUBLANES), 0))
    return pl.pallas_call(
        _mm_nt_cast_kernel,
        out_shape=(jax.ShapeDtypeStruct((m, n), F32), jax.ShapeDtypeStruct((n, D_MODEL), BF16)),
        grid=(n_tiles, m // tm),
        in_specs=[pl.BlockSpec((tm, D_MODEL), lambda j, i: (i, 0)), w_spec],
        out_specs=(pl.BlockSpec((tm, tn), lambda j, i: (i, j)),
                   pl.BlockSpec((tn, D_MODEL), lambda j, i: (j, 0))),
        compiler_params=_cparams(2),
        name="in_proj_prompt",
    )(h, w_in_t)


def _mm_nt_castw_kernel(a_ref, wt_ref, o_ref):
    o_ref[...] = _nt_dot(a_ref[...], wt_ref[0].astype(BF16))


def in_proj_sample(layer, h, w_in_t, row0, n_tiles, tn):
    m = h.shape[0]
    w_spec = pl.BlockSpec((pl.Element(1), pl.Element(tn), pl.Element(D_MODEL)),
                          lambda j: (layer, pl.multiple_of(row0 + j * tn, SUBLANES), 0))
    return pl.pallas_call(
        _mm_nt_castw_kernel,
        out_shape=jax.ShapeDtypeStruct((m, n_tiles * tn), F32),
        grid=(n_tiles,),
        in_specs=[pl.BlockSpec((m, D_MODEL), lambda j: (0, 0)), w_spec],
        out_specs=pl.BlockSpec((m, tn), lambda j: (0, j)),
        compiler_params=_cparams(1),
        name="in_proj_sample",
    )(h, w_in_t)


def _gated_sum(h, br_refs, gate_w, branch_w):
    acc = None
    for k, br_ref in enumerate(br_refs):
        gate = jax.nn.sigmoid(jnp.dot(h, gate_w(k), preferred_element_type=F32))
        br = jnp.dot(br_ref[...].astype(BF16), branch_w(k), preferred_element_type=F32)
        acc = gate * br if acc is None else acc + gate * br
    return acc


def _merge_cast_kernel(h_ref, a_ref, b_ref, c_ref, d_ref, wg_ref, wb_ref, o_ref, wbb_ref):
    @pl.when(pl.program_id(1) == 0)
    def _():
        for k in range(N_BRANCH):
            wbb_ref[k] = wb_ref[k].astype(BF16)

    acc = _gated_sum(h_ref[...], (a_ref, b_ref, c_ref, d_ref), lambda k: wg_ref[k], lambda k: wbb_ref[k])
    o_ref[...] = acc.astype(o_ref.dtype)


def gated_merge_prompt(layer, h, branches, w_gate_b, w_branch, tm, tn):
    m = h.shape[0]
    br_specs = [pl.BlockSpec((tm, BRANCH_W), lambda j, i: (i, 0)) for _ in range(N_BRANCH)]
    return pl.pallas_call(
        _merge_cast_kernel,
        out_shape=(jax.ShapeDtypeStruct((m, D_MODEL), BF16),
                   jax.ShapeDtypeStruct((N_BRANCH, BRANCH_W, D_MODEL), BF16)),
        grid=(D_MODEL // tn, m // tm),
        in_specs=[pl.BlockSpec((tm, D_MODEL), lambda j, i: (i, 0))] + br_specs
                 + [pl.BlockSpec((None, N_BRANCH, D_MODEL, tn), lambda j, i: (layer, 0, 0, j)),
                    pl.BlockSpec((None, N_BRANCH, BRANCH_W, tn), lambda j, i: (layer, 0, 0, j))],
        out_specs=(pl.BlockSpec((tm, tn), lambda j, i: (i, j)),
                   pl.BlockSpec((N_BRANCH, BRANCH_W, tn), lambda j, i: (0, 0, j))),
        compiler_params=_cparams(2),
        name="gated_merge_prompt",
    )(h, *branches, w_gate_b, w_branch)


def _merge_kernel(h_ref, a_ref, b_ref, c_ref, d_ref, wgb_ref, wbb_ref, o_ref):
    acc = _gated_sum(h_ref[...], (a_ref, b_ref, c_ref, d_ref), lambda k: wgb_ref[k], lambda k: wbb_ref[k])
    o_ref[...] = acc.astype(o_ref.dtype)


def gated_merge_sample(layer, h, branches, w_gate_b, w_branch_b, tn):
    m = h.shape[0]
    br_specs = [pl.BlockSpec((m, BRANCH_W), lambda j: (0, 0)) for _ in range(N_BRANCH)]
    return pl.pallas_call(
        _merge_kernel,
        out_shape=jax.ShapeDtypeStruct((m, D_MODEL), BF16),
        grid=(D_MODEL // tn,),
        in_specs=[pl.BlockSpec((m, D_MODEL), lambda j: (0, 0))] + br_specs
                 + [pl.BlockSpec((None, N_BRANCH, D_MODEL, tn), lambda j: (layer, 0, 0, j)),
                    pl.BlockSpec((N_BRANCH, BRANCH_W, tn), lambda j: (0, 0, j))],
        out_specs=pl.BlockSpec((m, tn), lambda j: (0, j)),
        compiler_params=_cparams(1),
        name="gated_merge_sample",
    )(h, *branches, w_gate_b, w_branch_b)


def _out_proj_kernel(m_ref, w_ref, x_ref, g_ref, xo_ref, ho_ref):
    x_new = x_ref[...] + jnp.dot(m_ref[...], w_ref[...], preferred_element_type=F32)
    xo_ref[...] = x_new
    ho_ref[...] = (_rms_rows(x_new) * g_ref[...]).astype(ho_ref.dtype)


def out_proj_residual_norm(layer, merged, w_out, x, g, tm):
    m = x.shape[0]
    return pl.pallas_call(
        _out_proj_kernel,
        out_shape=(jax.ShapeDtypeStruct((m, D_MODEL), F32), jax.ShapeDtypeStruct((m, D_MODEL), BF16)),
        grid=(m // tm,),
        in_specs=[pl.BlockSpec((tm, D_MODEL), lambda i: (i, 0)),
                  pl.BlockSpec((None, D_MODEL, D_MODEL), lambda i: (layer, 0, 0)),
                  pl.BlockSpec((tm, D_MODEL), lambda i: (i, 0)),
                  pl.BlockSpec((1, D_MODEL), lambda i: (0, 0))],
        out_specs=(pl.BlockSpec((tm, D_MODEL), lambda i: (i, 0)),
                   pl.BlockSpec((tm, D_MODEL), lambda i: (i, 0))),
        compiler_params=_cparams(1),
        name="out_proj",
    )(merged, w_out, x, g.reshape(1, D_MODEL))


def _down_proj_kernel(emit_x, a_ref, w_ref, x_ref, g_ref, *refs):
    if emit_x:
        xo_ref, no_ref, acc_ref = refs
    else:
        no_ref, acc_ref = refs
    kk = pl.program_id(1)

    @pl.when(kk == 0)
    def _():
        acc_ref[...] = x_ref[...]

    acc_ref[...] += jnp.dot(a_ref[...], w_ref[...], preferred_element_type=F32)

    @pl.when(kk == pl.num_programs(1) - 1)
    def _():
        x_new = acc_ref[...]
        if emit_x:
            xo_ref[...] = x_new
        no_ref[...] = (_rms_rows(x_new) * g_ref[...]).astype(no_ref.dtype)


def down_proj_residual_norm(layer, a, w_down, x, g, tm, tk, emit_x, norm_dtype):
    m = x.shape[0]
    out_shape = [jax.ShapeDtypeStruct((m, D_MODEL), norm_dtype)]
    out_specs = [pl.BlockSpec((tm, D_MODEL), lambda i, k: (i, 0))]
    if emit_x:
        out_shape = [jax.ShapeDtypeStruct((m, D_MODEL), F32)] + out_shape
        out_specs = [pl.BlockSpec((tm, D_MODEL), lambda i, k: (i, 0))] + out_specs
    return pl.pallas_call(
        functools.partial(_down_proj_kernel, emit_x),
        out_shape=tuple(out_shape),
        grid=(m // tm, D_FF // tk),
        in_specs=[pl.BlockSpec((tm, tk), lambda i, k: (i, k)),
                  pl.BlockSpec((None, tk, D_MODEL), lambda i, k: (layer, k, 0)),
                  pl.BlockSpec((tm, D_MODEL), lambda i, k: (i, 0)),
                  pl.BlockSpec((1, D_MODEL), lambda i, k: (0, 0))],
        out_specs=tuple(out_specs),
        scratch_shapes=[pltpu.VMEM((tm, D_MODEL), F32)],
        compiler_params=_cparams(2),
        name="down_proj",
    )(a, w_down, x, g.reshape(1, D_MODEL))


def _ffn_prompt_kernel(h_ref, wu_ref, wv_ref, cw_ref, cb_ref, a_ref, st_ref, wub_ref, wvb_ref):
    @pl.when(pl.program_id(1) == 0)
    def _():
        wub_ref[...] = wu_ref[...].astype(BF16)
        wvb_ref[...] = wv_ref[...].astype(BF16)

    rc = FFN_ROW_CHUNK
    tn = a_ref.shape[1]
    cw = cw_ref[...]
    cbias = cb_ref[...]
    row8 = lax.broadcasted_iota(jnp.int32, (SUBLANES, tn), 0)
    tail = jnp.zeros((SUBLANES, tn), F32)
    pending = None
    for c in range(h_ref.shape[0] // rc):
        rows = slice(c * rc, (c + 1) * rc)
        hc = h_ref[rows, :]
        u = jnp.dot(hc, wub_ref[...], preferred_element_type=F32)
        if pending is not None:
            prev_rows, g_prev, v_prev = pending
            a_ref[prev_rows, :] = (g_prev * v_prev).astype(a_ref.dtype)
        v = jnp.dot(hc, wvb_ref[...], preferred_element_type=F32)
        uc = cbias + cw[FFN_CONV_W - 1:FFN_CONV_W] * u
        for d in range(1, FFN_CONV_W):
            rolled = pltpu.roll(u, d, axis=0)
            top = jnp.where(row8 >= d, rolled[:SUBLANES], pltpu.roll(tail, d, axis=0))
            uc = uc + cw[FFN_CONV_W - 1 - d:FFN_CONV_W - d] * jnp.concatenate([top, rolled[SUBLANES:]], axis=0)
        pending = (rows, jax.nn.gelu(uc), v)
        tail = u[rc - SUBLANES:, :]
    prev_rows, g_prev, v_prev = pending
    a_ref[prev_rows, :] = (g_prev * v_prev).astype(a_ref.dtype)
    st_ref[...] = tail[SUBLANES - (FFN_CONV_W - 1):, :]


def ffn_prompt(layer, h2, w_up, w_val, conv_w, conv_b, tn):
    wspec = pl.BlockSpec((None, D_MODEL, tn), lambda j, b: (layer, 0, j))
    wbspec = pl.BlockSpec((D_MODEL, tn), lambda j, b: (0, j))
    return pl.pallas_call(
        _ffn_prompt_kernel,
        out_shape=(jax.ShapeDtypeStruct((BATCH * SEQ, D_FF), BF16),
                   jax.ShapeDtypeStruct((BATCH, FFN_CONV_W - 1, D_FF), F32),
                   jax.ShapeDtypeStruct((D_MODEL, D_FF), BF16),
                   jax.ShapeDtypeStruct((D_MODEL, D_FF), BF16)),
        grid=(D_FF // tn, BATCH),
        in_specs=[pl.BlockSpec((SEQ, D_MODEL), lambda j, b: (b, 0)), wspec, wspec,
                  pl.BlockSpec((FFN_CONV_W, tn), lambda j, b: (0, j)),
                  pl.BlockSpec((1, tn), lambda j, b: (0, j))],
        out_specs=(pl.BlockSpec((SEQ, tn), lambda j, b: (b, j)),
                   pl.BlockSpec((None, FFN_CONV_W - 1, tn), lambda j, b: (b, 0, j)),
                   wbspec, wbspec),
        compiler_params=_cparams(2),
        name="ffn_prompt",
    )(h2, w_up, w_val, conv_w, conv_b.reshape(1, D_FF))


def _ffn_sample_kernel(h_ref, wu_ref, wv_ref, cw_ref, cb_ref, buf_ref, a_ref, nb_ref):
    h = h_ref[...]
    u = jnp.dot(h, wu_ref[...], preferred_element_type=F32)
    v = jnp.dot(h, wv_ref[...], preferred_element_type=F32)
    cw = cw_ref[...]
    b0 = buf_ref[:, 0, :]
    b1 = buf_ref[:, 1, :]
    uc = cb_ref[...] + cw[0:1] * b0 + cw[1:2] * b1 + cw[2:3] * u
    a_ref[...] = (jax.nn.gelu(uc) * v).astype(a_ref.dtype)
    nb_ref[:, 0, :] = b1
    nb_ref[:, 1, :] = u


def ffn_sample(layer, h2, w_up, w_val, conv_w, conv_b, buf, tn):
    return pl.pallas_call(
        _ffn_sample_kernel,
        out_shape=(jax.ShapeDtypeStruct((DEC_BATCH, D_FF), BF16),
                   jax.ShapeDtypeStruct((DEC_BATCH, FFN_CONV_W - 1, D_FF), F32)),
        grid=(D_FF // tn,),
        in_specs=[pl.BlockSpec((DEC_BATCH, D_MODEL), lambda j: (0, 0)),
                  pl.BlockSpec((D_MODEL, tn), lambda j: (0, j)),
                  pl.BlockSpec((D_MODEL, tn), lambda j: (0, j)),
                  pl.BlockSpec((FFN_CONV_W, tn), lambda j: (0, j)),
                  pl.BlockSpec((1, tn), lambda j: (0, j)),
                  pl.BlockSpec((None, DEC_BATCH, FFN_CONV_W - 1, tn), lambda j: (layer, 0, 0, j))],
        out_specs=(pl.BlockSpec((DEC_BATCH, tn), lambda j: (0, j)),
                   pl.BlockSpec((DEC_BATCH, FFN_CONV_W - 1, tn), lambda j: (0, 0, j))),
        compiler_params=_cparams(1),
        name="ffn_sample",
    )(h2, w_up, w_val, conv_w, conv_b.reshape(1, D_FF), buf)


def _rope_table_kernel(start, consecutive, freq_ref, sign_ref, cos_ref, sin_ref):
    shape = cos_ref.shape
    if consecutive:
        pos = lax.broadcasted_iota(jnp.int32, shape, 0).astype(F32) + float(start)
    else:
        pos = jnp.full(shape, float(start), F32)
    ang = pos * freq_ref[...]
    cos_ref[...] = jnp.cos(ang)
    sin_ref[...] = sign_ref[...] * jnp.sin(ang)


def rope_tables(n_rows, start, consecutive):
    half = RET_DK // 2
    freqs = ROPE_BASE ** (-jnp.arange(half, dtype=F32) / half)
    freq2 = jnp.concatenate([freqs, freqs]).reshape(1, RET_DK)
    sign = jnp.concatenate([-jnp.ones((half,), F32), jnp.ones((half,), F32)]).reshape(1, RET_DK)
    return pl.pallas_call(
        functools.partial(_rope_table_kernel, start, consecutive),
        out_shape=(jax.ShapeDtypeStruct((n_rows, RET_DK), F32), jax.ShapeDtypeStruct((n_rows, RET_DK), F32)),
        name="rope_tables",
    )(freq2, sign)


def _rope(x, cos, sin_signed):
    return x * cos + pltpu.roll(x, RET_DK // 2, axis=1) * sin_signed


def _lru_gates(conv, wa_ref, ba_ref, wx_ref, bx_ref, lam_ref):
    xb = conv.astype(BF16)
    r = jax.nn.sigmoid(jnp.dot(xb, wa_ref[...].astype(BF16), preferred_element_type=F32) + ba_ref[...])
    i = jax.nn.sigmoid(jnp.dot(xb, wx_ref[...].astype(BF16), preferred_element_type=F32) + bx_ref[...])
    log_a = -LRU_C * r * jax.nn.softplus(-lam_ref[...])
    a = jnp.exp(log_a)
    u = jnp.sqrt(1.0 - a * a) * (i * conv)
    return a, u


def _lru_prompt_kernel(hin_ref, wxa_ref, wya_ref, cw_ref, cb_ref, wa_ref, ba_ref, wx_ref, bx_ref, lam_ref,
                       out_ref, h_ref, conv_ref, ag_s, ug_s):
    proj = _project(hin_ref, (wxa_ref, wya_ref))
    x = proj[:, :LRU_BLOCK]
    ya = proj[:, LRU_BLOCK:]
    t_len = x.shape[0]
    row = lax.broadcasted_iota(jnp.int32, x.shape, 0)
    cw = cw_ref[...]
    conv = cb_ref[...] + cw[CONV_W - 1:CONV_W] * x
    for d in range(1, CONV_W):
        conv = conv + cw[CONV_W - 1 - d:CONV_W - d] * _shift_rows(x, d, row)
    a, u = _lru_gates(conv, wa_ref, ba_ref, wx_ref, bx_ref, lam_ref)
    ng = t_len // SUBLANES
    a3 = a.reshape(ng, SUBLANES, LRU_BLOCK)
    u3 = u.reshape(ng, SUBLANES, LRU_BLOCK)
    sub = lax.broadcasted_iota(jnp.int32, a3.shape, 1)
    d = 1
    while d < SUBLANES:
        keep = sub >= d
        u3 = jnp.where(keep, a3 * pltpu.roll(u3, d, axis=1) + u3, u3)
        a3 = jnp.where(keep, a3 * pltpu.roll(a3, d, axis=1), a3)
        d *= 2
    ag_s[...] = a3.reshape(t_len, LRU_BLOCK)
    ug_s[...] = u3.reshape(t_len, LRU_BLOCK)
    ag = ag_s[pl.ds(SUBLANES - 1, ng, stride=SUBLANES), :]
    ug = ug_s[pl.ds(SUBLANES - 1, ng, stride=SUBLANES), :]
    grow = lax.broadcasted_iota(jnp.int32, ag.shape, 0)
    d = 1
    while d < ng:
        keep = grow >= d
        ug = jnp.where(keep, ag * pltpu.roll(ug, d, axis=0) + ug, ug)
        ag = jnp.where(keep, ag * pltpu.roll(ag, d, axis=0), ag)
        d *= 2
    carry = _shift_rows(ug, 1, grow)
    h3 = a3 * jnp.broadcast_to(carry[:, None, :], a3.shape) + u3
    hs = h3.reshape(t_len, LRU_BLOCK)
    out_ref[...] = (hs * jax.nn.gelu(ya)).astype(out_ref.dtype)
    h_ref[...] = ug[ng - 1:, :]
    conv_ref[...] = x[t_len - (CONV_W - 1):, :]


def _lru_param_specs(n_axes_fn):
    blk3 = lambda shape: pl.BlockSpec(shape, n_axes_fn(lambda n: (n, 0, 0)))
    return [pl.BlockSpec((CONV_W, LRU_BLOCK), n_axes_fn(lambda n: (0, n))),
            pl.BlockSpec((1, LRU_BLOCK), n_axes_fn(lambda n: (0, n))),
            blk3((None, LRU_BLOCK, LRU_BLOCK)), blk3((None, 1, LRU_BLOCK)),
            blk3((None, LRU_BLOCK, LRU_BLOCK)), blk3((None, 1, LRU_BLOCK)),
            blk3((None, 1, LRU_BLOCK))]


def _lru_params(p):
    return (p["lru_conv_w"], p["lru_conv_b"].reshape(1, BRANCH_W),
            p["lru_wa"], p["lru_ba"].reshape(LRU_BLOCKS, 1, LRU_BLOCK),
            p["lru_wx"], p["lru_bx"].reshape(LRU_BLOCKS, 1, LRU_BLOCK),
            p["lru_lambda"].reshape(LRU_BLOCKS, 1, LRU_BLOCK))


def _prompt_unit_rows(layer, off):
    return lambda b, u: (layer, pl.multiple_of(off + u * LANES, SUBLANES), 0)


def lru_prompt(layer, hin, w_in_t, p):
    wrap = lambda f: (lambda b, n: f(n))
    out, h, conv = pl.pallas_call(
        _lru_prompt_kernel,
        out_shape=(jax.ShapeDtypeStruct((BATCH * SEQ, BRANCH_W), BF16),
                   jax.ShapeDtypeStruct((BATCH, 1, BRANCH_W), F32),
                   jax.ShapeDtypeStruct((BATCH, CONV_W - 1, BRANCH_W), F32)),
        grid=(BATCH, LRU_BLOCKS),
        in_specs=[pl.BlockSpec((SEQ, D_MODEL), lambda b, n: (b, 0))]
                 + _w_in_row_specs(layer, (OFF_XA, OFF_YA), _prompt_unit_rows) + _lru_param_specs(wrap),
        out_specs=(pl.BlockSpec((SEQ, LRU_BLOCK), lambda b, n: (b, n)),
                   pl.BlockSpec((None, 1, LRU_BLOCK), lambda b, n: (b, 0, n)),
                   pl.BlockSpec((None, CONV_W - 1, LRU_BLOCK), lambda b, n: (b, 0, n))),
        scratch_shapes=[pltpu.VMEM((SEQ, LRU_BLOCK), F32)] * 2,
        compiler_params=_cparams(2),
        name="lru_prompt",
    )(hin, w_in_t, w_in_t, *_lru_params(p))
    return out, h.reshape(BATCH, BRANCH_W), conv


def _lru_sample_kernel(xa_ref, ya_ref, cw_ref, cb_ref, wa_ref, ba_ref, wx_ref, bx_ref, lam_ref,
                       h0_ref, buf_ref, out_ref, h_ref, nbuf_ref):
    x = xa_ref[...]
    cw = cw_ref[...]
    conv = cb_ref[...] + cw[CONV_W - 1:CONV_W] * x
    for j in range(CONV_W - 1):
        conv = conv + cw[j:j + 1] * buf_ref[j]
    a, u = _lru_gates(conv, wa_ref, ba_ref, wx_ref, bx_ref, lam_ref)
    h = a * h0_ref[...] + u
    out_ref[...] = h * jax.nn.gelu(ya_ref[...])
    h_ref[...] = h
    for j in range(CONV_W - 2):
        nbuf_ref[j] = buf_ref[j + 1]
    nbuf_ref[CONV_W - 2] = x


def lru_sample(proj, p, h0, buf_t):
    cb = lambda off: off // LRU_BLOCK
    wrap = lambda f: f
    return pl.pallas_call(
        _lru_sample_kernel,
        out_shape=(jax.ShapeDtypeStruct((DEC_BATCH, BRANCH_W), F32),
                   jax.ShapeDtypeStruct((DEC_BATCH, BRANCH_W), F32),
                   jax.ShapeDtypeStruct((CONV_W - 1, DEC_BATCH, BRANCH_W), F32)),
        grid=(LRU_BLOCKS,),
        in_specs=[pl.BlockSpec((DEC_BATCH, LRU_BLOCK), lambda n: (0, cb(OFF_XA) + n)),
                  pl.BlockSpec((DEC_BATCH, LRU_BLOCK), lambda n: (0, cb(OFF_YA) + n))]
                 + _lru_param_specs(wrap)
                 + [pl.BlockSpec((DEC_BATCH, LRU_BLOCK), lambda n: (0, n)),
                    pl.BlockSpec((CONV_W - 1, DEC_BATCH, LRU_BLOCK), lambda n: (0, 0, n))],
        out_specs=(pl.BlockSpec((DEC_BATCH, LRU_BLOCK), lambda n: (0, n)),
                   pl.BlockSpec((DEC_BATCH, LRU_BLOCK), lambda n: (0, n)),
                   pl.BlockSpec((CONV_W - 1, DEC_BATCH, LRU_BLOCK), lambda n: (0, 0, n))),
        compiler_params=_cparams(1),
        name="lru_sample",
    )(proj, proj, *_lru_params(p), h0, buf_t)


def _hgrn_lower_bound(layer, logits):
    mx = jnp.max(logits, axis=0, keepdims=True)
    e = jnp.exp(logits - mx)
    ls = e / jnp.sum(e, axis=0, keepdims=True)
    lb = jnp.zeros_like(ls[0:1])
    for i in range(1, layer + 1):
        lb = lb + ls[i:i + 1]
    return lb


def _hgrn_gates(layer, hq, hf, lbl_ref):
    lb = _hgrn_lower_bound(layer, lbl_ref[...])
    q = jax.nn.silu(hq)
    sg = jax.nn.sigmoid(hf)
    f = lb + (1.0 - lb) * sg
    k = (1.0 - lb) * (1.0 - sg)
    return q, f, k


def _hgrn_level_ids():
    c = HG_CHUNK
    t = np.arange(c)[:, None]
    s = np.arange(c)[None, :]
    level = np.zeros((c, c), np.int32)
    for li, m in enumerate(HG_LEVELS):
        same = (t // (2 * m)) == (s // (2 * m))
        level[same & ((t % (2 * m)) >= m) & ((s % (2 * m)) < m)] = li + 1
    level[t == s] = len(HG_LEVELS) + 1
    return level


def _split3_bf16(x):
    hi = x.astype(BF16)
    r1 = x - hi.astype(F32)
    mid = r1.astype(BF16)
    lo = (r1 - mid.astype(F32)).astype(BF16)
    return hi, mid, lo


def _cumsum_rows(tril_b, x):
    return sum(jnp.dot(tril_b, piece, preferred_element_type=F32) for piece in _split3_bf16(x))


def _hgrn_midpoint_factor(b, m):
    c = b.shape[0]
    if 2 * m >= SUBLANES:
        b3 = b.reshape(c // (2 * m), 2 * m, LANES)
        mid = b3[:, m - 1:m, :]
    else:
        b3 = b.reshape(c // SUBLANES, SUBLANES, LANES)
        sub = lax.broadcasted_iota(jnp.int32, b3.shape, 1)
        mid = b3[:, m - 1:m, :]
        for blk in range(1, SUBLANES // (2 * m)):
            lo = blk * 2 * m
            mid = jnp.where(sub >= lo, b3[:, lo + m - 1:lo + m, :], mid)
    return jnp.exp(-jnp.abs(b3 - mid)).reshape(c, LANES)


def _hgrn_prompt_kernel(layer, hin_ref, wq_ref, wf_ref, wi_ref, wg_ref, lbl_ref, nw_ref, lvl_ref,
                        out_ref, st_ref, lf_s, q_s, k_s, v_s, o_s):
    c = HG_CHUNK
    t_len = hin_ref.shape[0]
    proj = _project(hin_ref, (wq_ref, wf_ref, wi_ref, wg_ref))
    q, f, k = _hgrn_gates(layer, proj[:, 0:HG_DK], proj[:, HG_DK:2 * HG_DK], lbl_ref)
    lf_s[...] = jnp.log(f)
    q_s[...] = q
    k_s[...] = k
    v_s[...] = proj[:, 2 * HG_DK:3 * HG_DK].astype(BF16)
    gate = proj[:, 3 * HG_DK:]

    lvl = lvl_ref[...]
    tril_b = (lax.broadcasted_iota(jnp.int32, (c, c), 0) >= lax.broadcasted_iota(jnp.int32, (c, c), 1)).astype(BF16)

    st = jnp.zeros((HG_DK, HG_DK), F32)
    for ci in range(t_len // c):
        sl = slice(ci * c, (ci + 1) * c)
        qc = q_s[sl, :]
        kc = k_s[sl, :]
        vb = v_s[sl, :]
        b = _cumsum_rows(tril_b, lf_s[sl, :])
        att = jnp.where(lvl == len(HG_LEVELS) + 1, _nt_dot(qc.astype(BF16), kc.astype(BF16)), 0.0)
        for li, m in enumerate(HG_LEVELS):
            e = _hgrn_midpoint_factor(b, m)
            a_l = _nt_dot((qc * e).astype(BF16), (kc * e).astype(BF16))
            att = jnp.where(lvl == li + 1, a_l, att)
        o = (jnp.dot(att.astype(BF16), vb, preferred_element_type=F32)
             + _nt_dot((qc * jnp.exp(b)).astype(BF16), st.astype(BF16)))
        o_s[sl, :] = o
        bl = b[c - 1:c, :]
        kdec = (kc * jnp.exp(bl - b)).astype(BF16)
        st = st * jnp.exp(bl) + _tn_dot(vb, kdec)
    o = o_s[...]
    out_ref[...] = (_rms_rows(o) * nw_ref[...] * jax.nn.silu(gate)).astype(out_ref.dtype)
    st_ref[...] = st.T


def hgrn_prompt(layer, hin, w_in_t, p):
    level = _hgrn_level_ids()
    return pl.pallas_call(
        functools.partial(_hgrn_prompt_kernel, layer),
        out_shape=(jax.ShapeDtypeStruct((BATCH * SEQ, BRANCH_W), BF16),
                   jax.ShapeDtypeStruct((BATCH, HG_HEADS, HG_DK, HG_DK), F32)),
        grid=(BATCH, HG_HEADS),
        in_specs=[pl.BlockSpec((SEQ, D_MODEL), lambda b, h: (b, 0))]
                 + _w_in_row_specs(layer, (OFF_HQ, OFF_HF, OFF_HI, OFF_HG), _prompt_unit_rows)
                 + [pl.BlockSpec((DEPTH, HG_DK), lambda b, h: (0, h)),
                    pl.BlockSpec((1, HG_DK), lambda b, h: (0, 0)),
                    pl.BlockSpec(level.shape, lambda b, h: (0, 0))],
        out_specs=(pl.BlockSpec((SEQ, HG_DK), lambda b, h: (b, h)),
                   pl.BlockSpec((None, None, HG_DK, HG_DK), lambda b, h: (b, h, 0, 0))),
        scratch_shapes=[pltpu.VMEM((SEQ, HG_DK), F32)] * 3
                       + [pltpu.VMEM((SEQ, HG_DK), BF16), pltpu.VMEM((SEQ, HG_DK), F32)],
        compiler_params=_cparams(2),
        name="hgrn_prompt",
    )(hin, w_in_t, w_in_t, w_in_t, w_in_t, p["hg_lb_logits"], p["hg_norm_w"].reshape(1, HG_DK),
      jnp.asarray(level))


def _hgrn_sample_kernel(layer, q_ref, f_ref, i_ref, g_ref, lbl_ref, nw_ref, s_ref, out_ref, so_ref, o_s):
    q, f, k = _hgrn_gates(layer, q_ref[...], f_ref[...], lbl_ref)
    qc, fc, kc = _rows_to_cols(q), _rows_to_cols(f), _rows_to_cols(k)
    nb = q.shape[0]
    for j in range(nb):
        s_new = s_ref[j] * fc[:, j:j + 1] + kc[:, j:j + 1] * i_ref[j:j + 1, :]
        so_ref[j] = s_new
        o_s[j:j + 1, :] = jnp.sum(s_new * qc[:, j:j + 1], axis=0, keepdims=True)
    out_ref[...] = _rms_rows(o_s[...]) * nw_ref[...] * jax.nn.silu(g_ref[...])


def hgrn_sample(layer, proj, p, state, state_out):
    cb = lambda off: off // HG_DK
    nb = SAMPLE_BLK
    col = lambda off: pl.BlockSpec((nb, HG_DK), lambda h, i: (i, cb(off) + h))
    st_spec = pl.BlockSpec((None, nb, None, HG_DK, HG_DK), lambda h, i: (layer, i, h, 0, 0))
    args = [proj, proj, proj, proj, p["hg_lb_logits"], p["hg_norm_w"].reshape(1, HG_DK), state]
    in_specs = [col(OFF_HQ), col(OFF_HF), col(OFF_HI), col(OFF_HG),
                pl.BlockSpec((DEPTH, HG_DK), lambda h, i: (0, h)),
                pl.BlockSpec((1, HG_DK), lambda h, i: (0, 0)),
                st_spec]
    aliases = {}
    kern = functools.partial(_hgrn_sample_kernel, layer)
    if state_out is not None:
        args.append(state_out)
        in_specs.append(pl.BlockSpec(memory_space=pl.ANY))
        aliases = {len(args) - 1: 1}
        kern = functools.partial(_drop_alias_arg, kern, 7)
    return pl.pallas_call(
        kern,
        out_shape=(jax.ShapeDtypeStruct((DEC_BATCH, BRANCH_W), F32),
                   jax.ShapeDtypeStruct(state.shape, F32)),
        grid=(HG_HEADS, DEC_BATCH // nb),
        in_specs=in_specs,
        out_specs=(pl.BlockSpec((nb, HG_DK), lambda h, i: (i, h)), st_spec),
        scratch_shapes=[pltpu.VMEM((nb, HG_DK), F32)],
        input_output_aliases=aliases,
        compiler_params=_cparams(2),
        name="hgrn_sample",
    )(*args)


def _drop_alias_arg(kern, pos, *refs):
    return kern(*refs[:pos], *refs[pos + 1:])


def _head_pair(cols, h0, lane_lo):
    return jnp.where(lane_lo, cols[:, h0:h0 + 1], cols[:, h0 + 1:h0 + 2])


def _ssd_prompt_kernel(z_ref, x_ref, bc_ref, dt_ref, cw_ref, cb_ref, dtb_ref, alog_ref, dpar_ref, nw_ref,
                       tril_ref, out_ref, st_ref, cst_ref, cx_s, cbc_s, s_s):
    c = SSD_CHUNK
    ci = pl.program_id(1)

    @pl.when(ci == 0)
    def _():
        cx_s[...] = jnp.zeros_like(cx_s)
        cbc_s[...] = jnp.zeros_like(cbc_s)
        s_s[...] = jnp.zeros_like(s_s)

    cw = cw_ref[...]
    cbias = cb_ref[...]

    def conv_silu(raw, carry_ref, lo, hi):
        xx = jnp.concatenate([carry_ref[...], raw], axis=0)
        y = cbias[:, lo:hi] + cw[CONV_W - 1:CONV_W, lo:hi] * raw
        for d in range(1, CONV_W):
            y = y + cw[CONV_W - 1 - d:CONV_W - d, lo:hi] * pltpu.roll(xx, d, axis=0)[SUBLANES:]
        carry_ref[...] = raw[c - SUBLANES:, :]
        return jax.nn.silu(y)

    x_raw = x_ref[...]
    bc_raw = bc_ref[...]
    xs = conv_silu(x_raw, cx_s, 0, BRANCH_W)
    bc = conv_silu(bc_raw, cbc_s, BRANCH_W, SSD_CONV_DIM)

    dt = jax.nn.softplus(dt_ref[...] + dtb_ref[...])
    a_neg = -jnp.exp(alog_ref[...])
    logd = dt * a_neg
    b = jnp.dot(tril_ref[...], logd, precision=HIGHEST, preferred_element_type=F32)
    b_t = b.T
    bl = b[c - 1:c, :]
    e_in = jnp.exp(b)
    w_out = jnp.exp(bl - b)
    e_last = jnp.exp(bl)
    dfull = dpar_ref[...]

    tri = lax.broadcasted_iota(jnp.int32, (c, c), 0) >= lax.broadcasted_iota(jnp.int32, (c, c), 1)
    lane_lo = lax.broadcasted_iota(jnp.int32, (c, LANES), 1) < SSD_HEADDIM
    lane_lo_row = lax.broadcasted_iota(jnp.int32, (1, LANES), 1) < SSD_HEADDIM

    ys = []
    for g in range(SSD_GROUPS):
        bm = bc[:, g * SSD_STATE:(g + 1) * SSD_STATE].astype(BF16)
        cm = bc[:, (SSD_GROUPS + g) * SSD_STATE:(SSD_GROUPS + g + 1) * SSD_STATE].astype(BF16)
        gmat = _nt_dot(cm, bm)
        for pp in range(SSD_HEADS // SSD_GROUPS // 2):
            pi = g * (SSD_HEADS // SSD_GROUPS // 2) + pp
            h0 = 2 * pi
            xs_p = xs[:, pi * LANES:(pi + 1) * LANES]
            vdt = xs_p * _head_pair(dt, h0, lane_lo)
            vdt_b = vdt.astype(BF16)
            o_heads = []
            for hh in (h0, h0 + 1):
                diff = b[:, hh:hh + 1] - b_t[hh:hh + 1, :]
                dec = jnp.where(tri, jnp.exp(jnp.where(tri, diff, 0.0)), 0.0)
                o_heads.append(jnp.dot((gmat * dec).astype(BF16), vdt_b, preferred_element_type=F32))
            o_intra = jnp.where(lane_lo, o_heads[0], o_heads[1])
            s_p = s_s[pi]
            o_inter = _head_pair(e_in, h0, lane_lo) * jnp.dot(cm, s_p.astype(BF16), preferred_element_type=F32)
            ys.append(o_intra + o_inter + dfull[:, pi * LANES:(pi + 1) * LANES] * xs_p)
            upd = _tn_dot(bm, (vdt * _head_pair(w_out, h0, lane_lo)).astype(BF16))
            s_s[pi] = s_p * _head_pair(e_last, h0, lane_lo_row) + upd

    y = jnp.concatenate(ys, axis=1) * jax.nn.silu(z_ref[...])
    gw = BRANCH_W // SSD_GROUPS
    nw = nw_ref[...]
    outs = [_rms_rows(y[:, g * gw:(g + 1) * gw]) * nw[:, g * gw:(g + 1) * gw] for g in range(SSD_GROUPS)]
    out_ref[...] = jnp.concatenate(outs, axis=1).astype(out_ref.dtype)

    @pl.when(ci == pl.num_programs(1) - 1)
    def _():
        for pi in range(SSD_HEADS // 2):
            s_t = s_s[pi].T
            st_ref[2 * pi] = s_t[:SSD_HEADDIM, :]
            st_ref[2 * pi + 1] = s_t[SSD_HEADDIM:, :]
        cst_ref[:, 0:BRANCH_W] = x_raw[c - (CONV_W - 1):, :]
        cst_ref[:, BRANCH_W:SSD_CONV_DIM] = bc_raw[c - (CONV_W - 1):, :]


def _pad_lanes(v):
    return jnp.pad(v.astype(F32), (0, LANES - v.shape[0])).reshape(1, LANES)


def _ssd_params(p):
    return (p["ssd_conv_w"], p["ssd_conv_b"].reshape(1, SSD_CONV_DIM), _pad_lanes(p["ssd_dt_bias"]),
            _pad_lanes(p["ssd_a_log"]), jnp.repeat(p["ssd_d"].astype(F32), SSD_HEADDIM).reshape(1, BRANCH_W),
            p["ssd_norm_w"].reshape(1, BRANCH_W))


def ssd_prompt(proj, p):
    c = SSD_CHUNK
    nc = SEQ // c
    tril = jnp.asarray(np.tril(np.ones((c, c), np.float32)))
    const = lambda shape: pl.BlockSpec(shape, lambda b, i: (0, 0))
    rowblk = lambda w, off: pl.BlockSpec((c, w), lambda b, i: (b * nc + i, (off - OFF_SZ) // w))
    return pl.pallas_call(
        _ssd_prompt_kernel,
        out_shape=(jax.ShapeDtypeStruct((BATCH * SEQ, BRANCH_W), BF16),
                   jax.ShapeDtypeStruct((BATCH, SSD_HEADS, SSD_HEADDIM, SSD_STATE), F32),
                   jax.ShapeDtypeStruct((BATCH, CONV_W - 1, SSD_CONV_DIM), F32)),
        grid=(BATCH, nc),
        in_specs=[rowblk(BRANCH_W, OFF_SZ), rowblk(BRANCH_W, OFF_SX), rowblk(SSD_BC, OFF_SBC),
                  rowblk(LANES, OFF_SDT),
                  const((CONV_W, SSD_CONV_DIM)), const((1, SSD_CONV_DIM)), const((1, LANES)),
                  const((1, LANES)), const((1, BRANCH_W)), const((1, BRANCH_W)), const((c, c))],
        out_specs=(pl.BlockSpec((c, BRANCH_W), lambda b, i: (b * nc + i, 0)),
                   pl.BlockSpec((None, SSD_HEADS, SSD_HEADDIM, SSD_STATE), lambda b, i: (b, 0, 0, 0)),
                   pl.BlockSpec((None, CONV_W - 1, SSD_CONV_DIM), lambda b, i: (b, 0, 0))),
        scratch_shapes=[pltpu.VMEM((SUBLANES, BRANCH_W), F32), pltpu.VMEM((SUBLANES, SSD_BC), F32),
                        pltpu.VMEM((SSD_HEADS // 2, SSD_STATE, LANES), F32)],
        compiler_params=_cparams(2),
        name="ssd_prompt",
    )(proj, proj, proj, proj, *_ssd_params(p), tril)


def _ssd_sample_kernel(z_ref, x_ref, bc_ref, dt_ref, cw_ref, cb_ref, dtb_ref, alog_ref, dpar_ref, nw_ref,
                       bufx_ref, bufbc_ref, s_ref, out_ref, so_ref, nbx_ref, nbbc_ref, y_s):
    cw = cw_ref[...]
    cbias = cb_ref[...]

    def conv_silu(raw, buf_ref, nbuf_ref, lo, hi):
        y = cbias[:, lo:hi] + cw[CONV_W - 1:CONV_W, lo:hi] * raw
        for j in range(CONV_W - 1):
            y = y + cw[j:j + 1, lo:hi] * buf_ref[j]
        for j in range(CONV_W - 2):
            nbuf_ref[j] = buf_ref[j + 1]
        nbuf_ref[CONV_W - 2] = raw
        return jax.nn.silu(y)

    xs = conv_silu(x_ref[...], bufx_ref, nbx_ref, 0, BRANCH_W)
    bc = conv_silu(bc_ref[...], bufbc_ref, nbbc_ref, BRANCH_W, SSD_CONV_DIM)
    dt = jax.nn.softplus(dt_ref[...] + dtb_ref[...])
    decay = jnp.exp(dt * (-jnp.exp(alog_ref[...])))
    nb = xs.shape[0]
    hpg = SSD_HEADS // SSD_GROUPS
    lane_lo = lax.broadcasted_iota(jnp.int32, (nb, LANES), 1) < SSD_HEADDIM
    for pi in range(SSD_HEADS // 2):
        h0 = 2 * pi
        g = h0 // hpg
        xdt_cols = _rows_to_cols(xs[:, pi * LANES:(pi + 1) * LANES] * _head_pair(dt, h0, lane_lo))
        for e in range(2):
            h = h0 + e
            for j in range(nb):
                xcol = xdt_cols[e * SSD_HEADDIM:(e + 1) * SSD_HEADDIM, j:j + 1]
                brow = bc[j:j + 1, g * SSD_STATE:(g + 1) * SSD_STATE]
                so_ref[j, h] = s_ref[j, h] * decay[j:j + 1, h:h + 1] + xcol * brow
    for g in range(SSD_GROUPS):
        cm = bc[:, (SSD_GROUPS + g) * SSD_STATE:(SSD_GROUPS + g + 1) * SSD_STATE].astype(BF16)
        for j in range(nb):
            s_new = so_ref[j, g * hpg:(g + 1) * hpg].reshape(hpg * SSD_HEADDIM, SSD_STATE)
            y_s[j:j + 1, g * hpg * SSD_HEADDIM:(g + 1) * hpg * SSD_HEADDIM] = _nt_dot(cm, s_new.astype(BF16))[j:j + 1, :]
    y = (y_s[...] + dpar_ref[...] * xs) * jax.nn.silu(z_ref[...])
    gw = BRANCH_W // SSD_GROUPS
    nw = nw_ref[...]
    outs = [_rms_rows(y[:, g * gw:(g + 1) * gw]) * nw[:, g * gw:(g + 1) * gw] for g in range(SSD_GROUPS)]
    out_ref[...] = jnp.concatenate(outs, axis=1)


def ssd_sample(layer, proj, p, state, state_out, buf_t):
    nb = SSD_SAMPLE_BLK
    const = lambda shape: pl.BlockSpec(shape, lambda i: (0,) * len(shape))
    rowblk = lambda w, off: pl.BlockSpec((nb, w), lambda i: (i, off // w))
    st_spec = pl.BlockSpec((None, nb, SSD_HEADS, SSD_HEADDIM, SSD_STATE), lambda i: (layer, i, 0, 0, 0))
    bufx_spec = pl.BlockSpec((CONV_W - 1, nb, BRANCH_W), lambda i: (0, i, 0))
    bufbc_spec = pl.BlockSpec((CONV_W - 1, nb, SSD_BC), lambda i: (0, i, BRANCH_W // SSD_BC))
    args = [proj, proj, proj, proj, *_ssd_params(p), buf_t, buf_t, state]
    in_specs = [rowblk(BRANCH_W, OFF_SZ), rowblk(BRANCH_W, OFF_SX), rowblk(SSD_BC, OFF_SBC), rowblk(LANES, OFF_SDT),
                const((CONV_W, SSD_CONV_DIM)), const((1, SSD_CONV_DIM)), const((1, LANES)), const((1, LANES)),
                const((1, BRANCH_W)), const((1, BRANCH_W)), bufx_spec, bufbc_spec, st_spec]
    aliases = {}
    kern = _ssd_sample_kernel
    if state_out is not None:
        args.append(state_out)
        in_specs.append(pl.BlockSpec(memory_space=pl.ANY))
        aliases = {len(args) - 1: 1}
        kern = functools.partial(_drop_alias_arg, kern, 13)
    out, st, nbx, nbbc = pl.pallas_call(
        kern,
        out_shape=(jax.ShapeDtypeStruct((DEC_BATCH, BRANCH_W), F32),
                   jax.ShapeDtypeStruct(state.shape, F32),
                   jax.ShapeDtypeStruct((CONV_W - 1, DEC_BATCH, BRANCH_W), F32),
                   jax.ShapeDtypeStruct((CONV_W - 1, DEC_BATCH, SSD_BC), F32)),
        grid=(DEC_BATCH // nb,),
        in_specs=in_specs,
        out_specs=(pl.BlockSpec((nb, BRANCH_W), lambda i: (i, 0)), st_spec,
                   pl.BlockSpec((CONV_W - 1, nb, BRANCH_W), lambda i: (0, i, 0)),
                   pl.BlockSpec((CONV_W - 1, nb, SSD_BC), lambda i: (0, i, 0))),
        scratch_shapes=[pltpu.VMEM((nb, BRANCH_W), F32)],
        input_output_aliases=aliases,
        compiler_params=_cparams(1),
        name="ssd_sample",
    )(*args)
    return out, st, jnp.concatenate([nbx, nbbc], axis=-1)


def _ret_prompt_kernel(hin_ref, wq_ref, wk_ref, wv_ref, wg_ref, cos_ref, sin_ref, lg_ref, out_ref, st_ref,
                       q_s, k_s, v_s, o_s):
    c = RET_CHUNK
    t_len = hin_ref.shape[0]
    proj = _project(hin_ref, (wq_ref, wk_ref, wv_ref, wg_ref))
    cos = cos_ref[...]
    sin = sin_ref[...]
    q_s[...] = _rope(proj[:, 0:RET_DK], cos, sin)
    k_s[...] = _rope(proj[:, RET_DK:2 * RET_DK], cos, sin) * RET_DK ** -0.5
    v_s[...] = proj[:, 2 * RET_DK:3 * RET_DK].astype(BF16)
    gate = proj[:, 3 * RET_DK:]
    lg = lg_ref[...]
    lg128 = lg[:, :LANES]
    ti = lax.broadcasted_iota(jnp.int32, (c, c), 0)
    si = lax.broadcasted_iota(jnp.int32, (c, c), 1)
    tri = ti >= si
    dec = jnp.where(tri, jnp.exp(jnp.where(tri, (ti - si).astype(F32) * lg, 0.0)), 0.0)
    tt = lax.broadcasted_iota(jnp.int32, (c, LANES), 0).astype(F32)
    g_in = jnp.exp((tt + 1.0) * lg128)
    g_out = jnp.exp((c - 1.0 - tt) * lg128)
    g_all = jnp.exp(float(c) * lg128)
    s = jnp.zeros((RET_DK, RET_DK), F32)
    for ci in range(t_len // c):
        sl = slice(ci * c, (ci + 1) * c)
        qc = q_s[sl, :]
        kc = k_s[sl, :]
        vb = v_s[sl, :]
        scores = _nt_dot(qc.astype(BF16), kc.astype(BF16)) * dec
        o_s[sl, :] = (jnp.dot(scores.astype(BF16), vb, preferred_element_type=F32)
                      + jnp.dot((qc * g_in).astype(BF16), s.astype(BF16), preferred_element_type=F32))
        s = s * g_all + _tn_dot((kc * g_out).astype(BF16), vb)
    out_ref[...] = (_rms_rows(o_s[...]) * jax.nn.silu(gate)).astype(out_ref.dtype)
    st_ref[...] = s


def _log_gamma_rows(width):
    lg = jnp.log1p(-jnp.exp2(-5.0 - jnp.arange(RET_HEADS, dtype=F32)))
    return jnp.broadcast_to(lg[:, None, None], (RET_HEADS, 1, width))


def ret_prompt(layer, hin, w_in_t, cos, sin):
    tab = pl.BlockSpec((SEQ, RET_DK), lambda b, h: (0, 0))
    ret_rows = tuple(RET_COL0 + off for off in (OFF_RQ, OFF_RK, OFF_RV, OFF_RG))
    return pl.pallas_call(
        _ret_prompt_kernel,
        out_shape=(jax.ShapeDtypeStruct((BATCH * SEQ, BRANCH_W), BF16),
                   jax.ShapeDtypeStruct((BATCH, RET_HEADS, RET_DK, RET_DK), F32)),
        grid=(BATCH, RET_HEADS),
        in_specs=[pl.BlockSpec((SEQ, D_MODEL), lambda b, h: (b, 0))]
                 + _w_in_row_specs(layer, ret_rows, _prompt_unit_rows)
                 + [tab, tab, pl.BlockSpec((None, 1, RET_CHUNK), lambda b, h: (h, 0, 0))],
        out_specs=(pl.BlockSpec((SEQ, RET_DK), lambda b, h: (b, h)),
                   pl.BlockSpec((None, None, RET_DK, RET_DK), lambda b, h: (b, h, 0, 0))),
        scratch_shapes=[pltpu.VMEM((SEQ, RET_DK), F32), pltpu.VMEM((SEQ, RET_DK), F32),
                        pltpu.VMEM((SEQ, RET_DK), BF16), pltpu.VMEM((SEQ, RET_DK), F32)],
        compiler_params=_cparams(2),
        name="ret_prompt",
    )(hin, w_in_t, w_in_t, w_in_t, w_in_t, cos, sin, _log_gamma_rows(RET_CHUNK))


def _ret_sample_kernel(q_ref, k_ref, v_ref, g_ref, cos_ref, sin_ref, lg_ref, s_ref, out_ref, so_ref, o_s):
    cos = cos_ref[0:1, :]
    sin = sin_ref[0:1, :]
    q = _rope(q_ref[...], cos, sin)
    k = _rope(k_ref[...], cos, sin) * RET_DK ** -0.5
    v = v_ref[...]
    gamma = jnp.exp(lg_ref[...])
    qc, kc = _rows_to_cols(q), _rows_to_cols(k)
    nb = q.shape[0]
    for j in range(nb):
        s_new = s_ref[j] * gamma + kc[:, j:j + 1] * v[j:j + 1, :]
        so_ref[j] = s_new
        o_s[j:j + 1, :] = jnp.sum(s_new * qc[:, j:j + 1], axis=0, keepdims=True)
    out_ref[...] = _rms_rows(o_s[...]) * jax.nn.silu(g_ref[...])


def ret_sample(layer, proj, cos, sin, state, state_out):
    cb = lambda off: off // RET_DK
    nb = SAMPLE_BLK
    col = lambda off: pl.BlockSpec((nb, RET_DK), lambda h, i: (i, cb(off) + h))
    tab = pl.BlockSpec((SUBLANES, RET_DK), lambda h, i: (0, 0))
    st_spec = pl.BlockSpec((None, nb, None, RET_DK, RET_DK), lambda h, i: (layer, i, h, 0, 0))
    args = [proj, proj, proj, proj, cos, sin, _log_gamma_rows(LANES), state]
    in_specs = [col(OFF_RQ), col(OFF_RK), col(OFF_RV), col(OFF_RG), tab, tab,
                pl.BlockSpec((None, 1, LANES), lambda h, i: (h, 0, 0)), st_spec]
    aliases = {}
    kern = _ret_sample_kernel
    if state_out is not None:
        args.append(state_out)
        in_specs.append(pl.BlockSpec(memory_space=pl.ANY))
        aliases = {len(args) - 1: 1}
        kern = functools.partial(_drop_alias_arg, kern, len(args) - 1)
    return pl.pallas_call(
        kern,
        out_shape=(jax.ShapeDtypeStruct((DEC_BATCH, BRANCH_W), F32),
                   jax.ShapeDtypeStruct(state.shape, F32)),
        grid=(RET_HEADS, DEC_BATCH // nb),
        in_specs=in_specs,
        out_specs=(pl.BlockSpec((nb, RET_DK), lambda h, i: (i, h)), st_spec),
        scratch_shapes=[pltpu.VMEM((nb, RET_DK), F32)],
        input_output_aliases=aliases,
        compiler_params=_cparams(2),
        name="ret_sample",
    )(*args)


def kernel(x_prompt, x_sample, state_lru_h, state_lru_conv, state_hgrn, state_ssd, state_ssd_conv, state_ret, state_ffn_conv, g_mix, g_ffn, w_in, lru_conv_w, lru_conv_b, lru_wa, lru_ba, lru_wx, lru_bx, lru_lambda, hg_lb_logits, hg_norm_w, ssd_conv_w, ssd_conv_b, ssd_dt_bias, ssd_a_log, ssd_d, ssd_norm_w, w_branch, w_gate, w_out, ffn_w_up, ffn_w_val, ffn_conv_w, ffn_conv_b, ffn_w_down, g_final):
    xp = x_prompt.reshape(BATCH * SEQ, D_MODEL)
    xs = x_sample.reshape(DEC_BATCH, D_MODEL)

    cos_p, sin_p = rope_tables(SEQ, 0, True)
    cos_s, sin_s = rope_tables(SUBLANES, PAST_LEN, False)

    hp = rmsnorm(xp, g_mix[0], BF16, 512)
    hs = rmsnorm(xs, g_mix[0], BF16, DEC_BATCH)

    state_ssd_t = jnp.swapaxes(state_ssd, -1, -2)

    w_in_t = jnp.swapaxes(w_in, 1, 2)
    w_gate_b = jnp.transpose(w_gate, (0, 2, 1, 3)).astype(BF16)
    w_out_b = w_out.astype(BF16)
    w_down_b = ffn_w_down.astype(BF16)

    prompt_states, sample_small = [], []
    hg_out = ssd_out = ret_out = None
    for l in range(DEPTH):
        p = {"lru_conv_w": lru_conv_w[l], "lru_conv_b": lru_conv_b[l], "lru_wa": lru_wa[l], "lru_ba": lru_ba[l],
             "lru_wx": lru_wx[l], "lru_bx": lru_bx[l], "lru_lambda": lru_lambda[l],
             "hg_lb_logits": hg_lb_logits, "hg_norm_w": hg_norm_w[l],
             "ssd_conv_w": ssd_conv_w[l], "ssd_conv_b": ssd_conv_b[l], "ssd_dt_bias": ssd_dt_bias[l],
             "ssd_a_log": ssd_a_log[l], "ssd_d": ssd_d[l], "ssd_norm_w": ssd_norm_w[l]}
        last = l == DEPTH - 1
        g_next = g_final if last else g_mix[l + 1]

        out_a, lru_h_p, lru_conv_p = lru_prompt(l, hp, w_in_t, p)
        out_b, hg_p = hgrn_prompt(l, hp, w_in_t, p)
        proj_ssd_p, _ = in_proj_prompt(l, hp, w_in_t, OFF_SZ, SSD_TILES, SSD_TN, 1024)
        out_c, ssd_p, ssd_conv_p = ssd_prompt(proj_ssd_p, p)
        out_d, ret_p = ret_prompt(l, hp, w_in_t, cos_p, sin_p)
        merged, w_branch_b = gated_merge_prompt(l, hp, (out_a, out_b, out_c, out_d), w_gate_b, w_branch, 1024, 256)
        xp, h2 = out_proj_residual_norm(l, merged, w_out_b, xp, g_ffn[l], 512)
        act, ffn_conv_p, w_up_b, w_val_b = ffn_prompt(l, h2, ffn_w_up, ffn_w_val, ffn_conv_w[l], ffn_conv_b[l], 256)
        res = down_proj_residual_norm(l, act, w_down_b, xp, g_next, 512, DOWN_PROJ_TK, not last,
                                      F32 if last else BF16)
        if last:
            (yp,) = res
        else:
            xp, hp = res
        prompt_states.append((lru_h_p, lru_conv_p, hg_p, ssd_p, ssd_conv_p, ret_p, ffn_conv_p))

        proj_s = in_proj_sample(l, hs, w_in_t, 0, MAIN_TILES, MAIN_TN)
        proj_ret_s = in_proj_sample(l, hs, w_in_t, RET_COL0, RET_TILES, RET_TN)
        lru_buf_t = jnp.swapaxes(state_lru_conv[l], 0, 1)
        ssd_buf_t = jnp.swapaxes(state_ssd_conv[l], 0, 1)
        s_a, lru_h_s, lru_nbuf = lru_sample(proj_s, p, state_lru_h[l], lru_buf_t)
        s_b, hg_out = hgrn_sample(l, proj_s, p, state_hgrn, hg_out)
        s_c, ssd_out, ssd_nbuf = ssd_sample(l, proj_s, p, state_ssd_t, ssd_out, ssd_buf_t)
        s_d, ret_out = ret_sample(l, proj_ret_s, cos_s, sin_s, state_ret, ret_out)
        merged_s = gated_merge_sample(l, hs, (s_a, s_b, s_c, s_d), w_gate_b, w_branch_b, 256)
        xs, h2s = out_proj_residual_norm(l, merged_s, w_out_b, xs, g_ffn[l], DEC_BATCH)
        act_s, ffn_nbuf = ffn_sample(l, h2s, w_up_b, w_val_b, ffn_conv_w[l], ffn_conv_b[l], state_ffn_conv, 512)
        res = down_proj_residual_norm(l, act_s, w_down_b, xs, g_next, DEC_BATCH, DOWN_PROJ_TK, not last,
                                      F32 if last else BF16)
        if last:
            (ys,) = res
        else:
            xs, hs = res
        sample_small.append((lru_h_s, jnp.swapaxes(lru_nbuf, 0, 1), jnp.swapaxes(ssd_nbuf, 0, 1), ffn_nbuf))

    stack_p = lambda i: jnp.stack([st[i] for st in prompt_states], axis=0)
    stack_s = lambda i: jnp.stack([st[i] for st in sample_small], axis=0)
    return (yp.reshape(BATCH, SEQ, D_MODEL), ys.reshape(DEC_BATCH, 1, D_MODEL),
            stack_p(0), stack_s(0), stack_p(1), stack_s(1),
            stack_p(2), hg_out, jnp.swapaxes(stack_p(3), -1, -2), jnp.swapaxes(ssd_out, -1, -2),
            stack_p(4), stack_s(2), stack_p(5), ret_out,
            stack_p(6), stack_s(3))
```

```python
import functools

import numpy as np
import jax
import jax.numpy as jnp
from jax import lax
from jax.experimental import pallas as pl
from jax.experimental.pallas import tpu as pltpu

F32 = jnp.float32
BF16 = jnp.bfloat16

D_MODEL = 2048
BATCH = 4
SEQ = 2048
DEPTH = 2
DEC_BATCH = 128
PAST_LEN = 16384
BRANCH_W = D_MODEL // 2
EPS = 1e-6
LRU_BLOCKS = 8
LRU_BLOCK = BRANCH_W // LRU_BLOCKS
LRU_C = 8.0
CONV_W = 4
HG_HEADS = 8
HG_DK = BRANCH_W // HG_HEADS
SSD_HEADDIM = 64
SSD_HEADS = BRANCH_W // SSD_HEADDIM
SSD_GROUPS = 2
SSD_STATE = 128
SSD_BC = 2 * SSD_GROUPS * SSD_STATE
SSD_CONV_DIM = BRANCH_W + SSD_BC
RET_HEADS = 8
RET_DK = BRANCH_W // RET_HEADS
ROPE_BASE = 10000.0
D_FF = 5632
FFN_CONV_W = 3

V7X_VMEM_BYTES = 64 * 1024 * 1024
VMEM_LIMIT_BYTES = V7X_VMEM_BYTES - 8 * 1024 * 1024
LANES = 128
SUBLANES = 8

N_BRANCH = 4
N_IN = 12816
OFF_XA, OFF_YA = 0, 1024
OFF_HQ, OFF_HF, OFF_HI, OFF_HG = 2048, 3072, 4096, 5120
OFF_SZ, OFF_SX, OFF_SBC = 6144, 7168, 8192
OFF_SDT = 8704
MAIN_TN, MAIN_TILES = 1280, 7
N_MAIN = MAIN_TN * MAIN_TILES
RET_COL0 = OFF_SDT + SSD_HEADS
RET_TN, RET_TILES = 1024, 4
SSD_TN, SSD_TILES = 896, 3
SSD_PROJ_W = SSD_TN * SSD_TILES
OFF_RQ, OFF_RK, OFF_RV, OFF_RG = 0, 1024, 2048, 3072
assert RET_COL0 + RET_TN * RET_TILES == N_IN and OFF_SZ + SSD_PROJ_W >= OFF_SDT + LANES

NORM_TM = 512
SSD_PROJ_TM = 1024
MERGE_TM, MERGE_TN = 1024, 256
OUT_PROJ_TM = 512
FFN_TN = 256
FFN_ROW_CHUNK = 1024
DOWN_PROJ_TM = 512
DOWN_PROJ_TK = 2816
FFN_SAMPLE_TN = 512

HG_CHUNK = 128
HG_LEVELS = (1, 2, 4, 8, 16, 32, 64)
SSD_CHUNK = 128
RET_CHUNK = 256
SAMPLE_BLK = 32
SSD_SAMPLE_BLK = 8


def _cparams(n_axes):
    return pltpu.CompilerParams(dimension_semantics=("arbitrary",) * n_axes,
                                vmem_limit_bytes=VMEM_LIMIT_BYTES)


def _rms_rows(x):
    return x * lax.rsqrt(jnp.mean(x * x, axis=-1, keepdims=True) + EPS)


def _shift_rows(x, d, row):
    return jnp.where(row >= d, pltpu.roll(x, d, axis=0), 0.0)


def _nt_dot(a, b):
    return lax.dot_general(a, b, (((1,), (1,)), ((), ())), preferred_element_type=F32)


def _tn_dot(a, b):
    return lax.dot_general(a, b, (((0,), (0,)), ((), ())), preferred_element_type=F32)


def _project(h_ref, w_refs):
    w = jnp.concatenate([w_ref[0].astype(BF16) for w_ref in w_refs], axis=0)
    return _nt_dot(h_ref[...], w)


def _w_in_row_specs(layer, offsets, index_map_for):
    return [pl.BlockSpec((pl.Element(1), pl.Element(LANES), pl.Element(D_MODEL)), index_map_for(layer, off))
            for off in offsets]


def _rows_to_cols(x):
    n = x.shape[0]
    if n < LANES:
        x = jnp.concatenate([x, jnp.zeros((LANES - n, x.shape[1]), x.dtype)], axis=0)
    return x.T


def _norm_kernel(x_ref, g_ref, o_ref):
    o_ref[...] = (_rms_rows(x_ref[...]) * g_ref[...]).astype(o_ref.dtype)


def rmsnorm(x, g, out_dtype, tm):
    m, d = x.shape
    return pl.pallas_call(
        _norm_kernel,
        out_shape=jax.ShapeDtypeStruct((m, d), out_dtype),
        grid=(m // tm,),
        in_specs=[pl.BlockSpec((tm, d), lambda i: (i, 0)),
                  pl.BlockSpec((1, d), lambda i: (0, 0))],
        out_specs=pl.BlockSpec((tm, d), lambda i: (i, 0)),
        compiler_params=_cparams(1),
        name="rmsnorm",
    )(x, g.reshape(1, d))


def _mm_nt_cast_kernel(a_ref, wt_ref, o_ref, wb_ref):
    @pl.when(pl.program_id(1) == 0)
    def _():
        wb_ref[...] = wt_ref[0].astype(BF16)

    o_ref[...] = _nt_dot(a_ref[...], wb_ref[...])


def in_proj_prompt(layer, h, w_in_t, row0, n_tiles, tn, tm):
    m = h.shape[0]
    n = n_tiles * tn
    w_spec = pl.BlockSpec((pl.Element(1), pl.Element(tn), pl.Element(D_MODEL)),
                          lambda j, i: (layer, pl.multiple_of(row0 + j * tn, SUBLANES), 0))
    return pl.pallas_call(
        _mm_nt_cast_kernel,
        out_shape=(jax.ShapeDtypeStruct((m, n), F32), jax.ShapeDtypeStruct((n, D_MODEL), BF16)),
        grid=(n_tiles, m // tm),
        in_specs=[pl.BlockSpec((tm, D_MODEL), lambda j, i: (i, 0)), w_spec],
        out_specs=(pl.BlockSpec((tm, tn), lambda j, i: (i, j)),
                   pl.BlockSpec((tn, D_MODEL), lambda j, i: (j, 0))),
        compiler_params=_cparams(2),
        name="in_proj_prompt",
    )(h, w_in_t)


def _mm_nt_castw_kernel(a_ref, wt_ref, o_ref):
    o_ref[...] = _nt_dot(a_ref[...], wt_ref[0].astype(BF16))


def in_proj_sample(layer, h, w_in_t, row0, n_tiles, tn):
    m = h.shape[0]
    w_spec = pl.BlockSpec((pl.Element(1), pl.Element(tn), pl.Element(D_MODEL)),
                          lambda j: (layer, pl.multiple_of(row0 + j * tn, SUBLANES), 0))
    return pl.pallas_call(
        _mm_nt_castw_kernel,
        out_shape=jax.ShapeDtypeStruct((m, n_tiles * tn), F32),
        grid=(n_tiles,),
        in_specs=[pl.BlockSpec((m, D_MODEL), lambda j: (0, 0)), w_spec],
        out_specs=pl.BlockSpec((m, tn), lambda j: (0, j)),
        compiler_params=_cparams(1),
        name="in_proj_sample",
    )(h, w_in_t)


def _gated_sum(h, br_refs, gate_w, branch_w):
    acc = None
    for k, br_ref in enumerate(br_refs):
        gate = jax.nn.sigmoid(jnp.dot(h, gate_w(k), preferred_element_type=F32))
        br = jnp.dot(br_ref[...].astype(BF16), branch_w(k), preferred_element_type=F32)
        acc = gate * br if acc is None else acc + gate * br
    return acc


def _merge_cast_kernel(h_ref, a_ref, b_ref, c_ref, d_ref, wg_ref, wb_ref, o_ref, wbb_ref):
    @pl.when(pl.program_id(1) == 0)
    def _():
        for k in range(N_BRANCH):
            wbb_ref[k] = wb_ref[k].astype(BF16)

    acc = _gated_sum(h_ref[...], (a_ref, b_ref, c_ref, d_ref), lambda k: wg_ref[k], lambda k: wbb_ref[k])
    o_ref[...] = acc.astype(o_ref.dtype)


def gated_merge_prompt(layer, h, branches, w_gate_b, w_branch, tm, tn):
    m = h.shape[0]
    br_specs = [pl.BlockSpec((tm, BRANCH_W), lambda j, i: (i, 0)) for _ in range(N_BRANCH)]
    return pl.pallas_call(
        _merge_cast_kernel,
        out_shape=(jax.ShapeDtypeStruct((m, D_MODEL), BF16),
                   jax.ShapeDtypeStruct((N_BRANCH, BRANCH_W, D_MODEL), BF16)),
        grid=(D_MODEL // tn, m // tm),
        in_specs=[pl.BlockSpec((tm, D_MODEL), lambda j, i: (i, 0))] + br_specs
                 + [pl.BlockSpec((None, N_BRANCH, D_MODEL, tn), lambda j, i: (layer, 0, 0, j)),
                    pl.BlockSpec((None, N_BRANCH, BRANCH_W, tn), lambda j, i: (layer, 0, 0, j))],
        out_specs=(pl.BlockSpec((tm, tn), lambda j, i: (i, j)),
                   pl.BlockSpec((N_BRANCH, BRANCH_W, tn), lambda j, i: (0, 0, j))),
        compiler_params=_cparams(2),
        name="gated_merge_prompt",
    )(h, *branches, w_gate_b, w_branch)


def _merge_kernel(h_ref, a_ref, b_ref, c_ref, d_ref, wgb_ref, wbb_ref, o_ref):
    acc = _gated_sum(h_ref[...], (a_ref, b_ref, c_ref, d_ref), lambda k: wgb_ref[k], lambda k: wbb_ref[k])
    o_ref[...] = acc.astype(o_ref.dtype)


def gated_merge_sample(layer, h, branches, w_gate_b, w_branch_b, tn):
    m = h.shape[0]
    br_specs = [pl.BlockSpec((m, BRANCH_W), lambda j: (0, 0)) for _ in range(N_BRANCH)]
    return pl.pallas_call(
        _merge_kernel,
        out_shape=jax.ShapeDtypeStruct((m, D_MODEL), BF16),
        grid=(D_MODEL // tn,),
        in_specs=[pl.BlockSpec((m, D_MODEL), lambda j: (0, 0))] + br_specs
                 + [pl.BlockSpec((None, N_BRANCH, D_MODEL, tn), lambda j: (layer, 0, 0, j)),
                    pl.BlockSpec((N_BRANCH, BRANCH_W, tn), lambda j: (0, 0, j))],
        out_specs=pl.BlockSpec((m, tn), lambda j: (0, j)),
        compiler_params=_cparams(1),
        name="gated_merge_sample",
    )(h, *branches, w_gate_b, w_branch_b)


def _out_proj_kernel(m_ref, w_ref, x_ref, g_ref, xo_ref, ho_ref):
    x_new = x_ref[...] + jnp.dot(m_ref[...], w_ref[...], preferred_element_type=F32)
    xo_ref[...] = x_new
    ho_ref[...] = (_rms_rows(x_new) * g_ref[...]).astype(ho_ref.dtype)


def out_proj_residual_norm(layer, merged, w_out, x, g, tm):
    m = x.shape[0]
    return pl.pallas_call(
        _out_proj_kernel,
        out_shape=(jax.ShapeDtypeStruct((m, D_MODEL), F32), jax.ShapeDtypeStruct((m, D_MODEL), BF16)),
        grid=(m // tm,),
        in_specs=[pl.BlockSpec((tm, D_MODEL), lambda i: (i, 0)),
                  pl.BlockSpec((None, D_MODEL, D_MODEL), lambda i: (layer, 0, 0)),
                  pl.BlockSpec((tm, D_MODEL), lambda i: (i, 0)),
                  pl.BlockSpec((1, D_MODEL), lambda i: (0, 0))],
        out_specs=(pl.BlockSpec((tm, D_MODEL), lambda i: (i, 0)),
                   pl.BlockSpec((tm, D_MODEL), lambda i: (i, 0))),
        compiler_params=_cparams(1),
        name="out_proj",
    )(merged, w_out, x, g.reshape(1, D_MODEL))


def _down_proj_kernel(emit_x, a_ref, w_ref, x_ref, g_ref, *refs):
    if emit_x:
        xo_ref, no_ref, acc_ref = refs
    else:
        no_ref, acc_ref = refs
    kk = pl.program_id(1)

    @pl.when(kk == 0)
    def _():
        acc_ref[...] = x_ref[...]

    acc_ref[...] += jnp.dot(a_ref[...], w_ref[...], preferred_element_type=F32)

    @pl.when(kk == pl.num_programs(1) - 1)
    def _():
        x_new = acc_ref[...]
        if emit_x:
            xo_ref[...] = x_new
        no_ref[...] = (_rms_rows(x_new) * g_ref[...]).astype(no_ref.dtype)


def down_proj_residual_norm(layer, a, w_down, x, g, tm, tk, emit_x, norm_dtype):
    m = x.shape[0]
    out_shape = [jax.ShapeDtypeStruct((m, D_MODEL), norm_dtype)]
    out_specs = [pl.BlockSpec((tm, D_MODEL), lambda i, k: (i, 0))]
    if emit_x:
        out_shape = [jax.ShapeDtypeStruct((m, D_MODEL), F32)] + out_shape
        out_specs = [pl.BlockSpec((tm, D_MODEL), lambda i, k: (i, 0))] + out_specs
    return pl.pallas_call(
        functools.partial(_down_proj_kernel, emit_x),
        out_shape=tuple(out_shape),
        grid=(m // tm, D_FF // tk),
        in_specs=[pl.BlockSpec((tm, tk), lambda i, k: (i, k)),
                  pl.BlockSpec((None, tk, D_MODEL), lambda i, k: (layer, k, 0)),
                  pl.BlockSpec((tm, D_MODEL), lambda i, k: (i, 0)),
                  pl.BlockSpec((1, D_MODEL), lambda i, k: (0, 0))],
        out_specs=tuple(out_specs),
        scratch_shapes=[pltpu.VMEM((tm, D_MODEL), F32)],
        compiler_params=_cparams(2),
        name="down_proj",
    )(a, w_down, x, g.reshape(1, D_MODEL))


def _ffn_prompt_kernel(h_ref, wu_ref, wv_ref, cw_ref, cb_ref, a_ref, st_ref, wub_ref, wvb_ref):
    @pl.when(pl.program_id(1) == 0)
    def _():
        wub_ref[...] = wu_ref[...].astype(BF16)
        wvb_ref[...] = wv_ref[...].astype(BF16)

    rc = FFN_ROW_CHUNK
    tn = a_ref.shape[1]
    cw = cw_ref[...]
    cbias = cb_ref[...]
    row8 = lax.broadcasted_iota(jnp.int32, (SUBLANES, tn), 0)
    tail = jnp.zeros((SUBLANES, tn), F32)
    pending = None
    for c in range(h_ref.shape[0] // rc):
        rows = slice(c * rc, (c + 1) * rc)
        hc = h_ref[rows, :]
        u = jnp.dot(hc, wub_ref[...], preferred_element_type=F32)
        if pending is not None:
            prev_rows, g_prev, v_prev = pending
            a_ref[prev_rows, :] = (g_prev * v_prev).astype(a_ref.dtype)
        v = jnp.dot(hc, wvb_ref[...], preferred_element_type=F32)
        uc = cbias + cw[FFN_CONV_W - 1:FFN_CONV_W] * u
        for d in range(1, FFN_CONV_W):
            rolled = pltpu.roll(u, d, axis=0)
            top = jnp.where(row8 >= d, rolled[:SUBLANES], pltpu.roll(tail, d, axis=0))
            uc = uc + cw[FFN_CONV_W - 1 - d:FFN_CONV_W - d] * jnp.concatenate([top, rolled[SUBLANES:]], axis=0)
        pending = (rows, jax.nn.gelu(uc), v)
        tail = u[rc - SUBLANES:, :]
    prev_rows, g_prev, v_prev = pending
    a_ref[prev_rows, :] = (g_prev * v_prev).astype(a_ref.dtype)
    st_ref[...] = tail[SUBLANES - (FFN_CONV_W - 1):, :]


def ffn_prompt(layer, h2, w_up, w_val, conv_w, conv_b, tn):
    wspec = pl.BlockSpec((None, D_MODEL, tn), lambda j, b: (layer, 0, j))
    wbspec = pl.BlockSpec((D_MODEL, tn), lambda j, b: (0, j))
    return pl.pallas_call(
        _ffn_prompt_kernel,
        out_shape=(jax.ShapeDtypeStruct((BATCH * SEQ, D_FF), BF16),
                   jax.ShapeDtypeStruct((BATCH, FFN_CONV_W - 1, D_FF), F32),
                   jax.ShapeDtypeStruct((D_MODEL, D_FF), BF16),
                   jax.ShapeDtypeStruct((D_MODEL, D_FF), BF16)),
        grid=(D_FF // tn, BATCH),
        in_specs=[pl.BlockSpec((SEQ, D_MODEL), lambda j, b: (b, 0)), wspec, wspec,
                  pl.BlockSpec((FFN_CONV_W, tn), lambda j, b: (0, j)),
                  pl.BlockSpec((1, tn), lambda j, b: (0, j))],
        out_specs=(pl.BlockSpec((SEQ, tn), lambda j, b: (b, j)),
                   pl.BlockSpec((None, FFN_CONV_W - 1, tn), lambda j, b: (b, 0, j)),
                   wbspec, wbspec),
        compiler_params=_cparams(2),
        name="ffn_prompt",
    )(h2, w_up, w_val, conv_w, conv_b.reshape(1, D_FF))


def _ffn_sample_kernel(h_ref, wu_ref, wv_ref, cw_ref, cb_ref, buf_ref, a_ref, nb_ref):
    h = h_ref[...]
    u = jnp.dot(h, wu_ref[...], preferred_element_type=F32)
    v = jnp.dot(h, wv_ref[...], preferred_element_type=F32)
    cw = cw_ref[...]
    b0 = buf_ref[:, 0, :]
    b1 = buf_ref[:, 1, :]
    uc = cb_ref[...] + cw[0:1] * b0 + cw[1:2] * b1 + cw[2:3] * u
    a_ref[...] = (jax.nn.gelu(uc) * v).astype(a_ref.dtype)
    nb_ref[:, 0, :] = b1
    nb_ref[:, 1, :] = u


def ffn_sample(layer, h2, w_up, w_val, conv_w, conv_b, buf, tn):
    return pl.pallas_call(
        _ffn_sample_kernel,
        out_shape=(jax.ShapeDtypeStruct((DEC_BATCH, D_FF), BF16),
                   jax.ShapeDtypeStruct((DEC_BATCH, FFN_CONV_W - 1, D_FF), F32)),
        grid=(D_FF // tn,),
        in_specs=[pl.BlockSpec((DEC_BATCH, D_MODEL), lambda j: (0, 0)),
                  pl.BlockSpec((D_MODEL, tn), lambda j: (0, j)),
                  pl.BlockSpec((D_MODEL, tn), lambda j: (0, j)),
                  pl.BlockSpec((FFN_CONV_W, tn), lambda j: (0, j)),
                  pl.BlockSpec((1, tn), lambda j: (0, j)),
                  pl.BlockSpec((None, DEC_BATCH, FFN_CONV_W - 1, tn), lambda j: (layer, 0, 0, j))],
        out_specs=(pl.BlockSpec((DEC_BATCH, tn), lambda j: (0, j)),
                   pl.BlockSpec((DEC_BATCH, FFN_CONV_W - 1, tn), lambda j: (0, 0, j))),
        compiler_params=_cparams(1),
        name="ffn_sample",
    )(h2, w_up, w_val, conv_w, conv_b.reshape(1, D_FF), buf)


def _rope_table_kernel(start, consecutive, freq_ref, sign_ref, cos_ref, sin_ref):
    shape = cos_ref.shape
    if consecutive:
        pos = lax.broadcasted_iota(jnp.int32, shape, 0).astype(F32) + float(start)
    else:
        pos = jnp.full(shape, float(start), F32)
    ang = pos * freq_ref[...]
    cos_ref[...] = jnp.cos(ang)
    sin_ref[...] = sign_ref[...] * jnp.sin(ang)


def rope_tables(n_rows, start, consecutive):
    half = RET_DK // 2
    freqs = ROPE_BASE ** (-jnp.arange(half, dtype=F32) / half)
    freq2 = jnp.concatenate([freqs, freqs]).reshape(1, RET_DK)
    sign = jnp.concatenate([-jnp.ones((half,), F32), jnp.ones((half,), F32)]).reshape(1, RET_DK)
    return pl.pallas_call(
        functools.partial(_rope_table_kernel, start, consecutive),
        out_shape=(jax.ShapeDtypeStruct((n_rows, RET_DK), F32), jax.ShapeDtypeStruct((n_rows, RET_DK), F32)),
        name="rope_tables",
    )(freq2, sign)


def _rope(x, cos, sin_signed):
    return x * cos + pltpu.roll(x, RET_DK // 2, axis=1) * sin_signed


def _lru_gates(conv, wa_ref, ba_ref, wx_ref, bx_ref, lam_ref):
    xb = conv.astype(BF16)
    r = jax.nn.sigmoid(jnp.dot(xb, wa_ref[...].astype(BF16), preferred_element_type=F32) + ba_ref[...])
    i = jax.nn.sigmoid(jnp.dot(xb, wx_ref[...].astype(BF16), preferred_element_type=F32) + bx_ref[...])
    log_a = -LRU_C * r * jax.nn.softplus(-lam_ref[...])
    a = jnp.exp(log_a)
    u = jnp.sqrt(1.0 - a * a) * (i * conv)
    return a, u


def _lru_prompt_kernel(hin_ref, wxa_ref, wya_ref, cw_ref, cb_ref, wa_ref, ba_ref, wx_ref, bx_ref, lam_ref,
                       out_ref, h_ref, conv_ref, ag_s, ug_s):
    proj = _project(hin_ref, (wxa_ref, wya_ref))
    x = proj[:, :LRU_BLOCK]
    ya = proj[:, LRU_BLOCK:]
    t_len = x.shape[0]
    row = lax.broadcasted_iota(jnp.int32, x.shape, 0)
    cw = cw_ref[...]
    conv = cb_ref[...] + cw[CONV_W - 1:CONV_W] * x
    for d in range(1, CONV_W):
        conv = conv + cw[CONV_W - 1 - d:CONV_W - d] * _shift_rows(x, d, row)
    a, u = _lru_gates(conv, wa_ref, ba_ref, wx_ref, bx_ref, lam_ref)
    ng = t_len // SUBLANES
    a3 = a.reshape(ng, SUBLANES, LRU_BLOCK)
    u3 = u.reshape(ng, SUBLANES, LRU_BLOCK)
    sub = lax.broadcasted_iota(jnp.int32, a3.shape, 1)
    d = 1
    while d < SUBLANES:
        keep = sub >= d
        u3 = jnp.where(keep, a3 * pltpu.roll(u3, d, axis=1) + u3, u3)
        a3 = jnp.where(keep, a3 * pltpu.roll(a3, d, axis=1), a3)
        d *= 2
    ag_s[...] = a3.reshape(t_len, LRU_BLOCK)
    ug_s[...] = u3.reshape(t_len, LRU_BLOCK)
    ag = ag_s[pl.ds(SUBLANES - 1, ng, stride=SUBLANES), :]
    ug = ug_s[pl.ds(SUBLANES - 1, ng, stride=SUBLANES), :]
    grow = lax.broadcasted_iota(jnp.int32, ag.shape, 0)
    d = 1
    while d < ng:
        keep = grow >= d
        ug = jnp.where(keep, ag * pltpu.roll(ug, d, axis=0) + ug, ug)
        ag = jnp.where(keep, ag * pltpu.roll(ag, d, axis=0), ag)
        d *= 2
    carry = _shift_rows(ug, 1, grow)
    h3 = a3 * jnp.broadcast_to(carry[:, None, :], a3.shape) + u3
    hs = h3.reshape(t_len, LRU_BLOCK)
    out_ref[...] = (hs * jax.nn.gelu(ya)).astype(out_ref.dtype)
    h_ref[...] = ug[ng - 1:, :]
    conv_ref[...] = x[t_len - (CONV_W - 1):, :]


def _lru_param_specs(n_axes_fn):
    blk3 = lambda shape: pl.BlockSpec(shape, n_axes_fn(lambda n: (n, 0, 0)))
    return [pl.BlockSpec((CONV_W, LRU_BLOCK), n_axes_fn(lambda n: (0, n))),
            pl.BlockSpec((1, LRU_BLOCK), n_axes_fn(lambda n: (0, n))),
            blk3((None, LRU_BLOCK, LRU_BLOCK)), blk3((None, 1, LRU_BLOCK)),
            blk3((None, LRU_BLOCK, LRU_BLOCK)), blk3((None, 1, LRU_BLOCK)),
            blk3((None, 1, LRU_BLOCK))]


def _lru_params(p):
    return (p["lru_conv_w"], p["lru_conv_b"].reshape(1, BRANCH_W),
            p["lru_wa"], p["lru_ba"].reshape(LRU_BLOCKS, 1, LRU_BLOCK),
            p["lru_wx"], p["lru_bx"].reshape(LRU_BLOCKS, 1, LRU_BLOCK),
            p["lru_lambda"].reshape(LRU_BLOCKS, 1, LRU_BLOCK))


def _prompt_unit_rows(layer, off):
    return lambda b, u: (layer, pl.multiple_of(off + u * LANES, SUBLANES), 0)


def lru_prompt(layer, hin, w_in_t, p):
    wrap = lambda f: (lambda b, n: f(n))
    out, h, conv = pl.pallas_call(
        _lru_prompt_kernel,
        out_shape=(jax.ShapeDtypeStruct((BATCH * SEQ, BRANCH_W), BF16),
                   jax.ShapeDtypeStruct((BATCH, 1, BRANCH_W), F32),
                   jax.ShapeDtypeStruct((BATCH, CONV_W - 1, BRANCH_W), F32)),
        grid=(BATCH, LRU_BLOCKS),
        in_specs=[pl.BlockSpec((SEQ, D_MODEL), lambda b, n: (b, 0))]
                 + _w_in_row_specs(layer, (OFF_XA, OFF_YA), _prompt_unit_rows) + _lru_param_specs(wrap),
        out_specs=(pl.BlockSpec((SEQ, LRU_BLOCK), lambda b, n: (b, n)),
                   pl.BlockSpec((None, 1, LRU_BLOCK), lambda b, n: (b, 0, n)),
                   pl.BlockSpec((None, CONV_W - 1, LRU_BLOCK), lambda b, n: (b, 0, n))),
        scratch_shapes=[pltpu.VMEM((SEQ, LRU_BLOCK), F32)] * 2,
        compiler_params=_cparams(2),
        name="lru_prompt",
    )(hin, w_in_t, w_in_t, *_lru_params(p))
    return out, h.reshape(BATCH, BRANCH_W), conv


def _lru_sample_kernel(xa_ref, ya_ref, cw_ref, cb_ref, wa_ref, ba_ref, wx_ref, bx_ref, lam_ref,
                       h0_ref, buf_ref, out_ref, h_ref, nbuf_ref):
    x = xa_ref[...]
    cw = cw_ref[...]
    conv = cb_ref[...] + cw[CONV_W - 1:CONV_W] * x
    for j in range(CONV_W - 1):
        conv = conv + cw[j:j + 1] * buf_ref[j]
    a, u = _lru_gates(conv, wa_ref, ba_ref, wx_ref, bx_ref, lam_ref)
    h = a * h0_ref[...] + u
    out_ref[...] = h * jax.nn.gelu(ya_ref[...])
    h_ref[...] = h
    for j in range(CONV_W - 2):
        nbuf_ref[j] = buf_ref[j + 1]
    nbuf_ref[CONV_W - 2] = x


def lru_sample(proj, p, h0, buf_t):
    cb = lambda off: off // LRU_BLOCK
    wrap = lambda f: f
    return pl.pallas_call(
        _lru_sample_kernel,
        out_shape=(jax.ShapeDtypeStruct((DEC_BATCH, BRANCH_W), F32),
                   jax.ShapeDtypeStruct((DEC_BATCH, BRANCH_W), F32),
                   jax.ShapeDtypeStruct((CONV_W - 1, DEC_BATCH, BRANCH_W), F32)),
        grid=(LRU_BLOCKS,),
        in_specs=[pl.BlockSpec((DEC_BATCH, LRU_BLOCK), lambda n: (0, cb(OFF_XA) + n)),
                  pl.BlockSpec((DEC_BATCH, LRU_BLOCK), lambda n: (0, cb(OFF_YA) + n))]
                 + _lru_param_specs(wrap)
                 + [pl.BlockSpec((DEC_BATCH, LRU_BLOCK), lambda n: (0, n)),
                    pl.BlockSpec((CONV_W - 1, DEC_BATCH, LRU_BLOCK), lambda n: (0, 0, n))],
        out_specs=(pl.BlockSpec((DEC_BATCH, LRU_BLOCK), lambda n: (0, n)),
                   pl.BlockSpec((DEC_BATCH, LRU_BLOCK), lambda n: (0, n)),
                   pl.BlockSpec((CONV_W - 1, DEC_BATCH, LRU_BLOCK), lambda n: (0, 0, n))),
        compiler_params=_cparams(1),
        name="lru_sample",
    )(proj, proj, *_lru_params(p), h0, buf_t)


def _hgrn_lower_bound(layer, logits):
    mx = jnp.max(logits, axis=0, keepdims=True)
    e = jnp.exp(logits - mx)
    ls = e / jnp.sum(e, axis=0, keepdims=True)
    lb = jnp.zeros_like(ls[0:1])
    for i in range(1, layer + 1):
        lb = lb + ls[i:i + 1]
    return lb


def _hgrn_gates(layer, hq, hf, lbl_ref):
    lb = _hgrn_lower_bound(layer, lbl_ref[...])
    q = jax.nn.silu(hq)
    sg = jax.nn.sigmoid(hf)
    f = lb + (1.0 - lb) * sg
    k = (1.0 - lb) * (1.0 - sg)
    return q, f, k


def _hgrn_level_ids():
    c = HG_CHUNK
    t = np.arange(c)[:, None]
    s = np.arange(c)[None, :]
    level = np.zeros((c, c), np.int32)
    for li, m in enumerate(HG_LEVELS):
        same = (t // (2 * m)) == (s // (2 * m))
        level[same & ((t % (2 * m)) >= m) & ((s % (2 * m)) < m)] = li + 1
    level[t == s] = len(HG_LEVELS) + 1
    return level


def _split3_bf16(x):
    hi = x.astype(BF16)
    r1 = x - hi.astype(F32)
    mid = r1.astype(BF16)
    lo = (r1 - mid.astype(F32)).astype(BF16)
    return hi, mid, lo


def _cumsum_rows(tril_b, x):
    return sum(jnp.dot(tril_b, piece, preferred_element_type=F32) for piece in _split3_bf16(x))


def _hgrn_midpoint_factor(b, m):
    c = b.shape[0]
    if 2 * m >= SUBLANES:
        b3 = b.reshape(c // (2 * m), 2 * m, LANES)
        mid = b3[:, m - 1:m, :]
    else:
        b3 = b.reshape(c // SUBLANES, SUBLANES, LANES)
        sub = lax.broadcasted_iota(jnp.int32, b3.shape, 1)
        mid = b3[:, m - 1:m, :]
        for blk in range(1, SUBLANES // (2 * m)):
            lo = blk * 2 * m
            mid = jnp.where(sub >= lo, b3[:, lo + m - 1:lo + m, :], mid)
    return jnp.exp(-jnp.abs(b3 - mid)).reshape(c, LANES)


def _hgrn_prompt_kernel(layer, hin_ref, wq_ref, wf_ref, wi_ref, wg_ref, lbl_ref, nw_ref, lvl_ref,
                        out_ref, st_ref, lf_s, q_s, k_s, v_s, o_s):
    c = HG_CHUNK
    t_len = hin_ref.shape[0]
    proj = _project(hin_ref, (wq_ref, wf_ref, wi_ref, wg_ref))
    q, f, k = _hgrn_gates(layer, proj[:, 0:HG_DK], proj[:, HG_DK:2 * HG_DK], lbl_ref)
    lf_s[...] = jnp.log(f)
    q_s[...] = q
    k_s[...] = k
    v_s[...] = proj[:, 2 * HG_DK:3 * HG_DK].astype(BF16)
    gate = proj[:, 3 * HG_DK:]

    lvl = lvl_ref[...]
    tril_b = (lax.broadcasted_iota(jnp.int32, (c, c), 0) >= lax.broadcasted_iota(jnp.int32, (c, c), 1)).astype(BF16)

    st = jnp.zeros((HG_DK, HG_DK), F32)
    for ci in range(t_len // c):
        sl = slice(ci * c, (ci + 1) * c)
        qc = q_s[sl, :]
        kc = k_s[sl, :]
        vb = v_s[sl, :]
        b = _cumsum_rows(tril_b, lf_s[sl, :])
        att = jnp.where(lvl == len(HG_LEVELS) + 1, _nt_dot(qc.astype(BF16), kc.astype(BF16)), 0.0)
        for li, m in enumerate(HG_LEVELS):
            e = _hgrn_midpoint_factor(b, m)
            a_l = _nt_dot((qc * e).astype(BF16), (kc * e).astype(BF16))
            att = jnp.where(lvl == li + 1, a_l, att)
        o = (jnp.dot(att.astype(BF16), vb, preferred_element_type=F32)
             + _nt_dot((qc * jnp.exp(b)).astype(BF16), st.astype(BF16)))
        o_s[sl, :] = o
        bl = b[c - 1:c, :]
        kdec = (kc * jnp.exp(bl - b)).astype(BF16)
        st = st * jnp.exp(bl) + _tn_dot(vb, kdec)
    o = o_s[...]
    out_ref[...] = (_rms_rows(o) * nw_ref[...] * jax.nn.silu(gate)).astype(out_ref.dtype)
    st_ref[...] = st.T


def hgrn_prompt(layer, hin, w_in_t, p):
    level = _hgrn_level_ids()
    return pl.pallas_call(
        functools.partial(_hgrn_prompt_kernel, layer),
        out_shape=(jax.ShapeDtypeStruct((BATCH * SEQ, BRANCH_W), BF16),
                   jax.ShapeDtypeStruct((BATCH, HG_HEADS, HG_DK, HG_DK), F32)),
        grid=(BATCH, HG_HEADS),
        in_specs=[pl.BlockSpec((SEQ, D_MODEL), lambda b, h: (b, 0))]
                 + _w_in_row_specs(layer, (OFF_HQ, OFF_HF, OFF_HI, OFF_HG), _prompt_unit_rows)
                 + [pl.BlockSpec((DEPTH, HG_DK), lambda b, h: (0, h)),
                    pl.BlockSpec((1, HG_DK), lambda b, h: (0, 0)),
                    pl.BlockSpec(level.shape, lambda b, h: (0, 0))],
        out_specs=(pl.BlockSpec((SEQ, HG_DK), lambda b, h: (b, h)),
                   pl.BlockSpec((None, None, HG_DK, HG_DK), lambda b, h: (b, h, 0, 0))),
        scratch_shapes=[pltpu.VMEM((SEQ, HG_DK), F32)] * 3
                       + [pltpu.VMEM((SEQ, HG_DK), BF16), pltpu.VMEM((SEQ, HG_DK), F32)],
        compiler_params=_cparams(2),
        name="hgrn_prompt",
    )(hin, w_in_t, w_in_t, w_in_t, w_in_t, p["hg_lb_logits"], p["hg_norm_w"].reshape(1, HG_DK),
      jnp.asarray(level))


def _hgrn_sample_kernel(layer, q_ref, f_ref, i_ref, g_ref, lbl_ref, nw_ref, s_ref, out_ref, so_ref, o_s):
    q, f, k = _hgrn_gates(layer, q_ref[...], f_ref[...], lbl_ref)
    qc, fc, kc = _rows_to_cols(q), _rows_to_cols(f), _rows_to_cols(k)
    nb = q.shape[0]
    for j in range(nb):
        s_new = s_ref[j] * fc[:, j:j + 1] + kc[:, j:j + 1] * i_ref[j:j + 1, :]
        so_ref[j] = s_new
        o_s[j:j + 1, :] = jnp.sum(s_new * qc[:, j:j + 1], axis=0, keepdims=True)
    out_ref[...] = _rms_rows(o_s[...]) * nw_ref[...] * jax.nn.silu(g_ref[...])


def hgrn_sample(layer, proj, p, state, state_out):
    cb = lambda off: off // HG_DK
    nb = SAMPLE_BLK
    col = lambda off: pl.BlockSpec((nb, HG_DK), lambda h, i: (i, cb(off) + h))
    st_spec = pl.BlockSpec((None, nb, None, HG_DK, HG_DK), lambda h, i: (layer, i, h, 0, 0))
    args = [proj, proj, proj, proj, p["hg_lb_logits"], p["hg_norm_w"].reshape(1, HG_DK), state]
    in_specs = [col(OFF_HQ), col(OFF_HF), col(OFF_HI), col(OFF_HG),
                pl.BlockSpec((DEPTH, HG_DK), lambda h, i: (0, h)),
                pl.BlockSpec((1, HG_DK), lambda h, i: (0, 0)),
                st_spec]
    aliases = {}
    kern = functools.partial(_hgrn_sample_kernel, layer)
    if state_out is not None:
        args.append(state_out)
        in_specs.append(pl.BlockSpec(memory_space=pl.ANY))
        aliases = {len(args) - 1: 1}
        kern = functools.partial(_drop_alias_arg, kern, 7)
    return pl.pallas_call(
        kern,
        out_shape=(jax.ShapeDtypeStruct((DEC_BATCH, BRANCH_W), F32),
                   jax.ShapeDtypeStruct(state.shape, F32)),
        grid=(HG_HEADS, DEC_BATCH // nb),
        in_specs=in_specs,
        out_specs=(pl.BlockSpec((nb, HG_DK), lambda h, i: (i, h)), st_spec),
        scratch_shapes=[pltpu.VMEM((nb, HG_DK), F32)],
        input_output_aliases=aliases,
        compiler_params=_cparams(2),
        name="hgrn_sample",
    )(*args)


def _drop_alias_arg(kern, pos, *refs):
    return kern(*refs[:pos], *refs[pos + 1:])


def _head_pair(cols, h0, lane_lo):
    return jnp.where(lane_lo, cols[:, h0:h0 + 1], cols[:, h0 + 1:h0 + 2])


def _ssd_prompt_kernel(z_ref, x_ref, bc_ref, dt_ref, cw_ref, cb_ref, dtb_ref, alog_ref, dpar_ref, nw_ref,
                       out_ref, st_ref, cst_ref, cx_s, cbc_s, s_s):
    c = SSD_CHUNK
    ci = pl.program_id(1)

    @pl.when(ci == 0)
    def _():
        cx_s[...] = jnp.zeros_like(cx_s)
        cbc_s[...] = jnp.zeros_like(cbc_s)
        s_s[...] = jnp.zeros_like(s_s)

    cw = cw_ref[...]
    cbias = cb_ref[...]

    def conv_silu(raw, carry_ref, lo, hi):
        xx = jnp.concatenate([carry_ref[...], raw], axis=0)
        y = cbias[:, lo:hi] + cw[CONV_W - 1:CONV_W, lo:hi] * raw
        for d in range(1, CONV_W):
            y = y + cw[CONV_W - 1 - d:CONV_W - d, lo:hi] * pltpu.roll(xx, d, axis=0)[SUBLANES:]
        carry_ref[...] = raw[c - SUBLANES:, :]
        return jax.nn.silu(y)

    x_raw = x_ref[...]
    bc_raw = bc_ref[...]
    xs = conv_silu(x_raw, cx_s, 0, BRANCH_W)
    bc = conv_silu(bc_raw, cbc_s, BRANCH_W, SSD_CONV_DIM)

    dt = jax.nn.softplus(dt_ref[...] + dtb_ref[...])
    a_neg = -jnp.exp(alog_ref[...])
    logd = dt * a_neg
    tri = lax.broadcasted_iota(jnp.int32, (c, c), 0) >= lax.broadcasted_iota(jnp.int32, (c, c), 1)
    b = _cumsum_rows(tri.astype(BF16), logd)
    b_t = b.T
    bl = b[c - 1:c, :]
    e_in = jnp.exp(b)
    w_out = jnp.exp(bl - b)
    e_last = jnp.exp(bl)
    dfull = dpar_ref[...]

    lane_lo =lax.broadcasted_iota(jnp.int32, (c, LANES), 1) < SSD_HEADDIM
    lane_lo_row = lax.broadcasted_iota(jnp.int32, (1, LANES), 1) < SSD_HEADDIM

    ys = []
    for g in range(SSD_GROUPS):
        bm = bc[:, g * SSD_STATE:(g + 1) * SSD_STATE].astype(BF16)
        cm = bc[:, (SSD_GROUPS + g) * SSD_STATE:(SSD_GROUPS + g + 1) * SSD_STATE].astype(BF16)
        gmat = _nt_dot(cm, bm)
        for pp in range(SSD_HEADS // SSD_GROUPS // 2):
            pi = g * (SSD_HEADS // SSD_GROUPS // 2) + pp
            h0 = 2 * pi
            xs_p = xs[:, pi * LANES:(pi + 1) * LANES]
            vdt = xs_p * _head_pair(dt, h0, lane_lo)
            vdt_b = vdt.astype(BF16)
            o_heads = []
            for hh in (h0, h0 + 1):
                diff = b[:, hh:hh + 1] - b_t[hh:hh + 1, :]
                dec = jnp.where(tri, jnp.exp(jnp.where(tri, diff, 0.0)), 0.0)
                o_heads.append(jnp.dot((gmat * dec).astype(BF16), vdt_b, preferred_element_type=F32))
            o_intra = jnp.where(lane_lo, o_heads[0], o_heads[1])
            s_p = s_s[pi]
            o_inter = _head_pair(e_in, h0, lane_lo) * jnp.dot(cm, s_p.astype(BF16), preferred_element_type=F32)
            ys.append(o_intra + o_inter + dfull[:, pi * LANES:(pi + 1) * LANES] * xs_p)
            upd = _tn_dot(bm, (vdt * _head_pair(w_out, h0, lane_lo)).astype(BF16))
            s_s[pi] = s_p * _head_pair(e_last, h0, lane_lo_row) + upd

    y = jnp.concatenate(ys, axis=1) * jax.nn.silu(z_ref[...])
    gw = BRANCH_W // SSD_GROUPS
    nw = nw_ref[...]
    outs = [_rms_rows(y[:, g * gw:(g + 1) * gw]) * nw[:, g * gw:(g + 1) * gw] for g in range(SSD_GROUPS)]
    out_ref[...] = jnp.concatenate(outs, axis=1).astype(out_ref.dtype)

    @pl.when(ci == pl.num_programs(1) - 1)
    def _():
        for pi in range(SSD_HEADS // 2):
            s_t = s_s[pi].T
            st_ref[2 * pi] = s_t[:SSD_HEADDIM, :]
            st_ref[2 * pi + 1] = s_t[SSD_HEADDIM:, :]
        cst_ref[:, 0:BRANCH_W] = x_raw[c - (CONV_W - 1):, :]
        cst_ref[:, BRANCH_W:SSD_CONV_DIM] = bc_raw[c - (CONV_W - 1):, :]


def _pad_lanes(v):
    return jnp.pad(v.astype(F32), (0, LANES - v.shape[0])).reshape(1, LANES)


def _ssd_params(p):
    return (p["ssd_conv_w"], p["ssd_conv_b"].reshape(1, SSD_CONV_DIM), _pad_lanes(p["ssd_dt_bias"]),
            _pad_lanes(p["ssd_a_log"]), jnp.repeat(p["ssd_d"].astype(F32), SSD_HEADDIM).reshape(1, BRANCH_W),
            p["ssd_norm_w"].reshape(1, BRANCH_W))


def ssd_prompt(proj, p):
    c = SSD_CHUNK
    nc = SEQ // c
    const = lambda shape: pl.BlockSpec(shape, lambda b, i: (0, 0))
    rowblk = lambda w, off: pl.BlockSpec((c, w), lambda b, i: (b * nc + i, (off - OFF_SZ) // w))
    return pl.pallas_call(
        _ssd_prompt_kernel,
        out_shape=(jax.ShapeDtypeStruct((BATCH * SEQ, BRANCH_W), BF16),
                   jax.ShapeDtypeStruct((BATCH, SSD_HEADS, SSD_HEADDIM, SSD_STATE), F32),
                   jax.ShapeDtypeStruct((BATCH, CONV_W - 1, SSD_CONV_DIM), F32)),
        grid=(BATCH, nc),
        in_specs=[rowblk(BRANCH_W, OFF_SZ), rowblk(BRANCH_W, OFF_SX), rowblk(SSD_BC, OFF_SBC),
                  rowblk(LANES, OFF_SDT),
                  const((CONV_W, SSD_CONV_DIM)), const((1, SSD_CONV_DIM)), const((1, LANES)),
                  const((1, LANES)), const((1, BRANCH_W)), const((1, BRANCH_W))],
        out_specs=(pl.BlockSpec((c, BRANCH_W), lambda b, i: (b * nc + i, 0)),
                   pl.BlockSpec((None, SSD_HEADS, SSD_HEADDIM, SSD_STATE), lambda b, i: (b, 0, 0, 0)),
                   pl.BlockSpec((None, CONV_W - 1, SSD_CONV_DIM), lambda b, i: (b, 0, 0))),
        scratch_shapes=[pltpu.VMEM((SUBLANES, BRANCH_W), F32), pltpu.VMEM((SUBLANES, SSD_BC), F32),
                        pltpu.VMEM((SSD_HEADS // 2, SSD_STATE, LANES), F32)],
        compiler_params=_cparams(2),
        name="ssd_prompt",
    )(proj, proj, proj, proj, *_ssd_params(p))


def _ssd_sample_kernel(z_ref, x_ref, bc_ref, dt_ref, cw_ref, cb_ref, dtb_ref, alog_ref, dpar_ref, nw_ref,
                       bufx_ref, bufbc_ref, s_ref, out_ref, so_ref, nbx_ref, nbbc_ref, y_s):
    cw = cw_ref[...]
    cbias = cb_ref[...]

    def conv_silu(raw, buf_ref, nbuf_ref, lo, hi):
        y = cbias[:, lo:hi] + cw[CONV_W - 1:CONV_W, lo:hi] * raw
        for j in range(CONV_W - 1):
            y = y + cw[j:j + 1, lo:hi] * buf_ref[j]
        for j in range(CONV_W - 2):
            nbuf_ref[j] = buf_ref[j + 1]
        nbuf_ref[CONV_W - 2] = raw
        return jax.nn.silu(y)

    xs = conv_silu(x_ref[...], bufx_ref, nbx_ref, 0, BRANCH_W)
    bc = conv_silu(bc_ref[...], bufbc_ref, nbbc_ref, BRANCH_W, SSD_CONV_DIM)
    dt = jax.nn.softplus(dt_ref[...] + dtb_ref[...])
    decay = jnp.exp(dt * (-jnp.exp(alog_ref[...])))
    nb = xs.shape[0]
    hpg = SSD_HEADS // SSD_GROUPS
    lane_lo = lax.broadcasted_iota(jnp.int32, (nb, LANES), 1) < SSD_HEADDIM
    for pi in range(SSD_HEADS // 2):
        h0 = 2 * pi
        g = h0 // hpg
        xdt_cols = _rows_to_cols(xs[:, pi * LANES:(pi + 1) * LANES] * _head_pair(dt, h0, lane_lo))
        for e in range(2):
            h = h0 + e
            for j in range(nb):
                xcol = xdt_cols[e * SSD_HEADDIM:(e + 1) * SSD_HEADDIM, j:j + 1]
                brow = bc[j:j + 1, g * SSD_STATE:(g + 1) * SSD_STATE]
                so_ref[j, h] = s_ref[j, h] * decay[j:j + 1, h:h + 1] + xcol * brow
    for g in range(SSD_GROUPS):
        cm = bc[:, (SSD_GROUPS + g) * SSD_STATE:(SSD_GROUPS + g + 1) * SSD_STATE].astype(BF16)
        for j in range(nb):
            s_new = so_ref[j, g * hpg:(g + 1) * hpg].reshape(hpg * SSD_HEADDIM, SSD_STATE)
            y_s[j:j + 1, g * hpg * SSD_HEADDIM:(g + 1) * hpg * SSD_HEADDIM] = _nt_dot(cm, s_new.astype(BF16))[j:j + 1, :]
    y = (y_s[...] + dpar_ref[...] * xs) * jax.nn.silu(z_ref[...])
    gw = BRANCH_W // SSD_GROUPS
    nw = nw_ref[...]
    outs = [_rms_rows(y[:, g * gw:(g + 1) * gw]) * nw[:, g * gw:(g + 1) * gw] for g in range(SSD_GROUPS)]
    out_ref[...] = jnp.concatenate(outs, axis=1)


def ssd_sample(layer, proj, p, state, state_out, buf_t):
    nb = SSD_SAMPLE_BLK
    const = lambda shape: pl.BlockSpec(shape, lambda i: (0,) * len(shape))
    rowblk = lambda w, off: pl.BlockSpec((nb, w), lambda i: (i, off // w))
    st_spec = pl.BlockSpec((None, nb, SSD_HEADS, SSD_HEADDIM, SSD_STATE), lambda i: (layer, i, 0, 0, 0))
    bufx_spec = pl.BlockSpec((CONV_W - 1, nb, BRANCH_W), lambda i: (0, i, 0))
    bufbc_spec = pl.BlockSpec((CONV_W - 1, nb, SSD_BC), lambda i: (0, i, BRANCH_W // SSD_BC))
    args = [proj, proj, proj, proj, *_ssd_params(p), buf_t, buf_t, state]
    in_specs = [rowblk(BRANCH_W, OFF_SZ), rowblk(BRANCH_W, OFF_SX), rowblk(SSD_BC, OFF_SBC), rowblk(LANES, OFF_SDT),
                const((CONV_W, SSD_CONV_DIM)), const((1, SSD_CONV_DIM)), const((1, LANES)), const((1, LANES)),
                const((1, BRANCH_W)), const((1, BRANCH_W)), bufx_spec, bufbc_spec, st_spec]
    aliases = {}
    kern = _ssd_sample_kernel
    if state_out is not None:
        args.append(state_out)
        in_specs.append(pl.BlockSpec(memory_space=pl.ANY))
        aliases = {len(args) - 1: 1}
        kern = functools.partial(_drop_alias_arg, kern, 13)
    out, st, nbx, nbbc = pl.pallas_call(
        kern,
        out_shape=(jax.ShapeDtypeStruct((DEC_BATCH, BRANCH_W), F32),
                   jax.ShapeDtypeStruct(state.shape, F32),
                   jax.ShapeDtypeStruct((CONV_W - 1, DEC_BATCH, BRANCH_W), F32),
                   jax.ShapeDtypeStruct((CONV_W - 1, DEC_BATCH, SSD_BC), F32)),
        grid=(DEC_BATCH // nb,),
        in_specs=in_specs,
        out_specs=(pl.BlockSpec((nb, BRANCH_W), lambda i: (i, 0)), st_spec,
                   pl.BlockSpec((CONV_W - 1, nb, BRANCH_W), lambda i: (0, i, 0)),
                   pl.BlockSpec((CONV_W - 1, nb, SSD_BC), lambda i: (0, i, 0))),
        scratch_shapes=[pltpu.VMEM((nb, BRANCH_W), F32)],
        input_output_aliases=aliases,
        compiler_params=_cparams(1),
        name="ssd_sample",
    )(*args)
    return out, st, jnp.concatenate([nbx, nbbc], axis=-1)


def _ret_prompt_kernel(hin_ref, wq_ref, wk_ref, wv_ref, wg_ref, cos_ref, sin_ref, lg_ref, out_ref, st_ref,
                       q_s, k_s, v_s, o_s):
    c = RET_CHUNK
    t_len = hin_ref.shape[0]
    proj = _project(hin_ref, (wq_ref, wk_ref, wv_ref, wg_ref))
    cos = cos_ref[...]
    sin = sin_ref[...]
    q_s[...] = _rope(proj[:, 0:RET_DK], cos, sin)
    k_s[...] = _rope(proj[:, RET_DK:2 * RET_DK], cos, sin) * RET_DK ** -0.5
    v_s[...] = proj[:, 2 * RET_DK:3 * RET_DK].astype(BF16)
    gate = proj[:, 3 * RET_DK:]
    lg = lg_ref[...]
    lg128 = lg[:, :LANES]
    ti = lax.broadcasted_iota(jnp.int32, (c, c), 0)
    si = lax.broadcasted_iota(jnp.int32, (c, c), 1)
    tri = ti >= si
    dec = jnp.where(tri, jnp.exp(jnp.where(tri, (ti - si).astype(F32) * lg, 0.0)), 0.0)
    tt = lax.broadcasted_iota(jnp.int32, (c, LANES), 0).astype(F32)
    g_in = jnp.exp((tt + 1.0) * lg128)
    g_out = jnp.exp((c - 1.0 - tt) * lg128)
    g_all = jnp.exp(float(c) * lg128)
    s = jnp.zeros((RET_DK, RET_DK), F32)
    for ci in range(t_len // c):
        sl = slice(ci * c, (ci + 1) * c)
        qc = q_s[sl, :]
        kc = k_s[sl, :]
        vb = v_s[sl, :]
        scores = _nt_dot(qc.astype(BF16), kc.astype(BF16)) * dec
        o_s[sl, :] = (jnp.dot(scores.astype(BF16), vb, preferred_element_type=F32)
                      + jnp.dot((qc * g_in).astype(BF16), s.astype(BF16), preferred_element_type=F32))
        s = s * g_all + _tn_dot((kc * g_out).astype(BF16), vb)
    out_ref[...] = (_rms_rows(o_s[...]) * jax.nn.silu(gate)).astype(out_ref.dtype)
    st_ref[...] = s


def _log_gamma_rows(width):
    lg = jnp.log1p(-jnp.exp2(-5.0 - jnp.arange(RET_HEADS, dtype=F32)))
    return jnp.broadcast_to(lg[:, None, None], (RET_HEADS, 1, width))


def ret_prompt(layer, hin, w_in_t, cos, sin):
    tab = pl.BlockSpec((SEQ, RET_DK), lambda b, h: (0, 0))
    ret_rows = tuple(RET_COL0 + off for off in (OFF_RQ, OFF_RK, OFF_RV, OFF_RG))
    return pl.pallas_call(
        _ret_prompt_kernel,
        out_shape=(jax.ShapeDtypeStruct((BATCH * SEQ, BRANCH_W), BF16),
                   jax.ShapeDtypeStruct((BATCH, RET_HEADS, RET_DK, RET_DK), F32)),
        grid=(BATCH, RET_HEADS),
        in_specs=[pl.BlockSpec((SEQ, D_MODEL), lambda b, h: (b, 0))]
                 + _w_in_row_specs(layer, ret_rows, _prompt_unit_rows)
                 + [tab, tab, pl.BlockSpec((None, 1, RET_CHUNK), lambda b, h: (h, 0, 0))],
        out_specs=(pl.BlockSpec((SEQ, RET_DK), lambda b, h: (b, h)),
                   pl.BlockSpec((None, None, RET_DK, RET_DK), lambda b, h: (b, h, 0, 0))),
        scratch_shapes=[pltpu.VMEM((SEQ, RET_DK), F32), pltpu.VMEM((SEQ, RET_DK), F32),
                        pltpu.VMEM((SEQ, RET_DK), BF16), pltpu.VMEM((SEQ, RET_DK), F32)],
        compiler_params=_cparams(2),
        name="ret_prompt",
    )(hin, w_in_t, w_in_t, w_in_t, w_in_t, cos, sin, _log_gamma_rows(RET_CHUNK))


def _ret_sample_kernel(q_ref, k_ref, v_ref, g_ref, cos_ref, sin_ref, lg_ref, s_ref, out_ref, so_ref, o_s):
    cos = cos_ref[0:1, :]
    sin = sin_ref[0:1, :]
    q = _rope(q_ref[...], cos, sin)
    k = _rope(k_ref[...], cos, sin) * RET_DK ** -0.5
    v = v_ref[...]
    gamma = jnp.exp(lg_ref[...])
    qc, kc = _rows_to_cols(q), _rows_to_cols(k)
    nb = q.shape[0]
    for j in range(nb):
        s_new = s_ref[j] * gamma + kc[:, j:j + 1] * v[j:j + 1, :]
        so_ref[j] = s_new
        o_s[j:j + 1, :] = jnp.sum(s_new * qc[:, j:j + 1], axis=0, keepdims=True)
    out_ref[...] = _rms_rows(o_s[...]) * jax.nn.silu(g_ref[...])


def ret_sample(layer, proj, cos, sin, state, state_out):
    cb = lambda off: off // RET_DK
    nb = SAMPLE_BLK
    col = lambda off: pl.BlockSpec((nb, RET_DK), lambda h, i: (i, cb(off) + h))
    tab = pl.BlockSpec((SUBLANES, RET_DK), lambda h, i: (0, 0))
    st_spec = pl.BlockSpec((None, nb, None, RET_DK, RET_DK), lambda h, i: (layer, i, h, 0, 0))
    args = [proj, proj, proj, proj, cos, sin, _log_gamma_rows(LANES), state]
    in_specs = [col(OFF_RQ), col(OFF_RK), col(OFF_RV), col(OFF_RG), tab, tab,
                pl.BlockSpec((None, 1, LANES), lambda h, i: (h, 0, 0)), st_spec]
    aliases = {}
    kern = _ret_sample_kernel
    if state_out is not None:
        args.append(state_out)
        in_specs.append(pl.BlockSpec(memory_space=pl.ANY))
        aliases = {len(args) - 1: 1}
        kern = functools.partial(_drop_alias_arg, kern, len(args) - 1)
    return pl.pallas_call(
        kern,
        out_shape=(jax.ShapeDtypeStruct((DEC_BATCH, BRANCH_W), F32),
                   jax.ShapeDtypeStruct(state.shape, F32)),
        grid=(RET_HEADS, DEC_BATCH // nb),
        in_specs=in_specs,
        out_specs=(pl.BlockSpec((nb, RET_DK), lambda h, i: (i, h)), st_spec),
        scratch_shapes=[pltpu.VMEM((nb, RET_DK), F32)],
        input_output_aliases=aliases,
        compiler_params=_cparams(2),
        name="ret_sample",
    )(*args)


def kernel(x_prompt, x_sample, state_lru_h, state_lru_conv, state_hgrn, state_ssd, state_ssd_conv, state_ret, state_ffn_conv, g_mix, g_ffn, w_in, lru_conv_w, lru_conv_b, lru_wa, lru_ba, lru_wx, lru_bx, lru_lambda, hg_lb_logits, hg_norm_w, ssd_conv_w, ssd_conv_b, ssd_dt_bias, ssd_a_log, ssd_d, ssd_norm_w, w_branch, w_gate, w_out, ffn_w_up, ffn_w_val, ffn_conv_w, ffn_conv_b, ffn_w_down, g_final):
    xp = x_prompt.reshape(BATCH * SEQ, D_MODEL)
    xs = x_sample.reshape(DEC_BATCH, D_MODEL)

    cos_p, sin_p = rope_tables(SEQ, 0, True)
    cos_s, sin_s = rope_tables(SUBLANES, PAST_LEN, False)

    hp = rmsnorm(xp, g_mix[0], BF16, NORM_TM)
    hs = rmsnorm(xs, g_mix[0], BF16, DEC_BATCH)

    state_ssd_t = jnp.swapaxes(state_ssd, -1, -2)

    w_in_t = jnp.swapaxes(w_in, 1, 2)
    w_gate_b = jnp.transpose(w_gate, (0, 2, 1, 3)).astype(BF16)
    w_out_b = w_out.astype(BF16)
    w_down_b = ffn_w_down.astype(BF16)

    prompt_states, sample_small = [], []
    hg_out = ssd_out = ret_out = None
    for l in range(DEPTH):
        p = {"lru_conv_w": lru_conv_w[l], "lru_conv_b": lru_conv_b[l], "lru_wa": lru_wa[l], "lru_ba": lru_ba[l],
             "lru_wx": lru_wx[l], "lru_bx": lru_bx[l], "lru_lambda": lru_lambda[l],
             "hg_lb_logits": hg_lb_logits, "hg_norm_w": hg_norm_w[l],
             "ssd_conv_w": ssd_conv_w[l], "ssd_conv_b": ssd_conv_b[l], "ssd_dt_bias": ssd_dt_bias[l],
             "ssd_a_log": ssd_a_log[l], "ssd_d": ssd_d[l], "ssd_norm_w": ssd_norm_w[l]}
        last = l == DEPTH - 1
        g_next = g_final if last else g_mix[l + 1]

        out_a, lru_h_p, lru_conv_p = lru_prompt(l, hp, w_in_t, p)
        out_b, hg_p = hgrn_prompt(l, hp, w_in_t, p)
        proj_ssd_p, _ = in_proj_prompt(l, hp, w_in_t, OFF_SZ, SSD_TILES, SSD_TN, SSD_PROJ_TM)
        out_c, ssd_p, ssd_conv_p = ssd_prompt(proj_ssd_p, p)
        out_d, ret_p = ret_prompt(l, hp, w_in_t, cos_p, sin_p)
        merged, w_branch_b = gated_merge_prompt(l, hp, (out_a, out_b, out_c, out_d), w_gate_b, w_branch,
                                                MERGE_TM, MERGE_TN)
        xp, h2 = out_proj_residual_norm(l, merged, w_out_b, xp, g_ffn[l], OUT_PROJ_TM)
        act, ffn_conv_p, w_up_b, w_val_b = ffn_prompt(l, h2, ffn_w_up, ffn_w_val, ffn_conv_w[l], ffn_conv_b[l], FFN_TN)
        res = down_proj_residual_norm(l, act, w_down_b, xp, g_next, DOWN_PROJ_TM, DOWN_PROJ_TK, not last,
                                      F32 if last else BF16)
        if last:
            (yp,) = res
        else:
            xp, hp = res
        prompt_states.append((lru_h_p, lru_conv_p, hg_p, ssd_p, ssd_conv_p, ret_p, ffn_conv_p))

        proj_s = in_proj_sample(l, hs, w_in_t, 0, MAIN_TILES, MAIN_TN)
        proj_ret_s = in_proj_sample(l, hs, w_in_t, RET_COL0, RET_TILES, RET_TN)
        lru_buf_t = jnp.swapaxes(state_lru_conv[l], 0, 1)
        ssd_buf_t = jnp.swapaxes(state_ssd_conv[l], 0, 1)
        s_a, lru_h_s, lru_nbuf = lru_sample(proj_s, p, state_lru_h[l], lru_buf_t)
        s_b, hg_out = hgrn_sample(l, proj_s, p, state_hgrn, hg_out)
        s_c, ssd_out, ssd_nbuf = ssd_sample(l, proj_s, p, state_ssd_t, ssd_out, ssd_buf_t)
        s_d, ret_out = ret_sample(l, proj_ret_s, cos_s, sin_s, state_ret, ret_out)
        merged_s = gated_merge_sample(l, hs, (s_a, s_b, s_c, s_d), w_gate_b, w_branch_b, MERGE_TN)
        xs, h2s = out_proj_residual_norm(l, merged_s, w_out_b, xs, g_ffn[l], DEC_BATCH)
        act_s, ffn_nbuf = ffn_sample(l, h2s, w_up_b, w_val_b, ffn_conv_w[l], ffn_conv_b[l], state_ffn_conv,
                                       FFN_SAMPLE_TN)
        res = down_proj_residual_norm(l, act_s, w_down_b, xs, g_next, DEC_BATCH, DOWN_PROJ_TK, not last,
                                      F32 if last else BF16)
        if last:
            (ys,) = res
        else:
            xs, hs = res
        sample_small.append((lru_h_s, jnp.swapaxes(lru_nbuf, 0, 1), jnp.swapaxes(ssd_nbuf, 0, 1), ffn_nbuf))

    stack_p = lambda i: jnp.stack([st[i] for st in prompt_states], axis=0)
    stack_s = lambda i: jnp.stack([st[i] for st in sample_small], axis=0)
    return (yp.reshape(BATCH, SEQ, D_MODEL), ys.reshape(DEC_BATCH, 1, D_MODEL),
            stack_p(0), stack_s(0), stack_p(1), stack_s(1),
            stack_p(2), hg_out, jnp.swapaxes(stack_p(3), -1, -2), jnp.swapaxes(ssd_out, -1, -2),
            stack_p(4), stack_s(2), stack_p(5), ret_out,
            stack_p(6), stack_s(3))
```

```python
import functools

import numpy as np
import jax
import jax.numpy as jnp
from jax import lax
from jax.experimental import pallas as pl
from jax.experimental.pallas import tpu as pltpu

F32 = jnp.float32
BF16 = jnp.bfloat16

D_MODEL = 2048
BATCH = 4
SEQ = 2048
DEPTH = 2
DEC_BATCH = 128
PAST_LEN = 16384
BRANCH_W = D_MODEL // 2
EPS = 1e-6
LRU_BLOCKS = 8
LRU_BLOCK = BRANCH_W // LRU_BLOCKS
LRU_C = 8.0
CONV_W = 4
HG_HEADS = 8
HG_DK = BRANCH_W // HG_HEADS
SSD_HEADDIM = 64
SSD_HEADS = BRANCH_W // SSD_HEADDIM
SSD_GROUPS = 2
SSD_STATE = 128
SSD_BC = 2 * SSD_GROUPS * SSD_STATE
SSD_CONV_DIM = BRANCH_W + SSD_BC
RET_HEADS = 8
RET_DK = BRANCH_W // RET_HEADS
ROPE_BASE = 10000.0
D_FF = 5632
FFN_CONV_W = 3

V7X_VMEM_BYTES = 64 * 1024 * 1024
VMEM_LIMIT_BYTES = V7X_VMEM_BYTES - 8 * 1024 * 1024
LANES = 128
SUBLANES = 8

N_BRANCH = 4
N_IN = 12816
OFF_XA, OFF_YA = 0, 1024
OFF_HQ, OFF_HF, OFF_HI, OFF_HG = 2048, 3072, 4096, 5120
OFF_SZ, OFF_SX, OFF_SBC = 6144, 7168, 8192
OFF_SDT = 8704
MAIN_TN, MAIN_TILES = 1280, 7
N_MAIN = MAIN_TN * MAIN_TILES
RET_COL0 = OFF_SDT + SSD_HEADS
RET_TN, RET_TILES = 1024, 4
SSD_TN, SSD_TILES = 896, 3
SSD_PROJ_W = SSD_TN * SSD_TILES
OFF_RQ, OFF_RK, OFF_RV, OFF_RG = 0, 1024, 2048, 3072
assert RET_COL0 + RET_TN * RET_TILES == N_IN and OFF_SZ + SSD_PROJ_W >= OFF_SDT + LANES

NORM_TM = 512
SSD_PROJ_TM = 1024
MERGE_TM, MERGE_TN = 1024, 256
OUT_PROJ_TM = 512
FFN_TN = 256
FFN_ROW_CHUNK = 1024
DOWN_PROJ_TM = 512
DOWN_PROJ_TK = 2816
FFN_SAMPLE_TN = 512

HG_CHUNK = 128
HG_LEVELS = (1, 2, 4, 8, 16, 32, 64)
SSD_CHUNK = 128
RET_CHUNK = 256
SAMPLE_BLK = 32
SSD_SAMPLE_BLK = 8


def _cparams(n_axes):
    return pltpu.CompilerParams(dimension_semantics=("arbitrary",) * n_axes,
                                vmem_limit_bytes=VMEM_LIMIT_BYTES)


def _rms_rows(x):
    return x * lax.rsqrt(jnp.mean(x * x, axis=-1, keepdims=True) + EPS)


def _shift_rows(x, d, row):
    return jnp.where(row >= d, pltpu.roll(x, d, axis=0), 0.0)


def _nt_dot(a, b):
    return lax.dot_general(a, b, (((1,), (1,)), ((), ())), preferred_element_type=F32)


def _tn_dot(a, b):
    return lax.dot_general(a, b, (((0,), (0,)), ((), ())), preferred_element_type=F32)


def _project(h_ref, w_refs):
    w = jnp.concatenate([w_ref[0].astype(BF16) for w_ref in w_refs], axis=0)
    return _nt_dot(h_ref[...], w)


def _w_in_row_specs(layer, offsets, index_map_for):
    return [pl.BlockSpec((pl.Element(1), pl.Element(LANES), pl.Element(D_MODEL)), index_map_for(layer, off))
            for off in offsets]


def _rows_to_cols(x):
    n = x.shape[0]
    if n < LANES:
        x = jnp.concatenate([x, jnp.zeros((LANES - n, x.shape[1]), x.dtype)], axis=0)
    return x.T


def _norm_kernel(x_ref, g_ref, o_ref):
    o_ref[...] = (_rms_rows(x_ref[...]) * g_ref[...]).astype(o_ref.dtype)


def rmsnorm(x, g, out_dtype, tm):
    m, d = x.shape
    return pl.pallas_call(
        _norm_kernel,
        out_shape=jax.ShapeDtypeStruct((m, d), out_dtype),
        grid=(m // tm,),
        in_specs=[pl.BlockSpec((tm, d), lambda i: (i, 0)),
                  pl.BlockSpec((1, d), lambda i: (0, 0))],
        out_specs=pl.BlockSpec((tm, d), lambda i: (i, 0)),
        compiler_params=_cparams(1),
        name="rmsnorm",
    )(x, g.reshape(1, d))


def _mm_nt_cast_kernel(a_ref, wt_ref, o_ref, wb_ref):
    @pl.when(pl.program_id(1) == 0)
    def _():
        wb_ref[...] = wt_ref[0].astype(BF16)

    o_ref[...] = _nt_dot(a_ref[...], wb_ref[...])


def in_proj_prompt(layer, h, w_in_t, row0, n_tiles, tn, tm):
    m = h.shape[0]
    n = n_tiles * tn
    w_spec = pl.BlockSpec((pl.Element(1), pl.Element(tn), pl.Element(D_MODEL)),
                          lambda j, i: (layer, pl.multiple_of(row0 + j * tn, SUBLANES), 0))
    return pl.pallas_call(
        _mm_nt_cast_kernel,
        out_shape=(jax.ShapeDtypeStruct((m, n), F32), jax.ShapeDtypeStruct((n, D_MODEL), BF16)),
        grid=(n_tiles, m // tm),
        in_specs=[pl.BlockSpec((tm, D_MODEL), lambda j, i: (i, 0)), w_spec],
        out_specs=(pl.BlockSpec((tm, tn), lambda j, i: (i, j)),
                   pl.BlockSpec((tn, D_MODEL), lambda j, i: (j, 0))),
        compiler_params=_cparams(2),
        name="in_proj_prompt",
    )(h, w_in_t)


def _mm_nt_castw_kernel(a_ref, wt_ref, o_ref):
    o_ref[...] = _nt_dot(a_ref[...], wt_ref[0].astype(BF16))


def in_proj_sample(layer, h, w_in_t, row0, n_tiles, tn):
    m = h.shape[0]
    w_spec = pl.BlockSpec((pl.Element(1), pl.Element(tn), pl.Element(D_MODEL)),
                          lambda j: (layer, pl.multiple_of(row0 + j * tn, SUBLANES), 0))
    return pl.pallas_call(
        _mm_nt_castw_kernel,
        out_shape=jax.ShapeDtypeStruct((m, n_tiles * tn), F32),
        grid=(n_tiles,),
        in_specs=[pl.BlockSpec((m, D_MODEL), lambda j: (0, 0)), w_spec],
        out_specs=pl.BlockSpec((m, tn), lambda j: (0, j)),
        compiler_params=_cparams(1),
        name="in_proj_sample",
    )(h, w_in_t)


def _gated_sum(h, br_refs, gate_w, branch_w):
    acc = None
    for k, br_ref in enumerate(br_refs):
        gate = jax.nn.sigmoid(jnp.dot(h, gate_w(k), preferred_element_type=F32))
        br = jnp.dot(br_ref[...].astype(BF16), branch_w(k), preferred_element_type=F32)
        acc = gate * br if acc is None else acc + gate * br
    return acc


def _merge_cast_kernel(h_ref, a_ref, b_ref, c_ref, d_ref, wg_ref, wb_ref, o_ref, wbb_ref):
    @pl.when(pl.program_id(1) == 0)
    def _():
        for k in range(N_BRANCH):
            wbb_ref[k] = wb_ref[k].astype(BF16)

    acc = _gated_sum(h_ref[...], (a_ref, b_ref, c_ref, d_ref), lambda k: wg_ref[k], lambda k: wbb_ref[k])
    o_ref[...] = acc.astype(o_ref.dtype)


def gated_merge_prompt(layer, h, branches, w_gate_b, w_branch, tm, tn):
    m = h.shape[0]
    br_specs = [pl.BlockSpec((tm, BRANCH_W), lambda j, i: (i, 0)) for _ in range(N_BRANCH)]
    return pl.pallas_call(
        _merge_cast_kernel,
        out_shape=(jax.ShapeDtypeStruct((m, D_MODEL), BF16),
                   jax.ShapeDtypeStruct((N_BRANCH, BRANCH_W, D_MODEL), BF16)),
        grid=(D_MODEL // tn, m // tm),
        in_specs=[pl.BlockSpec((tm, D_MODEL), lambda j, i: (i, 0))] + br_specs
                 + [pl.BlockSpec((None, N_BRANCH, D_MODEL, tn), lambda j, i: (layer, 0, 0, j)),
                    pl.BlockSpec((None, N_BRANCH, BRANCH_W, tn), lambda j, i: (layer, 0, 0, j))],
        out_specs=(pl.BlockSpec((tm, tn), lambda j, i: (i, j)),
                   pl.BlockSpec((N_BRANCH, BRANCH_W, tn), lambda j, i: (0, 0, j))),
        compiler_params=_cparams(2),
        name="gated_merge_prompt",
    )(h, *branches, w_gate_b, w_branch)


def _merge_kernel(h_ref, a_ref, b_ref, c_ref, d_ref, wgb_ref, wbb_ref, o_ref):
    acc = _gated_sum(h_ref[...], (a_ref, b_ref, c_ref, d_ref), lambda k: wgb_ref[k], lambda k: wbb_ref[k])
    o_ref[...] = acc.astype(o_ref.dtype)


def gated_merge_sample(layer, h, branches, w_gate_b, w_branch_b, tn):
    m = h.shape[0]
    br_specs = [pl.BlockSpec((m, BRANCH_W), lambda j: (0, 0)) for _ in range(N_BRANCH)]
    return pl.pallas_call(
        _merge_kernel,
        out_shape=jax.ShapeDtypeStruct((m, D_MODEL), BF16),
        grid=(D_MODEL // tn,),
        in_specs=[pl.BlockSpec((m, D_MODEL), lambda j: (0, 0))] + br_specs
                 + [pl.BlockSpec((None, N_BRANCH, D_MODEL, tn), lambda j: (layer, 0, 0, j)),
                    pl.BlockSpec((N_BRANCH, BRANCH_W, tn), lambda j: (0, 0, j))],
        out_specs=pl.BlockSpec((m, tn), lambda j: (0, j)),
        compiler_params=_cparams(1),
        name="gated_merge_sample",
    )(h, *branches, w_gate_b, w_branch_b)


def _out_proj_kernel(m_ref, w_ref, x_ref, g_ref, xo_ref, ho_ref):
    x_new = x_ref[...] + jnp.dot(m_ref[...], w_ref[...], preferred_element_type=F32)
    xo_ref[...] = x_new
    ho_ref[...] = (_rms_rows(x_new) * g_ref[...]).astype(ho_ref.dtype)


def out_proj_residual_norm(layer, merged, w_out, x, g, tm):
    m = x.shape[0]
    return pl.pallas_call(
        _out_proj_kernel,
        out_shape=(jax.ShapeDtypeStruct((m, D_MODEL), F32), jax.ShapeDtypeStruct((m, D_MODEL), BF16)),
        grid=(m // tm,),
        in_specs=[pl.BlockSpec((tm, D_MODEL), lambda i: (i, 0)),
                  pl.BlockSpec((None, D_MODEL, D_MODEL), lambda i: (layer, 0, 0)),
                  pl.BlockSpec((tm, D_MODEL), lambda i: (i, 0)),
                  pl.BlockSpec((1, D_MODEL), lambda i: (0, 0))],
        out_specs=(pl.BlockSpec((tm, D_MODEL), lambda i: (i, 0)),
                   pl.BlockSpec((tm, D_MODEL), lambda i: (i, 0))),
        compiler_params=_cparams(1),
        name="out_proj",
    )(merged, w_out, x, g.reshape(1, D_MODEL))


def _down_proj_kernel(emit_x, a_ref, w_ref, x_ref, g_ref, *refs):
    if emit_x:
        xo_ref, no_ref, acc_ref = refs
    else:
        no_ref, acc_ref = refs
    kk = pl.program_id(1)

    @pl.when(kk == 0)
    def _():
        acc_ref[...] = x_ref[...]

    acc_ref[...] += jnp.dot(a_ref[...], w_ref[...], preferred_element_type=F32)

    @pl.when(kk == pl.num_programs(1) - 1)
    def _():
        x_new = acc_ref[...]
        if emit_x:
            xo_ref[...] = x_new
        no_ref[...] = (_rms_rows(x_new) * g_ref[...]).astype(no_ref.dtype)


def down_proj_residual_norm(layer, a, w_down, x, g, tm, tk, emit_x, norm_dtype):
    m = x.shape[0]
    out_shape = [jax.ShapeDtypeStruct((m, D_MODEL), norm_dtype)]
    out_specs = [pl.BlockSpec((tm, D_MODEL), lambda i, k: (i, 0))]
    if emit_x:
        out_shape = [jax.ShapeDtypeStruct((m, D_MODEL), F32)] + out_shape
        out_specs = [pl.BlockSpec((tm, D_MODEL), lambda i, k: (i, 0))] + out_specs
    return pl.pallas_call(
        functools.partial(_down_proj_kernel, emit_x),
        out_shape=tuple(out_shape),
        grid=(m // tm, D_FF // tk),
        in_specs=[pl.BlockSpec((tm, tk), lambda i, k: (i, k)),
                  pl.BlockSpec((None, tk, D_MODEL), lambda i, k: (layer, k, 0)),
                  pl.BlockSpec((tm, D_MODEL), lambda i, k: (i, 0)),
                  pl.BlockSpec((1, D_MODEL), lambda i, k: (0, 0))],
        out_specs=tuple(out_specs),
        scratch_shapes=[pltpu.VMEM((tm, D_MODEL), F32)],
        compiler_params=_cparams(2),
        name="down_proj",
    )(a, w_down, x, g.reshape(1, D_MODEL))


def _ffn_prompt_kernel(h_ref, wu_ref, wv_ref, cw_ref, cb_ref, a_ref, st_ref, wub_ref, wvb_ref):
    @pl.when(pl.program_id(1) == 0)
    def _():
        wub_ref[...] = wu_ref[...].astype(BF16)
        wvb_ref[...] = wv_ref[...].astype(BF16)

    rc = FFN_ROW_CHUNK
    tn = a_ref.shape[1]
    cw = cw_ref[...]
    cbias = cb_ref[...]
    row8 = lax.broadcasted_iota(jnp.int32, (SUBLANES, tn), 0)
    tail = jnp.zeros((SUBLANES, tn), F32)
    pending = None
    for c in range(h_ref.shape[0] // rc):
        rows = slice(c * rc, (c + 1) * rc)
        hc = h_ref[rows, :]
        u = jnp.dot(hc, wub_ref[...], preferred_element_type=F32)
        if pending is not None:
            prev_rows, g_prev, v_prev = pending
            a_ref[prev_rows, :] = (g_prev * v_prev).astype(a_ref.dtype)
        v = jnp.dot(hc, wvb_ref[...], preferred_element_type=F32)
        uc = cbias + cw[FFN_CONV_W - 1:FFN_CONV_W] * u
        for d in range(1, FFN_CONV_W):
            rolled = pltpu.roll(u, d, axis=0)
            top = jnp.where(row8 >= d, rolled[:SUBLANES], pltpu.roll(tail, d, axis=0))
            uc = uc + cw[FFN_CONV_W - 1 - d:FFN_CONV_W - d] * jnp.concatenate([top, rolled[SUBLANES:]], axis=0)
        pending = (rows, jax.nn.gelu(uc), v)
        tail = u[rc - SUBLANES:, :]
    prev_rows, g_prev, v_prev = pending
    a_ref[prev_rows, :] = (g_prev * v_prev).astype(a_ref.dtype)
    st_ref[...] = tail[SUBLANES - (FFN_CONV_W - 1):, :]


def ffn_prompt(layer, h2, w_up, w_val, conv_w, conv_b, tn):
    wspec = pl.BlockSpec((None, D_MODEL, tn), lambda j, b: (layer, 0, j))
    wbspec = pl.BlockSpec((D_MODEL, tn), lambda j, b: (0, j))
    return pl.pallas_call(
        _ffn_prompt_kernel,
        out_shape=(jax.ShapeDtypeStruct((BATCH * SEQ, D_FF), BF16),
                   jax.ShapeDtypeStruct((BATCH, FFN_CONV_W - 1, D_FF), F32),
                   jax.ShapeDtypeStruct((D_MODEL, D_FF), BF16),
                   jax.ShapeDtypeStruct((D_MODEL, D_FF), BF16)),
        grid=(D_FF // tn, BATCH),
        in_specs=[pl.BlockSpec((SEQ, D_MODEL), lambda j, b: (b, 0)), wspec, wspec,
                  pl.BlockSpec((FFN_CONV_W, tn), lambda j, b: (0, j)),
                  pl.BlockSpec((1, tn), lambda j, b: (0, j))],
        out_specs=(pl.BlockSpec((SEQ, tn), lambda j, b: (b, j)),
                   pl.BlockSpec((None, FFN_CONV_W - 1, tn), lambda j, b: (b, 0, j)),
                   wbspec, wbspec),
        compiler_params=_cparams(2),
        name="ffn_prompt",
    )(h2, w_up, w_val, conv_w, conv_b.reshape(1, D_FF))


def _ffn_sample_kernel(h_ref, wu_ref, wv_ref, cw_ref, cb_ref, buf_ref, a_ref, nb_ref):
    h = h_ref[...]
    u = jnp.dot(h, wu_ref[...], preferred_element_type=F32)
    v = jnp.dot(h, wv_ref[...], preferred_element_type=F32)
    cw = cw_ref[...]
    b0 = buf_ref[:, 0, :]
    b1 = buf_ref[:, 1, :]
    uc = cb_ref[...] + cw[0:1] * b0 + cw[1:2] * b1 + cw[2:3] * u
    a_ref[...] = (jax.nn.gelu(uc) * v).astype(a_ref.dtype)
    nb_ref[:, 0, :] = b1
    nb_ref[:, 1, :] = u


def ffn_sample(layer, h2, w_up, w_val, conv_w, conv_b, buf, tn):
    return pl.pallas_call(
        _ffn_sample_kernel,
        out_shape=(jax.ShapeDtypeStruct((DEC_BATCH, D_FF), BF16),
                   jax.ShapeDtypeStruct((DEC_BATCH, FFN_CONV_W - 1, D_FF), F32)),
        grid=(D_FF // tn,),
        in_specs=[pl.BlockSpec((DEC_BATCH, D_MODEL), lambda j: (0, 0)),
                  pl.BlockSpec((D_MODEL, tn), lambda j: (0, j)),
                  pl.BlockSpec((D_MODEL, tn), lambda j: (0, j)),
                  pl.BlockSpec((FFN_CONV_W, tn), lambda j: (0, j)),
                  pl.BlockSpec((1, tn), lambda j: (0, j)),
                  pl.BlockSpec((None, DEC_BATCH, FFN_CONV_W - 1, tn), lambda j: (layer, 0, 0, j))],
        out_specs=(pl.BlockSpec((DEC_BATCH, tn), lambda j: (0, j)),
                   pl.BlockSpec((DEC_BATCH, FFN_CONV_W - 1, tn), lambda j: (0, 0, j))),
        compiler_params=_cparams(1),
        name="ffn_sample",
    )(h2, w_up, w_val, conv_w, conv_b.reshape(1, D_FF), buf)


def _rope_table_kernel(start, consecutive, freq_ref, sign_ref, cos_ref, sin_ref):
    shape = cos_ref.shape
    if consecutive:
        pos = lax.broadcasted_iota(jnp.int32, shape, 0).astype(F32) + float(start)
    else:
        pos = jnp.full(shape, float(start), F32)
    ang = pos * freq_ref[...]
    cos_ref[...] = jnp.cos(ang)
    sin_ref[...] = sign_ref[...] * jnp.sin(ang)


def rope_tables(n_rows, start, consecutive):
    half = RET_DK // 2
    freqs = ROPE_BASE ** (-jnp.arange(half, dtype=F32) / half)
    freq2 = jnp.concatenate([freqs, freqs]).reshape(1, RET_DK)
    sign = jnp.concatenate([-jnp.ones((half,), F32), jnp.ones((half,), F32)]).reshape(1, RET_DK)
    return pl.pallas_call(
        functools.partial(_rope_table_kernel, start, consecutive),
        out_shape=(jax.ShapeDtypeStruct((n_rows, RET_DK), F32), jax.ShapeDtypeStruct((n_rows, RET_DK), F32)),
        name="rope_tables",
    )(freq2, sign)


def _rope(x, cos, sin_signed):
    return x * cos + pltpu.roll(x, RET_DK // 2, axis=1) * sin_signed


def _lru_gates(conv, wa_ref, ba_ref, wx_ref, bx_ref, lam_ref):
    xb = conv.astype(BF16)
    r = jax.nn.sigmoid(jnp.dot(xb, wa_ref[...].astype(BF16), preferred_element_type=F32) + ba_ref[...])
    i = jax.nn.sigmoid(jnp.dot(xb, wx_ref[...].astype(BF16), preferred_element_type=F32) + bx_ref[...])
    log_a = -LRU_C * r * jax.nn.softplus(-lam_ref[...])
    a = jnp.exp(log_a)
    u = jnp.sqrt(1.0 - a * a) * (i * conv)
    return a, u


def _lru_prompt_kernel(hin_ref, wxa_ref, wya_ref, cw_ref, cb_ref, wa_ref, ba_ref, wx_ref, bx_ref, lam_ref,
                       out_ref, h_ref, conv_ref, ag_s, ug_s):
    proj = _project(hin_ref, (wxa_ref, wya_ref))
    x = proj[:, :LRU_BLOCK]
    ya = proj[:, LRU_BLOCK:]
    t_len = x.shape[0]
    row = lax.broadcasted_iota(jnp.int32, x.shape, 0)
    cw = cw_ref[...]
    conv = cb_ref[...] + cw[CONV_W - 1:CONV_W] * x
    for d in range(1, CONV_W):
        conv = conv + cw[CONV_W - 1 - d:CONV_W - d] * _shift_rows(x, d, row)
    a, u = _lru_gates(conv, wa_ref, ba_ref, wx_ref, bx_ref, lam_ref)
    ng = t_len // SUBLANES
    a3 = a.reshape(ng, SUBLANES, LRU_BLOCK)
    u3 = u.reshape(ng, SUBLANES, LRU_BLOCK)
    sub = lax.broadcasted_iota(jnp.int32, a3.shape, 1)
    d = 1
    while d < SUBLANES:
        keep = sub >= d
        u3 = jnp.where(keep, a3 * pltpu.roll(u3, d, axis=1) + u3, u3)
        a3 = jnp.where(keep, a3 * pltpu.roll(a3, d, axis=1), a3)
        d *= 2
    ag_s[...] = a3.reshape(t_len, LRU_BLOCK)
    ug_s[...] = u3.reshape(t_len, LRU_BLOCK)
    ag = ag_s[pl.ds(SUBLANES - 1, ng, stride=SUBLANES), :]
    ug = ug_s[pl.ds(SUBLANES - 1, ng, stride=SUBLANES), :]
    grow = lax.broadcasted_iota(jnp.int32, ag.shape, 0)
    d = 1
    while d < ng:
        keep = grow >= d
        ug = jnp.where(keep, ag * pltpu.roll(ug, d, axis=0) + ug, ug)
        ag = jnp.where(keep, ag * pltpu.roll(ag, d, axis=0), ag)
        d *= 2
    carry = _shift_rows(ug, 1, grow)
    h3 = a3 * jnp.broadcast_to(carry[:, None, :], a3.shape) + u3
    hs = h3.reshape(t_len, LRU_BLOCK)
    out_ref[...] = (hs * jax.nn.gelu(ya)).astype(out_ref.dtype)
    h_ref[...] = ug[ng - 1:, :]
    conv_ref[...] = x[t_len - (CONV_W - 1):, :]


def _lru_param_specs(n_axes_fn):
    blk3 = lambda shape: pl.BlockSpec(shape, n_axes_fn(lambda n: (n, 0, 0)))
    return [pl.BlockSpec((CONV_W, LRU_BLOCK), n_axes_fn(lambda n: (0, n))),
            pl.BlockSpec((1, LRU_BLOCK), n_axes_fn(lambda n: (0, n))),
            blk3((None, LRU_BLOCK, LRU_BLOCK)), blk3((None, 1, LRU_BLOCK)),
            blk3((None, LRU_BLOCK, LRU_BLOCK)), blk3((None, 1, LRU_BLOCK)),
            blk3((None, 1, LRU_BLOCK))]


def _lru_params(p):
    return (p["lru_conv_w"], p["lru_conv_b"].reshape(1, BRANCH_W),
            p["lru_wa"], p["lru_ba"].reshape(LRU_BLOCKS, 1, LRU_BLOCK),
            p["lru_wx"], p["lru_bx"].reshape(LRU_BLOCKS, 1, LRU_BLOCK),
            p["lru_lambda"].reshape(LRU_BLOCKS, 1, LRU_BLOCK))


def _prompt_unit_rows(layer, off):
    return lambda b, u: (layer, pl.multiple_of(off + u * LANES, SUBLANES), 0)


def lru_prompt(layer, hin, w_in_t, p):
    wrap = lambda f: (lambda b, n: f(n))
    out, h, conv = pl.pallas_call(
        _lru_prompt_kernel,
        out_shape=(jax.ShapeDtypeStruct((BATCH * SEQ, BRANCH_W), BF16),
                   jax.ShapeDtypeStruct((BATCH, 1, BRANCH_W), F32),
                   jax.ShapeDtypeStruct((BATCH, CONV_W - 1, BRANCH_W), F32)),
        grid=(BATCH, LRU_BLOCKS),
        in_specs=[pl.BlockSpec((SEQ, D_MODEL), lambda b, n: (b, 0))]
                 + _w_in_row_specs(layer, (OFF_XA, OFF_YA), _prompt_unit_rows) + _lru_param_specs(wrap),
        out_specs=(pl.BlockSpec((SEQ, LRU_BLOCK), lambda b, n: (b, n)),
                   pl.BlockSpec((None, 1, LRU_BLOCK), lambda b, n: (b, 0, n)),
                   pl.BlockSpec((None, CONV_W - 1, LRU_BLOCK), lambda b, n: (b, 0, n))),
        scratch_shapes=[pltpu.VMEM((SEQ, LRU_BLOCK), F32)] * 2,
        compiler_params=_cparams(2),
        name="lru_prompt",
    )(hin, w_in_t, w_in_t, *_lru_params(p))
    return out, h.reshape(BATCH, BRANCH_W), conv


def _lru_sample_kernel(xa_ref, ya_ref, cw_ref, cb_ref, wa_ref, ba_ref, wx_ref, bx_ref, lam_ref,
                       h0_ref, buf_ref, out_ref, h_ref, nbuf_ref):
    x = xa_ref[...]
    cw = cw_ref[...]
    conv = cb_ref[...] + cw[CONV_W - 1:CONV_W] * x
    for j in range(CONV_W - 1):
        conv = conv + cw[j:j + 1] * buf_ref[j]
    a, u = _lru_gates(conv, wa_ref, ba_ref, wx_ref, bx_ref, lam_ref)
    h = a * h0_ref[...] + u
    out_ref[...] = h * jax.nn.gelu(ya_ref[...])
    h_ref[...] = h
    for j in range(CONV_W - 2):
        nbuf_ref[j] = buf_ref[j + 1]
    nbuf_ref[CONV_W - 2] = x


def lru_sample(proj, p, h0, buf_t):
    cb = lambda off: off // LRU_BLOCK
    wrap = lambda f: f
    return pl.pallas_call(
        _lru_sample_kernel,
        out_shape=(jax.ShapeDtypeStruct((DEC_BATCH, BRANCH_W), F32),
                   jax.ShapeDtypeStruct((DEC_BATCH, BRANCH_W), F32),
                   jax.ShapeDtypeStruct((CONV_W - 1, DEC_BATCH, BRANCH_W), F32)),
        grid=(LRU_BLOCKS,),
        in_specs=[pl.BlockSpec((DEC_BATCH, LRU_BLOCK), lambda n: (0, cb(OFF_XA) + n)),
                  pl.BlockSpec((DEC_BATCH, LRU_BLOCK), lambda n: (0, cb(OFF_YA) + n))]
                 + _lru_param_specs(wrap)
                 + [pl.BlockSpec((DEC_BATCH, LRU_BLOCK), lambda n: (0, n)),
                    pl.BlockSpec((CONV_W - 1, DEC_BATCH, LRU_BLOCK), lambda n: (0, 0, n))],
        out_specs=(pl.BlockSpec((DEC_BATCH, LRU_BLOCK), lambda n: (0, n)),
                   pl.BlockSpec((DEC_BATCH, LRU_BLOCK), lambda n: (0, n)),
                   pl.BlockSpec((CONV_W - 1, DEC_BATCH, LRU_BLOCK), lambda n: (0, 0, n))),
        compiler_params=_cparams(1),
        name="lru_sample",
    )(proj, proj, *_lru_params(p), h0, buf_t)


def _hgrn_lower_bound(layer, logits):
    mx = jnp.max(logits, axis=0, keepdims=True)
    e = jnp.exp(logits - mx)
    ls = e / jnp.sum(e, axis=0, keepdims=True)
    lb = jnp.zeros_like(ls[0:1])
    for i in range(1, layer + 1):
        lb = lb + ls[i:i + 1]
    return lb


def _hgrn_gates(layer, hq, hf, lbl_ref):
    lb = _hgrn_lower_bound(layer, lbl_ref[...])
    q = jax.nn.silu(hq)
    sg = jax.nn.sigmoid(hf)
    f = lb + (1.0 - lb) * sg
    k = (1.0 - lb) * (1.0 - sg)
    return q, f, k


def _hgrn_level_ids():
    c = HG_CHUNK
    t = np.arange(c)[:, None]
    s = np.arange(c)[None, :]
    level = np.zeros((c, c), np.int32)
    for li, m in enumerate(HG_LEVELS):
        same = (t // (2 * m)) == (s // (2 * m))
        level[same & ((t % (2 * m)) >= m) & ((s % (2 * m)) < m)] = li + 1
    level[t == s] = len(HG_LEVELS) + 1
    return level


def _split3_bf16(x):
    hi = x.astype(BF16)
    r1 = x - hi.astype(F32)
    mid = r1.astype(BF16)
    lo = (r1 - mid.astype(F32)).astype(BF16)
    return hi, mid, lo


def _cumsum_rows(tril_b, x):
    return sum(jnp.dot(tril_b, piece, preferred_element_type=F32) for piece in _split3_bf16(x))


def _hgrn_midpoint_factor(b, m):
    c = b.shape[0]
    if 2 * m >= SUBLANES:
        b3 = b.reshape(c // (2 * m), 2 * m, LANES)
        mid = b3[:, m - 1:m, :]
    else:
        b3 = b.reshape(c // SUBLANES, SUBLANES, LANES)
        sub = lax.broadcasted_iota(jnp.int32, b3.shape, 1)
        mid = b3[:, m - 1:m, :]
        for blk in range(1, SUBLANES // (2 * m)):
            lo = blk * 2 * m
            mid = jnp.where(sub >= lo, b3[:, lo + m - 1:lo + m, :], mid)
    return jnp.exp(-jnp.abs(b3 - mid)).reshape(c, LANES)


def _hgrn_prompt_kernel(layer, hin_ref, wq_ref, wf_ref, wi_ref, wg_ref, lbl_ref, nw_ref, lvl_ref,
                        out_ref, st_ref, lf_s, q_s, k_s, v_s, o_s):
    c = HG_CHUNK
    t_len = hin_ref.shape[0]
    proj = _project(hin_ref, (wq_ref, wf_ref, wi_ref, wg_ref))
    q, f, k = _hgrn_gates(layer, proj[:, 0:HG_DK], proj[:, HG_DK:2 * HG_DK], lbl_ref)
    lf_s[...] = jnp.log(f)
    q_s[...] = q
    k_s[...] = k
    v_s[...] = proj[:, 2 * HG_DK:3 * HG_DK].astype(BF16)
    gate = proj[:, 3 * HG_DK:]

    lvl = lvl_ref[...]
    tril_b = (lax.broadcasted_iota(jnp.int32, (c, c), 0) >= lax.broadcasted_iota(jnp.int32, (c, c), 1)).astype(BF16)

    st = jnp.zeros((HG_DK, HG_DK), F32)
    for ci in range(t_len // c):
        sl = slice(ci * c, (ci + 1) * c)
        qc = q_s[sl, :]
        kc = k_s[sl, :]
        vb = v_s[sl, :]
        b = _cumsum_rows(tril_b, lf_s[sl, :])
        att = jnp.where(lvl == len(HG_LEVELS) + 1, _nt_dot(qc.astype(BF16), kc.astype(BF16)), 0.0)
        for li, m in enumerate(HG_LEVELS):
            e = _hgrn_midpoint_factor(b, m)
            a_l = _nt_dot((qc * e).astype(BF16), (kc * e).astype(BF16))
            att = jnp.where(lvl == li + 1, a_l, att)
        o = (jnp.dot(att.astype(BF16), vb, preferred_element_type=F32)
             + _nt_dot((qc * jnp.exp(b)).astype(BF16), st.astype(BF16)))
        o_s[sl, :] = o
        bl = b[c - 1:c, :]
        kdec = (kc * jnp.exp(bl - b)).astype(BF16)
        st = st * jnp.exp(bl) + _tn_dot(vb, kdec)
    o = o_s[...]
    out_ref[...] = (_rms_rows(o) * nw_ref[...] * jax.nn.silu(gate)).astype(out_ref.dtype)
    st_ref[...] = st.T


def hgrn_prompt(layer, hin, w_in_t, p):
    level = _hgrn_level_ids()
    return pl.pallas_call(
        functools.partial(_hgrn_prompt_kernel, layer),
        out_shape=(jax.ShapeDtypeStruct((BATCH * SEQ, BRANCH_W), BF16),
                   jax.ShapeDtypeStruct((BATCH, HG_HEADS, HG_DK, HG_DK), F32)),
        grid=(BATCH, HG_HEADS),
        in_specs=[pl.BlockSpec((SEQ, D_MODEL), lambda b, h: (b, 0))]
                 + _w_in_row_specs(layer, (OFF_HQ, OFF_HF, OFF_HI, OFF_HG), _prompt_unit_rows)
                 + [pl.BlockSpec((DEPTH, HG_DK), lambda b, h: (0, h)),
                    pl.BlockSpec((1, HG_DK), lambda b, h: (0, 0)),
                    pl.BlockSpec(level.shape, lambda b, h: (0, 0))],
        out_specs=(pl.BlockSpec((SEQ, HG_DK), lambda b, h: (b, h)),
                   pl.BlockSpec((None, None, HG_DK, HG_DK), lambda b, h: (b, h, 0, 0))),
        scratch_shapes=[pltpu.VMEM((SEQ, HG_DK), F32)] * 3
                       + [pltpu.VMEM((SEQ, HG_DK), BF16), pltpu.VMEM((SEQ, HG_DK), F32)],
        compiler_params=_cparams(2),
        name="hgrn_prompt",
    )(hin, w_in_t, w_in_t, w_in_t, w_in_t, p["hg_lb_logits"], p["hg_norm_w"].reshape(1, HG_DK),
      jnp.asarray(level))


def _sample_state_step(s_ref, so_ref, o_s, decay_of, q, k, v):
    nb = k.shape[0]
    kt = _rows_to_cols(k)[:, :nb].astype(BF16)
    qb = q.astype(BF16)
    rowid = lax.broadcasted_iota(jnp.int32, v.shape, 0)
    for j in range(nb):
        v_j = jnp.where(rowid == j, v, 0.0).astype(BF16)
        s_new = decay_of(j) * s_ref[j] + jnp.dot(kt, v_j, preferred_element_type=F32)
        so_ref[j] = s_new
        o_s[j:j + 1, :] = jnp.dot(qb, s_new.astype(BF16), preferred_element_type=F32)[j:j + 1, :]


def _hgrn_sample_kernel(layer, q_ref, f_ref, i_ref, g_ref, lbl_ref, nw_ref, s_ref, out_ref, so_ref, o_s):
    q, f, k = _hgrn_gates(layer, q_ref[...], f_ref[...], lbl_ref)
    fc = _rows_to_cols(f)
    _sample_state_step(s_ref, so_ref, o_s, lambda j: fc[:, j:j + 1], q, k, i_ref[...])
    out_ref[...] = _rms_rows(o_s[...]) * nw_ref[...] * jax.nn.silu(g_ref[...])


def hgrn_sample(layer, proj, p, state, state_out):
    cb = lambda off: off // HG_DK
    nb = SAMPLE_BLK
    col = lambda off: pl.BlockSpec((nb, HG_DK), lambda h, i: (i, cb(off) + h))
    st_spec = pl.BlockSpec((None, nb, None, HG_DK, HG_DK), lambda h, i: (layer, i, h, 0, 0))
    args = [proj, proj, proj, proj, p["hg_lb_logits"], p["hg_norm_w"].reshape(1, HG_DK), state]
    in_specs = [col(OFF_HQ), col(OFF_HF), col(OFF_HI), col(OFF_HG),
                pl.BlockSpec((DEPTH, HG_DK), lambda h, i: (0, h)),
                pl.BlockSpec((1, HG_DK), lambda h, i: (0, 0)),
                st_spec]
    aliases = {}
    kern = functools.partial(_hgrn_sample_kernel, layer)
    if state_out is not None:
        args.append(state_out)
        in_specs.append(pl.BlockSpec(memory_space=pl.ANY))
        aliases = {len(args) - 1: 1}
        kern = functools.partial(_drop_alias_arg, kern, 7)
    return pl.pallas_call(
        kern,
        out_shape=(jax.ShapeDtypeStruct((DEC_BATCH, BRANCH_W), F32),
                   jax.ShapeDtypeStruct(state.shape, F32)),
        grid=(HG_HEADS, DEC_BATCH // nb),
        in_specs=in_specs,
        out_specs=(pl.BlockSpec((nb, HG_DK), lambda h, i: (i, h)), st_spec),
        scratch_shapes=[pltpu.VMEM((nb, HG_DK), F32)],
        input_output_aliases=aliases,
        compiler_params=_cparams(2),
        name="hgrn_sample",
    )(*args)


def _drop_alias_arg(kern, pos, *refs):
    return kern(*refs[:pos], *refs[pos + 1:])


def _head_pair(cols, h0, lane_lo):
    return jnp.where(lane_lo, cols[:, h0:h0 + 1], cols[:, h0 + 1:h0 + 2])


def _ssd_prompt_kernel(z_ref, x_ref, bc_ref, dt_ref, cw_ref, cb_ref, dtb_ref, alog_ref, dpar_ref, nw_ref,
                       out_ref, st_ref, cst_ref, cx_s, cbc_s, s_s):
    c = SSD_CHUNK
    ci = pl.program_id(1)

    @pl.when(ci == 0)
    def _():
        cx_s[...] = jnp.zeros_like(cx_s)
        cbc_s[...] = jnp.zeros_like(cbc_s)
        s_s[...] = jnp.zeros_like(s_s)

    cw = cw_ref[...]
    cbias = cb_ref[...]

    def conv_silu(raw, carry_ref, lo, hi):
        xx = jnp.concatenate([carry_ref[...], raw], axis=0)
        y = cbias[:, lo:hi] + cw[CONV_W - 1:CONV_W, lo:hi] * raw
        for d in range(1, CONV_W):
            y = y + cw[CONV_W - 1 - d:CONV_W - d, lo:hi] * pltpu.roll(xx, d, axis=0)[SUBLANES:]
        carry_ref[...] = raw[c - SUBLANES:, :]
        return jax.nn.silu(y)

    x_raw = x_ref[...]
    bc_raw = bc_ref[...]
    xs = conv_silu(x_raw, cx_s, 0, BRANCH_W)
    bc = conv_silu(bc_raw, cbc_s, BRANCH_W, SSD_CONV_DIM)

    dt = jax.nn.softplus(dt_ref[...] + dtb_ref[...])
    a_neg = -jnp.exp(alog_ref[...])
    logd = dt * a_neg
    tri = lax.broadcasted_iota(jnp.int32, (c, c), 0) >= lax.broadcasted_iota(jnp.int32, (c, c), 1)
    b = _cumsum_rows(tri.astype(BF16), logd)
    b_t = b.T
    bl = b[c - 1:c, :]
    e_in = jnp.exp(b)
    w_out = jnp.exp(bl - b)
    e_last = jnp.exp(bl)
    dfull = dpar_ref[...]

    lane_lo =lax.broadcasted_iota(jnp.int32, (c, LANES), 1) < SSD_HEADDIM
    lane_lo_row = lax.broadcasted_iota(jnp.int32, (1, LANES), 1) < SSD_HEADDIM

    ys = []
    for g in range(SSD_GROUPS):
        bm = bc[:, g * SSD_STATE:(g + 1) * SSD_STATE].astype(BF16)
        cm = bc[:, (SSD_GROUPS + g) * SSD_STATE:(SSD_GROUPS + g + 1) * SSD_STATE].astype(BF16)
        gmat = _nt_dot(cm, bm)
        for pp in range(SSD_HEADS // SSD_GROUPS // 2):
            pi = g * (SSD_HEADS // SSD_GROUPS // 2) + pp
            h0 = 2 * pi
            xs_p = xs[:, pi * LANES:(pi + 1) * LANES]
            vdt = xs_p * _head_pair(dt, h0, lane_lo)
            vdt_b = vdt.astype(BF16)
            o_heads = []
            for hh in (h0, h0 + 1):
                diff = b[:, hh:hh + 1] - b_t[hh:hh + 1, :]
                dec = jnp.where(tri, jnp.exp(jnp.where(tri, diff, 0.0)), 0.0)
                o_heads.append(jnp.dot((gmat * dec).astype(BF16), vdt_b, preferred_element_type=F32))
            o_intra = jnp.where(lane_lo, o_heads[0], o_heads[1])
            s_p = s_s[pi]
            o_inter = _head_pair(e_in, h0, lane_lo) * jnp.dot(cm, s_p.astype(BF16), preferred_element_type=F32)
            ys.append(o_intra + o_inter + dfull[:, pi * LANES:(pi + 1) * LANES] * xs_p)
            upd = _tn_dot(bm, (vdt * _head_pair(w_out, h0, lane_lo)).astype(BF16))
            s_s[pi] = s_p * _head_pair(e_last, h0, lane_lo_row) + upd

    y = jnp.concatenate(ys, axis=1) * jax.nn.silu(z_ref[...])
    gw = BRANCH_W // SSD_GROUPS
    nw = nw_ref[...]
    outs = [_rms_rows(y[:, g * gw:(g + 1) * gw]) * nw[:, g * gw:(g + 1) * gw] for g in range(SSD_GROUPS)]
    out_ref[...] = jnp.concatenate(outs, axis=1).astype(out_ref.dtype)

    @pl.when(ci == pl.num_programs(1) - 1)
    def _():
        for pi in range(SSD_HEADS // 2):
            s_t = s_s[pi].T
            st_ref[2 * pi] = s_t[:SSD_HEADDIM, :]
            st_ref[2 * pi + 1] = s_t[SSD_HEADDIM:, :]
        cst_ref[:, 0:BRANCH_W] = x_raw[c - (CONV_W - 1):, :]
        cst_ref[:, BRANCH_W:SSD_CONV_DIM] = bc_raw[c - (CONV_W - 1):, :]


def _pad_lanes(v):
    return jnp.pad(v.astype(F32), (0, LANES - v.shape[0])).reshape(1, LANES)


def _ssd_params(p):
    return (p["ssd_conv_w"], p["ssd_conv_b"].reshape(1, SSD_CONV_DIM), _pad_lanes(p["ssd_dt_bias"]),
            _pad_lanes(p["ssd_a_log"]), jnp.repeat(p["ssd_d"].astype(F32), SSD_HEADDIM).reshape(1, BRANCH_W),
            p["ssd_norm_w"].reshape(1, BRANCH_W))


def ssd_prompt(proj, p):
    c = SSD_CHUNK
    nc = SEQ // c
    const = lambda shape: pl.BlockSpec(shape, lambda b, i: (0, 0))
    rowblk = lambda w, off: pl.BlockSpec((c, w), lambda b, i: (b * nc + i, (off - OFF_SZ) // w))
    return pl.pallas_call(
        _ssd_prompt_kernel,
        out_shape=(jax.ShapeDtypeStruct((BATCH * SEQ, BRANCH_W), BF16),
                   jax.ShapeDtypeStruct((BATCH, SSD_HEADS, SSD_HEADDIM, SSD_STATE), F32),
                   jax.ShapeDtypeStruct((BATCH, CONV_W - 1, SSD_CONV_DIM), F32)),
        grid=(BATCH, nc),
        in_specs=[rowblk(BRANCH_W, OFF_SZ), rowblk(BRANCH_W, OFF_SX), rowblk(SSD_BC, OFF_SBC),
                  rowblk(LANES, OFF_SDT),
                  const((CONV_W, SSD_CONV_DIM)), const((1, SSD_CONV_DIM)), const((1, LANES)),
                  const((1, LANES)), const((1, BRANCH_W)), const((1, BRANCH_W))],
        out_specs=(pl.BlockSpec((c, BRANCH_W), lambda b, i: (b * nc + i, 0)),
                   pl.BlockSpec((None, SSD_HEADS, SSD_HEADDIM, SSD_STATE), lambda b, i: (b, 0, 0, 0)),
                   pl.BlockSpec((None, CONV_W - 1, SSD_CONV_DIM), lambda b, i: (b, 0, 0))),
        scratch_shapes=[pltpu.VMEM((SUBLANES, BRANCH_W), F32), pltpu.VMEM((SUBLANES, SSD_BC), F32),
                        pltpu.VMEM((SSD_HEADS // 2, SSD_STATE, LANES), F32)],
        compiler_params=_cparams(2),
        name="ssd_prompt",
    )(proj, proj, proj, proj, *_ssd_params(p))


def _ssd_sample_kernel(z_ref, x_ref, bc_ref, dt_ref, cw_ref, cb_ref, dtb_ref, alog_ref, dpar_ref, nw_ref,
                       bufx_ref, bufbc_ref, s_ref, out_ref, so_ref, nbx_ref, nbbc_ref, y_s):
    cw = cw_ref[...]
    cbias = cb_ref[...]

    def conv_silu(raw, buf_ref, nbuf_ref, lo, hi):
        y = cbias[:, lo:hi] + cw[CONV_W - 1:CONV_W, lo:hi] * raw
        for j in range(CONV_W - 1):
            y = y + cw[j:j + 1, lo:hi] * buf_ref[j]
        for j in range(CONV_W - 2):
            nbuf_ref[j] = buf_ref[j + 1]
        nbuf_ref[CONV_W - 2] = raw
        return jax.nn.silu(y)

    xs = conv_silu(x_ref[...], bufx_ref, nbx_ref, 0, BRANCH_W)
    bc = conv_silu(bc_ref[...], bufbc_ref, nbbc_ref, BRANCH_W, SSD_CONV_DIM)
    dt = jax.nn.softplus(dt_ref[...] + dtb_ref[...])
    decay = jnp.exp(dt * (-jnp.exp(alog_ref[...])))
    nb = xs.shape[0]
    hpg = SSD_HEADS // SSD_GROUPS
    lane_lo = lax.broadcasted_iota(jnp.int32, (nb, LANES), 1) < SSD_HEADDIM
    for pi in range(SSD_HEADS // 2):
        h0 = 2 * pi
        g = h0 // hpg
        xdt_cols = _rows_to_cols(xs[:, pi * LANES:(pi + 1) * LANES] * _head_pair(dt, h0, lane_lo))
        for e in range(2):
            h = h0 + e
            for j in range(nb):
                xcol = xdt_cols[e * SSD_HEADDIM:(e + 1) * SSD_HEADDIM, j:j + 1]
                brow = bc[j:j + 1, g * SSD_STATE:(g + 1) * SSD_STATE]
                so_ref[j, h] = s_ref[j, h] * decay[j:j + 1, h:h + 1] + xcol * brow
    for g in range(SSD_GROUPS):
        cm = bc[:, (SSD_GROUPS + g) * SSD_STATE:(SSD_GROUPS + g + 1) * SSD_STATE].astype(BF16)
        for j in range(nb):
            s_new = so_ref[j, g * hpg:(g + 1) * hpg].reshape(hpg * SSD_HEADDIM, SSD_STATE)
            y_s[j:j + 1, g * hpg * SSD_HEADDIM:(g + 1) * hpg * SSD_HEADDIM] = _nt_dot(cm, s_new.astype(BF16))[j:j + 1, :]
    y = (y_s[...] + dpar_ref[...] * xs) * jax.nn.silu(z_ref[...])
    gw = BRANCH_W // SSD_GROUPS
    nw = nw_ref[...]
    outs = [_rms_rows(y[:, g * gw:(g + 1) * gw]) * nw[:, g * gw:(g + 1) * gw] for g in range(SSD_GROUPS)]
    out_ref[...] = jnp.concatenate(outs, axis=1)


def ssd_sample(layer, proj, p, state, state_out, buf_t):
    nb = SSD_SAMPLE_BLK
    const = lambda shape: pl.BlockSpec(shape, lambda i: (0,) * len(shape))
    rowblk = lambda w, off: pl.BlockSpec((nb, w), lambda i: (i, off // w))
    st_spec = pl.BlockSpec((None, nb, SSD_HEADS, SSD_HEADDIM, SSD_STATE), lambda i: (layer, i, 0, 0, 0))
    bufx_spec = pl.BlockSpec((CONV_W - 1, nb, BRANCH_W), lambda i: (0, i, 0))
    bufbc_spec = pl.BlockSpec((CONV_W - 1, nb, SSD_BC), lambda i: (0, i, BRANCH_W // SSD_BC))
    args = [proj, proj, proj, proj, *_ssd_params(p), buf_t, buf_t, state]
    in_specs = [rowblk(BRANCH_W, OFF_SZ), rowblk(BRANCH_W, OFF_SX), rowblk(SSD_BC, OFF_SBC), rowblk(LANES, OFF_SDT),
                const((CONV_W, SSD_CONV_DIM)), const((1, SSD_CONV_DIM)), const((1, LANES)), const((1, LANES)),
                const((1, BRANCH_W)), const((1, BRANCH_W)), bufx_spec, bufbc_spec, st_spec]
    aliases = {}
    kern = _ssd_sample_kernel
    if state_out is not None:
        args.append(state_out)
        in_specs.append(pl.BlockSpec(memory_space=pl.ANY))
        aliases = {len(args) - 1: 1}
        kern = functools.partial(_drop_alias_arg, kern, 13)
    out, st, nbx, nbbc = pl.pallas_call(
        kern,
        out_shape=(jax.ShapeDtypeStruct((DEC_BATCH, BRANCH_W), F32),
                   jax.ShapeDtypeStruct(state.shape, F32),
                   jax.ShapeDtypeStruct((CONV_W - 1, DEC_BATCH, BRANCH_W), F32),
                   jax.ShapeDtypeStruct((CONV_W - 1, DEC_BATCH, SSD_BC), F32)),
        grid=(DEC_BATCH // nb,),
        in_specs=in_specs,
        out_specs=(pl.BlockSpec((nb, BRANCH_W), lambda i: (i, 0)), st_spec,
                   pl.BlockSpec((CONV_W - 1, nb, BRANCH_W), lambda i: (0, i, 0)),
                   pl.BlockSpec((CONV_W - 1, nb, SSD_BC), lambda i: (0, i, 0))),
        scratch_shapes=[pltpu.VMEM((nb, BRANCH_W), F32)],
        input_output_aliases=aliases,
        compiler_params=_cparams(1),
        name="ssd_sample",
    )(*args)
    return out, st, jnp.concatenate([nbx, nbbc], axis=-1)


def _ret_prompt_kernel(hin_ref, wq_ref, wk_ref, wv_ref, wg_ref, cos_ref, sin_ref, lg_ref, out_ref, st_ref,
                       q_s, k_s, v_s, o_s):
    c = RET_CHUNK
    t_len = hin_ref.shape[0]
    proj = _project(hin_ref, (wq_ref, wk_ref, wv_ref, wg_ref))
    cos = cos_ref[...]
    sin = sin_ref[...]
    q_s[...] = _rope(proj[:, 0:RET_DK], cos, sin)
    k_s[...] = _rope(proj[:, RET_DK:2 * RET_DK], cos, sin) * RET_DK ** -0.5
    v_s[...] = proj[:, 2 * RET_DK:3 * RET_DK].astype(BF16)
    gate = proj[:, 3 * RET_DK:]
    lg = lg_ref[...]
    lg128 = lg[:, :LANES]
    ti = lax.broadcasted_iota(jnp.int32, (c, c), 0)
    si = lax.broadcasted_iota(jnp.int32, (c, c), 1)
    tri = ti >= si
    dec = jnp.where(tri, jnp.exp(jnp.where(tri, (ti - si).astype(F32) * lg, 0.0)), 0.0)
    tt = lax.broadcasted_iota(jnp.int32, (c, LANES), 0).astype(F32)
    g_in = jnp.exp((tt + 1.0) * lg128)
    g_out = jnp.exp((c - 1.0 - tt) * lg128)
    g_all = jnp.exp(float(c) * lg128)
    s = jnp.zeros((RET_DK, RET_DK), F32)
    for ci in range(t_len // c):
        sl = slice(ci * c, (ci + 1) * c)
        qc = q_s[sl, :]
        kc = k_s[sl, :]
        vb = v_s[sl, :]
        scores = _nt_dot(qc.astype(BF16), kc.astype(BF16)) * dec
        o_s[sl, :] = (jnp.dot(scores.astype(BF16), vb, preferred_element_type=F32)
                      + jnp.dot((qc * g_in).astype(BF16), s.astype(BF16), preferred_element_type=F32))
        s = s * g_all + _tn_dot((kc * g_out).astype(BF16), vb)
    out_ref[...] = (_rms_rows(o_s[...]) * jax.nn.silu(gate)).astype(out_ref.dtype)
    st_ref[...] = s


def _log_gamma_rows(width):
    lg = jnp.log1p(-jnp.exp2(-5.0 - jnp.arange(RET_HEADS, dtype=F32)))
    return jnp.broadcast_to(lg[:, None, None], (RET_HEADS, 1, width))


def ret_prompt(layer, hin, w_in_t, cos, sin):
    tab = pl.BlockSpec((SEQ, RET_DK), lambda b, h: (0, 0))
    ret_rows = tuple(RET_COL0 + off for off in (OFF_RQ, OFF_RK, OFF_RV, OFF_RG))
    return pl.pallas_call(
        _ret_prompt_kernel,
        out_shape=(jax.ShapeDtypeStruct((BATCH * SEQ, BRANCH_W), BF16),
                   jax.ShapeDtypeStruct((BATCH, RET_HEADS, RET_DK, RET_DK), F32)),
        grid=(BATCH, RET_HEADS),
        in_specs=[pl.BlockSpec((SEQ, D_MODEL), lambda b, h: (b, 0))]
                 + _w_in_row_specs(layer, ret_rows, _prompt_unit_rows)
                 + [tab, tab, pl.BlockSpec((None, 1, RET_CHUNK), lambda b, h: (h, 0, 0))],
        out_specs=(pl.BlockSpec((SEQ, RET_DK), lambda b, h: (b, h)),
                   pl.BlockSpec((None, None, RET_DK, RET_DK), lambda b, h: (b, h, 0, 0))),
        scratch_shapes=[pltpu.VMEM((SEQ, RET_DK), F32), pltpu.VMEM((SEQ, RET_DK), F32),
                        pltpu.VMEM((SEQ, RET_DK), BF16), pltpu.VMEM((SEQ, RET_DK), F32)],
        compiler_params=_cparams(2),
        name="ret_prompt",
    )(hin, w_in_t, w_in_t, w_in_t, w_in_t, cos, sin, _log_gamma_rows(RET_CHUNK))


def _ret_sample_kernel(q_ref, k_ref, v_ref, g_ref, cos_ref, sin_ref, lg_ref, s_ref, out_ref, so_ref, o_s):
    cos = cos_ref[0:1, :]
    sin = sin_ref[0:1, :]
    q = _rope(q_ref[...], cos, sin)
    k = _rope(k_ref[...], cos, sin) * RET_DK ** -0.5
    v = v_ref[...]
    gamma = jnp.exp(lg_ref[...])
    _sample_state_step(s_ref, so_ref, o_s, lambda j: gamma, q, k, v)
    out_ref[...] = _rms_rows(o_s[...]) * jax.nn.silu(g_ref[...])


def ret_sample(layer, proj, cos, sin, state, state_out):
    cb = lambda off: off // RET_DK
    nb = SAMPLE_BLK
    col = lambda off: pl.BlockSpec((nb, RET_DK), lambda h, i: (i, cb(off) + h))
    tab = pl.BlockSpec((SUBLANES, RET_DK), lambda h, i: (0, 0))
    st_spec = pl.BlockSpec((None, nb, None, RET_DK, RET_DK), lambda h, i: (layer, i, h, 0, 0))
    args = [proj, proj, proj, proj, cos, sin, _log_gamma_rows(LANES), state]
    in_specs = [col(OFF_RQ), col(OFF_RK), col(OFF_RV), col(OFF_RG), tab, tab,
                pl.BlockSpec((None, 1, LANES), lambda h, i: (h, 0, 0)), st_spec]
    aliases = {}
    kern = _ret_sample_kernel
    if state_out is not None:
        args.append(state_out)
        in_specs.append(pl.BlockSpec(memory_space=pl.ANY))
        aliases = {len(args) - 1: 1}
        kern = functools.partial(_drop_alias_arg, kern, len(args) - 1)
    return pl.pallas_call(
        kern,
        out_shape=(jax.ShapeDtypeStruct((DEC_BATCH, BRANCH_W), F32),
                   jax.ShapeDtypeStruct(state.shape, F32)),
        grid=(RET_HEADS, DEC_BATCH // nb),
        in_specs=in_specs,
        out_specs=(pl.BlockSpec((nb, RET_DK), lambda h, i: (i, h)), st_spec),
        scratch_shapes=[pltpu.VMEM((nb, RET_DK), F32)],
        input_output_aliases=aliases,
        compiler_params=_cparams(2),
        name="ret_sample",
    )(*args)


def kernel(x_prompt, x_sample, state_lru_h, state_lru_conv, state_hgrn, state_ssd, state_ssd_conv, state_ret, state_ffn_conv, g_mix, g_ffn, w_in, lru_conv_w, lru_conv_b, lru_wa, lru_ba, lru_wx, lru_bx, lru_lambda, hg_lb_logits, hg_norm_w, ssd_conv_w, ssd_conv_b, ssd_dt_bias, ssd_a_log, ssd_d, ssd_norm_w, w_branch, w_gate, w_out, ffn_w_up, ffn_w_val, ffn_conv_w, ffn_conv_b, ffn_w_down, g_final):
    xp = x_prompt.reshape(BATCH * SEQ, D_MODEL)
    xs = x_sample.reshape(DEC_BATCH, D_MODEL)

    cos_p, sin_p = rope_tables(SEQ, 0, True)
    cos_s, sin_s = rope_tables(SUBLANES, PAST_LEN, False)

    hp = rmsnorm(xp, g_mix[0], BF16, NORM_TM)
    hs = rmsnorm(xs, g_mix[0], BF16, DEC_BATCH)

    state_ssd_t = jnp.swapaxes(state_ssd, -1, -2)

    w_in_t = jnp.swapaxes(w_in, 1, 2)
    w_gate_b = jnp.transpose(w_gate, (0, 2, 1, 3)).astype(BF16)
    w_out_b = w_out.astype(BF16)
    w_down_b = ffn_w_down.astype(BF16)

    prompt_states, sample_small = [], []
    hg_out = ssd_out = ret_out = None
    for l in range(DEPTH):
        p = {"lru_conv_w": lru_conv_w[l], "lru_conv_b": lru_conv_b[l], "lru_wa": lru_wa[l], "lru_ba": lru_ba[l],
             "lru_wx": lru_wx[l], "lru_bx": lru_bx[l], "lru_lambda": lru_lambda[l],
             "hg_lb_logits": hg_lb_logits, "hg_norm_w": hg_norm_w[l],
             "ssd_conv_w": ssd_conv_w[l], "ssd_conv_b": ssd_conv_b[l], "ssd_dt_bias": ssd_dt_bias[l],
             "ssd_a_log": ssd_a_log[l], "ssd_d": ssd_d[l], "ssd_norm_w": ssd_norm_w[l]}
        last = l == DEPTH - 1
        g_next = g_final if last else g_mix[l + 1]

        out_a, lru_h_p, lru_conv_p = lru_prompt(l, hp, w_in_t, p)
        out_b, hg_p = hgrn_prompt(l, hp, w_in_t, p)
        proj_ssd_p, _ = in_proj_prompt(l, hp, w_in_t, OFF_SZ, SSD_TILES, SSD_TN, SSD_PROJ_TM)
        out_c, ssd_p, ssd_conv_p = ssd_prompt(proj_ssd_p, p)
        out_d, ret_p = ret_prompt(l, hp, w_in_t, cos_p, sin_p)
        merged, w_branch_b = gated_merge_prompt(l, hp, (out_a, out_b, out_c, out_d), w_gate_b, w_branch,
                                                MERGE_TM, MERGE_TN)
        xp, h2 = out_proj_residual_norm(l, merged, w_out_b, xp, g_ffn[l], OUT_PROJ_TM)
        act, ffn_conv_p, w_up_b, w_val_b = ffn_prompt(l, h2, ffn_w_up, ffn_w_val, ffn_conv_w[l], ffn_conv_b[l], FFN_TN)
        res = down_proj_residual_norm(l, act, w_down_b, xp, g_next, DOWN_PROJ_TM, DOWN_PROJ_TK, not last,
                                      F32 if last else BF16)
        if last:
            (yp,) = res
        else:
            xp, hp = res
        prompt_states.append((lru_h_p, lru_conv_p, hg_p, ssd_p, ssd_conv_p, ret_p, ffn_conv_p))

        proj_s = in_proj_sample(l, hs, w_in_t, 0, MAIN_TILES, MAIN_TN)
        proj_ret_s = in_proj_sample(l, hs, w_in_t, RET_COL0, RET_TILES, RET_TN)
        lru_buf_t = jnp.swapaxes(state_lru_conv[l], 0, 1)
        ssd_buf_t = jnp.swapaxes(state_ssd_conv[l], 0, 1)
        s_a, lru_h_s, lru_nbuf = lru_sample(proj_s, p, state_lru_h[l], lru_buf_t)
        s_b, hg_out = hgrn_sample(l, proj_s, p, state_hgrn, hg_out)
        s_c, ssd_out, ssd_nbuf = ssd_sample(l, proj_s, p, state_ssd_t, ssd_out, ssd_buf_t)
        s_d, ret_out = ret_sample(l, proj_ret_s, cos_s, sin_s, state_ret, ret_out)
        merged_s = gated_merge_sample(l, hs, (s_a, s_b, s_c, s_d), w_gate_b, w_branch_b, MERGE_TN)
        xs, h2s = out_proj_residual_norm(l, merged_s, w_out_b, xs, g_ffn[l], DEC_BATCH)
        act_s, ffn_nbuf = ffn_sample(l, h2s, w_up_b, w_val_b, ffn_conv_w[l], ffn_conv_b[l], state_ffn_conv,
                                       FFN_SAMPLE_TN)
        res = down_proj_residual_norm(l, act_s, w_down_b, xs, g_next, DEC_BATCH, DOWN_PROJ_TK, not last,
                                      F32 if last else BF16)
        if last:
            (ys,) = res
        else:
            xs, hs = res
        sample_small.append((lru_h_s, jnp.swapaxes(lru_nbuf, 0, 1), jnp.swapaxes(ssd_nbuf, 0, 1), ffn_nbuf))

    stack_p = lambda i: jnp.stack([st[i] for st in prompt_states], axis=0)
    stack_s = lambda i: jnp.stack([st[i] for st in sample_small], axis=0)
    return (yp.reshape(BATCH, SEQ, D_MODEL), ys.reshape(DEC_BATCH, 1, D_MODEL),
            stack_p(0), stack_s(0), stack_p(1), stack_s(1),
            stack_p(2), hg_out, jnp.swapaxes(stack_p(3), -1, -2), jnp.swapaxes(ssd_out, -1, -2),
            stack_p(4), stack_s(2), stack_p(5), ret_out,
            stack_p(6), stack_s(3))
```

```python
import functools

import numpy as np
import jax
import jax.numpy as jnp
from jax import lax
from jax.experimental import pallas as pl
from jax.experimental.pallas import tpu as pltpu

F32 = jnp.float32
BF16 = jnp.bfloat16

D_MODEL = 2048
BATCH = 4
SEQ = 2048
DEPTH = 2
DEC_BATCH = 128
PAST_LEN = 16384
BRANCH_W = D_MODEL // 2
EPS = 1e-6
LRU_BLOCKS = 8
LRU_BLOCK = BRANCH_W // LRU_BLOCKS
LRU_C = 8.0
CONV_W = 4
HG_HEADS = 8
HG_DK = BRANCH_W // HG_HEADS
SSD_HEADDIM = 64
SSD_HEADS = BRANCH_W // SSD_HEADDIM
SSD_GROUPS = 2
SSD_STATE = 128
SSD_BC = 2 * SSD_GROUPS * SSD_STATE
SSD_CONV_DIM = BRANCH_W + SSD_BC
RET_HEADS = 8
RET_DK = BRANCH_W // RET_HEADS
ROPE_BASE = 10000.0
D_FF = 5632
FFN_CONV_W = 3

V7X_VMEM_BYTES = 64 * 1024 * 1024
VMEM_LIMIT_BYTES = V7X_VMEM_BYTES - 8 * 1024 * 1024
LANES = 128
SUBLANES = 8

N_BRANCH = 4
N_IN = 12816
OFF_XA, OFF_YA = 0, 1024
OFF_HQ, OFF_HF, OFF_HI, OFF_HG = 2048, 3072, 4096, 5120
OFF_SZ, OFF_SX, OFF_SBC = 6144, 7168, 8192
OFF_SDT = 8704
MAIN_TN, MAIN_TILES = 1280, 7
N_MAIN = MAIN_TN * MAIN_TILES
RET_COL0 = OFF_SDT + SSD_HEADS
RET_TN, RET_TILES = 1024, 4
SSD_TN, SSD_TILES = 896, 3
SSD_PROJ_W = SSD_TN * SSD_TILES
OFF_RQ, OFF_RK, OFF_RV, OFF_RG = 0, 1024, 2048, 3072
assert RET_COL0 + RET_TN * RET_TILES == N_IN and OFF_SZ + SSD_PROJ_W >= OFF_SDT + LANES

NORM_TM = 512
SSD_PROJ_TM = 1024
MERGE_TM, MERGE_TN = 1024, 256
OUT_PROJ_TM = 512
FFN_TN = 256
FFN_ROW_CHUNK = 512
DOWN_PROJ_TM = 512
DOWN_PROJ_TK = 2816
FFN_SAMPLE_TN = 512

HG_CHUNK = 128
HG_LEVELS = (1, 2, 4, 8, 16, 32, 64)
SSD_CHUNK = 128
RET_CHUNK = 256
SAMPLE_BLK = 64
SSD_SAMPLE_BLK = 8


def _cparams(n_axes):
    return pltpu.CompilerParams(dimension_semantics=("arbitrary",) * n_axes,
                                vmem_limit_bytes=VMEM_LIMIT_BYTES)


def _rms_rows(x):
    return x * lax.rsqrt(jnp.mean(x * x, axis=-1, keepdims=True) + EPS)


def _shift_rows(x, d, row):
    return jnp.where(row >= d, pltpu.roll(x, d, axis=0), 0.0)


def _nt_dot(a, b):
    return lax.dot_general(a, b, (((1,), (1,)), ((), ())), preferred_element_type=F32)


def _tn_dot(a, b):
    return lax.dot_general(a, b, (((0,), (0,)), ((), ())), preferred_element_type=F32)


def _project(h_ref, w_refs):
    w = jnp.concatenate([w_ref[0].astype(BF16) for w_ref in w_refs], axis=0)
    return _nt_dot(h_ref[...], w)


def _w_in_row_specs(layer, offsets, index_map_for):
    return [pl.BlockSpec((pl.Element(1), pl.Element(LANES), pl.Element(D_MODEL)), index_map_for(layer, off))
            for off in offsets]


def _rows_to_cols(x):
    n = x.shape[0]
    if n < LANES:
        x = jnp.concatenate([x, jnp.zeros((LANES - n, x.shape[1]), x.dtype)], axis=0)
    return x.T


def _norm_kernel(x_ref, g_ref, o_ref):
    o_ref[...] = (_rms_rows(x_ref[...]) * g_ref[...]).astype(o_ref.dtype)


def rmsnorm(x, g, out_dtype, tm):
    m, d = x.shape
    return pl.pallas_call(
        _norm_kernel,
        out_shape=jax.ShapeDtypeStruct((m, d), out_dtype),
        grid=(m // tm,),
        in_specs=[pl.BlockSpec((tm, d), lambda i: (i, 0)),
                  pl.BlockSpec((1, d), lambda i: (0, 0))],
        out_specs=pl.BlockSpec((tm, d), lambda i: (i, 0)),
        compiler_params=_cparams(1),
        name="rmsnorm",
    )(x, g.reshape(1, d))


def _mm_nt_cast_kernel(a_ref, wt_ref, o_ref, wb_ref):
    @pl.when(pl.program_id(1) == 0)
    def _():
        wb_ref[...] = wt_ref[0].astype(BF16)

    o_ref[...] = _nt_dot(a_ref[...], wb_ref[...])


def in_proj_prompt(layer, h, w_in_t, row0, n_tiles, tn, tm):
    m = h.shape[0]
    n = n_tiles * tn
    w_spec = pl.BlockSpec((pl.Element(1), pl.Element(tn), pl.Element(D_MODEL)),
                          lambda j, i: (layer, pl.multiple_of(row0 + j * tn, SUBLANES), 0))
    return pl.pallas_call(
        _mm_nt_cast_kernel,
        out_shape=(jax.ShapeDtypeStruct((m, n), F32), jax.ShapeDtypeStruct((n, D_MODEL), BF16)),
        grid=(n_tiles, m // tm),
        in_specs=[pl.BlockSpec((tm, D_MODEL), lambda j, i: (i, 0)), w_spec],
        out_specs=(pl.BlockSpec((tm, tn), lambda j, i: (i, j)),
                   pl.BlockSpec((tn, D_MODEL), lambda j, i: (j, 0))),
        compiler_params=_cparams(2),
        name="in_proj_prompt",
    )(h, w_in_t)


def _mm_nt_castw_kernel(a_ref, wt_ref, o_ref):
    o_ref[...] = _nt_dot(a_ref[...], wt_ref[0].astype(BF16))


def in_proj_sample(layer, h, w_in_t, row0, n_tiles, tn):
    m = h.shape[0]
    w_spec = pl.BlockSpec((pl.Element(1), pl.Element(tn), pl.Element(D_MODEL)),
                          lambda j: (layer, pl.multiple_of(row0 + j * tn, SUBLANES), 0))
    return pl.pallas_call(
        _mm_nt_castw_kernel,
        out_shape=jax.ShapeDtypeStruct((m, n_tiles * tn), F32),
        grid=(n_tiles,),
        in_specs=[pl.BlockSpec((m, D_MODEL), lambda j: (0, 0)), w_spec],
        out_specs=pl.BlockSpec((m, tn), lambda j: (0, j)),
        compiler_params=_cparams(1),
        name="in_proj_sample",
    )(h, w_in_t)


def _gated_sum(h, br_refs, gate_w, branch_w):
    acc = None
    for k, br_ref in enumerate(br_refs):
        gate = jax.nn.sigmoid(jnp.dot(h, gate_w(k), preferred_element_type=F32))
        br = jnp.dot(br_ref[...].astype(BF16), branch_w(k), preferred_element_type=F32)
        acc = gate * br if acc is None else acc + gate * br
    return acc


def _merge_cast_kernel(h_ref, a_ref, b_ref, c_ref, d_ref, wg_ref, wb_ref, o_ref, wbb_ref):
    @pl.when(pl.program_id(1) == 0)
    def _():
        for k in range(N_BRANCH):
            wbb_ref[k] = wb_ref[k].astype(BF16)

    acc = _gated_sum(h_ref[...], (a_ref, b_ref, c_ref, d_ref), lambda k: wg_ref[k], lambda k: wbb_ref[k])
    o_ref[...] = acc.astype(o_ref.dtype)


def gated_merge_prompt(layer, h, branches, w_gate_b, w_branch, tm, tn):
    m = h.shape[0]
    br_specs = [pl.BlockSpec((tm, BRANCH_W), lambda j, i: (i, 0)) for _ in range(N_BRANCH)]
    return pl.pallas_call(
        _merge_cast_kernel,
        out_shape=(jax.ShapeDtypeStruct((m, D_MODEL), BF16),
                   jax.ShapeDtypeStruct((N_BRANCH, BRANCH_W, D_MODEL), BF16)),
        grid=(D_MODEL // tn, m // tm),
        in_specs=[pl.BlockSpec((tm, D_MODEL), lambda j, i: (i, 0))] + br_specs
                 + [pl.BlockSpec((None, N_BRANCH, D_MODEL, tn), lambda j, i: (layer, 0, 0, j)),
                    pl.BlockSpec((None, N_BRANCH, BRANCH_W, tn), lambda j, i: (layer, 0, 0, j))],
        out_specs=(pl.BlockSpec((tm, tn), lambda j, i: (i, j)),
                   pl.BlockSpec((N_BRANCH, BRANCH_W, tn), lambda j, i: (0, 0, j))),
        compiler_params=_cparams(2),
        name="gated_merge_prompt",
    )(h, *branches, w_gate_b, w_branch)


def _merge_kernel(h_ref, a_ref, b_ref, c_ref, d_ref, wgb_ref, wbb_ref, o_ref):
    acc = _gated_sum(h_ref[...], (a_ref, b_ref, c_ref, d_ref), lambda k: wgb_ref[k], lambda k: wbb_ref[k])
    o_ref[...] = acc.astype(o_ref.dtype)


def gated_merge_sample(layer, h, branches, w_gate_b, w_branch_b, tn):
    m = h.shape[0]
    br_specs = [pl.BlockSpec((m, BRANCH_W), lambda j: (0, 0)) for _ in range(N_BRANCH)]
    return pl.pallas_call(
        _merge_kernel,
        out_shape=jax.ShapeDtypeStruct((m, D_MODEL), BF16),
        grid=(D_MODEL // tn,),
        in_specs=[pl.BlockSpec((m, D_MODEL), lambda j: (0, 0))] + br_specs
                 + [pl.BlockSpec((None, N_BRANCH, D_MODEL, tn), lambda j: (layer, 0, 0, j)),
                    pl.BlockSpec((N_BRANCH, BRANCH_W, tn), lambda j: (0, 0, j))],
        out_specs=pl.BlockSpec((m, tn), lambda j: (0, j)),
        compiler_params=_cparams(1),
        name="gated_merge_sample",
    )(h, *branches, w_gate_b, w_branch_b)


def _out_proj_kernel(m_ref, w_ref, x_ref, g_ref, xo_ref, ho_ref):
    x_new = x_ref[...] + jnp.dot(m_ref[...], w_ref[...], preferred_element_type=F32)
    xo_ref[...] = x_new
    ho_ref[...] = (_rms_rows(x_new) * g_ref[...]).astype(ho_ref.dtype)


def out_proj_residual_norm(layer, merged, w_out, x, g, tm):
    m = x.shape[0]
    return pl.pallas_call(
        _out_proj_kernel,
        out_shape=(jax.ShapeDtypeStruct((m, D_MODEL), F32), jax.ShapeDtypeStruct((m, D_MODEL), BF16)),
        grid=(m // tm,),
        in_specs=[pl.BlockSpec((tm, D_MODEL), lambda i: (i, 0)),
                  pl.BlockSpec((None, D_MODEL, D_MODEL), lambda i: (layer, 0, 0)),
                  pl.BlockSpec((tm, D_MODEL), lambda i: (i, 0)),
                  pl.BlockSpec((1, D_MODEL), lambda i: (0, 0))],
        out_specs=(pl.BlockSpec((tm, D_MODEL), lambda i: (i, 0)),
                   pl.BlockSpec((tm, D_MODEL), lambda i: (i, 0))),
        compiler_params=_cparams(1),
        name="out_proj",
    )(merged, w_out, x, g.reshape(1, D_MODEL))


def _down_proj_kernel(emit_x, a_ref, w_ref, x_ref, g_ref, *refs):
    if emit_x:
        xo_ref, no_ref, acc_ref = refs
    else:
        no_ref, acc_ref = refs
    kk = pl.program_id(1)

    @pl.when(kk == 0)
    def _():
        acc_ref[...] = x_ref[...]

    acc_ref[...] += jnp.dot(a_ref[...], w_ref[...], preferred_element_type=F32)

    @pl.when(kk == pl.num_programs(1) - 1)
    def _():
        x_new = acc_ref[...]
        if emit_x:
            xo_ref[...] = x_new
        no_ref[...] = (_rms_rows(x_new) * g_ref[...]).astype(no_ref.dtype)


def down_proj_residual_norm(layer, a, w_down, x, g, tm, tk, emit_x, norm_dtype):
    m = x.shape[0]
    out_shape = [jax.ShapeDtypeStruct((m, D_MODEL), norm_dtype)]
    out_specs = [pl.BlockSpec((tm, D_MODEL), lambda i, k: (i, 0))]
    if emit_x:
        out_shape = [jax.ShapeDtypeStruct((m, D_MODEL), F32)] + out_shape
        out_specs = [pl.BlockSpec((tm, D_MODEL), lambda i, k: (i, 0))] + out_specs
    return pl.pallas_call(
        functools.partial(_down_proj_kernel, emit_x),
        out_shape=tuple(out_shape),
        grid=(m // tm, D_FF // tk),
        in_specs=[pl.BlockSpec((tm, tk), lambda i, k: (i, k)),
                  pl.BlockSpec((None, tk, D_MODEL), lambda i, k: (layer, k, 0)),
                  pl.BlockSpec((tm, D_MODEL), lambda i, k: (i, 0)),
                  pl.BlockSpec((1, D_MODEL), lambda i, k: (0, 0))],
        out_specs=tuple(out_specs),
        scratch_shapes=[pltpu.VMEM((tm, D_MODEL), F32)],
        compiler_params=_cparams(2),
        name="down_proj",
    )(a, w_down, x, g.reshape(1, D_MODEL))


def _ffn_prompt_kernel(h_ref, wu_ref, wv_ref, cw_ref, cb_ref, a_ref, st_ref, wub_ref, wvb_ref):
    @pl.when(pl.program_id(1) == 0)
    def _():
        wub_ref[...] = wu_ref[...].astype(BF16)
        wvb_ref[...] = wv_ref[...].astype(BF16)

    rc = FFN_ROW_CHUNK
    tn = a_ref.shape[1]
    cw = cw_ref[...]
    cbias = cb_ref[...]
    row8 = lax.broadcasted_iota(jnp.int32, (SUBLANES, tn), 0)
    tail = jnp.zeros((SUBLANES, tn), F32)
    pending = None
    for c in range(h_ref.shape[0] // rc):
        rows = slice(c * rc, (c + 1) * rc)
        hc = h_ref[rows, :]
        u = jnp.dot(hc, wub_ref[...], preferred_element_type=F32)
        if pending is not None:
            prev_rows, g_prev, v_prev = pending
            a_ref[prev_rows, :] = (g_prev * v_prev).astype(a_ref.dtype)
        v = jnp.dot(hc, wvb_ref[...], preferred_element_type=F32)
        uc = cbias + cw[FFN_CONV_W - 1:FFN_CONV_W] * u
        for d in range(1, FFN_CONV_W):
            rolled = pltpu.roll(u, d, axis=0)
            top = jnp.where(row8 >= d, rolled[:SUBLANES], pltpu.roll(tail, d, axis=0))
            uc = uc + cw[FFN_CONV_W - 1 - d:FFN_CONV_W - d] * jnp.concatenate([top, rolled[SUBLANES:]], axis=0)
        pending = (rows, jax.nn.gelu(uc), v)
        tail = u[rc - SUBLANES:, :]
    prev_rows, g_prev, v_prev = pending
    a_ref[prev_rows, :] = (g_prev * v_prev).astype(a_ref.dtype)
    st_ref[...] = tail[SUBLANES - (FFN_CONV_W - 1):, :]


def ffn_prompt(layer, h2, w_up, w_val, conv_w, conv_b, tn):
    wspec = pl.BlockSpec((None, D_MODEL, tn), lambda j, b: (layer, 0, j))
    wbspec = pl.BlockSpec((D_MODEL, tn), lambda j, b: (0, j))
    return pl.pallas_call(
        _ffn_prompt_kernel,
        out_shape=(jax.ShapeDtypeStruct((BATCH * SEQ, D_FF), BF16),
                   jax.ShapeDtypeStruct((BATCH, FFN_CONV_W - 1, D_FF), F32),
                   jax.ShapeDtypeStruct((D_MODEL, D_FF), BF16),
                   jax.ShapeDtypeStruct((D_MODEL, D_FF), BF16)),
        grid=(D_FF // tn, BATCH),
        in_specs=[pl.BlockSpec((SEQ, D_MODEL), lambda j, b: (b, 0)), wspec, wspec,
                  pl.BlockSpec((FFN_CONV_W, tn), lambda j, b: (0, j)),
                  pl.BlockSpec((1, tn), lambda j, b: (0, j))],
        out_specs=(pl.BlockSpec((SEQ, tn), lambda j, b: (b, j)),
                   pl.BlockSpec((None, FFN_CONV_W - 1, tn), lambda j, b: (b, 0, j)),
                   wbspec, wbspec),
        compiler_params=_cparams(2),
        name="ffn_prompt",
    )(h2, w_up, w_val, conv_w, conv_b.reshape(1, D_FF))


def _ffn_sample_kernel(h_ref, wu_ref, wv_ref, cw_ref, cb_ref, buf_ref, a_ref, nb_ref):
    h = h_ref[...]
    u = jnp.dot(h, wu_ref[...], preferred_element_type=F32)
    v = jnp.dot(h, wv_ref[...], preferred_element_type=F32)
    cw = cw_ref[...]
    b0 = buf_ref[:, 0, :]
    b1 = buf_ref[:, 1, :]
    uc = cb_ref[...] + cw[0:1] * b0 + cw[1:2] * b1 + cw[2:3] * u
    a_ref[...] = (jax.nn.gelu(uc) * v).astype(a_ref.dtype)
    nb_ref[:, 0, :] = b1
    nb_ref[:, 1, :] = u


def ffn_sample(layer, h2, w_up, w_val, conv_w, conv_b, buf, tn):
    return pl.pallas_call(
        _ffn_sample_kernel,
        out_shape=(jax.ShapeDtypeStruct((DEC_BATCH, D_FF), BF16),
                   jax.ShapeDtypeStruct((DEC_BATCH, FFN_CONV_W - 1, D_FF), F32)),
        grid=(D_FF // tn,),
        in_specs=[pl.BlockSpec((DEC_BATCH, D_MODEL), lambda j: (0, 0)),
                  pl.BlockSpec((D_MODEL, tn), lambda j: (0, j)),
                  pl.BlockSpec((D_MODEL, tn), lambda j: (0, j)),
                  pl.BlockSpec((FFN_CONV_W, tn), lambda j: (0, j)),
                  pl.BlockSpec((1, tn), lambda j: (0, j)),
                  pl.BlockSpec((None, DEC_BATCH, FFN_CONV_W - 1, tn), lambda j: (layer, 0, 0, j))],
        out_specs=(pl.BlockSpec((DEC_BATCH, tn), lambda j: (0, j)),
                   pl.BlockSpec((DEC_BATCH, FFN_CONV_W - 1, tn), lambda j: (0, 0, j))),
        compiler_params=_cparams(1),
        name="ffn_sample",
    )(h2, w_up, w_val, conv_w, conv_b.reshape(1, D_FF), buf)


def _rope_table_kernel(start, consecutive, freq_ref, sign_ref, cos_ref, sin_ref):
    shape = cos_ref.shape
    if consecutive:
        pos = lax.broadcasted_iota(jnp.int32, shape, 0).astype(F32) + float(start)
    else:
        pos = jnp.full(shape, float(start), F32)
    ang = pos * freq_ref[...]
    cos_ref[...] = jnp.cos(ang)
    sin_ref[...] = sign_ref[...] * jnp.sin(ang)


def rope_tables(n_rows, start, consecutive):
    half = RET_DK // 2
    freqs = ROPE_BASE ** (-jnp.arange(half, dtype=F32) / half)
    freq2 = jnp.concatenate([freqs, freqs]).reshape(1, RET_DK)
    sign = jnp.concatenate([-jnp.ones((half,), F32), jnp.ones((half,), F32)]).reshape(1, RET_DK)
    return pl.pallas_call(
        functools.partial(_rope_table_kernel, start, consecutive),
        out_shape=(jax.ShapeDtypeStruct((n_rows, RET_DK), F32), jax.ShapeDtypeStruct((n_rows, RET_DK), F32)),
        name="rope_tables",
    )(freq2, sign)


def _rope(x, cos, sin_signed):
    return x * cos + pltpu.roll(x, RET_DK // 2, axis=1) * sin_signed


def _lru_gates(conv, wa_ref, ba_ref, wx_ref, bx_ref, lam_ref):
    xb = conv.astype(BF16)
    r = jax.nn.sigmoid(jnp.dot(xb, wa_ref[...].astype(BF16), preferred_element_type=F32) + ba_ref[...])
    i = jax.nn.sigmoid(jnp.dot(xb, wx_ref[...].astype(BF16), preferred_element_type=F32) + bx_ref[...])
    log_a = -LRU_C * r * jax.nn.softplus(-lam_ref[...])
    a = jnp.exp(log_a)
    u = jnp.sqrt(1.0 - a * a) * (i * conv)
    return a, u


def _lru_prompt_kernel(hin_ref, wxa_ref, wya_ref, cw_ref, cb_ref, wa_ref, ba_ref, wx_ref, bx_ref, lam_ref,
                       out_ref, h_ref, conv_ref, ag_s, ug_s):
    proj = _project(hin_ref, (wxa_ref, wya_ref))
    x = proj[:, :LRU_BLOCK]
    ya = proj[:, LRU_BLOCK:]
    t_len = x.shape[0]
    row = lax.broadcasted_iota(jnp.int32, x.shape, 0)
    cw = cw_ref[...]
    conv = cb_ref[...] + cw[CONV_W - 1:CONV_W] * x
    for d in range(1, CONV_W):
        conv = conv + cw[CONV_W - 1 - d:CONV_W - d] * _shift_rows(x, d, row)
    a, u = _lru_gates(conv, wa_ref, ba_ref, wx_ref, bx_ref, lam_ref)
    ng = t_len // SUBLANES
    a3 = a.reshape(ng, SUBLANES, LRU_BLOCK)
    u3 = u.reshape(ng, SUBLANES, LRU_BLOCK)
    sub = lax.broadcasted_iota(jnp.int32, a3.shape, 1)
    d = 1
    while d < SUBLANES:
        keep = sub >= d
        u3 = jnp.where(keep, a3 * pltpu.roll(u3, d, axis=1) + u3, u3)
        a3 = jnp.where(keep, a3 * pltpu.roll(a3, d, axis=1), a3)
        d *= 2
    ag_s[...] = a3.reshape(t_len, LRU_BLOCK)
    ug_s[...] = u3.reshape(t_len, LRU_BLOCK)
    ag = ag_s[pl.ds(SUBLANES - 1, ng, stride=SUBLANES), :]
    ug = ug_s[pl.ds(SUBLANES - 1, ng, stride=SUBLANES), :]
    grow = lax.broadcasted_iota(jnp.int32, ag.shape, 0)
    d = 1
    while d < ng:
        keep = grow >= d
        ug = jnp.where(keep, ag * pltpu.roll(ug, d, axis=0) + ug, ug)
        ag = jnp.where(keep, ag * pltpu.roll(ag, d, axis=0), ag)
        d *= 2
    carry = _shift_rows(ug, 1, grow)
    h3 = a3 * jnp.broadcast_to(carry[:, None, :], a3.shape) + u3
    hs = h3.reshape(t_len, LRU_BLOCK)
    out_ref[...] = (hs * jax.nn.gelu(ya)).astype(out_ref.dtype)
    h_ref[...] = ug[ng - 1:, :]
    conv_ref[...] = x[t_len - (CONV_W - 1):, :]


def _lru_param_specs(n_axes_fn):
    blk3 = lambda shape: pl.BlockSpec(shape, n_axes_fn(lambda n: (n, 0, 0)))
    return [pl.BlockSpec((CONV_W, LRU_BLOCK), n_axes_fn(lambda n: (0, n))),
            pl.BlockSpec((1, LRU_BLOCK), n_axes_fn(lambda n: (0, n))),
            blk3((None, LRU_BLOCK, LRU_BLOCK)), blk3((None, 1, LRU_BLOCK)),
            blk3((None, LRU_BLOCK, LRU_BLOCK)), blk3((None, 1, LRU_BLOCK)),
            blk3((None, 1, LRU_BLOCK))]


def _lru_params(p):
    return (p["lru_conv_w"], p["lru_conv_b"].reshape(1, BRANCH_W),
            p["lru_wa"], p["lru_ba"].reshape(LRU_BLOCKS, 1, LRU_BLOCK),
            p["lru_wx"], p["lru_bx"].reshape(LRU_BLOCKS, 1, LRU_BLOCK),
            p["lru_lambda"].reshape(LRU_BLOCKS, 1, LRU_BLOCK))


def _prompt_unit_rows(layer, off):
    return lambda b, u: (layer, pl.multiple_of(off + u * LANES, SUBLANES), 0)


def lru_prompt(layer, hin, w_in_t, p):
    wrap = lambda f: (lambda b, n: f(n))
    out, h, conv = pl.pallas_call(
        _lru_prompt_kernel,
        out_shape=(jax.ShapeDtypeStruct((BATCH * SEQ, BRANCH_W), BF16),
                   jax.ShapeDtypeStruct((BATCH, 1, BRANCH_W), F32),
                   jax.ShapeDtypeStruct((BATCH, CONV_W - 1, BRANCH_W), F32)),
        grid=(BATCH, LRU_BLOCKS),
        in_specs=[pl.BlockSpec((SEQ, D_MODEL), lambda b, n: (b, 0))]
                 + _w_in_row_specs(layer, (OFF_XA, OFF_YA), _prompt_unit_rows) + _lru_param_specs(wrap),
        out_specs=(pl.BlockSpec((SEQ, LRU_BLOCK), lambda b, n: (b, n)),
                   pl.BlockSpec((None, 1, LRU_BLOCK), lambda b, n: (b, 0, n)),
                   pl.BlockSpec((None, CONV_W - 1, LRU_BLOCK), lambda b, n: (b, 0, n))),
        scratch_shapes=[pltpu.VMEM((SEQ, LRU_BLOCK), F32)] * 2,
        compiler_params=_cparams(2),
        name="lru_prompt",
    )(hin, w_in_t, w_in_t, *_lru_params(p))
    return out, h.reshape(BATCH, BRANCH_W), conv


def _lru_sample_kernel(xa_ref, ya_ref, cw_ref, cb_ref, wa_ref, ba_ref, wx_ref, bx_ref, lam_ref,
                       h0_ref, buf_ref, out_ref, h_ref, nbuf_ref):
    x = xa_ref[...]
    cw = cw_ref[...]
    conv = cb_ref[...] + cw[CONV_W - 1:CONV_W] * x
    for j in range(CONV_W - 1):
        conv = conv + cw[j:j + 1] * buf_ref[j]
    a, u = _lru_gates(conv, wa_ref, ba_ref, wx_ref, bx_ref, lam_ref)
    h = a * h0_ref[...] + u
    out_ref[...] = h * jax.nn.gelu(ya_ref[...])
    h_ref[...] = h
    for j in range(CONV_W - 2):
        nbuf_ref[j] = buf_ref[j + 1]
    nbuf_ref[CONV_W - 2] = x


def lru_sample(proj, p, h0, buf_t):
    cb = lambda off: off // LRU_BLOCK
    wrap = lambda f: f
    return pl.pallas_call(
        _lru_sample_kernel,
        out_shape=(jax.ShapeDtypeStruct((DEC_BATCH, BRANCH_W), F32),
                   jax.ShapeDtypeStruct((DEC_BATCH, BRANCH_W), F32),
                   jax.ShapeDtypeStruct((CONV_W - 1, DEC_BATCH, BRANCH_W), F32)),
        grid=(LRU_BLOCKS,),
        in_specs=[pl.BlockSpec((DEC_BATCH, LRU_BLOCK), lambda n: (0, cb(OFF_XA) + n)),
                  pl.BlockSpec((DEC_BATCH, LRU_BLOCK), lambda n: (0, cb(OFF_YA) + n))]
                 + _lru_param_specs(wrap)
                 + [pl.BlockSpec((DEC_BATCH, LRU_BLOCK), lambda n: (0, n)),
                    pl.BlockSpec((CONV_W - 1, DEC_BATCH, LRU_BLOCK), lambda n: (0, 0, n))],
        out_specs=(pl.BlockSpec((DEC_BATCH, LRU_BLOCK), lambda n: (0, n)),
                   pl.BlockSpec((DEC_BATCH, LRU_BLOCK), lambda n: (0, n)),
                   pl.BlockSpec((CONV_W - 1, DEC_BATCH, LRU_BLOCK), lambda n: (0, 0, n))),
        compiler_params=_cparams(1),
        name="lru_sample",
    )(proj, proj, *_lru_params(p), h0, buf_t)


def _hgrn_lower_bound(layer, logits):
    mx = jnp.max(logits, axis=0, keepdims=True)
    e = jnp.exp(logits - mx)
    ls = e / jnp.sum(e, axis=0, keepdims=True)
    lb = jnp.zeros_like(ls[0:1])
    for i in range(1, layer + 1):
        lb = lb + ls[i:i + 1]
    return lb


def _hgrn_gates(layer, hq, hf, lbl_ref):
    lb = _hgrn_lower_bound(layer, lbl_ref[...])
    q = jax.nn.silu(hq)
    sg = jax.nn.sigmoid(hf)
    f = lb + (1.0 - lb) * sg
    k = (1.0 - lb) * (1.0 - sg)
    return q, f, k


def _hgrn_level_ids():
    c = HG_CHUNK
    t = np.arange(c)[:, None]
    s = np.arange(c)[None, :]
    level = np.zeros((c, c), np.int32)
    for li, m in enumerate(HG_LEVELS):
        same = (t // (2 * m)) == (s // (2 * m))
        level[same & ((t % (2 * m)) >= m) & ((s % (2 * m)) < m)] = li + 1
    level[t == s] = len(HG_LEVELS) + 1
    return level


def _split3_bf16(x):
    hi = x.astype(BF16)
    r1 = x - hi.astype(F32)
    mid = r1.astype(BF16)
    lo = (r1 - mid.astype(F32)).astype(BF16)
    return hi, mid, lo


def _cumsum_rows(tril_b, x):
    return sum(jnp.dot(tril_b, piece, preferred_element_type=F32) for piece in _split3_bf16(x))


def _hgrn_midpoint_factor(b, m):
    c = b.shape[0]
    if 2 * m >= SUBLANES:
        b3 = b.reshape(c // (2 * m), 2 * m, LANES)
        mid = b3[:, m - 1:m, :]
    else:
        b3 = b.reshape(c // SUBLANES, SUBLANES, LANES)
        sub = lax.broadcasted_iota(jnp.int32, b3.shape, 1)
        mid = b3[:, m - 1:m, :]
        for blk in range(1, SUBLANES // (2 * m)):
            lo = blk * 2 * m
            mid = jnp.where(sub >= lo, b3[:, lo + m - 1:lo + m, :], mid)
    return jnp.exp(-jnp.abs(b3 - mid)).reshape(c, LANES)


def _hgrn_prompt_kernel(layer, hin_ref, wq_ref, wf_ref, wi_ref, wg_ref, lbl_ref, nw_ref, lvl_ref,
                        out_ref, st_ref, lf_s, q_s, k_s, v_s, o_s):
    c = HG_CHUNK
    t_len = hin_ref.shape[0]
    proj = _project(hin_ref, (wq_ref, wf_ref, wi_ref, wg_ref))
    q, f, k = _hgrn_gates(layer, proj[:, 0:HG_DK], proj[:, HG_DK:2 * HG_DK], lbl_ref)
    lf_s[...] = jnp.log(f)
    q_s[...] = q
    k_s[...] = k
    v_s[...] = proj[:, 2 * HG_DK:3 * HG_DK].astype(BF16)
    gate = proj[:, 3 * HG_DK:]

    lvl = lvl_ref[...]
    tril_b = (lax.broadcasted_iota(jnp.int32, (c, c), 0) >= lax.broadcasted_iota(jnp.int32, (c, c), 1)).astype(BF16)

    st = jnp.zeros((HG_DK, HG_DK), F32)
    for ci in range(t_len // c):
        sl = slice(ci * c, (ci + 1) * c)
        qc = q_s[sl, :]
        kc = k_s[sl, :]
        vb = v_s[sl, :]
        b = _cumsum_rows(tril_b, lf_s[sl, :])
        att = jnp.where(lvl == len(HG_LEVELS) + 1, _nt_dot(qc.astype(BF16), kc.astype(BF16)), 0.0)
        for li, m in enumerate(HG_LEVELS):
            e = _hgrn_midpoint_factor(b, m)
            a_l = _nt_dot((qc * e).astype(BF16), (kc * e).astype(BF16))
            att = jnp.where(lvl == li + 1, a_l, att)
        o = (jnp.dot(att.astype(BF16), vb, preferred_element_type=F32)
             + _nt_dot((qc * jnp.exp(b)).astype(BF16), st.astype(BF16)))
        o_s[sl, :] = o
        bl = b[c - 1:c, :]
        kdec = (kc * jnp.exp(bl - b)).astype(BF16)
        st = st * jnp.exp(bl) + _tn_dot(vb, kdec)
    o = o_s[...]
    out_ref[...] = (_rms_rows(o) * nw_ref[...] * jax.nn.silu(gate)).astype(out_ref.dtype)
    st_ref[...] = st.T


def hgrn_prompt(layer, hin, w_in_t, p):
    level = _hgrn_level_ids()
    return pl.pallas_call(
        functools.partial(_hgrn_prompt_kernel, layer),
        out_shape=(jax.ShapeDtypeStruct((BATCH * SEQ, BRANCH_W), BF16),
                   jax.ShapeDtypeStruct((BATCH, HG_HEADS, HG_DK, HG_DK), F32)),
        grid=(BATCH, HG_HEADS),
        in_specs=[pl.BlockSpec((SEQ, D_MODEL), lambda b, h: (b, 0))]
                 + _w_in_row_specs(layer, (OFF_HQ, OFF_HF, OFF_HI, OFF_HG), _prompt_unit_rows)
                 + [pl.BlockSpec((DEPTH, HG_DK), lambda b, h: (0, h)),
                    pl.BlockSpec((1, HG_DK), lambda b, h: (0, 0)),
                    pl.BlockSpec(level.shape, lambda b, h: (0, 0))],
        out_specs=(pl.BlockSpec((SEQ, HG_DK), lambda b, h: (b, h)),
                   pl.BlockSpec((None, None, HG_DK, HG_DK), lambda b, h: (b, h, 0, 0))),
        scratch_shapes=[pltpu.VMEM((SEQ, HG_DK), F32)] * 3
                       + [pltpu.VMEM((SEQ, HG_DK), BF16), pltpu.VMEM((SEQ, HG_DK), F32)],
        compiler_params=_cparams(2),
        name="hgrn_prompt",
    )(hin, w_in_t, w_in_t, w_in_t, w_in_t, p["hg_lb_logits"], p["hg_norm_w"].reshape(1, HG_DK),
      jnp.asarray(level))


def _sample_state_step(s_ref, so_ref, o_s, decay_of, q, k, v):
    nb = k.shape[0]
    kt = _rows_to_cols(k)[:, :nb].astype(BF16)
    qb = q.astype(BF16)
    rowid = lax.broadcasted_iota(jnp.int32, v.shape, 0)
    for j in range(nb):
        v_j = jnp.where(rowid == j, v, 0.0).astype(BF16)
        s_new = decay_of(j) * s_ref[j] + jnp.dot(kt, v_j, preferred_element_type=F32)
        so_ref[j] = s_new
        o_s[j:j + 1, :] = jnp.dot(qb, s_new.astype(BF16), preferred_element_type=F32)[j:j + 1, :]


def _hgrn_sample_kernel(layer, q_ref, f_ref, i_ref, g_ref, lbl_ref, nw_ref, s_ref, out_ref, so_ref, o_s):
    q, f, k = _hgrn_gates(layer, q_ref[...], f_ref[...], lbl_ref)
    fc = _rows_to_cols(f)
    _sample_state_step(s_ref, so_ref, o_s, lambda j: fc[:, j:j + 1], q, k, i_ref[...])
    out_ref[...] = _rms_rows(o_s[...]) * nw_ref[...] * jax.nn.silu(g_ref[...])


def hgrn_sample(layer, proj, p, state, state_out):
    cb = lambda off: off // HG_DK
    nb = SAMPLE_BLK
    col = lambda off: pl.BlockSpec((nb, HG_DK), lambda h, i: (i, cb(off) + h))
    st_spec = pl.BlockSpec((None, nb, None, HG_DK, HG_DK), lambda h, i: (layer, i, h, 0, 0))
    args = [proj, proj, proj, proj, p["hg_lb_logits"], p["hg_norm_w"].reshape(1, HG_DK), state]
    in_specs = [col(OFF_HQ), col(OFF_HF), col(OFF_HI), col(OFF_HG),
                pl.BlockSpec((DEPTH, HG_DK), lambda h, i: (0, h)),
                pl.BlockSpec((1, HG_DK), lambda h, i: (0, 0)),
                st_spec]
    aliases = {}
    kern = functools.partial(_hgrn_sample_kernel, layer)
    if state_out is not None:
        args.append(state_out)
        in_specs.append(pl.BlockSpec(memory_space=pl.ANY))
        aliases = {len(args) - 1: 1}
        kern = functools.partial(_drop_alias_arg, kern, 7)
    return pl.pallas_call(
        kern,
        out_shape=(jax.ShapeDtypeStruct((DEC_BATCH, BRANCH_W), F32),
                   jax.ShapeDtypeStruct(state.shape, F32)),
        grid=(HG_HEADS, DEC_BATCH // nb),
        in_specs=in_specs,
        out_specs=(pl.BlockSpec((nb, HG_DK), lambda h, i: (i, h)), st_spec),
        scratch_shapes=[pltpu.VMEM((nb, HG_DK), F32)],
        input_output_aliases=aliases,
        compiler_params=_cparams(2),
        name="hgrn_sample",
    )(*args)


def _drop_alias_arg(kern, pos, *refs):
    return kern(*refs[:pos], *refs[pos + 1:])


def _head_pair(cols, h0, lane_lo):
    return jnp.where(lane_lo, cols[:, h0:h0 + 1], cols[:, h0 + 1:h0 + 2])


def _ssd_prompt_kernel(z_ref, x_ref, bc_ref, dt_ref, cw_ref, cb_ref, dtb_ref, alog_ref, dpar_ref, nw_ref,
                       out_ref, st_ref, cst_ref, cx_s, cbc_s, s_s):
    c = SSD_CHUNK
    ci = pl.program_id(1)

    @pl.when(ci == 0)
    def _():
        cx_s[...] = jnp.zeros_like(cx_s)
        cbc_s[...] = jnp.zeros_like(cbc_s)
        s_s[...] = jnp.zeros_like(s_s)

    cw = cw_ref[...]
    cbias = cb_ref[...]

    def conv_silu(raw, carry_ref, lo, hi):
        xx = jnp.concatenate([carry_ref[...], raw], axis=0)
        y = cbias[:, lo:hi] + cw[CONV_W - 1:CONV_W, lo:hi] * raw
        for d in range(1, CONV_W):
            y = y + cw[CONV_W - 1 - d:CONV_W - d, lo:hi] * pltpu.roll(xx, d, axis=0)[SUBLANES:]
        carry_ref[...] = raw[c - SUBLANES:, :]
        return jax.nn.silu(y)

    x_raw = x_ref[...]
    bc_raw = bc_ref[...]
    xs = conv_silu(x_raw, cx_s, 0, BRANCH_W)
    bc = conv_silu(bc_raw, cbc_s, BRANCH_W, SSD_CONV_DIM)

    dt = jax.nn.softplus(dt_ref[...] + dtb_ref[...])
    a_neg = -jnp.exp(alog_ref[...])
    logd = dt * a_neg
    tri = lax.broadcasted_iota(jnp.int32, (c, c), 0) >= lax.broadcasted_iota(jnp.int32, (c, c), 1)
    b = _cumsum_rows(tri.astype(BF16), logd)
    b_t = b.T
    bl = b[c - 1:c, :]
    e_in = jnp.exp(b)
    w_out = jnp.exp(bl - b)
    e_last = jnp.exp(bl)
    dfull = dpar_ref[...]

    lane_lo =lax.broadcasted_iota(jnp.int32, (c, LANES), 1) < SSD_HEADDIM
    lane_lo_row = lax.broadcasted_iota(jnp.int32, (1, LANES), 1) < SSD_HEADDIM

    ys = []
    for g in range(SSD_GROUPS):
        bm = bc[:, g * SSD_STATE:(g + 1) * SSD_STATE].astype(BF16)
        cm = bc[:, (SSD_GROUPS + g) * SSD_STATE:(SSD_GROUPS + g + 1) * SSD_STATE].astype(BF16)
        gmat = _nt_dot(cm, bm)
        for pp in range(SSD_HEADS // SSD_GROUPS // 2):
            pi = g * (SSD_HEADS // SSD_GROUPS // 2) + pp
            h0 = 2 * pi
            xs_p = xs[:, pi * LANES:(pi + 1) * LANES]
            vdt = xs_p * _head_pair(dt, h0, lane_lo)
            vdt_b = vdt.astype(BF16)
            o_heads = []
            for hh in (h0, h0 + 1):
                diff = b[:, hh:hh + 1] - b_t[hh:hh + 1, :]
                dec = jnp.where(tri, jnp.exp(jnp.where(tri, diff, 0.0)), 0.0)
                o_heads.append(jnp.dot((gmat * dec).astype(BF16), vdt_b, preferred_element_type=F32))
            o_intra = jnp.where(lane_lo, o_heads[0], o_heads[1])
            s_p = s_s[pi]
            o_inter = _head_pair(e_in, h0, lane_lo) * jnp.dot(cm, s_p.astype(BF16), preferred_element_type=F32)
            ys.append(o_intra + o_inter + dfull[:, pi * LANES:(pi + 1) * LANES] * xs_p)
            upd = _tn_dot(bm, (vdt * _head_pair(w_out, h0, lane_lo)).astype(BF16))
            s_s[pi] = s_p * _head_pair(e_last, h0, lane_lo_row) + upd

    y = jnp.concatenate(ys, axis=1) * jax.nn.silu(z_ref[...])
    gw = BRANCH_W // SSD_GROUPS
    nw = nw_ref[...]
    outs = [_rms_rows(y[:, g * gw:(g + 1) * gw]) * nw[:, g * gw:(g + 1) * gw] for g in range(SSD_GROUPS)]
    out_ref[...] = jnp.concatenate(outs, axis=1).astype(out_ref.dtype)

    @pl.when(ci == pl.num_programs(1) - 1)
    def _():
        for pi in range(SSD_HEADS // 2):
            s_t = s_s[pi].T
            st_ref[2 * pi] = s_t[:SSD_HEADDIM, :]
            st_ref[2 * pi + 1] = s_t[SSD_HEADDIM:, :]
        cst_ref[:, 0:BRANCH_W] = x_raw[c - (CONV_W - 1):, :]
        cst_ref[:, BRANCH_W:SSD_CONV_DIM] = bc_raw[c - (CONV_W - 1):, :]


def _pad_lanes(v):
    return jnp.pad(v.astype(F32), (0, LANES - v.shape[0])).reshape(1, LANES)


def _ssd_params(p):
    return (p["ssd_conv_w"], p["ssd_conv_b"].reshape(1, SSD_CONV_DIM), _pad_lanes(p["ssd_dt_bias"]),
            _pad_lanes(p["ssd_a_log"]), jnp.repeat(p["ssd_d"].astype(F32), SSD_HEADDIM).reshape(1, BRANCH_W),
            p["ssd_norm_w"].reshape(1, BRANCH_W))


def ssd_prompt(proj, p):
    c = SSD_CHUNK
    nc = SEQ // c
    const = lambda shape: pl.BlockSpec(shape, lambda b, i: (0, 0))
    rowblk = lambda w, off: pl.BlockSpec((c, w), lambda b, i: (b * nc + i, (off - OFF_SZ) // w))
    return pl.pallas_call(
        _ssd_prompt_kernel,
        out_shape=(jax.ShapeDtypeStruct((BATCH * SEQ, BRANCH_W), BF16),
                   jax.ShapeDtypeStruct((BATCH, SSD_HEADS, SSD_HEADDIM, SSD_STATE), F32),
                   jax.ShapeDtypeStruct((BATCH, CONV_W - 1, SSD_CONV_DIM), F32)),
        grid=(BATCH, nc),
        in_specs=[rowblk(BRANCH_W, OFF_SZ), rowblk(BRANCH_W, OFF_SX), rowblk(SSD_BC, OFF_SBC),
                  rowblk(LANES, OFF_SDT),
                  const((CONV_W, SSD_CONV_DIM)), const((1, SSD_CONV_DIM)), const((1, LANES)),
                  const((1, LANES)), const((1, BRANCH_W)), const((1, BRANCH_W))],
        out_specs=(pl.BlockSpec((c, BRANCH_W), lambda b, i: (b * nc + i, 0)),
                   pl.BlockSpec((None, SSD_HEADS, SSD_HEADDIM, SSD_STATE), lambda b, i: (b, 0, 0, 0)),
                   pl.BlockSpec((None, CONV_W - 1, SSD_CONV_DIM), lambda b, i: (b, 0, 0))),
        scratch_shapes=[pltpu.VMEM((SUBLANES, BRANCH_W), F32), pltpu.VMEM((SUBLANES, SSD_BC), F32),
                        pltpu.VMEM((SSD_HEADS // 2, SSD_STATE, LANES), F32)],
        compiler_params=_cparams(2),
        name="ssd_prompt",
    )(proj, proj, proj, proj, *_ssd_params(p))


def _ssd_sample_kernel(z_ref, x_ref, bc_ref, dt_ref, cw_ref, cb_ref, dtb_ref, alog_ref, dpar_ref, nw_ref,
                       bufx_ref, bufbc_ref, s_ref, out_ref, so_ref, nbx_ref, nbbc_ref, y_s):
    cw = cw_ref[...]
    cbias = cb_ref[...]

    def conv_silu(raw, buf_ref, nbuf_ref, lo, hi):
        y = cbias[:, lo:hi] + cw[CONV_W - 1:CONV_W, lo:hi] * raw
        for j in range(CONV_W - 1):
            y = y + cw[j:j + 1, lo:hi] * buf_ref[j]
        for j in range(CONV_W - 2):
            nbuf_ref[j] = buf_ref[j + 1]
        nbuf_ref[CONV_W - 2] = raw
        return jax.nn.silu(y)

    xs = conv_silu(x_ref[...], bufx_ref, nbx_ref, 0, BRANCH_W)
    bc = conv_silu(bc_ref[...], bufbc_ref, nbbc_ref, BRANCH_W, SSD_CONV_DIM)
    dt = jax.nn.softplus(dt_ref[...] + dtb_ref[...])
    decay = jnp.exp(dt * (-jnp.exp(alog_ref[...])))
    nb = xs.shape[0]
    hpg = SSD_HEADS // SSD_GROUPS
    lane_lo = lax.broadcasted_iota(jnp.int32, (nb, LANES), 1) < SSD_HEADDIM
    for pi in range(SSD_HEADS // 2):
        h0 = 2 * pi
        g = h0 // hpg
        xdt_cols = _rows_to_cols(xs[:, pi * LANES:(pi + 1) * LANES] * _head_pair(dt, h0, lane_lo))
        for e in range(2):
            h = h0 + e
            for j in range(nb):
                xcol = xdt_cols[e * SSD_HEADDIM:(e + 1) * SSD_HEADDIM, j:j + 1]
                brow = bc[j:j + 1, g * SSD_STATE:(g + 1) * SSD_STATE]
                so_ref[j, h] = s_ref[j, h] * decay[j:j + 1, h:h + 1] + xcol * brow
    for g in range(SSD_GROUPS):
        cm = bc[:, (SSD_GROUPS + g) * SSD_STATE:(SSD_GROUPS + g + 1) * SSD_STATE].astype(BF16)
        for j in range(nb):
            s_new = so_ref[j, g * hpg:(g + 1) * hpg].reshape(hpg * SSD_HEADDIM, SSD_STATE)
            y_s[j:j + 1, g * hpg * SSD_HEADDIM:(g + 1) * hpg * SSD_HEADDIM] = _nt_dot(cm, s_new.astype(BF16))[j:j + 1, :]
    y = (y_s[...] + dpar_ref[...] * xs) * jax.nn.silu(z_ref[...])
    gw = BRANCH_W // SSD_GROUPS
    nw = nw_ref[...]
    outs = [_rms_rows(y[:, g * gw:(g + 1) * gw]) * nw[:, g * gw:(g + 1) * gw] for g in range(SSD_GROUPS)]
    out_ref[...] = jnp.concatenate(outs, axis=1)


def ssd_sample(layer, proj, p, state, state_out, buf_t):
    nb = SSD_SAMPLE_BLK
    const = lambda shape: pl.BlockSpec(shape, lambda i: (0,) * len(shape))
    rowblk = lambda w, off: pl.BlockSpec((nb, w), lambda i: (i, off // w))
    st_spec = pl.BlockSpec((None, nb, SSD_HEADS, SSD_HEADDIM, SSD_STATE), lambda i: (layer, i, 0, 0, 0))
    bufx_spec = pl.BlockSpec((CONV_W - 1, nb, BRANCH_W), lambda i: (0, i, 0))
    bufbc_spec = pl.BlockSpec((CONV_W - 1, nb, SSD_BC), lambda i: (0, i, BRANCH_W // SSD_BC))
    args = [proj, proj, proj, proj, *_ssd_params(p), buf_t, buf_t, state]
    in_specs = [rowblk(BRANCH_W, OFF_SZ), rowblk(BRANCH_W, OFF_SX), rowblk(SSD_BC, OFF_SBC), rowblk(LANES, OFF_SDT),
                const((CONV_W, SSD_CONV_DIM)), const((1, SSD_CONV_DIM)), const((1, LANES)), const((1, LANES)),
                const((1, BRANCH_W)), const((1, BRANCH_W)), bufx_spec, bufbc_spec, st_spec]
    aliases = {}
    kern = _ssd_sample_kernel
    if state_out is not None:
        args.append(state_out)
        in_specs.append(pl.BlockSpec(memory_space=pl.ANY))
        aliases = {len(args) - 1: 1}
        kern = functools.partial(_drop_alias_arg, kern, 13)
    out, st, nbx, nbbc = pl.pallas_call(
        kern,
        out_shape=(jax.ShapeDtypeStruct((DEC_BATCH, BRANCH_W), F32),
                   jax.ShapeDtypeStruct(state.shape, F32),
                   jax.ShapeDtypeStruct((CONV_W - 1, DEC_BATCH, BRANCH_W), F32),
                   jax.ShapeDtypeStruct((CONV_W - 1, DEC_BATCH, SSD_BC), F32)),
        grid=(DEC_BATCH // nb,),
        in_specs=in_specs,
        out_specs=(pl.BlockSpec((nb, BRANCH_W), lambda i: (i, 0)), st_spec,
                   pl.BlockSpec((CONV_W - 1, nb, BRANCH_W), lambda i: (0, i, 0)),
                   pl.BlockSpec((CONV_W - 1, nb, SSD_BC), lambda i: (0, i, 0))),
        scratch_shapes=[pltpu.VMEM((nb, BRANCH_W), F32)],
        input_output_aliases=aliases,
        compiler_params=_cparams(1),
        name="ssd_sample",
    )(*args)
    return out, st, jnp.concatenate([nbx, nbbc], axis=-1)


def _ret_prompt_kernel(hin_ref, wq_ref, wk_ref, wv_ref, wg_ref, cos_ref, sin_ref, lg_ref, out_ref, st_ref,
                       q_s, k_s, v_s, o_s):
    c = RET_CHUNK
    t_len = hin_ref.shape[0]
    proj = _project(hin_ref, (wq_ref, wk_ref, wv_ref, wg_ref))
    cos = cos_ref[...]
    sin = sin_ref[...]
    q_s[...] = _rope(proj[:, 0:RET_DK], cos, sin)
    k_s[...] = _rope(proj[:, RET_DK:2 * RET_DK], cos, sin) * RET_DK ** -0.5
    v_s[...] = proj[:, 2 * RET_DK:3 * RET_DK].astype(BF16)
    gate = proj[:, 3 * RET_DK:]
    lg = lg_ref[...]
    lg128 = lg[:, :LANES]
    ti = lax.broadcasted_iota(jnp.int32, (c, c), 0)
    si = lax.broadcasted_iota(jnp.int32, (c, c), 1)
    tri = ti >= si
    dec = jnp.where(tri, jnp.exp(jnp.where(tri, (ti - si).astype(F32) * lg, 0.0)), 0.0)
    tt = lax.broadcasted_iota(jnp.int32, (c, LANES), 0).astype(F32)
    g_in = jnp.exp((tt + 1.0) * lg128)
    g_out = jnp.exp((c - 1.0 - tt) * lg128)
    g_all = jnp.exp(float(c) * lg128)
    s = jnp.zeros((RET_DK, RET_DK), F32)
    for ci in range(t_len // c):
        sl = slice(ci * c, (ci + 1) * c)
        qc = q_s[sl, :]
        kc = k_s[sl, :]
        vb = v_s[sl, :]
        scores = _nt_dot(qc.astype(BF16), kc.astype(BF16)) * dec
        o_s[sl, :] = (jnp.dot(scores.astype(BF16), vb, preferred_element_type=F32)
                      + jnp.dot((qc * g_in).astype(BF16), s.astype(BF16), preferred_element_type=F32))
        s = s * g_all + _tn_dot((kc * g_out).astype(BF16), vb)
    out_ref[...] = (_rms_rows(o_s[...]) * jax.nn.silu(gate)).astype(out_ref.dtype)
    st_ref[...] = s


def _log_gamma_rows(width):
    lg = jnp.log1p(-jnp.exp2(-5.0 - jnp.arange(RET_HEADS, dtype=F32)))
    return jnp.broadcast_to(lg[:, None, None], (RET_HEADS, 1, width))


def ret_prompt(layer, hin, w_in_t, cos, sin):
    tab = pl.BlockSpec((SEQ, RET_DK), lambda b, h: (0, 0))
    ret_rows = tuple(RET_COL0 + off for off in (OFF_RQ, OFF_RK, OFF_RV, OFF_RG))
    return pl.pallas_call(
        _ret_prompt_kernel,
        out_shape=(jax.ShapeDtypeStruct((BATCH * SEQ, BRANCH_W), BF16),
                   jax.ShapeDtypeStruct((BATCH, RET_HEADS, RET_DK, RET_DK), F32)),
        grid=(BATCH, RET_HEADS),
        in_specs=[pl.BlockSpec((SEQ, D_MODEL), lambda b, h: (b, 0))]
                 + _w_in_row_specs(layer, ret_rows, _prompt_unit_rows)
                 + [tab, tab, pl.BlockSpec((None, 1, RET_CHUNK), lambda b, h: (h, 0, 0))],
        out_specs=(pl.BlockSpec((SEQ, RET_DK), lambda b, h: (b, h)),
                   pl.BlockSpec((None, None, RET_DK, RET_DK), lambda b, h: (b, h, 0, 0))),
        scratch_shapes=[pltpu.VMEM((SEQ, RET_DK), F32), pltpu.VMEM((SEQ, RET_DK), F32),
                        pltpu.VMEM((SEQ, RET_DK), BF16), pltpu.VMEM((SEQ, RET_DK), F32)],
        compiler_params=_cparams(2),
        name="ret_prompt",
    )(hin, w_in_t, w_in_t, w_in_t, w_in_t, cos, sin, _log_gamma_rows(RET_CHUNK))


def _ret_sample_kernel(q_ref, k_ref, v_ref, g_ref, cos_ref, sin_ref, lg_ref, s_ref, out_ref, so_ref, o_s):
    cos = cos_ref[0:1, :]
    sin = sin_ref[0:1, :]
    q = _rope(q_ref[...], cos, sin)
    k = _rope(k_ref[...], cos, sin) * RET_DK ** -0.5
    v = v_ref[...]
    gamma = jnp.exp(lg_ref[...])
    _sample_state_step(s_ref, so_ref, o_s, lambda j: gamma, q, k, v)
    out_ref[...] = _rms_rows(o_s[...]) * jax.nn.silu(g_ref[...])


def ret_sample(layer, proj, cos, sin, state, state_out):
    cb = lambda off: off // RET_DK
    nb = SAMPLE_BLK
    col = lambda off: pl.BlockSpec((nb, RET_DK), lambda h, i: (i, cb(off) + h))
    tab = pl.BlockSpec((SUBLANES, RET_DK), lambda h, i: (0, 0))
    st_spec = pl.BlockSpec((None, nb, None, RET_DK, RET_DK), lambda h, i: (layer, i, h, 0, 0))
    args = [proj, proj, proj, proj, cos, sin, _log_gamma_rows(LANES), state]
    in_specs = [col(OFF_RQ), col(OFF_RK), col(OFF_RV), col(OFF_RG), tab, tab,
                pl.BlockSpec((None, 1, LANES), lambda h, i: (h, 0, 0)), st_spec]
    aliases = {}
    kern = _ret_sample_kernel
    if state_out is not None:
        args.append(state_out)
        in_specs.append(pl.BlockSpec(memory_space=pl.ANY))
        aliases = {len(args) - 1: 1}
        kern = functools.partial(_drop_alias_arg, kern, len(args) - 1)
    return pl.pallas_call(
        kern,
        out_shape=(jax.ShapeDtypeStruct((DEC_BATCH, BRANCH_W), F32),
                   jax.ShapeDtypeStruct(state.shape, F32)),
        grid=(RET_HEADS, DEC_BATCH // nb),
        in_specs=in_specs,
        out_specs=(pl.BlockSpec((nb, RET_DK), lambda h, i: (i, h)), st_spec),
        scratch_shapes=[pltpu.VMEM((nb, RET_DK), F32)],
        input_output_aliases=aliases,
        compiler_params=_cparams(2),
        name="ret_sample",
    )(*args)


def kernel(x_prompt, x_sample, state_lru_h, state_lru_conv, state_hgrn, state_ssd, state_ssd_conv, state_ret, state_ffn_conv, g_mix, g_ffn, w_in, lru_conv_w, lru_conv_b, lru_wa, lru_ba, lru_wx, lru_bx, lru_lambda, hg_lb_logits, hg_norm_w, ssd_conv_w, ssd_conv_b, ssd_dt_bias, ssd_a_log, ssd_d, ssd_norm_w, w_branch, w_gate, w_out, ffn_w_up, ffn_w_val, ffn_conv_w, ffn_conv_b, ffn_w_down, g_final):
    xp = x_prompt.reshape(BATCH * SEQ, D_MODEL)
    xs = x_sample.reshape(DEC_BATCH, D_MODEL)

    cos_p, sin_p = rope_tables(SEQ, 0, True)
    cos_s, sin_s = rope_tables(SUBLANES, PAST_LEN, False)

    hp = rmsnorm(xp, g_mix[0], BF16, NORM_TM)
    hs = rmsnorm(xs, g_mix[0], BF16, DEC_BATCH)

    state_ssd_t = jnp.swapaxes(state_ssd, -1, -2)

    w_in_t = jnp.swapaxes(w_in, 1, 2)
    w_gate_b = jnp.transpose(w_gate, (0, 2, 1, 3)).astype(BF16)
    w_out_b = w_out.astype(BF16)
    w_down_b = ffn_w_down.astype(BF16)

    prompt_states, sample_small = [], []
    hg_out = ssd_out = ret_out = None
    for l in range(DEPTH):
        p = {"lru_conv_w": lru_conv_w[l], "lru_conv_b": lru_conv_b[l], "lru_wa": lru_wa[l], "lru_ba": lru_ba[l],
             "lru_wx": lru_wx[l], "lru_bx": lru_bx[l], "lru_lambda": lru_lambda[l],
             "hg_lb_logits": hg_lb_logits, "hg_norm_w": hg_norm_w[l],
             "ssd_conv_w": ssd_conv_w[l], "ssd_conv_b": ssd_conv_b[l], "ssd_dt_bias": ssd_dt_bias[l],
             "ssd_a_log": ssd_a_log[l], "ssd_d": ssd_d[l], "ssd_norm_w": ssd_norm_w[l]}
        last = l == DEPTH - 1
        g_next = g_final if last else g_mix[l + 1]

        out_a, lru_h_p, lru_conv_p = lru_prompt(l, hp, w_in_t, p)
        out_b, hg_p = hgrn_prompt(l, hp, w_in_t, p)
        proj_ssd_p, _ = in_proj_prompt(l, hp, w_in_t, OFF_SZ, SSD_TILES, SSD_TN, SSD_PROJ_TM)
        out_c, ssd_p, ssd_conv_p = ssd_prompt(proj_ssd_p, p)
        out_d, ret_p = ret_prompt(l, hp, w_in_t, cos_p, sin_p)
        merged, w_branch_b = gated_merge_prompt(l, hp, (out_a, out_b, out_c, out_d), w_gate_b, w_branch,
                                                MERGE_TM, MERGE_TN)
        xp, h2 = out_proj_residual_norm(l, merged, w_out_b, xp, g_ffn[l], OUT_PROJ_TM)
        act, ffn_conv_p, w_up_b, w_val_b = ffn_prompt(l, h2, ffn_w_up, ffn_w_val, ffn_conv_w[l], ffn_conv_b[l], FFN_TN)
        res = down_proj_residual_norm(l, act, w_down_b, xp, g_next, DOWN_PROJ_TM, DOWN_PROJ_TK, not last,
                                      F32 if last else BF16)
        if last:
            (yp,) = res
        else:
            xp, hp = res
        prompt_states.append((lru_h_p, lru_conv_p, hg_p, ssd_p, ssd_conv_p, ret_p, ffn_conv_p))

        proj_s = in_proj_sample(l, hs, w_in_t, 0, MAIN_TILES, MAIN_TN)
        proj_ret_s = in_proj_sample(l, hs, w_in_t, RET_COL0, RET_TILES, RET_TN)
        lru_buf_t = jnp.swapaxes(state_lru_conv[l], 0, 1)
        ssd_buf_t = jnp.swapaxes(state_ssd_conv[l], 0, 1)
        s_a, lru_h_s, lru_nbuf = lru_sample(proj_s, p, state_lru_h[l], lru_buf_t)
        s_b, hg_out = hgrn_sample(l, proj_s, p, state_hgrn, hg_out)
        s_c, ssd_out, ssd_nbuf = ssd_sample(l, proj_s, p, state_ssd_t, ssd_out, ssd_buf_t)
        s_d, ret_out = ret_sample(l, proj_ret_s, cos_s, sin_s, state_ret, ret_out)
        merged_s = gated_merge_sample(l, hs, (s_a, s_b, s_c, s_d), w_gate_b, w_branch_b, MERGE_TN)
        xs, h2s = out_proj_residual_norm(l, merged_s, w_out_b, xs, g_ffn[l], DEC_BATCH)
        act_s, ffn_nbuf = ffn_sample(l, h2s, w_up_b, w_val_b, ffn_conv_w[l], ffn_conv_b[l], state_ffn_conv,
                                       FFN_SAMPLE_TN)
        res = down_proj_residual_norm(l, act_s, w_down_b, xs, g_next, DEC_BATCH, DOWN_PROJ_TK, not last,
                                      F32 if last else BF16)
        if last:
            (ys,) = res
        else:
            xs, hs = res
        sample_small.append((lru_h_s, jnp.swapaxes(lru_nbuf, 0, 1), jnp.swapaxes(ssd_nbuf, 0, 1), ffn_nbuf))

    stack_p = lambda i: jnp.stack([st[i] for st in prompt_states], axis=0)
    stack_s = lambda i: jnp.stack([st[i] for st in sample_small], axis=0)
    return (yp.reshape(BATCH, SEQ, D_MODEL), ys.reshape(DEC_BATCH, 1, D_MODEL),
            stack_p(0), stack_s(0), stack_p(1), stack_s(1),
            stack_p(2), hg_out, jnp.swapaxes(stack_p(3), -1, -2), jnp.swapaxes(ssd_out, -1, -2),
            stack_p(4), stack_s(2), stack_p(5), ret_out,
            stack_p(6), stack_s(3))
```

```python
import functools

import numpy as np
import jax
import jax.numpy as jnp
from jax import lax
from jax.experimental import pallas as pl
from jax.experimental.pallas import tpu as pltpu

F32 = jnp.float32
BF16 = jnp.bfloat16

D_MODEL = 2048
BATCH = 4
SEQ = 2048
DEPTH = 2
DEC_BATCH = 128
PAST_LEN = 16384
BRANCH_W = D_MODEL // 2
EPS = 1e-6
LRU_BLOCKS = 8
LRU_BLOCK = BRANCH_W // LRU_BLOCKS
LRU_C = 8.0
CONV_W = 4
HG_HEADS = 8
HG_DK = BRANCH_W // HG_HEADS
SSD_HEADDIM = 64
SSD_HEADS = BRANCH_W // SSD_HEADDIM
SSD_GROUPS = 2
SSD_STATE = 128
SSD_BC = 2 * SSD_GROUPS * SSD_STATE
SSD_CONV_DIM = BRANCH_W + SSD_BC
RET_HEADS = 8
RET_DK = BRANCH_W // RET_HEADS
ROPE_BASE = 10000.0
D_FF = 5632
FFN_CONV_W = 3

V7X_VMEM_BYTES = 64 * 1024 * 1024
VMEM_LIMIT_BYTES = V7X_VMEM_BYTES - 8 * 1024 * 1024
LANES = 128
SUBLANES = 8

N_BRANCH = 4
N_IN = 12816
OFF_XA, OFF_YA = 0, 1024
OFF_HQ, OFF_HF, OFF_HI, OFF_HG = 2048, 3072, 4096, 5120
OFF_SZ, OFF_SX, OFF_SBC = 6144, 7168, 8192
OFF_SDT = 8704
MAIN_TN, MAIN_TILES = 1280, 7
N_MAIN = MAIN_TN * MAIN_TILES
RET_COL0 = OFF_SDT + SSD_HEADS
RET_TN, RET_TILES = 1024, 4
SSD_TN, SSD_TILES = 896, 3
SSD_PROJ_W = SSD_TN * SSD_TILES
OFF_RQ, OFF_RK, OFF_RV, OFF_RG = 0, 1024, 2048, 3072
assert RET_COL0 + RET_TN * RET_TILES == N_IN and OFF_SZ + SSD_PROJ_W >= OFF_SDT + LANES

NORM_TM = 512
SSD_PROJ_TM = 1024
MERGE_TM, MERGE_TN = 1024, 256
OUT_PROJ_TM = 512
FFN_TN = 512
FFN_ROW_CHUNK = 512
DOWN_PROJ_TM = 512
DOWN_PROJ_TK = 2816
FFN_SAMPLE_TN = 512

HG_CHUNK = 128
HG_LEVELS = (1, 2, 4, 8, 16, 32, 64)
SSD_CHUNK = 128
RET_CHUNK = 256
SAMPLE_BLK = 128
SSD_SAMPLE_BLK = 8


def _cparams(n_axes):
    return pltpu.CompilerParams(dimension_semantics=("arbitrary",) * n_axes,
                                vmem_limit_bytes=VMEM_LIMIT_BYTES)


def _rms_rows(x):
    return x * lax.rsqrt(jnp.mean(x * x, axis=-1, keepdims=True) + EPS)


def _shift_rows(x, d, row):
    return jnp.where(row >= d, pltpu.roll(x, d, axis=0), 0.0)


def _nt_dot(a, b):
    return lax.dot_general(a, b, (((1,), (1,)), ((), ())), preferred_element_type=F32)


def _tn_dot(a, b):
    return lax.dot_general(a, b, (((0,), (0,)), ((), ())), preferred_element_type=F32)


def _project(h_ref, w_refs):
    w = jnp.concatenate([w_ref[0].astype(BF16) for w_ref in w_refs], axis=0)
    return _nt_dot(h_ref[...], w)


def _w_in_row_specs(layer, offsets, index_map_for):
    return [pl.BlockSpec((pl.Element(1), pl.Element(LANES), pl.Element(D_MODEL)), index_map_for(layer, off))
            for off in offsets]


def _rows_to_cols(x):
    n = x.shape[0]
    if n < LANES:
        x = jnp.concatenate([x, jnp.zeros((LANES - n, x.shape[1]), x.dtype)], axis=0)
    return x.T


def _norm_kernel(x_ref, g_ref, o_ref):
    o_ref[...] = (_rms_rows(x_ref[...]) * g_ref[...]).astype(o_ref.dtype)


def rmsnorm(x, g, out_dtype, tm):
    m, d = x.shape
    return pl.pallas_call(
        _norm_kernel,
        out_shape=jax.ShapeDtypeStruct((m, d), out_dtype),
        grid=(m // tm,),
        in_specs=[pl.BlockSpec((tm, d), lambda i: (i, 0)),
                  pl.BlockSpec((1, d), lambda i: (0, 0))],
        out_specs=pl.BlockSpec((tm, d), lambda i: (i, 0)),
        compiler_params=_cparams(1),
        name="rmsnorm",
    )(x, g.reshape(1, d))


def _mm_nt_cast_kernel(a_ref, wt_ref, o_ref, wb_ref):
    @pl.when(pl.program_id(1) == 0)
    def _():
        wb_ref[...] = wt_ref[0].astype(BF16)

    o_ref[...] = _nt_dot(a_ref[...], wb_ref[...])


def in_proj_prompt(layer, h, w_in_t, row0, n_tiles, tn, tm):
    m = h.shape[0]
    n = n_tiles * tn
    w_spec = pl.BlockSpec((pl.Element(1), pl.Element(tn), pl.Element(D_MODEL)),
                          lambda j, i: (layer, pl.multiple_of(row0 + j * tn, SUBLANES), 0))
    return pl.pallas_call(
        _mm_nt_cast_kernel,
        out_shape=(jax.ShapeDtypeStruct((m, n), F32), jax.ShapeDtypeStruct((n, D_MODEL), BF16)),
        grid=(n_tiles, m // tm),
        in_specs=[pl.BlockSpec((tm, D_MODEL), lambda j, i: (i, 0)), w_spec],
        out_specs=(pl.BlockSpec((tm, tn), lambda j, i: (i, j)),
                   pl.BlockSpec((tn, D_MODEL), lambda j, i: (j, 0))),
        compiler_params=_cparams(2),
        name="in_proj_prompt",
    )(h, w_in_t)


def _mm_nt_castw_kernel(a_ref, wt_ref, o_ref):
    o_ref[...] = _nt_dot(a_ref[...], wt_ref[0].astype(BF16))


def in_proj_sample(layer, h, w_in_t, row0, n_tiles, tn):
    m = h.shape[0]
    w_spec = pl.BlockSpec((pl.Element(1), pl.Element(tn), pl.Element(D_MODEL)),
                          lambda j: (layer, pl.multiple_of(row0 + j * tn, SUBLANES), 0))
    return pl.pallas_call(
        _mm_nt_castw_kernel,
        out_shape=jax.ShapeDtypeStruct((m, n_tiles * tn), F32),
        grid=(n_tiles,),
        in_specs=[pl.BlockSpec((m, D_MODEL), lambda j: (0, 0)), w_spec],
        out_specs=pl.BlockSpec((m, tn), lambda j: (0, j)),
        compiler_params=_cparams(1),
        name="in_proj_sample",
    )(h, w_in_t)


def _gated_sum(h, br_refs, gate_w, branch_w):
    acc = None
    for k, br_ref in enumerate(br_refs):
        gate = jax.nn.sigmoid(jnp.dot(h, gate_w(k), preferred_element_type=F32))
        br = jnp.dot(br_ref[...].astype(BF16), branch_w(k), preferred_element_type=F32)
        acc = gate * br if acc is None else acc + gate * br
    return acc


def _merge_cast_kernel(h_ref, a_ref, b_ref, c_ref, d_ref, wg_ref, wb_ref, o_ref, wbb_ref):
    @pl.when(pl.program_id(1) == 0)
    def _():
        for k in range(N_BRANCH):
            wbb_ref[k] = wb_ref[k].astype(BF16)

    acc = _gated_sum(h_ref[...], (a_ref, b_ref, c_ref, d_ref), lambda k: wg_ref[k], lambda k: wbb_ref[k])
    o_ref[...] = acc.astype(o_ref.dtype)


def gated_merge_prompt(layer, h, branches, w_gate_b, w_branch, tm, tn):
    m = h.shape[0]
    br_specs = [pl.BlockSpec((tm, BRANCH_W), lambda j, i: (i, 0)) for _ in range(N_BRANCH)]
    return pl.pallas_call(
        _merge_cast_kernel,
        out_shape=(jax.ShapeDtypeStruct((m, D_MODEL), BF16),
                   jax.ShapeDtypeStruct((N_BRANCH, BRANCH_W, D_MODEL), BF16)),
        grid=(D_MODEL // tn, m // tm),
        in_specs=[pl.BlockSpec((tm, D_MODEL), lambda j, i: (i, 0))] + br_specs
                 + [pl.BlockSpec((None, N_BRANCH, D_MODEL, tn), lambda j, i: (layer, 0, 0, j)),
                    pl.BlockSpec((None, N_BRANCH, BRANCH_W, tn), lambda j, i: (layer, 0, 0, j))],
        out_specs=(pl.BlockSpec((tm, tn), lambda j, i: (i, j)),
                   pl.BlockSpec((N_BRANCH, BRANCH_W, tn), lambda j, i: (0, 0, j))),
        compiler_params=_cparams(2),
        name="gated_merge_prompt",
    )(h, *branches, w_gate_b, w_branch)


def _merge_kernel(h_ref, a_ref, b_ref, c_ref, d_ref, wgb_ref, wbb_ref, o_ref):
    acc = _gated_sum(h_ref[...], (a_ref, b_ref, c_ref, d_ref), lambda k: wgb_ref[k], lambda k: wbb_ref[k])
    o_ref[...] = acc.astype(o_ref.dtype)


def gated_merge_sample(layer, h, branches, w_gate_b, w_branch_b, tn):
    m = h.shape[0]
    br_specs = [pl.BlockSpec((m, BRANCH_W), lambda j: (0, 0)) for _ in range(N_BRANCH)]
    return pl.pallas_call(
        _merge_kernel,
        out_shape=jax.ShapeDtypeStruct((m, D_MODEL), BF16),
        grid=(D_MODEL // tn,),
        in_specs=[pl.BlockSpec((m, D_MODEL), lambda j: (0, 0))] + br_specs
                 + [pl.BlockSpec((None, N_BRANCH, D_MODEL, tn), lambda j: (layer, 0, 0, j)),
                    pl.BlockSpec((N_BRANCH, BRANCH_W, tn), lambda j: (0, 0, j))],
        out_specs=pl.BlockSpec((m, tn), lambda j: (0, j)),
        compiler_params=_cparams(1),
        name="gated_merge_sample",
    )(h, *branches, w_gate_b, w_branch_b)


def _out_proj_kernel(m_ref, w_ref, x_ref, g_ref, xo_ref, ho_ref):
    x_new = x_ref[...] + jnp.dot(m_ref[...], w_ref[...], preferred_element_type=F32)
    xo_ref[...] = x_new
    ho_ref[...] = (_rms_rows(x_new) * g_ref[...]).astype(ho_ref.dtype)


def out_proj_residual_norm(layer, merged, w_out, x, g, tm):
    m = x.shape[0]
    return pl.pallas_call(
        _out_proj_kernel,
        out_shape=(jax.ShapeDtypeStruct((m, D_MODEL), F32), jax.ShapeDtypeStruct((m, D_MODEL), BF16)),
        grid=(m // tm,),
        in_specs=[pl.BlockSpec((tm, D_MODEL), lambda i: (i, 0)),
                  pl.BlockSpec((None, D_MODEL, D_MODEL), lambda i: (layer, 0, 0)),
                  pl.BlockSpec((tm, D_MODEL), lambda i: (i, 0)),
                  pl.BlockSpec((1, D_MODEL), lambda i: (0, 0))],
        out_specs=(pl.BlockSpec((tm, D_MODEL), lambda i: (i, 0)),
                   pl.BlockSpec((tm, D_MODEL), lambda i: (i, 0))),
        compiler_params=_cparams(1),
        name="out_proj",
    )(merged, w_out, x, g.reshape(1, D_MODEL))


def _down_proj_kernel(emit_x, a_ref, w_ref, x_ref, g_ref, *refs):
    if emit_x:
        xo_ref, no_ref, acc_ref = refs
    else:
        no_ref, acc_ref = refs
    kk = pl.program_id(1)

    @pl.when(kk == 0)
    def _():
        acc_ref[...] = x_ref[...]

    acc_ref[...] += jnp.dot(a_ref[...], w_ref[...], preferred_element_type=F32)

    @pl.when(kk == pl.num_programs(1) - 1)
    def _():
        x_new = acc_ref[...]
        if emit_x:
            xo_ref[...] = x_new
        no_ref[...] = (_rms_rows(x_new) * g_ref[...]).astype(no_ref.dtype)


def down_proj_residual_norm(layer, a, w_down, x, g, tm, tk, emit_x, norm_dtype):
    m = x.shape[0]
    out_shape = [jax.ShapeDtypeStruct((m, D_MODEL), norm_dtype)]
    out_specs = [pl.BlockSpec((tm, D_MODEL), lambda i, k: (i, 0))]
    if emit_x:
        out_shape = [jax.ShapeDtypeStruct((m, D_MODEL), F32)] + out_shape
        out_specs = [pl.BlockSpec((tm, D_MODEL), lambda i, k: (i, 0))] + out_specs
    return pl.pallas_call(
        functools.partial(_down_proj_kernel, emit_x),
        out_shape=tuple(out_shape),
        grid=(m // tm, D_FF // tk),
        in_specs=[pl.BlockSpec((tm, tk), lambda i, k: (i, k)),
                  pl.BlockSpec((None, tk, D_MODEL), lambda i, k: (layer, k, 0)),
                  pl.BlockSpec((tm, D_MODEL), lambda i, k: (i, 0)),
                  pl.BlockSpec((1, D_MODEL), lambda i, k: (0, 0))],
        out_specs=tuple(out_specs),
        scratch_shapes=[pltpu.VMEM((tm, D_MODEL), F32)],
        compiler_params=_cparams(2),
        name="down_proj",
    )(a, w_down, x, g.reshape(1, D_MODEL))


def _ffn_prompt_kernel(h_ref, wu_ref, wv_ref, cw_ref, cb_ref, a_ref, st_ref, wub_ref, wvb_ref):
    @pl.when(pl.program_id(1) == 0)
    def _():
        wub_ref[...] = wu_ref[...].astype(BF16)
        wvb_ref[...] = wv_ref[...].astype(BF16)

    rc = FFN_ROW_CHUNK
    tn = a_ref.shape[1]
    cw = cw_ref[...]
    cbias = cb_ref[...]
    row8 = lax.broadcasted_iota(jnp.int32, (SUBLANES, tn), 0)
    tail = jnp.zeros((SUBLANES, tn), F32)
    pending = None
    for c in range(h_ref.shape[0] // rc):
        rows = slice(c * rc, (c + 1) * rc)
        hc = h_ref[rows, :]
        u = jnp.dot(hc, wub_ref[...], preferred_element_type=F32)
        if pending is not None:
            prev_rows, g_prev, v_prev = pending
            a_ref[prev_rows, :] = (g_prev * v_prev).astype(a_ref.dtype)
        v = jnp.dot(hc, wvb_ref[...], preferred_element_type=F32)
        uc = cbias + cw[FFN_CONV_W - 1:FFN_CONV_W] * u
        for d in range(1, FFN_CONV_W):
            rolled = pltpu.roll(u, d, axis=0)
            top = jnp.where(row8 >= d, rolled[:SUBLANES], pltpu.roll(tail, d, axis=0))
            uc = uc + cw[FFN_CONV_W - 1 - d:FFN_CONV_W - d] * jnp.concatenate([top, rolled[SUBLANES:]], axis=0)
        pending = (rows, jax.nn.gelu(uc), v)
        tail = u[rc - SUBLANES:, :]
    prev_rows, g_prev, v_prev = pending
    a_ref[prev_rows, :] = (g_prev * v_prev).astype(a_ref.dtype)
    st_ref[...] = tail[SUBLANES - (FFN_CONV_W - 1):, :]


def ffn_prompt(layer, h2, w_up, w_val, conv_w, conv_b, tn):
    wspec = pl.BlockSpec((None, D_MODEL, tn), lambda j, b: (layer, 0, j))
    wbspec = pl.BlockSpec((D_MODEL, tn), lambda j, b: (0, j))
    return pl.pallas_call(
        _ffn_prompt_kernel,
        out_shape=(jax.ShapeDtypeStruct((BATCH * SEQ, D_FF), BF16),
                   jax.ShapeDtypeStruct((BATCH, FFN_CONV_W - 1, D_FF), F32),
                   jax.ShapeDtypeStruct((D_MODEL, D_FF), BF16),
                   jax.ShapeDtypeStruct((D_MODEL, D_FF), BF16)),
        grid=(D_FF // tn, BATCH),
        in_specs=[pl.BlockSpec((SEQ, D_MODEL), lambda j, b: (b, 0)), wspec, wspec,
                  pl.BlockSpec((FFN_CONV_W, tn), lambda j, b: (0, j)),
                  pl.BlockSpec((1, tn), lambda j, b: (0, j))],
        out_specs=(pl.BlockSpec((SEQ, tn), lambda j, b: (b, j)),
                   pl.BlockSpec((None, FFN_CONV_W - 1, tn), lambda j, b: (b, 0, j)),
                   wbspec, wbspec),
        compiler_params=_cparams(2),
        name="ffn_prompt",
    )(h2, w_up, w_val, conv_w, conv_b.reshape(1, D_FF))


def _ffn_sample_kernel(h_ref, wu_ref, wv_ref, cw_ref, cb_ref, buf_ref, a_ref, nb_ref):
    h = h_ref[...]
    u = jnp.dot(h, wu_ref[...], preferred_element_type=F32)
    v = jnp.dot(h, wv_ref[...], preferred_element_type=F32)
    cw = cw_ref[...]
    b0 = buf_ref[:, 0, :]
    b1 = buf_ref[:, 1, :]
    uc = cb_ref[...] + cw[0:1] * b0 + cw[1:2] * b1 + cw[2:3] * u
    a_ref[...] = (jax.nn.gelu(uc) * v).astype(a_ref.dtype)
    nb_ref[:, 0, :] = b1
    nb_ref[:, 1, :] = u


def ffn_sample(layer, h2, w_up, w_val, conv_w, conv_b, buf, tn):
    return pl.pallas_call(
        _ffn_sample_kernel,
        out_shape=(jax.ShapeDtypeStruct((DEC_BATCH, D_FF), BF16),
                   jax.ShapeDtypeStruct((DEC_BATCH, FFN_CONV_W - 1, D_FF), F32)),
        grid=(D_FF // tn,),
        in_specs=[pl.BlockSpec((DEC_BATCH, D_MODEL), lambda j: (0, 0)),
                  pl.BlockSpec((D_MODEL, tn), lambda j: (0, j)),
                  pl.BlockSpec((D_MODEL, tn), lambda j: (0, j)),
                  pl.BlockSpec((FFN_CONV_W, tn), lambda j: (0, j)),
                  pl.BlockSpec((1, tn), lambda j: (0, j)),
                  pl.BlockSpec((None, DEC_BATCH, FFN_CONV_W - 1, tn), lambda j: (layer, 0, 0, j))],
        out_specs=(pl.BlockSpec((DEC_BATCH, tn), lambda j: (0, j)),
                   pl.BlockSpec((DEC_BATCH, FFN_CONV_W - 1, tn), lambda j: (0, 0, j))),
        compiler_params=_cparams(1),
        name="ffn_sample",
    )(h2, w_up, w_val, conv_w, conv_b.reshape(1, D_FF), buf)


def _rope_table_kernel(start, consecutive, freq_ref, sign_ref, cos_ref, sin_ref):
    shape = cos_ref.shape
    if consecutive:
        pos = lax.broadcasted_iota(jnp.int32, shape, 0).astype(F32) + float(start)
    else:
        pos = jnp.full(shape, float(start), F32)
    ang = pos * freq_ref[...]
    cos_ref[...] = jnp.cos(ang)
    sin_ref[...] = sign_ref[...] * jnp.sin(ang)


def rope_tables(n_rows, start, consecutive):
    half = RET_DK // 2
    freqs = ROPE_BASE ** (-jnp.arange(half, dtype=F32) / half)
    freq2 = jnp.concatenate([freqs, freqs]).reshape(1, RET_DK)
    sign = jnp.concatenate([-jnp.ones((half,), F32), jnp.ones((half,), F32)]).reshape(1, RET_DK)
    return pl.pallas_call(
        functools.partial(_rope_table_kernel, start, consecutive),
        out_shape=(jax.ShapeDtypeStruct((n_rows, RET_DK), F32), jax.ShapeDtypeStruct((n_rows, RET_DK), F32)),
        name="rope_tables",
    )(freq2, sign)


def _rope(x, cos, sin_signed):
    return x * cos + pltpu.roll(x, RET_DK // 2, axis=1) * sin_signed


def _lru_gates(conv, wa_ref, ba_ref, wx_ref, bx_ref, lam_ref):
    xb = conv.astype(BF16)
    r = jax.nn.sigmoid(jnp.dot(xb, wa_ref[...].astype(BF16), preferred_element_type=F32) + ba_ref[...])
    i = jax.nn.sigmoid(jnp.dot(xb, wx_ref[...].astype(BF16), preferred_element_type=F32) + bx_ref[...])
    log_a = -LRU_C * r * jax.nn.softplus(-lam_ref[...])
    a = jnp.exp(log_a)
    u = jnp.sqrt(1.0 - a * a) * (i * conv)
    return a, u


def _lru_prompt_kernel(hin_ref, wxa_ref, wya_ref, cw_ref, cb_ref, wa_ref, ba_ref, wx_ref, bx_ref, lam_ref,
                       out_ref, h_ref, conv_ref, ag_s, ug_s):
    proj = _project(hin_ref, (wxa_ref, wya_ref))
    x = proj[:, :LRU_BLOCK]
    ya = proj[:, LRU_BLOCK:]
    t_len = x.shape[0]
    row = lax.broadcasted_iota(jnp.int32, x.shape, 0)
    cw = cw_ref[...]
    conv = cb_ref[...] + cw[CONV_W - 1:CONV_W] * x
    for d in range(1, CONV_W):
        conv = conv + cw[CONV_W - 1 - d:CONV_W - d] * _shift_rows(x, d, row)
    a, u = _lru_gates(conv, wa_ref, ba_ref, wx_ref, bx_ref, lam_ref)
    ng = t_len // SUBLANES
    a3 = a.reshape(ng, SUBLANES, LRU_BLOCK)
    u3 = u.reshape(ng, SUBLANES, LRU_BLOCK)
    sub = lax.broadcasted_iota(jnp.int32, a3.shape, 1)
    d = 1
    while d < SUBLANES:
        keep = sub >= d
        u3 = jnp.where(keep, a3 * pltpu.roll(u3, d, axis=1) + u3, u3)
        a3 = jnp.where(keep, a3 * pltpu.roll(a3, d, axis=1), a3)
        d *= 2
    ag_s[...] = a3.reshape(t_len, LRU_BLOCK)
    ug_s[...] = u3.reshape(t_len, LRU_BLOCK)
    ag = ag_s[pl.ds(SUBLANES - 1, ng, stride=SUBLANES), :]
    ug = ug_s[pl.ds(SUBLANES - 1, ng, stride=SUBLANES), :]
    grow = lax.broadcasted_iota(jnp.int32, ag.shape, 0)
    d = 1
    while d < ng:
        keep = grow >= d
        ug = jnp.where(keep, ag * pltpu.roll(ug, d, axis=0) + ug, ug)
        ag = jnp.where(keep, ag * pltpu.roll(ag, d, axis=0), ag)
        d *= 2
    carry = _shift_rows(ug, 1, grow)
    h3 = a3 * jnp.broadcast_to(carry[:, None, :], a3.shape) + u3
    hs = h3.reshape(t_len, LRU_BLOCK)
    out_ref[...] = (hs * jax.nn.gelu(ya)).astype(out_ref.dtype)
    h_ref[...] = ug[ng - 1:, :]
    conv_ref[...] = x[t_len - (CONV_W - 1):, :]


def _lru_param_specs(n_axes_fn):
    blk3 = lambda shape: pl.BlockSpec(shape, n_axes_fn(lambda n: (n, 0, 0)))
    return [pl.BlockSpec((CONV_W, LRU_BLOCK), n_axes_fn(lambda n: (0, n))),
            pl.BlockSpec((1, LRU_BLOCK), n_axes_fn(lambda n: (0, n))),
            blk3((None, LRU_BLOCK, LRU_BLOCK)), blk3((None, 1, LRU_BLOCK)),
            blk3((None, LRU_BLOCK, LRU_BLOCK)), blk3((None, 1, LRU_BLOCK)),
            blk3((None, 1, LRU_BLOCK))]


def _lru_params(p):
    return (p["lru_conv_w"], p["lru_conv_b"].reshape(1, BRANCH_W),
            p["lru_wa"], p["lru_ba"].reshape(LRU_BLOCKS, 1, LRU_BLOCK),
            p["lru_wx"], p["lru_bx"].reshape(LRU_BLOCKS, 1, LRU_BLOCK),
            p["lru_lambda"].reshape(LRU_BLOCKS, 1, LRU_BLOCK))


def _prompt_unit_rows(layer, off):
    return lambda b, u: (layer, pl.multiple_of(off + u * LANES, SUBLANES), 0)


def lru_prompt(layer, hin, w_in_t, p):
    wrap = lambda f: (lambda b, n: f(n))
    out, h, conv = pl.pallas_call(
        _lru_prompt_kernel,
        out_shape=(jax.ShapeDtypeStruct((BATCH * SEQ, BRANCH_W), BF16),
                   jax.ShapeDtypeStruct((BATCH, 1, BRANCH_W), F32),
                   jax.ShapeDtypeStruct((BATCH, CONV_W - 1, BRANCH_W), F32)),
        grid=(BATCH, LRU_BLOCKS),
        in_specs=[pl.BlockSpec((SEQ, D_MODEL), lambda b, n: (b, 0))]
                 + _w_in_row_specs(layer, (OFF_XA, OFF_YA), _prompt_unit_rows) + _lru_param_specs(wrap),
        out_specs=(pl.BlockSpec((SEQ, LRU_BLOCK), lambda b, n: (b, n)),
                   pl.BlockSpec((None, 1, LRU_BLOCK), lambda b, n: (b, 0, n)),
                   pl.BlockSpec((None, CONV_W - 1, LRU_BLOCK), lambda b, n: (b, 0, n))),
        scratch_shapes=[pltpu.VMEM((SEQ, LRU_BLOCK), F32)] * 2,
        compiler_params=_cparams(2),
        name="lru_prompt",
    )(hin, w_in_t, w_in_t, *_lru_params(p))
    return out, h.reshape(BATCH, BRANCH_W), conv


def _lru_sample_kernel(xa_ref, ya_ref, cw_ref, cb_ref, wa_ref, ba_ref, wx_ref, bx_ref, lam_ref,
                       h0_ref, buf_ref, out_ref, h_ref, nbuf_ref):
    x = xa_ref[...]
    cw = cw_ref[...]
    conv = cb_ref[...] + cw[CONV_W - 1:CONV_W] * x
    for j in range(CONV_W - 1):
        conv = conv + cw[j:j + 1] * buf_ref[j]
    a, u = _lru_gates(conv, wa_ref, ba_ref, wx_ref, bx_ref, lam_ref)
    h = a * h0_ref[...] + u
    out_ref[...] = h * jax.nn.gelu(ya_ref[...])
    h_ref[...] = h
    for j in range(CONV_W - 2):
        nbuf_ref[j] = buf_ref[j + 1]
    nbuf_ref[CONV_W - 2] = x


def lru_sample(proj, p, h0, buf_t):
    cb = lambda off: off // LRU_BLOCK
    wrap = lambda f: f
    return pl.pallas_call(
        _lru_sample_kernel,
        out_shape=(jax.ShapeDtypeStruct((DEC_BATCH, BRANCH_W), F32),
                   jax.ShapeDtypeStruct((DEC_BATCH, BRANCH_W), F32),
                   jax.ShapeDtypeStruct((CONV_W - 1, DEC_BATCH, BRANCH_W), F32)),
        grid=(LRU_BLOCKS,),
        in_specs=[pl.BlockSpec((DEC_BATCH, LRU_BLOCK), lambda n: (0, cb(OFF_XA) + n)),
                  pl.BlockSpec((DEC_BATCH, LRU_BLOCK), lambda n: (0, cb(OFF_YA) + n))]
                 + _lru_param_specs(wrap)
                 + [pl.BlockSpec((DEC_BATCH, LRU_BLOCK), lambda n: (0, n)),
                    pl.BlockSpec((CONV_W - 1, DEC_BATCH, LRU_BLOCK), lambda n: (0, 0, n))],
        out_specs=(pl.BlockSpec((DEC_BATCH, LRU_BLOCK), lambda n: (0, n)),
                   pl.BlockSpec((DEC_BATCH, LRU_BLOCK), lambda n: (0, n)),
                   pl.BlockSpec((CONV_W - 1, DEC_BATCH, LRU_BLOCK), lambda n: (0, 0, n))),
        compiler_params=_cparams(1),
        name="lru_sample",
    )(proj, proj, *_lru_params(p), h0, buf_t)


def _hgrn_lower_bound(layer, logits):
    mx = jnp.max(logits, axis=0, keepdims=True)
    e = jnp.exp(logits - mx)
    ls = e / jnp.sum(e, axis=0, keepdims=True)
    lb = jnp.zeros_like(ls[0:1])
    for i in range(1, layer + 1):
        lb = lb + ls[i:i + 1]
    return lb


def _hgrn_gates(layer, hq, hf, lbl_ref):
    lb = _hgrn_lower_bound(layer, lbl_ref[...])
    q = jax.nn.silu(hq)
    sg = jax.nn.sigmoid(hf)
    f = lb + (1.0 - lb) * sg
    k = (1.0 - lb) * (1.0 - sg)
    return q, f, k


def _hgrn_level_ids():
    c = HG_CHUNK
    t = np.arange(c)[:, None]
    s = np.arange(c)[None, :]
    level = np.zeros((c, c), np.int32)
    for li, m in enumerate(HG_LEVELS):
        same = (t // (2 * m)) == (s // (2 * m))
        level[same & ((t % (2 * m)) >= m) & ((s % (2 * m)) < m)] = li + 1
    level[t == s] = len(HG_LEVELS) + 1
    return level


def _split3_bf16(x):
    hi = x.astype(BF16)
    r1 = x - hi.astype(F32)
    mid = r1.astype(BF16)
    lo = (r1 - mid.astype(F32)).astype(BF16)
    return hi, mid, lo


def _cumsum_rows(tril_b, x):
    return sum(jnp.dot(tril_b, piece, preferred_element_type=F32) for piece in _split3_bf16(x))


def _hgrn_midpoint_factor(b, m):
    c = b.shape[0]
    if 2 * m >= SUBLANES:
        b3 = b.reshape(c // (2 * m), 2 * m, LANES)
        mid = b3[:, m - 1:m, :]
    else:
        b3 = b.reshape(c // SUBLANES, SUBLANES, LANES)
        sub = lax.broadcasted_iota(jnp.int32, b3.shape, 1)
        mid = b3[:, m - 1:m, :]
        for blk in range(1, SUBLANES // (2 * m)):
            lo = blk * 2 * m
            mid = jnp.where(sub >= lo, b3[:, lo + m - 1:lo + m, :], mid)
    return jnp.exp(-jnp.abs(b3 - mid)).reshape(c, LANES)


def _hgrn_prompt_kernel(layer, hin_ref, wq_ref, wf_ref, wi_ref, wg_ref, lbl_ref, nw_ref, lvl_ref,
                        out_ref, st_ref, lf_s, q_s, k_s, v_s, o_s):
    c = HG_CHUNK
    t_len = hin_ref.shape[0]
    proj = _project(hin_ref, (wq_ref, wf_ref, wi_ref, wg_ref))
    q, f, k = _hgrn_gates(layer, proj[:, 0:HG_DK], proj[:, HG_DK:2 * HG_DK], lbl_ref)
    lf_s[...] = jnp.log(f)
    q_s[...] = q
    k_s[...] = k
    v_s[...] = proj[:, 2 * HG_DK:3 * HG_DK].astype(BF16)
    gate = proj[:, 3 * HG_DK:]

    lvl = lvl_ref[...]
    tril_b = (lax.broadcasted_iota(jnp.int32, (c, c), 0) >= lax.broadcasted_iota(jnp.int32, (c, c), 1)).astype(BF16)

    st = jnp.zeros((HG_DK, HG_DK), F32)
    for ci in range(t_len // c):
        sl = slice(ci * c, (ci + 1) * c)
        qc = q_s[sl, :]
        kc = k_s[sl, :]
        vb = v_s[sl, :]
        b = _cumsum_rows(tril_b, lf_s[sl, :])
        att = jnp.where(lvl == len(HG_LEVELS) + 1, _nt_dot(qc.astype(BF16), kc.astype(BF16)), 0.0)
        for li, m in enumerate(HG_LEVELS):
            e = _hgrn_midpoint_factor(b, m)
            a_l = _nt_dot((qc * e).astype(BF16), (kc * e).astype(BF16))
            att = jnp.where(lvl == li + 1, a_l, att)
        o = (jnp.dot(att.astype(BF16), vb, preferred_element_type=F32)
             + _nt_dot((qc * jnp.exp(b)).astype(BF16), st.astype(BF16)))
        o_s[sl, :] = o
        bl = b[c - 1:c, :]
        kdec = (kc * jnp.exp(bl - b)).astype(BF16)
        st = st * jnp.exp(bl) + _tn_dot(vb, kdec)
    o = o_s[...]
    out_ref[...] = (_rms_rows(o) * nw_ref[...] * jax.nn.silu(gate)).astype(out_ref.dtype)
    st_ref[...] = st.T


def hgrn_prompt(layer, hin, w_in_t, p):
    level = _hgrn_level_ids()
    return pl.pallas_call(
        functools.partial(_hgrn_prompt_kernel, layer),
        out_shape=(jax.ShapeDtypeStruct((BATCH * SEQ, BRANCH_W), BF16),
                   jax.ShapeDtypeStruct((BATCH, HG_HEADS, HG_DK, HG_DK), F32)),
        grid=(BATCH, HG_HEADS),
        in_specs=[pl.BlockSpec((SEQ, D_MODEL), lambda b, h: (b, 0))]
                 + _w_in_row_specs(layer, (OFF_HQ, OFF_HF, OFF_HI, OFF_HG), _prompt_unit_rows)
                 + [pl.BlockSpec((DEPTH, HG_DK), lambda b, h: (0, h)),
                    pl.BlockSpec((1, HG_DK), lambda b, h: (0, 0)),
                    pl.BlockSpec(level.shape, lambda b, h: (0, 0))],
        out_specs=(pl.BlockSpec((SEQ, HG_DK), lambda b, h: (b, h)),
                   pl.BlockSpec((None, None, HG_DK, HG_DK), lambda b, h: (b, h, 0, 0))),
        scratch_shapes=[pltpu.VMEM((SEQ, HG_DK), F32)] * 3
                       + [pltpu.VMEM((SEQ, HG_DK), BF16), pltpu.VMEM((SEQ, HG_DK), F32)],
        compiler_params=_cparams(2),
        name="hgrn_prompt",
    )(hin, w_in_t, w_in_t, w_in_t, w_in_t, p["hg_lb_logits"], p["hg_norm_w"].reshape(1, HG_DK),
      jnp.asarray(level))


def _sample_state_step(s_ref, so_ref, o_s, decay_of, q, k, v):
    nb = k.shape[0]
    kt = _rows_to_cols(k)[:, :nb].astype(BF16)
    qb = q.astype(BF16)
    rowid = lax.broadcasted_iota(jnp.int32, v.shape, 0)
    for j in range(nb):
        v_j = jnp.where(rowid == j, v, 0.0).astype(BF16)
        s_new = decay_of(j) * s_ref[j] + jnp.dot(kt, v_j, preferred_element_type=F32)
        so_ref[j] = s_new
        o_s[j:j + 1, :] = jnp.dot(qb, s_new.astype(BF16), preferred_element_type=F32)[j:j + 1, :]


def _hgrn_sample_kernel(layer, q_ref, f_ref, i_ref, g_ref, lbl_ref, nw_ref, s_ref, out_ref, so_ref, o_s):
    q, f, k = _hgrn_gates(layer, q_ref[...], f_ref[...], lbl_ref)
    fc = _rows_to_cols(f)
    _sample_state_step(s_ref, so_ref, o_s, lambda j: fc[:, j:j + 1], q, k, i_ref[...])
    out_ref[...] = _rms_rows(o_s[...]) * nw_ref[...] * jax.nn.silu(g_ref[...])


def hgrn_sample(layer, proj, p, state, state_out):
    cb = lambda off: off // HG_DK
    nb = SAMPLE_BLK
    col = lambda off: pl.BlockSpec((nb, HG_DK), lambda h, i: (i, cb(off) + h))
    st_spec = pl.BlockSpec((None, nb, None, HG_DK, HG_DK), lambda h, i: (layer, i, h, 0, 0))
    args = [proj, proj, proj, proj, p["hg_lb_logits"], p["hg_norm_w"].reshape(1, HG_DK), state]
    in_specs = [col(OFF_HQ), col(OFF_HF), col(OFF_HI), col(OFF_HG),
                pl.BlockSpec((DEPTH, HG_DK), lambda h, i: (0, h)),
                pl.BlockSpec((1, HG_DK), lambda h, i: (0, 0)),
                st_spec]
    aliases = {}
    kern = functools.partial(_hgrn_sample_kernel, layer)
    if state_out is not None:
        args.append(state_out)
        in_specs.append(pl.BlockSpec(memory_space=pl.ANY))
        aliases = {len(args) - 1: 1}
        kern = functools.partial(_drop_alias_arg, kern, 7)
    return pl.pallas_call(
        kern,
        out_shape=(jax.ShapeDtypeStruct((DEC_BATCH, BRANCH_W), F32),
                   jax.ShapeDtypeStruct(state.shape, F32)),
        grid=(HG_HEADS, DEC_BATCH // nb),
        in_specs=in_specs,
        out_specs=(pl.BlockSpec((nb, HG_DK), lambda h, i: (i, h)), st_spec),
        scratch_shapes=[pltpu.VMEM((nb, HG_DK), F32)],
        input_output_aliases=aliases,
        compiler_params=_cparams(2),
        name="hgrn_sample",
    )(*args)


def _drop_alias_arg(kern, pos, *refs):
    return kern(*refs[:pos], *refs[pos + 1:])


def _head_pair(cols, h0, lane_lo):
    return jnp.where(lane_lo, cols[:, h0:h0 + 1], cols[:, h0 + 1:h0 + 2])


def _ssd_prompt_kernel(z_ref, x_ref, bc_ref, dt_ref, cw_ref, cb_ref, dtb_ref, alog_ref, dpar_ref, nw_ref,
                       out_ref, st_ref, cst_ref, cx_s, cbc_s, s_s):
    c = SSD_CHUNK
    ci = pl.program_id(1)

    @pl.when(ci == 0)
    def _():
        cx_s[...] = jnp.zeros_like(cx_s)
        cbc_s[...] = jnp.zeros_like(cbc_s)
        s_s[...] = jnp.zeros_like(s_s)

    cw = cw_ref[...]
    cbias = cb_ref[...]

    def conv_silu(raw, carry_ref, lo, hi):
        xx = jnp.concatenate([carry_ref[...], raw], axis=0)
        y = cbias[:, lo:hi] + cw[CONV_W - 1:CONV_W, lo:hi] * raw
        for d in range(1, CONV_W):
            y = y + cw[CONV_W - 1 - d:CONV_W - d, lo:hi] * pltpu.roll(xx, d, axis=0)[SUBLANES:]
        carry_ref[...] = raw[c - SUBLANES:, :]
        return jax.nn.silu(y)

    x_raw = x_ref[...]
    bc_raw = bc_ref[...]
    xs = conv_silu(x_raw, cx_s, 0, BRANCH_W)
    bc = conv_silu(bc_raw, cbc_s, BRANCH_W, SSD_CONV_DIM)

    dt = jax.nn.softplus(dt_ref[...] + dtb_ref[...])
    a_neg = -jnp.exp(alog_ref[...])
    logd = dt * a_neg
    tri = lax.broadcasted_iota(jnp.int32, (c, c), 0) >= lax.broadcasted_iota(jnp.int32, (c, c), 1)
    b = _cumsum_rows(tri.astype(BF16), logd)
    b_t = b.T
    bl = b[c - 1:c, :]
    e_in = jnp.exp(b)
    w_out = jnp.exp(bl - b)
    e_last = jnp.exp(bl)
    dfull = dpar_ref[...]

    lane_lo =lax.broadcasted_iota(jnp.int32, (c, LANES), 1) < SSD_HEADDIM
    lane_lo_row = lax.broadcasted_iota(jnp.int32, (1, LANES), 1) < SSD_HEADDIM

    ys = []
    for g in range(SSD_GROUPS):
        bm = bc[:, g * SSD_STATE:(g + 1) * SSD_STATE].astype(BF16)
        cm = bc[:, (SSD_GROUPS + g) * SSD_STATE:(SSD_GROUPS + g + 1) * SSD_STATE].astype(BF16)
        gmat = _nt_dot(cm, bm)
        for pp in range(SSD_HEADS // SSD_GROUPS // 2):
            pi = g * (SSD_HEADS // SSD_GROUPS // 2) + pp
            h0 = 2 * pi
            xs_p = xs[:, pi * LANES:(pi + 1) * LANES]
            vdt = xs_p * _head_pair(dt, h0, lane_lo)
            vdt_b = vdt.astype(BF16)
            o_heads = []
            for hh in (h0, h0 + 1):
                diff = b[:, hh:hh + 1] - b_t[hh:hh + 1, :]
                dec = jnp.where(tri, jnp.exp(jnp.where(tri, diff, 0.0)), 0.0)
                o_heads.append(jnp.dot((gmat * dec).astype(BF16), vdt_b, preferred_element_type=F32))
            o_intra = jnp.where(lane_lo, o_heads[0], o_heads[1])
            s_p = s_s[pi]
            o_inter = _head_pair(e_in, h0, lane_lo) * jnp.dot(cm, s_p.astype(BF16), preferred_element_type=F32)
            ys.append(o_intra + o_inter + dfull[:, pi * LANES:(pi + 1) * LANES] * xs_p)
            upd = _tn_dot(bm, (vdt * _head_pair(w_out, h0, lane_lo)).astype(BF16))
            s_s[pi] = s_p * _head_pair(e_last, h0, lane_lo_row) + upd

    y = jnp.concatenate(ys, axis=1) * jax.nn.silu(z_ref[...])
    gw = BRANCH_W // SSD_GROUPS
    nw = nw_ref[...]
    outs = [_rms_rows(y[:, g * gw:(g + 1) * gw]) * nw[:, g * gw:(g + 1) * gw] for g in range(SSD_GROUPS)]
    out_ref[...] = jnp.concatenate(outs, axis=1).astype(out_ref.dtype)

    @pl.when(ci == pl.num_programs(1) - 1)
    def _():
        for pi in range(SSD_HEADS // 2):
            s_t = s_s[pi].T
            st_ref[2 * pi] = s_t[:SSD_HEADDIM, :]
            st_ref[2 * pi + 1] = s_t[SSD_HEADDIM:, :]
        cst_ref[:, 0:BRANCH_W] = x_raw[c - (CONV_W - 1):, :]
        cst_ref[:, BRANCH_W:SSD_CONV_DIM] = bc_raw[c - (CONV_W - 1):, :]


def _pad_lanes(v):
    return jnp.pad(v.astype(F32), (0, LANES - v.shape[0])).reshape(1, LANES)


def _ssd_params(p):
    return (p["ssd_conv_w"], p["ssd_conv_b"].reshape(1, SSD_CONV_DIM), _pad_lanes(p["ssd_dt_bias"]),
            _pad_lanes(p["ssd_a_log"]), jnp.repeat(p["ssd_d"].astype(F32), SSD_HEADDIM).reshape(1, BRANCH_W),
            p["ssd_norm_w"].reshape(1, BRANCH_W))


def ssd_prompt(proj, p):
    c = SSD_CHUNK
    nc = SEQ // c
    const = lambda shape: pl.BlockSpec(shape, lambda b, i: (0, 0))
    rowblk = lambda w, off: pl.BlockSpec((c, w), lambda b, i: (b * nc + i, (off - OFF_SZ) // w))
    return pl.pallas_call(
        _ssd_prompt_kernel,
        out_shape=(jax.ShapeDtypeStruct((BATCH * SEQ, BRANCH_W), BF16),
                   jax.ShapeDtypeStruct((BATCH, SSD_HEADS, SSD_HEADDIM, SSD_STATE), F32),
                   jax.ShapeDtypeStruct((BATCH, CONV_W - 1, SSD_CONV_DIM), F32)),
        grid=(BATCH, nc),
        in_specs=[rowblk(BRANCH_W, OFF_SZ), rowblk(BRANCH_W, OFF_SX), rowblk(SSD_BC, OFF_SBC),
                  rowblk(LANES, OFF_SDT),
                  const((CONV_W, SSD_CONV_DIM)), const((1, SSD_CONV_DIM)), const((1, LANES)),
                  const((1, LANES)), const((1, BRANCH_W)), const((1, BRANCH_W))],
        out_specs=(pl.BlockSpec((c, BRANCH_W), lambda b, i: (b * nc + i, 0)),
                   pl.BlockSpec((None, SSD_HEADS, SSD_HEADDIM, SSD_STATE), lambda b, i: (b, 0, 0, 0)),
                   pl.BlockSpec((None, CONV_W - 1, SSD_CONV_DIM), lambda b, i: (b, 0, 0))),
        scratch_shapes=[pltpu.VMEM((SUBLANES, BRANCH_W), F32), pltpu.VMEM((SUBLANES, SSD_BC), F32),
                        pltpu.VMEM((SSD_HEADS // 2, SSD_STATE, LANES), F32)],
        compiler_params=_cparams(2),
        name="ssd_prompt",
    )(proj, proj, proj, proj, *_ssd_params(p))


def _ssd_sample_kernel(z_ref, x_ref, bc_ref, dt_ref, cw_ref, cb_ref, dtb_ref, alog_ref, dpar_ref, nw_ref,
                       bufx_ref, bufbc_ref, s_ref, out_ref, so_ref, nbx_ref, nbbc_ref, y_s):
    cw = cw_ref[...]
    cbias = cb_ref[...]

    def conv_silu(raw, buf_ref, nbuf_ref, lo, hi):
        y = cbias[:, lo:hi] + cw[CONV_W - 1:CONV_W, lo:hi] * raw
        for j in range(CONV_W - 1):
            y = y + cw[j:j + 1, lo:hi] * buf_ref[j]
        for j in range(CONV_W - 2):
            nbuf_ref[j] = buf_ref[j + 1]
        nbuf_ref[CONV_W - 2] = raw
        return jax.nn.silu(y)

    xs = conv_silu(x_ref[...], bufx_ref, nbx_ref, 0, BRANCH_W)
    bc = conv_silu(bc_ref[...], bufbc_ref, nbbc_ref, BRANCH_W, SSD_CONV_DIM)
    dt = jax.nn.softplus(dt_ref[...] + dtb_ref[...])
    decay = jnp.exp(dt * (-jnp.exp(alog_ref[...])))
    nb = xs.shape[0]
    hpg = SSD_HEADS // SSD_GROUPS
    lane_lo = lax.broadcasted_iota(jnp.int32, (nb, LANES), 1) < SSD_HEADDIM
    for pi in range(SSD_HEADS // 2):
        h0 = 2 * pi
        g = h0 // hpg
        xdt_cols = _rows_to_cols(xs[:, pi * LANES:(pi + 1) * LANES] * _head_pair(dt, h0, lane_lo))
        for e in range(2):
            h = h0 + e
            for j in range(nb):
                xcol = xdt_cols[e * SSD_HEADDIM:(e + 1) * SSD_HEADDIM, j:j + 1]
                brow = bc[j:j + 1, g * SSD_STATE:(g + 1) * SSD_STATE]
                so_ref[j, h] = s_ref[j, h] * decay[j:j + 1, h:h + 1] + xcol * brow
    for g in range(SSD_GROUPS):
        cm = bc[:, (SSD_GROUPS + g) * SSD_STATE:(SSD_GROUPS + g + 1) * SSD_STATE].astype(BF16)
        for j in range(nb):
            s_new = so_ref[j, g * hpg:(g + 1) * hpg].reshape(hpg * SSD_HEADDIM, SSD_STATE)
            y_s[j:j + 1, g * hpg * SSD_HEADDIM:(g + 1) * hpg * SSD_HEADDIM] = _nt_dot(cm, s_new.astype(BF16))[j:j + 1, :]
    y = (y_s[...] + dpar_ref[...] * xs) * jax.nn.silu(z_ref[...])
    gw = BRANCH_W // SSD_GROUPS
    nw = nw_ref[...]
    outs = [_rms_rows(y[:, g * gw:(g + 1) * gw]) * nw[:, g * gw:(g + 1) * gw] for g in range(SSD_GROUPS)]
    out_ref[...] = jnp.concatenate(outs, axis=1)


def ssd_sample(layer, proj, p, state, state_out, buf_t):
    nb = SSD_SAMPLE_BLK
    const = lambda shape: pl.BlockSpec(shape, lambda i: (0,) * len(shape))
    rowblk = lambda w, off: pl.BlockSpec((nb, w), lambda i: (i, off // w))
    st_spec = pl.BlockSpec((None, nb, SSD_HEADS, SSD_HEADDIM, SSD_STATE), lambda i: (layer, i, 0, 0, 0))
    bufx_spec = pl.BlockSpec((CONV_W - 1, nb, BRANCH_W), lambda i: (0, i, 0))
    bufbc_spec = pl.BlockSpec((CONV_W - 1, nb, SSD_BC), lambda i: (0, i, BRANCH_W // SSD_BC))
    args = [proj, proj, proj, proj, *_ssd_params(p), buf_t, buf_t, state]
    in_specs = [rowblk(BRANCH_W, OFF_SZ), rowblk(BRANCH_W, OFF_SX), rowblk(SSD_BC, OFF_SBC), rowblk(LANES, OFF_SDT),
                const((CONV_W, SSD_CONV_DIM)), const((1, SSD_CONV_DIM)), const((1, LANES)), const((1, LANES)),
                const((1, BRANCH_W)), const((1, BRANCH_W)), bufx_spec, bufbc_spec, st_spec]
    aliases = {}
    kern = _ssd_sample_kernel
    if state_out is not None:
        args.append(state_out)
        in_specs.append(pl.BlockSpec(memory_space=pl.ANY))
        aliases = {len(args) - 1: 1}
        kern = functools.partial(_drop_alias_arg, kern, 13)
    out, st, nbx, nbbc = pl.pallas_call(
        kern,
        out_shape=(jax.ShapeDtypeStruct((DEC_BATCH, BRANCH_W), F32),
                   jax.ShapeDtypeStruct(state.shape, F32),
                   jax.ShapeDtypeStruct((CONV_W - 1, DEC_BATCH, BRANCH_W), F32),
                   jax.ShapeDtypeStruct((CONV_W - 1, DEC_BATCH, SSD_BC), F32)),
        grid=(DEC_BATCH // nb,),
        in_specs=in_specs,
        out_specs=(pl.BlockSpec((nb, BRANCH_W), lambda i: (i, 0)), st_spec,
                   pl.BlockSpec((CONV_W - 1, nb, BRANCH_W), lambda i: (0, i, 0)),
                   pl.BlockSpec((CONV_W - 1, nb, SSD_BC), lambda i: (0, i, 0))),
        scratch_shapes=[pltpu.VMEM((nb, BRANCH_W), F32)],
        input_output_aliases=aliases,
        compiler_params=_cparams(1),
        name="ssd_sample",
    )(*args)
    return out, st, jnp.concatenate([nbx, nbbc], axis=-1)


def _ret_prompt_kernel(hin_ref, wq_ref, wk_ref, wv_ref, wg_ref, cos_ref, sin_ref, lg_ref, out_ref, st_ref,
                       q_s, k_s, v_s, o_s):
    c = RET_CHUNK
    t_len = hin_ref.shape[0]
    proj = _project(hin_ref, (wq_ref, wk_ref, wv_ref, wg_ref))
    cos = cos_ref[...]
    sin = sin_ref[...]
    q_s[...] = _rope(proj[:, 0:RET_DK], cos, sin)
    k_s[...] = _rope(proj[:, RET_DK:2 * RET_DK], cos, sin) * RET_DK ** -0.5
    v_s[...] = proj[:, 2 * RET_DK:3 * RET_DK].astype(BF16)
    gate = proj[:, 3 * RET_DK:]
    lg = lg_ref[...]
    lg128 = lg[:, :LANES]
    ti = lax.broadcasted_iota(jnp.int32, (c, c), 0)
    si = lax.broadcasted_iota(jnp.int32, (c, c), 1)
    tri = ti >= si
    dec = jnp.where(tri, jnp.exp(jnp.where(tri, (ti - si).astype(F32) * lg, 0.0)), 0.0)
    tt = lax.broadcasted_iota(jnp.int32, (c, LANES), 0).astype(F32)
    g_in = jnp.exp((tt + 1.0) * lg128)
    g_out = jnp.exp((c - 1.0 - tt) * lg128)
    g_all = jnp.exp(float(c) * lg128)
    s = jnp.zeros((RET_DK, RET_DK), F32)
    for ci in range(t_len // c):
        sl = slice(ci * c, (ci + 1) * c)
        qc = q_s[sl, :]
        kc = k_s[sl, :]
        vb = v_s[sl, :]
        scores = _nt_dot(qc.astype(BF16), kc.astype(BF16)) * dec
        o_s[sl, :] = (jnp.dot(scores.astype(BF16), vb, preferred_element_type=F32)
                      + jnp.dot((qc * g_in).astype(BF16), s.astype(BF16), preferred_element_type=F32))
        s = s * g_all + _tn_dot((kc * g_out).astype(BF16), vb)
    out_ref[...] = (_rms_rows(o_s[...]) * jax.nn.silu(gate)).astype(out_ref.dtype)
    st_ref[...] = s


def _log_gamma_rows(width):
    lg = jnp.log1p(-jnp.exp2(-5.0 - jnp.arange(RET_HEADS, dtype=F32)))
    return jnp.broadcast_to(lg[:, None, None], (RET_HEADS, 1, width))


def ret_prompt(layer, hin, w_in_t, cos, sin):
    tab = pl.BlockSpec((SEQ, RET_DK), lambda b, h: (0, 0))
    ret_rows = tuple(RET_COL0 + off for off in (OFF_RQ, OFF_RK, OFF_RV, OFF_RG))
    return pl.pallas_call(
        _ret_prompt_kernel,
        out_shape=(jax.ShapeDtypeStruct((BATCH * SEQ, BRANCH_W), BF16),
                   jax.ShapeDtypeStruct((BATCH, RET_HEADS, RET_DK, RET_DK), F32)),
        grid=(BATCH, RET_HEADS),
        in_specs=[pl.BlockSpec((SEQ, D_MODEL), lambda b, h: (b, 0))]
                 + _w_in_row_specs(layer, ret_rows, _prompt_unit_rows)
                 + [tab, tab, pl.BlockSpec((None, 1, RET_CHUNK), lambda b, h: (h, 0, 0))],
        out_specs=(pl.BlockSpec((SEQ, RET_DK), lambda b, h: (b, h)),
                   pl.BlockSpec((None, None, RET_DK, RET_DK), lambda b, h: (b, h, 0, 0))),
        scratch_shapes=[pltpu.VMEM((SEQ, RET_DK), F32), pltpu.VMEM((SEQ, RET_DK), F32),
                        pltpu.VMEM((SEQ, RET_DK), BF16), pltpu.VMEM((SEQ, RET_DK), F32)],
        compiler_params=_cparams(2),
        name="ret_prompt",
    )(hin, w_in_t, w_in_t, w_in_t, w_in_t, cos, sin, _log_gamma_rows(RET_CHUNK))


def _ret_sample_kernel(q_ref, k_ref, v_ref, g_ref, cos_ref, sin_ref, lg_ref, s_ref, out_ref, so_ref, o_s):
    cos = cos_ref[0:1, :]
    sin = sin_ref[0:1, :]
    q = _rope(q_ref[...], cos, sin)
    k = _rope(k_ref[...], cos, sin) * RET_DK ** -0.5
    v = v_ref[...]
    gamma = jnp.exp(lg_ref[...])
    _sample_state_step(s_ref, so_ref, o_s, lambda j: gamma, q, k, v)
    out_ref[...] = _rms_rows(o_s[...]) * jax.nn.silu(g_ref[...])


def ret_sample(layer, proj, cos, sin, state, state_out):
    cb = lambda off: off // RET_DK
    nb = SAMPLE_BLK
    col = lambda off: pl.BlockSpec((nb, RET_DK), lambda h, i: (i, cb(off) + h))
    tab = pl.BlockSpec((SUBLANES, RET_DK), lambda h, i: (0, 0))
    st_spec = pl.BlockSpec((None, nb, None, RET_DK, RET_DK), lambda h, i: (layer, i, h, 0, 0))
    args = [proj, proj, proj, proj, cos, sin, _log_gamma_rows(LANES), state]
    in_specs = [col(OFF_RQ), col(OFF_RK), col(OFF_RV), col(OFF_RG), tab, tab,
                pl.BlockSpec((None, 1, LANES), lambda h, i: (h, 0, 0)), st_spec]
    aliases = {}
    kern = _ret_sample_kernel
    if state_out is not None:
        args.append(state_out)
        in_specs.append(pl.BlockSpec(memory_space=pl.ANY))
        aliases = {len(args) - 1: 1}
        kern = functools.partial(_drop_alias_arg, kern, len(args) - 1)
    return pl.pallas_call(
        kern,
        out_shape=(jax.ShapeDtypeStruct((DEC_BATCH, BRANCH_W), F32),
                   jax.ShapeDtypeStruct(state.shape, F32)),
        grid=(RET_HEADS, DEC_BATCH // nb),
        in_specs=in_specs,
        out_specs=(pl.BlockSpec((nb, RET_DK), lambda h, i: (i, h)), st_spec),
        scratch_shapes=[pltpu.VMEM((nb, RET_DK), F32)],
        input_output_aliases=aliases,
        compiler_params=_cparams(2),
        name="ret_sample",
    )(*args)


def kernel(x_prompt, x_sample, state_lru_h, state_lru_conv, state_hgrn, state_ssd, state_ssd_conv, state_ret, state_ffn_conv, g_mix, g_ffn, w_in, lru_conv_w, lru_conv_b, lru_wa, lru_ba, lru_wx, lru_bx, lru_lambda, hg_lb_logits, hg_norm_w, ssd_conv_w, ssd_conv_b, ssd_dt_bias, ssd_a_log, ssd_d, ssd_norm_w, w_branch, w_gate, w_out, ffn_w_up, ffn_w_val, ffn_conv_w, ffn_conv_b, ffn_w_down, g_final):
    xp = x_prompt.reshape(BATCH * SEQ, D_MODEL)
    xs = x_sample.reshape(DEC_BATCH, D_MODEL)

    cos_p, sin_p = rope_tables(SEQ, 0, True)
    cos_s, sin_s = rope_tables(SUBLANES, PAST_LEN, False)

    hp = rmsnorm(xp, g_mix[0], BF16, NORM_TM)
    hs = rmsnorm(xs, g_mix[0], BF16, DEC_BATCH)

    state_ssd_t = jnp.swapaxes(state_ssd, -1, -2)

    w_in_t = jnp.swapaxes(w_in, 1, 2)
    w_gate_b = jnp.transpose(w_gate, (0, 2, 1, 3)).astype(BF16)
    w_out_b = w_out.astype(BF16)
    w_down_b = ffn_w_down.astype(BF16)

    prompt_states, sample_small = [], []
    hg_out = ssd_out = ret_out = None
    for l in range(DEPTH):
        p = {"lru_conv_w": lru_conv_w[l], "lru_conv_b": lru_conv_b[l], "lru_wa": lru_wa[l], "lru_ba": lru_ba[l],
             "lru_wx": lru_wx[l], "lru_bx": lru_bx[l], "lru_lambda": lru_lambda[l],
             "hg_lb_logits": hg_lb_logits, "hg_norm_w": hg_norm_w[l],
             "ssd_conv_w": ssd_conv_w[l], "ssd_conv_b": ssd_conv_b[l], "ssd_dt_bias": ssd_dt_bias[l],
             "ssd_a_log": ssd_a_log[l], "ssd_d": ssd_d[l], "ssd_norm_w": ssd_norm_w[l]}
        last = l == DEPTH - 1
        g_next = g_final if last else g_mix[l + 1]

        out_a, lru_h_p, lru_conv_p = lru_prompt(l, hp, w_in_t, p)
        out_b, hg_p = hgrn_prompt(l, hp, w_in_t, p)
        proj_ssd_p, _ = in_proj_prompt(l, hp, w_in_t, OFF_SZ, SSD_TILES, SSD_TN, SSD_PROJ_TM)
        out_c, ssd_p, ssd_conv_p = ssd_prompt(proj_ssd_p, p)
        out_d, ret_p = ret_prompt(l, hp, w_in_t, cos_p, sin_p)
        merged, w_branch_b = gated_merge_prompt(l, hp, (out_a, out_b, out_c, out_d), w_gate_b, w_branch,
                                                MERGE_TM, MERGE_TN)
        xp, h2 = out_proj_residual_norm(l, merged, w_out_b, xp, g_ffn[l], OUT_PROJ_TM)
        act, ffn_conv_p, w_up_b, w_val_b = ffn_prompt(l, h2, ffn_w_up, ffn_w_val, ffn_conv_w[l], ffn_conv_b[l], FFN_TN)
        res = down_proj_residual_norm(l, act, w_down_b, xp, g_next, DOWN_PROJ_TM, DOWN_PROJ_TK, not last,
                                      F32 if last else BF16)
        if last:
            (yp,) = res
        else:
            xp, hp = res
        prompt_states.append((lru_h_p, lru_conv_p, hg_p, ssd_p, ssd_conv_p, ret_p, ffn_conv_p))

        proj_s = in_proj_sample(l, hs, w_in_t, 0, MAIN_TILES, MAIN_TN)
        proj_ret_s = in_proj_sample(l, hs, w_in_t, RET_COL0, RET_TILES, RET_TN)
        lru_buf_t = jnp.swapaxes(state_lru_conv[l], 0, 1)
        ssd_buf_t = jnp.swapaxes(state_ssd_conv[l], 0, 1)
        s_a, lru_h_s, lru_nbuf = lru_sample(proj_s, p, state_lru_h[l], lru_buf_t)
        s_b, hg_out = hgrn_sample(l, proj_s, p, state_hgrn, hg_out)
        s_c, ssd_out, ssd_nbuf = ssd_sample(l, proj_s, p, state_ssd_t, ssd_out, ssd_buf_t)
        s_d, ret_out = ret_sample(l, proj_ret_s, cos_s, sin_s, state_ret, ret_out)
        merged_s = gated_merge_sample(l, hs, (s_a, s_b, s_c, s_d), w_gate_b, w_branch_b, MERGE_TN)
        xs, h2s = out_proj_residual_norm(l, merged_s, w_out_b, xs, g_ffn[l], DEC_BATCH)
        act_s, ffn_nbuf = ffn_sample(l, h2s, w_up_b, w_val_b, ffn_conv_w[l], ffn_conv_b[l], state_ffn_conv,
                                       FFN_SAMPLE_TN)
        res = down_proj_residual_norm(l, act_s, w_down_b, xs, g_next, DEC_BATCH, DOWN_PROJ_TK, not last,
                                      F32 if last else BF16)
        if last:
            (ys,) = res
        else:
            xs, hs = res
        sample_small.append((lru_h_s, jnp.swapaxes(lru_nbuf, 0, 1), jnp.swapaxes(ssd_nbuf, 0, 1), ffn_nbuf))

    stack_p = lambda i: jnp.stack([st[i] for st in prompt_states], axis=0)
    stack_s = lambda i: jnp.stack([st[i] for st in sample_small], axis=0)
    return (yp.reshape(BATCH, SEQ, D_MODEL), ys.reshape(DEC_BATCH, 1, D_MODEL),
            stack_p(0), stack_s(0), stack_p(1), stack_s(1),
            stack_p(2), hg_out, jnp.swapaxes(stack_p(3), -1, -2), jnp.swapaxes(ssd_out, -1, -2),
            stack_p(4), stack_s(2), stack_p(5), ret_out,
            stack_p(6), stack_s(3))
```

```python
import functools

import numpy as np
import jax
import jax.numpy as jnp
from jax import lax
from jax.experimental import pallas as pl
from jax.experimental.pallas import tpu as pltpu

F32 = jnp.float32
BF16 = jnp.bfloat16

D_MODEL = 2048
BATCH = 4
SEQ = 2048
DEPTH = 2
DEC_BATCH = 128
PAST_LEN = 16384
BRANCH_W = D_MODEL // 2
EPS = 1e-6
LRU_BLOCKS = 8
LRU_BLOCK = BRANCH_W // LRU_BLOCKS
LRU_C = 8.0
CONV_W = 4
HG_HEADS = 8
HG_DK = BRANCH_W // HG_HEADS
SSD_HEADDIM = 64
SSD_HEADS = BRANCH_W // SSD_HEADDIM
SSD_GROUPS = 2
SSD_STATE = 128
SSD_BC = 2 * SSD_GROUPS * SSD_STATE
SSD_CONV_DIM = BRANCH_W + SSD_BC
RET_HEADS = 8
RET_DK = BRANCH_W // RET_HEADS
ROPE_BASE = 10000.0
D_FF = 5632
FFN_CONV_W = 3

V7X_VMEM_BYTES = 64 * 1024 * 1024
VMEM_LIMIT_BYTES = V7X_VMEM_BYTES - 8 * 1024 * 1024
LANES = 128
SUBLANES = 8

N_BRANCH = 4
N_IN = 12816
OFF_XA, OFF_YA = 0, 1024
OFF_HQ, OFF_HF, OFF_HI, OFF_HG = 2048, 3072, 4096, 5120
OFF_SZ, OFF_SX, OFF_SBC = 6144, 7168, 8192
OFF_SDT = 8704
MAIN_TN, MAIN_TILES = 1280, 7
N_MAIN = MAIN_TN * MAIN_TILES
RET_COL0 = OFF_SDT + SSD_HEADS
RET_TN, RET_TILES = 1024, 4
SSD_TN, SSD_TILES = 896, 3
SSD_PROJ_W = SSD_TN * SSD_TILES
OFF_RQ, OFF_RK, OFF_RV, OFF_RG = 0, 1024, 2048, 3072
assert RET_COL0 + RET_TN * RET_TILES == N_IN and OFF_SZ + SSD_PROJ_W >= OFF_SDT + LANES

NORM_TM = 512
SSD_PROJ_TM = 1024
MERGE_TM, MERGE_TN = 512, 512
OUT_PROJ_TM = 512
FFN_TN = 512
FFN_ROW_CHUNK = 512
DOWN_PROJ_TM = 512
DOWN_PROJ_TK = 2816
FFN_SAMPLE_TN = 512

HG_CHUNK = 128
HG_LEVELS = (1, 2, 4, 8, 16, 32, 64)
SSD_CHUNK = 128
RET_CHUNK = 256
SAMPLE_BLK = 128
SSD_SAMPLE_BLK = 8


def _cparams(n_axes):
    return pltpu.CompilerParams(dimension_semantics=("arbitrary",) * n_axes,
                                vmem_limit_bytes=VMEM_LIMIT_BYTES)


def _rms_rows(x):
    return x * lax.rsqrt(jnp.mean(x * x, axis=-1, keepdims=True) + EPS)


def _shift_rows(x, d, row):
    return jnp.where(row >= d, pltpu.roll(x, d, axis=0), 0.0)


def _nt_dot(a, b):
    return lax.dot_general(a, b, (((1,), (1,)), ((), ())), preferred_element_type=F32)


def _tn_dot(a, b):
    return lax.dot_general(a, b, (((0,), (0,)), ((), ())), preferred_element_type=F32)


def _project(h_ref, w_refs):
    w = jnp.concatenate([w_ref[0].astype(BF16) for w_ref in w_refs], axis=0)
    return _nt_dot(h_ref[...], w)


def _w_in_row_specs(layer, offsets, index_map_for):
    return [pl.BlockSpec((pl.Element(1), pl.Element(LANES), pl.Element(D_MODEL)), index_map_for(layer, off))
            for off in offsets]


def _rows_to_cols(x):
    n = x.shape[0]
    if n < LANES:
        x = jnp.concatenate([x, jnp.zeros((LANES - n, x.shape[1]), x.dtype)], axis=0)
    return x.T


def _norm_kernel(x_ref, g_ref, o_ref):
    o_ref[...] = (_rms_rows(x_ref[...]) * g_ref[...]).astype(o_ref.dtype)


def rmsnorm(x, g, out_dtype, tm):
    m, d = x.shape
    return pl.pallas_call(
        _norm_kernel,
        out_shape=jax.ShapeDtypeStruct((m, d), out_dtype),
        grid=(m // tm,),
        in_specs=[pl.BlockSpec((tm, d), lambda i: (i, 0)),
                  pl.BlockSpec((1, d), lambda i: (0, 0))],
        out_specs=pl.BlockSpec((tm, d), lambda i: (i, 0)),
        compiler_params=_cparams(1),
        name="rmsnorm",
    )(x, g.reshape(1, d))


def _mm_nt_cast_kernel(a_ref, wt_ref, o_ref, wb_ref):
    @pl.when(pl.program_id(1) == 0)
    def _():
        wb_ref[...] = wt_ref[0].astype(BF16)

    o_ref[...] = _nt_dot(a_ref[...], wb_ref[...])


def in_proj_prompt(layer, h, w_in_t, row0, n_tiles, tn, tm):
    m = h.shape[0]
    n = n_tiles * tn
    w_spec = pl.BlockSpec((pl.Element(1), pl.Element(tn), pl.Element(D_MODEL)),
                          lambda j, i: (layer, pl.multiple_of(row0 + j * tn, SUBLANES), 0))
    return pl.pallas_call(
        _mm_nt_cast_kernel,
        out_shape=(jax.ShapeDtypeStruct((m, n), F32), jax.ShapeDtypeStruct((n, D_MODEL), BF16)),
        grid=(n_tiles, m // tm),
        in_specs=[pl.BlockSpec((tm, D_MODEL), lambda j, i: (i, 0)), w_spec],
        out_specs=(pl.BlockSpec((tm, tn), lambda j, i: (i, j)),
                   pl.BlockSpec((tn, D_MODEL), lambda j, i: (j, 0))),
        compiler_params=_cparams(2),
        name="in_proj_prompt",
    )(h, w_in_t)


def _mm_nt_castw_kernel(a_ref, wt_ref, o_ref):
    o_ref[...] = _nt_dot(a_ref[...], wt_ref[0].astype(BF16))


def in_proj_sample(layer, h, w_in_t, row0, n_tiles, tn):
    m = h.shape[0]
    w_spec = pl.BlockSpec((pl.Element(1), pl.Element(tn), pl.Element(D_MODEL)),
                          lambda j: (layer, pl.multiple_of(row0 + j * tn, SUBLANES), 0))
    return pl.pallas_call(
        _mm_nt_castw_kernel,
        out_shape=jax.ShapeDtypeStruct((m, n_tiles * tn), F32),
        grid=(n_tiles,),
        in_specs=[pl.BlockSpec((m, D_MODEL), lambda j: (0, 0)), w_spec],
        out_specs=pl.BlockSpec((m, tn), lambda j: (0, j)),
        compiler_params=_cparams(1),
        name="in_proj_sample",
    )(h, w_in_t)


def _gated_sum(h, br_refs, gate_w, branch_w):
    acc = None
    for k, br_ref in enumerate(br_refs):
        gate = jax.nn.sigmoid(jnp.dot(h, gate_w(k), preferred_element_type=F32))
        br = jnp.dot(br_ref[...].astype(BF16), branch_w(k), preferred_element_type=F32)
        acc = gate * br if acc is None else acc + gate * br
    return acc


def _merge_kernel(h_ref, a_ref, b_ref, c_ref, d_ref, wgb_ref, wbb_ref, o_ref):
    acc = _gated_sum(h_ref[...], (a_ref, b_ref, c_ref, d_ref), lambda k: wgb_ref[k], lambda k: wbb_ref[k])
    o_ref[...] = acc.astype(o_ref.dtype)


def gated_merge(layer, h, branches, w_gate_b, w_branch_b, tm, tn):
    m = h.shape[0]
    br_specs = [pl.BlockSpec((tm, BRANCH_W), lambda j, i: (i, 0)) for _ in range(N_BRANCH)]
    return pl.pallas_call(
        _merge_kernel,
        out_shape=jax.ShapeDtypeStruct((m, D_MODEL), BF16),
        grid=(D_MODEL // tn, m // tm),
        in_specs=[pl.BlockSpec((tm, D_MODEL), lambda j, i: (i, 0))] + br_specs
                 + [pl.BlockSpec((None, N_BRANCH, D_MODEL, tn), lambda j, i: (layer, 0, 0, j)),
                    pl.BlockSpec((None, N_BRANCH, BRANCH_W, tn), lambda j, i: (layer, 0, 0, j))],
        out_specs=pl.BlockSpec((tm, tn), lambda j, i: (i, j)),
        compiler_params=_cparams(2),
        name="gated_merge",
    )(h, *branches, w_gate_b, w_branch_b)


def _out_proj_kernel(m_ref, w_ref, x_ref, g_ref, xo_ref, ho_ref):
    x_new = x_ref[...] + jnp.dot(m_ref[...], w_ref[...], preferred_element_type=F32)
    xo_ref[...] = x_new
    ho_ref[...] = (_rms_rows(x_new) * g_ref[...]).astype(ho_ref.dtype)


def out_proj_residual_norm(layer, merged, w_out, x, g, tm):
    m = x.shape[0]
    return pl.pallas_call(
        _out_proj_kernel,
        out_shape=(jax.ShapeDtypeStruct((m, D_MODEL), F32), jax.ShapeDtypeStruct((m, D_MODEL), BF16)),
        grid=(m // tm,),
        in_specs=[pl.BlockSpec((tm, D_MODEL), lambda i: (i, 0)),
                  pl.BlockSpec((None, D_MODEL, D_MODEL), lambda i: (layer, 0, 0)),
                  pl.BlockSpec((tm, D_MODEL), lambda i: (i, 0)),
                  pl.BlockSpec((1, D_MODEL), lambda i: (0, 0))],
        out_specs=(pl.BlockSpec((tm, D_MODEL), lambda i: (i, 0)),
                   pl.BlockSpec((tm, D_MODEL), lambda i: (i, 0))),
        compiler_params=_cparams(1),
        name="out_proj",
    )(merged, w_out, x, g.reshape(1, D_MODEL))


def _down_proj_kernel(emit_x, a_ref, w_ref, x_ref, g_ref, *refs):
    if emit_x:
        xo_ref, no_ref, acc_ref = refs
    else:
        no_ref, acc_ref = refs
    kk = pl.program_id(1)

    @pl.when(kk == 0)
    def _():
        acc_ref[...] = x_ref[...]

    acc_ref[...] += jnp.dot(a_ref[...], w_ref[...], preferred_element_type=F32)

    @pl.when(kk == pl.num_programs(1) - 1)
    def _():
        x_new = acc_ref[...]
        if emit_x:
            xo_ref[...] = x_new
        no_ref[...] = (_rms_rows(x_new) * g_ref[...]).astype(no_ref.dtype)


def down_proj_residual_norm(layer, a, w_down, x, g, tm, tk, emit_x, norm_dtype):
    m = x.shape[0]
    out_shape = [jax.ShapeDtypeStruct((m, D_MODEL), norm_dtype)]
    out_specs = [pl.BlockSpec((tm, D_MODEL), lambda i, k: (i, 0))]
    if emit_x:
        out_shape = [jax.ShapeDtypeStruct((m, D_MODEL), F32)] + out_shape
        out_specs = [pl.BlockSpec((tm, D_MODEL), lambda i, k: (i, 0))] + out_specs
    return pl.pallas_call(
        functools.partial(_down_proj_kernel, emit_x),
        out_shape=tuple(out_shape),
        grid=(m // tm, D_FF // tk),
        in_specs=[pl.BlockSpec((tm, tk), lambda i, k: (i, k)),
                  pl.BlockSpec((None, tk, D_MODEL), lambda i, k: (layer, k, 0)),
                  pl.BlockSpec((tm, D_MODEL), lambda i, k: (i, 0)),
                  pl.BlockSpec((1, D_MODEL), lambda i, k: (0, 0))],
        out_specs=tuple(out_specs),
        scratch_shapes=[pltpu.VMEM((tm, D_MODEL), F32)],
        compiler_params=_cparams(2),
        name="down_proj",
    )(a, w_down, x, g.reshape(1, D_MODEL))


def _ffn_prompt_kernel(h_ref, wu_ref, wv_ref, cw_ref, cb_ref, a_ref, st_ref, wub_ref, wvb_ref):
    @pl.when(pl.program_id(1) == 0)
    def _():
        wub_ref[...] = wu_ref[...].astype(BF16)
        wvb_ref[...] = wv_ref[...].astype(BF16)

    rc = FFN_ROW_CHUNK
    tn = a_ref.shape[1]
    cw = cw_ref[...]
    cbias = cb_ref[...]
    row8 = lax.broadcasted_iota(jnp.int32, (SUBLANES, tn), 0)
    tail = jnp.zeros((SUBLANES, tn), F32)
    pending = None
    for c in range(h_ref.shape[0] // rc):
        rows = slice(c * rc, (c + 1) * rc)
        hc = h_ref[rows, :]
        u = jnp.dot(hc, wub_ref[...], preferred_element_type=F32)
        if pending is not None:
            prev_rows, g_prev, v_prev = pending
            a_ref[prev_rows, :] = (g_prev * v_prev).astype(a_ref.dtype)
        v = jnp.dot(hc, wvb_ref[...], preferred_element_type=F32)
        uc = cbias + cw[FFN_CONV_W - 1:FFN_CONV_W] * u
        for d in range(1, FFN_CONV_W):
            rolled = pltpu.roll(u, d, axis=0)
            top = jnp.where(row8 >= d, rolled[:SUBLANES], pltpu.roll(tail, d, axis=0))
            uc = uc + cw[FFN_CONV_W - 1 - d:FFN_CONV_W - d] * jnp.concatenate([top, rolled[SUBLANES:]], axis=0)
        pending = (rows, jax.nn.gelu(uc), v)
        tail = u[rc - SUBLANES:, :]
    prev_rows, g_prev, v_prev = pending
    a_ref[prev_rows, :] = (g_prev * v_prev).astype(a_ref.dtype)
    st_ref[...] = tail[SUBLANES - (FFN_CONV_W - 1):, :]


def ffn_prompt(layer, h2, w_up, w_val, conv_w, conv_b, tn):
    wspec = pl.BlockSpec((None, D_MODEL, tn), lambda j, b: (layer, 0, j))
    wbspec = pl.BlockSpec((D_MODEL, tn), lambda j, b: (0, j))
    return pl.pallas_call(
        _ffn_prompt_kernel,
        out_shape=(jax.ShapeDtypeStruct((BATCH * SEQ, D_FF), BF16),
                   jax.ShapeDtypeStruct((BATCH, FFN_CONV_W - 1, D_FF), F32),
                   jax.ShapeDtypeStruct((D_MODEL, D_FF), BF16),
                   jax.ShapeDtypeStruct((D_MODEL, D_FF), BF16)),
        grid=(D_FF // tn, BATCH),
        in_specs=[pl.BlockSpec((SEQ, D_MODEL), lambda j, b: (b, 0)), wspec, wspec,
                  pl.BlockSpec((FFN_CONV_W, tn), lambda j, b: (0, j)),
                  pl.BlockSpec((1, tn), lambda j, b: (0, j))],
        out_specs=(pl.BlockSpec((SEQ, tn), lambda j, b: (b, j)),
                   pl.BlockSpec((None, FFN_CONV_W - 1, tn), lambda j, b: (b, 0, j)),
                   wbspec, wbspec),
        compiler_params=_cparams(2),
        name="ffn_prompt",
    )(h2, w_up, w_val, conv_w, conv_b.reshape(1, D_FF))


def _ffn_sample_kernel(h_ref, wu_ref, wv_ref, cw_ref, cb_ref, buf_ref, a_ref, nb_ref):
    h = h_ref[...]
    u = jnp.dot(h, wu_ref[...], preferred_element_type=F32)
    v = jnp.dot(h, wv_ref[...], preferred_element_type=F32)
    cw = cw_ref[...]
    b0 = buf_ref[:, 0, :]
    b1 = buf_ref[:, 1, :]
    uc = cb_ref[...] + cw[0:1] * b0 + cw[1:2] * b1 + cw[2:3] * u
    a_ref[...] = (jax.nn.gelu(uc) * v).astype(a_ref.dtype)
    nb_ref[:, 0, :] = b1
    nb_ref[:, 1, :] = u


def ffn_sample(layer, h2, w_up, w_val, conv_w, conv_b, buf, tn):
    return pl.pallas_call(
        _ffn_sample_kernel,
        out_shape=(jax.ShapeDtypeStruct((DEC_BATCH, D_FF), BF16),
                   jax.ShapeDtypeStruct((DEC_BATCH, FFN_CONV_W - 1, D_FF), F32)),
        grid=(D_FF // tn,),
        in_specs=[pl.BlockSpec((DEC_BATCH, D_MODEL), lambda j: (0, 0)),
                  pl.BlockSpec((D_MODEL, tn), lambda j: (0, j)),
                  pl.BlockSpec((D_MODEL, tn), lambda j: (0, j)),
                  pl.BlockSpec((FFN_CONV_W, tn), lambda j: (0, j)),
                  pl.BlockSpec((1, tn), lambda j: (0, j)),
                  pl.BlockSpec((None, DEC_BATCH, FFN_CONV_W - 1, tn), lambda j: (layer, 0, 0, j))],
        out_specs=(pl.BlockSpec((DEC_BATCH, tn), lambda j: (0, j)),
                   pl.BlockSpec((DEC_BATCH, FFN_CONV_W - 1, tn), lambda j: (0, 0, j))),
        compiler_params=_cparams(1),
        name="ffn_sample",
    )(h2, w_up, w_val, conv_w, conv_b.reshape(1, D_FF), buf)


def _rope_table_kernel(start, consecutive, freq_ref, sign_ref, cos_ref, sin_ref):
    shape = cos_ref.shape
    if consecutive:
        pos = lax.broadcasted_iota(jnp.int32, shape, 0).astype(F32) + float(start)
    else:
        pos = jnp.full(shape, float(start), F32)
    ang = pos * freq_ref[...]
    cos_ref[...] = jnp.cos(ang)
    sin_ref[...] = sign_ref[...] * jnp.sin(ang)


def rope_tables(n_rows, start, consecutive):
    half = RET_DK // 2
    freqs = ROPE_BASE ** (-jnp.arange(half, dtype=F32) / half)
    freq2 = jnp.concatenate([freqs, freqs]).reshape(1, RET_DK)
    sign = jnp.concatenate([-jnp.ones((half,), F32), jnp.ones((half,), F32)]).reshape(1, RET_DK)
    return pl.pallas_call(
        functools.partial(_rope_table_kernel, start, consecutive),
        out_shape=(jax.ShapeDtypeStruct((n_rows, RET_DK), F32), jax.ShapeDtypeStruct((n_rows, RET_DK), F32)),
        name="rope_tables",
    )(freq2, sign)


def _rope(x, cos, sin_signed):
    return x * cos + pltpu.roll(x, RET_DK // 2, axis=1) * sin_signed


def _lru_gates(conv, wa_ref, ba_ref, wx_ref, bx_ref, lam_ref):
    xb = conv.astype(BF16)
    r = jax.nn.sigmoid(jnp.dot(xb, wa_ref[...].astype(BF16), preferred_element_type=F32) + ba_ref[...])
    i = jax.nn.sigmoid(jnp.dot(xb, wx_ref[...].astype(BF16), preferred_element_type=F32) + bx_ref[...])
    log_a = -LRU_C * r * jax.nn.softplus(-lam_ref[...])
    a = jnp.exp(log_a)
    u = jnp.sqrt(1.0 - a * a) * (i * conv)
    return a, u


def _lru_prompt_kernel(hin_ref, wxa_ref, wya_ref, cw_ref, cb_ref, wa_ref, ba_ref, wx_ref, bx_ref, lam_ref,
                       out_ref, h_ref, conv_ref, ag_s, ug_s):
    proj = _project(hin_ref, (wxa_ref, wya_ref))
    x = proj[:, :LRU_BLOCK]
    ya = proj[:, LRU_BLOCK:]
    t_len = x.shape[0]
    row = lax.broadcasted_iota(jnp.int32, x.shape, 0)
    cw = cw_ref[...]
    conv = cb_ref[...] + cw[CONV_W - 1:CONV_W] * x
    for d in range(1, CONV_W):
        conv = conv + cw[CONV_W - 1 - d:CONV_W - d] * _shift_rows(x, d, row)
    a, u = _lru_gates(conv, wa_ref, ba_ref, wx_ref, bx_ref, lam_ref)
    ng = t_len // SUBLANES
    a3 = a.reshape(ng, SUBLANES, LRU_BLOCK)
    u3 = u.reshape(ng, SUBLANES, LRU_BLOCK)
    sub = lax.broadcasted_iota(jnp.int32, a3.shape, 1)
    d = 1
    while d < SUBLANES:
        keep = sub >= d
        u3 = jnp.where(keep, a3 * pltpu.roll(u3, d, axis=1) + u3, u3)
        a3 = jnp.where(keep, a3 * pltpu.roll(a3, d, axis=1), a3)
        d *= 2
    ag_s[...] = a3.reshape(t_len, LRU_BLOCK)
    ug_s[...] = u3.reshape(t_len, LRU_BLOCK)
    ag = ag_s[pl.ds(SUBLANES - 1, ng, stride=SUBLANES), :]
    ug = ug_s[pl.ds(SUBLANES - 1, ng, stride=SUBLANES), :]
    grow = lax.broadcasted_iota(jnp.int32, ag.shape, 0)
    d = 1
    while d < ng:
        keep = grow >= d
        ug = jnp.where(keep, ag * pltpu.roll(ug, d, axis=0) + ug, ug)
        ag = jnp.where(keep, ag * pltpu.roll(ag, d, axis=0), ag)
        d *= 2
    carry = _shift_rows(ug, 1, grow)
    h3 = a3 * jnp.broadcast_to(carry[:, None, :], a3.shape) + u3
    hs = h3.reshape(t_len, LRU_BLOCK)
    out_ref[...] = (hs * jax.nn.gelu(ya)).astype(out_ref.dtype)
    h_ref[...] = ug[ng - 1:, :]
    conv_ref[...] = x[t_len - (CONV_W - 1):, :]


def _lru_param_specs(n_axes_fn):
    blk3 = lambda shape: pl.BlockSpec(shape, n_axes_fn(lambda n: (n, 0, 0)))
    return [pl.BlockSpec((CONV_W, LRU_BLOCK), n_axes_fn(lambda n: (0, n))),
            pl.BlockSpec((1, LRU_BLOCK), n_axes_fn(lambda n: (0, n))),
            blk3((None, LRU_BLOCK, LRU_BLOCK)), blk3((None, 1, LRU_BLOCK)),
            blk3((None, LRU_BLOCK, LRU_BLOCK)), blk3((None, 1, LRU_BLOCK)),
            blk3((None, 1, LRU_BLOCK))]


def _lru_params(p):
    return (p["lru_conv_w"], p["lru_conv_b"].reshape(1, BRANCH_W),
            p["lru_wa"], p["lru_ba"].reshape(LRU_BLOCKS, 1, LRU_BLOCK),
            p["lru_wx"], p["lru_bx"].reshape(LRU_BLOCKS, 1, LRU_BLOCK),
            p["lru_lambda"].reshape(LRU_BLOCKS, 1, LRU_BLOCK))


def _prompt_unit_rows(layer, off):
    return lambda b, u: (layer, pl.multiple_of(off + u * LANES, SUBLANES), 0)


def lru_prompt(layer, hin, w_in_t, p):
    wrap = lambda f: (lambda b, n: f(n))
    out, h, conv = pl.pallas_call(
        _lru_prompt_kernel,
        out_shape=(jax.ShapeDtypeStruct((BATCH * SEQ, BRANCH_W), BF16),
                   jax.ShapeDtypeStruct((BATCH, 1, BRANCH_W), F32),
                   jax.ShapeDtypeStruct((BATCH, CONV_W - 1, BRANCH_W), F32)),
        grid=(BATCH, LRU_BLOCKS),
        in_specs=[pl.BlockSpec((SEQ, D_MODEL), lambda b, n: (b, 0))]
                 + _w_in_row_specs(layer, (OFF_XA, OFF_YA), _prompt_unit_rows) + _lru_param_specs(wrap),
        out_specs=(pl.BlockSpec((SEQ, LRU_BLOCK), lambda b, n: (b, n)),
                   pl.BlockSpec((None, 1, LRU_BLOCK), lambda b, n: (b, 0, n)),
                   pl.BlockSpec((None, CONV_W - 1, LRU_BLOCK), lambda b, n: (b, 0, n))),
        scratch_shapes=[pltpu.VMEM((SEQ, LRU_BLOCK), F32)] * 2,
        compiler_params=_cparams(2),
        name="lru_prompt",
    )(hin, w_in_t, w_in_t, *_lru_params(p))
    return out, h.reshape(BATCH, BRANCH_W), conv


def _lru_sample_kernel(xa_ref, ya_ref, cw_ref, cb_ref, wa_ref, ba_ref, wx_ref, bx_ref, lam_ref,
                       h0_ref, buf_ref, out_ref, h_ref, nbuf_ref):
    x = xa_ref[...]
    cw = cw_ref[...]
    conv = cb_ref[...] + cw[CONV_W - 1:CONV_W] * x
    for j in range(CONV_W - 1):
        conv = conv + cw[j:j + 1] * buf_ref[j]
    a, u = _lru_gates(conv, wa_ref, ba_ref, wx_ref, bx_ref, lam_ref)
    h = a * h0_ref[...] + u
    out_ref[...] = h * jax.nn.gelu(ya_ref[...])
    h_ref[...] = h
    for j in range(CONV_W - 2):
        nbuf_ref[j] = buf_ref[j + 1]
    nbuf_ref[CONV_W - 2] = x


def lru_sample(proj, p, h0, buf_t):
    cb = lambda off: off // LRU_BLOCK
    wrap = lambda f: f
    return pl.pallas_call(
        _lru_sample_kernel,
        out_shape=(jax.ShapeDtypeStruct((DEC_BATCH, BRANCH_W), F32),
                   jax.ShapeDtypeStruct((DEC_BATCH, BRANCH_W), F32),
                   jax.ShapeDtypeStruct((CONV_W - 1, DEC_BATCH, BRANCH_W), F32)),
        grid=(LRU_BLOCKS,),
        in_specs=[pl.BlockSpec((DEC_BATCH, LRU_BLOCK), lambda n: (0, cb(OFF_XA) + n)),
                  pl.BlockSpec((DEC_BATCH, LRU_BLOCK), lambda n: (0, cb(OFF_YA) + n))]
                 + _lru_param_specs(wrap)
                 + [pl.BlockSpec((DEC_BATCH, LRU_BLOCK), lambda n: (0, n)),
                    pl.BlockSpec((CONV_W - 1, DEC_BATCH, LRU_BLOCK), lambda n: (0, 0, n))],
        out_specs=(pl.BlockSpec((DEC_BATCH, LRU_BLOCK), lambda n: (0, n)),
                   pl.BlockSpec((DEC_BATCH, LRU_BLOCK), lambda n: (0, n)),
                   pl.BlockSpec((CONV_W - 1, DEC_BATCH, LRU_BLOCK), lambda n: (0, 0, n))),
        compiler_params=_cparams(1),
        name="lru_sample",
    )(proj, proj, *_lru_params(p), h0, buf_t)


def _hgrn_lower_bound(layer, logits):
    mx = jnp.max(logits, axis=0, keepdims=True)
    e = jnp.exp(logits - mx)
    ls = e / jnp.sum(e, axis=0, keepdims=True)
    lb = jnp.zeros_like(ls[0:1])
    for i in range(1, layer + 1):
        lb = lb + ls[i:i + 1]
    return lb


def _hgrn_gates(layer, hq, hf, lbl_ref):
    lb = _hgrn_lower_bound(layer, lbl_ref[...])
    q = jax.nn.silu(hq)
    sg = jax.nn.sigmoid(hf)
    f = lb + (1.0 - lb) * sg
    k = (1.0 - lb) * (1.0 - sg)
    return q, f, k


def _hgrn_level_ids():
    c = HG_CHUNK
    t = np.arange(c)[:, None]
    s = np.arange(c)[None, :]
    level = np.zeros((c, c), np.int32)
    for li, m in enumerate(HG_LEVELS):
        same = (t // (2 * m)) == (s // (2 * m))
        level[same & ((t % (2 * m)) >= m) & ((s % (2 * m)) < m)] = li + 1
    level[t == s] = len(HG_LEVELS) + 1
    return level


def _split3_bf16(x):
    hi = x.astype(BF16)
    r1 = x - hi.astype(F32)
    mid = r1.astype(BF16)
    lo = (r1 - mid.astype(F32)).astype(BF16)
    return hi, mid, lo


def _cumsum_rows(tril_b, x):
    return sum(jnp.dot(tril_b, piece, preferred_element_type=F32) for piece in _split3_bf16(x))


def _hgrn_midpoint_factor(b, m):
    c = b.shape[0]
    if 2 * m >= SUBLANES:
        b3 = b.reshape(c // (2 * m), 2 * m, LANES)
        mid = b3[:, m - 1:m, :]
    else:
        b3 = b.reshape(c // SUBLANES, SUBLANES, LANES)
        sub = lax.broadcasted_iota(jnp.int32, b3.shape, 1)
        mid = b3[:, m - 1:m, :]
        for blk in range(1, SUBLANES // (2 * m)):
            lo = blk * 2 * m
            mid = jnp.where(sub >= lo, b3[:, lo + m - 1:lo + m, :], mid)
    return jnp.exp(-jnp.abs(b3 - mid)).reshape(c, LANES)


def _hgrn_prompt_kernel(layer, hin_ref, wq_ref, wf_ref, wi_ref, wg_ref, lbl_ref, nw_ref, lvl_ref,
                        out_ref, st_ref, lf_s, q_s, k_s, v_s, o_s):
    c = HG_CHUNK
    t_len = hin_ref.shape[0]
    proj = _project(hin_ref, (wq_ref, wf_ref, wi_ref, wg_ref))
    q, f, k = _hgrn_gates(layer, proj[:, 0:HG_DK], proj[:, HG_DK:2 * HG_DK], lbl_ref)
    lf_s[...] = jnp.log(f)
    q_s[...] = q
    k_s[...] = k
    v_s[...] = proj[:, 2 * HG_DK:3 * HG_DK].astype(BF16)
    gate = proj[:, 3 * HG_DK:]

    lvl = lvl_ref[...]
    tril_b = (lax.broadcasted_iota(jnp.int32, (c, c), 0) >= lax.broadcasted_iota(jnp.int32, (c, c), 1)).astype(BF16)

    st = jnp.zeros((HG_DK, HG_DK), F32)
    for ci in range(t_len // c):
        sl = slice(ci * c, (ci + 1) * c)
        qc = q_s[sl, :]
        kc = k_s[sl, :]
        vb = v_s[sl, :]
        b = _cumsum_rows(tril_b, lf_s[sl, :])
        att = jnp.where(lvl == len(HG_LEVELS) + 1, _nt_dot(qc.astype(BF16), kc.astype(BF16)), 0.0)
        for li, m in enumerate(HG_LEVELS):
            e = _hgrn_midpoint_factor(b, m)
            a_l = _nt_dot((qc * e).astype(BF16), (kc * e).astype(BF16))
            att = jnp.where(lvl == li + 1, a_l, att)
        o = (jnp.dot(att.astype(BF16), vb, preferred_element_type=F32)
             + _nt_dot((qc * jnp.exp(b)).astype(BF16), st.astype(BF16)))
        o_s[sl, :] = o
        bl = b[c - 1:c, :]
        kdec = (kc * jnp.exp(bl - b)).astype(BF16)
        st = st * jnp.exp(bl) + _tn_dot(vb, kdec)
    o = o_s[...]
    out_ref[...] = (_rms_rows(o) * nw_ref[...] * jax.nn.silu(gate)).astype(out_ref.dtype)
    st_ref[...] = st.T


def hgrn_prompt(layer, hin, w_in_t, p):
    level = _hgrn_level_ids()
    return pl.pallas_call(
        functools.partial(_hgrn_prompt_kernel, layer),
        out_shape=(jax.ShapeDtypeStruct((BATCH * SEQ, BRANCH_W), BF16),
                   jax.ShapeDtypeStruct((BATCH, HG_HEADS, HG_DK, HG_DK), F32)),
        grid=(BATCH, HG_HEADS),
        in_specs=[pl.BlockSpec((SEQ, D_MODEL), lambda b, h: (b, 0))]
                 + _w_in_row_specs(layer, (OFF_HQ, OFF_HF, OFF_HI, OFF_HG), _prompt_unit_rows)
                 + [pl.BlockSpec((DEPTH, HG_DK), lambda b, h: (0, h)),
                    pl.BlockSpec((1, HG_DK), lambda b, h: (0, 0)),
                    pl.BlockSpec(level.shape, lambda b, h: (0, 0))],
        out_specs=(pl.BlockSpec((SEQ, HG_DK), lambda b, h: (b, h)),
                   pl.BlockSpec((None, None, HG_DK, HG_DK), lambda b, h: (b, h, 0, 0))),
        scratch_shapes=[pltpu.VMEM((SEQ, HG_DK), F32)] * 3
                       + [pltpu.VMEM((SEQ, HG_DK), BF16), pltpu.VMEM((SEQ, HG_DK), F32)],
        compiler_params=_cparams(2),
        name="hgrn_prompt",
    )(hin, w_in_t, w_in_t, w_in_t, w_in_t, p["hg_lb_logits"], p["hg_norm_w"].reshape(1, HG_DK),
      jnp.asarray(level))


def _sample_state_step(s_ref, so_ref, o_s, decay_of, q, k, v):
    nb = k.shape[0]
    kt = _rows_to_cols(k)[:, :nb].astype(BF16)
    qb = q.astype(BF16)
    rowid = lax.broadcasted_iota(jnp.int32, v.shape, 0)
    for j in range(nb):
        v_j = jnp.where(rowid == j, v, 0.0).astype(BF16)
        s_new = decay_of(j) * s_ref[j] + jnp.dot(kt, v_j, preferred_element_type=F32)
        so_ref[j] = s_new
        o_s[j:j + 1, :] = jnp.dot(qb, s_new.astype(BF16), preferred_element_type=F32)[j:j + 1, :]


def _hgrn_sample_kernel(layer, q_ref, f_ref, i_ref, g_ref, lbl_ref, nw_ref, s_ref, out_ref, so_ref, o_s):
    q, f, k = _hgrn_gates(layer, q_ref[...], f_ref[...], lbl_ref)
    fc = _rows_to_cols(f)
    _sample_state_step(s_ref, so_ref, o_s, lambda j: fc[:, j:j + 1], q, k, i_ref[...])
    out_ref[...] = _rms_rows(o_s[...]) * nw_ref[...] * jax.nn.silu(g_ref[...])


def hgrn_sample(layer, proj, p, state, state_out):
    cb = lambda off: off // HG_DK
    nb = SAMPLE_BLK
    col = lambda off: pl.BlockSpec((nb, HG_DK), lambda h, i: (i, cb(off) + h))
    st_spec = pl.BlockSpec((None, nb, None, HG_DK, HG_DK), lambda h, i: (layer, i, h, 0, 0))
    args = [proj, proj, proj, proj, p["hg_lb_logits"], p["hg_norm_w"].reshape(1, HG_DK), state]
    in_specs = [col(OFF_HQ), col(OFF_HF), col(OFF_HI), col(OFF_HG),
                pl.BlockSpec((DEPTH, HG_DK), lambda h, i: (0, h)),
                pl.BlockSpec((1, HG_DK), lambda h, i: (0, 0)),
                st_spec]
    aliases = {}
    kern = functools.partial(_hgrn_sample_kernel, layer)
    if state_out is not None:
        args.append(state_out)
        in_specs.append(pl.BlockSpec(memory_space=pl.ANY))
        aliases = {len(args) - 1: 1}
        kern = functools.partial(_drop_alias_arg, kern, 7)
    return pl.pallas_call(
        kern,
        out_shape=(jax.ShapeDtypeStruct((DEC_BATCH, BRANCH_W), F32),
                   jax.ShapeDtypeStruct(state.shape, F32)),
        grid=(HG_HEADS, DEC_BATCH // nb),
        in_specs=in_specs,
        out_specs=(pl.BlockSpec((nb, HG_DK), lambda h, i: (i, h)), st_spec),
        scratch_shapes=[pltpu.VMEM((nb, HG_DK), F32)],
        input_output_aliases=aliases,
        compiler_params=_cparams(2),
        name="hgrn_sample",
    )(*args)


def _drop_alias_arg(kern, pos, *refs):
    return kern(*refs[:pos], *refs[pos + 1:])


def _head_pair(cols, h0, lane_lo):
    return jnp.where(lane_lo, cols[:, h0:h0 + 1], cols[:, h0 + 1:h0 + 2])


def _ssd_prompt_kernel(z_ref, x_ref, bc_ref, dt_ref, cw_ref, cb_ref, dtb_ref, alog_ref, dpar_ref, nw_ref,
                       out_ref, st_ref, cst_ref, cx_s, cbc_s, s_s):
    c = SSD_CHUNK
    ci = pl.program_id(1)

    @pl.when(ci == 0)
    def _():
        cx_s[...] = jnp.zeros_like(cx_s)
        cbc_s[...] = jnp.zeros_like(cbc_s)
        s_s[...] = jnp.zeros_like(s_s)

    cw = cw_ref[...]
    cbias = cb_ref[...]

    def conv_silu(raw, carry_ref, lo, hi):
        xx = jnp.concatenate([carry_ref[...], raw], axis=0)
        y = cbias[:, lo:hi] + cw[CONV_W - 1:CONV_W, lo:hi] * raw
        for d in range(1, CONV_W):
            y = y + cw[CONV_W - 1 - d:CONV_W - d, lo:hi] * pltpu.roll(xx, d, axis=0)[SUBLANES:]
        carry_ref[...] = raw[c - SUBLANES:, :]
        return jax.nn.silu(y)

    x_raw = x_ref[...]
    bc_raw = bc_ref[...]
    xs = conv_silu(x_raw, cx_s, 0, BRANCH_W)
    bc = conv_silu(bc_raw, cbc_s, BRANCH_W, SSD_CONV_DIM)

    dt = jax.nn.softplus(dt_ref[...] + dtb_ref[...])
    a_neg = -jnp.exp(alog_ref[...])
    logd = dt * a_neg
    tri = lax.broadcasted_iota(jnp.int32, (c, c), 0) >= lax.broadcasted_iota(jnp.int32, (c, c), 1)
    b = _cumsum_rows(tri.astype(BF16), logd)
    b_t = b.T
    bl = b[c - 1:c, :]
    e_in = jnp.exp(b)
    w_out = jnp.exp(bl - b)
    e_last = jnp.exp(bl)
    dfull = dpar_ref[...]

    lane_lo =lax.broadcasted_iota(jnp.int32, (c, LANES), 1) < SSD_HEADDIM
    lane_lo_row = lax.broadcasted_iota(jnp.int32, (1, LANES), 1) < SSD_HEADDIM

    ys = []
    for g in range(SSD_GROUPS):
        bm = bc[:, g * SSD_STATE:(g + 1) * SSD_STATE].astype(BF16)
        cm = bc[:, (SSD_GROUPS + g) * SSD_STATE:(SSD_GROUPS + g + 1) * SSD_STATE].astype(BF16)
        gmat = _nt_dot(cm, bm)
        for pp in range(SSD_HEADS // SSD_GROUPS // 2):
            pi = g * (SSD_HEADS // SSD_GROUPS // 2) + pp
            h0 = 2 * pi
            xs_p = xs[:, pi * LANES:(pi + 1) * LANES]
            vdt = xs_p * _head_pair(dt, h0, lane_lo)
            vdt_b = vdt.astype(BF16)
            o_heads = []
            for hh in (h0, h0 + 1):
                diff = b[:, hh:hh + 1] - b_t[hh:hh + 1, :]
                dec = jnp.where(tri, jnp.exp(jnp.where(tri, diff, 0.0)), 0.0)
                o_heads.append(jnp.dot((gmat * dec).astype(BF16), vdt_b, preferred_element_type=F32))
            o_intra = jnp.where(lane_lo, o_heads[0], o_heads[1])
            s_p = s_s[pi]
            o_inter = _head_pair(e_in, h0, lane_lo) * jnp.dot(cm, s_p.astype(BF16), preferred_element_type=F32)
            ys.append(o_intra + o_inter + dfull[:, pi * LANES:(pi + 1) * LANES] * xs_p)
            upd = _tn_dot(bm, (vdt * _head_pair(w_out, h0, lane_lo)).astype(BF16))
            s_s[pi] = s_p * _head_pair(e_last, h0, lane_lo_row) + upd

    y = jnp.concatenate(ys, axis=1) * jax.nn.silu(z_ref[...])
    gw = BRANCH_W // SSD_GROUPS
    nw = nw_ref[...]
    outs = [_rms_rows(y[:, g * gw:(g + 1) * gw]) * nw[:, g * gw:(g + 1) * gw] for g in range(SSD_GROUPS)]
    out_ref[...] = jnp.concatenate(outs, axis=1).astype(out_ref.dtype)

    @pl.when(ci == pl.num_programs(1) - 1)
    def _():
        for pi in range(SSD_HEADS // 2):
            s_t = s_s[pi].T
            st_ref[2 * pi] = s_t[:SSD_HEADDIM, :]
            st_ref[2 * pi + 1] = s_t[SSD_HEADDIM:, :]
        cst_ref[:, 0:BRANCH_W] = x_raw[c - (CONV_W - 1):, :]
        cst_ref[:, BRANCH_W:SSD_CONV_DIM] = bc_raw[c - (CONV_W - 1):, :]


def _pad_lanes(v):
    return jnp.pad(v.astype(F32), (0, LANES - v.shape[0])).reshape(1, LANES)


def _ssd_params(p):
    return (p["ssd_conv_w"], p["ssd_conv_b"].reshape(1, SSD_CONV_DIM), _pad_lanes(p["ssd_dt_bias"]),
            _pad_lanes(p["ssd_a_log"]), jnp.repeat(p["ssd_d"].astype(F32), SSD_HEADDIM).reshape(1, BRANCH_W),
            p["ssd_norm_w"].reshape(1, BRANCH_W))


def ssd_prompt(proj, p):
    c = SSD_CHUNK
    nc = SEQ // c
    const = lambda shape: pl.BlockSpec(shape, lambda b, i: (0, 0))
    rowblk = lambda w, off: pl.BlockSpec((c, w), lambda b, i: (b * nc + i, (off - OFF_SZ) // w))
    return pl.pallas_call(
        _ssd_prompt_kernel,
        out_shape=(jax.ShapeDtypeStruct((BATCH * SEQ, BRANCH_W), BF16),
                   jax.ShapeDtypeStruct((BATCH, SSD_HEADS, SSD_HEADDIM, SSD_STATE), F32),
                   jax.ShapeDtypeStruct((BATCH, CONV_W - 1, SSD_CONV_DIM), F32)),
        grid=(BATCH, nc),
        in_specs=[rowblk(BRANCH_W, OFF_SZ), rowblk(BRANCH_W, OFF_SX), rowblk(SSD_BC, OFF_SBC),
                  rowblk(LANES, OFF_SDT),
                  const((CONV_W, SSD_CONV_DIM)), const((1, SSD_CONV_DIM)), const((1, LANES)),
                  const((1, LANES)), const((1, BRANCH_W)), const((1, BRANCH_W))],
        out_specs=(pl.BlockSpec((c, BRANCH_W), lambda b, i: (b * nc + i, 0)),
                   pl.BlockSpec((None, SSD_HEADS, SSD_HEADDIM, SSD_STATE), lambda b, i: (b, 0, 0, 0)),
                   pl.BlockSpec((None, CONV_W - 1, SSD_CONV_DIM), lambda b, i: (b, 0, 0))),
        scratch_shapes=[pltpu.VMEM((SUBLANES, BRANCH_W), F32), pltpu.VMEM((SUBLANES, SSD_BC), F32),
                        pltpu.VMEM((SSD_HEADS // 2, SSD_STATE, LANES), F32)],
        compiler_params=_cparams(2),
        name="ssd_prompt",
    )(proj, proj, proj, proj, *_ssd_params(p))


def _ssd_sample_kernel(z_ref, x_ref, bc_ref, dt_ref, cw_ref, cb_ref, dtb_ref, alog_ref, dpar_ref, nw_ref,
                       bufx_ref, bufbc_ref, s_ref, out_ref, so_ref, nbx_ref, nbbc_ref, y_s):
    cw = cw_ref[...]
    cbias = cb_ref[...]

    def conv_silu(raw, buf_ref, nbuf_ref, lo, hi):
        y = cbias[:, lo:hi] + cw[CONV_W - 1:CONV_W, lo:hi] * raw
        for j in range(CONV_W - 1):
            y = y + cw[j:j + 1, lo:hi] * buf_ref[j]
        for j in range(CONV_W - 2):
            nbuf_ref[j] = buf_ref[j + 1]
        nbuf_ref[CONV_W - 2] = raw
        return jax.nn.silu(y)

    xs = conv_silu(x_ref[...], bufx_ref, nbx_ref, 0, BRANCH_W)
    bc = conv_silu(bc_ref[...], bufbc_ref, nbbc_ref, BRANCH_W, SSD_CONV_DIM)
    dt = jax.nn.softplus(dt_ref[...] + dtb_ref[...])
    decay = jnp.exp(dt * (-jnp.exp(alog_ref[...])))
    nb = xs.shape[0]
    hpg = SSD_HEADS // SSD_GROUPS
    lane_lo = lax.broadcasted_iota(jnp.int32, (nb, LANES), 1) < SSD_HEADDIM
    for pi in range(SSD_HEADS // 2):
        h0 = 2 * pi
        g = h0 // hpg
        xdt_cols = _rows_to_cols(xs[:, pi * LANES:(pi + 1) * LANES] * _head_pair(dt, h0, lane_lo))
        for e in range(2):
            h = h0 + e
            for j in range(nb):
                xcol = xdt_cols[e * SSD_HEADDIM:(e + 1) * SSD_HEADDIM, j:j + 1]
                brow = bc[j:j + 1, g * SSD_STATE:(g + 1) * SSD_STATE]
                so_ref[j, h] = s_ref[j, h] * decay[j:j + 1, h:h + 1] + xcol * brow
    for g in range(SSD_GROUPS):
        cm = bc[:, (SSD_GROUPS + g) * SSD_STATE:(SSD_GROUPS + g + 1) * SSD_STATE].astype(BF16)
        for j in range(nb):
            s_new = so_ref[j, g * hpg:(g + 1) * hpg].reshape(hpg * SSD_HEADDIM, SSD_STATE)
            y_s[j:j + 1, g * hpg * SSD_HEADDIM:(g + 1) * hpg * SSD_HEADDIM] = _nt_dot(cm, s_new.astype(BF16))[j:j + 1, :]
    y = (y_s[...] + dpar_ref[...] * xs) * jax.nn.silu(z_ref[...])
    gw = BRANCH_W // SSD_GROUPS
    nw = nw_ref[...]
    outs = [_rms_rows(y[:, g * gw:(g + 1) * gw]) * nw[:, g * gw:(g + 1) * gw] for g in range(SSD_GROUPS)]
    out_ref[...] = jnp.concatenate(outs, axis=1)


def ssd_sample(layer, proj, p, state, state_out, buf_t):
    nb = SSD_SAMPLE_BLK
    const = lambda shape: pl.BlockSpec(shape, lambda i: (0,) * len(shape))
    rowblk = lambda w, off: pl.BlockSpec((nb, w), lambda i: (i, off // w))
    st_spec = pl.BlockSpec((None, nb, SSD_HEADS, SSD_HEADDIM, SSD_STATE), lambda i: (layer, i, 0, 0, 0))
    bufx_spec = pl.BlockSpec((CONV_W - 1, nb, BRANCH_W), lambda i: (0, i, 0))
    bufbc_spec = pl.BlockSpec((CONV_W - 1, nb, SSD_BC), lambda i: (0, i, BRANCH_W // SSD_BC))
    args = [proj, proj, proj, proj, *_ssd_params(p), buf_t, buf_t, state]
    in_specs = [rowblk(BRANCH_W, OFF_SZ), rowblk(BRANCH_W, OFF_SX), rowblk(SSD_BC, OFF_SBC), rowblk(LANES, OFF_SDT),
                const((CONV_W, SSD_CONV_DIM)), const((1, SSD_CONV_DIM)), const((1, LANES)), const((1, LANES)),
                const((1, BRANCH_W)), const((1, BRANCH_W)), bufx_spec, bufbc_spec, st_spec]
    aliases = {}
    kern = _ssd_sample_kernel
    if state_out is not None:
        args.append(state_out)
        in_specs.append(pl.BlockSpec(memory_space=pl.ANY))
        aliases = {len(args) - 1: 1}
        kern = functools.partial(_drop_alias_arg, kern, 13)
    out, st, nbx, nbbc = pl.pallas_call(
        kern,
        out_shape=(jax.ShapeDtypeStruct((DEC_BATCH, BRANCH_W), F32),
                   jax.ShapeDtypeStruct(state.shape, F32),
                   jax.ShapeDtypeStruct((CONV_W - 1, DEC_BATCH, BRANCH_W), F32),
                   jax.ShapeDtypeStruct((CONV_W - 1, DEC_BATCH, SSD_BC), F32)),
        grid=(DEC_BATCH // nb,),
        in_specs=in_specs,
        out_specs=(pl.BlockSpec((nb, BRANCH_W), lambda i: (i, 0)), st_spec,
                   pl.BlockSpec((CONV_W - 1, nb, BRANCH_W), lambda i: (0, i, 0)),
                   pl.BlockSpec((CONV_W - 1, nb, SSD_BC), lambda i: (0, i, 0))),
        scratch_shapes=[pltpu.VMEM((nb, BRANCH_W), F32)],
        input_output_aliases=aliases,
        compiler_params=_cparams(1),
        name="ssd_sample",
    )(*args)
    return out, st, jnp.concatenate([nbx, nbbc], axis=-1)


def _ret_prompt_kernel(hin_ref, wq_ref, wk_ref, wv_ref, wg_ref, cos_ref, sin_ref, lg_ref, out_ref, st_ref,
                       q_s, k_s, v_s, o_s):
    c = RET_CHUNK
    t_len = hin_ref.shape[0]
    proj = _project(hin_ref, (wq_ref, wk_ref, wv_ref, wg_ref))
    cos = cos_ref[...]
    sin = sin_ref[...]
    q_s[...] = _rope(proj[:, 0:RET_DK], cos, sin)
    k_s[...] = _rope(proj[:, RET_DK:2 * RET_DK], cos, sin) * RET_DK ** -0.5
    v_s[...] = proj[:, 2 * RET_DK:3 * RET_DK].astype(BF16)
    gate = proj[:, 3 * RET_DK:]
    lg = lg_ref[...]
    lg128 = lg[:, :LANES]
    ti = lax.broadcasted_iota(jnp.int32, (c, c), 0)
    si = lax.broadcasted_iota(jnp.int32, (c, c), 1)
    tri = ti >= si
    dec = jnp.where(tri, jnp.exp(jnp.where(tri, (ti - si).astype(F32) * lg, 0.0)), 0.0)
    tt = lax.broadcasted_iota(jnp.int32, (c, LANES), 0).astype(F32)
    g_in = jnp.exp((tt + 1.0) * lg128)
    g_out = jnp.exp((c - 1.0 - tt) * lg128)
    g_all = jnp.exp(float(c) * lg128)
    s = jnp.zeros((RET_DK, RET_DK), F32)
    for ci in range(t_len // c):
        sl = slice(ci * c, (ci + 1) * c)
        qc = q_s[sl, :]
        kc = k_s[sl, :]
        vb = v_s[sl, :]
        scores = _nt_dot(qc.astype(BF16), kc.astype(BF16)) * dec
        o_s[sl, :] = (jnp.dot(scores.astype(BF16), vb, preferred_element_type=F32)
                      + jnp.dot((qc * g_in).astype(BF16), s.astype(BF16), preferred_element_type=F32))
        s = s * g_all + _tn_dot((kc * g_out).astype(BF16), vb)
    out_ref[...] = (_rms_rows(o_s[...]) * jax.nn.silu(gate)).astype(out_ref.dtype)
    st_ref[...] = s


def _log_gamma_rows(width):
    lg = jnp.log1p(-jnp.exp2(-5.0 - jnp.arange(RET_HEADS, dtype=F32)))
    return jnp.broadcast_to(lg[:, None, None], (RET_HEADS, 1, width))


def ret_prompt(layer, hin, w_in_t, cos, sin):
    tab = pl.BlockSpec((SEQ, RET_DK), lambda b, h: (0, 0))
    ret_rows = tuple(RET_COL0 + off for off in (OFF_RQ, OFF_RK, OFF_RV, OFF_RG))
    return pl.pallas_call(
        _ret_prompt_kernel,
        out_shape=(jax.ShapeDtypeStruct((BATCH * SEQ, BRANCH_W), BF16),
                   jax.ShapeDtypeStruct((BATCH, RET_HEADS, RET_DK, RET_DK), F32)),
        grid=(BATCH, RET_HEADS),
        in_specs=[pl.BlockSpec((SEQ, D_MODEL), lambda b, h: (b, 0))]
                 + _w_in_row_specs(layer, ret_rows, _prompt_unit_rows)
                 + [tab, tab, pl.BlockSpec((None, 1, RET_CHUNK), lambda b, h: (h, 0, 0))],
        out_specs=(pl.BlockSpec((SEQ, RET_DK), lambda b, h: (b, h)),
                   pl.BlockSpec((None, None, RET_DK, RET_DK), lambda b, h: (b, h, 0, 0))),
        scratch_shapes=[pltpu.VMEM((SEQ, RET_DK), F32), pltpu.VMEM((SEQ, RET_DK), F32),
                        pltpu.VMEM((SEQ, RET_DK), BF16), pltpu.VMEM((SEQ, RET_DK), F32)],
        compiler_params=_cparams(2),
        name="ret_prompt",
    )(hin, w_in_t, w_in_t, w_in_t, w_in_t, cos, sin, _log_gamma_rows(RET_CHUNK))


def _ret_sample_kernel(q_ref, k_ref, v_ref, g_ref, cos_ref, sin_ref, lg_ref, s_ref, out_ref, so_ref, o_s):
    cos = cos_ref[0:1, :]
    sin = sin_ref[0:1, :]
    q = _rope(q_ref[...], cos, sin)
    k = _rope(k_ref[...], cos, sin) * RET_DK ** -0.5
    v = v_ref[...]
    gamma = jnp.exp(lg_ref[...])
    _sample_state_step(s_ref, so_ref, o_s, lambda j: gamma, q, k, v)
    out_ref[...] = _rms_rows(o_s[...]) * jax.nn.silu(g_ref[...])


def ret_sample(layer, proj, cos, sin, state, state_out):
    cb = lambda off: off // RET_DK
    nb = SAMPLE_BLK
    col = lambda off: pl.BlockSpec((nb, RET_DK), lambda h, i: (i, cb(off) + h))
    tab = pl.BlockSpec((SUBLANES, RET_DK), lambda h, i: (0, 0))
    st_spec = pl.BlockSpec((None, nb, None, RET_DK, RET_DK), lambda h, i: (layer, i, h, 0, 0))
    args = [proj, proj, proj, proj, cos, sin, _log_gamma_rows(LANES), state]
    in_specs = [col(OFF_RQ), col(OFF_RK), col(OFF_RV), col(OFF_RG), tab, tab,
                pl.BlockSpec((None, 1, LANES), lambda h, i: (h, 0, 0)), st_spec]
    aliases = {}
    kern = _ret_sample_kernel
    if state_out is not None:
        args.append(state_out)
        in_specs.append(pl.BlockSpec(memory_space=pl.ANY))
        aliases = {len(args) - 1: 1}
        kern = functools.partial(_drop_alias_arg, kern, len(args) - 1)
    return pl.pallas_call(
        kern,
        out_shape=(jax.ShapeDtypeStruct((DEC_BATCH, BRANCH_W), F32),
                   jax.ShapeDtypeStruct(state.shape, F32)),
        grid=(RET_HEADS, DEC_BATCH // nb),
        in_specs=in_specs,
        out_specs=(pl.BlockSpec((nb, RET_DK), lambda h, i: (i, h)), st_spec),
        scratch_shapes=[pltpu.VMEM((nb, RET_DK), F32)],
        input_output_aliases=aliases,
        compiler_params=_cparams(2),
        name="ret_sample",
    )(*args)


def kernel(x_prompt, x_sample, state_lru_h, state_lru_conv, state_hgrn, state_ssd, state_ssd_conv, state_ret, state_ffn_conv, g_mix, g_ffn, w_in, lru_conv_w, lru_conv_b, lru_wa, lru_ba, lru_wx, lru_bx, lru_lambda, hg_lb_logits, hg_norm_w, ssd_conv_w, ssd_conv_b, ssd_dt_bias, ssd_a_log, ssd_d, ssd_norm_w, w_branch, w_gate, w_out, ffn_w_up, ffn_w_val, ffn_conv_w, ffn_conv_b, ffn_w_down, g_final):
    xp = x_prompt.reshape(BATCH * SEQ, D_MODEL)
    xs = x_sample.reshape(DEC_BATCH, D_MODEL)

    cos_p, sin_p = rope_tables(SEQ, 0, True)
    cos_s, sin_s = rope_tables(SUBLANES, PAST_LEN, False)

    hp = rmsnorm(xp, g_mix[0], BF16, NORM_TM)
    hs = rmsnorm(xs, g_mix[0], BF16, DEC_BATCH)

    state_ssd_t = jnp.swapaxes(state_ssd, -1, -2)

    w_in_t = jnp.swapaxes(w_in, 1, 2)
    w_gate_b = jnp.transpose(w_gate, (0, 2, 1, 3)).astype(BF16)
    w_branch_b = w_branch.astype(BF16)
    w_out_b = w_out.astype(BF16)
    w_down_b = ffn_w_down.astype(BF16)

    prompt_states, sample_small = [], []
    hg_out = ssd_out = ret_out = None
    for l in range(DEPTH):
        p = {"lru_conv_w": lru_conv_w[l], "lru_conv_b": lru_conv_b[l], "lru_wa": lru_wa[l], "lru_ba": lru_ba[l],
             "lru_wx": lru_wx[l], "lru_bx": lru_bx[l], "lru_lambda": lru_lambda[l],
             "hg_lb_logits": hg_lb_logits, "hg_norm_w": hg_norm_w[l],
             "ssd_conv_w": ssd_conv_w[l], "ssd_conv_b": ssd_conv_b[l], "ssd_dt_bias": ssd_dt_bias[l],
             "ssd_a_log": ssd_a_log[l], "ssd_d": ssd_d[l], "ssd_norm_w": ssd_norm_w[l]}
        last = l == DEPTH - 1
        g_next = g_final if last else g_mix[l + 1]

        out_a, lru_h_p, lru_conv_p = lru_prompt(l, hp, w_in_t, p)
        out_b, hg_p = hgrn_prompt(l, hp, w_in_t, p)
        proj_ssd_p, _ = in_proj_prompt(l, hp, w_in_t, OFF_SZ, SSD_TILES, SSD_TN, SSD_PROJ_TM)
        out_c, ssd_p, ssd_conv_p = ssd_prompt(proj_ssd_p, p)
        out_d, ret_p = ret_prompt(l, hp, w_in_t, cos_p, sin_p)
        merged = gated_merge(l, hp, (out_a, out_b, out_c, out_d), w_gate_b, w_branch_b, MERGE_TM, MERGE_TN)
        xp, h2 = out_proj_residual_norm(l, merged, w_out_b, xp, g_ffn[l], OUT_PROJ_TM)
        act, ffn_conv_p, w_up_b, w_val_b = ffn_prompt(l, h2, ffn_w_up, ffn_w_val, ffn_conv_w[l], ffn_conv_b[l], FFN_TN)
        res = down_proj_residual_norm(l, act, w_down_b, xp, g_next, DOWN_PROJ_TM, DOWN_PROJ_TK, not last,
                                      F32 if last else BF16)
        if last:
            (yp,) = res
        else:
            xp, hp = res
        prompt_states.append((lru_h_p, lru_conv_p, hg_p, ssd_p, ssd_conv_p, ret_p, ffn_conv_p))

        proj_s = in_proj_sample(l, hs, w_in_t, 0, MAIN_TILES, MAIN_TN)
        proj_ret_s = in_proj_sample(l, hs, w_in_t, RET_COL0, RET_TILES, RET_TN)
        lru_buf_t = jnp.swapaxes(state_lru_conv[l], 0, 1)
        ssd_buf_t = jnp.swapaxes(state_ssd_conv[l], 0, 1)
        s_a, lru_h_s, lru_nbuf = lru_sample(proj_s, p, state_lru_h[l], lru_buf_t)
        s_b, hg_out = hgrn_sample(l, proj_s, p, state_hgrn, hg_out)
        s_c, ssd_out, ssd_nbuf = ssd_sample(l, proj_s, p, state_ssd_t, ssd_out, ssd_buf_t)
        s_d, ret_out = ret_sample(l, proj_ret_s, cos_s, sin_s, state_ret, ret_out)
        merged_s = gated_merge(l, hs, (s_a, s_b, s_c, s_d), w_gate_b, w_branch_b, DEC_BATCH, MERGE_TN)
        xs, h2s = out_proj_residual_norm(l, merged_s, w_out_b, xs, g_ffn[l], DEC_BATCH)
        act_s, ffn_nbuf = ffn_sample(l, h2s, w_up_b, w_val_b, ffn_conv_w[l], ffn_conv_b[l], state_ffn_conv,
                                       FFN_SAMPLE_TN)
        res = down_proj_residual_norm(l, act_s, w_down_b, xs, g_next, DEC_BATCH, DOWN_PROJ_TK, not last,
                                      F32 if last else BF16)
        if last:
            (ys,) = res
        else:
            xs, hs = res
        sample_small.append((lru_h_s, jnp.swapaxes(lru_nbuf, 0, 1), jnp.swapaxes(ssd_nbuf, 0, 1), ffn_nbuf))

    stack_p = lambda i: jnp.stack([st[i] for st in prompt_states], axis=0)
    stack_s = lambda i: jnp.stack([st[i] for st in sample_small], axis=0)
    return (yp.reshape(BATCH, SEQ, D_MODEL), ys.reshape(DEC_BATCH, 1, D_MODEL),
            stack_p(0), stack_s(0), stack_p(1), stack_s(1),
            stack_p(2), hg_out, jnp.swapaxes(stack_p(3), -1, -2), jnp.swapaxes(ssd_out, -1, -2),
            stack_p(4), stack_s(2), stack_p(5), ret_out,
            stack_p(6), stack_s(3))
```

```python
import functools

import numpy as np
import jax
import jax.numpy as jnp
from jax import lax
from jax.experimental import pallas as pl
from jax.experimental.pallas import tpu as pltpu

F32 = jnp.float32
BF16 = jnp.bfloat16

D_MODEL = 2048
BATCH = 4
SEQ = 2048
DEPTH = 2
DEC_BATCH = 128
PAST_LEN = 16384
BRANCH_W = D_MODEL // 2
EPS = 1e-6
LRU_BLOCKS = 8
LRU_BLOCK = BRANCH_W // LRU_BLOCKS
LRU_C = 8.0
CONV_W = 4
HG_HEADS = 8
HG_DK = BRANCH_W // HG_HEADS
SSD_HEADDIM = 64
SSD_HEADS = BRANCH_W // SSD_HEADDIM
SSD_GROUPS = 2
SSD_STATE = 128
SSD_BC = 2 * SSD_GROUPS * SSD_STATE
SSD_CONV_DIM = BRANCH_W + SSD_BC
RET_HEADS = 8
RET_DK = BRANCH_W // RET_HEADS
ROPE_BASE = 10000.0
D_FF = 5632
FFN_CONV_W = 3

V7X_VMEM_BYTES = 64 * 1024 * 1024
VMEM_LIMIT_BYTES = V7X_VMEM_BYTES - 8 * 1024 * 1024
LANES = 128
SUBLANES = 8

N_BRANCH = 4
N_IN = 12816
OFF_XA, OFF_YA = 0, 1024
OFF_HQ, OFF_HF, OFF_HI, OFF_HG = 2048, 3072, 4096, 5120
OFF_SZ, OFF_SX, OFF_SBC = 6144, 7168, 8192
OFF_SDT = 8704
MAIN_TN, MAIN_TILES = 1280, 7
N_MAIN = MAIN_TN * MAIN_TILES
RET_COL0 = OFF_SDT + SSD_HEADS
RET_TN, RET_TILES = 1024, 4
SSD_TN, SSD_TILES = 896, 3
SSD_PROJ_W = SSD_TN * SSD_TILES
OFF_RQ, OFF_RK, OFF_RV, OFF_RG = 0, 1024, 2048, 3072
assert RET_COL0 + RET_TN * RET_TILES == N_IN and OFF_SZ + SSD_PROJ_W >= OFF_SDT + LANES

NORM_TM = 512
SSD_PROJ_TM = 1024
MERGE_TM, MERGE_TN = 1024, 256
OUT_PROJ_TM = 512
FFN_TN = 512
FFN_ROW_CHUNK = 512
DOWN_PROJ_TM = 512
DOWN_PROJ_TK = 2816
FFN_SAMPLE_TN = 512

HG_CHUNK = 128
HG_LEVELS = (1, 2, 4, 8, 16, 32, 64)
SSD_CHUNK = 128
RET_CHUNK = 256
SAMPLE_BLK = 128
SSD_SAMPLE_BLK = 8


def _cparams(n_axes):
    return pltpu.CompilerParams(dimension_semantics=("arbitrary",) * n_axes,
                                vmem_limit_bytes=VMEM_LIMIT_BYTES)


def _rms_rows(x):
    return x * lax.rsqrt(jnp.mean(x * x, axis=-1, keepdims=True) + EPS)


def _shift_rows(x, d, row):
    return jnp.where(row >= d, pltpu.roll(x, d, axis=0), 0.0)


def _nt_dot(a, b):
    return lax.dot_general(a, b, (((1,), (1,)), ((), ())), preferred_element_type=F32)


def _tn_dot(a, b):
    return lax.dot_general(a, b, (((0,), (0,)), ((), ())), preferred_element_type=F32)


def _project(h_ref, w_refs):
    w = jnp.concatenate([w_ref[0].astype(BF16) for w_ref in w_refs], axis=0)
    return _nt_dot(h_ref[...], w)


def _w_in_row_specs(layer, offsets, index_map_for):
    return [pl.BlockSpec((pl.Element(1), pl.Element(LANES), pl.Element(D_MODEL)), index_map_for(layer, off))
            for off in offsets]


def _rows_to_cols(x):
    n = x.shape[0]
    if n < LANES:
        x = jnp.concatenate([x, jnp.zeros((LANES - n, x.shape[1]), x.dtype)], axis=0)
    return x.T


def _norm_kernel(x_ref, g_ref, o_ref):
    o_ref[...] = (_rms_rows(x_ref[...]) * g_ref[...]).astype(o_ref.dtype)


def rmsnorm(x, g, out_dtype, tm):
    m, d = x.shape
    return pl.pallas_call(
        _norm_kernel,
        out_shape=jax.ShapeDtypeStruct((m, d), out_dtype),
        grid=(m // tm,),
        in_specs=[pl.BlockSpec((tm, d), lambda i: (i, 0)),
                  pl.BlockSpec((1, d), lambda i: (0, 0))],
        out_specs=pl.BlockSpec((tm, d), lambda i: (i, 0)),
        compiler_params=_cparams(1),
        name="rmsnorm",
    )(x, g.reshape(1, d))


def _mm_nt_cast_kernel(a_ref, wt_ref, o_ref, wb_s):
    @pl.when(pl.program_id(1) == 0)
    def _():
        wb_s[...] = wt_ref[0].astype(BF16)

    o_ref[...] = _nt_dot(a_ref[...], wb_s[...])


def in_proj_prompt(layer, h, w_in_t, row0, n_tiles, tn, tm):
    m = h.shape[0]
    w_spec = pl.BlockSpec((pl.Element(1), pl.Element(tn), pl.Element(D_MODEL)),
                          lambda j, i: (layer, pl.multiple_of(row0 + j * tn, SUBLANES), 0))
    return pl.pallas_call(
        _mm_nt_cast_kernel,
        out_shape=jax.ShapeDtypeStruct((m, n_tiles * tn), F32),
        grid=(n_tiles, m // tm),
        in_specs=[pl.BlockSpec((tm, D_MODEL), lambda j, i: (i, 0)), w_spec],
        out_specs=pl.BlockSpec((tm, tn), lambda j, i: (i, j)),
        scratch_shapes=[pltpu.VMEM((tn, D_MODEL), BF16)],
        compiler_params=_cparams(2),
        name="in_proj_prompt",
    )(h, w_in_t)


def _mm_nt_castw_kernel(a_ref, wt_ref, o_ref):
    o_ref[...] = _nt_dot(a_ref[...], wt_ref[0].astype(BF16))


def in_proj_sample(layer, h, w_in_t, row0, n_tiles, tn):
    m = h.shape[0]
    w_spec = pl.BlockSpec((pl.Element(1), pl.Element(tn), pl.Element(D_MODEL)),
                          lambda j: (layer, pl.multiple_of(row0 + j * tn, SUBLANES), 0))
    return pl.pallas_call(
        _mm_nt_castw_kernel,
        out_shape=jax.ShapeDtypeStruct((m, n_tiles * tn), F32),
        grid=(n_tiles,),
        in_specs=[pl.BlockSpec((m, D_MODEL), lambda j: (0, 0)), w_spec],
        out_specs=pl.BlockSpec((m, tn), lambda j: (0, j)),
        compiler_params=_cparams(1),
        name="in_proj_sample",
    )(h, w_in_t)


def _gated_sum(h, br_refs, gate_w, branch_w):
    acc = None
    for k, br_ref in enumerate(br_refs):
        gate = jax.nn.sigmoid(jnp.dot(h, gate_w(k), preferred_element_type=F32))
        br = jnp.dot(br_ref[...].astype(BF16), branch_w(k), preferred_element_type=F32)
        acc = gate * br if acc is None else acc + gate * br
    return acc


def _merge_cast_kernel(h_ref, a_ref, b_ref, c_ref, d_ref, wg_ref, wb_ref, o_ref, wbb_ref):
    @pl.when(pl.program_id(1) == 0)
    def _():
        for k in range(N_BRANCH):
            wbb_ref[k] = wb_ref[k].astype(BF16)

    acc = _gated_sum(h_ref[...], (a_ref, b_ref, c_ref, d_ref), lambda k: wg_ref[k], lambda k: wbb_ref[k])
    o_ref[...] = acc.astype(o_ref.dtype)


def gated_merge_prompt(layer, h, branches, w_gate_b, w_branch, tm, tn):
    m = h.shape[0]
    br_specs = [pl.BlockSpec((tm, BRANCH_W), lambda j, i: (i, 0)) for _ in range(N_BRANCH)]
    return pl.pallas_call(
        _merge_cast_kernel,
        out_shape=(jax.ShapeDtypeStruct((m, D_MODEL), BF16),
                   jax.ShapeDtypeStruct((N_BRANCH, BRANCH_W, D_MODEL), BF16)),
        grid=(D_MODEL // tn, m // tm),
        in_specs=[pl.BlockSpec((tm, D_MODEL), lambda j, i: (i, 0))] + br_specs
                 + [pl.BlockSpec((None, N_BRANCH, D_MODEL, tn), lambda j, i: (layer, 0, 0, j)),
                    pl.BlockSpec((None, N_BRANCH, BRANCH_W, tn), lambda j, i: (layer, 0, 0, j))],
        out_specs=(pl.BlockSpec((tm, tn), lambda j, i: (i, j)),
                   pl.BlockSpec((N_BRANCH, BRANCH_W, tn), lambda j, i: (0, 0, j))),
        compiler_params=_cparams(2),
        name="gated_merge_prompt",
    )(h, *branches, w_gate_b, w_branch)


def _merge_kernel(h_ref, a_ref, b_ref, c_ref, d_ref, wgb_ref, wbb_ref, o_ref):
    acc = _gated_sum(h_ref[...], (a_ref, b_ref, c_ref, d_ref), lambda k: wgb_ref[k], lambda k: wbb_ref[k])
    o_ref[...] = acc.astype(o_ref.dtype)


def gated_merge_sample(layer, h, branches, w_gate_b, w_branch_b, tn):
    m = h.shape[0]
    br_specs = [pl.BlockSpec((m, BRANCH_W), lambda j: (0, 0)) for _ in range(N_BRANCH)]
    return pl.pallas_call(
        _merge_kernel,
        out_shape=jax.ShapeDtypeStruct((m, D_MODEL), BF16),
        grid=(D_MODEL // tn,),
        in_specs=[pl.BlockSpec((m, D_MODEL), lambda j: (0, 0))] + br_specs
                 + [pl.BlockSpec((None, N_BRANCH, D_MODEL, tn), lambda j: (layer, 0, 0, j)),
                    pl.BlockSpec((N_BRANCH, BRANCH_W, tn), lambda j: (0, 0, j))],
        out_specs=pl.BlockSpec((m, tn), lambda j: (0, j)),
        compiler_params=_cparams(1),
        name="gated_merge_sample",
    )(h, *branches, w_gate_b, w_branch_b)


def _out_proj_kernel(m_ref, w_ref, x_ref, g_ref, xo_ref, ho_ref):
    x_new = x_ref[...] + jnp.dot(m_ref[...], w_ref[...], preferred_element_type=F32)
    xo_ref[...] = x_new
    ho_ref[...] = (_rms_rows(x_new) * g_ref[...]).astype(ho_ref.dtype)


def out_proj_residual_norm(layer, merged, w_out, x, g, tm):
    m = x.shape[0]
    return pl.pallas_call(
        _out_proj_kernel,
        out_shape=(jax.ShapeDtypeStruct((m, D_MODEL), F32), jax.ShapeDtypeStruct((m, D_MODEL), BF16)),
        grid=(m // tm,),
        in_specs=[pl.BlockSpec((tm, D_MODEL), lambda i: (i, 0)),
                  pl.BlockSpec((None, D_MODEL, D_MODEL), lambda i: (layer, 0, 0)),
                  pl.BlockSpec((tm, D_MODEL), lambda i: (i, 0)),
                  pl.BlockSpec((1, D_MODEL), lambda i: (0, 0))],
        out_specs=(pl.BlockSpec((tm, D_MODEL), lambda i: (i, 0)),
                   pl.BlockSpec((tm, D_MODEL), lambda i: (i, 0))),
        compiler_params=_cparams(1),
        name="out_proj",
    )(merged, w_out, x, g.reshape(1, D_MODEL))


def _down_proj_kernel(emit_x, a_ref, w_ref, x_ref, g_ref, *refs):
    if emit_x:
        xo_ref, no_ref, acc_ref = refs
    else:
        no_ref, acc_ref = refs
    kk = pl.program_id(1)

    @pl.when(kk == 0)
    def _():
        acc_ref[...] = x_ref[...]

    acc_ref[...] += jnp.dot(a_ref[...], w_ref[...], preferred_element_type=F32)

    @pl.when(kk == pl.num_programs(1) - 1)
    def _():
        x_new = acc_ref[...]
        if emit_x:
            xo_ref[...] = x_new
        no_ref[...] = (_rms_rows(x_new) * g_ref[...]).astype(no_ref.dtype)


def down_proj_residual_norm(layer, a, w_down, x, g, tm, tk, emit_x, norm_dtype):
    m = x.shape[0]
    out_shape = [jax.ShapeDtypeStruct((m, D_MODEL), norm_dtype)]
    out_specs = [pl.BlockSpec((tm, D_MODEL), lambda i, k: (i, 0))]
    if emit_x:
        out_shape = [jax.ShapeDtypeStruct((m, D_MODEL), F32)] + out_shape
        out_specs = [pl.BlockSpec((tm, D_MODEL), lambda i, k: (i, 0))] + out_specs
    return pl.pallas_call(
        functools.partial(_down_proj_kernel, emit_x),
        out_shape=tuple(out_shape),
        grid=(m // tm, D_FF // tk),
        in_specs=[pl.BlockSpec((tm, tk), lambda i, k: (i, k)),
                  pl.BlockSpec((None, tk, D_MODEL), lambda i, k: (layer, k, 0)),
                  pl.BlockSpec((tm, D_MODEL), lambda i, k: (i, 0)),
                  pl.BlockSpec((1, D_MODEL), lambda i, k: (0, 0))],
        out_specs=tuple(out_specs),
        scratch_shapes=[pltpu.VMEM((tm, D_MODEL), F32)],
        compiler_params=_cparams(2),
        name="down_proj",
    )(a, w_down, x, g.reshape(1, D_MODEL))


def _ffn_prompt_kernel(h_ref, wu_ref, wv_ref, cw_ref, cb_ref, a_ref, st_ref, wub_ref, wvb_ref):
    @pl.when(pl.program_id(1) == 0)
    def _():
        wub_ref[...] = wu_ref[...].astype(BF16)
        wvb_ref[...] = wv_ref[...].astype(BF16)

    rc = FFN_ROW_CHUNK
    tn = a_ref.shape[1]
    cw = cw_ref[...]
    cbias = cb_ref[...]
    row8 = lax.broadcasted_iota(jnp.int32, (SUBLANES, tn), 0)
    tail = jnp.zeros((SUBLANES, tn), F32)
    pending = None
    for c in range(h_ref.shape[0] // rc):
        rows = slice(c * rc, (c + 1) * rc)
        hc = h_ref[rows, :]
        u = jnp.dot(hc, wub_ref[...], preferred_element_type=F32)
        if pending is not None:
            prev_rows, g_prev, v_prev = pending
            a_ref[prev_rows, :] = (g_prev * v_prev).astype(a_ref.dtype)
        v = jnp.dot(hc, wvb_ref[...], preferred_element_type=F32)
        uc = cbias + cw[FFN_CONV_W - 1:FFN_CONV_W] * u
        for d in range(1, FFN_CONV_W):
            rolled = pltpu.roll(u, d, axis=0)
            top = jnp.where(row8 >= d, rolled[:SUBLANES], pltpu.roll(tail, d, axis=0))
            uc = uc + cw[FFN_CONV_W - 1 - d:FFN_CONV_W - d] * jnp.concatenate([top, rolled[SUBLANES:]], axis=0)
        pending = (rows, jax.nn.gelu(uc), v)
        tail = u[rc - SUBLANES:, :]
    prev_rows, g_prev, v_prev = pending
    a_ref[prev_rows, :] = (g_prev * v_prev).astype(a_ref.dtype)
    st_ref[...] = tail[SUBLANES - (FFN_CONV_W - 1):, :]


def ffn_prompt(layer, h2, w_up, w_val, conv_w, conv_b, tn):
    wspec = pl.BlockSpec((None, D_MODEL, tn), lambda j, b: (layer, 0, j))
    wbspec = pl.BlockSpec((D_MODEL, tn), lambda j, b: (0, j))
    return pl.pallas_call(
        _ffn_prompt_kernel,
        out_shape=(jax.ShapeDtypeStruct((BATCH * SEQ, D_FF), BF16),
                   jax.ShapeDtypeStruct((BATCH, FFN_CONV_W - 1, D_FF), F32),
                   jax.ShapeDtypeStruct((D_MODEL, D_FF), BF16),
                   jax.ShapeDtypeStruct((D_MODEL, D_FF), BF16)),
        grid=(D_FF // tn, BATCH),
        in_specs=[pl.BlockSpec((SEQ, D_MODEL), lambda j, b: (b, 0)), wspec, wspec,
                  pl.BlockSpec((FFN_CONV_W, tn), lambda j, b: (0, j)),
                  pl.BlockSpec((1, tn), lambda j, b: (0, j))],
        out_specs=(pl.BlockSpec((SEQ, tn), lambda j, b: (b, j)),
                   pl.BlockSpec((None, FFN_CONV_W - 1, tn), lambda j, b: (b, 0, j)),
                   wbspec, wbspec),
        compiler_params=_cparams(2),
        name="ffn_prompt",
    )(h2, w_up, w_val, conv_w, conv_b.reshape(1, D_FF))


def _ffn_sample_kernel(h_ref, wu_ref, wv_ref, cw_ref, cb_ref, buf_ref, a_ref, nb_ref):
    h = h_ref[...]
    u = jnp.dot(h, wu_ref[...], preferred_element_type=F32)
    v = jnp.dot(h, wv_ref[...], preferred_element_type=F32)
    cw = cw_ref[...]
    b0 = buf_ref[:, 0, :]
    b1 = buf_ref[:, 1, :]
    uc = cb_ref[...] + cw[0:1] * b0 + cw[1:2] * b1 + cw[2:3] * u
    a_ref[...] = (jax.nn.gelu(uc) * v).astype(a_ref.dtype)
    nb_ref[:, 0, :] = b1
    nb_ref[:, 1, :] = u


def ffn_sample(layer, h2, w_up, w_val, conv_w, conv_b, buf, tn):
    return pl.pallas_call(
        _ffn_sample_kernel,
        out_shape=(jax.ShapeDtypeStruct((DEC_BATCH, D_FF), BF16),
                   jax.ShapeDtypeStruct((DEC_BATCH, FFN_CONV_W - 1, D_FF), F32)),
        grid=(D_FF // tn,),
        in_specs=[pl.BlockSpec((DEC_BATCH, D_MODEL), lambda j: (0, 0)),
                  pl.BlockSpec((D_MODEL, tn), lambda j: (0, j)),
                  pl.BlockSpec((D_MODEL, tn), lambda j: (0, j)),
                  pl.BlockSpec((FFN_CONV_W, tn), lambda j: (0, j)),
                  pl.BlockSpec((1, tn), lambda j: (0, j)),
                  pl.BlockSpec((None, DEC_BATCH, FFN_CONV_W - 1, tn), lambda j: (layer, 0, 0, j))],
        out_specs=(pl.BlockSpec((DEC_BATCH, tn), lambda j: (0, j)),
                   pl.BlockSpec((DEC_BATCH, FFN_CONV_W - 1, tn), lambda j: (0, 0, j))),
        compiler_params=_cparams(1),
        name="ffn_sample",
    )(h2, w_up, w_val, conv_w, conv_b.reshape(1, D_FF), buf)


def _rope_table_kernel(start, consecutive, freq_ref, sign_ref, cos_ref, sin_ref):
    shape = cos_ref.shape
    if consecutive:
        pos = lax.broadcasted_iota(jnp.int32, shape, 0).astype(F32) + float(start)
    else:
        pos = jnp.full(shape, float(start), F32)
    ang = pos * freq_ref[...]
    cos_ref[...] = jnp.cos(ang)
    sin_ref[...] = sign_ref[...] * jnp.sin(ang)


def rope_tables(n_rows, start, consecutive):
    half = RET_DK // 2
    freqs = ROPE_BASE ** (-jnp.arange(half, dtype=F32) / half)
    freq2 = jnp.concatenate([freqs, freqs]).reshape(1, RET_DK)
    sign = jnp.concatenate([-jnp.ones((half,), F32), jnp.ones((half,), F32)]).reshape(1, RET_DK)
    return pl.pallas_call(
        functools.partial(_rope_table_kernel, start, consecutive),
        out_shape=(jax.ShapeDtypeStruct((n_rows, RET_DK), F32), jax.ShapeDtypeStruct((n_rows, RET_DK), F32)),
        name="rope_tables",
    )(freq2, sign)


def _rope(x, cos, sin_signed):
    return x * cos + pltpu.roll(x, RET_DK // 2, axis=1) * sin_signed


def _lru_gates(conv, wa_ref, ba_ref, wx_ref, bx_ref, lam_ref):
    xb = conv.astype(BF16)
    r = jax.nn.sigmoid(jnp.dot(xb, wa_ref[...].astype(BF16), preferred_element_type=F32) + ba_ref[...])
    i = jax.nn.sigmoid(jnp.dot(xb, wx_ref[...].astype(BF16), preferred_element_type=F32) + bx_ref[...])
    log_a = -LRU_C * r * jax.nn.softplus(-lam_ref[...])
    a = jnp.exp(log_a)
    u = jnp.sqrt(1.0 - a * a) * (i * conv)
    return a, u


def _lru_prompt_kernel(hin_ref, wxa_ref, wya_ref, cw_ref, cb_ref, wa_ref, ba_ref, wx_ref, bx_ref, lam_ref,
                       out_ref, h_ref, conv_ref, ag_s, ug_s):
    proj = _project(hin_ref, (wxa_ref, wya_ref))
    x = proj[:, :LRU_BLOCK]
    ya = proj[:, LRU_BLOCK:]
    t_len = x.shape[0]
    row = lax.broadcasted_iota(jnp.int32, x.shape, 0)
    cw = cw_ref[...]
    conv = cb_ref[...] + cw[CONV_W - 1:CONV_W] * x
    for d in range(1, CONV_W):
        conv = conv + cw[CONV_W - 1 - d:CONV_W - d] * _shift_rows(x, d, row)
    a, u = _lru_gates(conv, wa_ref, ba_ref, wx_ref, bx_ref, lam_ref)
    ng = t_len // SUBLANES
    a3 = a.reshape(ng, SUBLANES, LRU_BLOCK)
    u3 = u.reshape(ng, SUBLANES, LRU_BLOCK)
    sub = lax.broadcasted_iota(jnp.int32, a3.shape, 1)
    d = 1
    while d < SUBLANES:
        keep = sub >= d
        u3 = jnp.where(keep, a3 * pltpu.roll(u3, d, axis=1) + u3, u3)
        a3 = jnp.where(keep, a3 * pltpu.roll(a3, d, axis=1), a3)
        d *= 2
    ag_s[...] = a3.reshape(t_len, LRU_BLOCK)
    ug_s[...] = u3.reshape(t_len, LRU_BLOCK)
    ag = ag_s[pl.ds(SUBLANES - 1, ng, stride=SUBLANES), :]
    ug = ug_s[pl.ds(SUBLANES - 1, ng, stride=SUBLANES), :]
    grow = lax.broadcasted_iota(jnp.int32, ag.shape, 0)
    d = 1
    while d < ng:
        keep = grow >= d
        ug = jnp.where(keep, ag * pltpu.roll(ug, d, axis=0) + ug, ug)
        ag = jnp.where(keep, ag * pltpu.roll(ag, d, axis=0), ag)
        d *= 2
    carry = _shift_rows(ug, 1, grow)
    h3 = a3 * jnp.broadcast_to(carry[:, None, :], a3.shape) + u3
    hs = h3.reshape(t_len, LRU_BLOCK)
    out_ref[...] = (hs * jax.nn.gelu(ya)).astype(out_ref.dtype)
    h_ref[...] = ug[ng - 1:, :]
    conv_ref[...] = x[t_len - (CONV_W - 1):, :]


def _lru_param_specs(n_axes_fn):
    blk3 = lambda shape: pl.BlockSpec(shape, n_axes_fn(lambda n: (n, 0, 0)))
    return [pl.BlockSpec((CONV_W, LRU_BLOCK), n_axes_fn(lambda n: (0, n))),
            pl.BlockSpec((1, LRU_BLOCK), n_axes_fn(lambda n: (0, n))),
            blk3((None, LRU_BLOCK, LRU_BLOCK)), blk3((None, 1, LRU_BLOCK)),
            blk3((None, LRU_BLOCK, LRU_BLOCK)), blk3((None, 1, LRU_BLOCK)),
            blk3((None, 1, LRU_BLOCK))]


def _lru_params(p):
    return (p["lru_conv_w"], p["lru_conv_b"].reshape(1, BRANCH_W),
            p["lru_wa"], p["lru_ba"].reshape(LRU_BLOCKS, 1, LRU_BLOCK),
            p["lru_wx"], p["lru_bx"].reshape(LRU_BLOCKS, 1, LRU_BLOCK),
            p["lru_lambda"].reshape(LRU_BLOCKS, 1, LRU_BLOCK))


def _prompt_unit_rows(layer, off):
    return lambda b, u: (layer, pl.multiple_of(off + u * LANES, SUBLANES), 0)


def lru_prompt(layer, hin, w_in_t, p):
    wrap = lambda f: (lambda b, n: f(n))
    out, h, conv = pl.pallas_call(
        _lru_prompt_kernel,
        out_shape=(jax.ShapeDtypeStruct((BATCH * SEQ, BRANCH_W), BF16),
                   jax.ShapeDtypeStruct((BATCH, 1, BRANCH_W), F32),
                   jax.ShapeDtypeStruct((BATCH, CONV_W - 1, BRANCH_W), F32)),
        grid=(BATCH, LRU_BLOCKS),
        in_specs=[pl.BlockSpec((SEQ, D_MODEL), lambda b, n: (b, 0))]
                 + _w_in_row_specs(layer, (OFF_XA, OFF_YA), _prompt_unit_rows) + _lru_param_specs(wrap),
        out_specs=(pl.BlockSpec((SEQ, LRU_BLOCK), lambda b, n: (b, n)),
                   pl.BlockSpec((None, 1, LRU_BLOCK), lambda b, n: (b, 0, n)),
                   pl.BlockSpec((None, CONV_W - 1, LRU_BLOCK), lambda b, n: (b, 0, n))),
        scratch_shapes=[pltpu.VMEM((SEQ, LRU_BLOCK), F32)] * 2,
        compiler_params=_cparams(2),
        name="lru_prompt",
    )(hin, w_in_t, w_in_t, *_lru_params(p))
    return out, h.reshape(BATCH, BRANCH_W), conv


def _lru_sample_kernel(xa_ref, ya_ref, cw_ref, cb_ref, wa_ref, ba_ref, wx_ref, bx_ref, lam_ref,
                       h0_ref, buf_ref, out_ref, h_ref, nbuf_ref):
    x = xa_ref[...]
    cw = cw_ref[...]
    conv = cb_ref[...] + cw[CONV_W - 1:CONV_W] * x
    for j in range(CONV_W - 1):
        conv = conv + cw[j:j + 1] * buf_ref[j]
    a, u = _lru_gates(conv, wa_ref, ba_ref, wx_ref, bx_ref, lam_ref)
    h = a * h0_ref[...] + u
    out_ref[...] = h * jax.nn.gelu(ya_ref[...])
    h_ref[...] = h
    for j in range(CONV_W - 2):
        nbuf_ref[j] = buf_ref[j + 1]
    nbuf_ref[CONV_W - 2] = x


def lru_sample(proj, p, h0, buf_t):
    cb = lambda off: off // LRU_BLOCK
    wrap = lambda f: f
    return pl.pallas_call(
        _lru_sample_kernel,
        out_shape=(jax.ShapeDtypeStruct((DEC_BATCH, BRANCH_W), F32),
                   jax.ShapeDtypeStruct((DEC_BATCH, BRANCH_W), F32),
                   jax.ShapeDtypeStruct((CONV_W - 1, DEC_BATCH, BRANCH_W), F32)),
        grid=(LRU_BLOCKS,),
        in_specs=[pl.BlockSpec((DEC_BATCH, LRU_BLOCK), lambda n: (0, cb(OFF_XA) + n)),
                  pl.BlockSpec((DEC_BATCH, LRU_BLOCK), lambda n: (0, cb(OFF_YA) + n))]
                 + _lru_param_specs(wrap)
                 + [pl.BlockSpec((DEC_BATCH, LRU_BLOCK), lambda n: (0, n)),
                    pl.BlockSpec((CONV_W - 1, DEC_BATCH, LRU_BLOCK), lambda n: (0, 0, n))],
        out_specs=(pl.BlockSpec((DEC_BATCH, LRU_BLOCK), lambda n: (0, n)),
                   pl.BlockSpec((DEC_BATCH, LRU_BLOCK), lambda n: (0, n)),
                   pl.BlockSpec((CONV_W - 1, DEC_BATCH, LRU_BLOCK), lambda n: (0, 0, n))),
        compiler_params=_cparams(1),
        name="lru_sample",
    )(proj, proj, *_lru_params(p), h0, buf_t)


def _hgrn_lower_bound(layer, logits):
    mx = jnp.max(logits, axis=0, keepdims=True)
    e = jnp.exp(logits - mx)
    ls = e / jnp.sum(e, axis=0, keepdims=True)
    lb = jnp.zeros_like(ls[0:1])
    for i in range(1, layer + 1):
        lb = lb + ls[i:i + 1]
    return lb


def _hgrn_gates(layer, hq, hf, lbl_ref):
    lb = _hgrn_lower_bound(layer, lbl_ref[...])
    q = jax.nn.silu(hq)
    sg = jax.nn.sigmoid(hf)
    f = lb + (1.0 - lb) * sg
    k = (1.0 - lb) * (1.0 - sg)
    return q, f, k


def _hgrn_level_ids():
    c = HG_CHUNK
    t = np.arange(c)[:, None]
    s = np.arange(c)[None, :]
    level = np.zeros((c, c), np.int32)
    for li, m in enumerate(HG_LEVELS):
        same = (t // (2 * m)) == (s // (2 * m))
        level[same & ((t % (2 * m)) >= m) & ((s % (2 * m)) < m)] = li + 1
    level[t == s] = len(HG_LEVELS) + 1
    return level


def _split3_bf16(x):
    hi = x.astype(BF16)
    r1 = x - hi.astype(F32)
    mid = r1.astype(BF16)
    lo = (r1 - mid.astype(F32)).astype(BF16)
    return hi, mid, lo


def _cumsum_rows(tril_b, x):
    return sum(jnp.dot(tril_b, piece, preferred_element_type=F32) for piece in _split3_bf16(x))


def _hgrn_midpoint_factor(b, m):
    c = b.shape[0]
    if 2 * m >= SUBLANES:
        b3 = b.reshape(c // (2 * m), 2 * m, LANES)
        mid = b3[:, m - 1:m, :]
    else:
        b3 = b.reshape(c // SUBLANES, SUBLANES, LANES)
        sub = lax.broadcasted_iota(jnp.int32, b3.shape, 1)
        mid = b3[:, m - 1:m, :]
        for blk in range(1, SUBLANES // (2 * m)):
            lo = blk * 2 * m
            mid = jnp.where(sub >= lo, b3[:, lo + m - 1:lo + m, :], mid)
    return jnp.exp(-jnp.abs(b3 - mid)).reshape(c, LANES)


def _hgrn_prompt_kernel(layer, hin_ref, wq_ref, wf_ref, wi_ref, wg_ref, lbl_ref, nw_ref, lvl_ref,
                        out_ref, st_ref, lf_s, q_s, k_s, v_s, o_s):
    c = HG_CHUNK
    t_len = hin_ref.shape[0]
    proj = _project(hin_ref, (wq_ref, wf_ref, wi_ref, wg_ref))
    q, f, k = _hgrn_gates(layer, proj[:, 0:HG_DK], proj[:, HG_DK:2 * HG_DK], lbl_ref)
    lf_s[...] = jnp.log(f)
    q_s[...] = q
    k_s[...] = k
    v_s[...] = proj[:, 2 * HG_DK:3 * HG_DK].astype(BF16)
    gate = proj[:, 3 * HG_DK:]

    lvl = lvl_ref[...]
    tril_b = (lax.broadcasted_iota(jnp.int32, (c, c), 0) >= lax.broadcasted_iota(jnp.int32, (c, c), 1)).astype(BF16)

    st = jnp.zeros((HG_DK, HG_DK), F32)
    for ci in range(t_len // c):
        sl = slice(ci * c, (ci + 1) * c)
        qc = q_s[sl, :]
        kc = k_s[sl, :]
        vb = v_s[sl, :]
        b = _cumsum_rows(tril_b, lf_s[sl, :])
        att = jnp.where(lvl == len(HG_LEVELS) + 1, _nt_dot(qc.astype(BF16), kc.astype(BF16)), 0.0)
        for li, m in enumerate(HG_LEVELS):
            e = _hgrn_midpoint_factor(b, m)
            a_l = _nt_dot((qc * e).astype(BF16), (kc * e).astype(BF16))
            att = jnp.where(lvl == li + 1, a_l, att)
        o = (jnp.dot(att.astype(BF16), vb, preferred_element_type=F32)
             + _nt_dot((qc * jnp.exp(b)).astype(BF16), st.astype(BF16)))
        o_s[sl, :] = o
        bl = b[c - 1:c, :]
        kdec = (kc * jnp.exp(bl - b)).astype(BF16)
        st = st * jnp.exp(bl) + _tn_dot(vb, kdec)
    o = o_s[...]
    out_ref[...] = (_rms_rows(o) * nw_ref[...] * jax.nn.silu(gate)).astype(out_ref.dtype)
    st_ref[...] = st.T


def hgrn_prompt(layer, hin, w_in_t, p):
    level = _hgrn_level_ids()
    return pl.pallas_call(
        functools.partial(_hgrn_prompt_kernel, layer),
        out_shape=(jax.ShapeDtypeStruct((BATCH * SEQ, BRANCH_W), BF16),
                   jax.ShapeDtypeStruct((BATCH, HG_HEADS, HG_DK, HG_DK), F32)),
        grid=(BATCH, HG_HEADS),
        in_specs=[pl.BlockSpec((SEQ, D_MODEL), lambda b, h: (b, 0))]
                 + _w_in_row_specs(layer, (OFF_HQ, OFF_HF, OFF_HI, OFF_HG), _prompt_unit_rows)
                 + [pl.BlockSpec((DEPTH, HG_DK), lambda b, h: (0, h)),
                    pl.BlockSpec((1, HG_DK), lambda b, h: (0, 0)),
                    pl.BlockSpec(level.shape, lambda b, h: (0, 0))],
        out_specs=(pl.BlockSpec((SEQ, HG_DK), lambda b, h: (b, h)),
                   pl.BlockSpec((None, None, HG_DK, HG_DK), lambda b, h: (b, h, 0, 0))),
        scratch_shapes=[pltpu.VMEM((SEQ, HG_DK), F32)] * 3
                       + [pltpu.VMEM((SEQ, HG_DK), BF16), pltpu.VMEM((SEQ, HG_DK), F32)],
        compiler_params=_cparams(2),
        name="hgrn_prompt",
    )(hin, w_in_t, w_in_t, w_in_t, w_in_t, p["hg_lb_logits"], p["hg_norm_w"].reshape(1, HG_DK),
      jnp.asarray(level))


def _sample_state_step(s_ref, so_ref, o_s, decay_of, q, k, v):
    nb = k.shape[0]
    kt = _rows_to_cols(k)[:, :nb].astype(BF16)
    qb = q.astype(BF16)
    rowid = lax.broadcasted_iota(jnp.int32, v.shape, 0)
    for j in range(nb):
        v_j = jnp.where(rowid == j, v, 0.0).astype(BF16)
        s_new = decay_of(j) * s_ref[j] + jnp.dot(kt, v_j, preferred_element_type=F32)
        so_ref[j] = s_new
        o_s[j:j + 1, :] = jnp.dot(qb, s_new.astype(BF16), preferred_element_type=F32)[j:j + 1, :]


def _hgrn_sample_kernel(layer, q_ref, f_ref, i_ref, g_ref, lbl_ref, nw_ref, s_ref, out_ref, so_ref, o_s):
    q, f, k = _hgrn_gates(layer, q_ref[...], f_ref[...], lbl_ref)
    fc = _rows_to_cols(f)
    _sample_state_step(s_ref, so_ref, o_s, lambda j: fc[:, j:j + 1], q, k, i_ref[...])
    out_ref[...] = _rms_rows(o_s[...]) * nw_ref[...] * jax.nn.silu(g_ref[...])


def hgrn_sample(layer, proj, p, state, state_out):
    cb = lambda off: off // HG_DK
    nb = SAMPLE_BLK
    col = lambda off: pl.BlockSpec((nb, HG_DK), lambda h, i: (i, cb(off) + h))
    st_spec = pl.BlockSpec((None, nb, None, HG_DK, HG_DK), lambda h, i: (layer, i, h, 0, 0))
    args = [proj, proj, proj, proj, p["hg_lb_logits"], p["hg_norm_w"].reshape(1, HG_DK), state]
    in_specs = [col(OFF_HQ), col(OFF_HF), col(OFF_HI), col(OFF_HG),
                pl.BlockSpec((DEPTH, HG_DK), lambda h, i: (0, h)),
                pl.BlockSpec((1, HG_DK), lambda h, i: (0, 0)),
                st_spec]
    aliases = {}
    kern = functools.partial(_hgrn_sample_kernel, layer)
    if state_out is not None:
        args.append(state_out)
        in_specs.append(pl.BlockSpec(memory_space=pl.ANY))
        aliases = {len(args) - 1: 1}
        kern = functools.partial(_drop_alias_arg, kern, 7)
    return pl.pallas_call(
        kern,
        out_shape=(jax.ShapeDtypeStruct((DEC_BATCH, BRANCH_W), F32),
                   jax.ShapeDtypeStruct(state.shape, F32)),
        grid=(HG_HEADS, DEC_BATCH // nb),
        in_specs=in_specs,
        out_specs=(pl.BlockSpec((nb, HG_DK), lambda h, i: (i, h)), st_spec),
        scratch_shapes=[pltpu.VMEM((nb, HG_DK), F32)],
        input_output_aliases=aliases,
        compiler_params=_cparams(2),
        name="hgrn_sample",
    )(*args)


def _drop_alias_arg(kern, pos, *refs):
    return kern(*refs[:pos], *refs[pos + 1:])


def _head_pair(cols, h0, lane_lo):
    return jnp.where(lane_lo, cols[:, h0:h0 + 1], cols[:, h0 + 1:h0 + 2])


def _ssd_prompt_kernel(z_ref, x_ref, bc_ref, dt_ref, cw_ref, cb_ref, dtb_ref, alog_ref, dpar_ref, nw_ref,
                       out_ref, st_ref, cst_ref, cx_s, cbc_s, s_s):
    c = SSD_CHUNK
    ci = pl.program_id(1)

    @pl.when(ci == 0)
    def _():
        cx_s[...] = jnp.zeros_like(cx_s)
        cbc_s[...] = jnp.zeros_like(cbc_s)
        s_s[...] = jnp.zeros_like(s_s)

    cw = cw_ref[...]
    cbias = cb_ref[...]

    def conv_silu(raw, carry_ref, lo, hi):
        xx = jnp.concatenate([carry_ref[...], raw], axis=0)
        y = cbias[:, lo:hi] + cw[CONV_W - 1:CONV_W, lo:hi] * raw
        for d in range(1, CONV_W):
            y = y + cw[CONV_W - 1 - d:CONV_W - d, lo:hi] * pltpu.roll(xx, d, axis=0)[SUBLANES:]
        carry_ref[...] = raw[c - SUBLANES:, :]
        return jax.nn.silu(y)

    x_raw = x_ref[...]
    bc_raw = bc_ref[...]
    xs = conv_silu(x_raw, cx_s, 0, BRANCH_W)
    bc = conv_silu(bc_raw, cbc_s, BRANCH_W, SSD_CONV_DIM)

    dt = jax.nn.softplus(dt_ref[...] + dtb_ref[...])
    a_neg = -jnp.exp(alog_ref[...])
    logd = dt * a_neg
    tri = lax.broadcasted_iota(jnp.int32, (c, c), 0) >= lax.broadcasted_iota(jnp.int32, (c, c), 1)
    b = _cumsum_rows(tri.astype(BF16), logd)
    b_t = b.T
    bl = b[c - 1:c, :]
    e_in = jnp.exp(b)
    w_out = jnp.exp(bl - b)
    e_last = jnp.exp(bl)
    dfull = dpar_ref[...]

    lane_lo =lax.broadcasted_iota(jnp.int32, (c, LANES), 1) < SSD_HEADDIM
    lane_lo_row = lax.broadcasted_iota(jnp.int32, (1, LANES), 1) < SSD_HEADDIM

    ys = []
    for g in range(SSD_GROUPS):
        bm = bc[:, g * SSD_STATE:(g + 1) * SSD_STATE].astype(BF16)
        cm = bc[:, (SSD_GROUPS + g) * SSD_STATE:(SSD_GROUPS + g + 1) * SSD_STATE].astype(BF16)
        gmat = _nt_dot(cm, bm)
        for pp in range(SSD_HEADS // SSD_GROUPS // 2):
            pi = g * (SSD_HEADS // SSD_GROUPS // 2) + pp
            h0 = 2 * pi
            xs_p = xs[:, pi * LANES:(pi + 1) * LANES]
            vdt = xs_p * _head_pair(dt, h0, lane_lo)
            vdt_b = vdt.astype(BF16)
            o_heads = []
            for hh in (h0, h0 + 1):
                diff = b[:, hh:hh + 1] - b_t[hh:hh + 1, :]
                dec = jnp.where(tri, jnp.exp(jnp.where(tri, diff, 0.0)), 0.0)
                o_heads.append(jnp.dot((gmat * dec).astype(BF16), vdt_b, preferred_element_type=F32))
            o_intra = jnp.where(lane_lo, o_heads[0], o_heads[1])
            s_p = s_s[pi]
            o_inter = _head_pair(e_in, h0, lane_lo) * jnp.dot(cm, s_p.astype(BF16), preferred_element_type=F32)
            ys.append(o_intra + o_inter + dfull[:, pi * LANES:(pi + 1) * LANES] * xs_p)
            upd = _tn_dot(bm, (vdt * _head_pair(w_out, h0, lane_lo)).astype(BF16))
            s_s[pi] = s_p * _head_pair(e_last, h0, lane_lo_row) + upd

    y = jnp.concatenate(ys, axis=1) * jax.nn.silu(z_ref[...])
    gw = BRANCH_W // SSD_GROUPS
    nw = nw_ref[...]
    outs = [_rms_rows(y[:, g * gw:(g + 1) * gw]) * nw[:, g * gw:(g + 1) * gw] for g in range(SSD_GROUPS)]
    out_ref[...] = jnp.concatenate(outs, axis=1).astype(out_ref.dtype)

    @pl.when(ci == pl.num_programs(1) - 1)
    def _():
        for pi in range(SSD_HEADS // 2):
            s_t = s_s[pi].T
            st_ref[2 * pi] = s_t[:SSD_HEADDIM, :]
            st_ref[2 * pi + 1] = s_t[SSD_HEADDIM:, :]
        cst_ref[:, 0:BRANCH_W] = x_raw[c - (CONV_W - 1):, :]
        cst_ref[:, BRANCH_W:SSD_CONV_DIM] = bc_raw[c - (CONV_W - 1):, :]


def _pad_lanes(v):
    return jnp.pad(v.astype(F32), (0, LANES - v.shape[0])).reshape(1, LANES)


def _ssd_params(p):
    return (p["ssd_conv_w"], p["ssd_conv_b"].reshape(1, SSD_CONV_DIM), _pad_lanes(p["ssd_dt_bias"]),
            _pad_lanes(p["ssd_a_log"]), jnp.repeat(p["ssd_d"].astype(F32), SSD_HEADDIM).reshape(1, BRANCH_W),
            p["ssd_norm_w"].reshape(1, BRANCH_W))


def ssd_prompt(proj, p):
    c = SSD_CHUNK
    nc = SEQ // c
    const = lambda shape: pl.BlockSpec(shape, lambda b, i: (0, 0))
    rowblk = lambda w, off: pl.BlockSpec((c, w), lambda b, i: (b * nc + i, (off - OFF_SZ) // w))
    return pl.pallas_call(
        _ssd_prompt_kernel,
        out_shape=(jax.ShapeDtypeStruct((BATCH * SEQ, BRANCH_W), BF16),
                   jax.ShapeDtypeStruct((BATCH, SSD_HEADS, SSD_HEADDIM, SSD_STATE), F32),
                   jax.ShapeDtypeStruct((BATCH, CONV_W - 1, SSD_CONV_DIM), F32)),
        grid=(BATCH, nc),
        in_specs=[rowblk(BRANCH_W, OFF_SZ), rowblk(BRANCH_W, OFF_SX), rowblk(SSD_BC, OFF_SBC),
                  rowblk(LANES, OFF_SDT),
                  const((CONV_W, SSD_CONV_DIM)), const((1, SSD_CONV_DIM)), const((1, LANES)),
                  const((1, LANES)), const((1, BRANCH_W)), const((1, BRANCH_W))],
        out_specs=(pl.BlockSpec((c, BRANCH_W), lambda b, i: (b * nc + i, 0)),
                   pl.BlockSpec((None, SSD_HEADS, SSD_HEADDIM, SSD_STATE), lambda b, i: (b, 0, 0, 0)),
                   pl.BlockSpec((None, CONV_W - 1, SSD_CONV_DIM), lambda b, i: (b, 0, 0))),
        scratch_shapes=[pltpu.VMEM((SUBLANES, BRANCH_W), F32), pltpu.VMEM((SUBLANES, SSD_BC), F32),
                        pltpu.VMEM((SSD_HEADS // 2, SSD_STATE, LANES), F32)],
        compiler_params=_cparams(2),
        name="ssd_prompt",
    )(proj, proj, proj, proj, *_ssd_params(p))


def _ssd_sample_kernel(z_ref, x_ref, bc_ref, dt_ref, cw_ref, cb_ref, dtb_ref, alog_ref, dpar_ref, nw_ref,
                       bufx_ref, bufbc_ref, s_ref, out_ref, so_ref, nbx_ref, nbbc_ref, y_s):
    cw = cw_ref[...]
    cbias = cb_ref[...]

    def conv_silu(raw, buf_ref, nbuf_ref, lo, hi):
        y = cbias[:, lo:hi] + cw[CONV_W - 1:CONV_W, lo:hi] * raw
        for j in range(CONV_W - 1):
            y = y + cw[j:j + 1, lo:hi] * buf_ref[j]
        for j in range(CONV_W - 2):
            nbuf_ref[j] = buf_ref[j + 1]
        nbuf_ref[CONV_W - 2] = raw
        return jax.nn.silu(y)

    xs = conv_silu(x_ref[...], bufx_ref, nbx_ref, 0, BRANCH_W)
    bc = conv_silu(bc_ref[...], bufbc_ref, nbbc_ref, BRANCH_W, SSD_CONV_DIM)
    dt = jax.nn.softplus(dt_ref[...] + dtb_ref[...])
    decay = jnp.exp(dt * (-jnp.exp(alog_ref[...])))
    nb = xs.shape[0]
    hpg = SSD_HEADS // SSD_GROUPS
    lane_lo = lax.broadcasted_iota(jnp.int32, (nb, LANES), 1) < SSD_HEADDIM
    for pi in range(SSD_HEADS // 2):
        h0 = 2 * pi
        g = h0 // hpg
        xdt_cols = _rows_to_cols(xs[:, pi * LANES:(pi + 1) * LANES] * _head_pair(dt, h0, lane_lo))
        for e in range(2):
            h = h0 + e
            for j in range(nb):
                xcol = xdt_cols[e * SSD_HEADDIM:(e + 1) * SSD_HEADDIM, j:j + 1]
                brow = bc[j:j + 1, g * SSD_STATE:(g + 1) * SSD_STATE]
                so_ref[j, h] = s_ref[j, h] * decay[j:j + 1, h:h + 1] + xcol * brow
    for g in range(SSD_GROUPS):
        cm = bc[:, (SSD_GROUPS + g) * SSD_STATE:(SSD_GROUPS + g + 1) * SSD_STATE].astype(BF16)
        for j in range(nb):
            s_new = so_ref[j, g * hpg:(g + 1) * hpg].reshape(hpg * SSD_HEADDIM, SSD_STATE)
            y_s[j:j + 1, g * hpg * SSD_HEADDIM:(g + 1) * hpg * SSD_HEADDIM] = _nt_dot(cm, s_new.astype(BF16))[j:j + 1, :]
    y = (y_s[...] + dpar_ref[...] * xs) * jax.nn.silu(z_ref[...])
    gw = BRANCH_W // SSD_GROUPS
    nw = nw_ref[...]
    outs = [_rms_rows(y[:, g * gw:(g + 1) * gw]) * nw[:, g * gw:(g + 1) * gw] for g in range(SSD_GROUPS)]
    out_ref[...] = jnp.concatenate(outs, axis=1)


def ssd_sample(layer, proj, p, state, state_out, buf_t):
    nb = SSD_SAMPLE_BLK
    const = lambda shape: pl.BlockSpec(shape, lambda i: (0,) * len(shape))
    rowblk = lambda w, off: pl.BlockSpec((nb, w), lambda i: (i, off // w))
    st_spec = pl.BlockSpec((None, nb, SSD_HEADS, SSD_HEADDIM, SSD_STATE), lambda i: (layer, i, 0, 0, 0))
    bufx_spec = pl.BlockSpec((CONV_W - 1, nb, BRANCH_W), lambda i: (0, i, 0))
    bufbc_spec = pl.BlockSpec((CONV_W - 1, nb, SSD_BC), lambda i: (0, i, BRANCH_W // SSD_BC))
    args = [proj, proj, proj, proj, *_ssd_params(p), buf_t, buf_t, state]
    in_specs = [rowblk(BRANCH_W, OFF_SZ), rowblk(BRANCH_W, OFF_SX), rowblk(SSD_BC, OFF_SBC), rowblk(LANES, OFF_SDT),
                const((CONV_W, SSD_CONV_DIM)), const((1, SSD_CONV_DIM)), const((1, LANES)), const((1, LANES)),
                const((1, BRANCH_W)), const((1, BRANCH_W)), bufx_spec, bufbc_spec, st_spec]
    aliases = {}
    kern = _ssd_sample_kernel
    if state_out is not None:
        args.append(state_out)
        in_specs.append(pl.BlockSpec(memory_space=pl.ANY))
        aliases = {len(args) - 1: 1}
        kern = functools.partial(_drop_alias_arg, kern, 13)
    out, st, nbx, nbbc = pl.pallas_call(
        kern,
        out_shape=(jax.ShapeDtypeStruct((DEC_BATCH, BRANCH_W), F32),
                   jax.ShapeDtypeStruct(state.shape, F32),
                   jax.ShapeDtypeStruct((CONV_W - 1, DEC_BATCH, BRANCH_W), F32),
                   jax.ShapeDtypeStruct((CONV_W - 1, DEC_BATCH, SSD_BC), F32)),
        grid=(DEC_BATCH // nb,),
        in_specs=in_specs,
        out_specs=(pl.BlockSpec((nb, BRANCH_W), lambda i: (i, 0)), st_spec,
                   pl.BlockSpec((CONV_W - 1, nb, BRANCH_W), lambda i: (0, i, 0)),
                   pl.BlockSpec((CONV_W - 1, nb, SSD_BC), lambda i: (0, i, 0))),
        scratch_shapes=[pltpu.VMEM((nb, BRANCH_W), F32)],
        input_output_aliases=aliases,
        compiler_params=_cparams(1),
        name="ssd_sample",
    )(*args)
    return out, st, jnp.concatenate([nbx, nbbc], axis=-1)


def _ret_prompt_kernel(hin_ref, wq_ref, wk_ref, wv_ref, wg_ref, cos_ref, sin_ref, lg_ref, out_ref, st_ref,
                       q_s, k_s, v_s, o_s):
    c = RET_CHUNK
    t_len = hin_ref.shape[0]
    proj = _project(hin_ref, (wq_ref, wk_ref, wv_ref, wg_ref))
    cos = cos_ref[...]
    sin = sin_ref[...]
    q_s[...] = _rope(proj[:, 0:RET_DK], cos, sin)
    k_s[...] = _rope(proj[:, RET_DK:2 * RET_DK], cos, sin) * RET_DK ** -0.5
    v_s[...] = proj[:, 2 * RET_DK:3 * RET_DK].astype(BF16)
    gate = proj[:, 3 * RET_DK:]
    lg = lg_ref[...]
    lg128 = lg[:, :LANES]
    ti = lax.broadcasted_iota(jnp.int32, (c, c), 0)
    si = lax.broadcasted_iota(jnp.int32, (c, c), 1)
    tri = ti >= si
    dec = jnp.where(tri, jnp.exp(jnp.where(tri, (ti - si).astype(F32) * lg, 0.0)), 0.0)
    tt = lax.broadcasted_iota(jnp.int32, (c, LANES), 0).astype(F32)
    g_in = jnp.exp((tt + 1.0) * lg128)
    g_out = jnp.exp((c - 1.0 - tt) * lg128)
    g_all = jnp.exp(float(c) * lg128)
    s = jnp.zeros((RET_DK, RET_DK), F32)
    for ci in range(t_len // c):
        sl = slice(ci * c, (ci + 1) * c)
        qc = q_s[sl, :]
        kc = k_s[sl, :]
        vb = v_s[sl, :]
        scores = _nt_dot(qc.astype(BF16), kc.astype(BF16)) * dec
        o_s[sl, :] = (jnp.dot(scores.astype(BF16), vb, preferred_element_type=F32)
                      + jnp.dot((qc * g_in).astype(BF16), s.astype(BF16), preferred_element_type=F32))
        s = s * g_all + _tn_dot((kc * g_out).astype(BF16), vb)
    out_ref[...] = (_rms_rows(o_s[...]) * jax.nn.silu(gate)).astype(out_ref.dtype)
    st_ref[...] = s


def _log_gamma_rows(width):
    lg = jnp.log1p(-jnp.exp2(-5.0 - jnp.arange(RET_HEADS, dtype=F32)))
    return jnp.broadcast_to(lg[:, None, None], (RET_HEADS, 1, width))


def ret_prompt(layer, hin, w_in_t, cos, sin):
    tab = pl.BlockSpec((SEQ, RET_DK), lambda b, h: (0, 0))
    ret_rows = tuple(RET_COL0 + off for off in (OFF_RQ, OFF_RK, OFF_RV, OFF_RG))
    return pl.pallas_call(
        _ret_prompt_kernel,
        out_shape=(jax.ShapeDtypeStruct((BATCH * SEQ, BRANCH_W), BF16),
                   jax.ShapeDtypeStruct((BATCH, RET_HEADS, RET_DK, RET_DK), F32)),
        grid=(BATCH, RET_HEADS),
        in_specs=[pl.BlockSpec((SEQ, D_MODEL), lambda b, h: (b, 0))]
                 + _w_in_row_specs(layer, ret_rows, _prompt_unit_rows)
                 + [tab, tab, pl.BlockSpec((None, 1, RET_CHUNK), lambda b, h: (h, 0, 0))],
        out_specs=(pl.BlockSpec((SEQ, RET_DK), lambda b, h: (b, h)),
                   pl.BlockSpec((None, None, RET_DK, RET_DK), lambda b, h: (b, h, 0, 0))),
        scratch_shapes=[pltpu.VMEM((SEQ, RET_DK), F32), pltpu.VMEM((SEQ, RET_DK), F32),
                        pltpu.VMEM((SEQ, RET_DK), BF16), pltpu.VMEM((SEQ, RET_DK), F32)],
        compiler_params=_cparams(2),
        name="ret_prompt",
    )(hin, w_in_t, w_in_t, w_in_t, w_in_t, cos, sin, _log_gamma_rows(RET_CHUNK))


def _ret_sample_kernel(q_ref, k_ref, v_ref, g_ref, cos_ref, sin_ref, lg_ref, s_ref, out_ref, so_ref, o_s):
    cos = cos_ref[0:1, :]
    sin = sin_ref[0:1, :]
    q = _rope(q_ref[...], cos, sin)
    k = _rope(k_ref[...], cos, sin) * RET_DK ** -0.5
    v = v_ref[...]
    gamma = jnp.exp(lg_ref[...])
    _sample_state_step(s_ref, so_ref, o_s, lambda j: gamma, q, k, v)
    out_ref[...] = _rms_rows(o_s[...]) * jax.nn.silu(g_ref[...])


def ret_sample(layer, proj, cos, sin, state, state_out):
    cb = lambda off: off // RET_DK
    nb = SAMPLE_BLK
    col = lambda off: pl.BlockSpec((nb, RET_DK), lambda h, i: (i, cb(off) + h))
    tab = pl.BlockSpec((SUBLANES, RET_DK), lambda h, i: (0, 0))
    st_spec = pl.BlockSpec((None, nb, None, RET_DK, RET_DK), lambda h, i: (layer, i, h, 0, 0))
    args = [proj, proj, proj, proj, cos, sin, _log_gamma_rows(LANES), state]
    in_specs = [col(OFF_RQ), col(OFF_RK), col(OFF_RV), col(OFF_RG), tab, tab,
                pl.BlockSpec((None, 1, LANES), lambda h, i: (h, 0, 0)), st_spec]
    aliases = {}
    kern = _ret_sample_kernel
    if state_out is not None:
        args.append(state_out)
        in_specs.append(pl.BlockSpec(memory_space=pl.ANY))
        aliases = {len(args) - 1: 1}
        kern = functools.partial(_drop_alias_arg, kern, len(args) - 1)
    return pl.pallas_call(
        kern,
        out_shape=(jax.ShapeDtypeStruct((DEC_BATCH, BRANCH_W), F32),
                   jax.ShapeDtypeStruct(state.shape, F32)),
        grid=(RET_HEADS, DEC_BATCH // nb),
        in_specs=in_specs,
        out_specs=(pl.BlockSpec((nb, RET_DK), lambda h, i: (i, h)), st_spec),
        scratch_shapes=[pltpu.VMEM((nb, RET_DK), F32)],
        input_output_aliases=aliases,
        compiler_params=_cparams(2),
        name="ret_sample",
    )(*args)


def kernel(x_prompt, x_sample, state_lru_h, state_lru_conv, state_hgrn, state_ssd, state_ssd_conv, state_ret, state_ffn_conv, g_mix, g_ffn, w_in, lru_conv_w, lru_conv_b, lru_wa, lru_ba, lru_wx, lru_bx, lru_lambda, hg_lb_logits, hg_norm_w, ssd_conv_w, ssd_conv_b, ssd_dt_bias, ssd_a_log, ssd_d, ssd_norm_w, w_branch, w_gate, w_out, ffn_w_up, ffn_w_val, ffn_conv_w, ffn_conv_b, ffn_w_down, g_final):
    xp = x_prompt.reshape(BATCH * SEQ, D_MODEL)
    xs = x_sample.reshape(DEC_BATCH, D_MODEL)

    cos_p, sin_p = rope_tables(SEQ, 0, True)
    cos_s, sin_s = rope_tables(SUBLANES, PAST_LEN, False)

    hp = rmsnorm(xp, g_mix[0], BF16, NORM_TM)
    hs = rmsnorm(xs, g_mix[0], BF16, DEC_BATCH)

    state_ssd_t = jnp.swapaxes(state_ssd, -1, -2)

    w_in_t = jnp.swapaxes(w_in, 1, 2)
    w_gate_b = jnp.transpose(w_gate, (0, 2, 1, 3)).astype(BF16)
    w_out_b = w_out.astype(BF16)
    w_down_b = ffn_w_down.astype(BF16)

    prompt_states, sample_small = [], []
    hg_out = ssd_out = ret_out = None
    for l in range(DEPTH):
        p = {"lru_conv_w": lru_conv_w[l], "lru_conv_b": lru_conv_b[l], "lru_wa": lru_wa[l], "lru_ba": lru_ba[l],
             "lru_wx": lru_wx[l], "lru_bx": lru_bx[l], "lru_lambda": lru_lambda[l],
             "hg_lb_logits": hg_lb_logits, "hg_norm_w": hg_norm_w[l],
             "ssd_conv_w": ssd_conv_w[l], "ssd_conv_b": ssd_conv_b[l], "ssd_dt_bias": ssd_dt_bias[l],
             "ssd_a_log": ssd_a_log[l], "ssd_d": ssd_d[l], "ssd_norm_w": ssd_norm_w[l]}
        last = l == DEPTH - 1
        g_next = g_final if last else g_mix[l + 1]

        out_a, lru_h_p, lru_conv_p = lru_prompt(l, hp, w_in_t, p)
        out_b, hg_p = hgrn_prompt(l, hp, w_in_t, p)
        proj_ssd_p = in_proj_prompt(l, hp, w_in_t, OFF_SZ, SSD_TILES, SSD_TN, SSD_PROJ_TM)
        out_c, ssd_p, ssd_conv_p = ssd_prompt(proj_ssd_p, p)
        out_d, ret_p = ret_prompt(l, hp, w_in_t, cos_p, sin_p)
        merged, w_branch_b = gated_merge_prompt(l, hp, (out_a, out_b, out_c, out_d), w_gate_b, w_branch,
                                                MERGE_TM, MERGE_TN)
        xp, h2 = out_proj_residual_norm(l, merged, w_out_b, xp, g_ffn[l], OUT_PROJ_TM)
        act, ffn_conv_p, w_up_b, w_val_b = ffn_prompt(l, h2, ffn_w_up, ffn_w_val, ffn_conv_w[l], ffn_conv_b[l], FFN_TN)
        res = down_proj_residual_norm(l, act, w_down_b, xp, g_next, DOWN_PROJ_TM, DOWN_PROJ_TK, not last,
                                      F32 if last else BF16)
        if last:
            (yp,) = res
        else:
            xp, hp = res
        prompt_states.append((lru_h_p, lru_conv_p, hg_p, ssd_p, ssd_conv_p, ret_p, ffn_conv_p))

        proj_s = in_proj_sample(l, hs, w_in_t, 0, MAIN_TILES, MAIN_TN)
        proj_ret_s = in_proj_sample(l, hs, w_in_t, RET_COL0, RET_TILES, RET_TN)
        lru_buf_t = jnp.swapaxes(state_lru_conv[l], 0, 1)
        ssd_buf_t = jnp.swapaxes(state_ssd_conv[l], 0, 1)
        s_a, lru_h_s, lru_nbuf = lru_sample(proj_s, p, state_lru_h[l], lru_buf_t)
        s_b, hg_out = hgrn_sample(l, proj_s, p, state_hgrn, hg_out)
        s_c, ssd_out, ssd_nbuf = ssd_sample(l, proj_s, p, state_ssd_t, ssd_out, ssd_buf_t)
        s_d, ret_out = ret_sample(l, proj_ret_s, cos_s, sin_s, state_ret, ret_out)
        merged_s = gated_merge_sample(l, hs, (s_a, s_b, s_c, s_d), w_gate_b, w_branch_b, MERGE_TN)
        xs, h2s = out_proj_residual_norm(l, merged_s, w_out_b, xs, g_ffn[l], DEC_BATCH)
        act_s, ffn_nbuf = ffn_sample(l, h2s, w_up_b, w_val_b, ffn_conv_w[l], ffn_conv_b[l], state_ffn_conv,
                                       FFN_SAMPLE_TN)
        res = down_proj_residual_norm(l, act_s, w_down_b, xs, g_next, DEC_BATCH, DOWN_PROJ_TK, not last,
                                      F32 if last else BF16)
        if last:
            (ys,) = res
        else:
            xs, hs = res
        sample_small.append((lru_h_s, jnp.swapaxes(lru_nbuf, 0, 1), jnp.swapaxes(ssd_nbuf, 0, 1), ffn_nbuf))

    stack_p = lambda i: jnp.stack([st[i] for st in prompt_states], axis=0)
    stack_s = lambda i: jnp.stack([st[i] for st in sample_small], axis=0)
    return (yp.reshape(BATCH, SEQ, D_MODEL), ys.reshape(DEC_BATCH, 1, D_MODEL),
            stack_p(0), stack_s(0), stack_p(1), stack_s(1),
            stack_p(2), hg_out, jnp.swapaxes(stack_p(3), -1, -2), jnp.swapaxes(ssd_out, -1, -2),
            stack_p(4), stack_s(2), stack_p(5), ret_out,
            stack_p(6), stack_s(3))
```

```python
import functools

import numpy as np
import jax
import jax.numpy as jnp
from jax import lax
from jax.experimental import pallas as pl
from jax.experimental.pallas import tpu as pltpu

F32 = jnp.float32
BF16 = jnp.bfloat16

D_MODEL = 2048
BATCH = 4
SEQ = 2048
DEPTH = 2
DEC_BATCH = 128
PAST_LEN = 16384
BRANCH_W = D_MODEL // 2
EPS = 1e-6
LRU_BLOCKS = 8
LRU_BLOCK = BRANCH_W // LRU_BLOCKS
LRU_C = 8.0
CONV_W = 4
HG_HEADS = 8
HG_DK = BRANCH_W // HG_HEADS
SSD_HEADDIM = 64
SSD_HEADS = BRANCH_W // SSD_HEADDIM
SSD_GROUPS = 2
SSD_STATE = 128
SSD_BC = 2 * SSD_GROUPS * SSD_STATE
SSD_CONV_DIM = BRANCH_W + SSD_BC
RET_HEADS = 8
RET_DK = BRANCH_W // RET_HEADS
ROPE_BASE = 10000.0
D_FF = 5632
FFN_CONV_W = 3

V7X_VMEM_BYTES = 64 * 1024 * 1024
VMEM_LIMIT_BYTES = V7X_VMEM_BYTES - 8 * 1024 * 1024
LANES = 128
SUBLANES = 8

N_BRANCH = 4
N_IN = 12816
OFF_XA, OFF_YA = 0, 1024
OFF_HQ, OFF_HF, OFF_HI, OFF_HG = 2048, 3072, 4096, 5120
OFF_SZ, OFF_SX, OFF_SBC = 6144, 7168, 8192
OFF_SDT = 8704
MAIN_TN, MAIN_TILES = 1280, 7
N_MAIN = MAIN_TN * MAIN_TILES
RET_COL0 = OFF_SDT + SSD_HEADS
RET_TN, RET_TILES = 1024, 4
SSD_TN, SSD_TILES = 896, 3
SSD_PROJ_W = SSD_TN * SSD_TILES
OFF_RQ, OFF_RK, OFF_RV, OFF_RG = 0, 1024, 2048, 3072
assert RET_COL0 + RET_TN * RET_TILES == N_IN and OFF_SZ + SSD_PROJ_W >= OFF_SDT + LANES

NORM_TM = 512
SSD_PROJ_TM = 1024
MERGE_TM, MERGE_TN = 1024, 256
OUT_PROJ_TM = 512
FFN_TN = 512
FFN_ROW_CHUNK = 512
DOWN_PROJ_TM = 512
DOWN_PROJ_TK = 2816
FFN_SAMPLE_TN = 512

HG_CHUNK = 128
HG_LEVELS = (1, 2, 4, 8, 16, 32, 64)
SSD_CHUNK = 128
RET_CHUNK = 256
SAMPLE_BLK = 128
SSD_SAMPLE_BLK = 16


def _cparams(n_axes):
    return pltpu.CompilerParams(dimension_semantics=("arbitrary",) * n_axes,
                                vmem_limit_bytes=VMEM_LIMIT_BYTES)


def _rms_rows(x):
    return x * lax.rsqrt(jnp.mean(x * x, axis=-1, keepdims=True) + EPS)


def _shift_rows(x, d, row):
    return jnp.where(row >= d, pltpu.roll(x, d, axis=0), 0.0)


def _nt_dot(a, b):
    return lax.dot_general(a, b, (((1,), (1,)), ((), ())), preferred_element_type=F32)


def _tn_dot(a, b):
    return lax.dot_general(a, b, (((0,), (0,)), ((), ())), preferred_element_type=F32)


def _project(h_ref, w_refs):
    w = jnp.concatenate([w_ref[0].astype(BF16) for w_ref in w_refs], axis=0)
    return _nt_dot(h_ref[...], w)


def _w_in_row_specs(layer, offsets, index_map_for):
    return [pl.BlockSpec((pl.Element(1), pl.Element(LANES), pl.Element(D_MODEL)), index_map_for(layer, off))
            for off in offsets]


def _rows_to_cols(x):
    n = x.shape[0]
    if n < LANES:
        x = jnp.concatenate([x, jnp.zeros((LANES - n, x.shape[1]), x.dtype)], axis=0)
    return x.T


def _norm_kernel(x_ref, g_ref, o_ref):
    o_ref[...] = (_rms_rows(x_ref[...]) * g_ref[...]).astype(o_ref.dtype)


def rmsnorm(x, g, out_dtype, tm):
    m, d = x.shape
    return pl.pallas_call(
        _norm_kernel,
        out_shape=jax.ShapeDtypeStruct((m, d), out_dtype),
        grid=(m // tm,),
        in_specs=[pl.BlockSpec((tm, d), lambda i: (i, 0)),
                  pl.BlockSpec((1, d), lambda i: (0, 0))],
        out_specs=pl.BlockSpec((tm, d), lambda i: (i, 0)),
        compiler_params=_cparams(1),
        name="rmsnorm",
    )(x, g.reshape(1, d))


def _mm_nt_cast_kernel(a_ref, wt_ref, o_ref, wb_s):
    @pl.when(pl.program_id(1) == 0)
    def _():
        wb_s[...] = wt_ref[0].astype(BF16)

    o_ref[...] = _nt_dot(a_ref[...], wb_s[...])


def in_proj_prompt(layer, h, w_in_t, row0, n_tiles, tn, tm):
    m = h.shape[0]
    w_spec = pl.BlockSpec((pl.Element(1), pl.Element(tn), pl.Element(D_MODEL)),
                          lambda j, i: (layer, pl.multiple_of(row0 + j * tn, SUBLANES), 0))
    return pl.pallas_call(
        _mm_nt_cast_kernel,
        out_shape=jax.ShapeDtypeStruct((m, n_tiles * tn), F32),
        grid=(n_tiles, m // tm),
        in_specs=[pl.BlockSpec((tm, D_MODEL), lambda j, i: (i, 0)), w_spec],
        out_specs=pl.BlockSpec((tm, tn), lambda j, i: (i, j)),
        scratch_shapes=[pltpu.VMEM((tn, D_MODEL), BF16)],
        compiler_params=_cparams(2),
        name="in_proj_prompt",
    )(h, w_in_t)


def _mm_nt_castw_kernel(a_ref, wt_ref, o_ref):
    o_ref[...] = _nt_dot(a_ref[...], wt_ref[0].astype(BF16))


def in_proj_sample(layer, h, w_in_t, row0, n_tiles, tn):
    m = h.shape[0]
    w_spec = pl.BlockSpec((pl.Element(1), pl.Element(tn), pl.Element(D_MODEL)),
                          lambda j: (layer, pl.multiple_of(row0 + j * tn, SUBLANES), 0))
    return pl.pallas_call(
        _mm_nt_castw_kernel,
        out_shape=jax.ShapeDtypeStruct((m, n_tiles * tn), F32),
        grid=(n_tiles,),
        in_specs=[pl.BlockSpec((m, D_MODEL), lambda j: (0, 0)), w_spec],
        out_specs=pl.BlockSpec((m, tn), lambda j: (0, j)),
        compiler_params=_cparams(1),
        name="in_proj_sample",
    )(h, w_in_t)


def _gated_sum(h, br_refs, gate_w, branch_w):
    acc = None
    for k, br_ref in enumerate(br_refs):
        gate = jax.nn.sigmoid(jnp.dot(h, gate_w(k), preferred_element_type=F32))
        br = jnp.dot(br_ref[...].astype(BF16), branch_w(k), preferred_element_type=F32)
        acc = gate * br if acc is None else acc + gate * br
    return acc


def _merge_cast_kernel(h_ref, a_ref, b_ref, c_ref, d_ref, wg_ref, wb_ref, o_ref, wbb_ref):
    @pl.when(pl.program_id(1) == 0)
    def _():
        for k in range(N_BRANCH):
            wbb_ref[k] = wb_ref[k].astype(BF16)

    acc = _gated_sum(h_ref[...], (a_ref, b_ref, c_ref, d_ref), lambda k: wg_ref[k], lambda k: wbb_ref[k])
    o_ref[...] = acc.astype(o_ref.dtype)


def gated_merge_prompt(layer, h, branches, w_gate_b, w_branch, tm, tn):
    m = h.shape[0]
    br_specs = [pl.BlockSpec((tm, BRANCH_W), lambda j, i: (i, 0)) for _ in range(N_BRANCH)]
    return pl.pallas_call(
        _merge_cast_kernel,
        out_shape=(jax.ShapeDtypeStruct((m, D_MODEL), BF16),
                   jax.ShapeDtypeStruct((N_BRANCH, BRANCH_W, D_MODEL), BF16)),
        grid=(D_MODEL // tn, m // tm),
        in_specs=[pl.BlockSpec((tm, D_MODEL), lambda j, i: (i, 0))] + br_specs
                 + [pl.BlockSpec((None, N_BRANCH, D_MODEL, tn), lambda j, i: (layer, 0, 0, j)),
                    pl.BlockSpec((None, N_BRANCH, BRANCH_W, tn), lambda j, i: (layer, 0, 0, j))],
        out_specs=(pl.BlockSpec((tm, tn), lambda j, i: (i, j)),
                   pl.BlockSpec((N_BRANCH, BRANCH_W, tn), lambda j, i: (0, 0, j))),
        compiler_params=_cparams(2),
        name="gated_merge_prompt",
    )(h, *branches, w_gate_b, w_branch)


def _merge_kernel(h_ref, a_ref, b_ref, c_ref, d_ref, wgb_ref, wbb_ref, o_ref):
    acc = _gated_sum(h_ref[...], (a_ref, b_ref, c_ref, d_ref), lambda k: wgb_ref[k], lambda k: wbb_ref[k])
    o_ref[...] = acc.astype(o_ref.dtype)


def gated_merge_sample(layer, h, branches, w_gate_b, w_branch_b, tn):
    m = h.shape[0]
    br_specs = [pl.BlockSpec((m, BRANCH_W), lambda j: (0, 0)) for _ in range(N_BRANCH)]
    return pl.pallas_call(
        _merge_kernel,
        out_shape=jax.ShapeDtypeStruct((m, D_MODEL), BF16),
        grid=(D_MODEL // tn,),
        in_specs=[pl.BlockSpec((m, D_MODEL), lambda j: (0, 0))] + br_specs
                 + [pl.BlockSpec((None, N_BRANCH, D_MODEL, tn), lambda j: (layer, 0, 0, j)),
                    pl.BlockSpec((N_BRANCH, BRANCH_W, tn), lambda j: (0, 0, j))],
        out_specs=pl.BlockSpec((m, tn), lambda j: (0, j)),
        compiler_params=_cparams(1),
        name="gated_merge_sample",
    )(h, *branches, w_gate_b, w_branch_b)


def _out_proj_kernel(m_ref, w_ref, x_ref, g_ref, xo_ref, ho_ref):
    x_new = x_ref[...] + jnp.dot(m_ref[...], w_ref[...], preferred_element_type=F32)
    xo_ref[...] = x_new
    ho_ref[...] = (_rms_rows(x_new) * g_ref[...]).astype(ho_ref.dtype)


def out_proj_residual_norm(layer, merged, w_out, x, g, tm):
    m = x.shape[0]
    return pl.pallas_call(
        _out_proj_kernel,
        out_shape=(jax.ShapeDtypeStruct((m, D_MODEL), F32), jax.ShapeDtypeStruct((m, D_MODEL), BF16)),
        grid=(m // tm,),
        in_specs=[pl.BlockSpec((tm, D_MODEL), lambda i: (i, 0)),
                  pl.BlockSpec((None, D_MODEL, D_MODEL), lambda i: (layer, 0, 0)),
                  pl.BlockSpec((tm, D_MODEL), lambda i: (i, 0)),
                  pl.BlockSpec((1, D_MODEL), lambda i: (0, 0))],
        out_specs=(pl.BlockSpec((tm, D_MODEL), lambda i: (i, 0)),
                   pl.BlockSpec((tm, D_MODEL), lambda i: (i, 0))),
        compiler_params=_cparams(1),
        name="out_proj",
    )(merged, w_out, x, g.reshape(1, D_MODEL))


def _down_proj_kernel(emit_x, a_ref, w_ref, x_ref, g_ref, *refs):
    if emit_x:
        xo_ref, no_ref, acc_ref = refs
    else:
        no_ref, acc_ref = refs
    kk = pl.program_id(1)

    @pl.when(kk == 0)
    def _():
        acc_ref[...] = x_ref[...]

    acc_ref[...] += jnp.dot(a_ref[...], w_ref[...], preferred_element_type=F32)

    @pl.when(kk == pl.num_programs(1) - 1)
    def _():
        x_new = acc_ref[...]
        if emit_x:
            xo_ref[...] = x_new
        no_ref[...] = (_rms_rows(x_new) * g_ref[...]).astype(no_ref.dtype)


def down_proj_residual_norm(layer, a, w_down, x, g, tm, tk, emit_x, norm_dtype):
    m = x.shape[0]
    out_shape = [jax.ShapeDtypeStruct((m, D_MODEL), norm_dtype)]
    out_specs = [pl.BlockSpec((tm, D_MODEL), lambda i, k: (i, 0))]
    if emit_x:
        out_shape = [jax.ShapeDtypeStruct((m, D_MODEL), F32)] + out_shape
        out_specs = [pl.BlockSpec((tm, D_MODEL), lambda i, k: (i, 0))] + out_specs
    return pl.pallas_call(
        functools.partial(_down_proj_kernel, emit_x),
        out_shape=tuple(out_shape),
        grid=(m // tm, D_FF // tk),
        in_specs=[pl.BlockSpec((tm, tk), lambda i, k: (i, k)),
                  pl.BlockSpec((None, tk, D_MODEL), lambda i, k: (layer, k, 0)),
                  pl.BlockSpec((tm, D_MODEL), lambda i, k: (i, 0)),
                  pl.BlockSpec((1, D_MODEL), lambda i, k: (0, 0))],
        out_specs=tuple(out_specs),
        scratch_shapes=[pltpu.VMEM((tm, D_MODEL), F32)],
        compiler_params=_cparams(2),
        name="down_proj",
    )(a, w_down, x, g.reshape(1, D_MODEL))


def _ffn_prompt_kernel(h_ref, wu_ref, wv_ref, cw_ref, cb_ref, a_ref, st_ref, wub_ref, wvb_ref):
    @pl.when(pl.program_id(1) == 0)
    def _():
        wub_ref[...] = wu_ref[...].astype(BF16)
        wvb_ref[...] = wv_ref[...].astype(BF16)

    rc = FFN_ROW_CHUNK
    tn = a_ref.shape[1]
    cw = cw_ref[...]
    cbias = cb_ref[...]
    row8 = lax.broadcasted_iota(jnp.int32, (SUBLANES, tn), 0)
    tail = jnp.zeros((SUBLANES, tn), F32)
    pending = None
    for c in range(h_ref.shape[0] // rc):
        rows = slice(c * rc, (c + 1) * rc)
        hc = h_ref[rows, :]
        u = jnp.dot(hc, wub_ref[...], preferred_element_type=F32)
        if pending is not None:
            prev_rows, g_prev, v_prev = pending
            a_ref[prev_rows, :] = (g_prev * v_prev).astype(a_ref.dtype)
        v = jnp.dot(hc, wvb_ref[...], preferred_element_type=F32)
        uc = cbias + cw[FFN_CONV_W - 1:FFN_CONV_W] * u
        for d in range(1, FFN_CONV_W):
            rolled = pltpu.roll(u, d, axis=0)
            top = jnp.where(row8 >= d, rolled[:SUBLANES], pltpu.roll(tail, d, axis=0))
            uc = uc + cw[FFN_CONV_W - 1 - d:FFN_CONV_W - d] * jnp.concatenate([top, rolled[SUBLANES:]], axis=0)
        pending = (rows, jax.nn.gelu(uc), v)
        tail = u[rc - SUBLANES:, :]
    prev_rows, g_prev, v_prev = pending
    a_ref[prev_rows, :] = (g_prev * v_prev).astype(a_ref.dtype)
    st_ref[...] = tail[SUBLANES - (FFN_CONV_W - 1):, :]


def ffn_prompt(layer, h2, w_up, w_val, conv_w, conv_b, tn):
    wspec = pl.BlockSpec((None, D_MODEL, tn), lambda j, b: (layer, 0, j))
    wbspec = pl.BlockSpec((D_MODEL, tn), lambda j, b: (0, j))
    return pl.pallas_call(
        _ffn_prompt_kernel,
        out_shape=(jax.ShapeDtypeStruct((BATCH * SEQ, D_FF), BF16),
                   jax.ShapeDtypeStruct((BATCH, FFN_CONV_W - 1, D_FF), F32),
                   jax.ShapeDtypeStruct((D_MODEL, D_FF), BF16),
                   jax.ShapeDtypeStruct((D_MODEL, D_FF), BF16)),
        grid=(D_FF // tn, BATCH),
        in_specs=[pl.BlockSpec((SEQ, D_MODEL), lambda j, b: (b, 0)), wspec, wspec,
                  pl.BlockSpec((FFN_CONV_W, tn), lambda j, b: (0, j)),
                  pl.BlockSpec((1, tn), lambda j, b: (0, j))],
        out_specs=(pl.BlockSpec((SEQ, tn), lambda j, b: (b, j)),
                   pl.BlockSpec((None, FFN_CONV_W - 1, tn), lambda j, b: (b, 0, j)),
                   wbspec, wbspec),
        compiler_params=_cparams(2),
        name="ffn_prompt",
    )(h2, w_up, w_val, conv_w, conv_b.reshape(1, D_FF))


def _ffn_sample_kernel(h_ref, wu_ref, wv_ref, cw_ref, cb_ref, buf_ref, a_ref, nb_ref):
    h = h_ref[...]
    u = jnp.dot(h, wu_ref[...], preferred_element_type=F32)
    v = jnp.dot(h, wv_ref[...], preferred_element_type=F32)
    cw = cw_ref[...]
    b0 = buf_ref[:, 0, :]
    b1 = buf_ref[:, 1, :]
    uc = cb_ref[...] + cw[0:1] * b0 + cw[1:2] * b1 + cw[2:3] * u
    a_ref[...] = (jax.nn.gelu(uc) * v).astype(a_ref.dtype)
    nb_ref[:, 0, :] = b1
    nb_ref[:, 1, :] = u


def ffn_sample(layer, h2, w_up, w_val, conv_w, conv_b, buf, tn):
    return pl.pallas_call(
        _ffn_sample_kernel,
        out_shape=(jax.ShapeDtypeStruct((DEC_BATCH, D_FF), BF16),
                   jax.ShapeDtypeStruct((DEC_BATCH, FFN_CONV_W - 1, D_FF), F32)),
        grid=(D_FF // tn,),
        in_specs=[pl.BlockSpec((DEC_BATCH, D_MODEL), lambda j: (0, 0)),
                  pl.BlockSpec((D_MODEL, tn), lambda j: (0, j)),
                  pl.BlockSpec((D_MODEL, tn), lambda j: (0, j)),
                  pl.BlockSpec((FFN_CONV_W, tn), lambda j: (0, j)),
                  pl.BlockSpec((1, tn), lambda j: (0, j)),
                  pl.BlockSpec((None, DEC_BATCH, FFN_CONV_W - 1, tn), lambda j: (layer, 0, 0, j))],
        out_specs=(pl.BlockSpec((DEC_BATCH, tn), lambda j: (0, j)),
                   pl.BlockSpec((DEC_BATCH, FFN_CONV_W - 1, tn), lambda j: (0, 0, j))),
        compiler_params=_cparams(1),
        name="ffn_sample",
    )(h2, w_up, w_val, conv_w, conv_b.reshape(1, D_FF), buf)


def _rope_table_kernel(start, consecutive, freq_ref, sign_ref, cos_ref, sin_ref):
    shape = cos_ref.shape
    if consecutive:
        pos = lax.broadcasted_iota(jnp.int32, shape, 0).astype(F32) + float(start)
    else:
        pos = jnp.full(shape, float(start), F32)
    ang = pos * freq_ref[...]
    cos_ref[...] = jnp.cos(ang)
    sin_ref[...] = sign_ref[...] * jnp.sin(ang)


def rope_tables(n_rows, start, consecutive):
    half = RET_DK // 2
    freqs = ROPE_BASE ** (-jnp.arange(half, dtype=F32) / half)
    freq2 = jnp.concatenate([freqs, freqs]).reshape(1, RET_DK)
    sign = jnp.concatenate([-jnp.ones((half,), F32), jnp.ones((half,), F32)]).reshape(1, RET_DK)
    return pl.pallas_call(
        functools.partial(_rope_table_kernel, start, consecutive),
        out_shape=(jax.ShapeDtypeStruct((n_rows, RET_DK), F32), jax.ShapeDtypeStruct((n_rows, RET_DK), F32)),
        name="rope_tables",
    )(freq2, sign)


def _rope(x, cos, sin_signed):
    return x * cos + pltpu.roll(x, RET_DK // 2, axis=1) * sin_signed


def _lru_gates(conv, wa_ref, ba_ref, wx_ref, bx_ref, lam_ref):
    xb = conv.astype(BF16)
    r = jax.nn.sigmoid(jnp.dot(xb, wa_ref[...].astype(BF16), preferred_element_type=F32) + ba_ref[...])
    i = jax.nn.sigmoid(jnp.dot(xb, wx_ref[...].astype(BF16), preferred_element_type=F32) + bx_ref[...])
    log_a = -LRU_C * r * jax.nn.softplus(-lam_ref[...])
    a = jnp.exp(log_a)
    u = jnp.sqrt(1.0 - a * a) * (i * conv)
    return a, u


def _lru_prompt_kernel(hin_ref, wxa_ref, wya_ref, cw_ref, cb_ref, wa_ref, ba_ref, wx_ref, bx_ref, lam_ref,
                       out_ref, h_ref, conv_ref, ag_s, ug_s):
    proj = _project(hin_ref, (wxa_ref, wya_ref))
    x = proj[:, :LRU_BLOCK]
    ya = proj[:, LRU_BLOCK:]
    t_len = x.shape[0]
    row = lax.broadcasted_iota(jnp.int32, x.shape, 0)
    cw = cw_ref[...]
    conv = cb_ref[...] + cw[CONV_W - 1:CONV_W] * x
    for d in range(1, CONV_W):
        conv = conv + cw[CONV_W - 1 - d:CONV_W - d] * _shift_rows(x, d, row)
    a, u = _lru_gates(conv, wa_ref, ba_ref, wx_ref, bx_ref, lam_ref)
    ng = t_len // SUBLANES
    a3 = a.reshape(ng, SUBLANES, LRU_BLOCK)
    u3 = u.reshape(ng, SUBLANES, LRU_BLOCK)
    sub = lax.broadcasted_iota(jnp.int32, a3.shape, 1)
    d = 1
    while d < SUBLANES:
        keep = sub >= d
        u3 = jnp.where(keep, a3 * pltpu.roll(u3, d, axis=1) + u3, u3)
        a3 = jnp.where(keep, a3 * pltpu.roll(a3, d, axis=1), a3)
        d *= 2
    ag_s[...] = a3.reshape(t_len, LRU_BLOCK)
    ug_s[...] = u3.reshape(t_len, LRU_BLOCK)
    ag = ag_s[pl.ds(SUBLANES - 1, ng, stride=SUBLANES), :]
    ug = ug_s[pl.ds(SUBLANES - 1, ng, stride=SUBLANES), :]
    grow = lax.broadcasted_iota(jnp.int32, ag.shape, 0)
    d = 1
    while d < ng:
        keep = grow >= d
        ug = jnp.where(keep, ag * pltpu.roll(ug, d, axis=0) + ug, ug)
        ag = jnp.where(keep, ag * pltpu.roll(ag, d, axis=0), ag)
        d *= 2
    carry = _shift_rows(ug, 1, grow)
    h3 = a3 * jnp.broadcast_to(carry[:, None, :], a3.shape) + u3
    hs = h3.reshape(t_len, LRU_BLOCK)
    out_ref[...] = (hs * jax.nn.gelu(ya)).astype(out_ref.dtype)
    h_ref[...] = ug[ng - 1:, :]
    conv_ref[...] = x[t_len - (CONV_W - 1):, :]


def _lru_param_specs(n_axes_fn):
    blk3 = lambda shape: pl.BlockSpec(shape, n_axes_fn(lambda n: (n, 0, 0)))
    return [pl.BlockSpec((CONV_W, LRU_BLOCK), n_axes_fn(lambda n: (0, n))),
            pl.BlockSpec((1, LRU_BLOCK), n_axes_fn(lambda n: (0, n))),
            blk3((None, LRU_BLOCK, LRU_BLOCK)), blk3((None, 1, LRU_BLOCK)),
            blk3((None, LRU_BLOCK, LRU_BLOCK)), blk3((None, 1, LRU_BLOCK)),
            blk3((None, 1, LRU_BLOCK))]


def _lru_params(p):
    return (p["lru_conv_w"], p["lru_conv_b"].reshape(1, BRANCH_W),
            p["lru_wa"], p["lru_ba"].reshape(LRU_BLOCKS, 1, LRU_BLOCK),
            p["lru_wx"], p["lru_bx"].reshape(LRU_BLOCKS, 1, LRU_BLOCK),
            p["lru_lambda"].reshape(LRU_BLOCKS, 1, LRU_BLOCK))


def _prompt_unit_rows(layer, off):
    return lambda b, u: (layer, pl.multiple_of(off + u * LANES, SUBLANES), 0)


def lru_prompt(layer, hin, w_in_t, p):
    wrap = lambda f: (lambda b, n: f(n))
    out, h, conv = pl.pallas_call(
        _lru_prompt_kernel,
        out_shape=(jax.ShapeDtypeStruct((BATCH * SEQ, BRANCH_W), BF16),
                   jax.ShapeDtypeStruct((BATCH, 1, BRANCH_W), F32),
                   jax.ShapeDtypeStruct((BATCH, CONV_W - 1, BRANCH_W), F32)),
        grid=(BATCH, LRU_BLOCKS),
        in_specs=[pl.BlockSpec((SEQ, D_MODEL), lambda b, n: (b, 0))]
                 + _w_in_row_specs(layer, (OFF_XA, OFF_YA), _prompt_unit_rows) + _lru_param_specs(wrap),
        out_specs=(pl.BlockSpec((SEQ, LRU_BLOCK), lambda b, n: (b, n)),
                   pl.BlockSpec((None, 1, LRU_BLOCK), lambda b, n: (b, 0, n)),
                   pl.BlockSpec((None, CONV_W - 1, LRU_BLOCK), lambda b, n: (b, 0, n))),
        scratch_shapes=[pltpu.VMEM((SEQ, LRU_BLOCK), F32)] * 2,
        compiler_params=_cparams(2),
        name="lru_prompt",
    )(hin, w_in_t, w_in_t, *_lru_params(p))
    return out, h.reshape(BATCH, BRANCH_W), conv


def _lru_sample_kernel(xa_ref, ya_ref, cw_ref, cb_ref, wa_ref, ba_ref, wx_ref, bx_ref, lam_ref,
                       h0_ref, buf_ref, out_ref, h_ref, nbuf_ref):
    x = xa_ref[...]
    cw = cw_ref[...]
    conv = cb_ref[...] + cw[CONV_W - 1:CONV_W] * x
    for j in range(CONV_W - 1):
        conv = conv + cw[j:j + 1] * buf_ref[j]
    a, u = _lru_gates(conv, wa_ref, ba_ref, wx_ref, bx_ref, lam_ref)
    h = a * h0_ref[...] + u
    out_ref[...] = h * jax.nn.gelu(ya_ref[...])
    h_ref[...] = h
    for j in range(CONV_W - 2):
        nbuf_ref[j] = buf_ref[j + 1]
    nbuf_ref[CONV_W - 2] = x


def lru_sample(proj, p, h0, buf_t):
    cb = lambda off: off // LRU_BLOCK
    wrap = lambda f: f
    return pl.pallas_call(
        _lru_sample_kernel,
        out_shape=(jax.ShapeDtypeStruct((DEC_BATCH, BRANCH_W), F32),
                   jax.ShapeDtypeStruct((DEC_BATCH, BRANCH_W), F32),
                   jax.ShapeDtypeStruct((CONV_W - 1, DEC_BATCH, BRANCH_W), F32)),
        grid=(LRU_BLOCKS,),
        in_specs=[pl.BlockSpec((DEC_BATCH, LRU_BLOCK), lambda n: (0, cb(OFF_XA) + n)),
                  pl.BlockSpec((DEC_BATCH, LRU_BLOCK), lambda n: (0, cb(OFF_YA) + n))]
                 + _lru_param_specs(wrap)
                 + [pl.BlockSpec((DEC_BATCH, LRU_BLOCK), lambda n: (0, n)),
                    pl.BlockSpec((CONV_W - 1, DEC_BATCH, LRU_BLOCK), lambda n: (0, 0, n))],
        out_specs=(pl.BlockSpec((DEC_BATCH, LRU_BLOCK), lambda n: (0, n)),
                   pl.BlockSpec((DEC_BATCH, LRU_BLOCK), lambda n: (0, n)),
                   pl.BlockSpec((CONV_W - 1, DEC_BATCH, LRU_BLOCK), lambda n: (0, 0, n))),
        compiler_params=_cparams(1),
        name="lru_sample",
    )(proj, proj, *_lru_params(p), h0, buf_t)


def _hgrn_lower_bound(layer, logits):
    mx = jnp.max(logits, axis=0, keepdims=True)
    e = jnp.exp(logits - mx)
    ls = e / jnp.sum(e, axis=0, keepdims=True)
    lb = jnp.zeros_like(ls[0:1])
    for i in range(1, layer + 1):
        lb = lb + ls[i:i + 1]
    return lb


def _hgrn_gates(layer, hq, hf, lbl_ref):
    lb = _hgrn_lower_bound(layer, lbl_ref[...])
    q = jax.nn.silu(hq)
    sg = jax.nn.sigmoid(hf)
    f = lb + (1.0 - lb) * sg
    k = (1.0 - lb) * (1.0 - sg)
    return q, f, k


def _hgrn_level_ids():
    c = HG_CHUNK
    t = np.arange(c)[:, None]
    s = np.arange(c)[None, :]
    level = np.zeros((c, c), np.int32)
    for li, m in enumerate(HG_LEVELS):
        same = (t // (2 * m)) == (s // (2 * m))
        level[same & ((t % (2 * m)) >= m) & ((s % (2 * m)) < m)] = li + 1
    level[t == s] = len(HG_LEVELS) + 1
    return level


def _split3_bf16(x):
    hi = x.astype(BF16)
    r1 = x - hi.astype(F32)
    mid = r1.astype(BF16)
    lo = (r1 - mid.astype(F32)).astype(BF16)
    return hi, mid, lo


def _cumsum_rows(tril_b, x):
    return sum(jnp.dot(tril_b, piece, preferred_element_type=F32) for piece in _split3_bf16(x))


def _hgrn_midpoint_factor(b, m):
    c = b.shape[0]
    if 2 * m >= SUBLANES:
        b3 = b.reshape(c // (2 * m), 2 * m, LANES)
        mid = b3[:, m - 1:m, :]
    else:
        b3 = b.reshape(c // SUBLANES, SUBLANES, LANES)
        sub = lax.broadcasted_iota(jnp.int32, b3.shape, 1)
        mid = b3[:, m - 1:m, :]
        for blk in range(1, SUBLANES // (2 * m)):
            lo = blk * 2 * m
            mid = jnp.where(sub >= lo, b3[:, lo + m - 1:lo + m, :], mid)
    return jnp.exp(-jnp.abs(b3 - mid)).reshape(c, LANES)


def _hgrn_prompt_kernel(layer, hin_ref, wq_ref, wf_ref, wi_ref, wg_ref, lbl_ref, nw_ref, lvl_ref,
                        out_ref, st_ref, lf_s, q_s, k_s, v_s, o_s):
    c = HG_CHUNK
    t_len = hin_ref.shape[0]
    proj = _project(hin_ref, (wq_ref, wf_ref, wi_ref, wg_ref))
    q, f, k = _hgrn_gates(layer, proj[:, 0:HG_DK], proj[:, HG_DK:2 * HG_DK], lbl_ref)
    lf_s[...] = jnp.log(f)
    q_s[...] = q
    k_s[...] = k
    v_s[...] = proj[:, 2 * HG_DK:3 * HG_DK].astype(BF16)
    gate = proj[:, 3 * HG_DK:]

    lvl = lvl_ref[...]
    tril_b = (lax.broadcasted_iota(jnp.int32, (c, c), 0) >= lax.broadcasted_iota(jnp.int32, (c, c), 1)).astype(BF16)

    st = jnp.zeros((HG_DK, HG_DK), F32)
    for ci in range(t_len // c):
        sl = slice(ci * c, (ci + 1) * c)
        qc = q_s[sl, :]
        kc = k_s[sl, :]
        vb = v_s[sl, :]
        b = _cumsum_rows(tril_b, lf_s[sl, :])
        att = jnp.where(lvl == len(HG_LEVELS) + 1, _nt_dot(qc.astype(BF16), kc.astype(BF16)), 0.0)
        for li, m in enumerate(HG_LEVELS):
            e = _hgrn_midpoint_factor(b, m)
            a_l = _nt_dot((qc * e).astype(BF16), (kc * e).astype(BF16))
            att = jnp.where(lvl == li + 1, a_l, att)
        o = (jnp.dot(att.astype(BF16), vb, preferred_element_type=F32)
             + _nt_dot((qc * jnp.exp(b)).astype(BF16), st.astype(BF16)))
        o_s[sl, :] = o
        bl = b[c - 1:c, :]
        kdec = (kc * jnp.exp(bl - b)).astype(BF16)
        st = st * jnp.exp(bl) + _tn_dot(vb, kdec)
    o = o_s[...]
    out_ref[...] = (_rms_rows(o) * nw_ref[...] * jax.nn.silu(gate)).astype(out_ref.dtype)
    st_ref[...] = st.T


def hgrn_prompt(layer, hin, w_in_t, p):
    level = _hgrn_level_ids()
    return pl.pallas_call(
        functools.partial(_hgrn_prompt_kernel, layer),
        out_shape=(jax.ShapeDtypeStruct((BATCH * SEQ, BRANCH_W), BF16),
                   jax.ShapeDtypeStruct((BATCH, HG_HEADS, HG_DK, HG_DK), F32)),
        grid=(BATCH, HG_HEADS),
        in_specs=[pl.BlockSpec((SEQ, D_MODEL), lambda b, h: (b, 0))]
                 + _w_in_row_specs(layer, (OFF_HQ, OFF_HF, OFF_HI, OFF_HG), _prompt_unit_rows)
                 + [pl.BlockSpec((DEPTH, HG_DK), lambda b, h: (0, h)),
                    pl.BlockSpec((1, HG_DK), lambda b, h: (0, 0)),
                    pl.BlockSpec(level.shape, lambda b, h: (0, 0))],
        out_specs=(pl.BlockSpec((SEQ, HG_DK), lambda b, h: (b, h)),
                   pl.BlockSpec((None, None, HG_DK, HG_DK), lambda b, h: (b, h, 0, 0))),
        scratch_shapes=[pltpu.VMEM((SEQ, HG_DK), F32)] * 3
                       + [pltpu.VMEM((SEQ, HG_DK), BF16), pltpu.VMEM((SEQ, HG_DK), F32)],
        compiler_params=_cparams(2),
        name="hgrn_prompt",
    )(hin, w_in_t, w_in_t, w_in_t, w_in_t, p["hg_lb_logits"], p["hg_norm_w"].reshape(1, HG_DK),
      jnp.asarray(level))


def _sample_state_step(s_ref, so_ref, o_s, decay_of, q, k, v):
    nb = k.shape[0]
    kt = _rows_to_cols(k)[:, :nb].astype(BF16)
    qb = q.astype(BF16)
    rowid = lax.broadcasted_iota(jnp.int32, v.shape, 0)
    for j in range(nb):
        v_j = jnp.where(rowid == j, v, 0.0).astype(BF16)
        s_new = decay_of(j) * s_ref[j] + jnp.dot(kt, v_j, preferred_element_type=F32)
        so_ref[j] = s_new
        o_s[j:j + 1, :] = jnp.dot(qb, s_new.astype(BF16), preferred_element_type=F32)[j:j + 1, :]


def _hgrn_sample_kernel(layer, q_ref, f_ref, i_ref, g_ref, lbl_ref, nw_ref, s_ref, out_ref, so_ref, o_s):
    q, f, k = _hgrn_gates(layer, q_ref[...], f_ref[...], lbl_ref)
    fc = _rows_to_cols(f)
    _sample_state_step(s_ref, so_ref, o_s, lambda j: fc[:, j:j + 1], q, k, i_ref[...])
    out_ref[...] = _rms_rows(o_s[...]) * nw_ref[...] * jax.nn.silu(g_ref[...])


def hgrn_sample(layer, proj, p, state, state_out):
    cb = lambda off: off // HG_DK
    nb = SAMPLE_BLK
    col = lambda off: pl.BlockSpec((nb, HG_DK), lambda h, i: (i, cb(off) + h))
    st_spec = pl.BlockSpec((None, nb, None, HG_DK, HG_DK), lambda h, i: (layer, i, h, 0, 0))
    args = [proj, proj, proj, proj, p["hg_lb_logits"], p["hg_norm_w"].reshape(1, HG_DK), state]
    in_specs = [col(OFF_HQ), col(OFF_HF), col(OFF_HI), col(OFF_HG),
                pl.BlockSpec((DEPTH, HG_DK), lambda h, i: (0, h)),
                pl.BlockSpec((1, HG_DK), lambda h, i: (0, 0)),
                st_spec]
    aliases = {}
    kern = functools.partial(_hgrn_sample_kernel, layer)
    if state_out is not None:
        args.append(state_out)
        in_specs.append(pl.BlockSpec(memory_space=pl.ANY))
        aliases = {len(args) - 1: 1}
        kern = functools.partial(_drop_alias_arg, kern, 7)
    return pl.pallas_call(
        kern,
        out_shape=(jax.ShapeDtypeStruct((DEC_BATCH, BRANCH_W), F32),
                   jax.ShapeDtypeStruct(state.shape, F32)),
        grid=(HG_HEADS, DEC_BATCH // nb),
        in_specs=in_specs,
        out_specs=(pl.BlockSpec((nb, HG_DK), lambda h, i: (i, h)), st_spec),
        scratch_shapes=[pltpu.VMEM((nb, HG_DK), F32)],
        input_output_aliases=aliases,
        compiler_params=_cparams(2),
        name="hgrn_sample",
    )(*args)


def _drop_alias_arg(kern, pos, *refs):
    return kern(*refs[:pos], *refs[pos + 1:])


def _head_pair(cols, h0, lane_lo):
    return jnp.where(lane_lo, cols[:, h0:h0 + 1], cols[:, h0 + 1:h0 + 2])


def _ssd_prompt_kernel(z_ref, x_ref, bc_ref, dt_ref, cw_ref, cb_ref, dtb_ref, alog_ref, dpar_ref, nw_ref,
                       out_ref, st_ref, cst_ref, cx_s, cbc_s, s_s):
    c = SSD_CHUNK
    ci = pl.program_id(1)

    @pl.when(ci == 0)
    def _():
        cx_s[...] = jnp.zeros_like(cx_s)
        cbc_s[...] = jnp.zeros_like(cbc_s)
        s_s[...] = jnp.zeros_like(s_s)

    cw = cw_ref[...]
    cbias = cb_ref[...]

    def conv_silu(raw, carry_ref, lo, hi):
        xx = jnp.concatenate([carry_ref[...], raw], axis=0)
        y = cbias[:, lo:hi] + cw[CONV_W - 1:CONV_W, lo:hi] * raw
        for d in range(1, CONV_W):
            y = y + cw[CONV_W - 1 - d:CONV_W - d, lo:hi] * pltpu.roll(xx, d, axis=0)[SUBLANES:]
        carry_ref[...] = raw[c - SUBLANES:, :]
        return jax.nn.silu(y)

    x_raw = x_ref[...]
    bc_raw = bc_ref[...]
    xs = conv_silu(x_raw, cx_s, 0, BRANCH_W)
    bc = conv_silu(bc_raw, cbc_s, BRANCH_W, SSD_CONV_DIM)

    dt = jax.nn.softplus(dt_ref[...] + dtb_ref[...])
    a_neg = -jnp.exp(alog_ref[...])
    logd = dt * a_neg
    tri = lax.broadcasted_iota(jnp.int32, (c, c), 0) >= lax.broadcasted_iota(jnp.int32, (c, c), 1)
    b = _cumsum_rows(tri.astype(BF16), logd)
    b_t = b.T
    bl = b[c - 1:c, :]
    e_in = jnp.exp(b)
    w_out = jnp.exp(bl - b)
    e_last = jnp.exp(bl)
    dfull = dpar_ref[...]

    lane_lo =lax.broadcasted_iota(jnp.int32, (c, LANES), 1) < SSD_HEADDIM
    lane_lo_row = lax.broadcasted_iota(jnp.int32, (1, LANES), 1) < SSD_HEADDIM

    ys = []
    for g in range(SSD_GROUPS):
        bm = bc[:, g * SSD_STATE:(g + 1) * SSD_STATE].astype(BF16)
        cm = bc[:, (SSD_GROUPS + g) * SSD_STATE:(SSD_GROUPS + g + 1) * SSD_STATE].astype(BF16)
        gmat = _nt_dot(cm, bm)
        for pp in range(SSD_HEADS // SSD_GROUPS // 2):
            pi = g * (SSD_HEADS // SSD_GROUPS // 2) + pp
            h0 = 2 * pi
            xs_p = xs[:, pi * LANES:(pi + 1) * LANES]
            vdt = xs_p * _head_pair(dt, h0, lane_lo)
            vdt_b = vdt.astype(BF16)
            o_heads = []
            for hh in (h0, h0 + 1):
                diff = b[:, hh:hh + 1] - b_t[hh:hh + 1, :]
                dec = jnp.where(tri, jnp.exp(jnp.where(tri, diff, 0.0)), 0.0)
                o_heads.append(jnp.dot((gmat * dec).astype(BF16), vdt_b, preferred_element_type=F32))
            o_intra = jnp.where(lane_lo, o_heads[0], o_heads[1])
            s_p = s_s[pi]
            o_inter = _head_pair(e_in, h0, lane_lo) * jnp.dot(cm, s_p.astype(BF16), preferred_element_type=F32)
            ys.append(o_intra + o_inter + dfull[:, pi * LANES:(pi + 1) * LANES] * xs_p)
            upd = _tn_dot(bm, (vdt * _head_pair(w_out, h0, lane_lo)).astype(BF16))
            s_s[pi] = s_p * _head_pair(e_last, h0, lane_lo_row) + upd

    y = jnp.concatenate(ys, axis=1) * jax.nn.silu(z_ref[...])
    gw = BRANCH_W // SSD_GROUPS
    nw = nw_ref[...]
    outs = [_rms_rows(y[:, g * gw:(g + 1) * gw]) * nw[:, g * gw:(g + 1) * gw] for g in range(SSD_GROUPS)]
    out_ref[...] = jnp.concatenate(outs, axis=1).astype(out_ref.dtype)

    @pl.when(ci == pl.num_programs(1) - 1)
    def _():
        for pi in range(SSD_HEADS // 2):
            s_t = s_s[pi].T
            st_ref[2 * pi] = s_t[:SSD_HEADDIM, :]
            st_ref[2 * pi + 1] = s_t[SSD_HEADDIM:, :]
        cst_ref[:, 0:BRANCH_W] = x_raw[c - (CONV_W - 1):, :]
        cst_ref[:, BRANCH_W:SSD_CONV_DIM] = bc_raw[c - (CONV_W - 1):, :]


def _pad_lanes(v):
    return jnp.pad(v.astype(F32), (0, LANES - v.shape[0])).reshape(1, LANES)


def _ssd_params(p):
    return (p["ssd_conv_w"], p["ssd_conv_b"].reshape(1, SSD_CONV_DIM), _pad_lanes(p["ssd_dt_bias"]),
            _pad_lanes(p["ssd_a_log"]), jnp.repeat(p["ssd_d"].astype(F32), SSD_HEADDIM).reshape(1, BRANCH_W),
            p["ssd_norm_w"].reshape(1, BRANCH_W))


def ssd_prompt(proj, p):
    c = SSD_CHUNK
    nc = SEQ // c
    const = lambda shape: pl.BlockSpec(shape, lambda b, i: (0, 0))
    rowblk = lambda w, off: pl.BlockSpec((c, w), lambda b, i: (b * nc + i, (off - OFF_SZ) // w))
    return pl.pallas_call(
        _ssd_prompt_kernel,
        out_shape=(jax.ShapeDtypeStruct((BATCH * SEQ, BRANCH_W), BF16),
                   jax.ShapeDtypeStruct((BATCH, SSD_HEADS, SSD_HEADDIM, SSD_STATE), F32),
                   jax.ShapeDtypeStruct((BATCH, CONV_W - 1, SSD_CONV_DIM), F32)),
        grid=(BATCH, nc),
        in_specs=[rowblk(BRANCH_W, OFF_SZ), rowblk(BRANCH_W, OFF_SX), rowblk(SSD_BC, OFF_SBC),
                  rowblk(LANES, OFF_SDT),
                  const((CONV_W, SSD_CONV_DIM)), const((1, SSD_CONV_DIM)), const((1, LANES)),
                  const((1, LANES)), const((1, BRANCH_W)), const((1, BRANCH_W))],
        out_specs=(pl.BlockSpec((c, BRANCH_W), lambda b, i: (b * nc + i, 0)),
                   pl.BlockSpec((None, SSD_HEADS, SSD_HEADDIM, SSD_STATE), lambda b, i: (b, 0, 0, 0)),
                   pl.BlockSpec((None, CONV_W - 1, SSD_CONV_DIM), lambda b, i: (b, 0, 0))),
        scratch_shapes=[pltpu.VMEM((SUBLANES, BRANCH_W), F32), pltpu.VMEM((SUBLANES, SSD_BC), F32),
                        pltpu.VMEM((SSD_HEADS // 2, SSD_STATE, LANES), F32)],
        compiler_params=_cparams(2),
        name="ssd_prompt",
    )(proj, proj, proj, proj, *_ssd_params(p))


def _ssd_sample_kernel(z_ref, x_ref, bc_ref, dt_ref, cw_ref, cb_ref, dtb_ref, alog_ref, dpar_ref, nw_ref,
                       bufx_ref, bufbc_ref, s_ref, out_ref, so_ref, nbx_ref, nbbc_ref, y_s):
    cw = cw_ref[...]
    cbias = cb_ref[...]

    def conv_silu(raw, buf_ref, nbuf_ref, lo, hi):
        y = cbias[:, lo:hi] + cw[CONV_W - 1:CONV_W, lo:hi] * raw
        for j in range(CONV_W - 1):
            y = y + cw[j:j + 1, lo:hi] * buf_ref[j]
        for j in range(CONV_W - 2):
            nbuf_ref[j] = buf_ref[j + 1]
        nbuf_ref[CONV_W - 2] = raw
        return jax.nn.silu(y)

    xs = conv_silu(x_ref[...], bufx_ref, nbx_ref, 0, BRANCH_W)
    bc = conv_silu(bc_ref[...], bufbc_ref, nbbc_ref, BRANCH_W, SSD_CONV_DIM)
    dt = jax.nn.softplus(dt_ref[...] + dtb_ref[...])
    decay = jnp.exp(dt * (-jnp.exp(alog_ref[...])))
    nb = xs.shape[0]
    hpg = SSD_HEADS // SSD_GROUPS
    lane_lo = lax.broadcasted_iota(jnp.int32, (nb, LANES), 1) < SSD_HEADDIM
    for pi in range(SSD_HEADS // 2):
        h0 = 2 * pi
        g = h0 // hpg
        xdt_cols = _rows_to_cols(xs[:, pi * LANES:(pi + 1) * LANES] * _head_pair(dt, h0, lane_lo))
        for e in range(2):
            h = h0 + e
            for j in range(nb):
                xcol = xdt_cols[e * SSD_HEADDIM:(e + 1) * SSD_HEADDIM, j:j + 1]
                brow = bc[j:j + 1, g * SSD_STATE:(g + 1) * SSD_STATE]
                so_ref[j, h] = s_ref[j, h] * decay[j:j + 1, h:h + 1] + xcol * brow
    for g in range(SSD_GROUPS):
        cm = bc[:, (SSD_GROUPS + g) * SSD_STATE:(SSD_GROUPS + g + 1) * SSD_STATE].astype(BF16)
        for j in range(nb):
            s_new = so_ref[j, g * hpg:(g + 1) * hpg].reshape(hpg * SSD_HEADDIM, SSD_STATE)
            y_s[j:j + 1, g * hpg * SSD_HEADDIM:(g + 1) * hpg * SSD_HEADDIM] = _nt_dot(cm, s_new.astype(BF16))[j:j + 1, :]
    y = (y_s[...] + dpar_ref[...] * xs) * jax.nn.silu(z_ref[...])
    gw = BRANCH_W // SSD_GROUPS
    nw = nw_ref[...]
    outs = [_rms_rows(y[:, g * gw:(g + 1) * gw]) * nw[:, g * gw:(g + 1) * gw] for g in range(SSD_GROUPS)]
    out_ref[...] = jnp.concatenate(outs, axis=1)


def ssd_sample(layer, proj, p, state, state_out, buf_t):
    nb = SSD_SAMPLE_BLK
    const = lambda shape: pl.BlockSpec(shape, lambda i: (0,) * len(shape))
    rowblk = lambda w, off: pl.BlockSpec((nb, w), lambda i: (i, off // w))
    st_spec = pl.BlockSpec((None, nb, SSD_HEADS, SSD_HEADDIM, SSD_STATE), lambda i: (layer, i, 0, 0, 0))
    bufx_spec = pl.BlockSpec((CONV_W - 1, nb, BRANCH_W), lambda i: (0, i, 0))
    bufbc_spec = pl.BlockSpec((CONV_W - 1, nb, SSD_BC), lambda i: (0, i, BRANCH_W // SSD_BC))
    args = [proj, proj, proj, proj, *_ssd_params(p), buf_t, buf_t, state]
    in_specs = [rowblk(BRANCH_W, OFF_SZ), rowblk(BRANCH_W, OFF_SX), rowblk(SSD_BC, OFF_SBC), rowblk(LANES, OFF_SDT),
                const((CONV_W, SSD_CONV_DIM)), const((1, SSD_CONV_DIM)), const((1, LANES)), const((1, LANES)),
                const((1, BRANCH_W)), const((1, BRANCH_W)), bufx_spec, bufbc_spec, st_spec]
    aliases = {}
    kern = _ssd_sample_kernel
    if state_out is not None:
        args.append(state_out)
        in_specs.append(pl.BlockSpec(memory_space=pl.ANY))
        aliases = {len(args) - 1: 1}
        kern = functools.partial(_drop_alias_arg, kern, 13)
    out, st, nbx, nbbc = pl.pallas_call(
        kern,
        out_shape=(jax.ShapeDtypeStruct((DEC_BATCH, BRANCH_W), F32),
                   jax.ShapeDtypeStruct(state.shape, F32),
                   jax.ShapeDtypeStruct((CONV_W - 1, DEC_BATCH, BRANCH_W), F32),
                   jax.ShapeDtypeStruct((CONV_W - 1, DEC_BATCH, SSD_BC), F32)),
        grid=(DEC_BATCH // nb,),
        in_specs=in_specs,
        out_specs=(pl.BlockSpec((nb, BRANCH_W), lambda i: (i, 0)), st_spec,
                   pl.BlockSpec((CONV_W - 1, nb, BRANCH_W), lambda i: (0, i, 0)),
                   pl.BlockSpec((CONV_W - 1, nb, SSD_BC), lambda i: (0, i, 0))),
        scratch_shapes=[pltpu.VMEM((nb, BRANCH_W), F32)],
        input_output_aliases=aliases,
        compiler_params=_cparams(1),
        name="ssd_sample",
    )(*args)
    return out, st, jnp.concatenate([nbx, nbbc], axis=-1)


def _ret_prompt_kernel(hin_ref, wq_ref, wk_ref, wv_ref, wg_ref, cos_ref, sin_ref, lg_ref, out_ref, st_ref,
                       q_s, k_s, v_s, o_s):
    c = RET_CHUNK
    t_len = hin_ref.shape[0]
    proj = _project(hin_ref, (wq_ref, wk_ref, wv_ref, wg_ref))
    cos = cos_ref[...]
    sin = sin_ref[...]
    q_s[...] = _rope(proj[:, 0:RET_DK], cos, sin)
    k_s[...] = _rope(proj[:, RET_DK:2 * RET_DK], cos, sin) * RET_DK ** -0.5
    v_s[...] = proj[:, 2 * RET_DK:3 * RET_DK].astype(BF16)
    gate = proj[:, 3 * RET_DK:]
    lg = lg_ref[...]
    lg128 = lg[:, :LANES]
    ti = lax.broadcasted_iota(jnp.int32, (c, c), 0)
    si = lax.broadcasted_iota(jnp.int32, (c, c), 1)
    tri = ti >= si
    dec = jnp.where(tri, jnp.exp(jnp.where(tri, (ti - si).astype(F32) * lg, 0.0)), 0.0)
    tt = lax.broadcasted_iota(jnp.int32, (c, LANES), 0).astype(F32)
    g_in = jnp.exp((tt + 1.0) * lg128)
    g_out = jnp.exp((c - 1.0 - tt) * lg128)
    g_all = jnp.exp(float(c) * lg128)
    s = jnp.zeros((RET_DK, RET_DK), F32)
    for ci in range(t_len // c):
        sl = slice(ci * c, (ci + 1) * c)
        qc = q_s[sl, :]
        kc = k_s[sl, :]
        vb = v_s[sl, :]
        scores = _nt_dot(qc.astype(BF16), kc.astype(BF16)) * dec
        o_s[sl, :] = (jnp.dot(scores.astype(BF16), vb, preferred_element_type=F32)
                      + jnp.dot((qc * g_in).astype(BF16), s.astype(BF16), preferred_element_type=F32))
        s = s * g_all + _tn_dot((kc * g_out).astype(BF16), vb)
    out_ref[...] = (_rms_rows(o_s[...]) * jax.nn.silu(gate)).astype(out_ref.dtype)
    st_ref[...] = s


def _log_gamma_rows(width):
    lg = jnp.log1p(-jnp.exp2(-5.0 - jnp.arange(RET_HEADS, dtype=F32)))
    return jnp.broadcast_to(lg[:, None, None], (RET_HEADS, 1, width))


def ret_prompt(layer, hin, w_in_t, cos, sin):
    tab = pl.BlockSpec((SEQ, RET_DK), lambda b, h: (0, 0))
    ret_rows = tuple(RET_COL0 + off for off in (OFF_RQ, OFF_RK, OFF_RV, OFF_RG))
    return pl.pallas_call(
        _ret_prompt_kernel,
        out_shape=(jax.ShapeDtypeStruct((BATCH * SEQ, BRANCH_W), BF16),
                   jax.ShapeDtypeStruct((BATCH, RET_HEADS, RET_DK, RET_DK), F32)),
        grid=(BATCH, RET_HEADS),
        in_specs=[pl.BlockSpec((SEQ, D_MODEL), lambda b, h: (b, 0))]
                 + _w_in_row_specs(layer, ret_rows, _prompt_unit_rows)
                 + [tab, tab, pl.BlockSpec((None, 1, RET_CHUNK), lambda b, h: (h, 0, 0))],
        out_specs=(pl.BlockSpec((SEQ, RET_DK), lambda b, h: (b, h)),
                   pl.BlockSpec((None, None, RET_DK, RET_DK), lambda b, h: (b, h, 0, 0))),
        scratch_shapes=[pltpu.VMEM((SEQ, RET_DK), F32), pltpu.VMEM((SEQ, RET_DK), F32),
                        pltpu.VMEM((SEQ, RET_DK), BF16), pltpu.VMEM((SEQ, RET_DK), F32)],
        compiler_params=_cparams(2),
        name="ret_prompt",
    )(hin, w_in_t, w_in_t, w_in_t, w_in_t, cos, sin, _log_gamma_rows(RET_CHUNK))


def _ret_sample_kernel(q_ref, k_ref, v_ref, g_ref, cos_ref, sin_ref, lg_ref, s_ref, out_ref, so_ref, o_s):
    cos = cos_ref[0:1, :]
    sin = sin_ref[0:1, :]
    q = _rope(q_ref[...], cos, sin)
    k = _rope(k_ref[...], cos, sin) * RET_DK ** -0.5
    v = v_ref[...]
    gamma = jnp.exp(lg_ref[...])
    _sample_state_step(s_ref, so_ref, o_s, lambda j: gamma, q, k, v)
    out_ref[...] = _rms_rows(o_s[...]) * jax.nn.silu(g_ref[...])


def ret_sample(layer, proj, cos, sin, state, state_out):
    cb = lambda off: off // RET_DK
    nb = SAMPLE_BLK
    col = lambda off: pl.BlockSpec((nb, RET_DK), lambda h, i: (i, cb(off) + h))
    tab = pl.BlockSpec((SUBLANES, RET_DK), lambda h, i: (0, 0))
    st_spec = pl.BlockSpec((None, nb, None, RET_DK, RET_DK), lambda h, i: (layer, i, h, 0, 0))
    args = [proj, proj, proj, proj, cos, sin, _log_gamma_rows(LANES), state]
    in_specs = [col(OFF_RQ), col(OFF_RK), col(OFF_RV), col(OFF_RG), tab, tab,
                pl.BlockSpec((None, 1, LANES), lambda h, i: (h, 0, 0)), st_spec]
    aliases = {}
    kern = _ret_sample_kernel
    if state_out is not None:
        args.append(state_out)
        in_specs.append(pl.BlockSpec(memory_space=pl.ANY))
        aliases = {len(args) - 1: 1}
        kern = functools.partial(_drop_alias_arg, kern, len(args) - 1)
    return pl.pallas_call(
        kern,
        out_shape=(jax.ShapeDtypeStruct((DEC_BATCH, BRANCH_W), F32),
                   jax.ShapeDtypeStruct(state.shape, F32)),
        grid=(RET_HEADS, DEC_BATCH // nb),
        in_specs=in_specs,
        out_specs=(pl.BlockSpec((nb, RET_DK), lambda h, i: (i, h)), st_spec),
        scratch_shapes=[pltpu.VMEM((nb, RET_DK), F32)],
        input_output_aliases=aliases,
        compiler_params=_cparams(2),
        name="ret_sample",
    )(*args)


def kernel(x_prompt, x_sample, state_lru_h, state_lru_conv, state_hgrn, state_ssd, state_ssd_conv, state_ret, state_ffn_conv, g_mix, g_ffn, w_in, lru_conv_w, lru_conv_b, lru_wa, lru_ba, lru_wx, lru_bx, lru_lambda, hg_lb_logits, hg_norm_w, ssd_conv_w, ssd_conv_b, ssd_dt_bias, ssd_a_log, ssd_d, ssd_norm_w, w_branch, w_gate, w_out, ffn_w_up, ffn_w_val, ffn_conv_w, ffn_conv_b, ffn_w_down, g_final):
    xp = x_prompt.reshape(BATCH * SEQ, D_MODEL)
    xs = x_sample.reshape(DEC_BATCH, D_MODEL)

    cos_p, sin_p = rope_tables(SEQ, 0, True)
    cos_s, sin_s = rope_tables(SUBLANES, PAST_LEN, False)

    hp = rmsnorm(xp, g_mix[0], BF16, NORM_TM)
    hs = rmsnorm(xs, g_mix[0], BF16, DEC_BATCH)

    state_ssd_t = jnp.swapaxes(state_ssd, -1, -2)

    w_in_t = jnp.swapaxes(w_in, 1, 2)
    w_gate_b = jnp.transpose(w_gate, (0, 2, 1, 3)).astype(BF16)
    w_out_b = w_out.astype(BF16)
    w_down_b = ffn_w_down.astype(BF16)

    prompt_states, sample_small = [], []
    hg_out = ssd_out = ret_out = None
    for l in range(DEPTH):
        p = {"lru_conv_w": lru_conv_w[l], "lru_conv_b": lru_conv_b[l], "lru_wa": lru_wa[l], "lru_ba": lru_ba[l],
             "lru_wx": lru_wx[l], "lru_bx": lru_bx[l], "lru_lambda": lru_lambda[l],
             "hg_lb_logits": hg_lb_logits, "hg_norm_w": hg_norm_w[l],
             "ssd_conv_w": ssd_conv_w[l], "ssd_conv_b": ssd_conv_b[l], "ssd_dt_bias": ssd_dt_bias[l],
             "ssd_a_log": ssd_a_log[l], "ssd_d": ssd_d[l], "ssd_norm_w": ssd_norm_w[l]}
        last = l == DEPTH - 1
        g_next = g_final if last else g_mix[l + 1]

        out_a, lru_h_p, lru_conv_p = lru_prompt(l, hp, w_in_t, p)
        out_b, hg_p = hgrn_prompt(l, hp, w_in_t, p)
        proj_ssd_p = in_proj_prompt(l, hp, w_in_t, OFF_SZ, SSD_TILES, SSD_TN, SSD_PROJ_TM)
        out_c, ssd_p, ssd_conv_p = ssd_prompt(proj_ssd_p, p)
        out_d, ret_p = ret_prompt(l, hp, w_in_t, cos_p, sin_p)
        merged, w_branch_b = gated_merge_prompt(l, hp, (out_a, out_b, out_c, out_d), w_gate_b, w_branch,
                                                MERGE_TM, MERGE_TN)
        xp, h2 = out_proj_residual_norm(l, merged, w_out_b, xp, g_ffn[l], OUT_PROJ_TM)
        act, ffn_conv_p, w_up_b, w_val_b = ffn_prompt(l, h2, ffn_w_up, ffn_w_val, ffn_conv_w[l], ffn_conv_b[l], FFN_TN)
        res = down_proj_residual_norm(l, act, w_down_b, xp, g_next, DOWN_PROJ_TM, DOWN_PROJ_TK, not last,
                                      F32 if last else BF16)
        if last:
            (yp,) = res
        else:
            xp, hp = res
        prompt_states.append((lru_h_p, lru_conv_p, hg_p, ssd_p, ssd_conv_p, ret_p, ffn_conv_p))

        proj_s = in_proj_sample(l, hs, w_in_t, 0, MAIN_TILES, MAIN_TN)
        proj_ret_s = in_proj_sample(l, hs, w_in_t, RET_COL0, RET_TILES, RET_TN)
        lru_buf_t = jnp.swapaxes(state_lru_conv[l], 0, 1)
        ssd_buf_t = jnp.swapaxes(state_ssd_conv[l], 0, 1)
        s_a, lru_h_s, lru_nbuf = lru_sample(proj_s, p, state_lru_h[l], lru_buf_t)
        s_b, hg_out = hgrn_sample(l, proj_s, p, state_hgrn, hg_out)
        s_c, ssd_out, ssd_nbuf = ssd_sample(l, proj_s, p, state_ssd_t, ssd_out, ssd_buf_t)
        s_d, ret_out = ret_sample(l, proj_ret_s, cos_s, sin_s, state_ret, ret_out)
        merged_s = gated_merge_sample(l, hs, (s_a, s_b, s_c, s_d), w_gate_b, w_branch_b, MERGE_TN)
        xs, h2s = out_proj_residual_norm(l, merged_s, w_out_b, xs, g_ffn[l], DEC_BATCH)
        act_s, ffn_nbuf = ffn_sample(l, h2s, w_up_b, w_val_b, ffn_conv_w[l], ffn_conv_b[l], state_ffn_conv,
                                       FFN_SAMPLE_TN)
        res = down_proj_residual_norm(l, act_s, w_down_b, xs, g_next, DEC_BATCH, DOWN_PROJ_TK, not last,
                                      F32 if last else BF16)
        if last:
            (ys,) = res
        else:
            xs, hs = res
        sample_small.append((lru_h_s, jnp.swapaxes(lru_nbuf, 0, 1), jnp.swapaxes(ssd_nbuf, 0, 1), ffn_nbuf))

    stack_p = lambda i: jnp.stack([st[i] for st in prompt_states], axis=0)
    stack_s = lambda i: jnp.stack([st[i] for st in sample_small], axis=0)
    return (yp.reshape(BATCH, SEQ, D_MODEL), ys.reshape(DEC_BATCH, 1, D_MODEL),
            stack_p(0), stack_s(0), stack_p(1), stack_s(1),
            stack_p(2), hg_out, jnp.swapaxes(stack_p(3), -1, -2), jnp.swapaxes(ssd_out, -1, -2),
            stack_p(4), stack_s(2), stack_p(5), ret_out,
            stack_p(6), stack_s(3))
```
